```python
import math
import jax, jax.numpy as jnp
from jax import lax
import numpy as np

D_MODEL = 1024
BATCH = 16
SEQ = 2048
DEPTH = 2

N_MIXERS = 2
N_HEADS = 16
HEAD_DIM = D_MODEL // N_HEADS
Q_BLOCK = 128
SSM_GROUP = 16
N_GROUPS = D_MODEL // SSM_GROUP
STATE = 64
D_FF = ((8 * D_MODEL // 3 + 127) // 128) * 128
CONV_W = 3
N_ATTN = (DEPTH + 1) // 2
N_SSM = DEPTH // 2
EPS = 1e-6
DT_MIN = 1e-3
DT_MAX = 1e-1

kernel_name = "hybrid_stickbreak_s5_convffn_adaln"


def rms_norm(x, g):
    xf = x.astype(jnp.float32)
    y = xf * lax.rsqrt(jnp.mean(xf * xf, axis=-1, keepdims=True) + EPS)
    return (y * g.astype(jnp.float32)).astype(x.dtype)


def modulate(h, shift, scale):
    return h * (1 + scale[:, None, :]) + shift[:, None, :]


def stick_breaking_attention(h, w_qkv, w_o):
    b, s, d = h.shape
    q, k, v = jnp.split(h @ w_qkv, 3, axis=-1)
    to_heads = lambda t: t.reshape(b, s, N_HEADS, HEAD_DIM).transpose(0, 2, 1, 3)
    q, k, v = to_heads(q), to_heads(k), to_heads(v)
    kf = k.astype(jnp.float32)
    vf = v.astype(jnp.float32)
    n_blk = s // Q_BLOCK
    qb = q.reshape(b, N_HEADS, n_blk, Q_BLOCK, HEAD_DIM).transpose(2, 0, 1, 3, 4)
    key_pos = jnp.arange(s)
    scale = HEAD_DIM ** -0.5

    def block(args):
        q_blk, blk_idx = args
        q_pos = blk_idx * Q_BLOCK + jnp.arange(Q_BLOCK)
        z = jnp.einsum('bhqd,bhkd->bhqk', q_blk.astype(jnp.float32), kf) * scale
        mask = key_pos[None, :] < q_pos[:, None]
        log_beta = jax.nn.log_sigmoid(z)
        log_1mb = jnp.where(mask, jax.nn.log_sigmoid(-z), 0.0)
        suffix = lax.cumsum(log_1mb, axis=3, reverse=True) - log_1mb
        w = jnp.where(mask, jnp.exp(log_beta + suffix), 0.0)
        o = jnp.einsum('bhqk,bhkd->bhqd', w, vf)
        return o.astype(h.dtype)

    o = lax.map(block, (qb, jnp.arange(n_blk)))
    o = o.transpose(1, 0, 3, 2, 4).reshape(b, s, d)
    return o @ w_o


def s5_ssm(h, w_in, a_re, a_im, log_dt, b_re, b_im, c_re, c_im, d_skip, w_glu, b_glu, w_o):
    b, s, d = h.shape
    u = h @ w_in
    uf = u.astype(jnp.float32)
    ug = uf.reshape(b, s, N_GROUPS, SSM_GROUP)
    lam = lax.complex(a_re.astype(jnp.float32), a_im.astype(jnp.float32))
    dt = jnp.exp(log_dt.astype(jnp.float32))[:, None]
    lam_bar = jnp.exp(lam * dt)
    b_mat = lax.complex(b_re.astype(jnp.float32), b_im.astype(jnp.float32))
    b_bar = ((lam_bar - 1) / lam)[..., None] * b_mat
    bu = jnp.einsum('gph,bsgh->bsgp', b_bar, ug.astype(jnp.complex64))
    a_seq = jnp.broadcast_to(lam_bar, (1, s, N_GROUPS, STATE))

    def combine(e1, e2):
        a1, x1 = e1
        a2, x2 = e2
        return a2 * a1, a2 * x1 + x2

    _, states = lax.associative_scan(combine, (a_seq, bu), axis=1)
    c_mat = lax.complex(c_re.astype(jnp.float32), c_im.astype(jnp.float32))
    y = jnp.einsum('ghp,bsgp->bsgh', c_mat, states).real.reshape(b, s, d)
    y = (y + d_skip.astype(jnp.float32) * uf).astype(h.dtype)
    z = jax.nn.gelu(y)
    g = z * jax.nn.sigmoid(z @ w_glu + b_glu)
    return g @ w_o


def conv_ffn(h, w_up, conv_w, conv_b, w_down):
    up = h @ w_up
    up = lax.conv_general_dilated(
        up, conv_w[:, None, :], window_strides=(1,), padding=[(CONV_W - 1, 0)],
        dimension_numbers=('NWC', 'WIO', 'NWC'), feature_group_count=2 * D_FF) + conv_b
    gate, val = jnp.split(up, 2, axis=-1)
    return (jax.nn.silu(gate) * val) @ w_down


def _fwd_setup_inputs(seed: int = 0) -> dict:
    key = jax.random.key(seed)
    ks = iter(jax.random.split(key, 40))
    nrm = lambda shape, std: jax.random.normal(next(ks), shape, jnp.float32) * std
    D, G, P, H, F = D_MODEL, N_GROUPS, STATE, SSM_GROUP, D_FF
    n_idx = jnp.arange(P, dtype=jnp.float32)
    inp = {}
    inp["x"] = nrm((BATCH, SEQ, D), 1.0)
    inp["c"] = nrm((BATCH, D), 1.0)
    inp["norm_mix"] = 1.0 + nrm((DEPTH, D), 0.02)
    inp["norm_ffn"] = 1.0 + nrm((DEPTH, D), 0.02)
    inp["w_mod"] = nrm((DEPTH, D, 6 * D), 0.5 * D ** -0.5)
    inp["b_mod"] = nrm((DEPTH, 6 * D), 0.02)
    inp["w_qkv"] = nrm((N_ATTN, D, 3 * D), D ** -0.5)
    inp["w_o_attn"] = nrm((N_ATTN, D, D), D ** -0.5)
    inp["w_in_ssm"] = nrm((N_SSM, D, D), D ** -0.5)
    inp["a_re"] = -0.5 + nrm((N_SSM, G, P), 0.01)
    inp["a_im"] = math.pi * n_idx + nrm((N_SSM, G, P), 0.01)
    inp["log_dt"] = jax.random.uniform(next(ks), (N_SSM, G), jnp.float32,
                                       math.log(DT_MIN), math.log(DT_MAX))
    inp["b_re"] = nrm((N_SSM, G, P, H), (2 * H) ** -0.5)
    inp["b_im"] = nrm((N_SSM, G, P, H), (2 * H) ** -0.5)
    inp["c_re"] = nrm((N_SSM, G, H, P), (2 * P) ** -0.5 * 4.0)
    inp["c_im"] = nrm((N_SSM, G, H, P), (2 * P) ** -0.5 * 4.0)
    inp["d_skip"] = nrm((N_SSM, D), 1.0)
    inp["w_glu"] = nrm((N_SSM, D, D), D ** -0.5)
    inp["b_glu"] = nrm((N_SSM, D), 0.02)
    inp["w_o_ssm"] = nrm((N_SSM, D, D), D ** -0.5)
    inp["w_up"] = nrm((DEPTH, D, 2 * F), D ** -0.5)
    inp["conv_w"] = nrm((DEPTH, CONV_W, 2 * F), CONV_W ** -0.5)
    inp["conv_b"] = nrm((DEPTH, 2 * F), 0.02)
    inp["w_down"] = nrm((DEPTH, F, D), F ** -0.5)
    inp["norm_out"] = 1.0 + nrm((D,), 0.02)
    inp["w_fin"] = nrm((D, 2 * D), 0.5 * D ** -0.5)
    inp["b_fin"] = nrm((2 * D,), 0.02)
    return inp


def _fwd_reference(x, c, norm_mix, norm_ffn, w_mod, b_mod, w_qkv, w_o_attn, w_in_ssm,
              a_re, a_im, log_dt, b_re, b_im, c_re, c_im, d_skip, w_glu, b_glu, w_o_ssm,
              w_up, conv_w, conv_b, w_down, norm_out, w_fin, b_fin):
    c_act = jax.nn.silu(c)
    for i in range(DEPTH):
        mod = c_act @ w_mod[i] + b_mod[i]
        sh1, sc1, g1, sh2, sc2, g2 = jnp.split(mod, 6, axis=-1)
        h = modulate(rms_norm(x, norm_mix[i]), sh1, sc1)
        j = i // N_MIXERS
        if i % N_MIXERS == 0:
            y = stick_breaking_attention(h, w_qkv[j], w_o_attn[j])
        else:
            y = s5_ssm(h, w_in_ssm[j], a_re[j], a_im[j], log_dt[j], b_re[j], b_im[j],
                       c_re[j], c_im[j], d_skip[j], w_glu[j], b_glu[j], w_o_ssm[j])
        x = x + g1[:, None, :] * y
        h = modulate(rms_norm(x, norm_ffn[i]), sh2, sc2)
        x = x + g2[:, None, :] * conv_ffn(h, w_up[i], conv_w[i], conv_b[i], w_down[i])
    fin = c_act @ w_fin + b_fin
    sh, sc = jnp.split(fin, 2, axis=-1)
    return modulate(rms_norm(x, norm_out), sh, sc)


import jax as _jax
import jax.numpy as _jnp

TWIN_FORMAT = 'train_step'
FWD_PARAMS = ['x', 'c', 'norm_mix', 'norm_ffn', 'w_mod', 'b_mod', 'w_qkv', 'w_o_attn', 'w_in_ssm', 'a_re', 'a_im', 'log_dt', 'b_re', 'b_im', 'c_re', 'c_im', 'd_skip', 'w_glu', 'b_glu', 'w_o_ssm', 'w_up', 'conv_w', 'conv_b', 'w_down', 'norm_out', 'w_fin', 'b_fin']
TWIN_WEIGHTS = ['norm_mix', 'norm_ffn', 'w_mod', 'b_mod', 'w_qkv', 'w_o_attn', 'w_in_ssm', 'a_re', 'a_im', 'log_dt', 'b_re', 'b_im', 'c_re', 'c_im', 'd_skip', 'w_glu', 'b_glu', 'w_o_ssm', 'w_up', 'conv_w', 'conv_b', 'w_down', 'norm_out', 'w_fin', 'b_fin']
TWIN_DIFF_INPUT = 'x'
TWIN_INPUTS = ['x', 'c', 'norm_mix', 'norm_ffn', 'w_mod', 'b_mod', 'w_qkv', 'w_o_attn', 'w_in_ssm', 'a_re', 'a_im', 'log_dt', 'b_re', 'b_im', 'c_re', 'c_im', 'd_skip', 'w_glu', 'b_glu', 'w_o_ssm', 'w_up', 'conv_w', 'conv_b', 'w_down', 'norm_out', 'w_fin', 'b_fin', 'loss_target', 'm_norm_mix', 'm_norm_ffn', 'm_w_mod', 'm_b_mod', 'm_w_qkv', 'm_w_o_attn', 'm_w_in_ssm', 'm_a_re', 'm_a_im', 'm_log_dt', 'm_b_re', 'm_b_im', 'm_c_re', 'm_c_im', 'm_d_skip', 'm_w_glu', 'm_b_glu', 'm_w_o_ssm', 'm_w_up', 'm_conv_w', 'm_conv_b', 'm_w_down', 'm_norm_out', 'm_w_fin', 'm_b_fin', 'v_norm_mix', 'v_norm_ffn', 'v_w_mod', 'v_b_mod', 'v_w_qkv', 'v_w_o_attn', 'v_w_in_ssm', 'v_a_re', 'v_a_im', 'v_log_dt', 'v_b_re', 'v_b_im', 'v_c_re', 'v_c_im', 'v_d_skip', 'v_w_glu', 'v_b_glu', 'v_w_o_ssm', 'v_w_up', 'v_conv_w', 'v_conv_b', 'v_w_down', 'v_norm_out', 'v_w_fin', 'v_b_fin']
TWIN_OUTPUTS = ['loss', 'grad_x', 'grad_norm_mix', 'grad_norm_ffn', 'grad_w_mod', 'grad_b_mod', 'grad_w_qkv', 'grad_w_o_attn', 'grad_w_in_ssm', 'grad_a_re', 'grad_a_im', 'grad_log_dt', 'grad_b_re', 'grad_b_im', 'grad_c_re', 'grad_c_im', 'grad_d_skip', 'grad_w_glu', 'grad_b_glu', 'grad_w_o_ssm', 'grad_w_up', 'grad_conv_w', 'grad_conv_b', 'grad_w_down', 'grad_norm_out', 'grad_w_fin', 'grad_b_fin', 'delta_norm_mix', 'delta_norm_ffn', 'delta_w_mod', 'delta_b_mod', 'delta_w_qkv', 'delta_w_o_attn', 'delta_w_in_ssm', 'delta_a_re', 'delta_a_im', 'delta_log_dt', 'delta_b_re', 'delta_b_im', 'delta_c_re', 'delta_c_im', 'delta_d_skip', 'delta_w_glu', 'delta_b_glu', 'delta_w_o_ssm', 'delta_w_up', 'delta_conv_w', 'delta_conv_b', 'delta_w_down', 'delta_norm_out', 'delta_w_fin', 'delta_b_fin', 'new_m_norm_mix', 'new_m_norm_ffn', 'new_m_w_mod', 'new_m_b_mod', 'new_m_w_qkv', 'new_m_w_o_attn', 'new_m_w_in_ssm', 'new_m_a_re', 'new_m_a_im', 'new_m_log_dt', 'new_m_b_re', 'new_m_b_im', 'new_m_c_re', 'new_m_c_im', 'new_m_d_skip', 'new_m_w_glu', 'new_m_b_glu', 'new_m_w_o_ssm', 'new_m_w_up', 'new_m_conv_w', 'new_m_conv_b', 'new_m_w_down', 'new_m_norm_out', 'new_m_w_fin', 'new_m_b_fin', 'new_v_norm_mix', 'new_v_norm_ffn', 'new_v_w_mod', 'new_v_b_mod', 'new_v_w_qkv', 'new_v_w_o_attn', 'new_v_w_in_ssm', 'new_v_a_re', 'new_v_a_im', 'new_v_log_dt', 'new_v_b_re', 'new_v_b_im', 'new_v_c_re', 'new_v_c_im', 'new_v_d_skip', 'new_v_w_glu', 'new_v_b_glu', 'new_v_w_o_ssm', 'new_v_w_up', 'new_v_conv_w', 'new_v_conv_b', 'new_v_w_down', 'new_v_norm_out', 'new_v_w_fin', 'new_v_b_fin']
TWIN_LEAF_KINDS = {'loss': 'loss', 'grad_x': 'grad_x', 'grad_norm_mix': 'grad_w', 'grad_norm_ffn': 'grad_w', 'grad_w_mod': 'grad_w', 'grad_b_mod': 'grad_w', 'grad_w_qkv': 'grad_w', 'grad_w_o_attn': 'grad_w', 'grad_w_in_ssm': 'grad_w', 'grad_a_re': 'grad_w', 'grad_a_im': 'grad_w', 'grad_log_dt': 'grad_w', 'grad_b_re': 'grad_w', 'grad_b_im': 'grad_w', 'grad_c_re': 'grad_w', 'grad_c_im': 'grad_w', 'grad_d_skip': 'grad_w', 'grad_w_glu': 'grad_w', 'grad_b_glu': 'grad_w', 'grad_w_o_ssm': 'grad_w', 'grad_w_up': 'grad_w', 'grad_conv_w': 'grad_w', 'grad_conv_b': 'grad_w', 'grad_w_down': 'grad_w', 'grad_norm_out': 'grad_w', 'grad_w_fin': 'grad_w', 'grad_b_fin': 'grad_w', 'delta_norm_mix': 'delta_w', 'delta_norm_ffn': 'delta_w', 'delta_w_mod': 'delta_w', 'delta_b_mod': 'delta_w', 'delta_w_qkv': 'delta_w', 'delta_w_o_attn': 'delta_w', 'delta_w_in_ssm': 'delta_w', 'delta_a_re': 'delta_w', 'delta_a_im': 'delta_w', 'delta_log_dt': 'delta_w', 'delta_b_re': 'delta_w', 'delta_b_im': 'delta_w', 'delta_c_re': 'delta_w', 'delta_c_im': 'delta_w', 'delta_d_skip': 'delta_w', 'delta_w_glu': 'delta_w', 'delta_b_glu': 'delta_w', 'delta_w_o_ssm': 'delta_w', 'delta_w_up': 'delta_w', 'delta_conv_w': 'delta_w', 'delta_conv_b': 'delta_w', 'delta_w_down': 'delta_w', 'delta_norm_out': 'delta_w', 'delta_w_fin': 'delta_w', 'delta_b_fin': 'delta_w', 'new_m_norm_mix': 'new_m', 'new_m_norm_ffn': 'new_m', 'new_m_w_mod': 'new_m', 'new_m_b_mod': 'new_m', 'new_m_w_qkv': 'new_m', 'new_m_w_o_attn': 'new_m', 'new_m_w_in_ssm': 'new_m', 'new_m_a_re': 'new_m', 'new_m_a_im': 'new_m', 'new_m_log_dt': 'new_m', 'new_m_b_re': 'new_m', 'new_m_b_im': 'new_m', 'new_m_c_re': 'new_m', 'new_m_c_im': 'new_m', 'new_m_d_skip': 'new_m', 'new_m_w_glu': 'new_m', 'new_m_b_glu': 'new_m', 'new_m_w_o_ssm': 'new_m', 'new_m_w_up': 'new_m', 'new_m_conv_w': 'new_m', 'new_m_conv_b': 'new_m', 'new_m_w_down': 'new_m', 'new_m_norm_out': 'new_m', 'new_m_w_fin': 'new_m', 'new_m_b_fin': 'new_m', 'new_v_norm_mix': 'new_v', 'new_v_norm_ffn': 'new_v', 'new_v_w_mod': 'new_v', 'new_v_b_mod': 'new_v', 'new_v_w_qkv': 'new_v', 'new_v_w_o_attn': 'new_v', 'new_v_w_in_ssm': 'new_v', 'new_v_a_re': 'new_v', 'new_v_a_im': 'new_v', 'new_v_log_dt': 'new_v', 'new_v_b_re': 'new_v', 'new_v_b_im': 'new_v', 'new_v_c_re': 'new_v', 'new_v_c_im': 'new_v', 'new_v_d_skip': 'new_v', 'new_v_w_glu': 'new_v', 'new_v_b_glu': 'new_v', 'new_v_w_o_ssm': 'new_v', 'new_v_w_up': 'new_v', 'new_v_conv_w': 'new_v', 'new_v_conv_b': 'new_v', 'new_v_w_down': 'new_v', 'new_v_norm_out': 'new_v', 'new_v_w_fin': 'new_v', 'new_v_b_fin': 'new_v'}


def _forward(args):
    return _fwd_reference(*[args[k] for k in FWD_PARAMS])


def _output_shape():
    out = _jax.eval_shape(lambda: _forward(_fwd_setup_inputs(0)))
    return out.shape, out.dtype

N_MICROBATCH = 1
ADAM_LR = 0.001
ADAM_B1 = 0.9
ADAM_B2 = 0.999
ADAM_EPS = 1e-08
ADAM_WD = 0.01
ADAM_STEP = 10
PER_EXAMPLE_BATCH_AXIS = {'x': 0, 'c': 0, 'loss_target': 0}
SHARED_INPUTS = []
_WEIGHT_DTYPES = {'norm_mix': _jnp.float32, 'norm_ffn': _jnp.float32, 'w_mod': _jnp.float32, 'b_mod': _jnp.float32, 'w_qkv': _jnp.float32, 'w_o_attn': _jnp.float32, 'w_in_ssm': _jnp.float32, 'a_re': _jnp.float32, 'a_im': _jnp.float32, 'log_dt': _jnp.float32, 'b_re': _jnp.float32, 'b_im': _jnp.float32, 'c_re': _jnp.float32, 'c_im': _jnp.float32, 'd_skip': _jnp.float32, 'w_glu': _jnp.float32, 'b_glu': _jnp.float32, 'w_o_ssm': _jnp.float32, 'w_up': _jnp.float32, 'conv_w': _jnp.float32, 'conv_b': _jnp.float32, 'w_down': _jnp.float32, 'norm_out': _jnp.float32, 'w_fin': _jnp.float32, 'b_fin': _jnp.float32}
MOMENT_SCALE = {'norm_mix': 5.828571e-02, 'norm_ffn': 8.079048e-02, 'w_mod': 6.232426e-01, 'b_mod': 1.044511e+00, 'w_qkv': 2.348015e-01, 'w_o_attn': 4.234537e-01, 'w_in_ssm': 1.303545e-01, 'a_re': 3.998197e-02, 'a_im': 4.227341e-02, 'log_dt': 5.496307e+00, 'b_re': 3.268720e-02, 'b_im': 3.396746e-02, 'c_re': 1.807967e-02, 'c_im': 1.694121e-02, 'd_skip': 3.203510e-01, 'w_glu': 6.283198e-02, 'b_glu': 1.590968e-01, 'w_o_ssm': 3.422161e-01, 'w_up': 5.826724e-02, 'conv_w': 6.386339e-02, 'conv_b': 2.059892e-01, 'w_down': 1.203441e-01, 'norm_out': 3.692600e+01, 'w_fin': 8.774996e+00, 'b_fin': 2.338789e+01}


def _to_microbatches(a, axis):
    t = _jnp.moveaxis(a, axis, 0)
    t = t.reshape((N_MICROBATCH, t.shape[0] // N_MICROBATCH) + t.shape[1:])
    return _jnp.moveaxis(t, 1, axis + 1)


def setup_inputs(seed: int = 0) -> dict:
    inp = _fwd_setup_inputs(seed)
    key = _jax.random.fold_in(_jax.random.key(seed), 7919)
    shape, _ = _output_shape()
    out = dict(inp)
    out["loss_target"] = _jax.random.normal(_jax.random.fold_in(key, 0), shape, _jnp.float32)
    for i, name in enumerate(TWIN_WEIGHTS):
        w = inp[name].astype(_jnp.float32)
        if MOMENT_SCALE is None:
            s = _jnp.sqrt(_jnp.mean(_jnp.square(w)) + 1e-30)
        else:
            s = MOMENT_SCALE[name]
        km, kv = _jax.random.split(_jax.random.fold_in(key, i + 1))
        out[name] = w
        out["m_" + name] = s * _jax.random.normal(km, w.shape, _jnp.float32)
        out["v_" + name] = (s * s) * _jax.random.uniform(kv, w.shape, _jnp.float32, 0.5, 1.5)
    if N_MICROBATCH > 1:
        for name, axis in PER_EXAMPLE_BATCH_AXIS.items():
            out[name] = _to_microbatches(out[name], axis)
    return {'x': out['x'], 'c': out['c'], 'norm_mix': out['norm_mix'], 'norm_ffn': out['norm_ffn'], 'w_mod': out['w_mod'], 'b_mod': out['b_mod'], 'w_qkv': out['w_qkv'], 'w_o_attn': out['w_o_attn'], 'w_in_ssm': out['w_in_ssm'], 'a_re': out['a_re'], 'a_im': out['a_im'], 'log_dt': out['log_dt'], 'b_re': out['b_re'], 'b_im': out['b_im'], 'c_re': out['c_re'], 'c_im': out['c_im'], 'd_skip': out['d_skip'], 'w_glu': out['w_glu'], 'b_glu': out['b_glu'], 'w_o_ssm': out['w_o_ssm'], 'w_up': out['w_up'], 'conv_w': out['conv_w'], 'conv_b': out['conv_b'], 'w_down': out['w_down'], 'norm_out': out['norm_out'], 'w_fin': out['w_fin'], 'b_fin': out['b_fin'], 'loss_target': out['loss_target'], 'm_norm_mix': out['m_norm_mix'], 'm_norm_ffn': out['m_norm_ffn'], 'm_w_mod': out['m_w_mod'], 'm_b_mod': out['m_b_mod'], 'm_w_qkv': out['m_w_qkv'], 'm_w_o_attn': out['m_w_o_attn'], 'm_w_in_ssm': out['m_w_in_ssm'], 'm_a_re': out['m_a_re'], 'm_a_im': out['m_a_im'], 'm_log_dt': out['m_log_dt'], 'm_b_re': out['m_b_re'], 'm_b_im': out['m_b_im'], 'm_c_re': out['m_c_re'], 'm_c_im': out['m_c_im'], 'm_d_skip': out['m_d_skip'], 'm_w_glu': out['m_w_glu'], 'm_b_glu': out['m_b_glu'], 'm_w_o_ssm': out['m_w_o_ssm'], 'm_w_up': out['m_w_up'], 'm_conv_w': out['m_conv_w'], 'm_conv_b': out['m_conv_b'], 'm_w_down': out['m_w_down'], 'm_norm_out': out['m_norm_out'], 'm_w_fin': out['m_w_fin'], 'm_b_fin': out['m_b_fin'], 'v_norm_mix': out['v_norm_mix'], 'v_norm_ffn': out['v_norm_ffn'], 'v_w_mod': out['v_w_mod'], 'v_b_mod': out['v_b_mod'], 'v_w_qkv': out['v_w_qkv'], 'v_w_o_attn': out['v_w_o_attn'], 'v_w_in_ssm': out['v_w_in_ssm'], 'v_a_re': out['v_a_re'], 'v_a_im': out['v_a_im'], 'v_log_dt': out['v_log_dt'], 'v_b_re': out['v_b_re'], 'v_b_im': out['v_b_im'], 'v_c_re': out['v_c_re'], 'v_c_im': out['v_c_im'], 'v_d_skip': out['v_d_skip'], 'v_w_glu': out['v_w_glu'], 'v_b_glu': out['v_b_glu'], 'v_w_o_ssm': out['v_w_o_ssm'], 'v_w_up': out['v_w_up'], 'v_conv_w': out['v_conv_w'], 'v_conv_b': out['v_conv_b'], 'v_w_down': out['v_w_down'], 'v_norm_out': out['v_norm_out'], 'v_w_fin': out['v_w_fin'], 'v_b_fin': out['v_b_fin']}


def _loss(weights, diff, rest, loss_target):
    with _jax.named_scope("forward"):
        args = {**rest, TWIN_DIFF_INPUT: diff, **{k: w.astype(_WEIGHT_DTYPES[k]) for k, w in weights.items()}}
        y = _forward(args)
    with _jax.named_scope("loss_head"):
        err = _jnp.square(y.astype(_jnp.float32) - loss_target)
        return 0.5 * _jnp.sum(_jnp.mean(err, axis=-1)) if err.ndim else 0.5 * err


def _adamw(w, g, m, v):
    m = ADAM_B1 * m + (1.0 - ADAM_B1) * g
    v = ADAM_B2 * v + (1.0 - ADAM_B2) * _jnp.square(g)
    m_hat = m / (1.0 - ADAM_B1 ** ADAM_STEP)
    v_hat = v / (1.0 - ADAM_B2 ** ADAM_STEP)
    delta = -ADAM_LR * (m_hat / (_jnp.sqrt(v_hat) + ADAM_EPS) + ADAM_WD * w)
    return delta, m, v


def reference(x, c, norm_mix, norm_ffn, w_mod, b_mod, w_qkv, w_o_attn, w_in_ssm, a_re, a_im, log_dt, b_re, b_im, c_re, c_im, d_skip, w_glu, b_glu, w_o_ssm, w_up, conv_w, conv_b, w_down, norm_out, w_fin, b_fin, loss_target, m_norm_mix, m_norm_ffn, m_w_mod, m_b_mod, m_w_qkv, m_w_o_attn, m_w_in_ssm, m_a_re, m_a_im, m_log_dt, m_b_re, m_b_im, m_c_re, m_c_im, m_d_skip, m_w_glu, m_b_glu, m_w_o_ssm, m_w_up, m_conv_w, m_conv_b, m_w_down, m_norm_out, m_w_fin, m_b_fin, v_norm_mix, v_norm_ffn, v_w_mod, v_b_mod, v_w_qkv, v_w_o_attn, v_w_in_ssm, v_a_re, v_a_im, v_log_dt, v_b_re, v_b_im, v_c_re, v_c_im, v_d_skip, v_w_glu, v_b_glu, v_w_o_ssm, v_w_up, v_conv_w, v_conv_b, v_w_down, v_norm_out, v_w_fin, v_b_fin):
    given = dict(x=x, c=c, norm_mix=norm_mix, norm_ffn=norm_ffn, w_mod=w_mod, b_mod=b_mod, w_qkv=w_qkv, w_o_attn=w_o_attn, w_in_ssm=w_in_ssm, a_re=a_re, a_im=a_im, log_dt=log_dt, b_re=b_re, b_im=b_im, c_re=c_re, c_im=c_im, d_skip=d_skip, w_glu=w_glu, b_glu=b_glu, w_o_ssm=w_o_ssm, w_up=w_up, conv_w=conv_w, conv_b=conv_b, w_down=w_down, norm_out=norm_out, w_fin=w_fin, b_fin=b_fin, loss_target=loss_target, m_norm_mix=m_norm_mix, m_norm_ffn=m_norm_ffn, m_w_mod=m_w_mod, m_b_mod=m_b_mod, m_w_qkv=m_w_qkv, m_w_o_attn=m_w_o_attn, m_w_in_ssm=m_w_in_ssm, m_a_re=m_a_re, m_a_im=m_a_im, m_log_dt=m_log_dt, m_b_re=m_b_re, m_b_im=m_b_im, m_c_re=m_c_re, m_c_im=m_c_im, m_d_skip=m_d_skip, m_w_glu=m_w_glu, m_b_glu=m_b_glu, m_w_o_ssm=m_w_o_ssm, m_w_up=m_w_up, m_conv_w=m_conv_w, m_conv_b=m_conv_b, m_w_down=m_w_down, m_norm_out=m_norm_out, m_w_fin=m_w_fin, m_b_fin=m_b_fin, v_norm_mix=v_norm_mix, v_norm_ffn=v_norm_ffn, v_w_mod=v_w_mod, v_b_mod=v_b_mod, v_w_qkv=v_w_qkv, v_w_o_attn=v_w_o_attn, v_w_in_ssm=v_w_in_ssm, v_a_re=v_a_re, v_a_im=v_a_im, v_log_dt=v_log_dt, v_b_re=v_b_re, v_b_im=v_b_im, v_c_re=v_c_re, v_c_im=v_c_im, v_d_skip=v_d_skip, v_w_glu=v_w_glu, v_b_glu=v_b_glu, v_w_o_ssm=v_w_o_ssm, v_w_up=v_w_up, v_conv_w=v_conv_w, v_conv_b=v_conv_b, v_w_down=v_w_down, v_norm_out=v_norm_out, v_w_fin=v_w_fin, v_b_fin=v_b_fin)
    weights = {n: given[n] for n in TWIN_WEIGHTS}
    shared = {n: given[n] for n in SHARED_INPUTS}
    per_example = {n: given[n] for n in ['x', 'c']}
    grad_fn = _jax.value_and_grad(_loss, argnums=(0, 1))

    def one_microbatch(ex, loss_target):
        ex = dict(ex)
        diff = ex.pop(TWIN_DIFF_INPUT)
        return grad_fn(weights, diff, {**shared, **ex}, loss_target)

    if N_MICROBATCH == 1:
        loss, (grad_w, grad_x) = one_microbatch(per_example, given["loss_target"])
    else:
        def body(carry, xs):
            loss_sum, grad_sum = carry
            l_k, (gw_k, gx_k) = one_microbatch(xs[0], xs[1])
            with _jax.named_scope("update"):
                return (loss_sum + l_k, _jax.tree.map(_jnp.add, grad_sum, gw_k)), gx_k

        init = (_jnp.zeros((), _jnp.float32), _jax.tree.map(_jnp.zeros_like, weights))
        (loss, grad_w), grad_x = _jax.lax.scan(body, init, (per_example, given["loss_target"]))
    with _jax.named_scope("update"):
        delta_w, new_m, new_v = {}, {}, {}
        for n in TWIN_WEIGHTS:
            delta_w[n], new_m[n], new_v[n] = _adamw(weights[n], grad_w[n], given["m_" + n], given["v_" + n])
    return (loss, grad_x, *[grad_w[n] for n in TWIN_WEIGHTS], *[delta_w[n] for n in TWIN_WEIGHTS],
            *[new_m[n] for n in TWIN_WEIGHTS], *[new_v[n] for n in TWIN_WEIGHTS])
```

```python
import functools
import math

import jax
import jax.numpy as jnp
from jax import lax
from jax.experimental import pallas as pl
from jax.experimental.pallas import tpu as pltpu

F32 = jnp.float32
BF16 = jnp.bfloat16
MESH = pl.DeviceIdType.MESH

N_DEV = 8
HEAD_DIM = 64
ATT_BLK = 128
SSM_H = 16
SSM_P = 64
SSM_L = 16
EPS = 1e-6
ADAM_LR, ADAM_B1, ADAM_B2, ADAM_EPS, ADAM_WD, ADAM_STEP = 0.001, 0.9, 0.999, 1e-08, 0.01, 10
V7X_VMEM_LIMIT = 56 * 1024 * 1024
LANE = 128

_NN = (((1,), (0,)), ((), ()))
_NT = (((1,), (1,)), ((), ()))
_TN = (((0,), (0,)), ((), ()))


def _cp(sem):
    return pltpu.CompilerParams(dimension_semantics=sem, vmem_limit_bytes=V7X_VMEM_LIMIT)


def _tile(n, pref):
    if n <= pref:
        return n
    t = pref - pref % 16
    while t >= 16:
        if n % t == 0:
            return t
        t -= 16
    return n


def _mm(name, a, b, grid, a_spec, b_spec, out_spec, out_shape, dims, acc_shape):
    nk = grid[-1]
    kax = len(grid) - 1

    def body(a_ref, b_ref, o_ref, acc_ref):
        k = pl.program_id(kax)

        @pl.when(k == 0)
        def _():
            acc_ref[...] = jnp.zeros(acc_shape, F32)

        acc_ref[...] += lax.dot_general(a_ref[...].astype(BF16), b_ref[...].astype(BF16), dims,
                                        preferred_element_type=F32)

        @pl.when(k == nk - 1)
        def _():
            o_ref[...] = acc_ref[...].astype(o_ref.dtype)

    return pl.pallas_call(
        body, name=name, grid=grid, in_specs=[a_spec, b_spec], out_specs=out_spec, out_shape=out_shape,
        scratch_shapes=[pltpu.VMEM(acc_shape, F32)],
        compiler_params=_cp(("parallel",) * kax + ("arbitrary",)))(a, b)


def _mm_terms(name, terms, grid, out_spec, out_shape):
    n = len(terms)
    dims = [t[4] for t in terms]

    def body(*refs):
        o_ref = refs[2 * n]
        acc = None
        for i in range(n):
            d = lax.dot_general(refs[2 * i][...].astype(BF16), refs[2 * i + 1][...].astype(BF16), dims[i],
                                preferred_element_type=F32)
            acc = d if acc is None else acc + d
        o_ref[...] = acc.astype(o_ref.dtype)

    ops, specs = [], []
    for a, a_spec, b, b_spec, _ in terms:
        ops += [a, b]
        specs += [a_spec, b_spec]
    return pl.pallas_call(body, name=name, grid=grid, in_specs=specs, out_specs=out_spec, out_shape=out_shape,
                          compiler_params=_cp(("parallel",) * len(grid)))(*ops)


def _mm_nn(name, a, w, out_dtype):
    m, k = a.shape
    n = w.shape[1]
    tm, tn, tk = _tile(m, 512), _tile(n, 1024), _tile(k, 1024)
    return _mm(name, a, w, (m // tm, n // tn, k // tk),
               pl.BlockSpec((tm, tk), lambda i, j, kk: (i, kk)), pl.BlockSpec((tk, tn), lambda i, j, kk: (kk, j)),
               pl.BlockSpec((tm, tn), lambda i, j, kk: (i, j)), jax.ShapeDtypeStruct((m, n), out_dtype), _NN, (tm, tn))


def _mm_nt(name, a, w, out_dtype):
    m, n = a.shape
    k = w.shape[0]
    tm, tko, tn = _tile(m, 512), _tile(k, 1024), _tile(n, 1024)
    return _mm(name, a, w, (m // tm, k // tko, n // tn),
               pl.BlockSpec((tm, tn), lambda i, j, kk: (i, kk)), pl.BlockSpec((tko, tn), lambda i, j, kk: (j, kk)),
               pl.BlockSpec((tm, tko), lambda i, j, kk: (i, j)), jax.ShapeDtypeStruct((m, k), out_dtype), _NT, (tm, tko))


def _mm_tn(name, a, b, out_dtype):
    t, m = a.shape
    n = b.shape[1]
    tm, tn, tk = _tile(m, 512), _tile(n, 1024), _tile(t, 1024)
    return _mm(name, a, b, (m // tm, n // tn, t // tk),
               pl.BlockSpec((tk, tm), lambda i, j, kk: (kk, i)), pl.BlockSpec((tk, tn), lambda i, j, kk: (kk, j)),
               pl.BlockSpec((tm, tn), lambda i, j, kk: (i, j)), jax.ShapeDtypeStruct((m, n), out_dtype), _TN, (tm, tn))


def _norm_mod_fwd(name, x, g, shift, scale, nb):
    t, d = x.shape
    s = t // nb
    tr = _tile(s, 512)
    nt = s // tr

    def body(x_ref, g_ref, sh_ref, sc_ref, h_ref):
        xv = x_ref[...]
        r = lax.rsqrt(jnp.mean(xv * xv, axis=-1, keepdims=True) + EPS)
        y = xv * r * g_ref[...]
        h_ref[...] = (y * (1.0 + sc_ref[...]) + sh_ref[...]).astype(h_ref.dtype)

    row = pl.BlockSpec((tr, d), lambda b, i: (b * nt + i, 0))
    vec = pl.BlockSpec((None, 1, d), lambda b, i: (b, 0, 0))
    return pl.pallas_call(body, name=name, grid=(nb, nt),
                          in_specs=[row, pl.BlockSpec((1, d), lambda b, i: (0, 0)), vec, vec],
                          out_specs=row, out_shape=jax.ShapeDtypeStruct((t, d), BF16),
                          compiler_params=_cp(("parallel", "parallel")))(x, g, shift, scale)


def _norm_mod_bwd(name, dh, x, g, shift, scale, dres, target, nb):
    t, d = x.shape
    s = t // nb
    tr = _tile(s, 256)
    nt = s // tr
    final = target is not None

    def body(*refs):
        if final:
            x_ref, g_ref, sh_ref, sc_ref, tg_ref, dx_ref, dg_ref, dsh_ref, dsc_ref, loss_ref = refs
        else:
            dh_ref, x_ref, g_ref, sc_ref, dres_ref, dx_ref, dg_ref, dsh_ref, dsc_ref = refs
        b, i = pl.program_id(0), pl.program_id(1)
        xv = x_ref[...]
        gv = g_ref[...]
        r = lax.rsqrt(jnp.mean(xv * xv, axis=-1, keepdims=True) + EPS)
        nrm = xv * r
        y = nrm * gv
        one_sc = 1.0 + sc_ref[...]
        if final:
            err = y * one_sc + sh_ref[...] - tg_ref[...]
            dhv = err * (1.0 / d)
        else:
            dhv = dh_ref[...].astype(F32)
        dy = dhv * one_sc
        dn = dy * gv
        dxv = r * (dn - nrm * jnp.mean(dn * nrm, axis=-1, keepdims=True))
        if final:
            dx_ref[...] = dxv
        else:
            dx_ref[...] = dres_ref[...] + dxv

        @pl.when(i == 0)
        def _():
            dsh_ref[...] = jnp.zeros_like(dsh_ref)
            dsc_ref[...] = jnp.zeros_like(dsc_ref)

        @pl.when((i == 0) & (b == 0))
        def _():
            dg_ref[...] = jnp.zeros_like(dg_ref)
            if final:
                loss_ref[...] = jnp.zeros_like(loss_ref)

        dsh_ref[...] += jnp.sum(dhv, axis=0, keepdims=True)
        dsc_ref[...] += jnp.sum(dhv * y, axis=0, keepdims=True)
        dg_ref[...] += jnp.sum(dy * nrm, axis=0, keepdims=True)
        if final:
            loss_ref[...] += (0.5 / d) * jnp.sum(err * err)

    row = pl.BlockSpec((tr, d), lambda b, i: (b * nt + i, 0))
    vec = pl.BlockSpec((None, 1, d), lambda b, i: (b, 0, 0))
    gsp = pl.BlockSpec((1, d), lambda b, i: (0, 0))
    out_specs = [row, gsp, vec, vec]
    out_shape = [jax.ShapeDtypeStruct((t, d), F32), jax.ShapeDtypeStruct((1, d), F32),
                 jax.ShapeDtypeStruct((nb, 1, d), F32), jax.ShapeDtypeStruct((nb, 1, d), F32)]
    if final:
        ins, in_specs = [x, g, shift, scale, target], [row, gsp, vec, vec, row]
        out_specs.append(pl.BlockSpec((8, LANE), lambda b, i: (0, 0)))
        out_shape.append(jax.ShapeDtypeStruct((8, LANE), F32))
    else:
        ins, in_specs = [dh, x, g, scale, dres], [row, row, gsp, vec, row]
    return pl.pallas_call(body, name=name, grid=(nb, nt), in_specs=in_specs, out_specs=out_specs,
                          out_shape=out_shape, compiler_params=_cp(("arbitrary", "arbitrary")))(*ins)


def _gate_add(name, x, y, gate, nb):
    t, d = x.shape
    s = t // nb
    tr = _tile(s, 512)
    nt = s // tr

    def body(x_ref, y_ref, g_ref, o_ref):
        o_ref[...] = x_ref[...] + g_ref[...] * y_ref[...]

    row = pl.BlockSpec((tr, d), lambda b, i: (b * nt + i, 0))
    vec = pl.BlockSpec((None, 1, d), lambda b, i: (b, 0, 0))
    return pl.pallas_call(body, name=name, grid=(nb, nt), in_specs=[row, row, vec], out_specs=row,
                          out_shape=jax.ShapeDtypeStruct((t, d), F32),
                          compiler_params=_cp(("parallel", "parallel")))(x, y, gate)


def _gate_bwd(name, dx, y, gate, nb):
    t, d = dx.shape
    s = t // nb
    tr = _tile(s, 512)
    nt = s // tr

    def body(dx_ref, y_ref, g_ref, dy_ref, dg_ref):
        dxv = dx_ref[...]
        dy_ref[...] = (g_ref[...] * dxv).astype(dy_ref.dtype)

        @pl.when(pl.program_id(1) == 0)
        def _():
            dg_ref[...] = jnp.zeros_like(dg_ref)

        dg_ref[...] += jnp.sum(dxv * y_ref[...], axis=0, keepdims=True)

    row = pl.BlockSpec((tr, d), lambda b, i: (b * nt + i, 0))
    vec = pl.BlockSpec((None, 1, d), lambda b, i: (b, 0, 0))
    return pl.pallas_call(body, name=name, grid=(nb, nt), in_specs=[row, row, vec], out_specs=[row, vec],
                          out_shape=[jax.ShapeDtypeStruct((t, d), BF16), jax.ShapeDtypeStruct((nb, 1, d), F32)],
                          compiler_params=_cp(("parallel", "arbitrary")))(dx, y, gate)


def _log_sigmoid(z):
    return jnp.minimum(z, 0.0) - jnp.log(1.0 + jnp.exp(-jnp.abs(z)))


def _split_dot(v, tri):
    hi = v.astype(BF16)
    lo = (v - hi.astype(F32)).astype(BF16)
    return (jnp.dot(hi, tri, preferred_element_type=F32) + jnp.dot(lo, tri, preferred_element_type=F32))


def _attn_fwd(qkv, nb, s, d):
    t = nb * s
    npair = d // LANE
    blk = min(ATT_BLK, s)
    nq = s // blk
    scale = HEAD_DIM ** -0.5

    assert nq <= HEAD_DIM, "one carry lane per key block and head"

    assert blk == LANE, "the running sums are kept one (blk, 128) tile wide"

    def body(q_ref, k_ref, v_ref, o_ref, car_ref, acc_s, run_s):
        qi = pl.program_id(2)
        q = q_ref[...]
        lane = lax.broadcasted_iota(jnp.int32, (1, LANE), 1)
        row = lax.broadcasted_iota(jnp.int32, (blk, blk), 0)
        col = lax.broadcasted_iota(jnp.int32, (blk, blk), 1)
        tri = (row > col).astype(BF16)
        car_ref[...] = jnp.zeros((blk, LANE), F32)
        for hh in range(LANE // HEAD_DIM):
            hm = (lane // HEAD_DIM) == hh
            qh = jnp.where(hm, q, jnp.zeros_like(q))
            acc_s[...] = jnp.zeros((blk, LANE), F32)
            run_s[...] = jnp.zeros((blk, LANE), F32)

            def step(jj, carry, qh=qh, hh=hh):
                j = qi - jj
                k0 = pl.multiple_of(j * blk, blk)
                kj = k_ref[pl.ds(k0, blk), :]
                vj = v_ref[pl.ds(k0, blk), :]
                z = lax.dot_general(qh, kj, _NT, preferred_element_type=F32) * scale
                mask = (k0 + col) < (qi * blk + row)
                lb = _log_sigmoid(z)
                l1 = jnp.where(mask, lb - z, 0.0)
                run = run_s[...]
                suf = _split_dot(l1, tri) + run
                w = jnp.where(mask, jnp.exp(lb + suf), 0.0)
                acc_s[...] += jnp.dot(w.astype(BF16), vj, preferred_element_type=F32)
                car_ref[...] = jnp.where(lane == hh * HEAD_DIM + j, run, car_ref[...])
                run_s[...] = run + jnp.sum(l1, axis=1, keepdims=True)
                return carry

            lax.fori_loop(0, qi + 1, step, 0)
            if hh == 0:
                o_ref[...] = acc_s[...].astype(o_ref.dtype)
            else:
                o_ref[...] = jnp.where(hm, acc_s[...].astype(o_ref.dtype), o_ref[...])

    qspec = pl.BlockSpec((blk, LANE), lambda b, p, i: (b * nq + i, p))
    return pl.pallas_call(
        body, name="attn_fwd", grid=(nb, npair, nq),
        in_specs=[qspec,
                  pl.BlockSpec((s, LANE), lambda b, p, i: (b, npair + p)),
                  pl.BlockSpec((s, LANE), lambda b, p, i: (b, 2 * npair + p))],
        out_specs=[qspec, qspec],
        out_shape=[jax.ShapeDtypeStruct((t, d), BF16), jax.ShapeDtypeStruct((t, d), F32)],
        scratch_shapes=[pltpu.VMEM((blk, LANE), F32), pltpu.VMEM((blk, LANE), F32)],
        compiler_params=_cp(("parallel", "parallel", "parallel")))(qkv, qkv, qkv)


def _attn_bwd(qkv, car, do, nb, s, d):
    t = nb * s
    npair = d // LANE
    blk = min(ATT_BLK, s)
    nq = s // blk
    scale = HEAD_DIM ** -0.5

    def body(q_ref, k_ref, v_ref, car_ref, do_ref, dq_ref, dk_ref, dv_ref, dk_acc, dv_acc, dq_s, rune_s):
        qi = pl.program_id(2)

        @pl.when(qi == 0)
        def _():
            dk_acc[...] = jnp.zeros_like(dk_acc)
            dv_acc[...] = jnp.zeros_like(dv_acc)

        q = q_ref[...]
        dov = do_ref[...]
        car = car_ref[...]
        lane = lax.broadcasted_iota(jnp.int32, (1, LANE), 1)
        row = lax.broadcasted_iota(jnp.int32, (blk, blk), 0)
        col = lax.broadcasted_iota(jnp.int32, (blk, blk), 1)
        tri_suf = (row > col).astype(BF16)
        tri_pre = (row < col).astype(BF16)
        for hh in range(LANE // HEAD_DIM):
            hm = (lane // HEAD_DIM) == hh
            qh = jnp.where(hm, q, jnp.zeros_like(q))
            doh = jnp.where(hm, dov, jnp.zeros_like(dov))
            dq_s[...] = jnp.zeros((blk, LANE), F32)
            rune_s[...] = jnp.zeros((blk, LANE), F32)

            def step(j, carry, qh=qh, doh=doh, hh=hh):
                k0 = pl.multiple_of(j * blk, blk)
                kj = k_ref[pl.ds(k0, blk), :]
                vj = v_ref[pl.ds(k0, blk), :]
                z = lax.dot_general(qh, kj, _NT, preferred_element_type=F32) * scale
                mask = (k0 + col) < (qi * blk + row)
                lb = _log_sigmoid(z)
                l1u = lb - z
                l1 = jnp.where(mask, l1u, 0.0)
                run = jnp.sum(jnp.where(lane == hh * HEAD_DIM + j, car, 0.0), axis=1, keepdims=True)
                suf = _split_dot(l1, tri_suf) + run
                a = jnp.where(mask, jnp.exp(lb + suf), 0.0)
                da = lax.dot_general(doh, vj, _NT, preferred_element_type=F32)
                e = da * a
                run_e = rune_s[...]
                pre_e = _split_dot(e, tri_pre) + run_e
                dz = jnp.where(mask, e * jnp.exp(l1u) - pre_e * jnp.exp(lb), 0.0) * scale
                dzb = dz.astype(BF16)
                dq_s[...] += jnp.dot(dzb, kj, preferred_element_type=F32)
                dk_acc[pl.ds(k0, blk), :] += lax.dot_general(dzb, qh, _TN, preferred_element_type=F32)
                dv_acc[pl.ds(k0, blk), :] += lax.dot_general(a.astype(BF16), doh, _TN, preferred_element_type=F32)
                rune_s[...] = run_e + jnp.sum(e, axis=1, keepdims=True)
                return carry

            lax.fori_loop(0, qi + 1, step, 0)
            if hh == 0:
                dq_ref[...] = dq_s[...].astype(dq_ref.dtype)
            else:
                dq_ref[...] = jnp.where(hm, dq_s[...].astype(dq_ref.dtype), dq_ref[...])

        @pl.when(qi == nq - 1)
        def _():
            dk_ref[...] = dk_acc[...].astype(dk_ref.dtype)
            dv_ref[...] = dv_acc[...].astype(dv_ref.dtype)

    qspec = pl.BlockSpec((blk, LANE), lambda b, p, i: (b * nq + i, p))
    kvout = pl.BlockSpec((s, LANE), lambda b, p, i: (b, p))
    sds = jax.ShapeDtypeStruct((t, d), BF16)
    return pl.pallas_call(
        body, name="attn_bwd", grid=(nb, npair, nq),
        in_specs=[qspec,
                  pl.BlockSpec((s, LANE), lambda b, p, i: (b, npair + p)),
                  pl.BlockSpec((s, LANE), lambda b, p, i: (b, 2 * npair + p)),
                  qspec, qspec],
        out_specs=[qspec, kvout, kvout], out_shape=[sds, sds, sds],
        scratch_shapes=[pltpu.VMEM((s, LANE), F32), pltpu.VMEM((s, LANE), F32),
                        pltpu.VMEM((blk, LANE), F32), pltpu.VMEM((blk, LANE), F32)],
        compiler_params=_cp(("parallel", "parallel", "arbitrary")))(qkv, qkv, qkv, car, do)


def _conv3(u_ref, w, bias, c, r0, rc):
    x = u_ref[pl.ds(r0, rc), :].astype(F32)
    p0 = pl.multiple_of(jnp.maximum(r0 - 16, 0), 16)
    prev = u_ref[pl.ds(p0, 16), :].astype(F32)
    prev = jnp.where(c > 0, prev, 0.0)
    row = lax.broadcasted_iota(jnp.int32, (rc, 1), 0)
    s1 = jnp.where(row == 0, prev[15:16, :], pltpu.roll(x, 1, 0))
    s2 = jnp.where(row == 0, prev[14:15, :], jnp.where(row == 1, prev[15:16, :], pltpu.roll(x, 2, 0)))
    cv = w[2:3, :] * x + w[1:2, :] * s1 + w[0:1, :] * s2 + bias
    return cv, x, s1, s2


def _sigmoid(x):
    return 1.0 / (1.0 + jnp.exp(-x))


def _ffn_act_fwd(name, up8, cw8, cb8, nb, s):
    _, t, c_w = up8.shape
    rc = _tile(s, 256)
    nch = s // rc
    half = N_DEV // 2

    def body(ug_ref, uv_ref, wg_ref, wv_ref, bg_ref, bv_ref, act_ref):
        wg, wv, bg, bv = wg_ref[...], wv_ref[...], bg_ref[...], bv_ref[...]

        def chunk(c, carry):
            r0 = pl.multiple_of(c * rc, rc)
            cg = _conv3(ug_ref, wg, bg, c, r0, rc)[0]
            cv = _conv3(uv_ref, wv, bv, c, r0, rc)[0]
            act_ref[pl.ds(r0, rc), :] = (cg * _sigmoid(cg) * cv).astype(act_ref.dtype)
            return carry

        lax.fori_loop(0, nch, chunk, 0)

    def slab(off):
        return pl.BlockSpec((None, s, c_w), lambda k, b: (k + off, b, 0))

    def par(rows, off):
        return pl.BlockSpec((None, rows, c_w), lambda k, b: (k + off, 0, 0))

    return pl.pallas_call(
        body, name=name, grid=(half, nb),
        in_specs=[slab(0), slab(half), par(3, 0), par(3, half), par(1, 0), par(1, half)],
        out_specs=pl.BlockSpec((None, s, c_w), lambda k, b: (k, b, 0)),
        out_shape=jax.ShapeDtypeStruct((half, t, c_w), BF16),
        compiler_params=_cp(("parallel", "parallel")))(up8, up8, cw8, cw8, cb8, cb8)


def _ffn_act_bwd(name, up8, dact4, cw8, cb8, nb, s):
    _, t, c_w = up8.shape
    rc = _tile(s, 256)
    nch = s // rc
    half = N_DEV // 2

    def body(us_ref, uo_ref, da_ref, ws_ref, wo_ref, bs_ref, bo_ref, dup_ref, dcw_ref, dcb_ref):
        k = pl.program_id(0)
        ws, wo, bs, bo = ws_ref[...], wo_ref[...], bs_ref[...], bo_ref[...]
        is_gate = k < half
        row = lax.broadcasted_iota(jnp.int32, (rc, 1), 0)

        @pl.when(pl.program_id(1) == 0)
        def _():
            dcw_ref[...] = jnp.zeros_like(dcw_ref)
            dcb_ref[...] = jnp.zeros_like(dcb_ref)

        def chunk(i, carry):
            n0, n1, a0, a1, a2, ab = carry
            c = nch - 1 - i
            r0 = pl.multiple_of(c * rc, rc)
            cs, x, s1, s2 = _conv3(us_ref, ws, bs, c, r0, rc)
            co = _conv3(uo_ref, wo, bo, c, r0, rc)[0]
            da = da_ref[pl.ds(r0, rc), :].astype(F32)
            gt = jnp.where(is_gate, cs, co)
            vl = jnp.where(is_gate, co, cs)
            sg = _sigmoid(gt)
            d_gate = da * vl * sg * (1.0 + gt * (1.0 - sg))
            d_val = da * gt * sg
            dcv = jnp.where(is_gate, d_gate, d_val)
            t1 = jnp.where(row == rc - 1, n0, pltpu.roll(dcv, rc - 1, 0))
            t2 = jnp.where(row == rc - 2, n0, jnp.where(row == rc - 1, n1, pltpu.roll(dcv, rc - 2, 0)))
            dup = ws[2:3, :] * dcv + ws[1:2, :] * t1 + ws[0:1, :] * t2
            dup_ref[pl.ds(r0, rc), :] = dup.astype(dup_ref.dtype)
            a2 = a2 + jnp.sum(dcv * x, axis=0, keepdims=True)
            a1 = a1 + jnp.sum(dcv * s1, axis=0, keepdims=True)
            a0 = a0 + jnp.sum(dcv * s2, axis=0, keepdims=True)
            ab = ab + jnp.sum(dcv, axis=0, keepdims=True)
            return dcv[0:1, :], dcv[1:2, :], a0, a1, a2, ab

        z = jnp.zeros((1, c_w), F32)
        _, _, a0, a1, a2, ab = lax.fori_loop(0, nch, chunk, (z, z, z, z, z, z))
        dcw_ref[0:1, :] += a0
        dcw_ref[1:2, :] += a1
        dcw_ref[2:3, :] += a2
        dcb_ref[...] += ab

    def slab(fn):
        return pl.BlockSpec((None, s, c_w), lambda k, b: (fn(k), b, 0))

    def par(rows, fn):
        return pl.BlockSpec((None, rows, c_w), lambda k, b: (fn(k), 0, 0))

    same = lambda k: k
    other = lambda k: (k + half) % N_DEV
    return pl.pallas_call(
        body, name=name, grid=(N_DEV, nb),
        in_specs=[slab(same), slab(other), slab(lambda k: k % half), par(3, same), par(3, other),
                  par(1, same), par(1, other)],
        out_specs=[slab(same), par(3, same), par(1, same)],
        out_shape=[jax.ShapeDtypeStruct((N_DEV, t, c_w), BF16), jax.ShapeDtypeStruct((N_DEV, 3, c_w), F32),
                   jax.ShapeDtypeStruct((N_DEV, 1, c_w), F32)],
        compiler_params=_cp(("parallel", "arbitrary")))(up8, up8, dact4, cw8, cw8, cb8, cb8)


_GELU_C0 = math.sqrt(2.0 / math.pi)
_GELU_C1 = 0.044715


def _rowwise(name, body, ins, in_kinds, out_kinds, t, d, tr_pref=512):
    tr = _tile(t, tr_pref)
    row = pl.BlockSpec((tr, d), lambda i: (i, 0))
    vec = pl.BlockSpec((1, d), lambda i: (0, 0))
    in_specs = [row if k == "row" else vec for k in in_kinds]
    out_specs = [row if k[0] == "row" else vec for k in out_kinds]
    out_shape = [jax.ShapeDtypeStruct((t, d) if k[0] == "row" else (1, d), k[1]) for k in out_kinds]
    has_acc = any(k[0] == "acc" for k in out_kinds)
    return pl.pallas_call(body, name=name, grid=(t // tr,), in_specs=in_specs, out_specs=out_specs,
                          out_shape=out_shape,
                          compiler_params=_cp(("arbitrary",) if has_acc else ("parallel",)))(*ins)


def _ssm_post_fwd(ys, u, dskip):
    t, d = ys.shape

    def body(ys_ref, u_ref, ds_ref, y_ref, z_ref):
        y = ys_ref[...] + ds_ref[...] * u_ref[...].astype(F32)
        y_ref[...] = y
        th = jnp.tanh(_GELU_C0 * (y + _GELU_C1 * y * y * y))
        z_ref[...] = (0.5 * y * (1.0 + th)).astype(z_ref.dtype)

    return _rowwise("ssm_post_fwd", body, [ys, u, dskip], ["row", "row", "vec"],
                    [("row", F32), ("row", BF16)], t, d)


def _glu_fwd(z, gl, bglu):
    t, d = z.shape

    def body(z_ref, gl_ref, b_ref, o_ref):
        o_ref[...] = (z_ref[...].astype(F32) * _sigmoid(gl_ref[...] + b_ref[...])).astype(o_ref.dtype)

    return _rowwise("glu_fwd", body, [z, gl, bglu], ["row", "row", "vec"], [("row", BF16)], t, d)[0]


def _glu_bwd(dgg, z, gl, bglu):
    t, d = z.shape

    def body(dg_ref, z_ref, gl_ref, b_ref, dgl_ref, dz_ref, db_ref):
        sg = _sigmoid(gl_ref[...] + b_ref[...])
        dg = dg_ref[...]
        dgl = dg * z_ref[...].astype(F32) * sg * (1.0 - sg)
        dgl_ref[...] = dgl.astype(dgl_ref.dtype)
        dz_ref[...] = dg * sg

        @pl.when(pl.program_id(0) == 0)
        def _():
            db_ref[...] = jnp.zeros_like(db_ref)

        db_ref[...] += jnp.sum(dgl, axis=0, keepdims=True)

    return _rowwise("glu_bwd", body, [dgg, z, gl, bglu], ["row", "row", "row", "vec"],
                    [("row", BF16), ("row", F32), ("acc", F32)], t, d)


def _ssm_post_bwd(dz1, dz2, y, u, dskip):
    t, d = y.shape

    def body(a_ref, b_ref, y_ref, u_ref, ds_ref, dy_ref, du_ref, dd_ref):
        yv = y_ref[...]
        inner = _GELU_C0 * (yv + _GELU_C1 * yv * yv * yv)
        th = jnp.tanh(inner)
        dgelu = 0.5 * (1.0 + th) + 0.5 * yv * (1.0 - th * th) * _GELU_C0 * (1.0 + 3.0 * _GELU_C1 * yv * yv)
        dy = (a_ref[...] + b_ref[...]) * dgelu
        dy_ref[...] = dy.astype(dy_ref.dtype)
        du_ref[...] = dy * ds_ref[...]

        @pl.when(pl.program_id(0) == 0)
        def _():
            dd_ref[...] = jnp.zeros_like(dd_ref)

        dd_ref[...] += jnp.sum(dy * u_ref[...].astype(F32), axis=0, keepdims=True)

    return _rowwise("ssm_post_bwd", body, [dz1, dz2, y, u, dskip], ["row", "row", "row", "row", "vec"],
                    [("row", BF16), ("row", F32), ("acc", F32)], t, d)


def _add_cast(a, b):
    t, d = a.shape

    def body(a_ref, b_ref, o_ref):
        o_ref[...] = (a_ref[...] + b_ref[...]).astype(o_ref.dtype)

    return _rowwise("add_cast", body, [a, b], ["row", "row"], [("row", BF16)], t, d)[0]


def _ssm_scan(e_re, e_im, lam_re, lam_im, nb):
    r, n = e_re.shape
    nc = r // nb
    cb = _tile(n, 512)

    def body(er_ref, ei_ref, lr_ref, li_ref, xr_ref, xi_ref):
        lr, li = lr_ref[...], li_ref[...]
        rid = lax.broadcasted_iota(jnp.int32, (8, 1), 0)
        for b in range(nb):
            def tile(i, carry, b=b):
                xr, xi = carry
                r0 = pl.multiple_of(b * nc + i * 8, 8)
                er, ei = er_ref[pl.ds(r0, 8), :], ei_ref[pl.ds(r0, 8), :]
                outr, outi = jnp.zeros((8, cb), F32), jnp.zeros((8, cb), F32)
                for j in range(8):
                    outr = jnp.where(rid == j, xr, outr)
                    outi = jnp.where(rid == j, xi, outi)
                    xr, xi = lr * xr - li * xi + er[j:j + 1, :], li * xr + lr * xi + ei[j:j + 1, :]
                xr_ref[pl.ds(r0, 8), :] = outr
                xi_ref[pl.ds(r0, 8), :] = outi
                return xr, xi

            z = jnp.zeros((1, cb), F32)
            lax.fori_loop(0, nc // 8, tile, (z, z))

    mat = pl.BlockSpec((r, cb), lambda j: (0, j))
    vec = pl.BlockSpec((1, cb), lambda j: (0, j))
    sds = jax.ShapeDtypeStruct((r, n), F32)
    return pl.pallas_call(body, name="ssm_scan", grid=(n // cb,), in_specs=[mat, mat, vec, vec],
                          out_specs=[mat, mat], out_shape=[sds, sds],
                          compiler_params=_cp(("parallel",)))(e_re, e_im, lam_re, lam_im)


def _ssm_scan_bwd(dxp_re, dxp_im, xp_re, xp_im, lam_re, lam_im, nb):
    r, n = dxp_re.shape
    nc = r // nb
    cb = _tile(n, 512)

    def body(dr_ref, di_ref, xr_ref, xi_ref, lr_ref, li_ref, er_ref, ei_ref, dlr_ref, dli_ref):
        lr, li = lr_ref[...], li_ref[...]
        rid = lax.broadcasted_iota(jnp.int32, (8, 1), 0)
        z = jnp.zeros((1, cb), F32)
        alr, ali = z, z
        for b in range(nb):
            def tile(i, carry, b=b):
                gr, gi, alr, ali = carry
                r0 = pl.multiple_of(b * nc + (nc // 8 - 1 - i) * 8, 8)
                dr, di = dr_ref[pl.ds(r0, 8), :], di_ref[pl.ds(r0, 8), :]
                xr, xi = xr_ref[pl.ds(r0, 8), :], xi_ref[pl.ds(r0, 8), :]
                outr, outi = jnp.zeros((8, cb), F32), jnp.zeros((8, cb), F32)
                for j in range(7, -1, -1):
                    outr = jnp.where(rid == j, gr, outr)
                    outi = jnp.where(rid == j, gi, outi)
                    xrj, xij = xr[j:j + 1, :], xi[j:j + 1, :]
                    alr = alr + gr * xrj + gi * xij
                    ali = ali + gi * xrj - gr * xij
                    gr, gi = dr[j:j + 1, :] + lr * gr + li * gi, di[j:j + 1, :] + lr * gi - li * gr
                er_ref[pl.ds(r0, 8), :] = outr
                ei_ref[pl.ds(r0, 8), :] = outi
                return gr, gi, alr, ali

            _, _, alr, ali = lax.fori_loop(0, nc // 8, tile, (z, z, alr, ali))
        dlr_ref[...] = alr
        dli_ref[...] = ali

    mat = pl.BlockSpec((r, cb), lambda j: (0, j))
    vec = pl.BlockSpec((1, cb), lambda j: (0, j))
    sds = jax.ShapeDtypeStruct((r, n), F32)
    vds = jax.ShapeDtypeStruct((1, n), F32)
    return pl.pallas_call(body, name="ssm_scan_bwd", grid=(n // cb,), in_specs=[mat, mat, mat, mat, vec, vec],
                          out_specs=[mat, mat, vec, vec], out_shape=[sds, sds, vds, vds],
                          compiler_params=_cp(("parallel",)))(dxp_re, dxp_im, xp_re, xp_im, lam_re, lam_im)


def _ssm_operators(a_re, a_im, log_dt, b_re, b_im, c_re, c_im):
    g, p = a_re.shape
    h = b_re.shape[-1]
    ln = SSM_L
    hp = lax.Precision.HIGHEST
    lam = lax.complex(a_re, a_im)
    ldt = lam * jnp.exp(log_dt)[:, None]
    lam_bar = jnp.exp(ldt)
    bbar = ((lam_bar - 1.0) / lam)[..., None] * lax.complex(b_re, b_im)
    cm = lax.complex(c_re, c_im)
    steps = jnp.arange(ln + 1, dtype=F32)
    pw = jnp.exp(ldt[:, None, :] * steps[None, :, None])
    kd = jnp.einsum("ghp,gdp,gpk->gdhk", cm, pw[:, :ln], bbar, precision=hp).real
    sig = jnp.arange(ln)[:, None]
    tau = jnp.arange(ln)[None, :]
    lag = tau - sig
    tm = jnp.where((lag >= 0)[None, :, :, None, None], kd[:, jnp.clip(lag, 0, ln - 1)], 0.0)
    tm = tm.transpose(0, 1, 4, 2, 3).reshape(g, ln * h, ln * h)
    wx = pw[:, ln - 1 - jnp.arange(ln)][:, :, :, None] * bbar[:, None]
    wx = wx.transpose(0, 1, 3, 2).reshape(g, ln * h, p)
    cp = cm[:, None] * pw[:, 1:ln + 1][:, :, None, :]
    cp = cp.reshape(g, ln * h, p).transpose(0, 2, 1)
    odd = (jnp.arange(g) % 2 == 1)[:, None, None]

    def pad(m, axis):
        z = jnp.zeros_like(m)
        return jnp.where(odd, jnp.concatenate([z, m], axis), jnp.concatenate([m, z], axis))

    lam_l = pw[:, ln]
    return (tm, pad(wx.real, 2), pad(wx.imag, 2), pad(cp.real, 1), pad(-cp.imag, 1),
            lam_l.real.reshape(1, g * p), lam_l.imag.reshape(1, g * p))


def _to_groups(a, nb, s):
    d = a.shape[1]
    g = d // SSM_H
    return a.reshape(nb, s // SSM_L, SSM_L, g, SSM_H).transpose(3, 0, 1, 2, 4).reshape(g, nb * (s // SSM_L), SSM_L * SSM_H)


def _from_groups(a, nb, s):
    g = a.shape[0]
    return a.reshape(g, nb, s // SSM_L, SSM_L, SSM_H).transpose(1, 2, 3, 0, 4).reshape(nb * s, g * SSM_H)


def _ssm_core_fwd(ug, ops, nb):
    tm, wxr, wxi, wyr, wyi, lam_re, lam_im = ops
    g, r, w = ug.shape
    n = g * SSM_P
    grp = lambda off: pl.BlockSpec((None, r, w), lambda j: (2 * j + off, 0, 0))
    wxs = lambda off: pl.BlockSpec((None, w, LANE), lambda j: (2 * j + off, 0, 0))
    pair_out = pl.BlockSpec((r, LANE), lambda j: (0, j))
    sds = jax.ShapeDtypeStruct((r, n), F32)
    e_re = _mm_terms("ssm_e_re", [(ug, grp(0), wxr, wxs(0), _NN), (ug, grp(1), wxr, wxs(1), _NN)], (g // 2,), pair_out, sds)
    e_im = _mm_terms("ssm_e_im", [(ug, grp(0), wxi, wxs(0), _NN), (ug, grp(1), wxi, wxs(1), _NN)], (g // 2,), pair_out, sds)
    xp_re, xp_im = _ssm_scan(e_re, e_im, lam_re, lam_im, nb)
    one = pl.BlockSpec((None, r, w), lambda j: (j, 0, 0))
    sq = pl.BlockSpec((None, w, w), lambda j: (j, 0, 0))
    pair_in = pl.BlockSpec((r, LANE), lambda j: (0, j // 2))
    wys = pl.BlockSpec((None, LANE, w), lambda j: (j, 0, 0))
    y = _mm_terms("ssm_y", [(ug, one, tm, sq, _NN), (xp_re, pair_in, wyr, wys, _NN), (xp_im, pair_in, wyi, wys, _NN)],
                  (g,), one, jax.ShapeDtypeStruct((g, r, w), F32))
    return y, xp_re, xp_im


def _ssm_core_bwd(dyg, ug, xp_re, xp_im, ops, nb):
    tm, wxr, wxi, wyr, wyi, lam_re, lam_im = ops
    g, r, w = ug.shape
    n = g * SSM_P
    grp = lambda off: pl.BlockSpec((None, r, w), lambda j: (2 * j + off, 0, 0))
    wyp = lambda off: pl.BlockSpec((None, LANE, w), lambda j: (2 * j + off, 0, 0))
    pair_out = pl.BlockSpec((r, LANE), lambda j: (0, j))
    sds = jax.ShapeDtypeStruct((r, n), F32)
    dxp_re = _mm_terms("ssm_dxp_re", [(dyg, grp(0), wyr, wyp(0), _NT), (dyg, grp(1), wyr, wyp(1), _NT)], (g // 2,), pair_out, sds)
    dxp_im = _mm_terms("ssm_dxp_im", [(dyg, grp(0), wyi, wyp(0), _NT), (dyg, grp(1), wyi, wyp(1), _NT)], (g // 2,), pair_out, sds)
    de_re, de_im, dlam_re, dlam_im = _ssm_scan_bwd(dxp_re, dxp_im, xp_re, xp_im, lam_re, lam_im, nb)
    one = pl.BlockSpec((None, r, w), lambda j: (j, 0, 0))
    sq = pl.BlockSpec((None, w, w), lambda j: (j, 0, 0))
    pair_in = pl.BlockSpec((r, LANE), lambda j: (0, j // 2))
    wxs = pl.BlockSpec((None, w, LANE), lambda j: (j, 0, 0))
    wys = pl.BlockSpec((None, LANE, w), lambda j: (j, 0, 0))
    du = _mm_terms("ssm_du", [(dyg, one, tm, sq, _NT), (de_re, pair_in, wxr, wxs, _NT), (de_im, pair_in, wxi, wxs, _NT)],
                   (g,), one, jax.ShapeDtypeStruct((g, r, w), F32))
    dtm = _mm_terms("ssm_dtm", [(ug, one, dyg, one, _TN)], (g,), sq, jax.ShapeDtypeStruct((g, w, w), F32))
    dwy = lambda nm, xp: _mm_terms(nm, [(xp, pair_in, dyg, one, _TN)], (g,), wys, jax.ShapeDtypeStruct((g, LANE, w), F32))
    dwx = lambda nm, de: _mm_terms(nm, [(ug, one, de, pair_in, _TN)], (g,), wxs, jax.ShapeDtypeStruct((g, w, LANE), F32))
    return du, (dtm, dwx("ssm_dwx_re", de_re), dwx("ssm_dwx_im", de_im), dwy("ssm_dwy_re", xp_re),
                dwy("ssm_dwy_im", xp_im), dlam_re, dlam_im)


def _modfin_fwd(c_all, w_mod, w_fin):
    n, d = c_all.shape
    nl, _, cm = w_mod.shape
    cf = w_fin.shape[1]
    width = nl * cm + cf
    hp = lax.Precision.HIGHEST

    def body(c_ref, wm_ref, wf_ref, act_ref, out_ref):
        cv = c_ref[...]
        act = cv * _sigmoid(cv)
        act_ref[...] = act
        for i in range(nl):
            out_ref[:, i * cm:(i + 1) * cm] = jnp.dot(act, wm_ref[i], preferred_element_type=F32, precision=hp)
        out_ref[:, nl * cm:] = jnp.dot(act, wf_ref[...], preferred_element_type=F32, precision=hp)

    return pl.pallas_call(body, name="modfin_fwd",
                          out_shape=[jax.ShapeDtypeStruct((n, d), F32), jax.ShapeDtypeStruct((n, width), F32)],
                          compiler_params=_cp(None))(c_all, w_mod, w_fin)


def _modfin_bwd(c_act_t, dmod_loc, dfin_loc, dall):
    d, n = c_act_t.shape
    nl, _, cm = dmod_loc.shape
    cf = dfin_loc.shape[1]
    hp = lax.Precision.HIGHEST

    def body(ct_ref, dm_ref, df_ref, da_ref, gwm_ref, gwf_ref, gb_ref):
        ct = ct_ref[...]
        for i in range(nl):
            gwm_ref[i] = jnp.dot(ct, dm_ref[i], preferred_element_type=F32, precision=hp)
        gwf_ref[...] = jnp.dot(ct, df_ref[...], preferred_element_type=F32, precision=hp)
        gb_ref[...] = jnp.sum(da_ref[...], axis=0, keepdims=True)

    return pl.pallas_call(body, name="modfin_bwd",
                          out_shape=[jax.ShapeDtypeStruct((nl, d, cm), F32), jax.ShapeDtypeStruct((d, cf), F32),
                                     jax.ShapeDtypeStruct((1, dall.shape[1]), F32)],
                          compiler_params=_cp(None))(c_act_t, dmod_loc, dfin_loc, dall)


def _adamw(name, gparts, w, m, v):
    n, r, c = gparts.shape
    tr = _tile(r, 256)
    c1 = 1.0 / (1.0 - ADAM_B1 ** ADAM_STEP)
    c2 = 1.0 / (1.0 - ADAM_B2 ** ADAM_STEP)

    def body(gp_ref, w_ref, m_ref, v_ref, g_ref, d_ref, mo_ref, vo_ref):
        gsum = gp_ref[0].astype(F32)
        for j in range(1, n):
            gsum = gsum + gp_ref[j].astype(F32)
        mn = ADAM_B1 * m_ref[...] + (1.0 - ADAM_B1) * gsum
        vn = ADAM_B2 * v_ref[...] + (1.0 - ADAM_B2) * (gsum * gsum)
        g_ref[...] = gsum
        mo_ref[...] = mn
        vo_ref[...] = vn
        d_ref[...] = -ADAM_LR * ((mn * c1) / (jnp.sqrt(vn * c2) + ADAM_EPS) + ADAM_WD * w_ref[...])

    mat = pl.BlockSpec((tr, c), lambda i: (i, 0))
    sds = jax.ShapeDtypeStruct((r, c), F32)
    return pl.pallas_call(body, name=name, grid=(r // tr,),
                          in_specs=[pl.BlockSpec((n, tr, c), lambda i: (0, i, 0)), mat, mat, mat],
                          out_specs=[mat] * 4, out_shape=[sds] * 4,
                          compiler_params=_cp(("parallel",)))(gparts, w, m, v)


def _sum_parts(name, parts):
    n, r, c = parts.shape

    def body(p_ref, o_ref):
        acc = p_ref[0]
        for j in range(1, n):
            acc = acc + p_ref[j]
        o_ref[...] = acc

    return pl.pallas_call(body, name=name, out_shape=jax.ShapeDtypeStruct((r, c), F32),
                          compiler_params=_cp(None))(parts)


def _exchange(name, arrs, gather):
    n = len(arrs)
    out_shape = [pltpu.HBM(((N_DEV,) + a.shape) if gather else a.shape, a.dtype) for a in arrs]

    def body(*refs):
        ins, outs = refs[:n], refs[n:2 * n]
        send_sems, recv_sems, local_sems = refs[2 * n:]
        x, y, c = lax.axis_index("x"), lax.axis_index("y"), lax.axis_index("c")
        me = 4 * x + 2 * y + c

        def peer_of(dd):
            px = jnp.bitwise_xor(x, dd >> 2)
            py = jnp.bitwise_xor(y, (dd >> 1) & 1)
            pc = jnp.bitwise_xor(c, dd & 1)
            return (px, py, pc), 4 * px + 2 * py + pc

        local, sends, recvs = [], [], []
        for i in range(n):
            src_me = ins[i] if gather else ins[i].at[me]
            local.append(pltpu.make_async_copy(src_me, outs[i].at[me], local_sems.at[i]))
        for dd in range(1, N_DEV):
            peer, pid = peer_of(dd)
            for i in range(n):
                src = ins[i] if gather else ins[i].at[pid]
                sends.append(pltpu.make_async_remote_copy(
                    src_ref=src, dst_ref=outs[i].at[me], send_sem=send_sems.at[i, dd - 1],
                    recv_sem=recv_sems.at[i, dd - 1], device_id=peer, device_id_type=MESH))
                recvs.append(pltpu.make_async_remote_copy(
                    src_ref=src, dst_ref=outs[i].at[pid], send_sem=send_sems.at[i, dd - 1],
                    recv_sem=recv_sems.at[i, dd - 1], device_id=peer, device_id_type=MESH))
        for cp in local + sends:
            cp.start()
        for cp in recvs:
            cp.wait_recv()
        for cp in sends:
            cp.wait_send()
        for cp in local:
            cp.wait()

    hbm = pl.BlockSpec(memory_space=pltpu.HBM)
    ops = [pltpu.with_memory_space_constraint(a, pltpu.HBM) for a in arrs]
    outs = pl.pallas_call(
        body, name=name, in_specs=[hbm] * n, out_specs=[hbm] * n, out_shape=out_shape,
        scratch_shapes=[pltpu.SemaphoreType.DMA((n, N_DEV - 1)), pltpu.SemaphoreType.DMA((n, N_DEV - 1)),
                        pltpu.SemaphoreType.DMA((n,))])(*ops)
    return list(outs)


def _pack(pieces):
    flat = jnp.concatenate([p.reshape(-1) for p in pieces])
    pad = (-flat.shape[0]) % (8 * LANE)
    return jnp.pad(flat, (0, pad)).reshape(-1, LANE)


def _unpack(packed, shapes):
    flat = packed.reshape(-1)
    out, off = [], 0
    for shp in shapes:
        sz = math.prod(shp)
        out.append(flat[off:off + sz].reshape(shp))
        off += sz
    return out


def kernel(x, c, norm_mix, norm_ffn, w_mod, b_mod, w_qkv, w_o_attn, w_in_ssm, a_re, a_im, log_dt, b_re, b_im, c_re, c_im, d_skip, w_glu, b_glu, w_o_ssm, w_up, conv_w, conv_b, w_down, norm_out, w_fin, b_fin, loss_target, m_norm_mix, m_norm_ffn, m_w_mod, m_b_mod, m_w_qkv, m_w_o_attn, m_w_in_ssm, m_a_re, m_a_im, m_log_dt, m_b_re, m_b_im, m_c_re, m_c_im, m_d_skip, m_w_glu, m_b_glu, m_w_o_ssm, m_w_up, m_conv_w, m_conv_b, m_w_down, m_norm_out, m_w_fin, m_b_fin, v_norm_mix, v_norm_ffn, v_w_mod, v_b_mod, v_w_qkv, v_w_o_attn, v_w_in_ssm, v_a_re, v_a_im, v_log_dt, v_b_re, v_b_im, v_c_re, v_c_im, v_d_skip, v_w_glu, v_b_glu, v_w_o_ssm, v_w_up, v_conv_w, v_conv_b, v_w_down, v_norm_out, v_w_fin, v_b_fin):
    nb, s, d = x.shape
    t = nb * s
    n_seq = nb * N_DEV
    me = 4 * lax.axis_index("x") + 2 * lax.axis_index("y") + lax.axis_index("c")
    cm = w_mod.shape[2]
    cf = w_fin.shape[1]
    c_up = w_up.shape[2]
    r_dn = w_down.shape[1]
    g_ssm = d // SSM_H

    (wq8, wo8, win8, wglu8, wos8, wup8_0, wup8_1, wd8_0, wd8_1, cw8, dskip8, bglu8, c8) = _exchange(
        "gather_weights",
        [w_qkv[0].astype(BF16), w_o_attn[0].astype(BF16), w_in_ssm[0].astype(BF16), w_glu[0].astype(BF16),
         w_o_ssm[0].astype(BF16), w_up[0].astype(BF16), w_up[1].astype(BF16), w_down[0].astype(BF16),
         w_down[1].astype(BF16), conv_w, d_skip, b_glu, c], True)
    wo = wo8.reshape(d, d)
    win = win8.reshape(d, d)
    wglu = wglu8.reshape(d, d)
    wos = wos8.reshape(d, d)
    wup8 = [wup8_0, wup8_1]
    half = N_DEV // 2
    wd4 = [wd8_0.reshape(half, 2 * r_dn, d), wd8_1.reshape(half, 2 * r_dn, d)]
    cw_l = [cw8[:, 0], cw8[:, 1]]
    cb_l = [conv_b[i].reshape(N_DEV, 1, c_up) for i in range(2)]
    dskip_f = dskip8.reshape(1, d)
    bglu_f = bglu8.reshape(1, d)
    c_all = c8.reshape(n_seq, d)

    c_act, modloc = _modfin_fwd(c_all, w_mod, w_fin)
    (mod8,) = _exchange("gather_mod", [modloc], True)
    mine = lax.dynamic_slice_in_dim(mod8, me * nb, nb, axis=1)
    mods = []
    for i in range(2):
        mi = mine[:, :, i * cm:(i + 1) * cm].transpose(1, 0, 2).reshape(nb, N_DEV * cm) + b_mod[i]
        mods.append([mi[:, j * d:(j + 1) * d].reshape(nb, 1, d) for j in range(6)])
    fin = mine[:, :, 2 * cm:].transpose(1, 0, 2).reshape(nb, N_DEV * cf) + b_fin
    sh_f, sc_f = fin[:, :d].reshape(nb, 1, d), fin[:, d:].reshape(nb, 1, d)

    row = lambda a: a.reshape(1, -1)
    x0 = x.reshape(t, d)

    def ffn_fwd(i, xin, sh, sc, gate):
        h = _norm_mod_fwd(f"ffn{i}_norm", xin, row(norm_ffn[i]), sh, sc, nb)
        up = _mm(f"ffn{i}_up", h, wup8[i], (t // tm_, N_DEV, 1),
                 pl.BlockSpec((tm_, d), lambda a, b, k: (a, 0)), pl.BlockSpec((None, d, c_up), lambda a, b, k: (b, 0, 0)),
                 pl.BlockSpec((None, tm_, c_up), lambda a, b, k: (b, a, 0)),
                 jax.ShapeDtypeStruct((N_DEV, t, c_up), BF16), _NN, (tm_, c_up))
        act = _ffn_act_fwd(f"ffn{i}_act", up, cw_l[i], cb_l[i], nb, s)
        yf = _mm(f"ffn{i}_down", act, wd4[i], (t // tm_, 1, half),
                 pl.BlockSpec((None, tm_, c_up), lambda a, b, k: (k, a, 0)),
                 pl.BlockSpec((None, c_up, d), lambda a, b, k: (k, 0, 0)),
                 pl.BlockSpec((tm_, d), lambda a, b, k: (a, 0)), jax.ShapeDtypeStruct((t, d), F32), _NN, (tm_, d))
        xout = _gate_add(f"ffn{i}_res", xin, yf, gate, nb)
        return xout, (h, up, act, yf)

    tm_ = _tile(t, 512)
    sh1, sc1, g1, sh2, sc2, g2 = mods[0]
    h1 = _norm_mod_fwd("attn_norm", x0, row(norm_mix[0]), sh1, sc1, nb)
    cq = wq8.shape[2]
    qkv = _mm("attn_qkv", h1, wq8, (t // tm_, N_DEV, 1),
              pl.BlockSpec((tm_, d), lambda a, b, k: (a, 0)), pl.BlockSpec((None, d, cq), lambda a, b, k: (b, 0, 0)),
              pl.BlockSpec((tm_, cq), lambda a, b, k: (a, b)), jax.ShapeDtypeStruct((t, 3 * d), BF16), _NN, (tm_, cq))
    o_att, car_att = _attn_fwd(qkv, nb, s, d)
    ya = _mm_nn("attn_out", o_att, wo, F32)
    x1 = _gate_add("attn_res", x0, ya, g1, nb)
    x2, ffn0_saved = ffn_fwd(0, x1, sh2, sc2, g2)

    sh1b, sc1b, g1b, sh2b, sc2b, g2b = mods[1]
    ops = _ssm_operators(a_re[0], a_im[0], log_dt[0], b_re[0], b_im[0], c_re[0], c_im[0])
    h3 = _norm_mod_fwd("ssm_norm", x2, row(norm_mix[1]), sh1b, sc1b, nb)
    u = _mm_nn("ssm_in", h3, win, BF16)
    ug = _to_groups(u, nb, s)
    yg, xp_re, xp_im = _ssm_core_fwd(ug, ops, nb)
    y_ssm, z_ssm = _ssm_post_fwd(_from_groups(yg, nb, s), u, dskip_f)
    gl = _mm_nn("ssm_glu", z_ssm, wglu, F32)
    gg = _glu_fwd(z_ssm, gl, bglu_f)
    ys2 = _mm_nn("ssm_out", gg, wos, F32)
    x3 = _gate_add("ssm_res", x2, ys2, g1b, nb)
    x4, ffn1_saved = ffn_fwd(1, x3, sh2b, sc2b, g2b)

    dx4, g_norm_out, dsh_f, dsc_f, loss_blk = _norm_mod_bwd(
        "final_norm", None, x4, row(norm_out), sh_f, sc_f, None, loss_target.reshape(t, d), nb)
    loss = lax.psum(loss_blk[0, 0], ("x", "y", "c"))

    def ffn_bwd(i, dxo, xin, sc, gate, saved):
        h, up, act, yf = saved
        dyf, dgate = _gate_bwd(f"ffn{i}_res_bwd", dxo, yf, gate, nb)
        dact = _mm(f"ffn{i}_down_dx", dyf, wd4[i], (t // tm_, half, 1),
                   pl.BlockSpec((tm_, d), lambda a, b, k: (a, 0)), pl.BlockSpec((None, c_up, d), lambda a, b, k: (b, 0, 0)),
                   pl.BlockSpec((None, tm_, c_up), lambda a, b, k: (b, a, 0)),
                   jax.ShapeDtypeStruct((half, t, c_up), BF16), _NT, (tm_, c_up))
        tk = _tile(t, 1024)
        gwd = _mm(f"ffn{i}_down_dw", act, dyf, (half, 1, t // tk),
                  pl.BlockSpec((None, tk, c_up), lambda a, b, k: (a, k, 0)), pl.BlockSpec((tk, d), lambda a, b, k: (k, 0)),
                  pl.BlockSpec((None, c_up, d), lambda a, b, k: (a, 0, 0)),
                  jax.ShapeDtypeStruct((half, c_up, d), BF16), _TN, (c_up, d))
        dup, dcw, dcb = _ffn_act_bwd(f"ffn{i}_act_bwd", up, dact, cw_l[i], cb_l[i], nb, s)
        dh = _mm(f"ffn{i}_up_dx", dup, wup8[i], (t // tm_, 1, N_DEV),
                 pl.BlockSpec((None, tm_, c_up), lambda a, b, k: (k, a, 0)),
                 pl.BlockSpec((None, d, c_up), lambda a, b, k: (k, 0, 0)),
                 pl.BlockSpec((tm_, d), lambda a, b, k: (a, 0)), jax.ShapeDtypeStruct((t, d), F32), _NT, (tm_, d))
        gwup = _mm(f"ffn{i}_up_dw", h, dup, (1, N_DEV, t // tk),
                   pl.BlockSpec((tk, d), lambda a, b, k: (k, 0)), pl.BlockSpec((None, tk, c_up), lambda a, b, k: (b, k, 0)),
                   pl.BlockSpec((None, d, c_up), lambda a, b, k: (b, 0, 0)),
                   jax.ShapeDtypeStruct((N_DEV, d, c_up), BF16), _TN, (d, c_up))
        dxi, g_norm, dsh, dsc = _norm_mod_bwd(f"ffn{i}_norm_bwd", dh, xin, row(norm_ffn[i]), None, sc, dxo, None, nb)
        return dxi, (gwup, gwd.reshape(N_DEV, r_dn, d), dcw, dcb, g_norm, dsh, dsc, dgate)

    dx3, (gwup1, gwd1, dcw1, dcb1, g_nffn1, dsh2b, dsc2b, dg2b) = ffn_bwd(1, dx4, x3, sc2b, g2b, ffn1_saved)

    dys2, dg1b = _gate_bwd("ssm_res_bwd", dx3, ys2, g1b, nb)
    dgg = _mm_nt("ssm_out_dx", dys2, wos, F32)
    gwos = _mm_tn("ssm_out_dw", gg, dys2, BF16)
    dgl, dz1, g_bglu = _glu_bwd(dgg, z_ssm, gl, bglu_f)
    dz2 = _mm_nt("ssm_glu_dx", dgl, wglu, F32)
    gwglu = _mm_tn("ssm_glu_dw", z_ssm, dgl, BF16)
    dy_ssm, du_skip, g_dskip = _ssm_post_bwd(dz1, dz2, y_ssm, u, dskip_f)
    dug, d_ops = _ssm_core_bwd(_to_groups(dy_ssm, nb, s), ug, xp_re, xp_im, ops, nb)
    du = _add_cast(_from_groups(dug, nb, s), du_skip)
    dh3 = _mm_nt("ssm_in_dx", du, win, F32)
    gwin = _mm_tn("ssm_in_dw", h3, du, BF16)
    dx2, g_nmix1, dsh1b, dsc1b = _norm_mod_bwd("ssm_norm_bwd", dh3, x2, row(norm_mix[1]), None, sc1b, dx3, None, nb)
    _, ops_vjp = jax.vjp(_ssm_operators, a_re[0], a_im[0], log_dt[0], b_re[0], b_im[0], c_re[0], c_im[0])
    g_ssm_params = ops_vjp(d_ops)

    dx1, (gwup0, gwd0, dcw0, dcb0, g_nffn0, dsh2, dsc2, dg2) = ffn_bwd(0, dx2, x1, sc2, g2, ffn0_saved)

    dya, dg1 = _gate_bwd("attn_res_bwd", dx1, ya, g1, nb)
    do_att = _mm_nt("attn_out_dx", dya, wo, BF16)
    gwo = _mm_tn("attn_out_dw", o_att, dya, BF16)
    dq, dk, dv = _attn_bwd(qkv, car_att, do_att, nb, s, d)
    dqkv3 = jnp.stack([dq, dk, dv])
    npb = d // LANE
    nqb = cq // LANE
    nblk = 3 * npb
    dh1 = _mm("attn_qkv_dx", dqkv3, wq8, (t // tm_, 1, nblk),
              pl.BlockSpec((None, tm_, LANE), lambda a, b, k: (k // npb, a, k % npb)),
              pl.BlockSpec((None, d, LANE), lambda a, b, k: (k // nqb, 0, k % nqb)),
              pl.BlockSpec((tm_, d), lambda a, b, k: (a, 0)), jax.ShapeDtypeStruct((t, d), F32), _NT, (tm_, d))
    tk = _tile(t, 1024)
    gwq8 = _mm("attn_qkv_dw", h1, dqkv3, (1, nblk, t // tk),
               pl.BlockSpec((tk, d), lambda a, b, k: (k, 0)),
               pl.BlockSpec((None, tk, LANE), lambda a, b, k: (b // npb, k, b % npb)),
               pl.BlockSpec((None, d, LANE), lambda a, b, k: (b // nqb, 0, b % nqb)),
               jax.ShapeDtypeStruct((N_DEV, d, cq), BF16), _TN, (d, LANE))
    dx0, g_nmix0, dsh1, dsc1 = _norm_mod_bwd("attn_norm_bwd", dh1, x0, row(norm_mix[0]), None, sc1, dx1, None, nb)
    grad_x = dx0.reshape(nb, s, d)

    (rq, ro, rin, rglu, ros, rup0, rup1, rd0, rd1) = _exchange(
        "scatter_grads",
        [gwq8, gwo.reshape(N_DEV, d // N_DEV, d), gwin.reshape(N_DEV, d // N_DEV, d),
         gwglu.reshape(N_DEV, d // N_DEV, d), gwos.reshape(N_DEV, d // N_DEV, d), gwup0, gwup1, gwd0, gwd1], False)

    dmod = [jnp.concatenate([a.reshape(nb, d) for a in grp], axis=1) for grp in
            ([dsh1, dsc1, dg1, dsh2, dsc2, dg2], [dsh1b, dsc1b, dg1b, dsh2b, dsc2b, dg2b])]
    dfin = jnp.concatenate([dsh_f.reshape(nb, d), dsc_f.reshape(nb, d)], axis=1)
    dmodfin = jnp.concatenate(dmod + [dfin], axis=1)
    small_shapes = [(2, d), (2, d), (d,), (2, N_DEV * c_up), (g_ssm, SSM_P), (g_ssm, SSM_P), (g_ssm,),
                    (g_ssm, SSM_P, SSM_H), (g_ssm, SSM_P, SSM_H), (g_ssm, SSM_H, SSM_P), (g_ssm, SSM_H, SSM_P),
                    (d,), (d,), (2, N_DEV, 3, c_up)]
    small_partial = _pack([jnp.stack([g_nmix0, g_nmix1]), jnp.stack([g_nffn0, g_nffn1]), g_norm_out,
                           jnp.stack([dcb0, dcb1]), *g_ssm_params, g_dskip, g_bglu, jnp.stack([dcw0, dcw1])])
    dmf8, small8 = _exchange("gather_small_grads", [dmodfin, small_partial], True)
    small_sum = _sum_parts("sum_small_grads", small8)
    (g_norm_mix, g_norm_ffn, g_norm_out_s, g_conv_b, g_a_re, g_a_im, g_log_dt, g_b_re, g_b_im, g_c_re, g_c_im,
     g_dskip_full, g_bglu_full, g_cw_full) = _unpack(small_sum, small_shapes)

    dall = dmf8.reshape(n_seq, 14 * d)
    dmod_loc = jnp.stack([lax.dynamic_slice_in_dim(dall[:, i * 6 * d:(i + 1) * 6 * d], me * cm, cm, axis=1)
                          for i in range(2)])
    dfin_loc = lax.dynamic_slice_in_dim(dall[:, 12 * d:], me * cf, cf, axis=1)
    g_w_mod, g_w_fin, g_bias = _modfin_bwd(c_act.T, dmod_loc, dfin_loc, dall)
    g_b_mod = g_bias[0, :12 * d].reshape(2, 6 * d)
    g_b_fin = g_bias[0, 12 * d:]

    def big(name, parts, w, m, v):
        shp = w.shape
        r2 = lambda a: a.reshape(-1, shp[-1])
        res = _adamw(name, parts.reshape(parts.shape[0], -1, shp[-1]), r2(w), r2(m), r2(v))
        return [a.reshape(shp) for a in res]

    upd = {}
    upd["w_mod"] = big("adamw_w_mod", g_w_mod[None], w_mod, m_w_mod, v_w_mod)
    upd["w_fin"] = big("adamw_w_fin", g_w_fin[None], w_fin, m_w_fin, v_w_fin)
    upd["w_qkv"] = big("adamw_w_qkv", rq, w_qkv, m_w_qkv, v_w_qkv)
    upd["w_o_attn"] = big("adamw_w_o_attn", ro, w_o_attn, m_w_o_attn, v_w_o_attn)
    upd["w_in_ssm"] = big("adamw_w_in_ssm", rin, w_in_ssm, m_w_in_ssm, v_w_in_ssm)
    upd["w_glu"] = big("adamw_w_glu", rglu, w_glu, m_w_glu, v_w_glu)
    upd["w_o_ssm"] = big("adamw_w_o_ssm", ros, w_o_ssm, m_w_o_ssm, v_w_o_ssm)
    up_l = [big(f"adamw_w_up{i}", r, w_up[i], m_w_up[i], v_w_up[i]) for i, r in enumerate((rup0, rup1))]
    upd["w_up"] = [jnp.stack([up_l[0][j], up_l[1][j]]) for j in range(4)]
    dn_l = [big(f"adamw_w_down{i}", r, w_down[i], m_w_down[i], v_w_down[i]) for i, r in enumerate((rd0, rd1))]
    upd["w_down"] = [jnp.stack([dn_l[0][j], dn_l[1][j]]) for j in range(4)]

    g_dskip_loc = lax.dynamic_slice_in_dim(g_dskip_full.reshape(1, d), me * (d // N_DEV), d // N_DEV, axis=1)
    g_bglu_loc = lax.dynamic_slice_in_dim(g_bglu_full.reshape(1, d), me * (d // N_DEV), d // N_DEV, axis=1)
    g_cw_loc = lax.dynamic_slice_in_dim(g_cw_full, me, 1, axis=1).reshape(2, 3, c_up)
    small_names = ["norm_mix", "norm_ffn", "b_mod", "a_re", "a_im", "log_dt", "b_re", "b_im", "c_re", "c_im",
                   "d_skip", "b_glu", "conv_w", "conv_b", "norm_out", "b_fin"]
    small_g = [g_norm_mix, g_norm_ffn, g_b_mod, g_a_re[None], g_a_im[None], g_log_dt[None], g_b_re[None], g_b_im[None],
               g_c_re[None], g_c_im[None], g_dskip_loc, g_bglu_loc, g_cw_loc, g_conv_b, g_norm_out_s, g_b_fin]
    small_w = [norm_mix, norm_ffn, b_mod, a_re, a_im, log_dt, b_re, b_im, c_re, c_im, d_skip, b_glu, conv_w, conv_b,
               norm_out, b_fin]
    small_m = [m_norm_mix, m_norm_ffn, m_b_mod, m_a_re, m_a_im, m_log_dt, m_b_re, m_b_im, m_c_re, m_c_im, m_d_skip,
               m_b_glu, m_conv_w, m_conv_b, m_norm_out, m_b_fin]
    small_v = [v_norm_mix, v_norm_ffn, v_b_mod, v_a_re, v_a_im, v_log_dt, v_b_re, v_b_im, v_c_re, v_c_im, v_d_skip,
               v_b_glu, v_conv_w, v_conv_b, v_norm_out, v_b_fin]
    shapes = [w.shape for w in small_w]
    res = _adamw("adamw_small", _pack(small_g)[None], _pack(small_w), _pack(small_m), _pack(small_v))
    res = [_unpack(r, shapes) for r in res]
    for j, nm in enumerate(small_names):
        upd[nm] = [res[k][j] for k in range(4)]

    order = ["norm_mix", "norm_ffn", "w_mod", "b_mod", "w_qkv", "w_o_attn", "w_in_ssm", "a_re", "a_im", "log_dt",
             "b_re", "b_im", "c_re", "c_im", "d_skip", "w_glu", "b_glu", "w_o_ssm", "w_up", "conv_w", "conv_b",
             "w_down", "norm_out", "w_fin", "b_fin"]
    outs = [loss, grad_x]
    for k in range(4):
        outs += [upd[nm][k] for nm in order]
    return tuple(outs)
```

```python
import functools
import math

import jax
import jax.numpy as jnp
from jax import lax
from jax.experimental import pallas as pl
from jax.experimental.pallas import tpu as pltpu

F32 = jnp.float32
BF16 = jnp.bfloat16
MESH = pl.DeviceIdType.MESH

N_DEV = 8
HEAD_DIM = 64
ATT_BLK = 128
ATT_BQ = 256
SSM_H = 16
SSM_P = 64
SSM_L = 16
EPS = 1e-6
ADAM_LR, ADAM_B1, ADAM_B2, ADAM_EPS, ADAM_WD, ADAM_STEP = 0.001, 0.9, 0.999, 1e-08, 0.01, 10
V7X_VMEM_LIMIT = 56 * 1024 * 1024
LANE = 128

_NN = (((1,), (0,)), ((), ()))
_NT = (((1,), (1,)), ((), ()))
_TN = (((0,), (0,)), ((), ()))


def _cp(sem):
    return pltpu.CompilerParams(dimension_semantics=sem, vmem_limit_bytes=V7X_VMEM_LIMIT)


def _tile(n, pref):
    if n <= pref:
        return n
    t = pref - pref % 16
    while t >= 16:
        if n % t == 0:
            return t
        t -= 16
    return n


def _mm(name, a, b, grid, a_spec, b_spec, out_spec, out_shape, dims, acc_shape):
    nk = grid[-1]
    kax = len(grid) - 1

    def body(a_ref, b_ref, o_ref, acc_ref):
        k = pl.program_id(kax)

        @pl.when(k == 0)
        def _():
            acc_ref[...] = jnp.zeros(acc_shape, F32)

        acc_ref[...] += lax.dot_general(a_ref[...].astype(BF16), b_ref[...].astype(BF16), dims,
                                        preferred_element_type=F32)

        @pl.when(k == nk - 1)
        def _():
            o_ref[...] = acc_ref[...].astype(o_ref.dtype)

    return pl.pallas_call(
        body, name=name, grid=grid, in_specs=[a_spec, b_spec], out_specs=out_spec, out_shape=out_shape,
        scratch_shapes=[pltpu.VMEM(acc_shape, F32)],
        compiler_params=_cp(("parallel",) * kax + ("arbitrary",)))(a, b)


def _mm_terms(name, terms, grid, out_spec, out_shape):
    n = len(terms)
    dims = [t[4] for t in terms]

    def body(*refs):
        o_ref = refs[2 * n]
        acc = None
        for i in range(n):
            d = lax.dot_general(refs[2 * i][...].astype(BF16), refs[2 * i + 1][...].astype(BF16), dims[i],
                                preferred_element_type=F32)
            acc = d if acc is None else acc + d
        o_ref[...] = acc.astype(o_ref.dtype)

    ops, specs = [], []
    for a, a_spec, b, b_spec, _ in terms:
        ops += [a, b]
        specs += [a_spec, b_spec]
    return pl.pallas_call(body, name=name, grid=grid, in_specs=specs, out_specs=out_spec, out_shape=out_shape,
                          compiler_params=_cp(("parallel",) * len(grid)))(*ops)


def _mm_nn(name, a, w, out_dtype):
    m, k = a.shape
    n = w.shape[1]
    tm, tn, tk = _tile(m, 512), _tile(n, 1024), _tile(k, 1024)
    return _mm(name, a, w, (m // tm, n // tn, k // tk),
               pl.BlockSpec((tm, tk), lambda i, j, kk: (i, kk)), pl.BlockSpec((tk, tn), lambda i, j, kk: (kk, j)),
               pl.BlockSpec((tm, tn), lambda i, j, kk: (i, j)), jax.ShapeDtypeStruct((m, n), out_dtype), _NN, (tm, tn))


def _mm_nt(name, a, w, out_dtype):
    m, n = a.shape
    k = w.shape[0]
    tm, tko, tn = _tile(m, 512), _tile(k, 1024), _tile(n, 1024)
    return _mm(name, a, w, (m // tm, k // tko, n // tn),
               pl.BlockSpec((tm, tn), lambda i, j, kk: (i, kk)), pl.BlockSpec((tko, tn), lambda i, j, kk: (j, kk)),
               pl.BlockSpec((tm, tko), lambda i, j, kk: (i, j)), jax.ShapeDtypeStruct((m, k), out_dtype), _NT, (tm, tko))


def _mm_tn(name, a, b, out_dtype):
    t, m = a.shape
    n = b.shape[1]
    tm, tn, tk = _tile(m, 512), _tile(n, 1024), _tile(t, 1024)
    return _mm(name, a, b, (m // tm, n // tn, t // tk),
               pl.BlockSpec((tk, tm), lambda i, j, kk: (kk, i)), pl.BlockSpec((tk, tn), lambda i, j, kk: (kk, j)),
               pl.BlockSpec((tm, tn), lambda i, j, kk: (i, j)), jax.ShapeDtypeStruct((m, n), out_dtype), _TN, (tm, tn))


def _norm_mod_fwd(name, x, g, shift, scale, nb):
    t, d = x.shape
    s = t // nb
    tr = _tile(s, 512)
    nt = s // tr

    def body(x_ref, g_ref, sh_ref, sc_ref, h_ref):
        xv = x_ref[...]
        r = lax.rsqrt(jnp.mean(xv * xv, axis=-1, keepdims=True) + EPS)
        y = xv * r * g_ref[...]
        h_ref[...] = (y * (1.0 + sc_ref[...]) + sh_ref[...]).astype(h_ref.dtype)

    row = pl.BlockSpec((tr, d), lambda b, i: (b * nt + i, 0))
    vec = pl.BlockSpec((None, 1, d), lambda b, i: (b, 0, 0))
    return pl.pallas_call(body, name=name, grid=(nb, nt),
                          in_specs=[row, pl.BlockSpec((1, d), lambda b, i: (0, 0)), vec, vec],
                          out_specs=row, out_shape=jax.ShapeDtypeStruct((t, d), BF16),
                          compiler_params=_cp(("parallel", "parallel")))(x, g, shift, scale)


def _norm_mod_bwd(name, dh, x, g, shift, scale, dres, target, nb):
    t, d = x.shape
    s = t // nb
    tr = _tile(s, 256)
    nt = s // tr
    final = target is not None

    def body(*refs):
        if final:
            x_ref, g_ref, sh_ref, sc_ref, tg_ref, dx_ref, dg_ref, dsh_ref, dsc_ref, loss_ref = refs
        else:
            dh_ref, x_ref, g_ref, sc_ref, dres_ref, dx_ref, dg_ref, dsh_ref, dsc_ref = refs
        b, i = pl.program_id(0), pl.program_id(1)
        xv = x_ref[...]
        gv = g_ref[...]
        r = lax.rsqrt(jnp.mean(xv * xv, axis=-1, keepdims=True) + EPS)
        nrm = xv * r
        y = nrm * gv
        one_sc = 1.0 + sc_ref[...]
        if final:
            err = y * one_sc + sh_ref[...] - tg_ref[...]
            dhv = err * (1.0 / d)
        else:
            dhv = dh_ref[...].astype(F32)
        dy = dhv * one_sc
        dn = dy * gv
        dxv = r * (dn - nrm * jnp.mean(dn * nrm, axis=-1, keepdims=True))
        if final:
            dx_ref[...] = dxv
        else:
            dx_ref[...] = dres_ref[...] + dxv

        @pl.when(i == 0)
        def _():
            dsh_ref[...] = jnp.zeros_like(dsh_ref)
            dsc_ref[...] = jnp.zeros_like(dsc_ref)

        @pl.when((i == 0) & (b == 0))
        def _():
            dg_ref[...] = jnp.zeros_like(dg_ref)
            if final:
                loss_ref[...] = jnp.zeros_like(loss_ref)

        dsh_ref[...] += jnp.sum(dhv, axis=0, keepdims=True)
        dsc_ref[...] += jnp.sum(dhv * y, axis=0, keepdims=True)
        dg_ref[...] += jnp.sum(dy * nrm, axis=0, keepdims=True)
        if final:
            loss_ref[...] += (0.5 / d) * jnp.sum(err * err)

    row = pl.BlockSpec((tr, d), lambda b, i: (b * nt + i, 0))
    vec = pl.BlockSpec((None, 1, d), lambda b, i: (b, 0, 0))
    gsp = pl.BlockSpec((1, d), lambda b, i: (0, 0))
    out_specs = [row, gsp, vec, vec]
    out_shape = [jax.ShapeDtypeStruct((t, d), F32), jax.ShapeDtypeStruct((1, d), F32),
                 jax.ShapeDtypeStruct((nb, 1, d), F32), jax.ShapeDtypeStruct((nb, 1, d), F32)]
    if final:
        ins, in_specs = [x, g, shift, scale, target], [row, gsp, vec, vec, row]
        out_specs.append(pl.BlockSpec((8, LANE), lambda b, i: (0, 0)))
        out_shape.append(jax.ShapeDtypeStruct((8, LANE), F32))
    else:
        ins, in_specs = [dh, x, g, scale, dres], [row, row, gsp, vec, row]
    return pl.pallas_call(body, name=name, grid=(nb, nt), in_specs=in_specs, out_specs=out_specs,
                          out_shape=out_shape, compiler_params=_cp(("arbitrary", "arbitrary")))(*ins)


def _gate_add(name, x, y, gate, nb):
    t, d = x.shape
    s = t // nb
    tr = _tile(s, 512)
    nt = s // tr

    def body(x_ref, y_ref, g_ref, o_ref):
        o_ref[...] = x_ref[...] + g_ref[...] * y_ref[...]

    row = pl.BlockSpec((tr, d), lambda b, i: (b * nt + i, 0))
    vec = pl.BlockSpec((None, 1, d), lambda b, i: (b, 0, 0))
    return pl.pallas_call(body, name=name, grid=(nb, nt), in_specs=[row, row, vec], out_specs=row,
                          out_shape=jax.ShapeDtypeStruct((t, d), F32),
                          compiler_params=_cp(("parallel", "parallel")))(x, y, gate)


def _gate_bwd(name, dx, y, gate, nb):
    t, d = dx.shape
    s = t // nb
    tr = _tile(s, 512)
    nt = s // tr

    def body(dx_ref, y_ref, g_ref, dy_ref, dg_ref):
        dxv = dx_ref[...]
        dy_ref[...] = (g_ref[...] * dxv).astype(dy_ref.dtype)

        @pl.when(pl.program_id(1) == 0)
        def _():
            dg_ref[...] = jnp.zeros_like(dg_ref)

        dg_ref[...] += jnp.sum(dxv * y_ref[...], axis=0, keepdims=True)

    row = pl.BlockSpec((tr, d), lambda b, i: (b * nt + i, 0))
    vec = pl.BlockSpec((None, 1, d), lambda b, i: (b, 0, 0))
    return pl.pallas_call(body, name=name, grid=(nb, nt), in_specs=[row, row, vec], out_specs=[row, vec],
                          out_shape=[jax.ShapeDtypeStruct((t, d), BF16), jax.ShapeDtypeStruct((nb, 1, d), F32)],
                          compiler_params=_cp(("parallel", "arbitrary")))(dx, y, gate)


def _log_sigmoid(z):
    return jnp.minimum(z, 0.0) - jnp.log(1.0 + jnp.exp(-jnp.abs(z)))


def _split_dot(v, tri):
    hi = v.astype(BF16)
    lo = (v - hi.astype(F32)).astype(BF16)
    return (jnp.dot(hi, tri, preferred_element_type=F32) + jnp.dot(lo, tri, preferred_element_type=F32))


def _attn_fwd(qkv, nb, s, d):
    t = nb * s
    npair = d // LANE
    bk = ATT_BLK
    bq = min(ATT_BQ, s)
    nq = s // bq
    kpq = bq // bk
    nheads = LANE // HEAD_DIM
    scale = HEAD_DIM ** -0.5
    assert s // bk <= HEAD_DIM, "one carry lane per key block and head"
    assert bk == LANE, "the running sums are kept one 128-lane tile wide"

    def body(q_ref, k_ref, v_ref, o_ref, car_ref, acc_s, run_s):
        qi = pl.program_id(2)
        q = q_ref[...]
        lane = lax.broadcasted_iota(jnp.int32, (1, LANE), 1)
        row = lax.broadcasted_iota(jnp.int32, (bq, bk), 0)
        col = lax.broadcasted_iota(jnp.int32, (bq, bk), 1)
        trow = lax.broadcasted_iota(jnp.int32, (bk, bk), 0)
        tcol = lax.broadcasted_iota(jnp.int32, (bk, bk), 1)
        tri = (trow > tcol).astype(BF16)
        hms = [(lane // HEAD_DIM) == hh for hh in range(nheads)]
        qhs = [jnp.where(hm, q, jnp.zeros_like(q)) for hm in hms]
        car_ref[...] = jnp.zeros((bq, LANE), F32)
        acc_s[...] = jnp.zeros_like(acc_s)
        run_s[...] = jnp.zeros_like(run_s)
        nkb = (qi + 1) * kpq

        def step(jj, carry):
            j = nkb - 1 - jj
            k0 = pl.multiple_of(j * bk, bk)
            kj = k_ref[pl.ds(k0, bk), :]
            vj = v_ref[pl.ds(k0, bk), :]
            mask = (k0 + col) < (qi * bq + row)
            for hh in range(nheads):
                z = lax.dot_general(qhs[hh], kj, _NT, preferred_element_type=F32) * scale
                lb = _log_sigmoid(z)
                l1 = jnp.where(mask, lb - z, 0.0)
                run = run_s[hh]
                suf = _split_dot(l1, tri) + run
                w = jnp.where(mask, jnp.exp(lb + suf), 0.0)
                acc_s[hh] += jnp.dot(w.astype(BF16), vj, preferred_element_type=F32)
                car_ref[...] = jnp.where(lane == hh * HEAD_DIM + j, run, car_ref[...])
                run_s[hh] = run + jnp.sum(l1, axis=1, keepdims=True)
            return carry

        lax.fori_loop(0, nkb, step, 0)
        out = acc_s[0]
        for hh in range(1, nheads):
            out = jnp.where(hms[hh], acc_s[hh], out)
        o_ref[...] = out.astype(o_ref.dtype)

    qspec = pl.BlockSpec((bq, LANE), lambda b, p, i: (b * nq + i, p))
    return pl.pallas_call(
        body, name="attn_fwd", grid=(nb, npair, nq),
        in_specs=[qspec,
                  pl.BlockSpec((s, LANE), lambda b, p, i: (b, npair + p)),
                  pl.BlockSpec((s, LANE), lambda b, p, i: (b, 2 * npair + p))],
        out_specs=[qspec, qspec],
        out_shape=[jax.ShapeDtypeStruct((t, d), BF16), jax.ShapeDtypeStruct((t, d), F32)],
        scratch_shapes=[pltpu.VMEM((nheads, bq, LANE), F32), pltpu.VMEM((nheads, bq, LANE), F32)],
        compiler_params=_cp(("parallel", "parallel", "parallel")))(qkv, qkv, qkv)


def _attn_bwd(qkv, car, do, nb, s, d):
    t = nb * s
    npair = d // LANE
    bk = ATT_BLK
    bq = min(ATT_BQ, s)
    nq = s // bq
    kpq = bq // bk
    nheads = LANE // HEAD_DIM
    scale = HEAD_DIM ** -0.5

    def body(q_ref, k_ref, v_ref, car_ref, do_ref, dq_ref, dk_ref, dv_ref, dk_acc, dv_acc, dq_s, rune_s):
        qi = pl.program_id(2)

        @pl.when(qi == 0)
        def _():
            dk_acc[...] = jnp.zeros_like(dk_acc)
            dv_acc[...] = jnp.zeros_like(dv_acc)

        q = q_ref[...]
        dov = do_ref[...]
        lane = lax.broadcasted_iota(jnp.int32, (1, LANE), 1)
        row = lax.broadcasted_iota(jnp.int32, (bq, bk), 0)
        col = lax.broadcasted_iota(jnp.int32, (bq, bk), 1)
        trow = lax.broadcasted_iota(jnp.int32, (bk, bk), 0)
        tcol = lax.broadcasted_iota(jnp.int32, (bk, bk), 1)
        tri_suf = (trow > tcol).astype(BF16)
        tri_pre = (trow < tcol).astype(BF16)
        hms = [(lane // HEAD_DIM) == hh for hh in range(nheads)]
        qhs = [jnp.where(hm, q, jnp.zeros_like(q)) for hm in hms]
        dohs = [jnp.where(hm, dov, jnp.zeros_like(dov)) for hm in hms]
        dq_s[...] = jnp.zeros_like(dq_s)
        rune_s[...] = jnp.zeros_like(rune_s)

        def step(j, carry):
            k0 = pl.multiple_of(j * bk, bk)
            kj = k_ref[pl.ds(k0, bk), :]
            vj = v_ref[pl.ds(k0, bk), :]
            mask = (k0 + col) < (qi * bq + row)
            car = car_ref[...]
            for hh in range(nheads):
                z = lax.dot_general(qhs[hh], kj, _NT, preferred_element_type=F32) * scale
                lb = _log_sigmoid(z)
                l1u = lb - z
                l1 = jnp.where(mask, l1u, 0.0)
                run = jnp.sum(jnp.where(lane == hh * HEAD_DIM + j, car, 0.0), axis=1, keepdims=True)
                suf = _split_dot(l1, tri_suf) + run
                a = jnp.where(mask, jnp.exp(lb + suf), 0.0)
                da = lax.dot_general(dohs[hh], vj, _NT, preferred_element_type=F32)
                e = da * a
                run_e = rune_s[hh]
                pre_e = _split_dot(e, tri_pre) + run_e
                dz = jnp.where(mask, e * jnp.exp(l1u) - pre_e * jnp.exp(lb), 0.0) * scale
                dzb = dz.astype(BF16)
                dq_s[hh] += jnp.dot(dzb, kj, preferred_element_type=F32)
                dkh = lax.dot_general(dzb, qhs[hh], _TN, preferred_element_type=F32)
                dvh = lax.dot_general(a.astype(BF16), dohs[hh], _TN, preferred_element_type=F32)
                dk_blk = dkh if hh == 0 else dk_blk + dkh
                dv_blk = dvh if hh == 0 else dv_blk + dvh
                rune_s[hh] = run_e + jnp.sum(e, axis=1, keepdims=True)
            dk_acc[pl.ds(k0, bk), :] += dk_blk
            dv_acc[pl.ds(k0, bk), :] += dv_blk
            return carry

        lax.fori_loop(0, (qi + 1) * kpq, step, 0)
        dq_out = dq_s[0]
        for hh in range(1, nheads):
            dq_out = jnp.where(hms[hh], dq_s[hh], dq_out)
        dq_ref[...] = dq_out.astype(dq_ref.dtype)

        @pl.when(qi == nq - 1)
        def _():
            dk_ref[...] = dk_acc[...].astype(dk_ref.dtype)
            dv_ref[...] = dv_acc[...].astype(dv_ref.dtype)

    qspec = pl.BlockSpec((bq, LANE), lambda b, p, i: (b * nq + i, p))
    kvout = pl.BlockSpec((s, LANE), lambda b, p, i: (b, p))
    sds = jax.ShapeDtypeStruct((t, d), BF16)
    return pl.pallas_call(
        body, name="attn_bwd", grid=(nb, npair, nq),
        in_specs=[qspec,
                  pl.BlockSpec((s, LANE), lambda b, p, i: (b, npair + p)),
                  pl.BlockSpec((s, LANE), lambda b, p, i: (b, 2 * npair + p)),
                  qspec, qspec],
        out_specs=[qspec, kvout, kvout], out_shape=[sds, sds, sds],
        scratch_shapes=[pltpu.VMEM((s, LANE), F32), pltpu.VMEM((s, LANE), F32),
                        pltpu.VMEM((nheads, bq, LANE), F32), pltpu.VMEM((nheads, bq, LANE), F32)],
        compiler_params=_cp(("parallel", "parallel", "arbitrary")))(qkv, qkv, qkv, car, do)


def _conv3(u_ref, w, bias, c, r0, rc):
    x = u_ref[pl.ds(r0, rc), :].astype(F32)
    p0 = pl.multiple_of(jnp.maximum(r0 - 16, 0), 16)
    prev = u_ref[pl.ds(p0, 16), :].astype(F32)
    prev = jnp.where(c > 0, prev, 0.0)
    row = lax.broadcasted_iota(jnp.int32, (rc, 1), 0)
    s1 = jnp.where(row == 0, prev[15:16, :], pltpu.roll(x, 1, 0))
    s2 = jnp.where(row == 0, prev[14:15, :], jnp.where(row == 1, prev[15:16, :], pltpu.roll(x, 2, 0)))
    cv = w[2:3, :] * x + w[1:2, :] * s1 + w[0:1, :] * s2 + bias
    return cv, x, s1, s2


def _sigmoid(x):
    return 1.0 / (1.0 + jnp.exp(-x))


def _ffn_act_fwd(name, up8, cw8, cb8, nb, s):
    _, t, c_w = up8.shape
    rc = _tile(s, 256)
    nch = s // rc
    half = N_DEV // 2

    def body(ug_ref, uv_ref, wg_ref, wv_ref, bg_ref, bv_ref, act_ref):
        wg, wv, bg, bv = wg_ref[...], wv_ref[...], bg_ref[...], bv_ref[...]

        def chunk(c, carry):
            r0 = pl.multiple_of(c * rc, rc)
            cg = _conv3(ug_ref, wg, bg, c, r0, rc)[0]
            cv = _conv3(uv_ref, wv, bv, c, r0, rc)[0]
            act_ref[pl.ds(r0, rc), :] = (cg * _sigmoid(cg) * cv).astype(act_ref.dtype)
            return carry

        lax.fori_loop(0, nch, chunk, 0)

    def slab(off):
        return pl.BlockSpec((None, s, c_w), lambda k, b: (k + off, b, 0))

    def par(rows, off):
        return pl.BlockSpec((None, rows, c_w), lambda k, b: (k + off, 0, 0))

    return pl.pallas_call(
        body, name=name, grid=(half, nb),
        in_specs=[slab(0), slab(half), par(3, 0), par(3, half), par(1, 0), par(1, half)],
        out_specs=pl.BlockSpec((None, s, c_w), lambda k, b: (k, b, 0)),
        out_shape=jax.ShapeDtypeStruct((half, t, c_w), BF16),
        compiler_params=_cp(("parallel", "parallel")))(up8, up8, cw8, cw8, cb8, cb8)


def _ffn_act_bwd(name, up8, dact4, cw8, cb8, nb, s):
    _, t, c_w = up8.shape
    rc = _tile(s, 256)
    nch = s // rc
    half = N_DEV // 2

    def body(us_ref, uo_ref, da_ref, ws_ref, wo_ref, bs_ref, bo_ref, dup_ref, dcw_ref, dcb_ref):
        k = pl.program_id(0)
        ws, wo, bs, bo = ws_ref[...], wo_ref[...], bs_ref[...], bo_ref[...]
        is_gate = k < half
        row = lax.broadcasted_iota(jnp.int32, (rc, 1), 0)

        @pl.when(pl.program_id(1) == 0)
        def _():
            dcw_ref[...] = jnp.zeros_like(dcw_ref)
            dcb_ref[...] = jnp.zeros_like(dcb_ref)

        def chunk(i, carry):
            n0, n1, a0, a1, a2, ab = carry
            c = nch - 1 - i
            r0 = pl.multiple_of(c * rc, rc)
            cs, x, s1, s2 = _conv3(us_ref, ws, bs, c, r0, rc)
            co = _conv3(uo_ref, wo, bo, c, r0, rc)[0]
            da = da_ref[pl.ds(r0, rc), :].astype(F32)
            gt = jnp.where(is_gate, cs, co)
            vl = jnp.where(is_gate, co, cs)
            sg = _sigmoid(gt)
            d_gate = da * vl * sg * (1.0 + gt * (1.0 - sg))
            d_val = da * gt * sg
            dcv = jnp.where(is_gate, d_gate, d_val)
            t1 = jnp.where(row == rc - 1, n0, pltpu.roll(dcv, rc - 1, 0))
            t2 = jnp.where(row == rc - 2, n0, jnp.where(row == rc - 1, n1, pltpu.roll(dcv, rc - 2, 0)))
            dup = ws[2:3, :] * dcv + ws[1:2, :] * t1 + ws[0:1, :] * t2
            dup_ref[pl.ds(r0, rc), :] = dup.astype(dup_ref.dtype)
            a2 = a2 + jnp.sum(dcv * x, axis=0, keepdims=True)
            a1 = a1 + jnp.sum(dcv * s1, axis=0, keepdims=True)
            a0 = a0 + jnp.sum(dcv * s2, axis=0, keepdims=True)
            ab = ab + jnp.sum(dcv, axis=0, keepdims=True)
            return dcv[0:1, :], dcv[1:2, :], a0, a1, a2, ab

        z = jnp.zeros((1, c_w), F32)
        _, _, a0, a1, a2, ab = lax.fori_loop(0, nch, chunk, (z, z, z, z, z, z))
        dcw_ref[0:1, :] += a0
        dcw_ref[1:2, :] += a1
        dcw_ref[2:3, :] += a2
        dcb_ref[...] += ab

    def slab(fn):
        return pl.BlockSpec((None, s, c_w), lambda k, b: (fn(k), b, 0))

    def par(rows, fn):
        return pl.BlockSpec((None, rows, c_w), lambda k, b: (fn(k), 0, 0))

    same = lambda k: k
    other = lambda k: (k + half) % N_DEV
    return pl.pallas_call(
        body, name=name, grid=(N_DEV, nb),
        in_specs=[slab(same), slab(other), slab(lambda k: k % half), par(3, same), par(3, other),
                  par(1, same), par(1, other)],
        out_specs=[slab(same), par(3, same), par(1, same)],
        out_shape=[jax.ShapeDtypeStruct((N_DEV, t, c_w), BF16), jax.ShapeDtypeStruct((N_DEV, 3, c_w), F32),
                   jax.ShapeDtypeStruct((N_DEV, 1, c_w), F32)],
        compiler_params=_cp(("parallel", "arbitrary")))(up8, up8, dact4, cw8, cw8, cb8, cb8)


_GELU_C0 = math.sqrt(2.0 / math.pi)
_GELU_C1 = 0.044715


def _rowwise(name, body, ins, in_kinds, out_kinds, t, d, tr_pref=512):
    tr = _tile(t, tr_pref)
    row = pl.BlockSpec((tr, d), lambda i: (i, 0))
    vec = pl.BlockSpec((1, d), lambda i: (0, 0))
    in_specs = [row if k == "row" else vec for k in in_kinds]
    out_specs = [row if k[0] == "row" else vec for k in out_kinds]
    out_shape = [jax.ShapeDtypeStruct((t, d) if k[0] == "row" else (1, d), k[1]) for k in out_kinds]
    has_acc = any(k[0] == "acc" for k in out_kinds)
    return pl.pallas_call(body, name=name, grid=(t // tr,), in_specs=in_specs, out_specs=out_specs,
                          out_shape=out_shape,
                          compiler_params=_cp(("arbitrary",) if has_acc else ("parallel",)))(*ins)


def _ssm_post_fwd(ys, u, dskip):
    t, d = ys.shape

    def body(ys_ref, u_ref, ds_ref, y_ref, z_ref):
        y = ys_ref[...] + ds_ref[...] * u_ref[...].astype(F32)
        y_ref[...] = y
        th = jnp.tanh(_GELU_C0 * (y + _GELU_C1 * y * y * y))
        z_ref[...] = (0.5 * y * (1.0 + th)).astype(z_ref.dtype)

    return _rowwise("ssm_post_fwd", body, [ys, u, dskip], ["row", "row", "vec"],
                    [("row", F32), ("row", BF16)], t, d)


def _glu_fwd(z, gl, bglu):
    t, d = z.shape

    def body(z_ref, gl_ref, b_ref, o_ref):
        o_ref[...] = (z_ref[...].astype(F32) * _sigmoid(gl_ref[...] + b_ref[...])).astype(o_ref.dtype)

    return _rowwise("glu_fwd", body, [z, gl, bglu], ["row", "row", "vec"], [("row", BF16)], t, d)[0]


def _glu_bwd(dgg, z, gl, bglu):
    t, d = z.shape

    def body(dg_ref, z_ref, gl_ref, b_ref, dgl_ref, dz_ref, db_ref):
        sg = _sigmoid(gl_ref[...] + b_ref[...])
        dg = dg_ref[...]
        dgl = dg * z_ref[...].astype(F32) * sg * (1.0 - sg)
        dgl_ref[...] = dgl.astype(dgl_ref.dtype)
        dz_ref[...] = dg * sg

        @pl.when(pl.program_id(0) == 0)
        def _():
            db_ref[...] = jnp.zeros_like(db_ref)

        db_ref[...] += jnp.sum(dgl, axis=0, keepdims=True)

    return _rowwise("glu_bwd", body, [dgg, z, gl, bglu], ["row", "row", "row", "vec"],
                    [("row", BF16), ("row", F32), ("acc", F32)], t, d)


def _ssm_post_bwd(dz1, dz2, y, u, dskip):
    t, d = y.shape

    def body(a_ref, b_ref, y_ref, u_ref, ds_ref, dy_ref, du_ref, dd_ref):
        yv = y_ref[...]
        inner = _GELU_C0 * (yv + _GELU_C1 * yv * yv * yv)
        th = jnp.tanh(inner)
        dgelu = 0.5 * (1.0 + th) + 0.5 * yv * (1.0 - th * th) * _GELU_C0 * (1.0 + 3.0 * _GELU_C1 * yv * yv)
        dy = (a_ref[...] + b_ref[...]) * dgelu
        dy_ref[...] = dy.astype(dy_ref.dtype)
        du_ref[...] = dy * ds_ref[...]

        @pl.when(pl.program_id(0) == 0)
        def _():
            dd_ref[...] = jnp.zeros_like(dd_ref)

        dd_ref[...] += jnp.sum(dy * u_ref[...].astype(F32), axis=0, keepdims=True)

    return _rowwise("ssm_post_bwd", body, [dz1, dz2, y, u, dskip], ["row", "row", "row", "row", "vec"],
                    [("row", BF16), ("row", F32), ("acc", F32)], t, d)


def _add_cast(a, b):
    t, d = a.shape

    def body(a_ref, b_ref, o_ref):
        o_ref[...] = (a_ref[...] + b_ref[...]).astype(o_ref.dtype)

    return _rowwise("add_cast", body, [a, b], ["row", "row"], [("row", BF16)], t, d)[0]


def _ssm_scan(e_re, e_im, lam_re, lam_im, nb):
    r, n = e_re.shape
    nc = r // nb
    cb = _tile(n, 512)

    def body(er_ref, ei_ref, lr_ref, li_ref, xr_ref, xi_ref):
        lr, li = lr_ref[...], li_ref[...]
        rid = lax.broadcasted_iota(jnp.int32, (8, 1), 0)
        for b in range(nb):
            def tile(i, carry, b=b):
                xr, xi = carry
                r0 = pl.multiple_of(b * nc + i * 8, 8)
                er, ei = er_ref[pl.ds(r0, 8), :], ei_ref[pl.ds(r0, 8), :]
                outr, outi = jnp.zeros((8, cb), F32), jnp.zeros((8, cb), F32)
                for j in range(8):
                    outr = jnp.where(rid == j, xr, outr)
                    outi = jnp.where(rid == j, xi, outi)
                    xr, xi = lr * xr - li * xi + er[j:j + 1, :], li * xr + lr * xi + ei[j:j + 1, :]
                xr_ref[pl.ds(r0, 8), :] = outr
                xi_ref[pl.ds(r0, 8), :] = outi
                return xr, xi

            z = jnp.zeros((1, cb), F32)
            lax.fori_loop(0, nc // 8, tile, (z, z))

    mat = pl.BlockSpec((r, cb), lambda j: (0, j))
    vec = pl.BlockSpec((1, cb), lambda j: (0, j))
    sds = jax.ShapeDtypeStruct((r, n), F32)
    return pl.pallas_call(body, name="ssm_scan", grid=(n // cb,), in_specs=[mat, mat, vec, vec],
                          out_specs=[mat, mat], out_shape=[sds, sds],
                          compiler_params=_cp(("parallel",)))(e_re, e_im, lam_re, lam_im)


def _ssm_scan_bwd(dxp_re, dxp_im, xp_re, xp_im, lam_re, lam_im, nb):
    r, n = dxp_re.shape
    nc = r // nb
    cb = _tile(n, 512)

    def body(dr_ref, di_ref, xr_ref, xi_ref, lr_ref, li_ref, er_ref, ei_ref, dlr_ref, dli_ref):
        lr, li = lr_ref[...], li_ref[...]
        rid = lax.broadcasted_iota(jnp.int32, (8, 1), 0)
        z = jnp.zeros((1, cb), F32)
        alr, ali = z, z
        for b in range(nb):
            def tile(i, carry, b=b):
                gr, gi, alr, ali = carry
                r0 = pl.multiple_of(b * nc + (nc // 8 - 1 - i) * 8, 8)
                dr, di = dr_ref[pl.ds(r0, 8), :], di_ref[pl.ds(r0, 8), :]
                xr, xi = xr_ref[pl.ds(r0, 8), :], xi_ref[pl.ds(r0, 8), :]
                outr, outi = jnp.zeros((8, cb), F32), jnp.zeros((8, cb), F32)
                for j in range(7, -1, -1):
                    outr = jnp.where(rid == j, gr, outr)
                    outi = jnp.where(rid == j, gi, outi)
                    xrj, xij = xr[j:j + 1, :], xi[j:j + 1, :]
                    alr = alr + gr * xrj + gi * xij
                    ali = ali + gi * xrj - gr * xij
                    gr, gi = dr[j:j + 1, :] + lr * gr + li * gi, di[j:j + 1, :] + lr * gi - li * gr
                er_ref[pl.ds(r0, 8), :] = outr
                ei_ref[pl.ds(r0, 8), :] = outi
                return gr, gi, alr, ali

            _, _, alr, ali = lax.fori_loop(0, nc // 8, tile, (z, z, alr, ali))
        dlr_ref[...] = alr
        dli_ref[...] = ali

    mat = pl.BlockSpec((r, cb), lambda j: (0, j))
    vec = pl.BlockSpec((1, cb), lambda j: (0, j))
    sds = jax.ShapeDtypeStruct((r, n), F32)
    vds = jax.ShapeDtypeStruct((1, n), F32)
    return pl.pallas_call(body, name="ssm_scan_bwd", grid=(n // cb,), in_specs=[mat, mat, mat, mat, vec, vec],
                          out_specs=[mat, mat, vec, vec], out_shape=[sds, sds, vds, vds],
                          compiler_params=_cp(("parallel",)))(dxp_re, dxp_im, xp_re, xp_im, lam_re, lam_im)


def _ssm_operators(a_re, a_im, log_dt, b_re, b_im, c_re, c_im):
    g, p = a_re.shape
    h = b_re.shape[-1]
    ln = SSM_L
    hp = lax.Precision.HIGHEST
    lam = lax.complex(a_re, a_im)
    ldt = lam * jnp.exp(log_dt)[:, None]
    lam_bar = jnp.exp(ldt)
    bbar = ((lam_bar - 1.0) / lam)[..., None] * lax.complex(b_re, b_im)
    cm = lax.complex(c_re, c_im)
    steps = jnp.arange(ln + 1, dtype=F32)
    pw = jnp.exp(ldt[:, None, :] * steps[None, :, None])
    kd = jnp.einsum("ghp,gdp,gpk->gdhk", cm, pw[:, :ln], bbar, precision=hp).real
    sig = jnp.arange(ln)[:, None]
    tau = jnp.arange(ln)[None, :]
    lag = tau - sig
    tm = jnp.where((lag >= 0)[None, :, :, None, None], kd[:, jnp.clip(lag, 0, ln - 1)], 0.0)
    tm = tm.transpose(0, 1, 4, 2, 3).reshape(g, ln * h, ln * h)
    wx = pw[:, ln - 1 - jnp.arange(ln)][:, :, :, None] * bbar[:, None]
    wx = wx.transpose(0, 1, 3, 2).reshape(g, ln * h, p)
    cp = cm[:, None] * pw[:, 1:ln + 1][:, :, None, :]
    cp = cp.reshape(g, ln * h, p).transpose(0, 2, 1)
    odd = (jnp.arange(g) % 2 == 1)[:, None, None]

    def pad(m, axis):
        z = jnp.zeros_like(m)
        return jnp.where(odd, jnp.concatenate([z, m], axis), jnp.concatenate([m, z], axis))

    lam_l = pw[:, ln]
    return (tm, pad(wx.real, 2), pad(wx.imag, 2), pad(cp.real, 1), pad(-cp.imag, 1),
            lam_l.real.reshape(1, g * p), lam_l.imag.reshape(1, g * p))


def _to_groups(a, nb, s):
    d = a.shape[1]
    g = d // SSM_H
    return a.reshape(nb, s // SSM_L, SSM_L, g, SSM_H).transpose(3, 0, 1, 2, 4).reshape(g, nb * (s // SSM_L), SSM_L * SSM_H)


def _from_groups(a, nb, s):
    g = a.shape[0]
    return a.reshape(g, nb, s // SSM_L, SSM_L, SSM_H).transpose(1, 2, 3, 0, 4).reshape(nb * s, g * SSM_H)


def _ssm_core_fwd(ug, ops, nb):
    tm, wxr, wxi, wyr, wyi, lam_re, lam_im = ops
    g, r, w = ug.shape
    n = g * SSM_P
    grp = lambda off: pl.BlockSpec((None, r, w), lambda j: (2 * j + off, 0, 0))
    wxs = lambda off: pl.BlockSpec((None, w, LANE), lambda j: (2 * j + off, 0, 0))
    pair_out = pl.BlockSpec((r, LANE), lambda j: (0, j))
    sds = jax.ShapeDtypeStruct((r, n), F32)
    e_re = _mm_terms("ssm_e_re", [(ug, grp(0), wxr, wxs(0), _NN), (ug, grp(1), wxr, wxs(1), _NN)], (g // 2,), pair_out, sds)
    e_im = _mm_terms("ssm_e_im", [(ug, grp(0), wxi, wxs(0), _NN), (ug, grp(1), wxi, wxs(1), _NN)], (g // 2,), pair_out, sds)
    xp_re, xp_im = _ssm_scan(e_re, e_im, lam_re, lam_im, nb)
    one = pl.BlockSpec((None, r, w), lambda j: (j, 0, 0))
    sq = pl.BlockSpec((None, w, w), lambda j: (j, 0, 0))
    pair_in = pl.BlockSpec((r, LANE), lambda j: (0, j // 2))
    wys = pl.BlockSpec((None, LANE, w), lambda j: (j, 0, 0))
    y = _mm_terms("ssm_y", [(ug, one, tm, sq, _NN), (xp_re, pair_in, wyr, wys, _NN), (xp_im, pair_in, wyi, wys, _NN)],
                  (g,), one, jax.ShapeDtypeStruct((g, r, w), F32))
    return y, xp_re, xp_im


def _ssm_core_bwd(dyg, ug, xp_re, xp_im, ops, nb):
    tm, wxr, wxi, wyr, wyi, lam_re, lam_im = ops
    g, r, w = ug.shape
    n = g * SSM_P
    grp = lambda off: pl.BlockSpec((None, r, w), lambda j: (2 * j + off, 0, 0))
    wyp = lambda off: pl.BlockSpec((None, LANE, w), lambda j: (2 * j + off, 0, 0))
    pair_out = pl.BlockSpec((r, LANE), lambda j: (0, j))
    sds = jax.ShapeDtypeStruct((r, n), F32)
    dxp_re = _mm_terms("ssm_dxp_re", [(dyg, grp(0), wyr, wyp(0), _NT), (dyg, grp(1), wyr, wyp(1), _NT)], (g // 2,), pair_out, sds)
    dxp_im = _mm_terms("ssm_dxp_im", [(dyg, grp(0), wyi, wyp(0), _NT), (dyg, grp(1), wyi, wyp(1), _NT)], (g // 2,), pair_out, sds)
    de_re, de_im, dlam_re, dlam_im = _ssm_scan_bwd(dxp_re, dxp_im, xp_re, xp_im, lam_re, lam_im, nb)
    one = pl.BlockSpec((None, r, w), lambda j: (j, 0, 0))
    sq = pl.BlockSpec((None, w, w), lambda j: (j, 0, 0))
    pair_in = pl.BlockSpec((r, LANE), lambda j: (0, j // 2))
    wxs = pl.BlockSpec((None, w, LANE), lambda j: (j, 0, 0))
    wys = pl.BlockSpec((None, LANE, w), lambda j: (j, 0, 0))
    du = _mm_terms("ssm_du", [(dyg, one, tm, sq, _NT), (de_re, pair_in, wxr, wxs, _NT), (de_im, pair_in, wxi, wxs, _NT)],
                   (g,), one, jax.ShapeDtypeStruct((g, r, w), F32))
    dtm = _mm_terms("ssm_dtm", [(ug, one, dyg, one, _TN)], (g,), sq, jax.ShapeDtypeStruct((g, w, w), F32))
    dwy = lambda nm, xp: _mm_terms(nm, [(xp, pair_in, dyg, one, _TN)], (g,), wys, jax.ShapeDtypeStruct((g, LANE, w), F32))
    dwx = lambda nm, de: _mm_terms(nm, [(ug, one, de, pair_in, _TN)], (g,), wxs, jax.ShapeDtypeStruct((g, w, LANE), F32))
    return du, (dtm, dwx("ssm_dwx_re", de_re), dwx("ssm_dwx_im", de_im), dwy("ssm_dwy_re", xp_re),
                dwy("ssm_dwy_im", xp_im), dlam_re, dlam_im)


def _modfin_fwd(c_all, w_mod, w_fin):
    n, d = c_all.shape
    nl, _, cm = w_mod.shape
    cf = w_fin.shape[1]
    width = nl * cm + cf
    hp = lax.Precision.HIGHEST

    def body(c_ref, wm_ref, wf_ref, act_ref, out_ref):
        cv = c_ref[...]
        act = cv * _sigmoid(cv)
        act_ref[...] = act
        for i in range(nl):
            out_ref[:, i * cm:(i + 1) * cm] = jnp.dot(act, wm_ref[i], preferred_element_type=F32, precision=hp)
        out_ref[:, nl * cm:] = jnp.dot(act, wf_ref[...], preferred_element_type=F32, precision=hp)

    return pl.pallas_call(body, name="modfin_fwd",
                          out_shape=[jax.ShapeDtypeStruct((n, d), F32), jax.ShapeDtypeStruct((n, width), F32)],
                          compiler_params=_cp(None))(c_all, w_mod, w_fin)


def _modfin_bwd(c_act_t, dmod_loc, dfin_loc, dall):
    d, n = c_act_t.shape
    nl, _, cm = dmod_loc.shape
    cf = dfin_loc.shape[1]
    hp = lax.Precision.HIGHEST

    def body(ct_ref, dm_ref, df_ref, da_ref, gwm_ref, gwf_ref, gb_ref):
        ct = ct_ref[...]
        for i in range(nl):
            gwm_ref[i] = jnp.dot(ct, dm_ref[i], preferred_element_type=F32, precision=hp)
        gwf_ref[...] = jnp.dot(ct, df_ref[...], preferred_element_type=F32, precision=hp)
        gb_ref[...] = jnp.sum(da_ref[...], axis=0, keepdims=True)

    return pl.pallas_call(body, name="modfin_bwd",
                          out_shape=[jax.ShapeDtypeStruct((nl, d, cm), F32), jax.ShapeDtypeStruct((d, cf), F32),
                                     jax.ShapeDtypeStruct((1, dall.shape[1]), F32)],
                          compiler_params=_cp(None))(c_act_t, dmod_loc, dfin_loc, dall)


def _adamw(name, gparts, w, m, v):
    n, r, c = gparts.shape
    tr = _tile(r, 256)
    c1 = 1.0 / (1.0 - ADAM_B1 ** ADAM_STEP)
    c2 = 1.0 / (1.0 - ADAM_B2 ** ADAM_STEP)

    def body(gp_ref, w_ref, m_ref, v_ref, g_ref, d_ref, mo_ref, vo_ref):
        gsum = gp_ref[0].astype(F32)
        for j in range(1, n):
            gsum = gsum + gp_ref[j].astype(F32)
        mn = ADAM_B1 * m_ref[...] + (1.0 - ADAM_B1) * gsum
        vn = ADAM_B2 * v_ref[...] + (1.0 - ADAM_B2) * (gsum * gsum)
        g_ref[...] = gsum
        mo_ref[...] = mn
        vo_ref[...] = vn
        d_ref[...] = -ADAM_LR * ((mn * c1) / (jnp.sqrt(vn * c2) + ADAM_EPS) + ADAM_WD * w_ref[...])

    mat = pl.BlockSpec((tr, c), lambda i: (i, 0))
    sds = jax.ShapeDtypeStruct((r, c), F32)
    return pl.pallas_call(body, name=name, grid=(r // tr,),
                          in_specs=[pl.BlockSpec((n, tr, c), lambda i: (0, i, 0)), mat, mat, mat],
                          out_specs=[mat] * 4, out_shape=[sds] * 4,
                          compiler_params=_cp(("parallel",)))(gparts, w, m, v)


def _sum_parts(name, parts):
    n, r, c = parts.shape

    def body(p_ref, o_ref):
        acc = p_ref[0]
        for j in range(1, n):
            acc = acc + p_ref[j]
        o_ref[...] = acc

    return pl.pallas_call(body, name=name, out_shape=jax.ShapeDtypeStruct((r, c), F32),
                          compiler_params=_cp(None))(parts)


def _exchange(name, arrs, gather):
    n = len(arrs)
    out_shape = [pltpu.HBM(((N_DEV,) + a.shape) if gather else a.shape, a.dtype) for a in arrs]

    def body(*refs):
        ins, outs = refs[:n], refs[n:2 * n]
        send_sems, recv_sems, local_sems = refs[2 * n:]
        x, y, c = lax.axis_index("x"), lax.axis_index("y"), lax.axis_index("c")
        me = 4 * x + 2 * y + c

        def peer_of(dd):
            px = jnp.bitwise_xor(x, dd >> 2)
            py = jnp.bitwise_xor(y, (dd >> 1) & 1)
            pc = jnp.bitwise_xor(c, dd & 1)
            return (px, py, pc), 4 * px + 2 * py + pc

        local, sends, recvs = [], [], []
        for i in range(n):
            src_me = ins[i] if gather else ins[i].at[me]
            local.append(pltpu.make_async_copy(src_me, outs[i].at[me], local_sems.at[i]))
        for dd in range(1, N_DEV):
            peer, pid = peer_of(dd)
            for i in range(n):
                src = ins[i] if gather else ins[i].at[pid]
                sends.append(pltpu.make_async_remote_copy(
                    src_ref=src, dst_ref=outs[i].at[me], send_sem=send_sems.at[i, dd - 1],
                    recv_sem=recv_sems.at[i, dd - 1], device_id=peer, device_id_type=MESH))
                recvs.append(pltpu.make_async_remote_copy(
                    src_ref=src, dst_ref=outs[i].at[pid], send_sem=send_sems.at[i, dd - 1],
                    recv_sem=recv_sems.at[i, dd - 1], device_id=peer, device_id_type=MESH))
        for cp in local + sends:
            cp.start()
        for cp in recvs:
            cp.wait_recv()
        for cp in sends:
            cp.wait_send()
        for cp in local:
            cp.wait()

    hbm = pl.BlockSpec(memory_space=pltpu.HBM)
    ops = [pltpu.with_memory_space_constraint(a, pltpu.HBM) for a in arrs]
    outs = pl.pallas_call(
        body, name=name, in_specs=[hbm] * n, out_specs=[hbm] * n, out_shape=out_shape,
        scratch_shapes=[pltpu.SemaphoreType.DMA((n, N_DEV - 1)), pltpu.SemaphoreType.DMA((n, N_DEV - 1)),
                        pltpu.SemaphoreType.DMA((n,))])(*ops)
    return list(outs)


def _pack(pieces):
    flat = jnp.concatenate([p.reshape(-1) for p in pieces])
    pad = (-flat.shape[0]) % (8 * LANE)
    return jnp.pad(flat, (0, pad)).reshape(-1, LANE)


def _unpack(packed, shapes):
    flat = packed.reshape(-1)
    out, off = [], 0
    for shp in shapes:
        sz = math.prod(shp)
        out.append(flat[off:off + sz].reshape(shp))
        off += sz
    return out


def kernel(x, c, norm_mix, norm_ffn, w_mod, b_mod, w_qkv, w_o_attn, w_in_ssm, a_re, a_im, log_dt, b_re, b_im, c_re, c_im, d_skip, w_glu, b_glu, w_o_ssm, w_up, conv_w, conv_b, w_down, norm_out, w_fin, b_fin, loss_target, m_norm_mix, m_norm_ffn, m_w_mod, m_b_mod, m_w_qkv, m_w_o_attn, m_w_in_ssm, m_a_re, m_a_im, m_log_dt, m_b_re, m_b_im, m_c_re, m_c_im, m_d_skip, m_w_glu, m_b_glu, m_w_o_ssm, m_w_up, m_conv_w, m_conv_b, m_w_down, m_norm_out, m_w_fin, m_b_fin, v_norm_mix, v_norm_ffn, v_w_mod, v_b_mod, v_w_qkv, v_w_o_attn, v_w_in_ssm, v_a_re, v_a_im, v_log_dt, v_b_re, v_b_im, v_c_re, v_c_im, v_d_skip, v_w_glu, v_b_glu, v_w_o_ssm, v_w_up, v_conv_w, v_conv_b, v_w_down, v_norm_out, v_w_fin, v_b_fin):
    nb, s, d = x.shape
    t = nb * s
    n_seq = nb * N_DEV
    me = 4 * lax.axis_index("x") + 2 * lax.axis_index("y") + lax.axis_index("c")
    cm = w_mod.shape[2]
    cf = w_fin.shape[1]
    c_up = w_up.shape[2]
    r_dn = w_down.shape[1]
    g_ssm = d // SSM_H

    (wq8, wo8, win8, wglu8, wos8, wup8_0, wup8_1, wd8_0, wd8_1, cw8, dskip8, bglu8, c8) = _exchange(
        "gather_weights",
        [w_qkv[0].astype(BF16), w_o_attn[0].astype(BF16), w_in_ssm[0].astype(BF16), w_glu[0].astype(BF16),
         w_o_ssm[0].astype(BF16), w_up[0].astype(BF16), w_up[1].astype(BF16), w_down[0].astype(BF16),
         w_down[1].astype(BF16), conv_w, d_skip, b_glu, c], True)
    wo = wo8.reshape(d, d)
    win = win8.reshape(d, d)
    wglu = wglu8.reshape(d, d)
    wos = wos8.reshape(d, d)
    wup8 = [wup8_0, wup8_1]
    half = N_DEV // 2
    wd4 = [wd8_0.reshape(half, 2 * r_dn, d), wd8_1.reshape(half, 2 * r_dn, d)]
    cw_l = [cw8[:, 0], cw8[:, 1]]
    cb_l = [conv_b[i].reshape(N_DEV, 1, c_up) for i in range(2)]
    dskip_f = dskip8.reshape(1, d)
    bglu_f = bglu8.reshape(1, d)
    c_all = c8.reshape(n_seq, d)

    c_act, modloc = _modfin_fwd(c_all, w_mod, w_fin)
    (mod8,) = _exchange("gather_mod", [modloc], True)
    mine = lax.dynamic_slice_in_dim(mod8, me * nb, nb, axis=1)
    mods = []
    for i in range(2):
        mi = mine[:, :, i * cm:(i + 1) * cm].transpose(1, 0, 2).reshape(nb, N_DEV * cm) + b_mod[i]
        mods.append([mi[:, j * d:(j + 1) * d].reshape(nb, 1, d) for j in range(6)])
    fin = mine[:, :, 2 * cm:].transpose(1, 0, 2).reshape(nb, N_DEV * cf) + b_fin
    sh_f, sc_f = fin[:, :d].reshape(nb, 1, d), fin[:, d:].reshape(nb, 1, d)

    row = lambda a: a.reshape(1, -1)
    x0 = x.reshape(t, d)

    def ffn_fwd(i, xin, sh, sc, gate):
        h = _norm_mod_fwd(f"ffn{i}_norm", xin, row(norm_ffn[i]), sh, sc, nb)
        up = _mm(f"ffn{i}_up", h, wup8[i], (t // tm_, N_DEV, 1),
                 pl.BlockSpec((tm_, d), lambda a, b, k: (a, 0)), pl.BlockSpec((None, d, c_up), lambda a, b, k: (b, 0, 0)),
                 pl.BlockSpec((None, tm_, c_up), lambda a, b, k: (b, a, 0)),
                 jax.ShapeDtypeStruct((N_DEV, t, c_up), BF16), _NN, (tm_, c_up))
        act = _ffn_act_fwd(f"ffn{i}_act", up, cw_l[i], cb_l[i], nb, s)
        yf = _mm(f"ffn{i}_down", act, wd4[i], (t // tm_, 1, half),
                 pl.BlockSpec((None, tm_, c_up), lambda a, b, k: (k, a, 0)),
                 pl.BlockSpec((None, c_up, d), lambda a, b, k: (k, 0, 0)),
                 pl.BlockSpec((tm_, d), lambda a, b, k: (a, 0)), jax.ShapeDtypeStruct((t, d), F32), _NN, (tm_, d))
        xout = _gate_add(f"ffn{i}_res", xin, yf, gate, nb)
        return xout, (h, up, act, yf)

    tm_ = _tile(t, 512)
    sh1, sc1, g1, sh2, sc2, g2 = mods[0]
    h1 = _norm_mod_fwd("attn_norm", x0, row(norm_mix[0]), sh1, sc1, nb)
    cq = wq8.shape[2]
    qkv = _mm("attn_qkv", h1, wq8, (t // tm_, N_DEV, 1),
              pl.BlockSpec((tm_, d), lambda a, b, k: (a, 0)), pl.BlockSpec((None, d, cq), lambda a, b, k: (b, 0, 0)),
              pl.BlockSpec((tm_, cq), lambda a, b, k: (a, b)), jax.ShapeDtypeStruct((t, 3 * d), BF16), _NN, (tm_, cq))
    o_att, car_att = _attn_fwd(qkv, nb, s, d)
    ya = _mm_nn("attn_out", o_att, wo, F32)
    x1 = _gate_add("attn_res", x0, ya, g1, nb)
    x2, ffn0_saved = ffn_fwd(0, x1, sh2, sc2, g2)

    sh1b, sc1b, g1b, sh2b, sc2b, g2b = mods[1]
    ops = _ssm_operators(a_re[0], a_im[0], log_dt[0], b_re[0], b_im[0], c_re[0], c_im[0])
    h3 = _norm_mod_fwd("ssm_norm", x2, row(norm_mix[1]), sh1b, sc1b, nb)
    u = _mm_nn("ssm_in", h3, win, BF16)
    ug = _to_groups(u, nb, s)
    yg, xp_re, xp_im = _ssm_core_fwd(ug, ops, nb)
    y_ssm, z_ssm = _ssm_post_fwd(_from_groups(yg, nb, s), u, dskip_f)
    gl = _mm_nn("ssm_glu", z_ssm, wglu, F32)
    gg = _glu_fwd(z_ssm, gl, bglu_f)
    ys2 = _mm_nn("ssm_out", gg, wos, F32)
    x3 = _gate_add("ssm_res", x2, ys2, g1b, nb)
    x4, ffn1_saved = ffn_fwd(1, x3, sh2b, sc2b, g2b)

    dx4, g_norm_out, dsh_f, dsc_f, loss_blk = _norm_mod_bwd(
        "final_norm", None, x4, row(norm_out), sh_f, sc_f, None, loss_target.reshape(t, d), nb)
    loss = lax.psum(loss_blk[0, 0], ("x", "y", "c"))

    def ffn_bwd(i, dxo, xin, sc, gate, saved):
        h, up, act, yf = saved
        dyf, dgate = _gate_bwd(f"ffn{i}_res_bwd", dxo, yf, gate, nb)
        dact = _mm(f"ffn{i}_down_dx", dyf, wd4[i], (t // tm_, half, 1),
                   pl.BlockSpec((tm_, d), lambda a, b, k: (a, 0)), pl.BlockSpec((None, c_up, d), lambda a, b, k: (b, 0, 0)),
                   pl.BlockSpec((None, tm_, c_up), lambda a, b, k: (b, a, 0)),
                   jax.ShapeDtypeStruct((half, t, c_up), BF16), _NT, (tm_, c_up))
        tk = _tile(t, 1024)
        gwd = _mm(f"ffn{i}_down_dw", act, dyf, (half, 1, t // tk),
                  pl.BlockSpec((None, tk, c_up), lambda a, b, k: (a, k, 0)), pl.BlockSpec((tk, d), lambda a, b, k: (k, 0)),
                  pl.BlockSpec((None, c_up, d), lambda a, b, k: (a, 0, 0)),
                  jax.ShapeDtypeStruct((half, c_up, d), BF16), _TN, (c_up, d))
        dup, dcw, dcb = _ffn_act_bwd(f"ffn{i}_act_bwd", up, dact, cw_l[i], cb_l[i], nb, s)
        dh = _mm(f"ffn{i}_up_dx", dup, wup8[i], (t // tm_, 1, N_DEV),
                 pl.BlockSpec((None, tm_, c_up), lambda a, b, k: (k, a, 0)),
                 pl.BlockSpec((None, d, c_up), lambda a, b, k: (k, 0, 0)),
                 pl.BlockSpec((tm_, d), lambda a, b, k: (a, 0)), jax.ShapeDtypeStruct((t, d), F32), _NT, (tm_, d))
        gwup = _mm(f"ffn{i}_up_dw", h, dup, (1, N_DEV, t // tk),
                   pl.BlockSpec((tk, d), lambda a, b, k: (k, 0)), pl.BlockSpec((None, tk, c_up), lambda a, b, k: (b, k, 0)),
                   pl.BlockSpec((None, d, c_up), lambda a, b, k: (b, 0, 0)),
                   jax.ShapeDtypeStruct((N_DEV, d, c_up), BF16), _TN, (d, c_up))
        dxi, g_norm, dsh, dsc = _norm_mod_bwd(f"ffn{i}_norm_bwd", dh, xin, row(norm_ffn[i]), None, sc, dxo, None, nb)
        return dxi, (gwup, gwd.reshape(N_DEV, r_dn, d), dcw, dcb, g_norm, dsh, dsc, dgate)

    dx3, (gwup1, gwd1, dcw1, dcb1, g_nffn1, dsh2b, dsc2b, dg2b) = ffn_bwd(1, dx4, x3, sc2b, g2b, ffn1_saved)

    dys2, dg1b = _gate_bwd("ssm_res_bwd", dx3, ys2, g1b, nb)
    dgg = _mm_nt("ssm_out_dx", dys2, wos, F32)
    gwos = _mm_tn("ssm_out_dw", gg, dys2, BF16)
    dgl, dz1, g_bglu = _glu_bwd(dgg, z_ssm, gl, bglu_f)
    dz2 = _mm_nt("ssm_glu_dx", dgl, wglu, F32)
    gwglu = _mm_tn("ssm_glu_dw", z_ssm, dgl, BF16)
    dy_ssm, du_skip, g_dskip = _ssm_post_bwd(dz1, dz2, y_ssm, u, dskip_f)
    dug, d_ops = _ssm_core_bwd(_to_groups(dy_ssm, nb, s), ug, xp_re, xp_im, ops, nb)
    du = _add_cast(_from_groups(dug, nb, s), du_skip)
    dh3 = _mm_nt("ssm_in_dx", du, win, F32)
    gwin = _mm_tn("ssm_in_dw", h3, du, BF16)
    dx2, g_nmix1, dsh1b, dsc1b = _norm_mod_bwd("ssm_norm_bwd", dh3, x2, row(norm_mix[1]), None, sc1b, dx3, None, nb)
    _, ops_vjp = jax.vjp(_ssm_operators, a_re[0], a_im[0], log_dt[0], b_re[0], b_im[0], c_re[0], c_im[0])
    g_ssm_params = ops_vjp(d_ops)

    dx1, (gwup0, gwd0, dcw0, dcb0, g_nffn0, dsh2, dsc2, dg2) = ffn_bwd(0, dx2, x1, sc2, g2, ffn0_saved)

    dya, dg1 = _gate_bwd("attn_res_bwd", dx1, ya, g1, nb)
    do_att = _mm_nt("attn_out_dx", dya, wo, BF16)
    gwo = _mm_tn("attn_out_dw", o_att, dya, BF16)
    dq, dk, dv = _attn_bwd(qkv, car_att, do_att, nb, s, d)
    dqkv3 = jnp.stack([dq, dk, dv])
    npb = d // LANE
    nqb = cq // LANE
    nblk = 3 * npb
    dh1 = _mm("attn_qkv_dx", dqkv3, wq8, (t // tm_, 1, nblk),
              pl.BlockSpec((None, tm_, LANE), lambda a, b, k: (k // npb, a, k % npb)),
              pl.BlockSpec((None, d, LANE), lambda a, b, k: (k // nqb, 0, k % nqb)),
              pl.BlockSpec((tm_, d), lambda a, b, k: (a, 0)), jax.ShapeDtypeStruct((t, d), F32), _NT, (tm_, d))
    tk = _tile(t, 1024)
    gwq8 = _mm("attn_qkv_dw", h1, dqkv3, (1, nblk, t // tk),
               pl.BlockSpec((tk, d), lambda a, b, k: (k, 0)),
               pl.BlockSpec((None, tk, LANE), lambda a, b, k: (b // npb, k, b % npb)),
               pl.BlockSpec((None, d, LANE), lambda a, b, k: (b // nqb, 0, b % nqb)),
               jax.ShapeDtypeStruct((N_DEV, d, cq), BF16), _TN, (d, LANE))
    dx0, g_nmix0, dsh1, dsc1 = _norm_mod_bwd("attn_norm_bwd", dh1, x0, row(norm_mix[0]), None, sc1, dx1, None, nb)
    grad_x = dx0.reshape(nb, s, d)

    (rq, ro, rin, rglu, ros, rup0, rup1, rd0, rd1) = _exchange(
        "scatter_grads",
        [gwq8, gwo.reshape(N_DEV, d // N_DEV, d), gwin.reshape(N_DEV, d // N_DEV, d),
         gwglu.reshape(N_DEV, d // N_DEV, d), gwos.reshape(N_DEV, d // N_DEV, d), gwup0, gwup1, gwd0, gwd1], False)

    dmod = [jnp.concatenate([a.reshape(nb, d) for a in grp], axis=1) for grp in
            ([dsh1, dsc1, dg1, dsh2, dsc2, dg2], [dsh1b, dsc1b, dg1b, dsh2b, dsc2b, dg2b])]
    dfin = jnp.concatenate([dsh_f.reshape(nb, d), dsc_f.reshape(nb, d)], axis=1)
    dmodfin = jnp.concatenate(dmod + [dfin], axis=1)
    small_shapes = [(2, d), (2, d), (d,), (2, N_DEV * c_up), (g_ssm, SSM_P), (g_ssm, SSM_P), (g_ssm,),
                    (g_ssm, SSM_P, SSM_H), (g_ssm, SSM_P, SSM_H), (g_ssm, SSM_H, SSM_P), (g_ssm, SSM_H, SSM_P),
                    (d,), (d,), (2, N_DEV, 3, c_up)]
    small_partial = _pack([jnp.stack([g_nmix0, g_nmix1]), jnp.stack([g_nffn0, g_nffn1]), g_norm_out,
                           jnp.stack([dcb0, dcb1]), *g_ssm_params, g_dskip, g_bglu, jnp.stack([dcw0, dcw1])])
    dmf8, small8 = _exchange("gather_small_grads", [dmodfin, small_partial], True)
    small_sum = _sum_parts("sum_small_grads", small8)
    (g_norm_mix, g_norm_ffn, g_norm_out_s, g_conv_b, g_a_re, g_a_im, g_log_dt, g_b_re, g_b_im, g_c_re, g_c_im,
     g_dskip_full, g_bglu_full, g_cw_full) = _unpack(small_sum, small_shapes)

    dall = dmf8.reshape(n_seq, 14 * d)
    dmod_loc = jnp.stack([lax.dynamic_slice_in_dim(dall[:, i * 6 * d:(i + 1) * 6 * d], me * cm, cm, axis=1)
                          for i in range(2)])
    dfin_loc = lax.dynamic_slice_in_dim(dall[:, 12 * d:], me * cf, cf, axis=1)
    g_w_mod, g_w_fin, g_bias = _modfin_bwd(c_act.T, dmod_loc, dfin_loc, dall)
    g_b_mod = g_bias[0, :12 * d].reshape(2, 6 * d)
    g_b_fin = g_bias[0, 12 * d:]

    def big(name, parts, w, m, v):
        shp = w.shape
        r2 = lambda a: a.reshape(-1, shp[-1])
        res = _adamw(name, parts.reshape(parts.shape[0], -1, shp[-1]), r2(w), r2(m), r2(v))
        return [a.reshape(shp) for a in res]

    upd = {}
    upd["w_mod"] = big("adamw_w_mod", g_w_mod[None], w_mod, m_w_mod, v_w_mod)
    upd["w_fin"] = big("adamw_w_fin", g_w_fin[None], w_fin, m_w_fin, v_w_fin)
    upd["w_qkv"] = big("adamw_w_qkv", rq, w_qkv, m_w_qkv, v_w_qkv)
    upd["w_o_attn"] = big("adamw_w_o_attn", ro, w_o_attn, m_w_o_attn, v_w_o_attn)
    upd["w_in_ssm"] = big("adamw_w_in_ssm", rin, w_in_ssm, m_w_in_ssm, v_w_in_ssm)
    upd["w_glu"] = big("adamw_w_glu", rglu, w_glu, m_w_glu, v_w_glu)
    upd["w_o_ssm"] = big("adamw_w_o_ssm", ros, w_o_ssm, m_w_o_ssm, v_w_o_ssm)
    up_l = [big(f"adamw_w_up{i}", r, w_up[i], m_w_up[i], v_w_up[i]) for i, r in enumerate((rup0, rup1))]
    upd["w_up"] = [jnp.stack([up_l[0][j], up_l[1][j]]) for j in range(4)]
    dn_l = [big(f"adamw_w_down{i}", r, w_down[i], m_w_down[i], v_w_down[i]) for i, r in enumerate((rd0, rd1))]
    upd["w_down"] = [jnp.stack([dn_l[0][j], dn_l[1][j]]) for j in range(4)]

    g_dskip_loc = lax.dynamic_slice_in_dim(g_dskip_full.reshape(1, d), me * (d // N_DEV), d // N_DEV, axis=1)
    g_bglu_loc = lax.dynamic_slice_in_dim(g_bglu_full.reshape(1, d), me * (d // N_DEV), d // N_DEV, axis=1)
    g_cw_loc = lax.dynamic_slice_in_dim(g_cw_full, me, 1, axis=1).reshape(2, 3, c_up)
    small_names = ["norm_mix", "norm_ffn", "b_mod", "a_re", "a_im", "log_dt", "b_re", "b_im", "c_re", "c_im",
                   "d_skip", "b_glu", "conv_w", "conv_b", "norm_out", "b_fin"]
    small_g = [g_norm_mix, g_norm_ffn, g_b_mod, g_a_re[None], g_a_im[None], g_log_dt[None], g_b_re[None], g_b_im[None],
               g_c_re[None], g_c_im[None], g_dskip_loc, g_bglu_loc, g_cw_loc, g_conv_b, g_norm_out_s, g_b_fin]
    small_w = [norm_mix, norm_ffn, b_mod, a_re, a_im, log_dt, b_re, b_im, c_re, c_im, d_skip, b_glu, conv_w, conv_b,
               norm_out, b_fin]
    small_m = [m_norm_mix, m_norm_ffn, m_b_mod, m_a_re, m_a_im, m_log_dt, m_b_re, m_b_im, m_c_re, m_c_im, m_d_skip,
               m_b_glu, m_conv_w, m_conv_b, m_norm_out, m_b_fin]
    small_v = [v_norm_mix, v_norm_ffn, v_b_mod, v_a_re, v_a_im, v_log_dt, v_b_re, v_b_im, v_c_re, v_c_im, v_d_skip,
               v_b_glu, v_conv_w, v_conv_b, v_norm_out, v_b_fin]
    shapes = [w.shape for w in small_w]
    res = _adamw("adamw_small", _pack(small_g)[None], _pack(small_w), _pack(small_m), _pack(small_v))
    res = [_unpack(r, shapes) for r in res]
    for j, nm in enumerate(small_names):
        upd[nm] = [res[k][j] for k in range(4)]

    order = ["norm_mix", "norm_ffn", "w_mod", "b_mod", "w_qkv", "w_o_attn", "w_in_ssm", "a_re", "a_im", "log_dt",
             "b_re", "b_im", "c_re", "c_im", "d_skip", "w_glu", "b_glu", "w_o_ssm", "w_up", "conv_w", "conv_b",
             "w_down", "norm_out", "w_fin", "b_fin"]
    outs = [loss, grad_x]
    for k in range(4):
        outs += [upd[nm][k] for nm in order]
    return tuple(outs)
```

```python
import functools
import math

import jax
import jax.numpy as jnp
from jax import lax
from jax.experimental import pallas as pl
from jax.experimental.pallas import tpu as pltpu

F32 = jnp.float32
BF16 = jnp.bfloat16
MESH = pl.DeviceIdType.MESH

N_DEV = 8
HEAD_DIM = 64
ATT_BLK = 128
ATT_BQ = 256
SSM_H = 16
SSM_P = 64
SSM_L = 16
EPS = 1e-6
ADAM_LR, ADAM_B1, ADAM_B2, ADAM_EPS, ADAM_WD, ADAM_STEP = 0.001, 0.9, 0.999, 1e-08, 0.01, 10
V7X_VMEM_LIMIT = 56 * 1024 * 1024
LANE = 128

_NN = (((1,), (0,)), ((), ()))
_NT = (((1,), (1,)), ((), ()))
_TN = (((0,), (0,)), ((), ()))


def _cp(sem):
    return pltpu.CompilerParams(dimension_semantics=sem, vmem_limit_bytes=V7X_VMEM_LIMIT)


def _tile(n, pref):
    if n <= pref:
        return n
    t = pref - pref % 16
    while t >= 16:
        if n % t == 0:
            return t
        t -= 16
    return n


def _mm(name, a, b, grid, a_spec, b_spec, out_spec, out_shape, dims, acc_shape):
    nk = grid[-1]
    kax = len(grid) - 1

    def body(a_ref, b_ref, o_ref, acc_ref):
        k = pl.program_id(kax)

        @pl.when(k == 0)
        def _():
            acc_ref[...] = jnp.zeros(acc_shape, F32)

        acc_ref[...] += lax.dot_general(a_ref[...].astype(BF16), b_ref[...].astype(BF16), dims,
                                        preferred_element_type=F32)

        @pl.when(k == nk - 1)
        def _():
            o_ref[...] = acc_ref[...].astype(o_ref.dtype)

    return pl.pallas_call(
        body, name=name, grid=grid, in_specs=[a_spec, b_spec], out_specs=out_spec, out_shape=out_shape,
        scratch_shapes=[pltpu.VMEM(acc_shape, F32)],
        compiler_params=_cp(("parallel",) * kax + ("arbitrary",)))(a, b)


def _mm_terms(name, terms, grid, out_spec, out_shape):
    n = len(terms)
    dims = [t[4] for t in terms]

    def body(*refs):
        o_ref = refs[2 * n]
        acc = None
        for i in range(n):
            d = lax.dot_general(refs[2 * i][...].astype(BF16), refs[2 * i + 1][...].astype(BF16), dims[i],
                                preferred_element_type=F32)
            acc = d if acc is None else acc + d
        o_ref[...] = acc.astype(o_ref.dtype)

    ops, specs = [], []
    for a, a_spec, b, b_spec, _ in terms:
        ops += [a, b]
        specs += [a_spec, b_spec]
    return pl.pallas_call(body, name=name, grid=grid, in_specs=specs, out_specs=out_spec, out_shape=out_shape,
                          compiler_params=_cp(("parallel",) * len(grid)))(*ops)


def _mm_nn(name, a, w, out_dtype):
    m, k = a.shape
    n = w.shape[1]
    tm, tn, tk = _tile(m, 512), _tile(n, 1024), _tile(k, 1024)
    return _mm(name, a, w, (m // tm, n // tn, k // tk),
               pl.BlockSpec((tm, tk), lambda i, j, kk: (i, kk)), pl.BlockSpec((tk, tn), lambda i, j, kk: (kk, j)),
               pl.BlockSpec((tm, tn), lambda i, j, kk: (i, j)), jax.ShapeDtypeStruct((m, n), out_dtype), _NN, (tm, tn))


def _mm_nt(name, a, w, out_dtype):
    m, n = a.shape
    k = w.shape[0]
    tm, tko, tn = _tile(m, 512), _tile(k, 1024), _tile(n, 1024)
    return _mm(name, a, w, (m // tm, k // tko, n // tn),
               pl.BlockSpec((tm, tn), lambda i, j, kk: (i, kk)), pl.BlockSpec((tko, tn), lambda i, j, kk: (j, kk)),
               pl.BlockSpec((tm, tko), lambda i, j, kk: (i, j)), jax.ShapeDtypeStruct((m, k), out_dtype), _NT, (tm, tko))


def _mm_tn(name, a, b, out_dtype):
    t, m = a.shape
    n = b.shape[1]
    tm, tn, tk = _tile(m, 512), _tile(n, 1024), _tile(t, 1024)
    return _mm(name, a, b, (m // tm, n // tn, t // tk),
               pl.BlockSpec((tk, tm), lambda i, j, kk: (kk, i)), pl.BlockSpec((tk, tn), lambda i, j, kk: (kk, j)),
               pl.BlockSpec((tm, tn), lambda i, j, kk: (i, j)), jax.ShapeDtypeStruct((m, n), out_dtype), _TN, (tm, tn))


def _norm_mod_fwd(name, x, g, shift, scale, nb):
    t, d = x.shape
    s = t // nb
    tr = _tile(s, 512)
    nt = s // tr

    def body(x_ref, g_ref, sh_ref, sc_ref, h_ref):
        xv = x_ref[...]
        r = lax.rsqrt(jnp.mean(xv * xv, axis=-1, keepdims=True) + EPS)
        y = xv * r * g_ref[...]
        h_ref[...] = (y * (1.0 + sc_ref[...]) + sh_ref[...]).astype(h_ref.dtype)

    row = pl.BlockSpec((tr, d), lambda b, i: (b * nt + i, 0))
    vec = pl.BlockSpec((None, 1, d), lambda b, i: (b, 0, 0))
    return pl.pallas_call(body, name=name, grid=(nb, nt),
                          in_specs=[row, pl.BlockSpec((1, d), lambda b, i: (0, 0)), vec, vec],
                          out_specs=row, out_shape=jax.ShapeDtypeStruct((t, d), BF16),
                          compiler_params=_cp(("parallel", "parallel")))(x, g, shift, scale)


def _norm_mod_bwd(name, dh, x, g, shift, scale, dres, target, nb):
    t, d = x.shape
    s = t // nb
    tr = _tile(s, 256)
    nt = s // tr
    final = target is not None

    def body(*refs):
        if final:
            x_ref, g_ref, sh_ref, sc_ref, tg_ref, dx_ref, dg_ref, dsh_ref, dsc_ref, loss_ref = refs
        else:
            dh_ref, x_ref, g_ref, sc_ref, dres_ref, dx_ref, dg_ref, dsh_ref, dsc_ref = refs
        b, i = pl.program_id(0), pl.program_id(1)
        xv = x_ref[...]
        gv = g_ref[...]
        r = lax.rsqrt(jnp.mean(xv * xv, axis=-1, keepdims=True) + EPS)
        nrm = xv * r
        y = nrm * gv
        one_sc = 1.0 + sc_ref[...]
        if final:
            err = y * one_sc + sh_ref[...] - tg_ref[...]
            dhv = err * (1.0 / d)
        else:
            dhv = dh_ref[...].astype(F32)
        dy = dhv * one_sc
        dn = dy * gv
        dxv = r * (dn - nrm * jnp.mean(dn * nrm, axis=-1, keepdims=True))
        if final:
            dx_ref[...] = dxv
        else:
            dx_ref[...] = dres_ref[...] + dxv

        @pl.when(i == 0)
        def _():
            dsh_ref[...] = jnp.zeros_like(dsh_ref)
            dsc_ref[...] = jnp.zeros_like(dsc_ref)

        @pl.when((i == 0) & (b == 0))
        def _():
            dg_ref[...] = jnp.zeros_like(dg_ref)
            if final:
                loss_ref[...] = jnp.zeros_like(loss_ref)

        dsh_ref[...] += jnp.sum(dhv, axis=0, keepdims=True)
        dsc_ref[...] += jnp.sum(dhv * y, axis=0, keepdims=True)
        dg_ref[...] += jnp.sum(dy * nrm, axis=0, keepdims=True)
        if final:
            loss_ref[...] += (0.5 / d) * jnp.sum(err * err)

    row = pl.BlockSpec((tr, d), lambda b, i: (b * nt + i, 0))
    vec = pl.BlockSpec((None, 1, d), lambda b, i: (b, 0, 0))
    gsp = pl.BlockSpec((1, d), lambda b, i: (0, 0))
    out_specs = [row, gsp, vec, vec]
    out_shape = [jax.ShapeDtypeStruct((t, d), F32), jax.ShapeDtypeStruct((1, d), F32),
                 jax.ShapeDtypeStruct((nb, 1, d), F32), jax.ShapeDtypeStruct((nb, 1, d), F32)]
    if final:
        ins, in_specs = [x, g, shift, scale, target], [row, gsp, vec, vec, row]
        out_specs.append(pl.BlockSpec((8, LANE), lambda b, i: (0, 0)))
        out_shape.append(jax.ShapeDtypeStruct((8, LANE), F32))
    else:
        ins, in_specs = [dh, x, g, scale, dres], [row, row, gsp, vec, row]
    return pl.pallas_call(body, name=name, grid=(nb, nt), in_specs=in_specs, out_specs=out_specs,
                          out_shape=out_shape, compiler_params=_cp(("arbitrary", "arbitrary")))(*ins)


def _gate_add(name, x, y, gate, nb):
    t, d = x.shape
    s = t // nb
    tr = _tile(s, 512)
    nt = s // tr

    def body(x_ref, y_ref, g_ref, o_ref):
        o_ref[...] = x_ref[...] + g_ref[...] * y_ref[...]

    row = pl.BlockSpec((tr, d), lambda b, i: (b * nt + i, 0))
    vec = pl.BlockSpec((None, 1, d), lambda b, i: (b, 0, 0))
    return pl.pallas_call(body, name=name, grid=(nb, nt), in_specs=[row, row, vec], out_specs=row,
                          out_shape=jax.ShapeDtypeStruct((t, d), F32),
                          compiler_params=_cp(("parallel", "parallel")))(x, y, gate)


def _gate_bwd(name, dx, y, gate, nb):
    t, d = dx.shape
    s = t // nb
    tr = _tile(s, 512)
    nt = s // tr

    def body(dx_ref, y_ref, g_ref, dy_ref, dg_ref):
        dxv = dx_ref[...]
        dy_ref[...] = (g_ref[...] * dxv).astype(dy_ref.dtype)

        @pl.when(pl.program_id(1) == 0)
        def _():
            dg_ref[...] = jnp.zeros_like(dg_ref)

        dg_ref[...] += jnp.sum(dxv * y_ref[...], axis=0, keepdims=True)

    row = pl.BlockSpec((tr, d), lambda b, i: (b * nt + i, 0))
    vec = pl.BlockSpec((None, 1, d), lambda b, i: (b, 0, 0))
    return pl.pallas_call(body, name=name, grid=(nb, nt), in_specs=[row, row, vec], out_specs=[row, vec],
                          out_shape=[jax.ShapeDtypeStruct((t, d), BF16), jax.ShapeDtypeStruct((nb, 1, d), F32)],
                          compiler_params=_cp(("parallel", "arbitrary")))(dx, y, gate)


def _log_sigmoid(z):
    return jnp.minimum(z, 0.0) - jnp.log(1.0 + jnp.exp(-jnp.abs(z)))


def _split_dot(v, tri):
    hi = v.astype(BF16)
    lo = (v - hi.astype(F32)).astype(BF16)
    return (jnp.dot(hi, tri, preferred_element_type=F32) + jnp.dot(lo, tri, preferred_element_type=F32))


def _grid_ends(grid):
    ids = [pl.program_id(a) for a in range(len(grid))]
    first = functools.reduce(lambda u, w: u & w, [i == 0 for i in ids])
    last = functools.reduce(lambda u, w: u & w, [i == n - 1 for i, n in zip(ids, grid)])
    return first, last


def _attn_fwd(qkv, nb, s, d, ex):
    t = nb * s
    npair = d // LANE
    bk = ATT_BLK
    bq = min(ATT_BQ, s)
    nq = s // bq
    kpq = bq // bk
    nheads = LANE // HEAD_DIM
    scale = HEAD_DIM ** -0.5
    grid = (nb, npair, nq)
    assert s // bk <= HEAD_DIM, "one carry lane per key block and head"
    assert bk == LANE, "the running sums are kept one 128-lane tile wide"

    def body(*refs):
        q_ref, k_ref, v_ref = refs[:3]
        ex_ins = refs[3:3 + ex.n]
        o_ref, car_ref = refs[3 + ex.n:5 + ex.n]
        ex_outs = refs[5 + ex.n:5 + 2 * ex.n]
        acc_s, run_s = refs[5 + 2 * ex.n:7 + 2 * ex.n]
        sems = refs[7 + 2 * ex.n:]
        first, last = _grid_ends(grid)

        @pl.when(first)
        def _():
            ex.start(ex_ins, ex_outs, sems)

        qi = pl.program_id(2)
        q = q_ref[...]
        lane = lax.broadcasted_iota(jnp.int32, (1, LANE), 1)
        row = lax.broadcasted_iota(jnp.int32, (bq, bk), 0)
        col = lax.broadcasted_iota(jnp.int32, (bq, bk), 1)
        trow = lax.broadcasted_iota(jnp.int32, (bk, bk), 0)
        tcol = lax.broadcasted_iota(jnp.int32, (bk, bk), 1)
        tri = (trow > tcol).astype(BF16)
        hms = [(lane // HEAD_DIM) == hh for hh in range(nheads)]
        qhs = [jnp.where(hm, q, jnp.zeros_like(q)) for hm in hms]
        car_ref[...] = jnp.zeros((bq, LANE), F32)
        acc_s[...] = jnp.zeros_like(acc_s)
        run_s[...] = jnp.zeros_like(run_s)
        nkb = (qi + 1) * kpq

        def step(jj, carry):
            j = nkb - 1 - jj
            k0 = pl.multiple_of(j * bk, bk)
            kj = k_ref[pl.ds(k0, bk), :]
            vj = v_ref[pl.ds(k0, bk), :]
            mask = (k0 + col) < (qi * bq + row)
            for hh in range(nheads):
                z = lax.dot_general(qhs[hh], kj, _NT, preferred_element_type=F32) * scale
                lb = _log_sigmoid(z)
                l1 = jnp.where(mask, lb - z, 0.0)
                run = run_s[hh]
                suf = _split_dot(l1, tri) + run
                w = jnp.where(mask, jnp.exp(lb + suf), 0.0)
                acc_s[hh] += jnp.dot(w.astype(BF16), vj, preferred_element_type=F32)
                car_ref[...] = jnp.where(lane == hh * HEAD_DIM + j, run, car_ref[...])
                run_s[hh] = run + jnp.sum(l1, axis=1, keepdims=True)
            return carry

        lax.fori_loop(0, nkb, step, 0)
        out = acc_s[0]
        for hh in range(1, nheads):
            out = jnp.where(hms[hh], acc_s[hh], out)
        o_ref[...] = out.astype(o_ref.dtype)

        @pl.when(last)
        def _():
            ex.wait(ex_ins, ex_outs, sems)

    qspec = pl.BlockSpec((bq, LANE), lambda b, p, i: (b * nq + i, p))
    res = pl.pallas_call(
        body, name="attn_fwd", grid=grid,
        in_specs=[qspec,
                  pl.BlockSpec((s, LANE), lambda b, p, i: (b, npair + p)),
                  pl.BlockSpec((s, LANE), lambda b, p, i: (b, 2 * npair + p))] + ex.specs,
        out_specs=[qspec, qspec] + ex.specs,
        out_shape=[jax.ShapeDtypeStruct((t, d), BF16), jax.ShapeDtypeStruct((t, d), F32)] + ex.out_shape,
        scratch_shapes=[pltpu.VMEM((nheads, bq, LANE), F32), pltpu.VMEM((nheads, bq, LANE), F32)] + ex.scratch,
        compiler_params=_cp(("arbitrary", "arbitrary", "arbitrary")))(qkv, qkv, qkv, *ex.arrs)
    return res[0], res[1], list(res[2:])


def _attn_bwd(qkv, car, do, nb, s, d, ex):
    t = nb * s
    npair = d // LANE
    bk = ATT_BLK
    bq = min(ATT_BQ, s)
    nq = s // bq
    kpq = bq // bk
    nheads = LANE // HEAD_DIM
    scale = HEAD_DIM ** -0.5
    grid = (nb, npair, nq)

    def body(*refs):
        q_ref, k_ref, v_ref, car_ref, do_ref = refs[:5]
        ex_ins = refs[5:5 + ex.n]
        dq_ref, dk_ref, dv_ref = refs[5 + ex.n:8 + ex.n]
        ex_outs = refs[8 + ex.n:8 + 2 * ex.n]
        dk_acc, dv_acc, dq_s, rune_s = refs[8 + 2 * ex.n:12 + 2 * ex.n]
        sems = refs[12 + 2 * ex.n:]
        first, last = _grid_ends(grid)

        @pl.when(first)
        def _():
            ex.start(ex_ins, ex_outs, sems)

        qi = pl.program_id(2)

        @pl.when(qi == 0)
        def _():
            dk_acc[...] = jnp.zeros_like(dk_acc)
            dv_acc[...] = jnp.zeros_like(dv_acc)

        q = q_ref[...]
        dov = do_ref[...]
        lane = lax.broadcasted_iota(jnp.int32, (1, LANE), 1)
        row = lax.broadcasted_iota(jnp.int32, (bq, bk), 0)
        col = lax.broadcasted_iota(jnp.int32, (bq, bk), 1)
        trow = lax.broadcasted_iota(jnp.int32, (bk, bk), 0)
        tcol = lax.broadcasted_iota(jnp.int32, (bk, bk), 1)
        tri_suf = (trow > tcol).astype(BF16)
        tri_pre = (trow < tcol).astype(BF16)
        hms = [(lane // HEAD_DIM) == hh for hh in range(nheads)]
        qhs = [jnp.where(hm, q, jnp.zeros_like(q)) for hm in hms]
        dohs = [jnp.where(hm, dov, jnp.zeros_like(dov)) for hm in hms]
        dq_s[...] = jnp.zeros_like(dq_s)
        rune_s[...] = jnp.zeros_like(rune_s)

        def step(j, carry):
            k0 = pl.multiple_of(j * bk, bk)
            kj = k_ref[pl.ds(k0, bk), :]
            vj = v_ref[pl.ds(k0, bk), :]
            mask = (k0 + col) < (qi * bq + row)
            car = car_ref[...]
            for hh in range(nheads):
                z = lax.dot_general(qhs[hh], kj, _NT, preferred_element_type=F32) * scale
                lb = _log_sigmoid(z)
                l1u = lb - z
                l1 = jnp.where(mask, l1u, 0.0)
                run = jnp.sum(jnp.where(lane == hh * HEAD_DIM + j, car, 0.0), axis=1, keepdims=True)
                suf = _split_dot(l1, tri_suf) + run
                a = jnp.where(mask, jnp.exp(lb + suf), 0.0)
                da = lax.dot_general(dohs[hh], vj, _NT, preferred_element_type=F32)
                e = da * a
                run_e = rune_s[hh]
                pre_e = _split_dot(e, tri_pre) + run_e
                dz = jnp.where(mask, e * jnp.exp(l1u) - pre_e * jnp.exp(lb), 0.0) * scale
                dzb = dz.astype(BF16)
                dq_s[hh] += jnp.dot(dzb, kj, preferred_element_type=F32)
                dkh = lax.dot_general(dzb, qhs[hh], _TN, preferred_element_type=F32)
                dvh = lax.dot_general(a.astype(BF16), dohs[hh], _TN, preferred_element_type=F32)
                dk_blk = dkh if hh == 0 else dk_blk + dkh
                dv_blk = dvh if hh == 0 else dv_blk + dvh
                rune_s[hh] = run_e + jnp.sum(e, axis=1, keepdims=True)
            dk_acc[pl.ds(k0, bk), :] += dk_blk
            dv_acc[pl.ds(k0, bk), :] += dv_blk
            return carry

        lax.fori_loop(0, (qi + 1) * kpq, step, 0)
        dq_out = dq_s[0]
        for hh in range(1, nheads):
            dq_out = jnp.where(hms[hh], dq_s[hh], dq_out)
        dq_ref[...] = dq_out.astype(dq_ref.dtype)

        @pl.when(qi == nq - 1)
        def _():
            dk_ref[...] = dk_acc[...].astype(dk_ref.dtype)
            dv_ref[...] = dv_acc[...].astype(dv_ref.dtype)

        @pl.when(last)
        def _():
            ex.wait(ex_ins, ex_outs, sems)

    qspec = pl.BlockSpec((bq, LANE), lambda b, p, i: (b * nq + i, p))
    kvout = pl.BlockSpec((s, LANE), lambda b, p, i: (b, p))
    sds = jax.ShapeDtypeStruct((t, d), BF16)
    res = pl.pallas_call(
        body, name="attn_bwd", grid=grid,
        in_specs=[qspec,
                  pl.BlockSpec((s, LANE), lambda b, p, i: (b, npair + p)),
                  pl.BlockSpec((s, LANE), lambda b, p, i: (b, 2 * npair + p)),
                  qspec, qspec] + ex.specs,
        out_specs=[qspec, kvout, kvout] + ex.specs, out_shape=[sds, sds, sds] + ex.out_shape,
        scratch_shapes=[pltpu.VMEM((s, LANE), F32), pltpu.VMEM((s, LANE), F32),
                        pltpu.VMEM((nheads, bq, LANE), F32), pltpu.VMEM((nheads, bq, LANE), F32)] + ex.scratch,
        compiler_params=_cp(("arbitrary", "arbitrary", "arbitrary")))(qkv, qkv, qkv, car, do, *ex.arrs)
    return res[0], res[1], res[2], list(res[3:])


def _conv3(u_ref, w, bias, c, r0, rc):
    x = u_ref[pl.ds(r0, rc), :].astype(F32)
    p0 = pl.multiple_of(jnp.maximum(r0 - 16, 0), 16)
    prev = u_ref[pl.ds(p0, 16), :].astype(F32)
    prev = jnp.where(c > 0, prev, 0.0)
    row = lax.broadcasted_iota(jnp.int32, (rc, 1), 0)
    s1 = jnp.where(row == 0, prev[15:16, :], pltpu.roll(x, 1, 0))
    s2 = jnp.where(row == 0, prev[14:15, :], jnp.where(row == 1, prev[15:16, :], pltpu.roll(x, 2, 0)))
    cv = w[2:3, :] * x + w[1:2, :] * s1 + w[0:1, :] * s2 + bias
    return cv, x, s1, s2


def _sigmoid(x):
    return 1.0 / (1.0 + jnp.exp(-x))


def _ffn_act_fwd(name, up8, cw8, cb8, nb, s):
    _, t, c_w = up8.shape
    rc = _tile(s, 256)
    nch = s // rc
    half = N_DEV // 2

    def body(ug_ref, uv_ref, wg_ref, wv_ref, bg_ref, bv_ref, act_ref):
        wg, wv, bg, bv = wg_ref[...], wv_ref[...], bg_ref[...], bv_ref[...]

        def chunk(c, carry):
            r0 = pl.multiple_of(c * rc, rc)
            cg = _conv3(ug_ref, wg, bg, c, r0, rc)[0]
            cv = _conv3(uv_ref, wv, bv, c, r0, rc)[0]
            act_ref[pl.ds(r0, rc), :] = (cg * _sigmoid(cg) * cv).astype(act_ref.dtype)
            return carry

        lax.fori_loop(0, nch, chunk, 0)

    def slab(off):
        return pl.BlockSpec((None, s, c_w), lambda k, b: (k + off, b, 0))

    def par(rows, off):
        return pl.BlockSpec((None, rows, c_w), lambda k, b: (k + off, 0, 0))

    return pl.pallas_call(
        body, name=name, grid=(half, nb),
        in_specs=[slab(0), slab(half), par(3, 0), par(3, half), par(1, 0), par(1, half)],
        out_specs=pl.BlockSpec((None, s, c_w), lambda k, b: (k, b, 0)),
        out_shape=jax.ShapeDtypeStruct((half, t, c_w), BF16),
        compiler_params=_cp(("parallel", "parallel")))(up8, up8, cw8, cw8, cb8, cb8)


def _ffn_act_bwd(name, up8, dact4, cw8, cb8, nb, s):
    _, t, c_w = up8.shape
    rc = _tile(s, 256)
    nch = s // rc
    half = N_DEV // 2

    def body(u_ref, da_ref, w_ref, b_ref, dup_ref, dcw_ref, dcb_ref):
        w2, b2 = w_ref[...], b_ref[...]
        row = lax.broadcasted_iota(jnp.int32, (rc, 1), 0)

        @pl.when(pl.program_id(1) == 0)
        def _():
            dcw_ref[...] = jnp.zeros_like(dcw_ref)
            dcb_ref[...] = jnp.zeros_like(dcb_ref)

        def chunk(i, carry):
            c = nch - 1 - i
            r0 = pl.multiple_of(c * rc, rc)
            convs = [_conv3(u_ref.at[h], w2[h], b2[h], c, r0, rc) for h in range(2)]
            gt, vl = convs[0][0], convs[1][0]
            da = da_ref[pl.ds(r0, rc), :].astype(F32)
            sg = _sigmoid(gt)
            dcvs = [da * vl * sg * (1.0 + gt * (1.0 - sg)), da * gt * sg]
            out = []
            for h in range(2):
                n0, n1, a0, a1, a2, ab = carry[6 * h:6 * h + 6]
                dcv, (_, x, s1, s2), w = dcvs[h], convs[h], w2[h]
                t1 = jnp.where(row == rc - 1, n0, pltpu.roll(dcv, rc - 1, 0))
                t2 = jnp.where(row == rc - 2, n0, jnp.where(row == rc - 1, n1, pltpu.roll(dcv, rc - 2, 0)))
                dup = w[2:3, :] * dcv + w[1:2, :] * t1 + w[0:1, :] * t2
                dup_ref[h, pl.ds(r0, rc), :] = dup.astype(dup_ref.dtype)
                out += [dcv[0:1, :], dcv[1:2, :],
                        a0 + jnp.sum(dcv * s2, axis=0, keepdims=True), a1 + jnp.sum(dcv * s1, axis=0, keepdims=True),
                        a2 + jnp.sum(dcv * x, axis=0, keepdims=True), ab + jnp.sum(dcv, axis=0, keepdims=True)]
            return tuple(out)

        z = jnp.zeros((1, c_w), F32)
        fin = lax.fori_loop(0, nch, chunk, (z,) * 12)
        for h in range(2):
            _, _, a0, a1, a2, ab = fin[6 * h:6 * h + 6]
            dcw_ref[h, 0:1, :] += a0
            dcw_ref[h, 1:2, :] += a1
            dcw_ref[h, 2:3, :] += a2
            dcb_ref[h] += ab

    def pair(rows, per_seq):
        return pl.BlockSpec((2, None, rows, c_w), (lambda k, b: (0, k, b, 0)) if per_seq else (lambda k, b: (0, k, 0, 0)))

    four = lambda a: a.reshape((2, half) + a.shape[1:])
    dup, dcw, dcb = pl.pallas_call(
        body, name=name, grid=(half, nb),
        in_specs=[pair(s, True), pl.BlockSpec((None, s, c_w), lambda k, b: (k, b, 0)), pair(3, False), pair(1, False)],
        out_specs=[pair(s, True), pair(3, False), pair(1, False)],
        out_shape=[jax.ShapeDtypeStruct((2, half, t, c_w), BF16), jax.ShapeDtypeStruct((2, half, 3, c_w), F32),
                   jax.ShapeDtypeStruct((2, half, 1, c_w), F32)],
        compiler_params=_cp(("parallel", "arbitrary")))(four(up8), dact4, four(cw8), four(cb8))
    return dup.reshape(N_DEV, t, c_w), dcw.reshape(N_DEV, 3, c_w), dcb.reshape(N_DEV, 1, c_w)


_GELU_C0 = math.sqrt(2.0 / math.pi)
_GELU_C1 = 0.044715


def _rowwise(name, body, ins, in_kinds, out_kinds, t, d, tr_pref=512):
    tr = _tile(t, tr_pref)
    row = pl.BlockSpec((tr, d), lambda i: (i, 0))
    vec = pl.BlockSpec((1, d), lambda i: (0, 0))
    in_specs = [row if k == "row" else vec for k in in_kinds]
    out_specs = [row if k[0] == "row" else vec for k in out_kinds]
    out_shape = [jax.ShapeDtypeStruct((t, d) if k[0] == "row" else (1, d), k[1]) for k in out_kinds]
    has_acc = any(k[0] == "acc" for k in out_kinds)
    return pl.pallas_call(body, name=name, grid=(t // tr,), in_specs=in_specs, out_specs=out_specs,
                          out_shape=out_shape,
                          compiler_params=_cp(("arbitrary",) if has_acc else ("parallel",)))(*ins)


def _ssm_post_fwd(ys, u, dskip):
    t, d = ys.shape

    def body(ys_ref, u_ref, ds_ref, y_ref, z_ref):
        y = ys_ref[...].astype(F32) + ds_ref[...] * u_ref[...].astype(F32)
        y_ref[...] = y
        th = jnp.tanh(_GELU_C0 * (y + _GELU_C1 * y * y * y))
        z_ref[...] = (0.5 * y * (1.0 + th)).astype(z_ref.dtype)

    return _rowwise("ssm_post_fwd", body, [ys, u, dskip], ["row", "row", "vec"],
                    [("row", F32), ("row", BF16)], t, d)


def _glu_fwd(z, gl, bglu):
    t, d = z.shape

    def body(z_ref, gl_ref, b_ref, o_ref):
        o_ref[...] = (z_ref[...].astype(F32) * _sigmoid(gl_ref[...] + b_ref[...])).astype(o_ref.dtype)

    return _rowwise("glu_fwd", body, [z, gl, bglu], ["row", "row", "vec"], [("row", BF16)], t, d)[0]


def _glu_bwd(dgg, z, gl, bglu):
    t, d = z.shape

    def body(dg_ref, z_ref, gl_ref, b_ref, dgl_ref, dz_ref, db_ref):
        sg = _sigmoid(gl_ref[...] + b_ref[...])
        dg = dg_ref[...]
        dgl = dg * z_ref[...].astype(F32) * sg * (1.0 - sg)
        dgl_ref[...] = dgl.astype(dgl_ref.dtype)
        dz_ref[...] = dg * sg

        @pl.when(pl.program_id(0) == 0)
        def _():
            db_ref[...] = jnp.zeros_like(db_ref)

        db_ref[...] += jnp.sum(dgl, axis=0, keepdims=True)

    return _rowwise("glu_bwd", body, [dgg, z, gl, bglu], ["row", "row", "row", "vec"],
                    [("row", BF16), ("row", F32), ("acc", F32)], t, d)


def _ssm_post_bwd(dz1, dz2, y, u, dskip):
    t, d = y.shape

    def body(a_ref, b_ref, y_ref, u_ref, ds_ref, dy_ref, du_ref, dd_ref):
        yv = y_ref[...]
        inner = _GELU_C0 * (yv + _GELU_C1 * yv * yv * yv)
        th = jnp.tanh(inner)
        dgelu = 0.5 * (1.0 + th) + 0.5 * yv * (1.0 - th * th) * _GELU_C0 * (1.0 + 3.0 * _GELU_C1 * yv * yv)
        dy = (a_ref[...] + b_ref[...]) * dgelu
        dy_ref[...] = dy.astype(dy_ref.dtype)
        du_ref[...] = dy * ds_ref[...]

        @pl.when(pl.program_id(0) == 0)
        def _():
            dd_ref[...] = jnp.zeros_like(dd_ref)

        dd_ref[...] += jnp.sum(dy * u_ref[...].astype(F32), axis=0, keepdims=True)

    return _rowwise("ssm_post_bwd", body, [dz1, dz2, y, u, dskip], ["row", "row", "row", "row", "vec"],
                    [("row", BF16), ("row", F32), ("acc", F32)], t, d)


def _add_cast(a, b):
    t, d = a.shape

    def body(a_ref, b_ref, o_ref):
        o_ref[...] = (a_ref[...].astype(F32) + b_ref[...].astype(F32)).astype(o_ref.dtype)

    return _rowwise("add_cast", body, [a, b], ["row", "row"], [("row", BF16)], t, d)[0]


def _ssm_scan(e_re, e_im, lam_re, lam_im, nb):
    r, n = e_re.shape
    nc = r // nb
    cb = _tile(n, 512)

    def body(er_ref, ei_ref, lr_ref, li_ref, xr_ref, xi_ref):
        lr, li = lr_ref[...], li_ref[...]
        rid = lax.broadcasted_iota(jnp.int32, (8, 1), 0)
        for b in range(nb):
            def tile(i, carry, b=b):
                xr, xi = carry
                r0 = pl.multiple_of(b * nc + i * 8, 8)
                er, ei = er_ref[pl.ds(r0, 8), :], ei_ref[pl.ds(r0, 8), :]
                outr, outi = jnp.zeros((8, cb), F32), jnp.zeros((8, cb), F32)
                for j in range(8):
                    outr = jnp.where(rid == j, xr, outr)
                    outi = jnp.where(rid == j, xi, outi)
                    xr, xi = lr * xr - li * xi + er[j:j + 1, :], li * xr + lr * xi + ei[j:j + 1, :]
                xr_ref[pl.ds(r0, 8), :] = outr
                xi_ref[pl.ds(r0, 8), :] = outi
                return xr, xi

            z = jnp.zeros((1, cb), F32)
            lax.fori_loop(0, nc // 8, tile, (z, z))

    mat = pl.BlockSpec((r, cb), lambda j: (0, j))
    vec = pl.BlockSpec((1, cb), lambda j: (0, j))
    sds = jax.ShapeDtypeStruct((r, n), F32)
    return pl.pallas_call(body, name="ssm_scan", grid=(n // cb,), in_specs=[mat, mat, vec, vec],
                          out_specs=[mat, mat], out_shape=[sds, sds],
                          compiler_params=_cp(("parallel",)))(e_re, e_im, lam_re, lam_im)


def _ssm_scan_bwd(dxp_re, dxp_im, xp_re, xp_im, lam_re, lam_im, nb):
    r, n = dxp_re.shape
    nc = r // nb
    cb = _tile(n, 512)

    def body(dr_ref, di_ref, xr_ref, xi_ref, lr_ref, li_ref, er_ref, ei_ref, dlr_ref, dli_ref):
        lr, li = lr_ref[...], li_ref[...]
        rid = lax.broadcasted_iota(jnp.int32, (8, 1), 0)
        z = jnp.zeros((1, cb), F32)
        alr, ali = z, z
        for b in range(nb):
            def tile(i, carry, b=b):
                gr, gi, alr, ali = carry
                r0 = pl.multiple_of(b * nc + (nc // 8 - 1 - i) * 8, 8)
                dr, di = dr_ref[pl.ds(r0, 8), :], di_ref[pl.ds(r0, 8), :]
                xr, xi = xr_ref[pl.ds(r0, 8), :], xi_ref[pl.ds(r0, 8), :]
                outr, outi = jnp.zeros((8, cb), F32), jnp.zeros((8, cb), F32)
                for j in range(7, -1, -1):
                    outr = jnp.where(rid == j, gr, outr)
                    outi = jnp.where(rid == j, gi, outi)
                    xrj, xij = xr[j:j + 1, :], xi[j:j + 1, :]
                    alr = alr + gr * xrj + gi * xij
                    ali = ali + gi * xrj - gr * xij
                    gr, gi = dr[j:j + 1, :] + lr * gr + li * gi, di[j:j + 1, :] + lr * gi - li * gr
                er_ref[pl.ds(r0, 8), :] = outr
                ei_ref[pl.ds(r0, 8), :] = outi
                return gr, gi, alr, ali

            _, _, alr, ali = lax.fori_loop(0, nc // 8, tile, (z, z, alr, ali))
        dlr_ref[...] = alr
        dli_ref[...] = ali

    mat = pl.BlockSpec((r, cb), lambda j: (0, j))
    vec = pl.BlockSpec((1, cb), lambda j: (0, j))
    sds = jax.ShapeDtypeStruct((r, n), F32)
    vds = jax.ShapeDtypeStruct((1, n), F32)
    return pl.pallas_call(body, name="ssm_scan_bwd", grid=(n // cb,), in_specs=[mat, mat, mat, mat, vec, vec],
                          out_specs=[mat, mat, vec, vec], out_shape=[sds, sds, vds, vds],
                          compiler_params=_cp(("parallel",)))(dxp_re, dxp_im, xp_re, xp_im, lam_re, lam_im)


def _ssm_operators(a_re, a_im, log_dt, b_re, b_im, c_re, c_im):
    g, p = a_re.shape
    h = b_re.shape[-1]
    ln = SSM_L
    hp = lax.Precision.HIGHEST
    lam = lax.complex(a_re, a_im)
    ldt = lam * jnp.exp(log_dt)[:, None]
    lam_bar = jnp.exp(ldt)
    bbar = ((lam_bar - 1.0) / lam)[..., None] * lax.complex(b_re, b_im)
    cm = lax.complex(c_re, c_im)
    steps = jnp.arange(ln + 1, dtype=F32)
    pw = jnp.exp(ldt[:, None, :] * steps[None, :, None])
    kd = jnp.einsum("ghp,gdp,gpk->gdhk", cm, pw[:, :ln], bbar, precision=hp).real
    sig = jnp.arange(ln)[:, None]
    tau = jnp.arange(ln)[None, :]
    lag = tau - sig
    sel = (lag[None] == jnp.arange(ln)[:, None, None]).astype(F32)
    tm = jnp.einsum("gdhk,dst->gsthk", kd, sel, precision=hp)
    tm = tm.transpose(0, 1, 4, 2, 3).reshape(g, ln * h, ln * h)
    wx = pw[:, ln - 1 - jnp.arange(ln)][:, :, :, None] * bbar[:, None]
    wx = wx.transpose(0, 1, 3, 2).reshape(g, ln * h, p)
    cp = cm[:, None] * pw[:, 1:ln + 1][:, :, None, :]
    cp = cp.reshape(g, ln * h, p).transpose(0, 2, 1)
    odd = (jnp.arange(g) % 2 == 1)[:, None, None]

    def pad(m, axis):
        z = jnp.zeros_like(m)
        return jnp.where(odd, jnp.concatenate([z, m], axis), jnp.concatenate([m, z], axis))

    lam_l = pw[:, ln]
    return (tm, pad(wx.real, 2), pad(wx.imag, 2), pad(cp.real, 1), pad(-cp.imag, 1),
            lam_l.real.reshape(1, g * p), lam_l.imag.reshape(1, g * p))


def _to_groups(a, nb, s):
    d = a.shape[1]
    g = d // SSM_H
    return a.reshape(nb, s // SSM_L, SSM_L, g, SSM_H).transpose(3, 0, 1, 2, 4).reshape(g, nb * (s // SSM_L), SSM_L * SSM_H)


def _from_groups(a, nb, s):
    g = a.shape[0]
    return a.reshape(g, nb, s // SSM_L, SSM_L, SSM_H).transpose(1, 2, 3, 0, 4).reshape(nb * s, g * SSM_H)


def _ssm_core_fwd(ug, ops, nb):
    tm, wxr, wxi, wyr, wyi, lam_re, lam_im = ops
    g, r, w = ug.shape
    n = g * SSM_P
    grp = lambda off: pl.BlockSpec((None, r, w), lambda j: (2 * j + off, 0, 0))
    wxs = lambda off: pl.BlockSpec((None, w, LANE), lambda j: (2 * j + off, 0, 0))
    pair_out = pl.BlockSpec((r, LANE), lambda j: (0, j))
    sds = jax.ShapeDtypeStruct((r, n), F32)
    e_re = _mm_terms("ssm_e_re", [(ug, grp(0), wxr, wxs(0), _NN), (ug, grp(1), wxr, wxs(1), _NN)], (g // 2,), pair_out, sds)
    e_im = _mm_terms("ssm_e_im", [(ug, grp(0), wxi, wxs(0), _NN), (ug, grp(1), wxi, wxs(1), _NN)], (g // 2,), pair_out, sds)
    xp_re, xp_im = _ssm_scan(e_re, e_im, lam_re, lam_im, nb)
    one = pl.BlockSpec((None, r, w), lambda j: (j, 0, 0))
    sq = pl.BlockSpec((None, w, w), lambda j: (j, 0, 0))
    pair_in = pl.BlockSpec((r, LANE), lambda j: (0, j // 2))
    wys = pl.BlockSpec((None, LANE, w), lambda j: (j, 0, 0))
    y = _mm_terms("ssm_y", [(ug, one, tm, sq, _NN), (xp_re, pair_in, wyr, wys, _NN), (xp_im, pair_in, wyi, wys, _NN)],
                  (g,), one, jax.ShapeDtypeStruct((g, r, w), BF16))
    return y, xp_re, xp_im


def _ssm_core_bwd(dyg, ug, xp_re, xp_im, ops, nb):
    tm, wxr, wxi, wyr, wyi, lam_re, lam_im = ops
    g, r, w = ug.shape
    n = g * SSM_P
    grp = lambda off: pl.BlockSpec((None, r, w), lambda j: (2 * j + off, 0, 0))
    wyp = lambda off: pl.BlockSpec((None, LANE, w), lambda j: (2 * j + off, 0, 0))
    pair_out = pl.BlockSpec((r, LANE), lambda j: (0, j))
    sds = jax.ShapeDtypeStruct((r, n), F32)
    dxp_re = _mm_terms("ssm_dxp_re", [(dyg, grp(0), wyr, wyp(0), _NT), (dyg, grp(1), wyr, wyp(1), _NT)], (g // 2,), pair_out, sds)
    dxp_im = _mm_terms("ssm_dxp_im", [(dyg, grp(0), wyi, wyp(0), _NT), (dyg, grp(1), wyi, wyp(1), _NT)], (g // 2,), pair_out, sds)
    de_re, de_im, dlam_re, dlam_im = _ssm_scan_bwd(dxp_re, dxp_im, xp_re, xp_im, lam_re, lam_im, nb)
    one = pl.BlockSpec((None, r, w), lambda j: (j, 0, 0))
    sq = pl.BlockSpec((None, w, w), lambda j: (j, 0, 0))
    pair_in = pl.BlockSpec((r, LANE), lambda j: (0, j // 2))
    wxs = pl.BlockSpec((None, w, LANE), lambda j: (j, 0, 0))
    wys = pl.BlockSpec((None, LANE, w), lambda j: (j, 0, 0))
    du = _mm_terms("ssm_du", [(dyg, one, tm, sq, _NT), (de_re, pair_in, wxr, wxs, _NT), (de_im, pair_in, wxi, wxs, _NT)],
                   (g,), one, jax.ShapeDtypeStruct((g, r, w), BF16))
    dtm = _mm_terms("ssm_dtm", [(ug, one, dyg, one, _TN)], (g,), sq, jax.ShapeDtypeStruct((g, w, w), F32))
    dwy = lambda nm, xp: _mm_terms(nm, [(xp, pair_in, dyg, one, _TN)], (g,), wys, jax.ShapeDtypeStruct((g, LANE, w), F32))
    dwx = lambda nm, de: _mm_terms(nm, [(ug, one, de, pair_in, _TN)], (g,), wxs, jax.ShapeDtypeStruct((g, w, LANE), F32))
    return du, (dtm, dwx("ssm_dwx_re", de_re), dwx("ssm_dwx_im", de_im), dwy("ssm_dwy_re", xp_re),
                dwy("ssm_dwy_im", xp_im), dlam_re, dlam_im)


def _modfin_fwd(c_all, w_mod, w_fin):
    n, d = c_all.shape
    nl, _, cm = w_mod.shape
    cf = w_fin.shape[1]
    width = nl * cm + cf
    hp = lax.Precision.HIGHEST

    def body(c_ref, wm_ref, wf_ref, act_ref, out_ref):
        cv = c_ref[...]
        act = cv * _sigmoid(cv)
        act_ref[...] = act
        for i in range(nl):
            out_ref[:, i * cm:(i + 1) * cm] = jnp.dot(act, wm_ref[i], preferred_element_type=F32, precision=hp)
        out_ref[:, nl * cm:] = jnp.dot(act, wf_ref[...], preferred_element_type=F32, precision=hp)

    return pl.pallas_call(body, name="modfin_fwd",
                          out_shape=[jax.ShapeDtypeStruct((n, d), F32), jax.ShapeDtypeStruct((n, width), F32)],
                          compiler_params=_cp(None))(c_all, w_mod, w_fin)


def _modfin_bwd(c_act_t, dmod_loc, dfin_loc, dall):
    d, n = c_act_t.shape
    nl, _, cm = dmod_loc.shape
    cf = dfin_loc.shape[1]
    hp = lax.Precision.HIGHEST

    def body(ct_ref, dm_ref, df_ref, da_ref, gwm_ref, gwf_ref, gb_ref):
        ct = ct_ref[...]
        for i in range(nl):
            gwm_ref[i] = jnp.dot(ct, dm_ref[i], preferred_element_type=F32, precision=hp)
        gwf_ref[...] = jnp.dot(ct, df_ref[...], preferred_element_type=F32, precision=hp)
        gb_ref[...] = jnp.sum(da_ref[...], axis=0, keepdims=True)

    return pl.pallas_call(body, name="modfin_bwd",
                          out_shape=[jax.ShapeDtypeStruct((nl, d, cm), F32), jax.ShapeDtypeStruct((d, cf), F32),
                                     jax.ShapeDtypeStruct((1, dall.shape[1]), F32)],
                          compiler_params=_cp(None))(c_act_t, dmod_loc, dfin_loc, dall)


def _adamw(name, gparts, w, m, v):
    n, r, c = gparts.shape
    tr = _tile(r, 256)
    c1 = 1.0 / (1.0 - ADAM_B1 ** ADAM_STEP)
    c2 = 1.0 / (1.0 - ADAM_B2 ** ADAM_STEP)

    def body(gp_ref, w_ref, m_ref, v_ref, g_ref, d_ref, mo_ref, vo_ref):
        gsum = gp_ref[0].astype(F32)
        for j in range(1, n):
            gsum = gsum + gp_ref[j].astype(F32)
        mn = ADAM_B1 * m_ref[...] + (1.0 - ADAM_B1) * gsum
        vn = ADAM_B2 * v_ref[...] + (1.0 - ADAM_B2) * (gsum * gsum)
        g_ref[...] = gsum
        mo_ref[...] = mn
        vo_ref[...] = vn
        d_ref[...] = -ADAM_LR * ((mn * c1) / (jnp.sqrt(vn * c2) + ADAM_EPS) + ADAM_WD * w_ref[...])

    mat = pl.BlockSpec((tr, c), lambda i: (i, 0))
    sds = jax.ShapeDtypeStruct((r, c), F32)
    return pl.pallas_call(body, name=name, grid=(r // tr,),
                          in_specs=[pl.BlockSpec((n, tr, c), lambda i: (0, i, 0)), mat, mat, mat],
                          out_specs=[mat] * 4, out_shape=[sds] * 4,
                          compiler_params=_cp(("parallel",)))(gparts, w, m, v)


def _sum_parts(name, parts):
    n, r, c = parts.shape

    def body(p_ref, o_ref):
        acc = p_ref[0]
        for j in range(1, n):
            acc = acc + p_ref[j]
        o_ref[...] = acc

    return pl.pallas_call(body, name=name, out_shape=jax.ShapeDtypeStruct((r, c), F32),
                          compiler_params=_cp(None))(parts)


class _Exchange:
    def __init__(self, arrs, gathers):
        self.arrs = [pltpu.with_memory_space_constraint(a, pltpu.HBM) for a in arrs]
        self.gathers = list(gathers)
        self.n = len(arrs)
        self.out_shape = [pltpu.HBM(((N_DEV,) + a.shape) if g else a.shape, a.dtype)
                          for a, g in zip(arrs, self.gathers)]
        self.specs = [pl.BlockSpec(memory_space=pltpu.HBM)] * self.n
        self.scratch = [pltpu.SemaphoreType.DMA((self.n, N_DEV - 1)), pltpu.SemaphoreType.DMA((self.n, N_DEV - 1)),
                        pltpu.SemaphoreType.DMA((self.n,))]

    def _copies(self, ins, outs, sems):
        send_sems, recv_sems, local_sems = sems
        x, y, c = lax.axis_index("x"), lax.axis_index("y"), lax.axis_index("c")
        me = 4 * x + 2 * y + c
        local, sends, recvs = [], [], []
        for i in range(self.n):
            src_me = ins[i] if self.gathers[i] else ins[i].at[me]
            local.append(pltpu.make_async_copy(src_me, outs[i].at[me], local_sems.at[i]))
        for dd in range(1, N_DEV):
            px = jnp.bitwise_xor(x, dd >> 2)
            py = jnp.bitwise_xor(y, (dd >> 1) & 1)
            pc = jnp.bitwise_xor(c, dd & 1)
            pid = 4 * px + 2 * py + pc
            for i in range(self.n):
                src = ins[i] if self.gathers[i] else ins[i].at[pid]
                sems_i = dict(send_sem=send_sems.at[i, dd - 1], recv_sem=recv_sems.at[i, dd - 1],
                              device_id=(px, py, pc), device_id_type=MESH)
                sends.append(pltpu.make_async_remote_copy(src_ref=src, dst_ref=outs[i].at[me], **sems_i))
                recvs.append(pltpu.make_async_remote_copy(src_ref=src, dst_ref=outs[i].at[pid], **sems_i))
        return local, sends, recvs

    def start(self, ins, outs, sems):
        local, sends, _ = self._copies(ins, outs, sems)
        for cp in local + sends:
            cp.start()

    def wait(self, ins, outs, sems):
        local, sends, recvs = self._copies(ins, outs, sems)
        for cp in recvs:
            cp.wait_recv()
        for cp in sends:
            cp.wait_send()
        for cp in local:
            cp.wait()


class _NoExchange:
    n, arrs, specs, out_shape, scratch = 0, [], [], [], []

    def start(self, ins, outs, sems):
        pass

    def wait(self, ins, outs, sems):
        pass


def _exchange(name, arrs, gathers):
    ex = _Exchange(arrs, gathers)
    n = ex.n

    def body(*refs):
        ins, outs, sems = refs[:n], refs[n:2 * n], refs[2 * n:]
        ex.start(ins, outs, sems)
        ex.wait(ins, outs, sems)

    outs = pl.pallas_call(body, name=name, in_specs=ex.specs, out_specs=ex.specs, out_shape=ex.out_shape,
                          scratch_shapes=ex.scratch)(*ex.arrs)
    return list(outs)


def _pack(pieces):
    flat = jnp.concatenate([p.reshape(-1) for p in pieces])
    pad = (-flat.shape[0]) % (8 * LANE)
    return jnp.pad(flat, (0, pad)).reshape(-1, LANE)


def _unpack(packed, shapes):
    flat = packed.reshape(-1)
    out, off = [], 0
    for shp in shapes:
        sz = math.prod(shp)
        out.append(flat[off:off + sz].reshape(shp))
        off += sz
    return out


def kernel(x, c, norm_mix, norm_ffn, w_mod, b_mod, w_qkv, w_o_attn, w_in_ssm, a_re, a_im, log_dt, b_re, b_im, c_re, c_im, d_skip, w_glu, b_glu, w_o_ssm, w_up, conv_w, conv_b, w_down, norm_out, w_fin, b_fin, loss_target, m_norm_mix, m_norm_ffn, m_w_mod, m_b_mod, m_w_qkv, m_w_o_attn, m_w_in_ssm, m_a_re, m_a_im, m_log_dt, m_b_re, m_b_im, m_c_re, m_c_im, m_d_skip, m_w_glu, m_b_glu, m_w_o_ssm, m_w_up, m_conv_w, m_conv_b, m_w_down, m_norm_out, m_w_fin, m_b_fin, v_norm_mix, v_norm_ffn, v_w_mod, v_b_mod, v_w_qkv, v_w_o_attn, v_w_in_ssm, v_a_re, v_a_im, v_log_dt, v_b_re, v_b_im, v_c_re, v_c_im, v_d_skip, v_w_glu, v_b_glu, v_w_o_ssm, v_w_up, v_conv_w, v_conv_b, v_w_down, v_norm_out, v_w_fin, v_b_fin):
    nb, s, d = x.shape
    t = nb * s
    n_seq = nb * N_DEV
    me = 4 * lax.axis_index("x") + 2 * lax.axis_index("y") + lax.axis_index("c")
    cm = w_mod.shape[2]
    cf = w_fin.shape[1]
    c_up = w_up.shape[2]
    r_dn = w_down.shape[1]
    g_ssm = d // SSM_H

    wq8, c8 = _exchange("gather_first", [w_qkv[0].astype(BF16), c], [True, True])
    later = _Exchange([w_o_attn[0].astype(BF16), w_in_ssm[0].astype(BF16), w_glu[0].astype(BF16),
                       w_o_ssm[0].astype(BF16), w_up[0].astype(BF16), w_up[1].astype(BF16),
                       w_down[0].astype(BF16), w_down[1].astype(BF16), conv_w, d_skip, b_glu], [True] * 11)
    half = N_DEV // 2
    cb_l = [conv_b[i].reshape(N_DEV, 1, c_up) for i in range(2)]
    c_all = c8.reshape(n_seq, d)

    c_act, modloc = _modfin_fwd(c_all, w_mod, w_fin)
    (mod8,) = _exchange("gather_mod", [modloc], [True])
    mine = lax.dynamic_slice_in_dim(mod8, me * nb, nb, axis=1)
    mods = []
    for i in range(2):
        mi = mine[:, :, i * cm:(i + 1) * cm].transpose(1, 0, 2).reshape(nb, N_DEV * cm) + b_mod[i]
        mods.append([mi[:, j * d:(j + 1) * d].reshape(nb, 1, d) for j in range(6)])
    fin = mine[:, :, 2 * cm:].transpose(1, 0, 2).reshape(nb, N_DEV * cf) + b_fin
    sh_f, sc_f = fin[:, :d].reshape(nb, 1, d), fin[:, d:].reshape(nb, 1, d)

    row = lambda a: a.reshape(1, -1)
    x0 = x.reshape(t, d)

    def ffn_fwd(i, xin, sh, sc, gate):
        h = _norm_mod_fwd(f"ffn{i}_norm", xin, row(norm_ffn[i]), sh, sc, nb)
        up = _mm(f"ffn{i}_up", h, wup8[i], (t // tm_, N_DEV, 1),
                 pl.BlockSpec((tm_, d), lambda a, b, k: (a, 0)), pl.BlockSpec((None, d, c_up), lambda a, b, k: (b, 0, 0)),
                 pl.BlockSpec((None, tm_, c_up), lambda a, b, k: (b, a, 0)),
                 jax.ShapeDtypeStruct((N_DEV, t, c_up), BF16), _NN, (tm_, c_up))
        act = _ffn_act_fwd(f"ffn{i}_act", up, cw_l[i], cb_l[i], nb, s)
        yf = _mm(f"ffn{i}_down", act, wd4[i], (t // tm_, 1, half),
                 pl.BlockSpec((None, tm_, c_up), lambda a, b, k: (k, a, 0)),
                 pl.BlockSpec((None, c_up, d), lambda a, b, k: (k, 0, 0)),
                 pl.BlockSpec((tm_, d), lambda a, b, k: (a, 0)), jax.ShapeDtypeStruct((t, d), F32), _NN, (tm_, d))
        xout = _gate_add(f"ffn{i}_res", xin, yf, gate, nb)
        return xout, (h, up, act, yf)

    tm_ = _tile(t, 512)
    sh1, sc1, g1, sh2, sc2, g2 = mods[0]
    h1 = _norm_mod_fwd("attn_norm", x0, row(norm_mix[0]), sh1, sc1, nb)
    cq = wq8.shape[2]
    qkv = _mm("attn_qkv", h1, wq8, (t // tm_, N_DEV, 1),
              pl.BlockSpec((tm_, d), lambda a, b, k: (a, 0)), pl.BlockSpec((None, d, cq), lambda a, b, k: (b, 0, 0)),
              pl.BlockSpec((tm_, cq), lambda a, b, k: (a, b)), jax.ShapeDtypeStruct((t, 3 * d), BF16), _NN, (tm_, cq))
    o_att, car_att, (wo8, win8, wglu8, wos8, wup8_0, wup8_1, wd8_0, wd8_1, cw8, dskip8, bglu8) = _attn_fwd(
        qkv, nb, s, d, later)
    wo = wo8.reshape(d, d)
    win = win8.reshape(d, d)
    wglu = wglu8.reshape(d, d)
    wos = wos8.reshape(d, d)
    wup8 = [wup8_0, wup8_1]
    wd4 = [wd8_0.reshape(half, 2 * r_dn, d), wd8_1.reshape(half, 2 * r_dn, d)]
    cw_l = [cw8[:, 0], cw8[:, 1]]
    dskip_f = dskip8.reshape(1, d)
    bglu_f = bglu8.reshape(1, d)
    ya = _mm_nn("attn_out", o_att, wo, F32)
    x1 = _gate_add("attn_res", x0, ya, g1, nb)
    x2, ffn0_saved = ffn_fwd(0, x1, sh2, sc2, g2)

    sh1b, sc1b, g1b, sh2b, sc2b, g2b = mods[1]
    ops = _ssm_operators(a_re[0], a_im[0], log_dt[0], b_re[0], b_im[0], c_re[0], c_im[0])
    h3 = _norm_mod_fwd("ssm_norm", x2, row(norm_mix[1]), sh1b, sc1b, nb)
    u = _mm_nn("ssm_in", h3, win, BF16)
    ug = _to_groups(u, nb, s)
    yg, xp_re, xp_im = _ssm_core_fwd(ug, ops, nb)
    y_ssm, z_ssm = _ssm_post_fwd(_from_groups(yg, nb, s), u, dskip_f)
    gl = _mm_nn("ssm_glu", z_ssm, wglu, F32)
    gg = _glu_fwd(z_ssm, gl, bglu_f)
    ys2 = _mm_nn("ssm_out", gg, wos, F32)
    x3 = _gate_add("ssm_res", x2, ys2, g1b, nb)
    x4, ffn1_saved = ffn_fwd(1, x3, sh2b, sc2b, g2b)

    dx4, g_norm_out, dsh_f, dsc_f, loss_blk = _norm_mod_bwd(
        "final_norm", None, x4, row(norm_out), sh_f, sc_f, None, loss_target.reshape(t, d), nb)
    loss = lax.psum(loss_blk[0, 0], ("x", "y", "c"))

    def ffn_bwd(i, dxo, xin, sc, gate, saved):
        h, up, act, yf = saved
        dyf, dgate = _gate_bwd(f"ffn{i}_res_bwd", dxo, yf, gate, nb)
        dact = _mm(f"ffn{i}_down_dx", dyf, wd4[i], (t // tm_, half, 1),
                   pl.BlockSpec((tm_, d), lambda a, b, k: (a, 0)), pl.BlockSpec((None, c_up, d), lambda a, b, k: (b, 0, 0)),
                   pl.BlockSpec((None, tm_, c_up), lambda a, b, k: (b, a, 0)),
                   jax.ShapeDtypeStruct((half, t, c_up), BF16), _NT, (tm_, c_up))
        tk = _tile(t, 1024)
        gwd = _mm(f"ffn{i}_down_dw", act, dyf, (half, 1, t // tk),
                  pl.BlockSpec((None, tk, c_up), lambda a, b, k: (a, k, 0)), pl.BlockSpec((tk, d), lambda a, b, k: (k, 0)),
                  pl.BlockSpec((None, c_up, d), lambda a, b, k: (a, 0, 0)),
                  jax.ShapeDtypeStruct((half, c_up, d), BF16), _TN, (c_up, d))
        dup, dcw, dcb = _ffn_act_bwd(f"ffn{i}_act_bwd", up, dact, cw_l[i], cb_l[i], nb, s)
        dh = _mm(f"ffn{i}_up_dx", dup, wup8[i], (t // tm_, 1, N_DEV),
                 pl.BlockSpec((None, tm_, c_up), lambda a, b, k: (k, a, 0)),
                 pl.BlockSpec((None, d, c_up), lambda a, b, k: (k, 0, 0)),
                 pl.BlockSpec((tm_, d), lambda a, b, k: (a, 0)), jax.ShapeDtypeStruct((t, d), F32), _NT, (tm_, d))
        gwup = _mm(f"ffn{i}_up_dw", h, dup, (1, N_DEV, t // tk),
                   pl.BlockSpec((tk, d), lambda a, b, k: (k, 0)), pl.BlockSpec((None, tk, c_up), lambda a, b, k: (b, k, 0)),
                   pl.BlockSpec((None, d, c_up), lambda a, b, k: (b, 0, 0)),
                   jax.ShapeDtypeStruct((N_DEV, d, c_up), BF16), _TN, (d, c_up))
        dxi, g_norm, dsh, dsc = _norm_mod_bwd(f"ffn{i}_norm_bwd", dh, xin, row(norm_ffn[i]), None, sc, dxo, None, nb)
        return dxi, (gwup, gwd.reshape(N_DEV, r_dn, d), dcw, dcb, g_norm, dsh, dsc, dgate)

    dx3, (gwup1, gwd1, dcw1, dcb1, g_nffn1, dsh2b, dsc2b, dg2b) = ffn_bwd(1, dx4, x3, sc2b, g2b, ffn1_saved)

    dys2, dg1b = _gate_bwd("ssm_res_bwd", dx3, ys2, g1b, nb)
    dgg = _mm_nt("ssm_out_dx", dys2, wos, F32)
    gwos = _mm_tn("ssm_out_dw", gg, dys2, BF16)
    dgl, dz1, g_bglu = _glu_bwd(dgg, z_ssm, gl, bglu_f)
    dz2 = _mm_nt("ssm_glu_dx", dgl, wglu, F32)
    gwglu = _mm_tn("ssm_glu_dw", z_ssm, dgl, BF16)
    dy_ssm, du_skip, g_dskip = _ssm_post_bwd(dz1, dz2, y_ssm, u, dskip_f)
    dug, d_ops = _ssm_core_bwd(_to_groups(dy_ssm, nb, s), ug, xp_re, xp_im, ops, nb)
    du = _add_cast(_from_groups(dug, nb, s), du_skip)
    dh3 = _mm_nt("ssm_in_dx", du, win, F32)
    gwin = _mm_tn("ssm_in_dw", h3, du, BF16)
    dx2, g_nmix1, dsh1b, dsc1b = _norm_mod_bwd("ssm_norm_bwd", dh3, x2, row(norm_mix[1]), None, sc1b, dx3, None, nb)
    _, ops_vjp = jax.vjp(_ssm_operators, a_re[0], a_im[0], log_dt[0], b_re[0], b_im[0], c_re[0], c_im[0])
    g_ssm_params = ops_vjp(d_ops)

    dx1, (gwup0, gwd0, dcw0, dcb0, g_nffn0, dsh2, dsc2, dg2) = ffn_bwd(0, dx2, x1, sc2, g2, ffn0_saved)

    dya, dg1 = _gate_bwd("attn_res_bwd", dx1, ya, g1, nb)
    do_att = _mm_nt("attn_out_dx", dya, wo, BF16)
    gwo = _mm_tn("attn_out_dw", o_att, dya, BF16)
    rows8 = lambda a: a.reshape(N_DEV, d // N_DEV, d)
    early = _Exchange([rows8(gwo), rows8(gwin), rows8(gwglu), rows8(gwos), gwup0, gwup1, gwd0, gwd1], [False] * 8)
    dq, dk, dv, (ro, rin, rglu, ros, rup0, rup1, rd0, rd1) = _attn_bwd(qkv, car_att, do_att, nb, s, d, early)
    dqkv = jnp.concatenate([dq, dk, dv], axis=1)
    dh1 = _mm("attn_qkv_dx", dqkv, wq8, (t // tm_, 1, N_DEV),
              pl.BlockSpec((tm_, cq), lambda a, b, k: (a, k)),
              pl.BlockSpec((None, d, cq), lambda a, b, k: (k, 0, 0)),
              pl.BlockSpec((tm_, d), lambda a, b, k: (a, 0)), jax.ShapeDtypeStruct((t, d), F32), _NT, (tm_, d))
    tk = _tile(t, 1024)
    gwq8 = _mm("attn_qkv_dw", h1, dqkv, (1, N_DEV, t // tk),
               pl.BlockSpec((tk, d), lambda a, b, k: (k, 0)),
               pl.BlockSpec((tk, cq), lambda a, b, k: (k, b)),
               pl.BlockSpec((None, d, cq), lambda a, b, k: (b, 0, 0)),
               jax.ShapeDtypeStruct((N_DEV, d, cq), BF16), _TN, (d, cq))
    dx0, g_nmix0, dsh1, dsc1 = _norm_mod_bwd("attn_norm_bwd", dh1, x0, row(norm_mix[0]), None, sc1, dx1, None, nb)
    grad_x = dx0.reshape(nb, s, d)

    dmod =[jnp.concatenate([a.reshape(nb, d) for a in grp], axis=1) for grp in
            ([dsh1, dsc1, dg1, dsh2, dsc2, dg2], [dsh1b, dsc1b, dg1b, dsh2b, dsc2b, dg2b])]
    dfin = jnp.concatenate([dsh_f.reshape(nb, d), dsc_f.reshape(nb, d)], axis=1)
    dmodfin = jnp.concatenate(dmod + [dfin], axis=1)
    small_shapes = [(2, d), (2, d), (d,), (2, N_DEV * c_up), (g_ssm, SSM_P), (g_ssm, SSM_P), (g_ssm,),
                    (g_ssm, SSM_P, SSM_H), (g_ssm, SSM_P, SSM_H), (g_ssm, SSM_H, SSM_P), (g_ssm, SSM_H, SSM_P),
                    (d,), (d,), (2, N_DEV, 3, c_up)]
    small_partial = _pack([jnp.stack([g_nmix0, g_nmix1]), jnp.stack([g_nffn0, g_nffn1]), g_norm_out,
                           jnp.stack([dcb0, dcb1]), *g_ssm_params, g_dskip, g_bglu, jnp.stack([dcw0, dcw1])])
    rq, dmf8, small8 = _exchange("exchange_last", [gwq8, dmodfin, small_partial], [False, True, True])
    small_sum = _sum_parts("sum_small_grads", small8)
    (g_norm_mix, g_norm_ffn, g_norm_out_s, g_conv_b, g_a_re, g_a_im, g_log_dt, g_b_re, g_b_im, g_c_re, g_c_im,
     g_dskip_full, g_bglu_full, g_cw_full) = _unpack(small_sum, small_shapes)

    dall = dmf8.reshape(n_seq, 14 * d)
    dmod_loc = jnp.stack([lax.dynamic_slice_in_dim(dall[:, i * 6 * d:(i + 1) * 6 * d], me * cm, cm, axis=1)
                          for i in range(2)])
    dfin_loc = lax.dynamic_slice_in_dim(dall[:, 12 * d:], me * cf, cf, axis=1)
    g_w_mod, g_w_fin, g_bias = _modfin_bwd(c_act.T, dmod_loc, dfin_loc, dall)
    g_b_mod = g_bias[0, :12 * d].reshape(2, 6 * d)
    g_b_fin = g_bias[0, 12 * d:]

    def big(name, parts, w, m, v):
        shp = w.shape
        r2 = lambda a: a.reshape(-1, shp[-1])
        res = _adamw(name, parts.reshape(parts.shape[0], -1, shp[-1]), r2(w), r2(m), r2(v))
        return [a.reshape(shp) for a in res]

    upd = {}
    upd["w_mod"] = big("adamw_w_mod", g_w_mod[None], w_mod, m_w_mod, v_w_mod)
    upd["w_fin"] = big("adamw_w_fin", g_w_fin[None], w_fin, m_w_fin, v_w_fin)
    upd["w_qkv"] = big("adamw_w_qkv", rq, w_qkv, m_w_qkv, v_w_qkv)
    upd["w_o_attn"] = big("adamw_w_o_attn", ro, w_o_attn, m_w_o_attn, v_w_o_attn)
    upd["w_in_ssm"] = big("adamw_w_in_ssm", rin, w_in_ssm, m_w_in_ssm, v_w_in_ssm)
    upd["w_glu"] = big("adamw_w_glu", rglu, w_glu, m_w_glu, v_w_glu)
    upd["w_o_ssm"] = big("adamw_w_o_ssm", ros, w_o_ssm, m_w_o_ssm, v_w_o_ssm)
    up_l = [big(f"adamw_w_up{i}", r, w_up[i], m_w_up[i], v_w_up[i]) for i, r in enumerate((rup0, rup1))]
    upd["w_up"] = [jnp.stack([up_l[0][j], up_l[1][j]]) for j in range(4)]
    dn_l = [big(f"adamw_w_down{i}", r, w_down[i], m_w_down[i], v_w_down[i]) for i, r in enumerate((rd0, rd1))]
    upd["w_down"] = [jnp.stack([dn_l[0][j], dn_l[1][j]]) for j in range(4)]

    g_dskip_loc = lax.dynamic_slice_in_dim(g_dskip_full.reshape(1, d), me * (d // N_DEV), d // N_DEV, axis=1)
    g_bglu_loc = lax.dynamic_slice_in_dim(g_bglu_full.reshape(1, d), me * (d // N_DEV), d // N_DEV, axis=1)
    g_cw_loc = lax.dynamic_slice_in_dim(g_cw_full, me, 1, axis=1).reshape(2, 3, c_up)
    small_names = ["norm_mix", "norm_ffn", "b_mod", "a_re", "a_im", "log_dt", "b_re", "b_im", "c_re", "c_im",
                   "d_skip", "b_glu", "conv_w", "conv_b", "norm_out", "b_fin"]
    small_g = [g_norm_mix, g_norm_ffn, g_b_mod, g_a_re[None], g_a_im[None], g_log_dt[None], g_b_re[None], g_b_im[None],
               g_c_re[None], g_c_im[None], g_dskip_loc, g_bglu_loc, g_cw_loc, g_conv_b, g_norm_out_s, g_b_fin]
    small_w = [norm_mix, norm_ffn, b_mod, a_re, a_im, log_dt, b_re, b_im, c_re, c_im, d_skip, b_glu, conv_w, conv_b,
               norm_out, b_fin]
    small_m = [m_norm_mix, m_norm_ffn, m_b_mod, m_a_re, m_a_im, m_log_dt, m_b_re, m_b_im, m_c_re, m_c_im, m_d_skip,
               m_b_glu, m_conv_w, m_conv_b, m_norm_out, m_b_fin]
    small_v = [v_norm_mix, v_norm_ffn, v_b_mod, v_a_re, v_a_im, v_log_dt, v_b_re, v_b_im, v_c_re, v_c_im, v_d_skip,
               v_b_glu, v_conv_w, v_conv_b, v_norm_out, v_b_fin]
    shapes = [w.shape for w in small_w]
    res = _adamw("adamw_small", _pack(small_g)[None], _pack(small_w), _pack(small_m), _pack(small_v))
    res = [_unpack(r, shapes) for r in res]
    for j, nm in enumerate(small_names):
        upd[nm] = [res[k][j] for k in range(4)]

    order = ["norm_mix", "norm_ffn", "w_mod", "b_mod", "w_qkv", "w_o_attn", "w_in_ssm", "a_re", "a_im", "log_dt",
             "b_re", "b_im", "c_re", "c_im", "d_skip", "w_glu", "b_glu", "w_o_ssm", "w_up", "conv_w", "conv_b",
             "w_down", "norm_out", "w_fin", "b_fin"]
    outs = [loss, grad_x]
    for k in range(4):
        outs += [upd[nm][k] for nm in order]
    return tuple(outs)
```

```python
import functools
import math

import jax
import jax.numpy as jnp
from jax import lax
from jax.experimental import pallas as pl
from jax.experimental.pallas import tpu as pltpu

F32 = jnp.float32
BF16 = jnp.bfloat16
MESH = pl.DeviceIdType.MESH

N_DEV = 8
HEAD_DIM = 64
ATT_BLK = 128
ATT_BQ = 256
ATT_UNROLL = 2
SSM_H = 16
SSM_P = 64
SSM_L = 16
EPS = 1e-6
ADAM_LR, ADAM_B1, ADAM_B2, ADAM_EPS, ADAM_WD, ADAM_STEP = 0.001, 0.9, 0.999, 1e-08, 0.01, 10
V7X_VMEM_LIMIT = 56 * 1024 * 1024
LANE = 128

_NN = (((1,), (0,)), ((), ()))
_NT = (((1,), (1,)), ((), ()))
_TN = (((0,), (0,)), ((), ()))


def _cp(sem):
    return pltpu.CompilerParams(dimension_semantics=sem, vmem_limit_bytes=V7X_VMEM_LIMIT)


def _tile(n, pref):
    if n <= pref:
        return n
    t = pref - pref % 16
    while t >= 16:
        if n % t == 0:
            return t
        t -= 16
    return n


def _mm(name, a, b, grid, a_spec, b_spec, out_spec, out_shape, dims, acc_shape):
    nk = grid[-1]
    kax = len(grid) - 1

    def body(a_ref, b_ref, o_ref, acc_ref):
        k = pl.program_id(kax)

        @pl.when(k == 0)
        def _():
            acc_ref[...] = jnp.zeros(acc_shape, F32)

        acc_ref[...] += lax.dot_general(a_ref[...].astype(BF16), b_ref[...].astype(BF16), dims,
                                        preferred_element_type=F32)

        @pl.when(k == nk - 1)
        def _():
            o_ref[...] = acc_ref[...].astype(o_ref.dtype)

    return pl.pallas_call(
        body, name=name, grid=grid, in_specs=[a_spec, b_spec], out_specs=out_spec, out_shape=out_shape,
        scratch_shapes=[pltpu.VMEM(acc_shape, F32)],
        compiler_params=_cp(("parallel",) * kax + ("arbitrary",)))(a, b)


def _mm_terms(name, terms, grid, out_spec, out_shape):
    n = len(terms)
    dims = [t[4] for t in terms]

    def body(*refs):
        o_ref = refs[2 * n]
        acc = None
        for i in range(n):
            d = lax.dot_general(refs[2 * i][...].astype(BF16), refs[2 * i + 1][...].astype(BF16), dims[i],
                                preferred_element_type=F32)
            acc = d if acc is None else acc + d
        o_ref[...] = acc.astype(o_ref.dtype)

    ops, specs = [], []
    for a, a_spec, b, b_spec, _ in terms:
        ops += [a, b]
        specs += [a_spec, b_spec]
    return pl.pallas_call(body, name=name, grid=grid, in_specs=specs, out_specs=out_spec, out_shape=out_shape,
                          compiler_params=_cp(("parallel",) * len(grid)))(*ops)


def _mm_nn(name, a, w, out_dtype):
    m, k = a.shape
    n = w.shape[1]
    tm, tn, tk = _tile(m, 512), _tile(n, 1024), _tile(k, 1024)
    return _mm(name, a, w, (m // tm, n // tn, k // tk),
               pl.BlockSpec((tm, tk), lambda i, j, kk: (i, kk)), pl.BlockSpec((tk, tn), lambda i, j, kk: (kk, j)),
               pl.BlockSpec((tm, tn), lambda i, j, kk: (i, j)), jax.ShapeDtypeStruct((m, n), out_dtype), _NN, (tm, tn))


def _mm_nt(name, a, w, out_dtype):
    m, n = a.shape
    k = w.shape[0]
    tm, tko, tn = _tile(m, 512), _tile(k, 1024), _tile(n, 1024)
    return _mm(name, a, w, (m // tm, k // tko, n // tn),
               pl.BlockSpec((tm, tn), lambda i, j, kk: (i, kk)), pl.BlockSpec((tko, tn), lambda i, j, kk: (j, kk)),
               pl.BlockSpec((tm, tko), lambda i, j, kk: (i, j)), jax.ShapeDtypeStruct((m, k), out_dtype), _NT, (tm, tko))


def _mm_tn(name, a, b, out_dtype):
    t, m = a.shape
    n = b.shape[1]
    tm, tn, tk = _tile(m, 512), _tile(n, 1024), _tile(t, 1024)
    return _mm(name, a, b, (m // tm, n // tn, t // tk),
               pl.BlockSpec((tk, tm), lambda i, j, kk: (kk, i)), pl.BlockSpec((tk, tn), lambda i, j, kk: (kk, j)),
               pl.BlockSpec((tm, tn), lambda i, j, kk: (i, j)), jax.ShapeDtypeStruct((m, n), out_dtype), _TN, (tm, tn))


def _norm_mod_fwd(name, x, g, shift, scale, nb):
    t, d = x.shape
    s = t // nb
    tr = _tile(s, 512)
    nt = s // tr

    def body(x_ref, g_ref, sh_ref, sc_ref, h_ref):
        xv = x_ref[...]
        r = lax.rsqrt(jnp.mean(xv * xv, axis=-1, keepdims=True) + EPS)
        y = xv * r * g_ref[...]
        h_ref[...] = (y * (1.0 + sc_ref[...]) + sh_ref[...]).astype(h_ref.dtype)

    row = pl.BlockSpec((tr, d), lambda b, i: (b * nt + i, 0))
    vec = pl.BlockSpec((None, 1, d), lambda b, i: (b, 0, 0))
    return pl.pallas_call(body, name=name, grid=(nb, nt),
                          in_specs=[row, pl.BlockSpec((1, d), lambda b, i: (0, 0)), vec, vec],
                          out_specs=row, out_shape=jax.ShapeDtypeStruct((t, d), BF16),
                          compiler_params=_cp(("parallel", "parallel")))(x, g, shift, scale)


def _norm_mod_bwd(name, dh, x, g, shift, scale, dres, target, nb):
    t, d = x.shape
    s = t // nb
    tr = _tile(s, 256)
    nt = s // tr
    final = target is not None

    def body(*refs):
        if final:
            x_ref, g_ref, sh_ref, sc_ref, tg_ref, dx_ref, dg_ref, dsh_ref, dsc_ref, loss_ref = refs
        else:
            dh_ref, x_ref, g_ref, sc_ref, dres_ref, dx_ref, dg_ref, dsh_ref, dsc_ref = refs
        b, i = pl.program_id(0), pl.program_id(1)
        xv = x_ref[...]
        gv = g_ref[...]
        r = lax.rsqrt(jnp.mean(xv * xv, axis=-1, keepdims=True) + EPS)
        nrm = xv * r
        y = nrm * gv
        one_sc = 1.0 + sc_ref[...]
        if final:
            err = y * one_sc + sh_ref[...] - tg_ref[...]
            dhv = err * (1.0 / d)
        else:
            dhv = dh_ref[...].astype(F32)
        dy = dhv * one_sc
        dn = dy * gv
        dxv = r * (dn - nrm * jnp.mean(dn * nrm, axis=-1, keepdims=True))
        if final:
            dx_ref[...] = dxv
        else:
            dx_ref[...] = dres_ref[...] + dxv

        @pl.when(i == 0)
        def _():
            dsh_ref[...] = jnp.zeros_like(dsh_ref)
            dsc_ref[...] = jnp.zeros_like(dsc_ref)

        @pl.when((i == 0) & (b == 0))
        def _():
            dg_ref[...] = jnp.zeros_like(dg_ref)
            if final:
                loss_ref[...] = jnp.zeros_like(loss_ref)

        dsh_ref[...] += jnp.sum(dhv, axis=0, keepdims=True)
        dsc_ref[...] += jnp.sum(dhv * y, axis=0, keepdims=True)
        dg_ref[...] += jnp.sum(dy * nrm, axis=0, keepdims=True)
        if final:
            loss_ref[...] += (0.5 / d) * jnp.sum(err * err)

    row = pl.BlockSpec((tr, d), lambda b, i: (b * nt + i, 0))
    vec = pl.BlockSpec((None, 1, d), lambda b, i: (b, 0, 0))
    gsp = pl.BlockSpec((1, d), lambda b, i: (0, 0))
    out_specs = [row, gsp, vec, vec]
    out_shape = [jax.ShapeDtypeStruct((t, d), F32), jax.ShapeDtypeStruct((1, d), F32),
                 jax.ShapeDtypeStruct((nb, 1, d), F32), jax.ShapeDtypeStruct((nb, 1, d), F32)]
    if final:
        ins, in_specs = [x, g, shift, scale, target], [row, gsp, vec, vec, row]
        out_specs.append(pl.BlockSpec((8, LANE), lambda b, i: (0, 0)))
        out_shape.append(jax.ShapeDtypeStruct((8, LANE), F32))
    else:
        ins, in_specs = [dh, x, g, scale, dres], [row, row, gsp, vec, row]
    return pl.pallas_call(body, name=name, grid=(nb, nt), in_specs=in_specs, out_specs=out_specs,
                          out_shape=out_shape, compiler_params=_cp(("arbitrary", "arbitrary")))(*ins)


def _gate_add(name, x, y, gate, nb):
    t, d = x.shape
    s = t // nb
    tr = _tile(s, 512)
    nt = s // tr

    def body(x_ref, y_ref, g_ref, o_ref):
        o_ref[...] = x_ref[...] + g_ref[...] * y_ref[...]

    row = pl.BlockSpec((tr, d), lambda b, i: (b * nt + i, 0))
    vec = pl.BlockSpec((None, 1, d), lambda b, i: (b, 0, 0))
    return pl.pallas_call(body, name=name, grid=(nb, nt), in_specs=[row, row, vec], out_specs=row,
                          out_shape=jax.ShapeDtypeStruct((t, d), F32),
                          compiler_params=_cp(("parallel", "parallel")))(x, y, gate)


def _gate_bwd(name, dx, y, gate, nb):
    t, d = dx.shape
    s = t // nb
    tr = _tile(s, 512)
    nt = s // tr

    def body(dx_ref, y_ref, g_ref, dy_ref, dg_ref):
        dxv = dx_ref[...]
        dy_ref[...] = (g_ref[...] * dxv).astype(dy_ref.dtype)

        @pl.when(pl.program_id(1) == 0)
        def _():
            dg_ref[...] = jnp.zeros_like(dg_ref)

        dg_ref[...] += jnp.sum(dxv * y_ref[...], axis=0, keepdims=True)

    row = pl.BlockSpec((tr, d), lambda b, i: (b * nt + i, 0))
    vec = pl.BlockSpec((None, 1, d), lambda b, i: (b, 0, 0))
    return pl.pallas_call(body, name=name, grid=(nb, nt), in_specs=[row, row, vec], out_specs=[row, vec],
                          out_shape=[jax.ShapeDtypeStruct((t, d), BF16), jax.ShapeDtypeStruct((nb, 1, d), F32)],
                          compiler_params=_cp(("parallel", "arbitrary")))(dx, y, gate)


def _log_sigmoid(z):
    return jnp.minimum(z, 0.0) - jnp.log(1.0 + jnp.exp(-jnp.abs(z)))


def _split_dot(v, tri):
    hi = v.astype(BF16)
    lo = (v - hi.astype(F32)).astype(BF16)
    return (jnp.dot(hi, tri, preferred_element_type=F32) + jnp.dot(lo, tri, preferred_element_type=F32))


def _grid_ends(grid):
    ids = [pl.program_id(a) for a in range(len(grid))]
    first = functools.reduce(lambda u, w: u & w, [i == 0 for i in ids])
    last = functools.reduce(lambda u, w: u & w, [i == n - 1 for i, n in zip(ids, grid)])
    return first, last


def _attn_fwd(qkv, nb, s, d, ex):
    t = nb * s
    npair = d // LANE
    bk = ATT_BLK
    bq = min(ATT_BQ, s)
    nq = s // bq
    kpq = bq // bk
    nheads = LANE // HEAD_DIM
    scale = HEAD_DIM ** -0.5
    grid = (nb, npair, nq)
    assert s // bk <= HEAD_DIM, "one carry lane per key block and head"
    assert bk == LANE, "the running sums are kept one 128-lane tile wide"
    assert kpq == ATT_UNROLL, "one loop trip covers exactly the key blocks under a query block's diagonal"

    def body(*refs):
        q_ref, k_ref, v_ref = refs[:3]
        ex_ins = refs[3:3 + ex.n]
        o_ref, car_ref = refs[3 + ex.n:5 + ex.n]
        ex_outs = refs[5 + ex.n:5 + 2 * ex.n]
        acc_s, run_s = refs[5 + 2 * ex.n:7 + 2 * ex.n]
        sems = refs[7 + 2 * ex.n:]
        first, last = _grid_ends(grid)

        @pl.when(first)
        def _():
            ex.start(ex_ins, ex_outs, sems)

        qi = pl.program_id(2)
        q = q_ref[...]
        lane = lax.broadcasted_iota(jnp.int32, (1, LANE), 1)
        row = lax.broadcasted_iota(jnp.int32, (bq, bk), 0)
        col = lax.broadcasted_iota(jnp.int32, (bq, bk), 1)
        trow = lax.broadcasted_iota(jnp.int32, (bk, bk), 0)
        tcol = lax.broadcasted_iota(jnp.int32, (bk, bk), 1)
        tri = (trow > tcol).astype(BF16)
        hms = [(lane // HEAD_DIM) == hh for hh in range(nheads)]
        qhs = [jnp.where(hm, q, jnp.zeros_like(q)) * scale for hm in hms]
        car_ref[...] = jnp.zeros((bq, LANE), F32)
        acc_s[...] = jnp.zeros_like(acc_s)
        run_s[...] = jnp.zeros_like(run_s)
        nkb = (qi + 1) * kpq

        def step(jj, carry, masked):
            js = [nkb - 1 - (ATT_UNROLL * jj + u) for u in range(ATT_UNROLL)]
            k0s = [pl.multiple_of(j * bk, bk) for j in js]
            kjs = [k_ref[pl.ds(k0, bk), :] for k0 in k0s]
            vjs = [v_ref[pl.ds(k0, bk), :] for k0 in k0s]
            masks = [(k0 + col) < (qi * bq + row) for k0 in k0s] if masked else None
            zs = [[lax.dot_general(qhs[hh], kj, _NT, preferred_element_type=F32) for kj in kjs]
                  for hh in range(nheads)]
            car = car_ref[...]
            for hh in range(nheads):
                run = run_s[hh]
                acc = None
                for u in range(ATT_UNROLL):
                    z = zs[hh][u]
                    lb = _log_sigmoid(z)
                    l1 = lb - z
                    if masked:
                        l1 = jnp.where(masks[u], l1, 0.0)
                    suf = _split_dot(l1, tri) + run
                    w = jnp.exp(lb + suf)
                    if masked:
                        w = jnp.where(masks[u], w, 0.0)
                    pv = jnp.dot(w.astype(BF16), vjs[u], preferred_element_type=F32)
                    acc = pv if acc is None else acc + pv
                    car = jnp.where(lane == hh * HEAD_DIM + js[u], run, car)
                    run = run + jnp.sum(l1, axis=1, keepdims=True)
                acc_s[hh] += acc
                run_s[hh] = run
            car_ref[...] = car
            return carry

        step(0, 0, True)
        lax.fori_loop(1, nkb // ATT_UNROLL, functools.partial(step, masked=False), 0)
        out = acc_s[0]
        for hh in range(1, nheads):
            out = jnp.where(hms[hh], acc_s[hh], out)
        o_ref[...] = out.astype(o_ref.dtype)

        @pl.when(last)
        def _():
            ex.wait(ex_ins, ex_outs, sems)

    qspec = pl.BlockSpec((bq, LANE), lambda b, p, i: (b * nq + i, p))
    res = pl.pallas_call(
        body, name="attn_fwd", grid=grid,
        in_specs=[qspec,
                  pl.BlockSpec((s, LANE), lambda b, p, i: (b, npair + p)),
                  pl.BlockSpec((s, LANE), lambda b, p, i: (b, 2 * npair + p))] + ex.specs,
        out_specs=[qspec, qspec] + ex.specs,
        out_shape=[jax.ShapeDtypeStruct((t, d), BF16), jax.ShapeDtypeStruct((t, d), F32)] + ex.out_shape,
        scratch_shapes=[pltpu.VMEM((nheads, bq, LANE), F32), pltpu.VMEM((nheads, bq, LANE), F32)] + ex.scratch,
        compiler_params=_cp(("arbitrary", "arbitrary", "arbitrary")))(qkv, qkv, qkv, *ex.arrs)
    return res[0], res[1], list(res[2:])


def _attn_bwd(qkv, car, do, nb, s, d, ex):
    t = nb * s
    npair = d // LANE
    bk = ATT_BLK
    bq = min(ATT_BQ, s)
    nq = s // bq
    kpq = bq // bk
    nheads = LANE // HEAD_DIM
    scale = HEAD_DIM ** -0.5
    grid = (nb, npair, nq)

    def body(*refs):
        q_ref, k_ref, v_ref, car_ref, do_ref = refs[:5]
        ex_ins = refs[5:5 + ex.n]
        dq_ref, dk_ref, dv_ref = refs[5 + ex.n:8 + ex.n]
        ex_outs = refs[8 + ex.n:8 + 2 * ex.n]
        dk_acc, dv_acc, dq_s, rune_s = refs[8 + 2 * ex.n:12 + 2 * ex.n]
        sems = refs[12 + 2 * ex.n:]
        first, last = _grid_ends(grid)

        @pl.when(first)
        def _():
            ex.start(ex_ins, ex_outs, sems)

        qi = pl.program_id(2)

        @pl.when(qi == 0)
        def _():
            dk_acc[...] = jnp.zeros_like(dk_acc)
            dv_acc[...] = jnp.zeros_like(dv_acc)

        q = q_ref[...]
        dov = do_ref[...]
        lane = lax.broadcasted_iota(jnp.int32, (1, LANE), 1)
        row = lax.broadcasted_iota(jnp.int32, (bq, bk), 0)
        col = lax.broadcasted_iota(jnp.int32, (bq, bk), 1)
        trow = lax.broadcasted_iota(jnp.int32, (bk, bk), 0)
        tcol = lax.broadcasted_iota(jnp.int32, (bk, bk), 1)
        tri_suf = (trow > tcol).astype(BF16)
        tri_pre = (trow < tcol).astype(BF16)
        hms = [(lane // HEAD_DIM) == hh for hh in range(nheads)]
        qhs = [jnp.where(hm, q, jnp.zeros_like(q)) * scale for hm in hms]
        dohs = [jnp.where(hm, dov, jnp.zeros_like(dov)) for hm in hms]
        dq_s[...] = jnp.zeros_like(dq_s)
        rune_s[...] = jnp.zeros_like(rune_s)

        def step(jj, carry, masked):
            js = [ATT_UNROLL * jj + u for u in range(ATT_UNROLL)]
            k0s = [pl.multiple_of(j * bk, bk) for j in js]
            kjs = [k_ref[pl.ds(k0, bk), :] for k0 in k0s]
            vjs = [v_ref[pl.ds(k0, bk), :] for k0 in k0s]
            masks = [(k0 + col) < (qi * bq + row) for k0 in k0s] if masked else None
            car = car_ref[...]
            zs = [[lax.dot_general(qhs[hh], kj, _NT, preferred_element_type=F32) for kj in kjs]
                  for hh in range(nheads)]
            das = [[lax.dot_general(dohs[hh], vj, _NT, preferred_element_type=F32) for vj in vjs]
                   for hh in range(nheads)]
            dk_blk = [None] * ATT_UNROLL
            dv_blk = [None] * ATT_UNROLL
            for hh in range(nheads):
                run_e = rune_s[hh]
                dq = None
                for u in range(ATT_UNROLL):
                    z = zs[hh][u]
                    lb = _log_sigmoid(z)
                    l1u = lb - z
                    l1 = jnp.where(masks[u], l1u, 0.0) if masked else l1u
                    run = jnp.sum(jnp.where(lane == hh * HEAD_DIM + js[u], car, 0.0), axis=1, keepdims=True)
                    suf = _split_dot(l1, tri_suf) + run
                    a = jnp.exp(lb + suf)
                    if masked:
                        a = jnp.where(masks[u], a, 0.0)
                    e = das[hh][u] * a
                    pre_e = _split_dot(e, tri_pre) + run_e
                    dz = e * jnp.exp(l1u) - pre_e * jnp.exp(lb)
                    if masked:
                        dz = jnp.where(masks[u], dz, 0.0)
                    dzb = dz.astype(BF16)
                    dqu = jnp.dot(dzb, kjs[u], preferred_element_type=F32)
                    dq = dqu if dq is None else dq + dqu
                    dkh = lax.dot_general(dzb, qhs[hh], _TN, preferred_element_type=F32)
                    dvh = lax.dot_general(a.astype(BF16), dohs[hh], _TN, preferred_element_type=F32)
                    dk_blk[u] = dkh if hh == 0 else dk_blk[u] + dkh
                    dv_blk[u] = dvh if hh == 0 else dv_blk[u] + dvh
                    run_e = run_e + jnp.sum(e, axis=1, keepdims=True)
                dq_s[hh] += dq
                rune_s[hh] = run_e
            for u in range(ATT_UNROLL):
                dk_acc[pl.ds(k0s[u], bk), :] += dk_blk[u]
                dv_acc[pl.ds(k0s[u], bk), :] += dv_blk[u]
            return carry

        ntrip = (qi + 1) * kpq // ATT_UNROLL
        lax.fori_loop(0, ntrip - 1, functools.partial(step, masked=False), 0)
        step(ntrip - 1, 0, True)
        dq_out = dq_s[0]
        for hh in range(1, nheads):
            dq_out = jnp.where(hms[hh], dq_s[hh], dq_out)
        dq_ref[...] = (dq_out * scale).astype(dq_ref.dtype)

        @pl.when(qi == nq - 1)
        def _():
            dk_ref[...] = dk_acc[...].astype(dk_ref.dtype)
            dv_ref[...] = dv_acc[...].astype(dv_ref.dtype)

        @pl.when(last)
        def _():
            ex.wait(ex_ins, ex_outs, sems)

    qspec = pl.BlockSpec((bq, LANE), lambda b, p, i: (b * nq + i, p))
    kvout = pl.BlockSpec((s, LANE), lambda b, p, i: (b, p))
    sds = jax.ShapeDtypeStruct((t, d), BF16)
    res = pl.pallas_call(
        body, name="attn_bwd", grid=grid,
        in_specs=[qspec,
                  pl.BlockSpec((s, LANE), lambda b, p, i: (b, npair + p)),
                  pl.BlockSpec((s, LANE), lambda b, p, i: (b, 2 * npair + p)),
                  qspec, qspec] + ex.specs,
        out_specs=[qspec, kvout, kvout] + ex.specs, out_shape=[sds, sds, sds] + ex.out_shape,
        scratch_shapes=[pltpu.VMEM((s, LANE), F32), pltpu.VMEM((s, LANE), F32),
                        pltpu.VMEM((nheads, bq, LANE), F32), pltpu.VMEM((nheads, bq, LANE), F32)] + ex.scratch,
        compiler_params=_cp(("arbitrary", "arbitrary", "arbitrary")))(qkv, qkv, qkv, car, do, *ex.arrs)
    return res[0], res[1], res[2], list(res[3:])


def _conv3(u_ref, w, bias, c, r0, rc):
    x = u_ref[pl.ds(r0, rc), :].astype(F32)
    p0 = pl.multiple_of(jnp.maximum(r0 - 16, 0), 16)
    prev = u_ref[pl.ds(p0, 16), :].astype(F32)
    prev = jnp.where(c > 0, prev, 0.0)
    row = lax.broadcasted_iota(jnp.int32, (rc, 1), 0)
    s1 = jnp.where(row == 0, prev[15:16, :], pltpu.roll(x, 1, 0))
    s2 = jnp.where(row == 0, prev[14:15, :], jnp.where(row == 1, prev[15:16, :], pltpu.roll(x, 2, 0)))
    cv = w[2:3, :] * x + w[1:2, :] * s1 + w[0:1, :] * s2 + bias
    return cv, x, s1, s2


def _sigmoid(x):
    return 1.0 / (1.0 + jnp.exp(-x))


def _ffn_act_fwd(name, up8, cw8, cb8, nb, s):
    _, t, c_w = up8.shape
    rc = _tile(s, 256)
    nch = s // rc
    half = N_DEV // 2

    def body(ug_ref, uv_ref, wg_ref, wv_ref, bg_ref, bv_ref, act_ref):
        wg, wv, bg, bv = wg_ref[...], wv_ref[...], bg_ref[...], bv_ref[...]

        def chunk(c, carry):
            r0 = pl.multiple_of(c * rc, rc)
            cg = _conv3(ug_ref, wg, bg, c, r0, rc)[0]
            cv = _conv3(uv_ref, wv, bv, c, r0, rc)[0]
            act_ref[pl.ds(r0, rc), :] = (cg * _sigmoid(cg) * cv).astype(act_ref.dtype)
            return carry

        lax.fori_loop(0, nch, chunk, 0)

    def slab(off):
        return pl.BlockSpec((None, s, c_w), lambda k, b: (k + off, b, 0))

    def par(rows, off):
        return pl.BlockSpec((None, rows, c_w), lambda k, b: (k + off, 0, 0))

    return pl.pallas_call(
        body, name=name, grid=(half, nb),
        in_specs=[slab(0), slab(half), par(3, 0), par(3, half), par(1, 0), par(1, half)],
        out_specs=pl.BlockSpec((None, s, c_w), lambda k, b: (k, b, 0)),
        out_shape=jax.ShapeDtypeStruct((half, t, c_w), BF16),
        compiler_params=_cp(("parallel", "parallel")))(up8, up8, cw8, cw8, cb8, cb8)


def _ffn_act_bwd(name, up8, dact4, cw8, cb8, nb, s):
    _, t, c_w = up8.shape
    rc = _tile(s, 256)
    nch = s // rc
    half = N_DEV // 2

    def body(u_ref, da_ref, w_ref, b_ref, dup_ref, dcw_ref, dcb_ref):
        w2, b2 = w_ref[...], b_ref[...]
        row = lax.broadcasted_iota(jnp.int32, (rc, 1), 0)

        @pl.when(pl.program_id(1) == 0)
        def _():
            dcw_ref[...] = jnp.zeros_like(dcw_ref)
            dcb_ref[...] = jnp.zeros_like(dcb_ref)

        def chunk(i, carry):
            c = nch - 1 - i
            r0 = pl.multiple_of(c * rc, rc)
            convs = [_conv3(u_ref.at[h], w2[h], b2[h], c, r0, rc) for h in range(2)]
            gt, vl = convs[0][0], convs[1][0]
            da = da_ref[pl.ds(r0, rc), :].astype(F32)
            sg = _sigmoid(gt)
            dcvs = [da * vl * sg * (1.0 + gt * (1.0 - sg)), da * gt * sg]
            out = []
            for h in range(2):
                n0, n1, a0, a1, a2, ab = carry[6 * h:6 * h + 6]
                dcv, (_, x, s1, s2), w = dcvs[h], convs[h], w2[h]
                t1 = jnp.where(row == rc - 1, n0, pltpu.roll(dcv, rc - 1, 0))
                t2 = jnp.where(row == rc - 2, n0, jnp.where(row == rc - 1, n1, pltpu.roll(dcv, rc - 2, 0)))
                dup = w[2:3, :] * dcv + w[1:2, :] * t1 + w[0:1, :] * t2
                dup_ref[h, pl.ds(r0, rc), :] = dup.astype(dup_ref.dtype)
                out += [dcv[0:1, :], dcv[1:2, :],
                        a0 + jnp.sum(dcv * s2, axis=0, keepdims=True), a1 + jnp.sum(dcv * s1, axis=0, keepdims=True),
                        a2 + jnp.sum(dcv * x, axis=0, keepdims=True), ab + jnp.sum(dcv, axis=0, keepdims=True)]
            return tuple(out)

        z = jnp.zeros((1, c_w), F32)
        fin = lax.fori_loop(0, nch, chunk, (z,) * 12)
        for h in range(2):
            _, _, a0, a1, a2, ab = fin[6 * h:6 * h + 6]
            dcw_ref[h, 0:1, :] += a0
            dcw_ref[h, 1:2, :] += a1
            dcw_ref[h, 2:3, :] += a2
            dcb_ref[h] += ab

    def pair(rows, per_seq):
        return pl.BlockSpec((2, None, rows, c_w), (lambda k, b: (0, k, b, 0)) if per_seq else (lambda k, b: (0, k, 0, 0)))

    four = lambda a: a.reshape((2, half) + a.shape[1:])
    dup, dcw, dcb = pl.pallas_call(
        body, name=name, grid=(half, nb),
        in_specs=[pair(s, True), pl.BlockSpec((None, s, c_w), lambda k, b: (k, b, 0)), pair(3, False), pair(1, False)],
        out_specs=[pair(s, True), pair(3, False), pair(1, False)],
        out_shape=[jax.ShapeDtypeStruct((2, half, t, c_w), BF16), jax.ShapeDtypeStruct((2, half, 3, c_w), F32),
                   jax.ShapeDtypeStruct((2, half, 1, c_w), F32)],
        compiler_params=_cp(("parallel", "arbitrary")))(four(up8), dact4, four(cw8), four(cb8))
    return dup.reshape(N_DEV, t, c_w), dcw.reshape(N_DEV, 3, c_w), dcb.reshape(N_DEV, 1, c_w)


_GELU_C0 = math.sqrt(2.0 / math.pi)
_GELU_C1 = 0.044715


def _rowwise(name, body, ins, in_kinds, out_kinds, t, d, tr_pref=512):
    tr = _tile(t, tr_pref)
    row = pl.BlockSpec((tr, d), lambda i: (i, 0))
    vec = pl.BlockSpec((1, d), lambda i: (0, 0))
    in_specs = [row if k == "row" else vec for k in in_kinds]
    out_specs = [row if k[0] == "row" else vec for k in out_kinds]
    out_shape = [jax.ShapeDtypeStruct((t, d) if k[0] == "row" else (1, d), k[1]) for k in out_kinds]
    has_acc = any(k[0] == "acc" for k in out_kinds)
    return pl.pallas_call(body, name=name, grid=(t // tr,), in_specs=in_specs, out_specs=out_specs,
                          out_shape=out_shape,
                          compiler_params=_cp(("arbitrary",) if has_acc else ("parallel",)))(*ins)


def _ssm_post_fwd(ys, u, dskip):
    t, d = ys.shape

    def body(ys_ref, u_ref, ds_ref, y_ref, z_ref):
        y = ys_ref[...].astype(F32) + ds_ref[...] * u_ref[...].astype(F32)
        y_ref[...] = y
        th = jnp.tanh(_GELU_C0 * (y + _GELU_C1 * y * y * y))
        z_ref[...] = (0.5 * y * (1.0 + th)).astype(z_ref.dtype)

    return _rowwise("ssm_post_fwd", body, [ys, u, dskip], ["row", "row", "vec"],
                    [("row", F32), ("row", BF16)], t, d)


def _glu_fwd(z, gl, bglu):
    t, d = z.shape

    def body(z_ref, gl_ref, b_ref, o_ref):
        o_ref[...] = (z_ref[...].astype(F32) * _sigmoid(gl_ref[...] + b_ref[...])).astype(o_ref.dtype)

    return _rowwise("glu_fwd", body, [z, gl, bglu], ["row", "row", "vec"], [("row", BF16)], t, d)[0]


def _glu_bwd(dgg, z, gl, bglu):
    t, d = z.shape

    def body(dg_ref, z_ref, gl_ref, b_ref, dgl_ref, dz_ref, db_ref):
        sg = _sigmoid(gl_ref[...] + b_ref[...])
        dg = dg_ref[...]
        dgl = dg * z_ref[...].astype(F32) * sg * (1.0 - sg)
        dgl_ref[...] = dgl.astype(dgl_ref.dtype)
        dz_ref[...] = dg * sg

        @pl.when(pl.program_id(0) == 0)
        def _():
            db_ref[...] = jnp.zeros_like(db_ref)

        db_ref[...] += jnp.sum(dgl, axis=0, keepdims=True)

    return _rowwise("glu_bwd", body, [dgg, z, gl, bglu], ["row", "row", "row", "vec"],
                    [("row", BF16), ("row", F32), ("acc", F32)], t, d)


def _ssm_post_bwd(dz1, dz2, y, u, dskip):
    t, d = y.shape

    def body(a_ref, b_ref, y_ref, u_ref, ds_ref, dy_ref, du_ref, dd_ref):
        yv = y_ref[...]
        inner = _GELU_C0 * (yv + _GELU_C1 * yv * yv * yv)
        th = jnp.tanh(inner)
        dgelu = 0.5 * (1.0 + th) + 0.5 * yv * (1.0 - th * th) * _GELU_C0 * (1.0 + 3.0 * _GELU_C1 * yv * yv)
        dy = (a_ref[...] + b_ref[...]) * dgelu
        dy_ref[...] = dy.astype(dy_ref.dtype)
        du_ref[...] = dy * ds_ref[...]

        @pl.when(pl.program_id(0) == 0)
        def _():
            dd_ref[...] = jnp.zeros_like(dd_ref)

        dd_ref[...] += jnp.sum(dy * u_ref[...].astype(F32), axis=0, keepdims=True)

    return _rowwise("ssm_post_bwd", body, [dz1, dz2, y, u, dskip], ["row", "row", "row", "row", "vec"],
                    [("row", BF16), ("row", F32), ("acc", F32)], t, d)


def _add_cast(a, b):
    t, d = a.shape

    def body(a_ref, b_ref, o_ref):
        o_ref[...] = (a_ref[...].astype(F32) + b_ref[...].astype(F32)).astype(o_ref.dtype)

    return _rowwise("add_cast", body, [a, b], ["row", "row"], [("row", BF16)], t, d)[0]


def _ssm_scan(e_re, e_im, lam_re, lam_im, nb):
    r, n = e_re.shape
    nc = r // nb
    cb = _tile(n, 512)

    def body(er_ref, ei_ref, lr_ref, li_ref, xr_ref, xi_ref):
        lr, li = lr_ref[...], li_ref[...]
        rid = lax.broadcasted_iota(jnp.int32, (8, 1), 0)
        for b in range(nb):
            def tile(i, carry, b=b):
                xr, xi = carry
                r0 = pl.multiple_of(b * nc + i * 8, 8)
                er, ei = er_ref[pl.ds(r0, 8), :], ei_ref[pl.ds(r0, 8), :]
                outr, outi = jnp.zeros((8, cb), F32), jnp.zeros((8, cb), F32)
                for j in range(8):
                    outr = jnp.where(rid == j, xr, outr)
                    outi = jnp.where(rid == j, xi, outi)
                    xr, xi = lr * xr - li * xi + er[j:j + 1, :], li * xr + lr * xi + ei[j:j + 1, :]
                xr_ref[pl.ds(r0, 8), :] = outr
                xi_ref[pl.ds(r0, 8), :] = outi
                return xr, xi

            z = jnp.zeros((1, cb), F32)
            lax.fori_loop(0, nc // 8, tile, (z, z))

    mat = pl.BlockSpec((r, cb), lambda j: (0, j))
    vec = pl.BlockSpec((1, cb), lambda j: (0, j))
    sds = jax.ShapeDtypeStruct((r, n), F32)
    return pl.pallas_call(body, name="ssm_scan", grid=(n // cb,), in_specs=[mat, mat, vec, vec],
                          out_specs=[mat, mat], out_shape=[sds, sds],
                          compiler_params=_cp(("parallel",)))(e_re, e_im, lam_re, lam_im)


def _ssm_scan_bwd(dxp_re, dxp_im, xp_re, xp_im, lam_re, lam_im, nb):
    r, n = dxp_re.shape
    nc = r // nb
    cb = _tile(n, 512)

    def body(dr_ref, di_ref, xr_ref, xi_ref, lr_ref, li_ref, er_ref, ei_ref, dlr_ref, dli_ref):
        lr, li = lr_ref[...], li_ref[...]
        rid = lax.broadcasted_iota(jnp.int32, (8, 1), 0)
        z = jnp.zeros((1, cb), F32)
        alr, ali = z, z
        for b in range(nb):
            def tile(i, carry, b=b):
                gr, gi, alr, ali = carry
                r0 = pl.multiple_of(b * nc + (nc // 8 - 1 - i) * 8, 8)
                dr, di = dr_ref[pl.ds(r0, 8), :], di_ref[pl.ds(r0, 8), :]
                xr, xi = xr_ref[pl.ds(r0, 8), :], xi_ref[pl.ds(r0, 8), :]
                outr, outi = jnp.zeros((8, cb), F32), jnp.zeros((8, cb), F32)
                for j in range(7, -1, -1):
                    outr = jnp.where(rid == j, gr, outr)
                    outi = jnp.where(rid == j, gi, outi)
                    xrj, xij = xr[j:j + 1, :], xi[j:j + 1, :]
                    alr = alr + gr * xrj + gi * xij
                    ali = ali + gi * xrj - gr * xij
                    gr, gi = dr[j:j + 1, :] + lr * gr + li * gi, di[j:j + 1, :] + lr * gi - li * gr
                er_ref[pl.ds(r0, 8), :] = outr
                ei_ref[pl.ds(r0, 8), :] = outi
                return gr, gi, alr, ali

            _, _, alr, ali = lax.fori_loop(0, nc // 8, tile, (z, z, alr, ali))
        dlr_ref[...] = alr
        dli_ref[...] = ali

    mat = pl.BlockSpec((r, cb), lambda j: (0, j))
    vec = pl.BlockSpec((1, cb), lambda j: (0, j))
    sds = jax.ShapeDtypeStruct((r, n), F32)
    vds = jax.ShapeDtypeStruct((1, n), F32)
    return pl.pallas_call(body, name="ssm_scan_bwd", grid=(n // cb,), in_specs=[mat, mat, mat, mat, vec, vec],
                          out_specs=[mat, mat, vec, vec], out_shape=[sds, sds, vds, vds],
                          compiler_params=_cp(("parallel",)))(dxp_re, dxp_im, xp_re, xp_im, lam_re, lam_im)


def _ssm_operators(a_re, a_im, log_dt, b_re, b_im, c_re, c_im):
    g, p = a_re.shape
    h = b_re.shape[-1]
    ln = SSM_L
    hp = lax.Precision.HIGHEST
    lam = lax.complex(a_re, a_im)
    ldt = lam * jnp.exp(log_dt)[:, None]
    lam_bar = jnp.exp(ldt)
    bbar = ((lam_bar - 1.0) / lam)[..., None] * lax.complex(b_re, b_im)
    cm = lax.complex(c_re, c_im)
    steps = jnp.arange(ln + 1, dtype=F32)
    pw = jnp.exp(ldt[:, None, :] * steps[None, :, None])
    kd = jnp.einsum("ghp,gdp,gpk->gdhk", cm, pw[:, :ln], bbar, precision=hp).real
    sig = jnp.arange(ln)[:, None]
    tau = jnp.arange(ln)[None, :]
    lag = tau - sig
    sel = (lag[None] == jnp.arange(ln)[:, None, None]).astype(F32)
    tm = jnp.einsum("gdhk,dst->gsthk", kd, sel, precision=hp)
    tm = tm.transpose(0, 1, 4, 2, 3).reshape(g, ln * h, ln * h)
    wx = pw[:, ln - 1 - jnp.arange(ln)][:, :, :, None] * bbar[:, None]
    wx = wx.transpose(0, 1, 3, 2).reshape(g, ln * h, p)
    cp = cm[:, None] * pw[:, 1:ln + 1][:, :, None, :]
    cp = cp.reshape(g, ln * h, p).transpose(0, 2, 1)
    odd = (jnp.arange(g) % 2 == 1)[:, None, None]

    def pad(m, axis):
        z = jnp.zeros_like(m)
        return jnp.where(odd, jnp.concatenate([z, m], axis), jnp.concatenate([m, z], axis))

    lam_l = pw[:, ln]
    return (tm, pad(wx.real, 2), pad(wx.imag, 2), pad(cp.real, 1), pad(-cp.imag, 1),
            lam_l.real.reshape(1, g * p), lam_l.imag.reshape(1, g * p))


def _to_groups(a, nb, s):
    d = a.shape[1]
    g = d // SSM_H
    return a.reshape(nb, s // SSM_L, SSM_L, g, SSM_H).transpose(3, 0, 1, 2, 4).reshape(g, nb * (s // SSM_L), SSM_L * SSM_H)


def _from_groups(a, nb, s):
    g = a.shape[0]
    return a.reshape(g, nb, s // SSM_L, SSM_L, SSM_H).transpose(1, 2, 3, 0, 4).reshape(nb * s, g * SSM_H)


def _ssm_core_fwd(ug, ops, nb):
    tm, wxr, wxi, wyr, wyi, lam_re, lam_im = ops
    g, r, w = ug.shape
    n = g * SSM_P
    grp = lambda off: pl.BlockSpec((None, r, w), lambda j: (2 * j + off, 0, 0))
    wxs = lambda off: pl.BlockSpec((None, w, LANE), lambda j: (2 * j + off, 0, 0))
    pair_out = pl.BlockSpec((r, LANE), lambda j: (0, j))
    sds = jax.ShapeDtypeStruct((r, n), F32)
    e_re = _mm_terms("ssm_e_re", [(ug, grp(0), wxr, wxs(0), _NN), (ug, grp(1), wxr, wxs(1), _NN)], (g // 2,), pair_out, sds)
    e_im = _mm_terms("ssm_e_im", [(ug, grp(0), wxi, wxs(0), _NN), (ug, grp(1), wxi, wxs(1), _NN)], (g // 2,), pair_out, sds)
    xp_re, xp_im = _ssm_scan(e_re, e_im, lam_re, lam_im, nb)
    one = pl.BlockSpec((None, r, w), lambda j: (j, 0, 0))
    sq = pl.BlockSpec((None, w, w), lambda j: (j, 0, 0))
    pair_in = pl.BlockSpec((r, LANE), lambda j: (0, j // 2))
    wys = pl.BlockSpec((None, LANE, w), lambda j: (j, 0, 0))
    y = _mm_terms("ssm_y", [(ug, one, tm, sq, _NN), (xp_re, pair_in, wyr, wys, _NN), (xp_im, pair_in, wyi, wys, _NN)],
                  (g,), one, jax.ShapeDtypeStruct((g, r, w), BF16))
    return y, xp_re, xp_im


def _ssm_core_bwd(dyg, ug, xp_re, xp_im, ops, nb):
    tm, wxr, wxi, wyr, wyi, lam_re, lam_im = ops
    g, r, w = ug.shape
    n = g * SSM_P
    grp = lambda off: pl.BlockSpec((None, r, w), lambda j: (2 * j + off, 0, 0))
    wyp = lambda off: pl.BlockSpec((None, LANE, w), lambda j: (2 * j + off, 0, 0))
    pair_out = pl.BlockSpec((r, LANE), lambda j: (0, j))
    sds = jax.ShapeDtypeStruct((r, n), F32)
    dxp_re = _mm_terms("ssm_dxp_re", [(dyg, grp(0), wyr, wyp(0), _NT), (dyg, grp(1), wyr, wyp(1), _NT)], (g // 2,), pair_out, sds)
    dxp_im = _mm_terms("ssm_dxp_im", [(dyg, grp(0), wyi, wyp(0), _NT), (dyg, grp(1), wyi, wyp(1), _NT)], (g // 2,), pair_out, sds)
    de_re, de_im, dlam_re, dlam_im = _ssm_scan_bwd(dxp_re, dxp_im, xp_re, xp_im, lam_re, lam_im, nb)
    one = pl.BlockSpec((None, r, w), lambda j: (j, 0, 0))
    sq = pl.BlockSpec((None, w, w), lambda j: (j, 0, 0))
    pair_in = pl.BlockSpec((r, LANE), lambda j: (0, j // 2))
    wxs = pl.BlockSpec((None, w, LANE), lambda j: (j, 0, 0))
    wys = pl.BlockSpec((None, LANE, w), lambda j: (j, 0, 0))
    du = _mm_terms("ssm_du", [(dyg, one, tm, sq, _NT), (de_re, pair_in, wxr, wxs, _NT), (de_im, pair_in, wxi, wxs, _NT)],
                   (g,), one, jax.ShapeDtypeStruct((g, r, w), BF16))
    dtm = _mm_terms("ssm_dtm", [(ug, one, dyg, one, _TN)], (g,), sq, jax.ShapeDtypeStruct((g, w, w), F32))
    dwy = lambda nm, xp: _mm_terms(nm, [(xp, pair_in, dyg, one, _TN)], (g,), wys, jax.ShapeDtypeStruct((g, LANE, w), F32))
    dwx = lambda nm, de: _mm_terms(nm, [(ug, one, de, pair_in, _TN)], (g,), wxs, jax.ShapeDtypeStruct((g, w, LANE), F32))
    return du, (dtm, dwx("ssm_dwx_re", de_re), dwx("ssm_dwx_im", de_im), dwy("ssm_dwy_re", xp_re),
                dwy("ssm_dwy_im", xp_im), dlam_re, dlam_im)


def _modfin_fwd(c_all, w_mod, w_fin):
    n, d = c_all.shape
    nl, _, cm = w_mod.shape
    cf = w_fin.shape[1]
    width = nl * cm + cf
    hp = lax.Precision.HIGHEST

    def body(c_ref, wm_ref, wf_ref, act_ref, out_ref):
        cv = c_ref[...]
        act = cv * _sigmoid(cv)
        act_ref[...] = act
        for i in range(nl):
            out_ref[:, i * cm:(i + 1) * cm] = jnp.dot(act, wm_ref[i], preferred_element_type=F32, precision=hp)
        out_ref[:, nl * cm:] = jnp.dot(act, wf_ref[...], preferred_element_type=F32, precision=hp)

    return pl.pallas_call(body, name="modfin_fwd",
                          out_shape=[jax.ShapeDtypeStruct((n, d), F32), jax.ShapeDtypeStruct((n, width), F32)],
                          compiler_params=_cp(None))(c_all, w_mod, w_fin)


def _modfin_bwd(c_act_t, dmod_loc, dfin_loc, dall):
    d, n = c_act_t.shape
    nl, _, cm = dmod_loc.shape
    cf = dfin_loc.shape[1]
    hp = lax.Precision.HIGHEST

    def body(ct_ref, dm_ref, df_ref, da_ref, gwm_ref, gwf_ref, gb_ref):
        ct = ct_ref[...]
        for i in range(nl):
            gwm_ref[i] = jnp.dot(ct, dm_ref[i], preferred_element_type=F32, precision=hp)
        gwf_ref[...] = jnp.dot(ct, df_ref[...], preferred_element_type=F32, precision=hp)
        gb_ref[...] = jnp.sum(da_ref[...], axis=0, keepdims=True)

    return pl.pallas_call(body, name="modfin_bwd",
                          out_shape=[jax.ShapeDtypeStruct((nl, d, cm), F32), jax.ShapeDtypeStruct((d, cf), F32),
                                     jax.ShapeDtypeStruct((1, dall.shape[1]), F32)],
                          compiler_params=_cp(None))(c_act_t, dmod_loc, dfin_loc, dall)


def _adamw(name, gparts, w, m, v):
    n, r, c = gparts.shape
    tr = _tile(r, 256)
    c1 = 1.0 / (1.0 - ADAM_B1 ** ADAM_STEP)
    c2 = 1.0 / (1.0 - ADAM_B2 ** ADAM_STEP)

    def body(gp_ref, w_ref, m_ref, v_ref, g_ref, d_ref, mo_ref, vo_ref):
        gsum = gp_ref[0].astype(F32)
        for j in range(1, n):
            gsum = gsum + gp_ref[j].astype(F32)
        mn = ADAM_B1 * m_ref[...] + (1.0 - ADAM_B1) * gsum
        vn = ADAM_B2 * v_ref[...] + (1.0 - ADAM_B2) * (gsum * gsum)
        g_ref[...] = gsum
        mo_ref[...] = mn
        vo_ref[...] = vn
        d_ref[...] = -ADAM_LR * ((mn * c1) / (jnp.sqrt(vn * c2) + ADAM_EPS) + ADAM_WD * w_ref[...])

    mat = pl.BlockSpec((tr, c), lambda i: (i, 0))
    sds = jax.ShapeDtypeStruct((r, c), F32)
    return pl.pallas_call(body, name=name, grid=(r // tr,),
                          in_specs=[pl.BlockSpec((n, tr, c), lambda i: (0, i, 0)), mat, mat, mat],
                          out_specs=[mat] * 4, out_shape=[sds] * 4,
                          compiler_params=_cp(("parallel",)))(gparts, w, m, v)


def _sum_parts(name, parts):
    n, r, c = parts.shape

    def body(p_ref, o_ref):
        acc = p_ref[0]
        for j in range(1, n):
            acc = acc + p_ref[j]
        o_ref[...] = acc

    return pl.pallas_call(body, name=name, out_shape=jax.ShapeDtypeStruct((r, c), F32),
                          compiler_params=_cp(None))(parts)


class _Exchange:
    def __init__(self, arrs, gathers):
        self.arrs = [pltpu.with_memory_space_constraint(a, pltpu.HBM) for a in arrs]
        self.gathers = list(gathers)
        self.n = len(arrs)
        self.out_shape = [pltpu.HBM(((N_DEV,) + a.shape) if g else a.shape, a.dtype)
                          for a, g in zip(arrs, self.gathers)]
        self.specs = [pl.BlockSpec(memory_space=pltpu.HBM)] * self.n
        self.scratch = [pltpu.SemaphoreType.DMA((self.n, N_DEV - 1)), pltpu.SemaphoreType.DMA((self.n, N_DEV - 1)),
                        pltpu.SemaphoreType.DMA((self.n,))]

    def _copies(self, ins, outs, sems):
        send_sems, recv_sems, local_sems = sems
        x, y, c = lax.axis_index("x"), lax.axis_index("y"), lax.axis_index("c")
        me = 4 * x + 2 * y + c
        local, sends, recvs = [], [], []
        for i in range(self.n):
            src_me = ins[i] if self.gathers[i] else ins[i].at[me]
            local.append(pltpu.make_async_copy(src_me, outs[i].at[me], local_sems.at[i]))
        for dd in range(1, N_DEV):
            px = jnp.bitwise_xor(x, dd >> 2)
            py = jnp.bitwise_xor(y, (dd >> 1) & 1)
            pc = jnp.bitwise_xor(c, dd & 1)
            pid = 4 * px + 2 * py + pc
            for i in range(self.n):
                src = ins[i] if self.gathers[i] else ins[i].at[pid]
                sems_i = dict(send_sem=send_sems.at[i, dd - 1], recv_sem=recv_sems.at[i, dd - 1],
                              device_id=(px, py, pc), device_id_type=MESH)
                sends.append(pltpu.make_async_remote_copy(src_ref=src, dst_ref=outs[i].at[me], **sems_i))
                recvs.append(pltpu.make_async_remote_copy(src_ref=src, dst_ref=outs[i].at[pid], **sems_i))
        return local, sends, recvs

    def start(self, ins, outs, sems):
        local, sends, _ = self._copies(ins, outs, sems)
        for cp in local + sends:
            cp.start()

    def wait(self, ins, outs, sems):
        local, sends, recvs = self._copies(ins, outs, sems)
        for cp in recvs:
            cp.wait_recv()
        for cp in sends:
            cp.wait_send()
        for cp in local:
            cp.wait()


class _NoExchange:
    n, arrs, specs, out_shape, scratch = 0, [], [], [], []

    def start(self, ins, outs, sems):
        pass

    def wait(self, ins, outs, sems):
        pass


def _exchange(name, arrs, gathers):
    ex = _Exchange(arrs, gathers)
    n = ex.n

    def body(*refs):
        ins, outs, sems = refs[:n], refs[n:2 * n], refs[2 * n:]
        ex.start(ins, outs, sems)
        ex.wait(ins, outs, sems)

    outs = pl.pallas_call(body, name=name, in_specs=ex.specs, out_specs=ex.specs, out_shape=ex.out_shape,
                          scratch_shapes=ex.scratch)(*ex.arrs)
    return list(outs)


def _pack(pieces):
    flat = jnp.concatenate([p.reshape(-1) for p in pieces])
    pad = (-flat.shape[0]) % (8 * LANE)
    return jnp.pad(flat, (0, pad)).reshape(-1, LANE)


def _unpack(packed, shapes):
    flat = packed.reshape(-1)
    out, off = [], 0
    for shp in shapes:
        sz = math.prod(shp)
        out.append(flat[off:off + sz].reshape(shp))
        off += sz
    return out


def kernel(x, c, norm_mix, norm_ffn, w_mod, b_mod, w_qkv, w_o_attn, w_in_ssm, a_re, a_im, log_dt, b_re, b_im, c_re, c_im, d_skip, w_glu, b_glu, w_o_ssm, w_up, conv_w, conv_b, w_down, norm_out, w_fin, b_fin, loss_target, m_norm_mix, m_norm_ffn, m_w_mod, m_b_mod, m_w_qkv, m_w_o_attn, m_w_in_ssm, m_a_re, m_a_im, m_log_dt, m_b_re, m_b_im, m_c_re, m_c_im, m_d_skip, m_w_glu, m_b_glu, m_w_o_ssm, m_w_up, m_conv_w, m_conv_b, m_w_down, m_norm_out, m_w_fin, m_b_fin, v_norm_mix, v_norm_ffn, v_w_mod, v_b_mod, v_w_qkv, v_w_o_attn, v_w_in_ssm, v_a_re, v_a_im, v_log_dt, v_b_re, v_b_im, v_c_re, v_c_im, v_d_skip, v_w_glu, v_b_glu, v_w_o_ssm, v_w_up, v_conv_w, v_conv_b, v_w_down, v_norm_out, v_w_fin, v_b_fin):
    nb, s, d = x.shape
    t = nb * s
    n_seq = nb * N_DEV
    me = 4 * lax.axis_index("x") + 2 * lax.axis_index("y") + lax.axis_index("c")
    cm = w_mod.shape[2]
    cf = w_fin.shape[1]
    c_up = w_up.shape[2]
    r_dn = w_down.shape[1]
    g_ssm = d // SSM_H

    wq8, c8 = _exchange("gather_first", [w_qkv[0].astype(BF16), c], [True, True])
    later = _Exchange([w_o_attn[0].astype(BF16), w_in_ssm[0].astype(BF16), w_glu[0].astype(BF16),
                       w_o_ssm[0].astype(BF16), w_up[0].astype(BF16), w_up[1].astype(BF16),
                       w_down[0].astype(BF16), w_down[1].astype(BF16), conv_w, d_skip, b_glu], [True] * 11)
    half = N_DEV // 2
    cb_l = [conv_b[i].reshape(N_DEV, 1, c_up) for i in range(2)]
    c_all = c8.reshape(n_seq, d)

    c_act, modloc = _modfin_fwd(c_all, w_mod, w_fin)
    (mod8,) = _exchange("gather_mod", [modloc], [True])
    mine = lax.dynamic_slice_in_dim(mod8, me * nb, nb, axis=1)
    mods = []
    for i in range(2):
        mi = mine[:, :, i * cm:(i + 1) * cm].transpose(1, 0, 2).reshape(nb, N_DEV * cm) + b_mod[i]
        mods.append([mi[:, j * d:(j + 1) * d].reshape(nb, 1, d) for j in range(6)])
    fin = mine[:, :, 2 * cm:].transpose(1, 0, 2).reshape(nb, N_DEV * cf) + b_fin
    sh_f, sc_f = fin[:, :d].reshape(nb, 1, d), fin[:, d:].reshape(nb, 1, d)

    row = lambda a: a.reshape(1, -1)
    x0 = x.reshape(t, d)

    def ffn_fwd(i, xin, sh, sc, gate):
        h = _norm_mod_fwd(f"ffn{i}_norm", xin, row(norm_ffn[i]), sh, sc, nb)
        up = _mm(f"ffn{i}_up", h, wup8[i], (t // tm_, N_DEV, 1),
                 pl.BlockSpec((tm_, d), lambda a, b, k: (a, 0)), pl.BlockSpec((None, d, c_up), lambda a, b, k: (b, 0, 0)),
                 pl.BlockSpec((None, tm_, c_up), lambda a, b, k: (b, a, 0)),
                 jax.ShapeDtypeStruct((N_DEV, t, c_up), BF16), _NN, (tm_, c_up))
        act = _ffn_act_fwd(f"ffn{i}_act", up, cw_l[i], cb_l[i], nb, s)
        yf = _mm(f"ffn{i}_down", act, wd4[i], (t // tm_, 1, half),
                 pl.BlockSpec((None, tm_, c_up), lambda a, b, k: (k, a, 0)),
                 pl.BlockSpec((None, c_up, d), lambda a, b, k: (k, 0, 0)),
                 pl.BlockSpec((tm_, d), lambda a, b, k: (a, 0)), jax.ShapeDtypeStruct((t, d), F32), _NN, (tm_, d))
        xout = _gate_add(f"ffn{i}_res", xin, yf, gate, nb)
        return xout, (h, up, act, yf)

    tm_ = _tile(t, 512)
    sh1, sc1, g1, sh2, sc2, g2 = mods[0]
    h1 = _norm_mod_fwd("attn_norm", x0, row(norm_mix[0]), sh1, sc1, nb)
    cq = wq8.shape[2]
    qkv = _mm("attn_qkv", h1, wq8, (t // tm_, N_DEV, 1),
              pl.BlockSpec((tm_, d), lambda a, b, k: (a, 0)), pl.BlockSpec((None, d, cq), lambda a, b, k: (b, 0, 0)),
              pl.BlockSpec((tm_, cq), lambda a, b, k: (a, b)), jax.ShapeDtypeStruct((t, 3 * d), BF16), _NN, (tm_, cq))
    o_att, car_att, (wo8, win8, wglu8, wos8, wup8_0, wup8_1, wd8_0, wd8_1, cw8, dskip8, bglu8) = _attn_fwd(
        qkv, nb, s, d, later)
    wo = wo8.reshape(d, d)
    win = win8.reshape(d, d)
    wglu = wglu8.reshape(d, d)
    wos = wos8.reshape(d, d)
    wup8 = [wup8_0, wup8_1]
    wd4 = [wd8_0.reshape(half, 2 * r_dn, d), wd8_1.reshape(half, 2 * r_dn, d)]
    cw_l = [cw8[:, 0], cw8[:, 1]]
    dskip_f = dskip8.reshape(1, d)
    bglu_f = bglu8.reshape(1, d)
    ya = _mm_nn("attn_out", o_att, wo, F32)
    x1 = _gate_add("attn_res", x0, ya, g1, nb)
    x2, ffn0_saved = ffn_fwd(0, x1, sh2, sc2, g2)

    sh1b, sc1b, g1b, sh2b, sc2b, g2b = mods[1]
    ops = _ssm_operators(a_re[0], a_im[0], log_dt[0], b_re[0], b_im[0], c_re[0], c_im[0])
    h3 = _norm_mod_fwd("ssm_norm", x2, row(norm_mix[1]), sh1b, sc1b, nb)
    u = _mm_nn("ssm_in", h3, win, BF16)
    ug = _to_groups(u, nb, s)
    yg, xp_re, xp_im = _ssm_core_fwd(ug, ops, nb)
    y_ssm, z_ssm = _ssm_post_fwd(_from_groups(yg, nb, s), u, dskip_f)
    gl = _mm_nn("ssm_glu", z_ssm, wglu, F32)
    gg = _glu_fwd(z_ssm, gl, bglu_f)
    ys2 = _mm_nn("ssm_out", gg, wos, F32)
    x3 = _gate_add("ssm_res", x2, ys2, g1b, nb)
    x4, ffn1_saved = ffn_fwd(1, x3, sh2b, sc2b, g2b)

    dx4, g_norm_out, dsh_f, dsc_f, loss_blk = _norm_mod_bwd(
        "final_norm", None, x4, row(norm_out), sh_f, sc_f, None, loss_target.reshape(t, d), nb)
    loss = lax.psum(loss_blk[0, 0], ("x", "y", "c"))

    def ffn_bwd(i, dxo, xin, sc, gate, saved):
        h, up, act, yf = saved
        dyf, dgate = _gate_bwd(f"ffn{i}_res_bwd", dxo, yf, gate, nb)
        dact = _mm(f"ffn{i}_down_dx", dyf, wd4[i], (t // tm_, half, 1),
                   pl.BlockSpec((tm_, d), lambda a, b, k: (a, 0)), pl.BlockSpec((None, c_up, d), lambda a, b, k: (b, 0, 0)),
                   pl.BlockSpec((None, tm_, c_up), lambda a, b, k: (b, a, 0)),
                   jax.ShapeDtypeStruct((half, t, c_up), BF16), _NT, (tm_, c_up))
        tk = _tile(t, 1024)
        gwd = _mm(f"ffn{i}_down_dw", act, dyf, (half, 1, t // tk),
                  pl.BlockSpec((None, tk, c_up), lambda a, b, k: (a, k, 0)), pl.BlockSpec((tk, d), lambda a, b, k: (k, 0)),
                  pl.BlockSpec((None, c_up, d), lambda a, b, k: (a, 0, 0)),
                  jax.ShapeDtypeStruct((half, c_up, d), BF16), _TN, (c_up, d))
        dup, dcw, dcb = _ffn_act_bwd(f"ffn{i}_act_bwd", up, dact, cw_l[i], cb_l[i], nb, s)
        dh = _mm(f"ffn{i}_up_dx", dup, wup8[i], (t // tm_, 1, N_DEV),
                 pl.BlockSpec((None, tm_, c_up), lambda a, b, k: (k, a, 0)),
                 pl.BlockSpec((None, d, c_up), lambda a, b, k: (k, 0, 0)),
                 pl.BlockSpec((tm_, d), lambda a, b, k: (a, 0)), jax.ShapeDtypeStruct((t, d), F32), _NT, (tm_, d))
        gwup = _mm(f"ffn{i}_up_dw", h, dup, (1, N_DEV, t // tk),
                   pl.BlockSpec((tk, d), lambda a, b, k: (k, 0)), pl.BlockSpec((None, tk, c_up), lambda a, b, k: (b, k, 0)),
                   pl.BlockSpec((None, d, c_up), lambda a, b, k: (b, 0, 0)),
                   jax.ShapeDtypeStruct((N_DEV, d, c_up), BF16), _TN, (d, c_up))
        dxi, g_norm, dsh, dsc = _norm_mod_bwd(f"ffn{i}_norm_bwd", dh, xin, row(norm_ffn[i]), None, sc, dxo, None, nb)
        return dxi, (gwup, gwd.reshape(N_DEV, r_dn, d), dcw, dcb, g_norm, dsh, dsc, dgate)

    dx3, (gwup1, gwd1, dcw1, dcb1, g_nffn1, dsh2b, dsc2b, dg2b) = ffn_bwd(1, dx4, x3, sc2b, g2b, ffn1_saved)

    dys2, dg1b = _gate_bwd("ssm_res_bwd", dx3, ys2, g1b, nb)
    dgg = _mm_nt("ssm_out_dx", dys2, wos, F32)
    gwos = _mm_tn("ssm_out_dw", gg, dys2, BF16)
    dgl, dz1, g_bglu = _glu_bwd(dgg, z_ssm, gl, bglu_f)
    dz2 = _mm_nt("ssm_glu_dx", dgl, wglu, F32)
    gwglu = _mm_tn("ssm_glu_dw", z_ssm, dgl, BF16)
    dy_ssm, du_skip, g_dskip = _ssm_post_bwd(dz1, dz2, y_ssm, u, dskip_f)
    dug, d_ops = _ssm_core_bwd(_to_groups(dy_ssm, nb, s), ug, xp_re, xp_im, ops, nb)
    du = _add_cast(_from_groups(dug, nb, s), du_skip)
    dh3 = _mm_nt("ssm_in_dx", du, win, F32)
    gwin = _mm_tn("ssm_in_dw", h3, du, BF16)
    dx2, g_nmix1, dsh1b, dsc1b = _norm_mod_bwd("ssm_norm_bwd", dh3, x2, row(norm_mix[1]), None, sc1b, dx3, None, nb)
    _, ops_vjp = jax.vjp(_ssm_operators, a_re[0], a_im[0], log_dt[0], b_re[0], b_im[0], c_re[0], c_im[0])
    g_ssm_params = ops_vjp(d_ops)

    dx1, (gwup0, gwd0, dcw0, dcb0, g_nffn0, dsh2, dsc2, dg2) = ffn_bwd(0, dx2, x1, sc2, g2, ffn0_saved)

    dya, dg1 = _gate_bwd("attn_res_bwd", dx1, ya, g1, nb)
    do_att = _mm_nt("attn_out_dx", dya, wo, BF16)
    gwo = _mm_tn("attn_out_dw", o_att, dya, BF16)
    rows8 = lambda a: a.reshape(N_DEV, d // N_DEV, d)
    early = _Exchange([rows8(gwo), rows8(gwin), rows8(gwglu), rows8(gwos), gwup0, gwup1, gwd0, gwd1,
                       _pack(g_ssm_params)], [False] * 8 + [True])
    dq, dk, dv, (ro, rin, rglu, ros, rup0, rup1, rd0, rd1, ssm8) = _attn_bwd(qkv, car_att, do_att, nb, s, d, early)
    dqkv = jnp.concatenate([dq, dk, dv], axis=1)
    dh1 = _mm("attn_qkv_dx", dqkv, wq8, (t // tm_, 1, N_DEV),
              pl.BlockSpec((tm_, cq), lambda a, b, k: (a, k)),
              pl.BlockSpec((None, d, cq), lambda a, b, k: (k, 0, 0)),
              pl.BlockSpec((tm_, d), lambda a, b, k: (a, 0)), jax.ShapeDtypeStruct((t, d), F32), _NT, (tm_, d))
    tk = _tile(t, 1024)
    gwq8 = _mm("attn_qkv_dw", h1, dqkv, (1, N_DEV, t // tk),
               pl.BlockSpec((tk, d), lambda a, b, k: (k, 0)),
               pl.BlockSpec((tk, cq), lambda a, b, k: (k, b)),
               pl.BlockSpec((None, d, cq), lambda a, b, k: (b, 0, 0)),
               jax.ShapeDtypeStruct((N_DEV, d, cq), BF16), _TN, (d, cq))
    dx0, g_nmix0, dsh1, dsc1 = _norm_mod_bwd("attn_norm_bwd", dh1, x0, row(norm_mix[0]), None, sc1, dx1, None, nb)
    grad_x = dx0.reshape(nb, s, d)

    dmod =[jnp.concatenate([a.reshape(nb, d) for a in grp], axis=1) for grp in
            ([dsh1, dsc1, dg1, dsh2, dsc2, dg2], [dsh1b, dsc1b, dg1b, dsh2b, dsc2b, dg2b])]
    dfin = jnp.concatenate([dsh_f.reshape(nb, d), dsc_f.reshape(nb, d)], axis=1)
    dmodfin = jnp.concatenate(dmod + [dfin], axis=1)
    ssm_shapes = [(g_ssm, SSM_P), (g_ssm, SSM_P), (g_ssm,), (g_ssm, SSM_P, SSM_H), (g_ssm, SSM_P, SSM_H),
                  (g_ssm, SSM_H, SSM_P), (g_ssm, SSM_H, SSM_P)]
    small_shapes = [(2, d), (2, d), (d,), (2, N_DEV * c_up), (d,), (d,), (2, N_DEV, 3, c_up)]
    small_partial = _pack([jnp.stack([g_nmix0, g_nmix1]), jnp.stack([g_nffn0, g_nffn1]), g_norm_out,
                           jnp.stack([dcb0, dcb1]), g_dskip, g_bglu, jnp.stack([dcw0, dcw1])])
    rq, dmf8, small8 = _exchange("exchange_last", [gwq8, dmodfin, small_partial], [False, True, True])
    (g_norm_mix, g_norm_ffn, g_norm_out_s, g_conv_b, g_dskip_full, g_bglu_full, g_cw_full) = _unpack(
        _sum_parts("sum_small_grads", small8), small_shapes)
    g_a_re, g_a_im, g_log_dt, g_b_re, g_b_im, g_c_re, g_c_im = _unpack(_sum_parts("sum_ssm_grads", ssm8), ssm_shapes)

    dall = dmf8.reshape(n_seq, 14 * d)
    dmod_loc = jnp.stack([lax.dynamic_slice_in_dim(dall[:, i * 6 * d:(i + 1) * 6 * d], me * cm, cm, axis=1)
                          for i in range(2)])
    dfin_loc = lax.dynamic_slice_in_dim(dall[:, 12 * d:], me * cf, cf, axis=1)
    g_w_mod, g_w_fin, g_bias = _modfin_bwd(c_act.T, dmod_loc, dfin_loc, dall)
    g_b_mod = g_bias[0, :12 * d].reshape(2, 6 * d)
    g_b_fin = g_bias[0, 12 * d:]

    def big(name, parts, w, m, v):
        shp = w.shape
        r2 = lambda a: a.reshape(-1, shp[-1])
        res = _adamw(name, parts.reshape(parts.shape[0], -1, shp[-1]), r2(w), r2(m), r2(v))
        return [a.reshape(shp) for a in res]

    upd = {}
    upd["w_mod"] = big("adamw_w_mod", g_w_mod[None], w_mod, m_w_mod, v_w_mod)
    upd["w_fin"] = big("adamw_w_fin", g_w_fin[None], w_fin, m_w_fin, v_w_fin)
    upd["w_qkv"] = big("adamw_w_qkv", rq, w_qkv, m_w_qkv, v_w_qkv)
    upd["w_o_attn"] = big("adamw_w_o_attn", ro, w_o_attn, m_w_o_attn, v_w_o_attn)
    upd["w_in_ssm"] = big("adamw_w_in_ssm", rin, w_in_ssm, m_w_in_ssm, v_w_in_ssm)
    upd["w_glu"] = big("adamw_w_glu", rglu, w_glu, m_w_glu, v_w_glu)
    upd["w_o_ssm"] = big("adamw_w_o_ssm", ros, w_o_ssm, m_w_o_ssm, v_w_o_ssm)
    up_l = [big(f"adamw_w_up{i}", r, w_up[i], m_w_up[i], v_w_up[i]) for i, r in enumerate((rup0, rup1))]
    upd["w_up"] = [jnp.stack([up_l[0][j], up_l[1][j]]) for j in range(4)]
    dn_l = [big(f"adamw_w_down{i}", r, w_down[i], m_w_down[i], v_w_down[i]) for i, r in enumerate((rd0, rd1))]
    upd["w_down"] = [jnp.stack([dn_l[0][j], dn_l[1][j]]) for j in range(4)]

    g_dskip_loc = lax.dynamic_slice_in_dim(g_dskip_full.reshape(1, d), me * (d // N_DEV), d // N_DEV, axis=1)
    g_bglu_loc = lax.dynamic_slice_in_dim(g_bglu_full.reshape(1, d), me * (d // N_DEV), d // N_DEV, axis=1)
    g_cw_loc = lax.dynamic_slice_in_dim(g_cw_full, me, 1, axis=1).reshape(2, 3, c_up)
    small_names = ["norm_mix", "norm_ffn", "b_mod", "a_re", "a_im", "log_dt", "b_re", "b_im", "c_re", "c_im",
                   "d_skip", "b_glu", "conv_w", "conv_b", "norm_out", "b_fin"]
    small_g = [g_norm_mix, g_norm_ffn, g_b_mod, g_a_re[None], g_a_im[None], g_log_dt[None], g_b_re[None], g_b_im[None],
               g_c_re[None], g_c_im[None], g_dskip_loc, g_bglu_loc, g_cw_loc, g_conv_b, g_norm_out_s, g_b_fin]
    small_w = [norm_mix, norm_ffn, b_mod, a_re, a_im, log_dt, b_re, b_im, c_re, c_im, d_skip, b_glu, conv_w, conv_b,
               norm_out, b_fin]
    small_m = [m_norm_mix, m_norm_ffn, m_b_mod, m_a_re, m_a_im, m_log_dt, m_b_re, m_b_im, m_c_re, m_c_im, m_d_skip,
               m_b_glu, m_conv_w, m_conv_b, m_norm_out, m_b_fin]
    small_v = [v_norm_mix, v_norm_ffn, v_b_mod, v_a_re, v_a_im, v_log_dt, v_b_re, v_b_im, v_c_re, v_c_im, v_d_skip,
               v_b_glu, v_conv_w, v_conv_b, v_norm_out, v_b_fin]
    shapes = [w.shape for w in small_w]
    res = _adamw("adamw_small", _pack(small_g)[None], _pack(small_w), _pack(small_m), _pack(small_v))
    res = [_unpack(r, shapes) for r in res]
    for j, nm in enumerate(small_names):
        upd[nm] = [res[k][j] for k in range(4)]

    order = ["norm_mix", "norm_ffn", "w_mod", "b_mod", "w_qkv", "w_o_attn", "w_in_ssm", "a_re", "a_im", "log_dt",
             "b_re", "b_im", "c_re", "c_im", "d_skip", "w_glu", "b_glu", "w_o_ssm", "w_up", "conv_w", "conv_b",
             "w_down", "norm_out", "w_fin", "b_fin"]
    outs = [loss, grad_x]
    for k in range(4):
        outs += [upd[nm][k] for nm in order]
    return tuple(outs)
```

```python
import functools
import math

import jax
import jax.numpy as jnp
from jax import lax
from jax.experimental import pallas as pl
from jax.experimental.pallas import tpu as pltpu

F32 = jnp.float32
BF16 = jnp.bfloat16
MESH = pl.DeviceIdType.MESH

N_DEV = 8
HEAD_DIM = 64
ATT_BLK = 128
ATT_BQ = 256
ATT_UNROLL = 2
SSM_H = 16
SSM_P = 64
SSM_L = 4
EPS = 1e-6
ADAM_LR, ADAM_B1, ADAM_B2, ADAM_EPS, ADAM_WD, ADAM_STEP = 0.001, 0.9, 0.999, 1e-08, 0.01, 10
V7X_VMEM_LIMIT = 56 * 1024 * 1024
LANE = 128

_NN = (((1,), (0,)), ((), ()))
_NT = (((1,), (1,)), ((), ()))
_TN = (((0,), (0,)), ((), ()))


def _cp(sem):
    return pltpu.CompilerParams(dimension_semantics=sem, vmem_limit_bytes=V7X_VMEM_LIMIT)


def _tile(n, pref):
    if n <= pref:
        return n
    t = pref - pref % 16
    while t >= 16:
        if n % t == 0:
            return t
        t -= 16
    return n


def _mm(name, a, b, grid, a_spec, b_spec, out_spec, out_shape, dims, acc_shape):
    nk = grid[-1]
    kax = len(grid) - 1

    def body(a_ref, b_ref, o_ref, acc_ref):
        k = pl.program_id(kax)

        @pl.when(k == 0)
        def _():
            acc_ref[...] = jnp.zeros(acc_shape, F32)

        acc_ref[...] += lax.dot_general(a_ref[...].astype(BF16), b_ref[...].astype(BF16), dims,
                                        preferred_element_type=F32)

        @pl.when(k == nk - 1)
        def _():
            o_ref[...] = acc_ref[...].astype(o_ref.dtype)

    return pl.pallas_call(
        body, name=name, grid=grid, in_specs=[a_spec, b_spec], out_specs=out_spec, out_shape=out_shape,
        scratch_shapes=[pltpu.VMEM(acc_shape, F32)],
        compiler_params=_cp(("parallel",) * kax + ("arbitrary",)))(a, b)


def _mm_nn(name, a, w, out_dtype):
    m, k = a.shape
    n = w.shape[1]
    tm, tn, tk = _tile(m, 512), _tile(n, 1024), _tile(k, 1024)
    return _mm(name, a, w, (m // tm, n // tn, k // tk),
               pl.BlockSpec((tm, tk), lambda i, j, kk: (i, kk)), pl.BlockSpec((tk, tn), lambda i, j, kk: (kk, j)),
               pl.BlockSpec((tm, tn), lambda i, j, kk: (i, j)), jax.ShapeDtypeStruct((m, n), out_dtype), _NN, (tm, tn))


def _mm_nt(name, a, w, out_dtype):
    m, n = a.shape
    k = w.shape[0]
    tm, tko, tn = _tile(m, 512), _tile(k, 1024), _tile(n, 1024)
    return _mm(name, a, w, (m // tm, k // tko, n // tn),
               pl.BlockSpec((tm, tn), lambda i, j, kk: (i, kk)), pl.BlockSpec((tko, tn), lambda i, j, kk: (j, kk)),
               pl.BlockSpec((tm, tko), lambda i, j, kk: (i, j)), jax.ShapeDtypeStruct((m, k), out_dtype), _NT, (tm, tko))


def _mm_tn(name, a, b, out_dtype):
    t, m = a.shape
    n = b.shape[1]
    tm, tn, tk = _tile(m, 512), _tile(n, 1024), _tile(t, 1024)
    return _mm(name, a, b, (m // tm, n // tn, t // tk),
               pl.BlockSpec((tk, tm), lambda i, j, kk: (kk, i)), pl.BlockSpec((tk, tn), lambda i, j, kk: (kk, j)),
               pl.BlockSpec((tm, tn), lambda i, j, kk: (i, j)), jax.ShapeDtypeStruct((m, n), out_dtype), _TN, (tm, tn))


def _norm_mod_fwd(name, x, g, shift, scale, nb):
    t, d = x.shape
    s = t // nb
    tr = _tile(s, 512)
    nt = s // tr

    def body(x_ref, g_ref, sh_ref, sc_ref, h_ref):
        xv = x_ref[...]
        r = lax.rsqrt(jnp.mean(xv * xv, axis=-1, keepdims=True) + EPS)
        y = xv * r * g_ref[...]
        h_ref[...] = (y * (1.0 + sc_ref[...]) + sh_ref[...]).astype(h_ref.dtype)

    row = pl.BlockSpec((tr, d), lambda b, i: (b * nt + i, 0))
    vec = pl.BlockSpec((None, 1, d), lambda b, i: (b, 0, 0))
    return pl.pallas_call(body, name=name, grid=(nb, nt),
                          in_specs=[row, pl.BlockSpec((1, d), lambda b, i: (0, 0)), vec, vec],
                          out_specs=row, out_shape=jax.ShapeDtypeStruct((t, d), BF16),
                          compiler_params=_cp(("parallel", "parallel")))(x, g, shift, scale)


def _norm_mod_bwd(name, dh, x, g, shift, scale, dres, target, nb):
    t, d = x.shape
    s = t // nb
    tr = _tile(s, 256)
    nt = s // tr
    final = target is not None

    def body(*refs):
        if final:
            x_ref, g_ref, sh_ref, sc_ref, tg_ref, dx_ref, dg_ref, dsh_ref, dsc_ref, loss_ref = refs
        else:
            dh_ref, x_ref, g_ref, sc_ref, dres_ref, dx_ref, dg_ref, dsh_ref, dsc_ref = refs
        b, i = pl.program_id(0), pl.program_id(1)
        xv = x_ref[...]
        gv = g_ref[...]
        r = lax.rsqrt(jnp.mean(xv * xv, axis=-1, keepdims=True) + EPS)
        nrm = xv * r
        y = nrm * gv
        one_sc = 1.0 + sc_ref[...]
        if final:
            err = y * one_sc + sh_ref[...] - tg_ref[...]
            dhv = err * (1.0 / d)
        else:
            dhv = dh_ref[...].astype(F32)
        dy = dhv * one_sc
        dn = dy * gv
        dxv = r * (dn - nrm * jnp.mean(dn * nrm, axis=-1, keepdims=True))
        if final:
            dx_ref[...] = dxv
        else:
            dx_ref[...] = dres_ref[...] + dxv

        @pl.when(i == 0)
        def _():
            dsh_ref[...] = jnp.zeros_like(dsh_ref)
            dsc_ref[...] = jnp.zeros_like(dsc_ref)

        @pl.when((i == 0) & (b == 0))
        def _():
            dg_ref[...] = jnp.zeros_like(dg_ref)
            if final:
                loss_ref[...] = jnp.zeros_like(loss_ref)

        dsh_ref[...] += jnp.sum(dhv, axis=0, keepdims=True)
        dsc_ref[...] += jnp.sum(dhv * y, axis=0, keepdims=True)
        dg_ref[...] += jnp.sum(dy * nrm, axis=0, keepdims=True)
        if final:
            loss_ref[...] += (0.5 / d) * jnp.sum(err * err)

    row = pl.BlockSpec((tr, d), lambda b, i: (b * nt + i, 0))
    vec = pl.BlockSpec((None, 1, d), lambda b, i: (b, 0, 0))
    gsp = pl.BlockSpec((1, d), lambda b, i: (0, 0))
    out_specs = [row, gsp, vec, vec]
    out_shape = [jax.ShapeDtypeStruct((t, d), F32), jax.ShapeDtypeStruct((1, d), F32),
                 jax.ShapeDtypeStruct((nb, 1, d), F32), jax.ShapeDtypeStruct((nb, 1, d), F32)]
    if final:
        ins, in_specs = [x, g, shift, scale, target], [row, gsp, vec, vec, row]
        out_specs.append(pl.BlockSpec((8, LANE), lambda b, i: (0, 0)))
        out_shape.append(jax.ShapeDtypeStruct((8, LANE), F32))
    else:
        ins, in_specs = [dh, x, g, scale, dres], [row, row, gsp, vec, row]
    return pl.pallas_call(body, name=name, grid=(nb, nt), in_specs=in_specs, out_specs=out_specs,
                          out_shape=out_shape, compiler_params=_cp(("arbitrary", "arbitrary")))(*ins)


def _gate_add(name, x, y, gate, nb):
    t, d = x.shape
    s = t // nb
    tr = _tile(s, 512)
    nt = s // tr

    def body(x_ref, y_ref, g_ref, o_ref):
        o_ref[...] = x_ref[...] + g_ref[...] * y_ref[...]

    row = pl.BlockSpec((tr, d), lambda b, i: (b * nt + i, 0))
    vec = pl.BlockSpec((None, 1, d), lambda b, i: (b, 0, 0))
    return pl.pallas_call(body, name=name, grid=(nb, nt), in_specs=[row, row, vec], out_specs=row,
                          out_shape=jax.ShapeDtypeStruct((t, d), F32),
                          compiler_params=_cp(("parallel", "parallel")))(x, y, gate)


def _gate_bwd(name, dx, y, gate, nb):
    t, d = dx.shape
    s = t // nb
    tr = _tile(s, 512)
    nt = s // tr

    def body(dx_ref, y_ref, g_ref, dy_ref, dg_ref):
        dxv = dx_ref[...]
        dy_ref[...] = (g_ref[...] * dxv).astype(dy_ref.dtype)

        @pl.when(pl.program_id(1) == 0)
        def _():
            dg_ref[...] = jnp.zeros_like(dg_ref)

        dg_ref[...] += jnp.sum(dxv * y_ref[...], axis=0, keepdims=True)

    row = pl.BlockSpec((tr, d), lambda b, i: (b * nt + i, 0))
    vec = pl.BlockSpec((None, 1, d), lambda b, i: (b, 0, 0))
    return pl.pallas_call(body, name=name, grid=(nb, nt), in_specs=[row, row, vec], out_specs=[row, vec],
                          out_shape=[jax.ShapeDtypeStruct((t, d), BF16), jax.ShapeDtypeStruct((nb, 1, d), F32)],
                          compiler_params=_cp(("parallel", "arbitrary")))(dx, y, gate)


def _log_sigmoid(z):
    return jnp.minimum(z, 0.0) - jnp.log(1.0 + jnp.exp(-jnp.abs(z)))


def _split_dot(v, tri):
    hi = v.astype(BF16)
    lo = (v - hi.astype(F32)).astype(BF16)
    return (jnp.dot(hi, tri, preferred_element_type=F32) + jnp.dot(lo, tri, preferred_element_type=F32))


def _grid_ends(grid):
    ids = [pl.program_id(a) for a in range(len(grid))]
    first = functools.reduce(lambda u, w: u & w, [i == 0 for i in ids])
    last = functools.reduce(lambda u, w: u & w, [i == n - 1 for i, n in zip(ids, grid)])
    return first, last


def _attn_fwd(qkv, nb, s, d, ex):
    t = nb * s
    npair = d // LANE
    bk = ATT_BLK
    bq = min(ATT_BQ, s)
    nq = s // bq
    kpq = bq // bk
    nheads = LANE // HEAD_DIM
    scale = HEAD_DIM ** -0.5
    grid = (nb, npair, nq)
    assert s // bk <= HEAD_DIM, "one carry lane per key block and head"
    assert bk == LANE, "the running sums are kept one 128-lane tile wide"
    assert kpq == ATT_UNROLL, "one loop trip covers exactly the key blocks under a query block's diagonal"

    def body(*refs):
        q_ref, k_ref, v_ref = refs[:3]
        ex_ins = refs[3:3 + ex.n]
        o_ref, car_ref = refs[3 + ex.n:5 + ex.n]
        ex_outs = refs[5 + ex.n:5 + 2 * ex.n]
        acc_s, run_s = refs[5 + 2 * ex.n:7 + 2 * ex.n]
        sems = refs[7 + 2 * ex.n:]
        first, last = _grid_ends(grid)

        @pl.when(first)
        def _():
            ex.start(ex_ins, ex_outs, sems)

        qi = pl.program_id(2)
        q = q_ref[...]
        lane = lax.broadcasted_iota(jnp.int32, (1, LANE), 1)
        row = lax.broadcasted_iota(jnp.int32, (bq, bk), 0)
        col = lax.broadcasted_iota(jnp.int32, (bq, bk), 1)
        trow = lax.broadcasted_iota(jnp.int32, (bk, bk), 0)
        tcol = lax.broadcasted_iota(jnp.int32, (bk, bk), 1)
        tri = (trow > tcol).astype(BF16)
        hms = [(lane // HEAD_DIM) == hh for hh in range(nheads)]
        qhs = [jnp.where(hm, q, jnp.zeros_like(q)) * scale for hm in hms]
        car_ref[...] = jnp.zeros((bq, LANE), F32)
        acc_s[...] = jnp.zeros_like(acc_s)
        run_s[...] = jnp.zeros_like(run_s)
        nkb = (qi + 1) * kpq

        def step(jj, carry, masked):
            js = [nkb - 1 - (ATT_UNROLL * jj + u) for u in range(ATT_UNROLL)]
            k0s = [pl.multiple_of(j * bk, bk) for j in js]
            kjs = [k_ref[pl.ds(k0, bk), :] for k0 in k0s]
            vjs = [v_ref[pl.ds(k0, bk), :] for k0 in k0s]
            masks = [(k0 + col) < (qi * bq + row) for k0 in k0s] if masked else None
            zs = [[lax.dot_general(qhs[hh], kj, _NT, preferred_element_type=F32) for kj in kjs]
                  for hh in range(nheads)]
            car = car_ref[...]
            for hh in range(nheads):
                run = run_s[hh]
                acc = None
                for u in range(ATT_UNROLL):
                    z = zs[hh][u]
                    lb = _log_sigmoid(z)
                    l1 = lb - z
                    if masked:
                        l1 = jnp.where(masks[u], l1, 0.0)
                    suf = _split_dot(l1, tri) + run
                    w = jnp.exp(lb + suf)
                    if masked:
                        w = jnp.where(masks[u], w, 0.0)
                    pv = jnp.dot(w.astype(BF16), vjs[u], preferred_element_type=F32)
                    acc = pv if acc is None else acc + pv
                    car = jnp.where(lane == hh * HEAD_DIM + js[u], run, car)
                    run = run + jnp.sum(l1, axis=1, keepdims=True)
                acc_s[hh] += acc
                run_s[hh] = run
            car_ref[...] = car
            return carry

        step(0, 0, True)
        lax.fori_loop(1, nkb // ATT_UNROLL, functools.partial(step, masked=False), 0)
        out = acc_s[0]
        for hh in range(1, nheads):
            out = jnp.where(hms[hh], acc_s[hh], out)
        o_ref[...] = out.astype(o_ref.dtype)

        @pl.when(last)
        def _():
            ex.wait(ex_ins, ex_outs, sems)

    qspec = pl.BlockSpec((bq, LANE), lambda b, p, i: (b * nq + i, p))
    res = pl.pallas_call(
        body, name="attn_fwd", grid=grid,
        in_specs=[qspec,
                  pl.BlockSpec((s, LANE), lambda b, p, i: (b, npair + p)),
                  pl.BlockSpec((s, LANE), lambda b, p, i: (b, 2 * npair + p))] + ex.specs,
        out_specs=[qspec, qspec] + ex.specs,
        out_shape=[jax.ShapeDtypeStruct((t, d), BF16), jax.ShapeDtypeStruct((t, d), F32)] + ex.out_shape,
        scratch_shapes=[pltpu.VMEM((nheads, bq, LANE), F32), pltpu.VMEM((nheads, bq, LANE), F32)] + ex.scratch,
        compiler_params=_cp(("arbitrary", "arbitrary", "arbitrary")))(qkv, qkv, qkv, *ex.arrs)
    return res[0], res[1], list(res[2:])


def _attn_bwd(qkv, car, do, nb, s, d, ex):
    t = nb * s
    npair = d // LANE
    bk = ATT_BLK
    bq = min(ATT_BQ, s)
    nq = s // bq
    kpq = bq // bk
    nheads = LANE // HEAD_DIM
    scale = HEAD_DIM ** -0.5
    grid = (nb, npair, nq)

    def body(*refs):
        q_ref, k_ref, v_ref, car_ref, do_ref = refs[:5]
        ex_ins = refs[5:5 + ex.n]
        dq_ref, dk_ref, dv_ref = refs[5 + ex.n:8 + ex.n]
        ex_outs = refs[8 + ex.n:8 + 2 * ex.n]
        dk_acc, dv_acc, dq_s, rune_s = refs[8 + 2 * ex.n:12 + 2 * ex.n]
        sems = refs[12 + 2 * ex.n:]
        first, last = _grid_ends(grid)

        @pl.when(first)
        def _():
            ex.start(ex_ins, ex_outs, sems)

        qi = pl.program_id(2)

        @pl.when(qi == 0)
        def _():
            dk_acc[...] = jnp.zeros_like(dk_acc)
            dv_acc[...] = jnp.zeros_like(dv_acc)

        q = q_ref[...]
        dov = do_ref[...]
        lane = lax.broadcasted_iota(jnp.int32, (1, LANE), 1)
        row = lax.broadcasted_iota(jnp.int32, (bq, bk), 0)
        col = lax.broadcasted_iota(jnp.int32, (bq, bk), 1)
        trow = lax.broadcasted_iota(jnp.int32, (bk, bk), 0)
        tcol = lax.broadcasted_iota(jnp.int32, (bk, bk), 1)
        tri_suf = (trow > tcol).astype(BF16)
        tri_pre = (trow < tcol).astype(BF16)
        hms = [(lane // HEAD_DIM) == hh for hh in range(nheads)]
        qhs = [jnp.where(hm, q, jnp.zeros_like(q)) * scale for hm in hms]
        dohs = [jnp.where(hm, dov, jnp.zeros_like(dov)) for hm in hms]
        dq_s[...] = jnp.zeros_like(dq_s)
        rune_s[...] = jnp.zeros_like(rune_s)

        def step(jj, carry, masked):
            js = [ATT_UNROLL * jj + u for u in range(ATT_UNROLL)]
            k0s = [pl.multiple_of(j * bk, bk) for j in js]
            kjs = [k_ref[pl.ds(k0, bk), :] for k0 in k0s]
            vjs = [v_ref[pl.ds(k0, bk), :] for k0 in k0s]
            masks = [(k0 + col) < (qi * bq + row) for k0 in k0s] if masked else None
            car = car_ref[...]
            zs = [[lax.dot_general(qhs[hh], kj, _NT, preferred_element_type=F32) for kj in kjs]
                  for hh in range(nheads)]
            das = [[lax.dot_general(dohs[hh], vj, _NT, preferred_element_type=F32) for vj in vjs]
                   for hh in range(nheads)]
            dk_blk = [None] * ATT_UNROLL
            dv_blk = [None] * ATT_UNROLL
            for hh in range(nheads):
                run_e = rune_s[hh]
                dq = None
                for u in range(ATT_UNROLL):
                    z = zs[hh][u]
                    lb = _log_sigmoid(z)
                    l1u = lb - z
                    l1 = jnp.where(masks[u], l1u, 0.0) if masked else l1u
                    run = jnp.sum(jnp.where(lane == hh * HEAD_DIM + js[u], car, 0.0), axis=1, keepdims=True)
                    suf = _split_dot(l1, tri_suf) + run
                    a = jnp.exp(lb + suf)
                    if masked:
                        a = jnp.where(masks[u], a, 0.0)
                    e = das[hh][u] * a
                    pre_e = _split_dot(e, tri_pre) + run_e
                    dz = e * jnp.exp(l1u) - pre_e * jnp.exp(lb)
                    if masked:
                        dz = jnp.where(masks[u], dz, 0.0)
                    dzb = dz.astype(BF16)
                    dqu = jnp.dot(dzb, kjs[u], preferred_element_type=F32)
                    dq = dqu if dq is None else dq + dqu
                    dkh = lax.dot_general(dzb, qhs[hh], _TN, preferred_element_type=F32)
                    dvh = lax.dot_general(a.astype(BF16), dohs[hh], _TN, preferred_element_type=F32)
                    dk_blk[u] = dkh if hh == 0 else dk_blk[u] + dkh
                    dv_blk[u] = dvh if hh == 0 else dv_blk[u] + dvh
                    run_e = run_e + jnp.sum(e, axis=1, keepdims=True)
                dq_s[hh] += dq
                rune_s[hh] = run_e
            for u in range(ATT_UNROLL):
                dk_acc[pl.ds(k0s[u], bk), :] += dk_blk[u]
                dv_acc[pl.ds(k0s[u], bk), :] += dv_blk[u]
            return carry

        ntrip = (qi + 1) * kpq // ATT_UNROLL
        lax.fori_loop(0, ntrip - 1, functools.partial(step, masked=False), 0)
        step(ntrip - 1, 0, True)
        dq_out = dq_s[0]
        for hh in range(1, nheads):
            dq_out = jnp.where(hms[hh], dq_s[hh], dq_out)
        dq_ref[...] = (dq_out * scale).astype(dq_ref.dtype)

        @pl.when(qi == nq - 1)
        def _():
            dk_ref[...] = dk_acc[...].astype(dk_ref.dtype)
            dv_ref[...] = dv_acc[...].astype(dv_ref.dtype)

        @pl.when(last)
        def _():
            ex.wait(ex_ins, ex_outs, sems)

    qspec = pl.BlockSpec((bq, LANE), lambda b, p, i: (b * nq + i, p))
    kvout = pl.BlockSpec((s, LANE), lambda b, p, i: (b, p))
    sds = jax.ShapeDtypeStruct((t, d), BF16)
    res = pl.pallas_call(
        body, name="attn_bwd", grid=grid,
        in_specs=[qspec,
                  pl.BlockSpec((s, LANE), lambda b, p, i: (b, npair + p)),
                  pl.BlockSpec((s, LANE), lambda b, p, i: (b, 2 * npair + p)),
                  qspec, qspec] + ex.specs,
        out_specs=[qspec, kvout, kvout] + ex.specs, out_shape=[sds, sds, sds] + ex.out_shape,
        scratch_shapes=[pltpu.VMEM((s, LANE), F32), pltpu.VMEM((s, LANE), F32),
                        pltpu.VMEM((nheads, bq, LANE), F32), pltpu.VMEM((nheads, bq, LANE), F32)] + ex.scratch,
        compiler_params=_cp(("arbitrary", "arbitrary", "arbitrary")))(qkv, qkv, qkv, car, do, *ex.arrs)
    return res[0], res[1], res[2], list(res[3:])


def _conv3(u_ref, w, bias, c, r0, rc):
    x = u_ref[pl.ds(r0, rc), :].astype(F32)
    p0 = pl.multiple_of(jnp.maximum(r0 - 16, 0), 16)
    prev = u_ref[pl.ds(p0, 16), :].astype(F32)
    prev = jnp.where(c > 0, prev, 0.0)
    row = lax.broadcasted_iota(jnp.int32, (rc, 1), 0)
    s1 = jnp.where(row == 0, prev[15:16, :], pltpu.roll(x, 1, 0))
    s2 = jnp.where(row == 0, prev[14:15, :], jnp.where(row == 1, prev[15:16, :], pltpu.roll(x, 2, 0)))
    cv = w[2:3, :] * x + w[1:2, :] * s1 + w[0:1, :] * s2 + bias
    return cv, x, s1, s2


def _sigmoid(x):
    return 1.0 / (1.0 + jnp.exp(-x))


def _ffn_act_fwd(name, up8, cw8, cb8, nb, s):
    _, t, c_w = up8.shape
    rc = _tile(s, 256)
    nch = s // rc
    half = N_DEV // 2

    def body(ug_ref, uv_ref, wg_ref, wv_ref, bg_ref, bv_ref, act_ref):
        wg, wv, bg, bv = wg_ref[...], wv_ref[...], bg_ref[...], bv_ref[...]

        def chunk(c, carry):
            r0 = pl.multiple_of(c * rc, rc)
            cg = _conv3(ug_ref, wg, bg, c, r0, rc)[0]
            cv = _conv3(uv_ref, wv, bv, c, r0, rc)[0]
            act_ref[pl.ds(r0, rc), :] = (cg * _sigmoid(cg) * cv).astype(act_ref.dtype)
            return carry

        lax.fori_loop(0, nch, chunk, 0)

    def slab(off):
        return pl.BlockSpec((None, s, c_w), lambda k, b: (k + off, b, 0))

    def par(rows, off):
        return pl.BlockSpec((None, rows, c_w), lambda k, b: (k + off, 0, 0))

    return pl.pallas_call(
        body, name=name, grid=(half, nb),
        in_specs=[slab(0), slab(half), par(3, 0), par(3, half), par(1, 0), par(1, half)],
        out_specs=pl.BlockSpec((None, s, c_w), lambda k, b: (k, b, 0)),
        out_shape=jax.ShapeDtypeStruct((half, t, c_w), BF16),
        compiler_params=_cp(("parallel", "parallel")))(up8, up8, cw8, cw8, cb8, cb8)


def _ffn_act_bwd(name, up8, dact4, cw8, cb8, nb, s):
    _, t, c_w = up8.shape
    rc = _tile(s, 256)
    nch = s // rc
    half = N_DEV // 2

    def body(u_ref, da_ref, w_ref, b_ref, dup_ref, dcw_ref, dcb_ref):
        w2, b2 = w_ref[...], b_ref[...]
        row = lax.broadcasted_iota(jnp.int32, (rc, 1), 0)

        @pl.when(pl.program_id(1) == 0)
        def _():
            dcw_ref[...] = jnp.zeros_like(dcw_ref)
            dcb_ref[...] = jnp.zeros_like(dcb_ref)

        def chunk(i, carry):
            c = nch - 1 - i
            r0 = pl.multiple_of(c * rc, rc)
            convs = [_conv3(u_ref.at[h], w2[h], b2[h], c, r0, rc) for h in range(2)]
            gt, vl = convs[0][0], convs[1][0]
            da = da_ref[pl.ds(r0, rc), :].astype(F32)
            sg = _sigmoid(gt)
            dcvs = [da * vl * sg * (1.0 + gt * (1.0 - sg)), da * gt * sg]
            out = []
            for h in range(2):
                n0, n1, a0, a1, a2, ab = carry[6 * h:6 * h + 6]
                dcv, (_, x, s1, s2), w = dcvs[h], convs[h], w2[h]
                t1 = jnp.where(row == rc - 1, n0, pltpu.roll(dcv, rc - 1, 0))
                t2 = jnp.where(row == rc - 2, n0, jnp.where(row == rc - 1, n1, pltpu.roll(dcv, rc - 2, 0)))
                dup = w[2:3, :] * dcv + w[1:2, :] * t1 + w[0:1, :] * t2
                dup_ref[h, pl.ds(r0, rc), :] = dup.astype(dup_ref.dtype)
                out += [dcv[0:1, :], dcv[1:2, :],
                        a0 + jnp.sum(dcv * s2, axis=0, keepdims=True), a1 + jnp.sum(dcv * s1, axis=0, keepdims=True),
                        a2 + jnp.sum(dcv * x, axis=0, keepdims=True), ab + jnp.sum(dcv, axis=0, keepdims=True)]
            return tuple(out)

        z = jnp.zeros((1, c_w), F32)
        fin = lax.fori_loop(0, nch, chunk, (z,) * 12)
        for h in range(2):
            _, _, a0, a1, a2, ab = fin[6 * h:6 * h + 6]
            dcw_ref[h, 0:1, :] += a0
            dcw_ref[h, 1:2, :] += a1
            dcw_ref[h, 2:3, :] += a2
            dcb_ref[h] += ab

    def pair(rows, per_seq):
        return pl.BlockSpec((2, None, rows, c_w), (lambda k, b: (0, k, b, 0)) if per_seq else (lambda k, b: (0, k, 0, 0)))

    four = lambda a: a.reshape((2, half) + a.shape[1:])
    dup, dcw, dcb = pl.pallas_call(
        body, name=name, grid=(half, nb),
        in_specs=[pair(s, True), pl.BlockSpec((None, s, c_w), lambda k, b: (k, b, 0)), pair(3, False), pair(1, False)],
        out_specs=[pair(s, True), pair(3, False), pair(1, False)],
        out_shape=[jax.ShapeDtypeStruct((2, half, t, c_w), BF16), jax.ShapeDtypeStruct((2, half, 3, c_w), F32),
                   jax.ShapeDtypeStruct((2, half, 1, c_w), F32)],
        compiler_params=_cp(("parallel", "arbitrary")))(four(up8), dact4, four(cw8), four(cb8))
    return dup.reshape(N_DEV, t, c_w), dcw.reshape(N_DEV, 3, c_w), dcb.reshape(N_DEV, 1, c_w)


_GELU_C0 = math.sqrt(2.0 / math.pi)
_GELU_C1 = 0.044715


def _rowwise(name, body, ins, in_kinds, out_kinds, t, d, tr_pref=512):
    tr = _tile(t, tr_pref)
    row = pl.BlockSpec((tr, d), lambda i: (i, 0))
    vec = pl.BlockSpec((1, d), lambda i: (0, 0))
    in_specs = [row if k == "row" else vec for k in in_kinds]
    out_specs = [row if k[0] == "row" else vec for k in out_kinds]
    out_shape = [jax.ShapeDtypeStruct((t, d) if k[0] == "row" else (1, d), k[1]) for k in out_kinds]
    has_acc = any(k[0] == "acc" for k in out_kinds)
    return pl.pallas_call(body, name=name, grid=(t // tr,), in_specs=in_specs, out_specs=out_specs,
                          out_shape=out_shape,
                          compiler_params=_cp(("arbitrary",) if has_acc else ("parallel",)))(*ins)


def _ssm_post_fwd(ys, u, dskip):
    t, d = ys.shape

    def body(ys_ref, u_ref, ds_ref, y_ref, z_ref):
        y = ys_ref[...].astype(F32) + ds_ref[...] * u_ref[...].astype(F32)
        y_ref[...] = y
        th = jnp.tanh(_GELU_C0 * (y + _GELU_C1 * y * y * y))
        z_ref[...] = (0.5 * y * (1.0 + th)).astype(z_ref.dtype)

    return _rowwise("ssm_post_fwd", body, [ys, u, dskip], ["row", "row", "vec"],
                    [("row", F32), ("row", BF16)], t, d)


def _glu_fwd(z, gl, bglu):
    t, d = z.shape

    def body(z_ref, gl_ref, b_ref, o_ref):
        o_ref[...] = (z_ref[...].astype(F32) * _sigmoid(gl_ref[...] + b_ref[...])).astype(o_ref.dtype)

    return _rowwise("glu_fwd", body, [z, gl, bglu], ["row", "row", "vec"], [("row", BF16)], t, d)[0]


def _glu_bwd(dgg, z, gl, bglu):
    t, d = z.shape

    def body(dg_ref, z_ref, gl_ref, b_ref, dgl_ref, dz_ref, db_ref):
        sg = _sigmoid(gl_ref[...] + b_ref[...])
        dg = dg_ref[...]
        dgl = dg * z_ref[...].astype(F32) * sg * (1.0 - sg)
        dgl_ref[...] = dgl.astype(dgl_ref.dtype)
        dz_ref[...] = dg * sg

        @pl.when(pl.program_id(0) == 0)
        def _():
            db_ref[...] = jnp.zeros_like(db_ref)

        db_ref[...] += jnp.sum(dgl, axis=0, keepdims=True)

    return _rowwise("glu_bwd", body, [dgg, z, gl, bglu], ["row", "row", "row", "vec"],
                    [("row", BF16), ("row", F32), ("acc", F32)], t, d)


def _ssm_post_bwd(dz1, dz2, y, u, dskip):
    t, d = y.shape

    def body(a_ref, b_ref, y_ref, u_ref, ds_ref, dy_ref, du_ref, dd_ref):
        yv = y_ref[...]
        inner = _GELU_C0 * (yv + _GELU_C1 * yv * yv * yv)
        th = jnp.tanh(inner)
        dgelu = 0.5 * (1.0 + th) + 0.5 * yv * (1.0 - th * th) * _GELU_C0 * (1.0 + 3.0 * _GELU_C1 * yv * yv)
        dy = (a_ref[...] + b_ref[...]) * dgelu
        dy_ref[...] = dy.astype(dy_ref.dtype)
        du_ref[...] = dy * ds_ref[...]

        @pl.when(pl.program_id(0) == 0)
        def _():
            dd_ref[...] = jnp.zeros_like(dd_ref)

        dd_ref[...] += jnp.sum(dy * u_ref[...].astype(F32), axis=0, keepdims=True)

    return _rowwise("ssm_post_bwd", body, [dz1, dz2, y, u, dskip], ["row", "row", "row", "row", "vec"],
                    [("row", BF16), ("row", F32), ("acc", F32)], t, d)


def _add_cast(a, b):
    t, d = a.shape

    def body(a_ref, b_ref, o_ref):
        o_ref[...] = (a_ref[...].astype(F32) + b_ref[...].astype(F32)).astype(o_ref.dtype)

    return _rowwise("add_cast", body, [a, b], ["row", "row"], [("row", BF16)], t, d)[0]


def _ssm_scan(e_re, e_im, lam_re, lam_im, nb):
    r, n = e_re.shape
    nc = r // nb
    cb = _tile(n, 512)

    def body(er_ref, ei_ref, lr_ref, li_ref, xr_ref, xi_ref):
        lr, li = lr_ref[...], li_ref[...]
        rid = lax.broadcasted_iota(jnp.int32, (8, 1), 0)
        for b in range(nb):
            def tile(i, carry, b=b):
                xr, xi = carry
                r0 = pl.multiple_of(b * nc + i * 8, 8)
                er, ei = er_ref[pl.ds(r0, 8), :], ei_ref[pl.ds(r0, 8), :]
                outr, outi = jnp.zeros((8, cb), F32), jnp.zeros((8, cb), F32)
                for j in range(8):
                    outr = jnp.where(rid == j, xr, outr)
                    outi = jnp.where(rid == j, xi, outi)
                    xr, xi = lr * xr - li * xi + er[j:j + 1, :], li * xr + lr * xi + ei[j:j + 1, :]
                xr_ref[pl.ds(r0, 8), :] = outr
                xi_ref[pl.ds(r0, 8), :] = outi
                return xr, xi

            z = jnp.zeros((1, cb), F32)
            lax.fori_loop(0, nc // 8, tile, (z, z))

    mat = pl.BlockSpec((r, cb), lambda j: (0, j))
    vec = pl.BlockSpec((1, cb), lambda j: (0, j))
    sds = jax.ShapeDtypeStruct((r, n), F32)
    return pl.pallas_call(body, name="ssm_scan", grid=(n // cb,), in_specs=[mat, mat, vec, vec],
                          out_specs=[mat, mat], out_shape=[sds, sds],
                          compiler_params=_cp(("parallel",)))(e_re, e_im, lam_re, lam_im)


def _ssm_scan_bwd(dxp_re, dxp_im, xp_re, xp_im, lam_re, lam_im, nb):
    r, n = dxp_re.shape
    nc = r // nb
    cb = _tile(n, 512)

    def body(dr_ref, di_ref, xr_ref, xi_ref, lr_ref, li_ref, er_ref, ei_ref, dlr_ref, dli_ref):
        lr, li = lr_ref[...], li_ref[...]
        rid = lax.broadcasted_iota(jnp.int32, (8, 1), 0)
        z = jnp.zeros((1, cb), F32)
        alr, ali = z, z
        for b in range(nb):
            def tile(i, carry, b=b):
                gr, gi, alr, ali = carry
                r0 = pl.multiple_of(b * nc + (nc // 8 - 1 - i) * 8, 8)
                dr, di = dr_ref[pl.ds(r0, 8), :], di_ref[pl.ds(r0, 8), :]
                xr, xi = xr_ref[pl.ds(r0, 8), :], xi_ref[pl.ds(r0, 8), :]
                outr, outi = jnp.zeros((8, cb), F32), jnp.zeros((8, cb), F32)
                for j in range(7, -1, -1):
                    outr = jnp.where(rid == j, gr, outr)
                    outi = jnp.where(rid == j, gi, outi)
                    xrj, xij = xr[j:j + 1, :], xi[j:j + 1, :]
                    alr = alr + gr * xrj + gi * xij
                    ali = ali + gi * xrj - gr * xij
                    gr, gi = dr[j:j + 1, :] + lr * gr + li * gi, di[j:j + 1, :] + lr * gi - li * gr
                er_ref[pl.ds(r0, 8), :] = outr
                ei_ref[pl.ds(r0, 8), :] = outi
                return gr, gi, alr, ali

            _, _, alr, ali = lax.fori_loop(0, nc // 8, tile, (z, z, alr, ali))
        dlr_ref[...] = alr
        dli_ref[...] = ali

    mat = pl.BlockSpec((r, cb), lambda j: (0, j))
    vec = pl.BlockSpec((1, cb), lambda j: (0, j))
    sds = jax.ShapeDtypeStruct((r, n), F32)
    vds = jax.ShapeDtypeStruct((1, n), F32)
    return pl.pallas_call(body, name="ssm_scan_bwd", grid=(n // cb,), in_specs=[mat, mat, mat, mat, vec, vec],
                          out_specs=[mat, mat, vec, vec], out_shape=[sds, sds, vds, vds],
                          compiler_params=_cp(("parallel",)))(dxp_re, dxp_im, xp_re, xp_im, lam_re, lam_im)


def _ssm_operators(a_re, a_im, log_dt, b_re, b_im, c_re, c_im):
    g, p = a_re.shape
    h = b_re.shape[-1]
    ln = SSM_L
    sg = LANE // h
    na = g // sg
    hp = lax.Precision.HIGHEST
    lam = lax.complex(a_re, a_im)
    ldt = lam * jnp.exp(log_dt)[:, None]
    lam_bar = jnp.exp(ldt)
    bbar = ((lam_bar - 1.0) / lam)[..., None] * lax.complex(b_re, b_im)
    cm = lax.complex(c_re, c_im)
    steps = jnp.arange(ln + 1, dtype=F32)
    pw = jnp.exp(ldt[:, None, :] * steps[None, :, None])
    kd = jnp.einsum("ghp,gdp,gpk->gdhk", cm, pw[:, :ln], bbar, precision=hp).real
    lag = jnp.arange(ln)[None, :] - jnp.arange(ln)[:, None]
    sel = (lag[None] == jnp.arange(ln)[:, None, None]).astype(F32)
    eye = jnp.eye(sg, dtype=F32)
    tm = jnp.einsum("agdhk,dst,gj->asgktjh", kd.reshape(na, sg, ln, h, h), sel, eye, precision=hp)
    tm = tm.reshape(na, ln * LANE, ln * LANE)
    wxc = pw[:, ln - 1 - jnp.arange(ln)][:, :, :, None] * bbar[:, None]
    wxc = wxc.transpose(0, 1, 3, 2).reshape(na, sg, ln, h, p)
    wx = lambda part: jnp.einsum("agskp,gj->asgkjp", part, eye, precision=hp).reshape(na, ln * LANE, sg * p)
    cpc = (cm[:, None] * pw[:, 1:ln + 1][:, :, None, :]).reshape(na, sg, ln, h, p)
    wy = lambda part: jnp.einsum("agthp,gj->ajptgh", part, eye, precision=hp).reshape(na, sg * p, ln * LANE)
    lam_l = pw[:, ln]
    return (tm, wx(wxc.real), wx(wxc.imag), wy(cpc.real), wy(-cpc.imag),
            lam_l.real.reshape(1, g * p), lam_l.imag.reshape(1, g * p))


def _chunk_view(a):
    t, d = a.shape
    return a.reshape(t // SSM_L, SSM_L * d)


def _sg_specs(r4, d, wst):
    nblk = d // LANE
    cat = [pl.BlockSpec((r4, LANE), functools.partial(lambda j, tau: (0, tau * nblk + j), tau=tau))
           for tau in range(SSM_L)]
    plane = pl.BlockSpec((r4, wst), lambda j: (0, j))
    mat = lambda rows, cols: pl.BlockSpec((None, rows, cols), lambda j: (j, 0, 0))
    piece = pl.BlockSpec((r4, LANE), lambda j: (0, j))
    return cat, plane, mat, piece


def _lane_cat(refs):
    return jnp.concatenate([r[...] for r in refs], axis=1)


def _bdot(a, b, dims):
    return lax.dot_general(a.astype(BF16), b.astype(BF16), dims, preferred_element_type=F32)


def _ssm_core_fwd(u, ops, nb):
    tm, wxr, wxi, wyr, wyi, lam_re, lam_im = ops
    t, d = u.shape
    ln, na, wch, wst = SSM_L, tm.shape[0], tm.shape[1], wxr.shape[2]
    r4 = t // ln
    n = na * wst
    u4 = _chunk_view(u)
    cat, plane, mat, piece = _sg_specs(r4, d, wst)
    pds = jax.ShapeDtypeStruct((r4, n), F32)

    def states(*refs):
        ucat = _lane_cat(refs[:ln])
        wr_ref, wi_ref, er_ref, ei_ref = refs[ln:]
        er_ref[...] = _bdot(ucat, wr_ref[...], _NN)
        ei_ref[...] = _bdot(ucat, wi_ref[...], _NN)

    e_re, e_im = pl.pallas_call(
        states, name="ssm_states", grid=(na,), in_specs=cat + [mat(wch, wst)] * 2, out_specs=[plane, plane],
        out_shape=[pds, pds], compiler_params=_cp(("parallel",)))(*([u4] * ln), wxr, wxi)
    xp_re, xp_im = _ssm_scan(e_re, e_im, lam_re, lam_im, nb)

    def outputs(*refs):
        ucat = _lane_cat(refs[:ln])
        tm_ref, xr_ref, xi_ref, wr_ref, wi_ref = refs[ln:ln + 5]
        y = (_bdot(ucat, tm_ref[...], _NN) + _bdot(xr_ref[...], wr_ref[...], _NN)
             + _bdot(xi_ref[...], wi_ref[...], _NN))
        for tau, o_ref in enumerate(refs[ln + 5:]):
            o_ref[...] = y[:, tau * LANE:(tau + 1) * LANE].astype(o_ref.dtype)

    ys = pl.pallas_call(
        outputs, name="ssm_y", grid=(na,),
        in_specs=cat + [mat(wch, wch), plane, plane, mat(wst, wch), mat(wst, wch)], out_specs=[piece] * ln,
        out_shape=[jax.ShapeDtypeStruct((r4, d), BF16)] * ln,
        compiler_params=_cp(("parallel",)))(*([u4] * ln), tm, xp_re, xp_im, wyr, wyi)
    return jnp.concatenate(ys, axis=1).reshape(t, d), xp_re, xp_im


def _ssm_core_bwd(dy, u, xp_re, xp_im, ops, nb):
    tm, wxr, wxi, wyr, wyi, lam_re, lam_im = ops
    t, d = u.shape
    ln, na, wch, wst = SSM_L, tm.shape[0], tm.shape[1], wxr.shape[2]
    r4 = t // ln
    n = na * wst
    u4, dy4 = _chunk_view(u), _chunk_view(dy)
    cat, plane, mat, piece = _sg_specs(r4, d, wst)
    pds = jax.ShapeDtypeStruct((r4, n), F32)

    def dstates(*refs):
        dycat = _lane_cat(refs[:ln])
        wr_ref, wi_ref, dr_ref, di_ref = refs[ln:]
        dr_ref[...] = _bdot(dycat, wr_ref[...], _NT)
        di_ref[...] = _bdot(dycat, wi_ref[...], _NT)

    dxp_re, dxp_im = pl.pallas_call(
        dstates, name="ssm_dxp", grid=(na,), in_specs=cat + [mat(wst, wch)] * 2, out_specs=[plane, plane],
        out_shape=[pds, pds], compiler_params=_cp(("parallel",)))(*([dy4] * ln), wyr, wyi)
    de_re, de_im, dlam_re, dlam_im = _ssm_scan_bwd(dxp_re, dxp_im, xp_re, xp_im, lam_re, lam_im, nb)

    def dinputs(*refs):
        dycat = _lane_cat(refs[:ln])
        tm_ref, er_ref, ei_ref, wr_ref, wi_ref = refs[ln:ln + 5]
        du = (_bdot(dycat, tm_ref[...], _NT) + _bdot(er_ref[...], wr_ref[...], _NT)
              + _bdot(ei_ref[...], wi_ref[...], _NT))
        for tau, o_ref in enumerate(refs[ln + 5:]):
            o_ref[...] = du[:, tau * LANE:(tau + 1) * LANE].astype(o_ref.dtype)

    dus = pl.pallas_call(
        dinputs, name="ssm_du", grid=(na,),
        in_specs=cat + [mat(wch, wch), plane, plane, mat(wch, wst), mat(wch, wst)], out_specs=[piece] * ln,
        out_shape=[jax.ShapeDtypeStruct((r4, d), BF16)] * ln,
        compiler_params=_cp(("parallel",)))(*([dy4] * ln), tm, de_re, de_im, wxr, wxi)

    def doperators(*refs):
        ucat, dycat = _lane_cat(refs[:ln]), _lane_cat(refs[ln:2 * ln])
        er_ref, ei_ref, xr_ref, xi_ref, dtm_ref, dwxr_ref, dwxi_ref, dwyr_ref, dwyi_ref = refs[2 * ln:]
        dtm_ref[...] = _bdot(ucat, dycat, _TN)
        dwxr_ref[...] = _bdot(ucat, er_ref[...], _TN)
        dwxi_ref[...] = _bdot(ucat, ei_ref[...], _TN)
        dwyr_ref[...] = _bdot(xr_ref[...], dycat, _TN)
        dwyi_ref[...] = _bdot(xi_ref[...], dycat, _TN)

    mds = lambda rows, cols: jax.ShapeDtypeStruct((na, rows, cols), F32)
    d_ops = pl.pallas_call(
        doperators, name="ssm_dops", grid=(na,), in_specs=cat + cat + [plane] * 4,
        out_specs=[mat(wch, wch), mat(wch, wst), mat(wch, wst), mat(wst, wch), mat(wst, wch)],
        out_shape=[mds(wch, wch), mds(wch, wst), mds(wch, wst), mds(wst, wch), mds(wst, wch)],
        compiler_params=_cp(("parallel",)))(*([u4] * ln), *([dy4] * ln), de_re, de_im, xp_re, xp_im)
    return jnp.concatenate(dus, axis=1).reshape(t, d), (*d_ops, dlam_re, dlam_im)


def _modfin_fwd(c_all, w_mod, w_fin):
    n, d = c_all.shape
    nl, _, cm = w_mod.shape
    cf = w_fin.shape[1]
    width = nl * cm + cf
    hp = lax.Precision.HIGHEST

    def body(c_ref, wm_ref, wf_ref, act_ref, out_ref):
        cv = c_ref[...]
        act = cv * _sigmoid(cv)
        act_ref[...] = act
        for i in range(nl):
            out_ref[:, i * cm:(i + 1) * cm] = jnp.dot(act, wm_ref[i], preferred_element_type=F32, precision=hp)
        out_ref[:, nl * cm:] = jnp.dot(act, wf_ref[...], preferred_element_type=F32, precision=hp)

    return pl.pallas_call(body, name="modfin_fwd",
                          out_shape=[jax.ShapeDtypeStruct((n, d), F32), jax.ShapeDtypeStruct((n, width), F32)],
                          compiler_params=_cp(None))(c_all, w_mod, w_fin)


def _modfin_bwd(c_act_t, dmod_loc, dfin_loc, dall):
    d, n = c_act_t.shape
    nl, _, cm = dmod_loc.shape
    cf = dfin_loc.shape[1]
    hp = lax.Precision.HIGHEST

    def body(ct_ref, dm_ref, df_ref, da_ref, gwm_ref, gwf_ref, gb_ref):
        ct = ct_ref[...]
        for i in range(nl):
            gwm_ref[i] = jnp.dot(ct, dm_ref[i], preferred_element_type=F32, precision=hp)
        gwf_ref[...] = jnp.dot(ct, df_ref[...], preferred_element_type=F32, precision=hp)
        gb_ref[...] = jnp.sum(da_ref[...], axis=0, keepdims=True)

    return pl.pallas_call(body, name="modfin_bwd",
                          out_shape=[jax.ShapeDtypeStruct((nl, d, cm), F32), jax.ShapeDtypeStruct((d, cf), F32),
                                     jax.ShapeDtypeStruct((1, dall.shape[1]), F32)],
                          compiler_params=_cp(None))(c_act_t, dmod_loc, dfin_loc, dall)


def _adamw(name, gparts, w, m, v):
    n, r, c = gparts.shape
    tr = _tile(r, 256)
    c1 = 1.0 / (1.0 - ADAM_B1 ** ADAM_STEP)
    c2 = 1.0 / (1.0 - ADAM_B2 ** ADAM_STEP)

    def body(gp_ref, w_ref, m_ref, v_ref, g_ref, d_ref, mo_ref, vo_ref):
        gsum = gp_ref[0].astype(F32)
        for j in range(1, n):
            gsum = gsum + gp_ref[j].astype(F32)
        mn = ADAM_B1 * m_ref[...] + (1.0 - ADAM_B1) * gsum
        vn = ADAM_B2 * v_ref[...] + (1.0 - ADAM_B2) * (gsum * gsum)
        g_ref[...] = gsum
        mo_ref[...] = mn
        vo_ref[...] = vn
        d_ref[...] = -ADAM_LR * ((mn * c1) / (jnp.sqrt(vn * c2) + ADAM_EPS) + ADAM_WD * w_ref[...])

    mat = pl.BlockSpec((tr, c), lambda i: (i, 0))
    sds = jax.ShapeDtypeStruct((r, c), F32)
    return pl.pallas_call(body, name=name, grid=(r // tr,),
                          in_specs=[pl.BlockSpec((n, tr, c), lambda i: (0, i, 0)), mat, mat, mat],
                          out_specs=[mat] * 4, out_shape=[sds] * 4,
                          compiler_params=_cp(("parallel",)))(gparts, w, m, v)


def _sum_parts(name, parts):
    n, r, c = parts.shape

    def body(p_ref, o_ref):
        acc = p_ref[0]
        for j in range(1, n):
            acc = acc + p_ref[j]
        o_ref[...] = acc

    return pl.pallas_call(body, name=name, out_shape=jax.ShapeDtypeStruct((r, c), F32),
                          compiler_params=_cp(None))(parts)


class _Exchange:
    def __init__(self, arrs, gathers):
        self.arrs = [pltpu.with_memory_space_constraint(a, pltpu.HBM) for a in arrs]
        self.gathers = list(gathers)
        self.n = len(arrs)
        self.out_shape = [pltpu.HBM(((N_DEV,) + a.shape) if g else a.shape, a.dtype)
                          for a, g in zip(arrs, self.gathers)]
        self.specs = [pl.BlockSpec(memory_space=pltpu.HBM)] * self.n
        self.scratch = [pltpu.SemaphoreType.DMA((self.n, N_DEV - 1)), pltpu.SemaphoreType.DMA((self.n, N_DEV - 1)),
                        pltpu.SemaphoreType.DMA((self.n,))]

    def _copies(self, ins, outs, sems):
        send_sems, recv_sems, local_sems = sems
        x, y, c = lax.axis_index("x"), lax.axis_index("y"), lax.axis_index("c")
        me = 4 * x + 2 * y + c
        local, sends, recvs = [], [], []
        for i in range(self.n):
            src_me = ins[i] if self.gathers[i] else ins[i].at[me]
            local.append(pltpu.make_async_copy(src_me, outs[i].at[me], local_sems.at[i]))
        for dd in range(1, N_DEV):
            px = jnp.bitwise_xor(x, dd >> 2)
            py = jnp.bitwise_xor(y, (dd >> 1) & 1)
            pc = jnp.bitwise_xor(c, dd & 1)
            pid = 4 * px + 2 * py + pc
            for i in range(self.n):
                src = ins[i] if self.gathers[i] else ins[i].at[pid]
                sems_i = dict(send_sem=send_sems.at[i, dd - 1], recv_sem=recv_sems.at[i, dd - 1],
                              device_id=(px, py, pc), device_id_type=MESH)
                sends.append(pltpu.make_async_remote_copy(src_ref=src, dst_ref=outs[i].at[me], **sems_i))
                recvs.append(pltpu.make_async_remote_copy(src_ref=src, dst_ref=outs[i].at[pid], **sems_i))
        return local, sends, recvs

    def start(self, ins, outs, sems):
        local, sends, _ = self._copies(ins, outs, sems)
        for cp in local + sends:
            cp.start()

    def wait(self, ins, outs, sems):
        local, sends, recvs = self._copies(ins, outs, sems)
        for cp in recvs:
            cp.wait_recv()
        for cp in sends:
            cp.wait_send()
        for cp in local:
            cp.wait()


class _NoExchange:
    n, arrs, specs, out_shape, scratch = 0, [], [], [], []

    def start(self, ins, outs, sems):
        pass

    def wait(self, ins, outs, sems):
        pass


def _exchange(name, arrs, gathers):
    ex = _Exchange(arrs, gathers)
    n = ex.n

    def body(*refs):
        ins, outs, sems = refs[:n], refs[n:2 * n], refs[2 * n:]
        ex.start(ins, outs, sems)
        ex.wait(ins, outs, sems)

    outs = pl.pallas_call(body, name=name, in_specs=ex.specs, out_specs=ex.specs, out_shape=ex.out_shape,
                          scratch_shapes=ex.scratch)(*ex.arrs)
    return list(outs)


def _pack(pieces):
    flat = jnp.concatenate([p.reshape(-1) for p in pieces])
    pad = (-flat.shape[0]) % (8 * LANE)
    return jnp.pad(flat, (0, pad)).reshape(-1, LANE)


def _unpack(packed, shapes):
    flat = packed.reshape(-1)
    out, off = [], 0
    for shp in shapes:
        sz = math.prod(shp)
        out.append(flat[off:off + sz].reshape(shp))
        off += sz
    return out


def kernel(x, c, norm_mix, norm_ffn, w_mod, b_mod, w_qkv, w_o_attn, w_in_ssm, a_re, a_im, log_dt, b_re, b_im, c_re, c_im, d_skip, w_glu, b_glu, w_o_ssm, w_up, conv_w, conv_b, w_down, norm_out, w_fin, b_fin, loss_target, m_norm_mix, m_norm_ffn, m_w_mod, m_b_mod, m_w_qkv, m_w_o_attn, m_w_in_ssm, m_a_re, m_a_im, m_log_dt, m_b_re, m_b_im, m_c_re, m_c_im, m_d_skip, m_w_glu, m_b_glu, m_w_o_ssm, m_w_up, m_conv_w, m_conv_b, m_w_down, m_norm_out, m_w_fin, m_b_fin, v_norm_mix, v_norm_ffn, v_w_mod, v_b_mod, v_w_qkv, v_w_o_attn, v_w_in_ssm, v_a_re, v_a_im, v_log_dt, v_b_re, v_b_im, v_c_re, v_c_im, v_d_skip, v_w_glu, v_b_glu, v_w_o_ssm, v_w_up, v_conv_w, v_conv_b, v_w_down, v_norm_out, v_w_fin, v_b_fin):
    nb, s, d = x.shape
    t = nb * s
    n_seq = nb * N_DEV
    me = 4 * lax.axis_index("x") + 2 * lax.axis_index("y") + lax.axis_index("c")
    cm = w_mod.shape[2]
    cf = w_fin.shape[1]
    c_up = w_up.shape[2]
    r_dn = w_down.shape[1]
    g_ssm = d // SSM_H

    wq8, c8 = _exchange("gather_first", [w_qkv[0].astype(BF16), c], [True, True])
    later = _Exchange([w_o_attn[0].astype(BF16), w_in_ssm[0].astype(BF16), w_glu[0].astype(BF16),
                       w_o_ssm[0].astype(BF16), w_up[0].astype(BF16), w_up[1].astype(BF16),
                       w_down[0].astype(BF16), w_down[1].astype(BF16), conv_w, d_skip, b_glu], [True] * 11)
    half = N_DEV // 2
    cb_l = [conv_b[i].reshape(N_DEV, 1, c_up) for i in range(2)]
    c_all = c8.reshape(n_seq, d)

    c_act, modloc = _modfin_fwd(c_all, w_mod, w_fin)
    (mod8,) = _exchange("gather_mod", [modloc], [True])
    mine = lax.dynamic_slice_in_dim(mod8, me * nb, nb, axis=1)
    mods = []
    for i in range(2):
        mi = mine[:, :, i * cm:(i + 1) * cm].transpose(1, 0, 2).reshape(nb, N_DEV * cm) + b_mod[i]
        mods.append([mi[:, j * d:(j + 1) * d].reshape(nb, 1, d) for j in range(6)])
    fin = mine[:, :, 2 * cm:].transpose(1, 0, 2).reshape(nb, N_DEV * cf) + b_fin
    sh_f, sc_f = fin[:, :d].reshape(nb, 1, d), fin[:, d:].reshape(nb, 1, d)

    row = lambda a: a.reshape(1, -1)
    x0 = x.reshape(t, d)

    def ffn_fwd(i, xin, sh, sc, gate):
        h = _norm_mod_fwd(f"ffn{i}_norm", xin, row(norm_ffn[i]), sh, sc, nb)
        up = _mm(f"ffn{i}_up", h, wup8[i], (t // tm_, N_DEV, 1),
                 pl.BlockSpec((tm_, d), lambda a, b, k: (a, 0)), pl.BlockSpec((None, d, c_up), lambda a, b, k: (b, 0, 0)),
                 pl.BlockSpec((None, tm_, c_up), lambda a, b, k: (b, a, 0)),
                 jax.ShapeDtypeStruct((N_DEV, t, c_up), BF16), _NN, (tm_, c_up))
        act = _ffn_act_fwd(f"ffn{i}_act", up, cw_l[i], cb_l[i], nb, s)
        yf = _mm(f"ffn{i}_down", act, wd4[i], (t // tm_, 1, half),
                 pl.BlockSpec((None, tm_, c_up), lambda a, b, k: (k, a, 0)),
                 pl.BlockSpec((None, c_up, d), lambda a, b, k: (k, 0, 0)),
                 pl.BlockSpec((tm_, d), lambda a, b, k: (a, 0)), jax.ShapeDtypeStruct((t, d), F32), _NN, (tm_, d))
        xout = _gate_add(f"ffn{i}_res", xin, yf, gate, nb)
        return xout, (h, up, act, yf)

    tm_ = _tile(t, 512)
    sh1, sc1, g1, sh2, sc2, g2 = mods[0]
    h1 = _norm_mod_fwd("attn_norm", x0, row(norm_mix[0]), sh1, sc1, nb)
    cq = wq8.shape[2]
    qkv = _mm("attn_qkv", h1, wq8, (t // tm_, N_DEV, 1),
              pl.BlockSpec((tm_, d), lambda a, b, k: (a, 0)), pl.BlockSpec((None, d, cq), lambda a, b, k: (b, 0, 0)),
              pl.BlockSpec((tm_, cq), lambda a, b, k: (a, b)), jax.ShapeDtypeStruct((t, 3 * d), BF16), _NN, (tm_, cq))
    o_att, car_att, (wo8, win8, wglu8, wos8, wup8_0, wup8_1, wd8_0, wd8_1, cw8, dskip8, bglu8) = _attn_fwd(
        qkv, nb, s, d, later)
    wo = wo8.reshape(d, d)
    win = win8.reshape(d, d)
    wglu = wglu8.reshape(d, d)
    wos = wos8.reshape(d, d)
    wup8 = [wup8_0, wup8_1]
    wd4 = [wd8_0.reshape(half, 2 * r_dn, d), wd8_1.reshape(half, 2 * r_dn, d)]
    cw_l = [cw8[:, 0], cw8[:, 1]]
    dskip_f = dskip8.reshape(1, d)
    bglu_f = bglu8.reshape(1, d)
    ya = _mm_nn("attn_out", o_att, wo, F32)
    x1 = _gate_add("attn_res", x0, ya, g1, nb)
    x2, ffn0_saved = ffn_fwd(0, x1, sh2, sc2, g2)

    sh1b, sc1b, g1b, sh2b, sc2b, g2b = mods[1]
    ops = _ssm_operators(a_re[0], a_im[0], log_dt[0], b_re[0], b_im[0], c_re[0], c_im[0])
    h3 = _norm_mod_fwd("ssm_norm", x2, row(norm_mix[1]), sh1b, sc1b, nb)
    u = _mm_nn("ssm_in", h3, win, BF16)
    ys_core, xp_re, xp_im = _ssm_core_fwd(u, ops, nb)
    y_ssm, z_ssm = _ssm_post_fwd(ys_core, u, dskip_f)
    gl = _mm_nn("ssm_glu", z_ssm, wglu, F32)
    gg = _glu_fwd(z_ssm, gl, bglu_f)
    ys2 = _mm_nn("ssm_out", gg, wos, F32)
    x3 = _gate_add("ssm_res", x2, ys2, g1b, nb)
    x4, ffn1_saved = ffn_fwd(1, x3, sh2b, sc2b, g2b)

    dx4, g_norm_out, dsh_f, dsc_f, loss_blk = _norm_mod_bwd(
        "final_norm", None, x4, row(norm_out), sh_f, sc_f, None, loss_target.reshape(t, d), nb)
    loss = lax.psum(loss_blk[0, 0], ("x", "y", "c"))

    def ffn_bwd(i, dxo, xin, sc, gate, saved):
        h, up, act, yf = saved
        dyf, dgate = _gate_bwd(f"ffn{i}_res_bwd", dxo, yf, gate, nb)
        dact = _mm(f"ffn{i}_down_dx", dyf, wd4[i], (t // tm_, half, 1),
                   pl.BlockSpec((tm_, d), lambda a, b, k: (a, 0)), pl.BlockSpec((None, c_up, d), lambda a, b, k: (b, 0, 0)),
                   pl.BlockSpec((None, tm_, c_up), lambda a, b, k: (b, a, 0)),
                   jax.ShapeDtypeStruct((half, t, c_up), BF16), _NT, (tm_, c_up))
        tk = _tile(t, 1024)
        gwd = _mm(f"ffn{i}_down_dw", act, dyf, (half, 1, t // tk),
                  pl.BlockSpec((None, tk, c_up), lambda a, b, k: (a, k, 0)), pl.BlockSpec((tk, d), lambda a, b, k: (k, 0)),
                  pl.BlockSpec((None, c_up, d), lambda a, b, k: (a, 0, 0)),
                  jax.ShapeDtypeStruct((half, c_up, d), BF16), _TN, (c_up, d))
        dup, dcw, dcb = _ffn_act_bwd(f"ffn{i}_act_bwd", up, dact, cw_l[i], cb_l[i], nb, s)
        dh = _mm(f"ffn{i}_up_dx", dup, wup8[i], (t // tm_, 1, N_DEV),
                 pl.BlockSpec((None, tm_, c_up), lambda a, b, k: (k, a, 0)),
                 pl.BlockSpec((None, d, c_up), lambda a, b, k: (k, 0, 0)),
                 pl.BlockSpec((tm_, d), lambda a, b, k: (a, 0)), jax.ShapeDtypeStruct((t, d), F32), _NT, (tm_, d))
        gwup = _mm(f"ffn{i}_up_dw", h, dup, (1, N_DEV, t // tk),
                   pl.BlockSpec((tk, d), lambda a, b, k: (k, 0)), pl.BlockSpec((None, tk, c_up), lambda a, b, k: (b, k, 0)),
                   pl.BlockSpec((None, d, c_up), lambda a, b, k: (b, 0, 0)),
                   jax.ShapeDtypeStruct((N_DEV, d, c_up), BF16), _TN, (d, c_up))
        dxi, g_norm, dsh, dsc = _norm_mod_bwd(f"ffn{i}_norm_bwd", dh, xin, row(norm_ffn[i]), None, sc, dxo, None, nb)
        return dxi, (gwup, gwd.reshape(N_DEV, r_dn, d), dcw, dcb, g_norm, dsh, dsc, dgate)

    dx3, (gwup1, gwd1, dcw1, dcb1, g_nffn1, dsh2b, dsc2b, dg2b) = ffn_bwd(1, dx4, x3, sc2b, g2b, ffn1_saved)

    dys2, dg1b = _gate_bwd("ssm_res_bwd", dx3, ys2, g1b, nb)
    dgg = _mm_nt("ssm_out_dx", dys2, wos, F32)
    gwos = _mm_tn("ssm_out_dw", gg, dys2, BF16)
    dgl, dz1, g_bglu = _glu_bwd(dgg, z_ssm, gl, bglu_f)
    dz2 = _mm_nt("ssm_glu_dx", dgl, wglu, F32)
    gwglu = _mm_tn("ssm_glu_dw", z_ssm, dgl, BF16)
    dy_ssm, du_skip, g_dskip = _ssm_post_bwd(dz1, dz2, y_ssm, u, dskip_f)
    du_core, d_ops = _ssm_core_bwd(dy_ssm, u, xp_re, xp_im, ops, nb)
    du = _add_cast(du_core, du_skip)
    dh3 = _mm_nt("ssm_in_dx", du, win, F32)
    gwin = _mm_tn("ssm_in_dw", h3, du, BF16)
    dx2, g_nmix1, dsh1b, dsc1b = _norm_mod_bwd("ssm_norm_bwd", dh3, x2, row(norm_mix[1]), None, sc1b, dx3, None, nb)
    _, ops_vjp = jax.vjp(_ssm_operators, a_re[0], a_im[0], log_dt[0], b_re[0], b_im[0], c_re[0], c_im[0])
    g_ssm_params = ops_vjp(d_ops)

    dx1, (gwup0, gwd0, dcw0, dcb0, g_nffn0, dsh2, dsc2, dg2) = ffn_bwd(0, dx2, x1, sc2, g2, ffn0_saved)

    dya, dg1 = _gate_bwd("attn_res_bwd", dx1, ya, g1, nb)
    do_att = _mm_nt("attn_out_dx", dya, wo, BF16)
    gwo = _mm_tn("attn_out_dw", o_att, dya, BF16)
    rows8 = lambda a: a.reshape(N_DEV, d // N_DEV, d)
    early = _Exchange([rows8(gwo), rows8(gwin), rows8(gwglu), rows8(gwos), gwup0, gwup1, gwd0, gwd1,
                       _pack(g_ssm_params)], [False] * 8 + [True])
    dq, dk, dv, (ro, rin, rglu, ros, rup0, rup1, rd0, rd1, ssm8) = _attn_bwd(qkv, car_att, do_att, nb, s, d, early)
    dqkv = jnp.concatenate([dq, dk, dv], axis=1)
    dh1 = _mm("attn_qkv_dx", dqkv, wq8, (t // tm_, 1, N_DEV),
              pl.BlockSpec((tm_, cq), lambda a, b, k: (a, k)),
              pl.BlockSpec((None, d, cq), lambda a, b, k: (k, 0, 0)),
              pl.BlockSpec((tm_, d), lambda a, b, k: (a, 0)), jax.ShapeDtypeStruct((t, d), F32), _NT, (tm_, d))
    tk = _tile(t, 1024)
    gwq8 = _mm("attn_qkv_dw", h1, dqkv, (1, N_DEV, t // tk),
               pl.BlockSpec((tk, d), lambda a, b, k: (k, 0)),
               pl.BlockSpec((tk, cq), lambda a, b, k: (k, b)),
               pl.BlockSpec((None, d, cq), lambda a, b, k: (b, 0, 0)),
               jax.ShapeDtypeStruct((N_DEV, d, cq), BF16), _TN, (d, cq))
    dx0, g_nmix0, dsh1, dsc1 = _norm_mod_bwd("attn_norm_bwd", dh1, x0, row(norm_mix[0]), None, sc1, dx1, None, nb)
    grad_x = dx0.reshape(nb, s, d)

    dmod =[jnp.concatenate([a.reshape(nb, d) for a in grp], axis=1) for grp in
            ([dsh1, dsc1, dg1, dsh2, dsc2, dg2], [dsh1b, dsc1b, dg1b, dsh2b, dsc2b, dg2b])]
    dfin = jnp.concatenate([dsh_f.reshape(nb, d), dsc_f.reshape(nb, d)], axis=1)
    dmodfin = jnp.concatenate(dmod + [dfin], axis=1)
    ssm_shapes = [(g_ssm, SSM_P), (g_ssm, SSM_P), (g_ssm,), (g_ssm, SSM_P, SSM_H), (g_ssm, SSM_P, SSM_H),
                  (g_ssm, SSM_H, SSM_P), (g_ssm, SSM_H, SSM_P)]
    small_shapes = [(2, d), (2, d), (d,), (2, N_DEV * c_up), (d,), (d,), (2, N_DEV, 3, c_up)]
    small_partial = _pack([jnp.stack([g_nmix0, g_nmix1]), jnp.stack([g_nffn0, g_nffn1]), g_norm_out,
                           jnp.stack([dcb0, dcb1]), g_dskip, g_bglu, jnp.stack([dcw0, dcw1])])
    rq, dmf8, small8 = _exchange("exchange_last", [gwq8, dmodfin, small_partial], [False, True, True])
    (g_norm_mix, g_norm_ffn, g_norm_out_s, g_conv_b, g_dskip_full, g_bglu_full, g_cw_full) = _unpack(
        _sum_parts("sum_small_grads", small8), small_shapes)
    g_a_re, g_a_im, g_log_dt, g_b_re, g_b_im, g_c_re, g_c_im = _unpack(_sum_parts("sum_ssm_grads", ssm8), ssm_shapes)

    dall = dmf8.reshape(n_seq, 14 * d)
    dmod_loc = jnp.stack([lax.dynamic_slice_in_dim(dall[:, i * 6 * d:(i + 1) * 6 * d], me * cm, cm, axis=1)
                          for i in range(2)])
    dfin_loc = lax.dynamic_slice_in_dim(dall[:, 12 * d:], me * cf, cf, axis=1)
    g_w_mod, g_w_fin, g_bias = _modfin_bwd(c_act.T, dmod_loc, dfin_loc, dall)
    g_b_mod = g_bias[0, :12 * d].reshape(2, 6 * d)
    g_b_fin = g_bias[0, 12 * d:]

    def big(name, parts, w, m, v):
        shp = w.shape
        r2 = lambda a: a.reshape(-1, shp[-1])
        res = _adamw(name, parts.reshape(parts.shape[0], -1, shp[-1]), r2(w), r2(m), r2(v))
        return [a.reshape(shp) for a in res]

    upd = {}
    upd["w_mod"] = big("adamw_w_mod", g_w_mod[None], w_mod, m_w_mod, v_w_mod)
    upd["w_fin"] = big("adamw_w_fin", g_w_fin[None], w_fin, m_w_fin, v_w_fin)
    upd["w_qkv"] = big("adamw_w_qkv", rq, w_qkv, m_w_qkv, v_w_qkv)
    upd["w_o_attn"] = big("adamw_w_o_attn", ro, w_o_attn, m_w_o_attn, v_w_o_attn)
    upd["w_in_ssm"] = big("adamw_w_in_ssm", rin, w_in_ssm, m_w_in_ssm, v_w_in_ssm)
    upd["w_glu"] = big("adamw_w_glu", rglu, w_glu, m_w_glu, v_w_glu)
    upd["w_o_ssm"] = big("adamw_w_o_ssm", ros, w_o_ssm, m_w_o_ssm, v_w_o_ssm)
    up_l = [big(f"adamw_w_up{i}", r, w_up[i], m_w_up[i], v_w_up[i]) for i, r in enumerate((rup0, rup1))]
    upd["w_up"] = [jnp.stack([up_l[0][j], up_l[1][j]]) for j in range(4)]
    dn_l = [big(f"adamw_w_down{i}", r, w_down[i], m_w_down[i], v_w_down[i]) for i, r in enumerate((rd0, rd1))]
    upd["w_down"] = [jnp.stack([dn_l[0][j], dn_l[1][j]]) for j in range(4)]

    g_dskip_loc = lax.dynamic_slice_in_dim(g_dskip_full.reshape(1, d), me * (d // N_DEV), d // N_DEV, axis=1)
    g_bglu_loc = lax.dynamic_slice_in_dim(g_bglu_full.reshape(1, d), me * (d // N_DEV), d // N_DEV, axis=1)
    g_cw_loc = lax.dynamic_slice_in_dim(g_cw_full, me, 1, axis=1).reshape(2, 3, c_up)
    small_names = ["norm_mix", "norm_ffn", "b_mod", "a_re", "a_im", "log_dt", "b_re", "b_im", "c_re", "c_im",
                   "d_skip", "b_glu", "conv_w", "conv_b", "norm_out", "b_fin"]
    small_g = [g_norm_mix, g_norm_ffn, g_b_mod, g_a_re[None], g_a_im[None], g_log_dt[None], g_b_re[None], g_b_im[None],
               g_c_re[None], g_c_im[None], g_dskip_loc, g_bglu_loc, g_cw_loc, g_conv_b, g_norm_out_s, g_b_fin]
    small_w = [norm_mix, norm_ffn, b_mod, a_re, a_im, log_dt, b_re, b_im, c_re, c_im, d_skip, b_glu, conv_w, conv_b,
               norm_out, b_fin]
    small_m = [m_norm_mix, m_norm_ffn, m_b_mod, m_a_re, m_a_im, m_log_dt, m_b_re, m_b_im, m_c_re, m_c_im, m_d_skip,
               m_b_glu, m_conv_w, m_conv_b, m_norm_out, m_b_fin]
    small_v = [v_norm_mix, v_norm_ffn, v_b_mod, v_a_re, v_a_im, v_log_dt, v_b_re, v_b_im, v_c_re, v_c_im, v_d_skip,
               v_b_glu, v_conv_w, v_conv_b, v_norm_out, v_b_fin]
    shapes = [w.shape for w in small_w]
    res = _adamw("adamw_small", _pack(small_g)[None], _pack(small_w), _pack(small_m), _pack(small_v))
    res = [_unpack(r, shapes) for r in res]
    for j, nm in enumerate(small_names):
        upd[nm] = [res[k][j] for k in range(4)]

    order = ["norm_mix", "norm_ffn", "w_mod", "b_mod", "w_qkv", "w_o_attn", "w_in_ssm", "a_re", "a_im", "log_dt",
             "b_re", "b_im", "c_re", "c_im", "d_skip", "w_glu", "b_glu", "w_o_ssm", "w_up", "conv_w", "conv_b",
             "w_down", "norm_out", "w_fin", "b_fin"]
    outs = [loss, grad_x]
    for k in range(4):
        outs += [upd[nm][k] for nm in order]
    return tuple(outs)
```

```python
import functools
import math

import jax
import jax.numpy as jnp
from jax import lax
from jax.experimental import pallas as pl
from jax.experimental.pallas import tpu as pltpu

F32 = jnp.float32
BF16 = jnp.bfloat16
MESH = pl.DeviceIdType.MESH

N_DEV = 8
HEAD_DIM = 64
ATT_BLK = 128
ATT_BQ = 256
ATT_UNROLL = 2
SSM_H = 16
SSM_P = 64
SSM_L = 4
EPS = 1e-6
ADAM_LR, ADAM_B1, ADAM_B2, ADAM_EPS, ADAM_WD, ADAM_STEP = 0.001, 0.9, 0.999, 1e-08, 0.01, 10
V7X_VMEM_LIMIT = 56 * 1024 * 1024
LANE = 128

_NN = (((1,), (0,)), ((), ()))
_NT = (((1,), (1,)), ((), ()))
_TN = (((0,), (0,)), ((), ()))


def _cp(sem):
    return pltpu.CompilerParams(dimension_semantics=sem, vmem_limit_bytes=V7X_VMEM_LIMIT)


def _tile(n, pref):
    if n <= pref:
        return n
    t = pref - pref % 16
    while t >= 16:
        if n % t == 0:
            return t
        t -= 16
    return n


def _mm(name, a, b, grid, a_spec, b_spec, out_spec, out_shape, dims, acc_shape):
    nk = grid[-1]
    kax = len(grid) - 1

    def body(a_ref, b_ref, o_ref, acc_ref):
        k = pl.program_id(kax)

        @pl.when(k == 0)
        def _():
            acc_ref[...] = jnp.zeros(acc_shape, F32)

        acc_ref[...] += lax.dot_general(a_ref[...].astype(BF16), b_ref[...].astype(BF16), dims,
                                        preferred_element_type=F32)

        @pl.when(k == nk - 1)
        def _():
            o_ref[...] = acc_ref[...].astype(o_ref.dtype)

    return pl.pallas_call(
        body, name=name, grid=grid, in_specs=[a_spec, b_spec], out_specs=out_spec, out_shape=out_shape,
        scratch_shapes=[pltpu.VMEM(acc_shape, F32)],
        compiler_params=_cp(("parallel",) * kax + ("arbitrary",)))(a, b)


def _mm_nn(name, a, w, out_dtype):
    m, k = a.shape
    n = w.shape[1]
    tm, tn, tk = _tile(m, 512), _tile(n, 1024), _tile(k, 1024)
    return _mm(name, a, w, (m // tm, n // tn, k // tk),
               pl.BlockSpec((tm, tk), lambda i, j, kk: (i, kk)), pl.BlockSpec((tk, tn), lambda i, j, kk: (kk, j)),
               pl.BlockSpec((tm, tn), lambda i, j, kk: (i, j)), jax.ShapeDtypeStruct((m, n), out_dtype), _NN, (tm, tn))


def _mm_nt(name, a, w, out_dtype):
    m, n = a.shape
    k = w.shape[0]
    tm, tko, tn = _tile(m, 512), _tile(k, 1024), _tile(n, 1024)
    return _mm(name, a, w, (m // tm, k // tko, n // tn),
               pl.BlockSpec((tm, tn), lambda i, j, kk: (i, kk)), pl.BlockSpec((tko, tn), lambda i, j, kk: (j, kk)),
               pl.BlockSpec((tm, tko), lambda i, j, kk: (i, j)), jax.ShapeDtypeStruct((m, k), out_dtype), _NT, (tm, tko))


def _mm_tn(name, a, b, out_dtype):
    t, m = a.shape
    n = b.shape[1]
    tm, tn, tk = _tile(m, 512), _tile(n, 1024), _tile(t, 1024)
    return _mm(name, a, b, (m // tm, n // tn, t // tk),
               pl.BlockSpec((tk, tm), lambda i, j, kk: (kk, i)), pl.BlockSpec((tk, tn), lambda i, j, kk: (kk, j)),
               pl.BlockSpec((tm, tn), lambda i, j, kk: (i, j)), jax.ShapeDtypeStruct((m, n), out_dtype), _TN, (tm, tn))


def _norm_mod_fwd(name, x, g, shift, scale, nb):
    t, d = x.shape
    s = t // nb
    tr = _tile(s, 512)
    nt = s // tr

    def body(x_ref, g_ref, sh_ref, sc_ref, h_ref):
        xv = x_ref[...]
        r = lax.rsqrt(jnp.mean(xv * xv, axis=-1, keepdims=True) + EPS)
        y = xv * r * g_ref[...]
        h_ref[...] = (y * (1.0 + sc_ref[...]) + sh_ref[...]).astype(h_ref.dtype)

    row = pl.BlockSpec((tr, d), lambda b, i: (b * nt + i, 0))
    vec = pl.BlockSpec((None, 1, d), lambda b, i: (b, 0, 0))
    return pl.pallas_call(body, name=name, grid=(nb, nt),
                          in_specs=[row, pl.BlockSpec((1, d), lambda b, i: (0, 0)), vec, vec],
                          out_specs=row, out_shape=jax.ShapeDtypeStruct((t, d), BF16),
                          compiler_params=_cp(("parallel", "parallel")))(x, g, shift, scale)


def _norm_mod_bwd(name, dh, x, g, shift, scale, dres, target, nb):
    t, d = x.shape
    s = t // nb
    tr = _tile(s, 256)
    nt = s // tr
    final = target is not None

    def body(*refs):
        if final:
            x_ref, g_ref, sh_ref, sc_ref, tg_ref, dx_ref, dg_ref, dsh_ref, dsc_ref, loss_ref = refs
        else:
            dh_ref, x_ref, g_ref, sc_ref, dres_ref, dx_ref, dg_ref, dsh_ref, dsc_ref = refs
        b, i = pl.program_id(0), pl.program_id(1)
        xv = x_ref[...]
        gv = g_ref[...]
        r = lax.rsqrt(jnp.mean(xv * xv, axis=-1, keepdims=True) + EPS)
        nrm = xv * r
        y = nrm * gv
        one_sc = 1.0 + sc_ref[...]
        if final:
            err = y * one_sc + sh_ref[...] - tg_ref[...]
            dhv = err * (1.0 / d)
        else:
            dhv = dh_ref[...].astype(F32)
        dy = dhv * one_sc
        dn = dy * gv
        dxv = r * (dn - nrm * jnp.mean(dn * nrm, axis=-1, keepdims=True))
        if final:
            dx_ref[...] = dxv
        else:
            dx_ref[...] = dres_ref[...] + dxv

        @pl.when(i == 0)
        def _():
            dsh_ref[...] = jnp.zeros_like(dsh_ref)
            dsc_ref[...] = jnp.zeros_like(dsc_ref)

        @pl.when((i == 0) & (b == 0))
        def _():
            dg_ref[...] = jnp.zeros_like(dg_ref)
            if final:
                loss_ref[...] = jnp.zeros_like(loss_ref)

        dsh_ref[...] += jnp.sum(dhv, axis=0, keepdims=True)
        dsc_ref[...] += jnp.sum(dhv * y, axis=0, keepdims=True)
        dg_ref[...] += jnp.sum(dy * nrm, axis=0, keepdims=True)
        if final:
            loss_ref[...] += (0.5 / d) * jnp.sum(err * err)

    row = pl.BlockSpec((tr, d), lambda b, i: (b * nt + i, 0))
    vec = pl.BlockSpec((None, 1, d), lambda b, i: (b, 0, 0))
    gsp = pl.BlockSpec((1, d), lambda b, i: (0, 0))
    out_specs = [row, gsp, vec, vec]
    out_shape = [jax.ShapeDtypeStruct((t, d), F32), jax.ShapeDtypeStruct((1, d), F32),
                 jax.ShapeDtypeStruct((nb, 1, d), F32), jax.ShapeDtypeStruct((nb, 1, d), F32)]
    if final:
        ins, in_specs = [x, g, shift, scale, target], [row, gsp, vec, vec, row]
        out_specs.append(pl.BlockSpec((8, LANE), lambda b, i: (0, 0)))
        out_shape.append(jax.ShapeDtypeStruct((8, LANE), F32))
    else:
        ins, in_specs = [dh, x, g, scale, dres], [row, row, gsp, vec, row]
    return pl.pallas_call(body, name=name, grid=(nb, nt), in_specs=in_specs, out_specs=out_specs,
                          out_shape=out_shape, compiler_params=_cp(("arbitrary", "arbitrary")))(*ins)


def _gate_add(name, x, y, gate, nb):
    t, d = x.shape
    s = t // nb
    tr = _tile(s, 512)
    nt = s // tr

    def body(x_ref, y_ref, g_ref, o_ref):
        o_ref[...] = x_ref[...] + g_ref[...] * y_ref[...]

    row = pl.BlockSpec((tr, d), lambda b, i: (b * nt + i, 0))
    vec = pl.BlockSpec((None, 1, d), lambda b, i: (b, 0, 0))
    return pl.pallas_call(body, name=name, grid=(nb, nt), in_specs=[row, row, vec], out_specs=row,
                          out_shape=jax.ShapeDtypeStruct((t, d), F32),
                          compiler_params=_cp(("parallel", "parallel")))(x, y, gate)


def _gate_bwd(name, dx, y, gate, nb):
    t, d = dx.shape
    s = t // nb
    tr = _tile(s, 512)
    nt = s // tr

    def body(dx_ref, y_ref, g_ref, dy_ref, dg_ref):
        dxv = dx_ref[...]
        dy_ref[...] = (g_ref[...] * dxv).astype(dy_ref.dtype)

        @pl.when(pl.program_id(1) == 0)
        def _():
            dg_ref[...] = jnp.zeros_like(dg_ref)

        dg_ref[...] += jnp.sum(dxv * y_ref[...], axis=0, keepdims=True)

    row = pl.BlockSpec((tr, d), lambda b, i: (b * nt + i, 0))
    vec = pl.BlockSpec((None, 1, d), lambda b, i: (b, 0, 0))
    return pl.pallas_call(body, name=name, grid=(nb, nt), in_specs=[row, row, vec], out_specs=[row, vec],
                          out_shape=[jax.ShapeDtypeStruct((t, d), BF16), jax.ShapeDtypeStruct((nb, 1, d), F32)],
                          compiler_params=_cp(("parallel", "arbitrary")))(dx, y, gate)


def _log_sigmoid(z):
    return jnp.minimum(z, 0.0) - jnp.log(1.0 + jnp.exp(-jnp.abs(z)))


def _split_dot(v, tri):
    hi = v.astype(BF16)
    lo = (v - hi.astype(F32)).astype(BF16)
    return (jnp.dot(hi, tri, preferred_element_type=F32) + jnp.dot(lo, tri, preferred_element_type=F32))


def _grid_ends(grid):
    ids = [pl.program_id(a) for a in range(len(grid))]
    first = functools.reduce(lambda u, w: u & w, [i == 0 for i in ids])
    last = functools.reduce(lambda u, w: u & w, [i == n - 1 for i, n in zip(ids, grid)])
    return first, last


def _attn_fwd(qkv, nb, s, d, ex):
    t = nb * s
    npair = d // LANE
    bk = ATT_BLK
    bq = min(ATT_BQ, s)
    nq = s // bq
    kpq = bq // bk
    nheads = LANE // HEAD_DIM
    scale = HEAD_DIM ** -0.5
    grid = (nb, npair, nq)
    assert s // bk <= HEAD_DIM, "one carry lane per key block and head"
    assert bk == LANE, "the running sums are kept one 128-lane tile wide"
    assert kpq == ATT_UNROLL, "one loop trip covers exactly the key blocks under a query block's diagonal"

    def body(*refs):
        q_ref, k_ref, v_ref = refs[:3]
        ex_ins = refs[3:3 + ex.n]
        o_ref, car_ref = refs[3 + ex.n:5 + ex.n]
        ex_outs = refs[5 + ex.n:5 + 2 * ex.n]
        acc_s, run_s, z_s, arg_s = refs[5 + 2 * ex.n:9 + 2 * ex.n]
        sems = refs[9 + 2 * ex.n:]
        first, last = _grid_ends(grid)

        @pl.when(first)
        def _():
            ex.start(ex_ins, ex_outs, sems)

        qi = pl.program_id(2)
        q = q_ref[...]
        lane = lax.broadcasted_iota(jnp.int32, (1, LANE), 1)
        row = lax.broadcasted_iota(jnp.int32, (bq, bk), 0)
        col = lax.broadcasted_iota(jnp.int32, (bq, bk), 1)
        trow = lax.broadcasted_iota(jnp.int32, (bk, bk), 0)
        tcol = lax.broadcasted_iota(jnp.int32, (bk, bk), 1)
        tri = (trow > tcol).astype(BF16)
        hms = [(lane // HEAD_DIM) == hh for hh in range(nheads)]
        qhs = [jnp.where(hm, q, jnp.zeros_like(q)) * scale for hm in hms]
        car_ref[...] = jnp.zeros((bq, LANE), F32)
        acc_s[...] = jnp.zeros_like(acc_s)
        run_s[...] = jnp.zeros_like(run_s)
        nkb = (qi + 1) * kpq

        ntrip = nkb // ATT_UNROLL

        def k0_of(jj, u):
            return pl.multiple_of((nkb - 1 - (ATT_UNROLL * jj + u)) * bk, bk)

        def scores(jj):
            jc = jnp.minimum(jj, ntrip - 1)
            for u in range(ATT_UNROLL):
                kj = k_ref[pl.ds(k0_of(jc, u), bk), :]
                for hh in range(nheads):
                    z_s[hh, u] = lax.dot_general(qhs[hh], kj, _NT, preferred_element_type=F32)

        def exponents(jj, masked):
            car = car_ref[...]
            for hh in range(nheads):
                run = run_s[hh]
                for u in range(ATT_UNROLL):
                    z = z_s[hh, u]
                    lb = _log_sigmoid(z)
                    l1 = lb - z
                    if masked:
                        mask = (k0_of(jj, u) + col) < (qi * bq + row)
                        l1 = jnp.where(mask, l1, 0.0)
                    arg = lb + (_split_dot(l1, tri) + run)
                    arg_s[hh, u] = jnp.where(mask, arg, -1e30) if masked else arg
                    car = jnp.where(lane == hh * HEAD_DIM + (nkb - 1 - (ATT_UNROLL * jj + u)), run, car)
                    run = run + jnp.sum(l1, axis=1, keepdims=True)
                run_s[hh] = run
            car_ref[...] = car

        def weigh(jj):
            for hh in range(nheads):
                acc = None
                for u in range(ATT_UNROLL):
                    vj = v_ref[pl.ds(k0_of(jj, u), bk), :]
                    pv = jnp.dot(jnp.exp(arg_s[hh, u]).astype(BF16), vj, preferred_element_type=F32)
                    acc = pv if acc is None else acc + pv
                acc_s[hh] += acc

        def step(jj, carry):
            zn = [[lax.dot_general(qhs[hh], k_ref[pl.ds(k0_of(jnp.minimum(jj + 1, ntrip - 1), u), bk), :], _NT,
                                   preferred_element_type=F32) for u in range(ATT_UNROLL)] for hh in range(nheads)]
            weigh(jj - 1)
            exponents(jj, False)
            for hh in range(nheads):
                for u in range(ATT_UNROLL):
                    z_s[hh, u] = zn[hh][u]
            return carry

        scores(0)
        exponents(0, True)
        scores(1)
        lax.fori_loop(1, ntrip, step, 0)
        weigh(ntrip - 1)
        out = acc_s[0]
        for hh in range(1, nheads):
            out = jnp.where(hms[hh], acc_s[hh], out)
        o_ref[...] = out.astype(o_ref.dtype)

        @pl.when(last)
        def _():
            ex.wait(ex_ins, ex_outs, sems)

    qspec = pl.BlockSpec((bq, LANE), lambda b, p, i: (b * nq + i, p))
    res = pl.pallas_call(
        body, name="attn_fwd", grid=grid,
        in_specs=[qspec,
                  pl.BlockSpec((s, LANE), lambda b, p, i: (b, npair + p)),
                  pl.BlockSpec((s, LANE), lambda b, p, i: (b, 2 * npair + p))] + ex.specs,
        out_specs=[qspec, qspec] + ex.specs,
        out_shape=[jax.ShapeDtypeStruct((t, d), BF16), jax.ShapeDtypeStruct((t, d), F32)] + ex.out_shape,
        scratch_shapes=[pltpu.VMEM((nheads, bq, LANE), F32), pltpu.VMEM((nheads, bq, LANE), F32),
                        pltpu.VMEM((nheads, ATT_UNROLL, bq, bk), F32),
                        pltpu.VMEM((nheads, ATT_UNROLL, bq, bk), F32)] + ex.scratch,
        compiler_params=_cp(("arbitrary", "arbitrary", "arbitrary")))(qkv, qkv, qkv, *ex.arrs)
    return res[0], res[1], list(res[2:])


def _attn_bwd(qkv, car, do, nb, s, d, ex):
    t = nb * s
    npair = d // LANE
    bk = ATT_BLK
    bq = min(ATT_BQ, s)
    nq = s // bq
    kpq = bq // bk
    nheads = LANE // HEAD_DIM
    scale = HEAD_DIM ** -0.5
    grid = (nb, npair, nq)

    def body(*refs):
        q_ref, k_ref, v_ref, car_ref, do_ref = refs[:5]
        ex_ins = refs[5:5 + ex.n]
        dq_ref, dk_ref, dv_ref = refs[5 + ex.n:8 + ex.n]
        ex_outs = refs[8 + ex.n:8 + 2 * ex.n]
        dk_acc, dv_acc, dq_s, rune_s, z_s, da_s, dz_s, a_s = refs[8 + 2 * ex.n:16 + 2 * ex.n]
        sems = refs[16 + 2 * ex.n:]
        first, last = _grid_ends(grid)

        @pl.when(first)
        def _():
            ex.start(ex_ins, ex_outs, sems)

        qi = pl.program_id(2)

        @pl.when(qi == 0)
        def _():
            dk_acc[...] = jnp.zeros_like(dk_acc)
            dv_acc[...] = jnp.zeros_like(dv_acc)

        q = q_ref[...]
        dov = do_ref[...]
        lane = lax.broadcasted_iota(jnp.int32, (1, LANE), 1)
        row = lax.broadcasted_iota(jnp.int32, (bq, bk), 0)
        col = lax.broadcasted_iota(jnp.int32, (bq, bk), 1)
        trow = lax.broadcasted_iota(jnp.int32, (bk, bk), 0)
        tcol = lax.broadcasted_iota(jnp.int32, (bk, bk), 1)
        tri_suf = (trow > tcol).astype(BF16)
        tri_pre = (trow < tcol).astype(BF16)
        hms = [(lane // HEAD_DIM) == hh for hh in range(nheads)]
        qhs = [jnp.where(hm, q, jnp.zeros_like(q)) * scale for hm in hms]
        dohs = [jnp.where(hm, dov, jnp.zeros_like(dov)) for hm in hms]
        dq_s[...] = jnp.zeros_like(dq_s)
        rune_s[...] = jnp.zeros_like(rune_s)
        dz_s[...] = jnp.zeros_like(dz_s)
        a_s[...] = jnp.zeros_like(a_s)
        ntrip = (qi + 1) * kpq // ATT_UNROLL

        def k0_of(jj, u):
            return pl.multiple_of((ATT_UNROLL * jnp.maximum(jj, 0) + u) * bk, bk)

        def products(jj):
            ks = [k_ref[pl.ds(k0_of(jj, u), bk), :] for u in range(ATT_UNROLL)]
            vs = [v_ref[pl.ds(k0_of(jj, u), bk), :] for u in range(ATT_UNROLL)]
            zn = [[lax.dot_general(qhs[hh], kj, _NT, preferred_element_type=F32) for kj in ks] for hh in range(nheads)]
            dn = [[lax.dot_general(dohs[hh], vj, _NT, preferred_element_type=F32) for vj in vs] for hh in range(nheads)]
            return zn, dn

        def keep(zn, dn):
            for hh in range(nheads):
                for u in range(ATT_UNROLL):
                    z_s[hh, u] = zn[hh][u]
                    da_s[hh, u] = dn[hh][u]

        def middle(jj, masked):
            car = car_ref[...]
            for hh in range(nheads):
                run_e = rune_s[hh]
                for u in range(ATT_UNROLL):
                    z = z_s[hh, u]
                    lb = _log_sigmoid(z)
                    l1u = lb - z
                    if masked:
                        mask = (k0_of(jj, u) + col) < (qi * bq + row)
                    l1 = jnp.where(mask, l1u, 0.0) if masked else l1u
                    run = jnp.sum(jnp.where(lane == hh * HEAD_DIM + ATT_UNROLL * jj + u, car, 0.0), axis=1,
                                  keepdims=True)
                    a = jnp.exp(lb + (_split_dot(l1, tri_suf) + run))
                    if masked:
                        a = jnp.where(mask, a, 0.0)
                    e = da_s[hh, u] * a
                    dz = e * jnp.exp(l1u) - (_split_dot(e, tri_pre) + run_e) * jnp.exp(lb)
                    if masked:
                        dz = jnp.where(mask, dz, 0.0)
                    dz_s[hh, u] = dz.astype(BF16)
                    a_s[hh, u] = a.astype(BF16)
                    run_e = run_e + jnp.sum(e, axis=1, keepdims=True)
                rune_s[hh] = run_e

        def grads(jj):
            for u in range(ATT_UNROLL):
                k0 = k0_of(jj, u)
                kj = k_ref[pl.ds(k0, bk), :]
                for hh in range(nheads):
                    dzb = dz_s[hh, u]
                    dq_s[hh] += jnp.dot(dzb, kj, preferred_element_type=F32)
                    dkh = lax.dot_general(dzb, qhs[hh], _TN, preferred_element_type=F32)
                    dvh = lax.dot_general(a_s[hh, u], dohs[hh], _TN, preferred_element_type=F32)
                    dk_blk = dkh if hh == 0 else dk_blk + dkh
                    dv_blk = dvh if hh == 0 else dv_blk + dvh
                dk_acc[pl.ds(k0, bk), :] += dk_blk
                dv_acc[pl.ds(k0, bk), :] += dv_blk

        def step(jj, carry):
            zn, dn = products(jj + 1)
            grads(jj - 1)
            middle(jj, False)
            keep(zn, dn)
            return carry

        keep(*products(0))
        lax.fori_loop(0, ntrip - 1, step, 0)
        grads(ntrip - 2)
        middle(ntrip - 1, True)
        grads(ntrip - 1)
        dq_out = dq_s[0]
        for hh in range(1, nheads):
            dq_out = jnp.where(hms[hh], dq_s[hh], dq_out)
        dq_ref[...] = (dq_out * scale).astype(dq_ref.dtype)

        @pl.when(qi == nq - 1)
        def _():
            dk_ref[...] = dk_acc[...].astype(dk_ref.dtype)
            dv_ref[...] = dv_acc[...].astype(dv_ref.dtype)

        @pl.when(last)
        def _():
            ex.wait(ex_ins, ex_outs, sems)

    qspec = pl.BlockSpec((bq, LANE), lambda b, p, i: (b * nq + i, p))
    kvout = pl.BlockSpec((s, LANE), lambda b, p, i: (b, p))
    sds = jax.ShapeDtypeStruct((t, d), BF16)
    res = pl.pallas_call(
        body, name="attn_bwd", grid=grid,
        in_specs=[qspec,
                  pl.BlockSpec((s, LANE), lambda b, p, i: (b, npair + p)),
                  pl.BlockSpec((s, LANE), lambda b, p, i: (b, 2 * npair + p)),
                  qspec, qspec] + ex.specs,
        out_specs=[qspec, kvout, kvout] + ex.specs, out_shape=[sds, sds, sds] + ex.out_shape,
        scratch_shapes=[pltpu.VMEM((s, LANE), F32), pltpu.VMEM((s, LANE), F32),
                        pltpu.VMEM((nheads, bq, LANE), F32), pltpu.VMEM((nheads, bq, LANE), F32),
                        pltpu.VMEM((nheads, ATT_UNROLL, bq, bk), F32), pltpu.VMEM((nheads, ATT_UNROLL, bq, bk), F32),
                        pltpu.VMEM((nheads, ATT_UNROLL, bq, bk), BF16),
                        pltpu.VMEM((nheads, ATT_UNROLL, bq, bk), BF16)] + ex.scratch,
        compiler_params=_cp(("arbitrary", "arbitrary", "arbitrary")))(qkv, qkv, qkv, car, do, *ex.arrs)
    return res[0], res[1], res[2], list(res[3:])


def _conv3(u_ref, w, bias, c, r0, rc):
    x = u_ref[pl.ds(r0, rc), :].astype(F32)
    p0 = pl.multiple_of(jnp.maximum(r0 - 16, 0), 16)
    prev = u_ref[pl.ds(p0, 16), :].astype(F32)
    prev = jnp.where(c > 0, prev, 0.0)
    row = lax.broadcasted_iota(jnp.int32, (rc, 1), 0)
    s1 = jnp.where(row == 0, prev[15:16, :], pltpu.roll(x, 1, 0))
    s2 = jnp.where(row == 0, prev[14:15, :], jnp.where(row == 1, prev[15:16, :], pltpu.roll(x, 2, 0)))
    cv = w[2:3, :] * x + w[1:2, :] * s1 + w[0:1, :] * s2 + bias
    return cv, x, s1, s2


def _sigmoid(x):
    return 1.0 / (1.0 + jnp.exp(-x))


def _ffn_act_fwd(name, up8, cw8, cb8, nb, s):
    _, t, c_w = up8.shape
    rc = _tile(s, 256)
    nch = s // rc
    half = N_DEV // 2

    def body(ug_ref, uv_ref, wg_ref, wv_ref, bg_ref, bv_ref, act_ref):
        wg, wv, bg, bv = wg_ref[...], wv_ref[...], bg_ref[...], bv_ref[...]

        def chunk(c, carry):
            r0 = pl.multiple_of(c * rc, rc)
            cg = _conv3(ug_ref, wg, bg, c, r0, rc)[0]
            cv = _conv3(uv_ref, wv, bv, c, r0, rc)[0]
            act_ref[pl.ds(r0, rc), :] = (cg * _sigmoid(cg) * cv).astype(act_ref.dtype)
            return carry

        lax.fori_loop(0, nch, chunk, 0)

    def slab(off):
        return pl.BlockSpec((None, s, c_w), lambda k, b: (k + off, b, 0))

    def par(rows, off):
        return pl.BlockSpec((None, rows, c_w), lambda k, b: (k + off, 0, 0))

    return pl.pallas_call(
        body, name=name, grid=(half, nb),
        in_specs=[slab(0), slab(half), par(3, 0), par(3, half), par(1, 0), par(1, half)],
        out_specs=pl.BlockSpec((None, s, c_w), lambda k, b: (k, b, 0)),
        out_shape=jax.ShapeDtypeStruct((half, t, c_w), BF16),
        compiler_params=_cp(("parallel", "parallel")))(up8, up8, cw8, cw8, cb8, cb8)


def _ffn_act_bwd(name, up8, dact4, cw8, cb8, nb, s):
    _, t, c_w = up8.shape
    rc = _tile(s, 256)
    nch = s // rc
    half = N_DEV // 2

    def body(u_ref, da_ref, w_ref, b_ref, dup_ref, dcw_ref, dcb_ref):
        w2, b2 = w_ref[...], b_ref[...]
        row = lax.broadcasted_iota(jnp.int32, (rc, 1), 0)

        @pl.when(pl.program_id(1) == 0)
        def _():
            dcw_ref[...] = jnp.zeros_like(dcw_ref)
            dcb_ref[...] = jnp.zeros_like(dcb_ref)

        def chunk(i, carry):
            c = nch - 1 - i
            r0 = pl.multiple_of(c * rc, rc)
            convs = [_conv3(u_ref.at[h], w2[h], b2[h], c, r0, rc) for h in range(2)]
            gt, vl = convs[0][0], convs[1][0]
            da = da_ref[pl.ds(r0, rc), :].astype(F32)
            sg = _sigmoid(gt)
            dcvs = [da * vl * sg * (1.0 + gt * (1.0 - sg)), da * gt * sg]
            out = []
            for h in range(2):
                n0, n1, a0, a1, a2, ab = carry[6 * h:6 * h + 6]
                dcv, (_, x, s1, s2), w = dcvs[h], convs[h], w2[h]
                t1 = jnp.where(row == rc - 1, n0, pltpu.roll(dcv, rc - 1, 0))
                t2 = jnp.where(row == rc - 2, n0, jnp.where(row == rc - 1, n1, pltpu.roll(dcv, rc - 2, 0)))
                dup = w[2:3, :] * dcv + w[1:2, :] * t1 + w[0:1, :] * t2
                dup_ref[h, pl.ds(r0, rc), :] = dup.astype(dup_ref.dtype)
                out += [dcv[0:1, :], dcv[1:2, :],
                        a0 + jnp.sum(dcv * s2, axis=0, keepdims=True), a1 + jnp.sum(dcv * s1, axis=0, keepdims=True),
                        a2 + jnp.sum(dcv * x, axis=0, keepdims=True), ab + jnp.sum(dcv, axis=0, keepdims=True)]
            return tuple(out)

        z = jnp.zeros((1, c_w), F32)
        fin = lax.fori_loop(0, nch, chunk, (z,) * 12)
        for h in range(2):
            _, _, a0, a1, a2, ab = fin[6 * h:6 * h + 6]
            dcw_ref[h, 0:1, :] += a0
            dcw_ref[h, 1:2, :] += a1
            dcw_ref[h, 2:3, :] += a2
            dcb_ref[h] += ab

    def pair(rows, per_seq):
        return pl.BlockSpec((2, None, rows, c_w), (lambda k, b: (0, k, b, 0)) if per_seq else (lambda k, b: (0, k, 0, 0)))

    four = lambda a: a.reshape((2, half) + a.shape[1:])
    dup, dcw, dcb = pl.pallas_call(
        body, name=name, grid=(half, nb),
        in_specs=[pair(s, True), pl.BlockSpec((None, s, c_w), lambda k, b: (k, b, 0)), pair(3, False), pair(1, False)],
        out_specs=[pair(s, True), pair(3, False), pair(1, False)],
        out_shape=[jax.ShapeDtypeStruct((2, half, t, c_w), BF16), jax.ShapeDtypeStruct((2, half, 3, c_w), F32),
                   jax.ShapeDtypeStruct((2, half, 1, c_w), F32)],
        compiler_params=_cp(("parallel", "arbitrary")))(four(up8), dact4, four(cw8), four(cb8))
    return dup.reshape(N_DEV, t, c_w), dcw.reshape(N_DEV, 3, c_w), dcb.reshape(N_DEV, 1, c_w)


_GELU_C0 = math.sqrt(2.0 / math.pi)
_GELU_C1 = 0.044715


def _rowwise(name, body, ins, in_kinds, out_kinds, t, d, tr_pref=512):
    tr = _tile(t, tr_pref)
    row = pl.BlockSpec((tr, d), lambda i: (i, 0))
    vec = pl.BlockSpec((1, d), lambda i: (0, 0))
    in_specs = [row if k == "row" else vec for k in in_kinds]
    out_specs = [row if k[0] == "row" else vec for k in out_kinds]
    out_shape = [jax.ShapeDtypeStruct((t, d) if k[0] == "row" else (1, d), k[1]) for k in out_kinds]
    has_acc = any(k[0] == "acc" for k in out_kinds)
    return pl.pallas_call(body, name=name, grid=(t // tr,), in_specs=in_specs, out_specs=out_specs,
                          out_shape=out_shape,
                          compiler_params=_cp(("arbitrary",) if has_acc else ("parallel",)))(*ins)


def _ssm_post_fwd(ys, u, dskip):
    t, d = ys.shape

    def body(ys_ref, u_ref, ds_ref, y_ref, z_ref):
        y = ys_ref[...].astype(F32) + ds_ref[...] * u_ref[...].astype(F32)
        y_ref[...] = y
        th = jnp.tanh(_GELU_C0 * (y + _GELU_C1 * y * y * y))
        z_ref[...] = (0.5 * y * (1.0 + th)).astype(z_ref.dtype)

    return _rowwise("ssm_post_fwd", body, [ys, u, dskip], ["row", "row", "vec"],
                    [("row", F32), ("row", BF16)], t, d)


def _glu_fwd(z, gl, bglu):
    t, d = z.shape

    def body(z_ref, gl_ref, b_ref, o_ref):
        o_ref[...] = (z_ref[...].astype(F32) * _sigmoid(gl_ref[...] + b_ref[...])).astype(o_ref.dtype)

    return _rowwise("glu_fwd", body, [z, gl, bglu], ["row", "row", "vec"], [("row", BF16)], t, d)[0]


def _glu_bwd(dgg, z, gl, bglu):
    t, d = z.shape

    def body(dg_ref, z_ref, gl_ref, b_ref, dgl_ref, dz_ref, db_ref):
        sg = _sigmoid(gl_ref[...] + b_ref[...])
        dg = dg_ref[...]
        dgl = dg * z_ref[...].astype(F32) * sg * (1.0 - sg)
        dgl_ref[...] = dgl.astype(dgl_ref.dtype)
        dz_ref[...] = dg * sg

        @pl.when(pl.program_id(0) == 0)
        def _():
            db_ref[...] = jnp.zeros_like(db_ref)

        db_ref[...] += jnp.sum(dgl, axis=0, keepdims=True)

    return _rowwise("glu_bwd", body, [dgg, z, gl, bglu], ["row", "row", "row", "vec"],
                    [("row", BF16), ("row", F32), ("acc", F32)], t, d)


def _ssm_post_bwd(dz1, dz2, y, u, dskip):
    t, d = y.shape

    def body(a_ref, b_ref, y_ref, u_ref, ds_ref, dy_ref, du_ref, dd_ref):
        yv = y_ref[...]
        inner = _GELU_C0 * (yv + _GELU_C1 * yv * yv * yv)
        th = jnp.tanh(inner)
        dgelu = 0.5 * (1.0 + th) + 0.5 * yv * (1.0 - th * th) * _GELU_C0 * (1.0 + 3.0 * _GELU_C1 * yv * yv)
        dy = (a_ref[...] + b_ref[...]) * dgelu
        dy_ref[...] = dy.astype(dy_ref.dtype)
        du_ref[...] = dy * ds_ref[...]

        @pl.when(pl.program_id(0) == 0)
        def _():
            dd_ref[...] = jnp.zeros_like(dd_ref)

        dd_ref[...] += jnp.sum(dy * u_ref[...].astype(F32), axis=0, keepdims=True)

    return _rowwise("ssm_post_bwd", body, [dz1, dz2, y, u, dskip], ["row", "row", "row", "row", "vec"],
                    [("row", BF16), ("row", F32), ("acc", F32)], t, d)


def _add_cast(a, b):
    t, d = a.shape

    def body(a_ref, b_ref, o_ref):
        o_ref[...] = (a_ref[...].astype(F32) + b_ref[...].astype(F32)).astype(o_ref.dtype)

    return _rowwise("add_cast", body, [a, b], ["row", "row"], [("row", BF16)], t, d)[0]


def _ssm_scan(e_re, e_im, lam_re, lam_im, nb):
    r, n = e_re.shape
    nc = r // nb
    cb = _tile(n, 512)

    def body(er_ref, ei_ref, lr_ref, li_ref, xr_ref, xi_ref):
        lr, li = lr_ref[...], li_ref[...]
        rid = lax.broadcasted_iota(jnp.int32, (8, 1), 0)
        for b in range(nb):
            def tile(i, carry, b=b):
                xr, xi = carry
                r0 = pl.multiple_of(b * nc + i * 8, 8)
                er, ei = er_ref[pl.ds(r0, 8), :], ei_ref[pl.ds(r0, 8), :]
                outr, outi = jnp.zeros((8, cb), F32), jnp.zeros((8, cb), F32)
                for j in range(8):
                    outr = jnp.where(rid == j, xr, outr)
                    outi = jnp.where(rid == j, xi, outi)
                    xr, xi = lr * xr - li * xi + er[j:j + 1, :], li * xr + lr * xi + ei[j:j + 1, :]
                xr_ref[pl.ds(r0, 8), :] = outr
                xi_ref[pl.ds(r0, 8), :] = outi
                return xr, xi

            z = jnp.zeros((1, cb), F32)
            lax.fori_loop(0, nc // 8, tile, (z, z))

    mat = pl.BlockSpec((r, cb), lambda j: (0, j))
    vec = pl.BlockSpec((1, cb), lambda j: (0, j))
    sds = jax.ShapeDtypeStruct((r, n), F32)
    return pl.pallas_call(body, name="ssm_scan", grid=(n // cb,), in_specs=[mat, mat, vec, vec],
                          out_specs=[mat, mat], out_shape=[sds, sds],
                          compiler_params=_cp(("parallel",)))(e_re, e_im, lam_re, lam_im)


def _ssm_scan_bwd(dxp_re, dxp_im, xp_re, xp_im, lam_re, lam_im, nb):
    r, n = dxp_re.shape
    nc = r // nb
    cb = _tile(n, 512)

    def body(dr_ref, di_ref, xr_ref, xi_ref, lr_ref, li_ref, er_ref, ei_ref, dlr_ref, dli_ref):
        lr, li = lr_ref[...], li_ref[...]
        rid = lax.broadcasted_iota(jnp.int32, (8, 1), 0)
        z = jnp.zeros((1, cb), F32)
        alr, ali = z, z
        for b in range(nb):
            def tile(i, carry, b=b):
                gr, gi, alr, ali = carry
                r0 = pl.multiple_of(b * nc + (nc // 8 - 1 - i) * 8, 8)
                dr, di = dr_ref[pl.ds(r0, 8), :], di_ref[pl.ds(r0, 8), :]
                xr, xi = xr_ref[pl.ds(r0, 8), :], xi_ref[pl.ds(r0, 8), :]
                outr, outi = jnp.zeros((8, cb), F32), jnp.zeros((8, cb), F32)
                for j in range(7, -1, -1):
                    outr = jnp.where(rid == j, gr, outr)
                    outi = jnp.where(rid == j, gi, outi)
                    xrj, xij = xr[j:j + 1, :], xi[j:j + 1, :]
                    alr = alr + gr * xrj + gi * xij
                    ali = ali + gi * xrj - gr * xij
                    gr, gi = dr[j:j + 1, :] + lr * gr + li * gi, di[j:j + 1, :] + lr * gi - li * gr
                er_ref[pl.ds(r0, 8), :] = outr
                ei_ref[pl.ds(r0, 8), :] = outi
                return gr, gi, alr, ali

            _, _, alr, ali = lax.fori_loop(0, nc // 8, tile, (z, z, alr, ali))
        dlr_ref[...] = alr
        dli_ref[...] = ali

    mat = pl.BlockSpec((r, cb), lambda j: (0, j))
    vec = pl.BlockSpec((1, cb), lambda j: (0, j))
    sds = jax.ShapeDtypeStruct((r, n), F32)
    vds = jax.ShapeDtypeStruct((1, n), F32)
    return pl.pallas_call(body, name="ssm_scan_bwd", grid=(n // cb,), in_specs=[mat, mat, mat, mat, vec, vec],
                          out_specs=[mat, mat, vec, vec], out_shape=[sds, sds, vds, vds],
                          compiler_params=_cp(("parallel",)))(dxp_re, dxp_im, xp_re, xp_im, lam_re, lam_im)


def _ssm_operators(a_re, a_im, log_dt, b_re, b_im, c_re, c_im):
    g, p = a_re.shape
    h = b_re.shape[-1]
    ln = SSM_L
    sg = LANE // h
    na = g // sg
    hp = lax.Precision.HIGHEST
    lam = lax.complex(a_re, a_im)
    ldt = lam * jnp.exp(log_dt)[:, None]
    lam_bar = jnp.exp(ldt)
    bbar = ((lam_bar - 1.0) / lam)[..., None] * lax.complex(b_re, b_im)
    cm = lax.complex(c_re, c_im)
    steps = jnp.arange(ln + 1, dtype=F32)
    pw = jnp.exp(ldt[:, None, :] * steps[None, :, None])
    kd = jnp.einsum("ghp,gdp,gpk->gdhk", cm, pw[:, :ln], bbar, precision=hp).real

    def blockdiag(compact, rows_per, cols_per):
        spread = jnp.tile(jnp.eye(cols_per, dtype=F32), (1, sg))
        same = (jnp.arange(sg * rows_per)[:, None] // rows_per == jnp.arange(sg * cols_per)[None, :] // cols_per)
        return jnp.einsum("...rc,cl->...rl", compact, spread, precision=hp) * same.astype(F32)

    def by_supergroup(x, rows_per, cols_per):
        return x.reshape(na, sg, ln, rows_per, cols_per).transpose(0, 2, 1, 3, 4).reshape(na, ln, sg * rows_per, cols_per)

    kt = blockdiag(by_supergroup(kd.transpose(0, 1, 3, 2), h, h), h, h)
    zero = jnp.zeros_like(kt[:, 0])
    tm = jnp.concatenate([jnp.concatenate([kt[:, tau - sig] if tau >= sig else zero for tau in range(ln)], axis=2)
                          for sig in range(ln)], axis=1)
    wxc = (pw[:, ln - 1 - jnp.arange(ln)][:, :, :, None] * bbar[:, None]).transpose(0, 1, 3, 2)
    wx = lambda part: blockdiag(by_supergroup(part, h, p), h, p).reshape(na, ln * LANE, sg * p)
    cpc = (cm[:, None] * pw[:, 1:ln + 1][:, :, None, :]).transpose(0, 1, 3, 2)

    def wy(part):
        big = blockdiag(by_supergroup(part, p, h), p, h)
        return jnp.concatenate([big[:, tau] for tau in range(ln)], axis=2)

    lam_l = pw[:, ln]
    return (tm, wx(wxc.real), wx(wxc.imag), wy(cpc.real), wy(-cpc.imag),
            lam_l.real.reshape(1, g * p), lam_l.imag.reshape(1, g * p))


def _chunk_view(a):
    t, d = a.shape
    return a.reshape(t // SSM_L, SSM_L * d)


def _sg_specs(r4, d, wst):
    nblk = d // LANE
    cat = [pl.BlockSpec((r4, LANE), functools.partial(lambda j, tau: (0, tau * nblk + j), tau=tau))
           for tau in range(SSM_L)]
    plane = pl.BlockSpec((r4, wst), lambda j: (0, j))
    mat = lambda rows, cols: pl.BlockSpec((None, rows, cols), lambda j: (j, 0, 0))
    piece = pl.BlockSpec((r4, LANE), lambda j: (0, j))
    return cat, plane, mat, piece


def _lane_cat(refs):
    return jnp.concatenate([r[...] for r in refs], axis=1)


def _bdot(a, b, dims):
    return lax.dot_general(a.astype(BF16), b.astype(BF16), dims, preferred_element_type=F32)


def _ssm_core_fwd(u, ops, nb):
    tm, wxr, wxi, wyr, wyi, lam_re, lam_im = ops
    t, d = u.shape
    ln, na, wch, wst = SSM_L, tm.shape[0], tm.shape[1], wxr.shape[2]
    r4 = t // ln
    n = na * wst
    u4 = _chunk_view(u)
    cat, plane, mat, piece = _sg_specs(r4, d, wst)
    pds = jax.ShapeDtypeStruct((r4, n), F32)

    def states(*refs):
        ucat = _lane_cat(refs[:ln])
        wr_ref, wi_ref, er_ref, ei_ref = refs[ln:]
        er_ref[...] = _bdot(ucat, wr_ref[...], _NN)
        ei_ref[...] = _bdot(ucat, wi_ref[...], _NN)

    e_re, e_im = pl.pallas_call(
        states, name="ssm_states", grid=(na,), in_specs=cat + [mat(wch, wst)] * 2, out_specs=[plane, plane],
        out_shape=[pds, pds], compiler_params=_cp(("parallel",)))(*([u4] * ln), wxr, wxi)
    xp_re, xp_im = _ssm_scan(e_re, e_im, lam_re, lam_im, nb)

    def outputs(*refs):
        ucat = _lane_cat(refs[:ln])
        tm_ref, xr_ref, xi_ref, wr_ref, wi_ref = refs[ln:ln + 5]
        y = (_bdot(ucat, tm_ref[...], _NN) + _bdot(xr_ref[...], wr_ref[...], _NN)
             + _bdot(xi_ref[...], wi_ref[...], _NN))
        for tau, o_ref in enumerate(refs[ln + 5:]):
            o_ref[...] = y[:, tau * LANE:(tau + 1) * LANE].astype(o_ref.dtype)

    ys = pl.pallas_call(
        outputs, name="ssm_y", grid=(na,),
        in_specs=cat + [mat(wch, wch), plane, plane, mat(wst, wch), mat(wst, wch)], out_specs=[piece] * ln,
        out_shape=[jax.ShapeDtypeStruct((r4, d), BF16)] * ln,
        compiler_params=_cp(("parallel",)))(*([u4] * ln), tm, xp_re, xp_im, wyr, wyi)
    return jnp.concatenate(ys, axis=1).reshape(t, d), xp_re, xp_im


def _ssm_core_bwd(dy, u, xp_re, xp_im, ops, nb):
    tm, wxr, wxi, wyr, wyi, lam_re, lam_im = ops
    t, d = u.shape
    ln, na, wch, wst = SSM_L, tm.shape[0], tm.shape[1], wxr.shape[2]
    r4 = t // ln
    n = na * wst
    u4, dy4 = _chunk_view(u), _chunk_view(dy)
    cat, plane, mat, piece = _sg_specs(r4, d, wst)
    pds = jax.ShapeDtypeStruct((r4, n), F32)

    def dstates(*refs):
        dycat = _lane_cat(refs[:ln])
        wr_ref, wi_ref, dr_ref, di_ref = refs[ln:]
        dr_ref[...] = _bdot(dycat, wr_ref[...], _NT)
        di_ref[...] = _bdot(dycat, wi_ref[...], _NT)

    dxp_re, dxp_im = pl.pallas_call(
        dstates, name="ssm_dxp", grid=(na,), in_specs=cat + [mat(wst, wch)] * 2, out_specs=[plane, plane],
        out_shape=[pds, pds], compiler_params=_cp(("parallel",)))(*([dy4] * ln), wyr, wyi)
    de_re, de_im, dlam_re, dlam_im = _ssm_scan_bwd(dxp_re, dxp_im, xp_re, xp_im, lam_re, lam_im, nb)

    def dinputs(*refs):
        dycat = _lane_cat(refs[:ln])
        tm_ref, er_ref, ei_ref, wr_ref, wi_ref = refs[ln:ln + 5]
        du = (_bdot(dycat, tm_ref[...], _NT) + _bdot(er_ref[...], wr_ref[...], _NT)
              + _bdot(ei_ref[...], wi_ref[...], _NT))
        for tau, o_ref in enumerate(refs[ln + 5:]):
            o_ref[...] = du[:, tau * LANE:(tau + 1) * LANE].astype(o_ref.dtype)

    dus = pl.pallas_call(
        dinputs, name="ssm_du", grid=(na,),
        in_specs=cat + [mat(wch, wch), plane, plane, mat(wch, wst), mat(wch, wst)], out_specs=[piece] * ln,
        out_shape=[jax.ShapeDtypeStruct((r4, d), BF16)] * ln,
        compiler_params=_cp(("parallel",)))(*([dy4] * ln), tm, de_re, de_im, wxr, wxi)

    def doperators(*refs):
        ucat, dycat = _lane_cat(refs[:ln]), _lane_cat(refs[ln:2 * ln])
        er_ref, ei_ref, xr_ref, xi_ref, dtm_ref, dwxr_ref, dwxi_ref, dwyr_ref, dwyi_ref = refs[2 * ln:]
        dtm_ref[...] = _bdot(ucat, dycat, _TN)
        dwxr_ref[...] = _bdot(ucat, er_ref[...], _TN)
        dwxi_ref[...] = _bdot(ucat, ei_ref[...], _TN)
        dwyr_ref[...] = _bdot(xr_ref[...], dycat, _TN)
        dwyi_ref[...] = _bdot(xi_ref[...], dycat, _TN)

    mds = lambda rows, cols: jax.ShapeDtypeStruct((na, rows, cols), F32)
    d_ops = pl.pallas_call(
        doperators, name="ssm_dops", grid=(na,), in_specs=cat + cat + [plane] * 4,
        out_specs=[mat(wch, wch), mat(wch, wst), mat(wch, wst), mat(wst, wch), mat(wst, wch)],
        out_shape=[mds(wch, wch), mds(wch, wst), mds(wch, wst), mds(wst, wch), mds(wst, wch)],
        compiler_params=_cp(("parallel",)))(*([u4] * ln), *([dy4] * ln), de_re, de_im, xp_re, xp_im)
    return jnp.concatenate(dus, axis=1).reshape(t, d), (*d_ops, dlam_re, dlam_im)


def _modfin_fwd(c_all, w_mod, w_fin):
    n, d = c_all.shape
    nl, _, cm = w_mod.shape
    cf = w_fin.shape[1]
    width = nl * cm + cf
    hp = lax.Precision.HIGHEST

    def body(c_ref, wm_ref, wf_ref, act_ref, out_ref):
        cv = c_ref[...]
        act = cv * _sigmoid(cv)
        act_ref[...] = act
        for i in range(nl):
            out_ref[:, i * cm:(i + 1) * cm] = jnp.dot(act, wm_ref[i], preferred_element_type=F32, precision=hp)
        out_ref[:, nl * cm:] = jnp.dot(act, wf_ref[...], preferred_element_type=F32, precision=hp)

    return pl.pallas_call(body, name="modfin_fwd",
                          out_shape=[jax.ShapeDtypeStruct((n, d), F32), jax.ShapeDtypeStruct((n, width), F32)],
                          compiler_params=_cp(None))(c_all, w_mod, w_fin)


def _modfin_bwd(c_act_t, dmod_loc, dfin_loc, dall):
    d, n = c_act_t.shape
    nl, _, cm = dmod_loc.shape
    cf = dfin_loc.shape[1]
    hp = lax.Precision.HIGHEST

    def body(ct_ref, dm_ref, df_ref, da_ref, gwm_ref, gwf_ref, gb_ref):
        ct = ct_ref[...]
        for i in range(nl):
            gwm_ref[i] = jnp.dot(ct, dm_ref[i], preferred_element_type=F32, precision=hp)
        gwf_ref[...] = jnp.dot(ct, df_ref[...], preferred_element_type=F32, precision=hp)
        gb_ref[...] = jnp.sum(da_ref[...], axis=0, keepdims=True)

    return pl.pallas_call(body, name="modfin_bwd",
                          out_shape=[jax.ShapeDtypeStruct((nl, d, cm), F32), jax.ShapeDtypeStruct((d, cf), F32),
                                     jax.ShapeDtypeStruct((1, dall.shape[1]), F32)],
                          compiler_params=_cp(None))(c_act_t, dmod_loc, dfin_loc, dall)


def _adamw(name, gparts, w, m, v):
    n, r, c = gparts.shape
    tr = _tile(r, 256)
    c1 = 1.0 / (1.0 - ADAM_B1 ** ADAM_STEP)
    c2 = 1.0 / (1.0 - ADAM_B2 ** ADAM_STEP)

    def body(gp_ref, w_ref, m_ref, v_ref, g_ref, d_ref, mo_ref, vo_ref):
        gsum = gp_ref[0].astype(F32)
        for j in range(1, n):
            gsum = gsum + gp_ref[j].astype(F32)
        mn = ADAM_B1 * m_ref[...] + (1.0 - ADAM_B1) * gsum
        vn = ADAM_B2 * v_ref[...] + (1.0 - ADAM_B2) * (gsum * gsum)
        g_ref[...] = gsum
        mo_ref[...] = mn
        vo_ref[...] = vn
        d_ref[...] = -ADAM_LR * ((mn * c1) / (jnp.sqrt(vn * c2) + ADAM_EPS) + ADAM_WD * w_ref[...])

    mat = pl.BlockSpec((tr, c), lambda i: (i, 0))
    sds = jax.ShapeDtypeStruct((r, c), F32)
    return pl.pallas_call(body, name=name, grid=(r // tr,),
                          in_specs=[pl.BlockSpec((n, tr, c), lambda i: (0, i, 0)), mat, mat, mat],
                          out_specs=[mat] * 4, out_shape=[sds] * 4,
                          compiler_params=_cp(("parallel",)))(gparts, w, m, v)


def _sum_parts(name, parts):
    n, r, c = parts.shape

    def body(p_ref, o_ref):
        acc = p_ref[0]
        for j in range(1, n):
            acc = acc + p_ref[j]
        o_ref[...] = acc

    return pl.pallas_call(body, name=name, out_shape=jax.ShapeDtypeStruct((r, c), F32),
                          compiler_params=_cp(None))(parts)


class _Exchange:
    def __init__(self, arrs, gathers):
        self.arrs = [pltpu.with_memory_space_constraint(a, pltpu.HBM) for a in arrs]
        self.gathers = list(gathers)
        self.n = len(arrs)
        self.out_shape = [pltpu.HBM(((N_DEV,) + a.shape) if g else a.shape, a.dtype)
                          for a, g in zip(arrs, self.gathers)]
        self.specs = [pl.BlockSpec(memory_space=pltpu.HBM)] * self.n
        self.scratch = [pltpu.SemaphoreType.DMA((self.n, N_DEV - 1)), pltpu.SemaphoreType.DMA((self.n, N_DEV - 1)),
                        pltpu.SemaphoreType.DMA((self.n,))]

    def _copies(self, ins, outs, sems):
        send_sems, recv_sems, local_sems = sems
        x, y, c = lax.axis_index("x"), lax.axis_index("y"), lax.axis_index("c")
        me = 4 * x + 2 * y + c
        local, sends, recvs = [], [], []
        for i in range(self.n):
            src_me = ins[i] if self.gathers[i] else ins[i].at[me]
            local.append(pltpu.make_async_copy(src_me, outs[i].at[me], local_sems.at[i]))
        for dd in range(1, N_DEV):
            px = jnp.bitwise_xor(x, dd >> 2)
            py = jnp.bitwise_xor(y, (dd >> 1) & 1)
            pc = jnp.bitwise_xor(c, dd & 1)
            pid = 4 * px + 2 * py + pc
            for i in range(self.n):
                src = ins[i] if self.gathers[i] else ins[i].at[pid]
                sems_i = dict(send_sem=send_sems.at[i, dd - 1], recv_sem=recv_sems.at[i, dd - 1],
                              device_id=(px, py, pc), device_id_type=MESH)
                sends.append(pltpu.make_async_remote_copy(src_ref=src, dst_ref=outs[i].at[me], **sems_i))
                recvs.append(pltpu.make_async_remote_copy(src_ref=src, dst_ref=outs[i].at[pid], **sems_i))
        return local, sends, recvs

    def start(self, ins, outs, sems):
        local, sends, _ = self._copies(ins, outs, sems)
        for cp in local + sends:
            cp.start()

    def wait(self, ins, outs, sems):
        local, sends, recvs = self._copies(ins, outs, sems)
        for cp in recvs:
            cp.wait_recv()
        for cp in sends:
            cp.wait_send()
        for cp in local:
            cp.wait()


class _NoExchange:
    n, arrs, specs, out_shape, scratch = 0, [], [], [], []

    def start(self, ins, outs, sems):
        pass

    def wait(self, ins, outs, sems):
        pass


def _exchange(name, arrs, gathers):
    ex = _Exchange(arrs, gathers)
    n = ex.n

    def body(*refs):
        ins, outs, sems = refs[:n], refs[n:2 * n], refs[2 * n:]
        ex.start(ins, outs, sems)
        ex.wait(ins, outs, sems)

    outs = pl.pallas_call(body, name=name, in_specs=ex.specs, out_specs=ex.specs, out_shape=ex.out_shape,
                          scratch_shapes=ex.scratch)(*ex.arrs)
    return list(outs)


def _pack(pieces):
    flat = jnp.concatenate([p.reshape(-1) for p in pieces])
    pad = (-flat.shape[0]) % (8 * LANE)
    return jnp.pad(flat, (0, pad)).reshape(-1, LANE)


def _unpack(packed, shapes):
    flat = packed.reshape(-1)
    out, off = [], 0
    for shp in shapes:
        sz = math.prod(shp)
        out.append(flat[off:off + sz].reshape(shp))
        off += sz
    return out


def kernel(x, c, norm_mix, norm_ffn, w_mod, b_mod, w_qkv, w_o_attn, w_in_ssm, a_re, a_im, log_dt, b_re, b_im, c_re, c_im, d_skip, w_glu, b_glu, w_o_ssm, w_up, conv_w, conv_b, w_down, norm_out, w_fin, b_fin, loss_target, m_norm_mix, m_norm_ffn, m_w_mod, m_b_mod, m_w_qkv, m_w_o_attn, m_w_in_ssm, m_a_re, m_a_im, m_log_dt, m_b_re, m_b_im, m_c_re, m_c_im, m_d_skip, m_w_glu, m_b_glu, m_w_o_ssm, m_w_up, m_conv_w, m_conv_b, m_w_down, m_norm_out, m_w_fin, m_b_fin, v_norm_mix, v_norm_ffn, v_w_mod, v_b_mod, v_w_qkv, v_w_o_attn, v_w_in_ssm, v_a_re, v_a_im, v_log_dt, v_b_re, v_b_im, v_c_re, v_c_im, v_d_skip, v_w_glu, v_b_glu, v_w_o_ssm, v_w_up, v_conv_w, v_conv_b, v_w_down, v_norm_out, v_w_fin, v_b_fin):
    nb, s, d = x.shape
    t = nb * s
    n_seq = nb * N_DEV
    me = 4 * lax.axis_index("x") + 2 * lax.axis_index("y") + lax.axis_index("c")
    cm = w_mod.shape[2]
    cf = w_fin.shape[1]
    c_up = w_up.shape[2]
    r_dn = w_down.shape[1]
    g_ssm = d // SSM_H

    wq8, c8 = _exchange("gather_first", [w_qkv[0].astype(BF16), c], [True, True])
    later = _Exchange([w_o_attn[0].astype(BF16), w_in_ssm[0].astype(BF16), w_glu[0].astype(BF16),
                       w_o_ssm[0].astype(BF16), w_up[0].astype(BF16), w_up[1].astype(BF16),
                       w_down[0].astype(BF16), w_down[1].astype(BF16), conv_w, d_skip, b_glu], [True] * 11)
    half = N_DEV // 2
    cb_l = [conv_b[i].reshape(N_DEV, 1, c_up) for i in range(2)]
    c_all = c8.reshape(n_seq, d)

    c_act, modloc = _modfin_fwd(c_all, w_mod, w_fin)
    (mod8,) = _exchange("gather_mod", [modloc], [True])
    mine = lax.dynamic_slice_in_dim(mod8, me * nb, nb, axis=1)
    mods = []
    for i in range(2):
        mi = mine[:, :, i * cm:(i + 1) * cm].transpose(1, 0, 2).reshape(nb, N_DEV * cm) + b_mod[i]
        mods.append([mi[:, j * d:(j + 1) * d].reshape(nb, 1, d) for j in range(6)])
    fin = mine[:, :, 2 * cm:].transpose(1, 0, 2).reshape(nb, N_DEV * cf) + b_fin
    sh_f, sc_f = fin[:, :d].reshape(nb, 1, d), fin[:, d:].reshape(nb, 1, d)

    row = lambda a: a.reshape(1, -1)
    x0 = x.reshape(t, d)

    def ffn_fwd(i, xin, sh, sc, gate):
        h = _norm_mod_fwd(f"ffn{i}_norm", xin, row(norm_ffn[i]), sh, sc, nb)
        up = _mm(f"ffn{i}_up", h, wup8[i], (t // tm_, N_DEV, 1),
                 pl.BlockSpec((tm_, d), lambda a, b, k: (a, 0)), pl.BlockSpec((None, d, c_up), lambda a, b, k: (b, 0, 0)),
                 pl.BlockSpec((None, tm_, c_up), lambda a, b, k: (b, a, 0)),
                 jax.ShapeDtypeStruct((N_DEV, t, c_up), BF16), _NN, (tm_, c_up))
        act = _ffn_act_fwd(f"ffn{i}_act", up, cw_l[i], cb_l[i], nb, s)
        yf = _mm(f"ffn{i}_down", act, wd4[i], (t // tm_, 1, half),
                 pl.BlockSpec((None, tm_, c_up), lambda a, b, k: (k, a, 0)),
                 pl.BlockSpec((None, c_up, d), lambda a, b, k: (k, 0, 0)),
                 pl.BlockSpec((tm_, d), lambda a, b, k: (a, 0)), jax.ShapeDtypeStruct((t, d), F32), _NN, (tm_, d))
        xout = _gate_add(f"ffn{i}_res", xin, yf, gate, nb)
        return xout, (h, up, act, yf)

    tm_ = _tile(t, 512)
    sh1, sc1, g1, sh2, sc2, g2 = mods[0]
    h1 = _norm_mod_fwd("attn_norm", x0, row(norm_mix[0]), sh1, sc1, nb)
    cq = wq8.shape[2]
    qkv = _mm("attn_qkv", h1, wq8, (t // tm_, N_DEV, 1),
              pl.BlockSpec((tm_, d), lambda a, b, k: (a, 0)), pl.BlockSpec((None, d, cq), lambda a, b, k: (b, 0, 0)),
              pl.BlockSpec((tm_, cq), lambda a, b, k: (a, b)), jax.ShapeDtypeStruct((t, 3 * d), BF16), _NN, (tm_, cq))
    o_att, car_att, (wo8, win8, wglu8, wos8, wup8_0, wup8_1, wd8_0, wd8_1, cw8, dskip8, bglu8) = _attn_fwd(
        qkv, nb, s, d, later)
    wo = wo8.reshape(d, d)
    win = win8.reshape(d, d)
    wglu = wglu8.reshape(d, d)
    wos = wos8.reshape(d, d)
    wup8 = [wup8_0, wup8_1]
    wd4 = [wd8_0.reshape(half, 2 * r_dn, d), wd8_1.reshape(half, 2 * r_dn, d)]
    cw_l = [cw8[:, 0], cw8[:, 1]]
    dskip_f = dskip8.reshape(1, d)
    bglu_f = bglu8.reshape(1, d)
    ya = _mm_nn("attn_out", o_att, wo, F32)
    x1 = _gate_add("attn_res", x0, ya, g1, nb)
    x2, ffn0_saved = ffn_fwd(0, x1, sh2, sc2, g2)

    sh1b, sc1b, g1b, sh2b, sc2b, g2b = mods[1]
    ops = _ssm_operators(a_re[0], a_im[0], log_dt[0], b_re[0], b_im[0], c_re[0], c_im[0])
    h3 = _norm_mod_fwd("ssm_norm", x2, row(norm_mix[1]), sh1b, sc1b, nb)
    u = _mm_nn("ssm_in", h3, win, BF16)
    ys_core, xp_re, xp_im = _ssm_core_fwd(u, ops, nb)
    y_ssm, z_ssm = _ssm_post_fwd(ys_core, u, dskip_f)
    gl = _mm_nn("ssm_glu", z_ssm, wglu, F32)
    gg = _glu_fwd(z_ssm, gl, bglu_f)
    ys2 = _mm_nn("ssm_out", gg, wos, F32)
    x3 = _gate_add("ssm_res", x2, ys2, g1b, nb)
    x4, ffn1_saved = ffn_fwd(1, x3, sh2b, sc2b, g2b)

    dx4, g_norm_out, dsh_f, dsc_f, loss_blk = _norm_mod_bwd(
        "final_norm", None, x4, row(norm_out), sh_f, sc_f, None, loss_target.reshape(t, d), nb)
    loss = lax.psum(loss_blk[0, 0], ("x", "y", "c"))

    def ffn_bwd(i, dxo, xin, sc, gate, saved):
        h, up, act, yf = saved
        dyf, dgate = _gate_bwd(f"ffn{i}_res_bwd", dxo, yf, gate, nb)
        dact = _mm(f"ffn{i}_down_dx", dyf, wd4[i], (t // tm_, half, 1),
                   pl.BlockSpec((tm_, d), lambda a, b, k: (a, 0)), pl.BlockSpec((None, c_up, d), lambda a, b, k: (b, 0, 0)),
                   pl.BlockSpec((None, tm_, c_up), lambda a, b, k: (b, a, 0)),
                   jax.ShapeDtypeStruct((half, t, c_up), BF16), _NT, (tm_, c_up))
        tk = _tile(t, 1024)
        gwd = _mm(f"ffn{i}_down_dw", act, dyf, (half, 1, t // tk),
                  pl.BlockSpec((None, tk, c_up), lambda a, b, k: (a, k, 0)), pl.BlockSpec((tk, d), lambda a, b, k: (k, 0)),
                  pl.BlockSpec((None, c_up, d), lambda a, b, k: (a, 0, 0)),
                  jax.ShapeDtypeStruct((half, c_up, d), BF16), _TN, (c_up, d))
        dup, dcw, dcb = _ffn_act_bwd(f"ffn{i}_act_bwd", up, dact, cw_l[i], cb_l[i], nb, s)
        dh = _mm(f"ffn{i}_up_dx", dup, wup8[i], (t // tm_, 1, N_DEV),
                 pl.BlockSpec((None, tm_, c_up), lambda a, b, k: (k, a, 0)),
                 pl.BlockSpec((None, d, c_up), lambda a, b, k: (k, 0, 0)),
                 pl.BlockSpec((tm_, d), lambda a, b, k: (a, 0)), jax.ShapeDtypeStruct((t, d), F32), _NT, (tm_, d))
        gwup = _mm(f"ffn{i}_up_dw", h, dup, (1, N_DEV, t // tk),
                   pl.BlockSpec((tk, d), lambda a, b, k: (k, 0)), pl.BlockSpec((None, tk, c_up), lambda a, b, k: (b, k, 0)),
                   pl.BlockSpec((None, d, c_up), lambda a, b, k: (b, 0, 0)),
                   jax.ShapeDtypeStruct((N_DEV, d, c_up), BF16), _TN, (d, c_up))
        dxi, g_norm, dsh, dsc = _norm_mod_bwd(f"ffn{i}_norm_bwd", dh, xin, row(norm_ffn[i]), None, sc, dxo, None, nb)
        return dxi, (gwup, gwd.reshape(N_DEV, r_dn, d), dcw, dcb, g_norm, dsh, dsc, dgate)

    dx3, (gwup1, gwd1, dcw1, dcb1, g_nffn1, dsh2b, dsc2b, dg2b) = ffn_bwd(1, dx4, x3, sc2b, g2b, ffn1_saved)

    dys2, dg1b = _gate_bwd("ssm_res_bwd", dx3, ys2, g1b, nb)
    dgg = _mm_nt("ssm_out_dx", dys2, wos, F32)
    gwos = _mm_tn("ssm_out_dw", gg, dys2, BF16)
    dgl, dz1, g_bglu = _glu_bwd(dgg, z_ssm, gl, bglu_f)
    dz2 = _mm_nt("ssm_glu_dx", dgl, wglu, F32)
    gwglu = _mm_tn("ssm_glu_dw", z_ssm, dgl, BF16)
    dy_ssm, du_skip, g_dskip = _ssm_post_bwd(dz1, dz2, y_ssm, u, dskip_f)
    du_core, d_ops = _ssm_core_bwd(dy_ssm, u, xp_re, xp_im, ops, nb)
    du = _add_cast(du_core, du_skip)
    dh3 = _mm_nt("ssm_in_dx", du, win, F32)
    gwin = _mm_tn("ssm_in_dw", h3, du, BF16)
    dx2, g_nmix1, dsh1b, dsc1b = _norm_mod_bwd("ssm_norm_bwd", dh3, x2, row(norm_mix[1]), None, sc1b, dx3, None, nb)
    _, ops_vjp = jax.vjp(_ssm_operators, a_re[0], a_im[0], log_dt[0], b_re[0], b_im[0], c_re[0], c_im[0])
    g_ssm_params = ops_vjp(d_ops)

    dx1, (gwup0, gwd0, dcw0, dcb0, g_nffn0, dsh2, dsc2, dg2) = ffn_bwd(0, dx2, x1, sc2, g2, ffn0_saved)

    dya, dg1 = _gate_bwd("attn_res_bwd", dx1, ya, g1, nb)
    do_att = _mm_nt("attn_out_dx", dya, wo, BF16)
    gwo = _mm_tn("attn_out_dw", o_att, dya, BF16)
    rows8 = lambda a: a.reshape(N_DEV, d // N_DEV, d)
    early = _Exchange([rows8(gwo), rows8(gwin), rows8(gwglu), rows8(gwos), gwup0, gwup1, gwd0, gwd1,
                       _pack(g_ssm_params)], [False] * 8 + [True])
    dq, dk, dv, (ro, rin, rglu, ros, rup0, rup1, rd0, rd1, ssm8) = _attn_bwd(qkv, car_att, do_att, nb, s, d, early)
    dqkv = jnp.concatenate([dq, dk, dv], axis=1)
    dh1 = _mm("attn_qkv_dx", dqkv, wq8, (t // tm_, 1, N_DEV),
              pl.BlockSpec((tm_, cq), lambda a, b, k: (a, k)),
              pl.BlockSpec((None, d, cq), lambda a, b, k: (k, 0, 0)),
              pl.BlockSpec((tm_, d), lambda a, b, k: (a, 0)), jax.ShapeDtypeStruct((t, d), F32), _NT, (tm_, d))
    tk = _tile(t, 1024)
    gwq8 = _mm("attn_qkv_dw", h1, dqkv, (1, N_DEV, t // tk),
               pl.BlockSpec((tk, d), lambda a, b, k: (k, 0)),
               pl.BlockSpec((tk, cq), lambda a, b, k: (k, b)),
               pl.BlockSpec((None, d, cq), lambda a, b, k: (b, 0, 0)),
               jax.ShapeDtypeStruct((N_DEV, d, cq), BF16), _TN, (d, cq))
    dx0, g_nmix0, dsh1, dsc1 = _norm_mod_bwd("attn_norm_bwd", dh1, x0, row(norm_mix[0]), None, sc1, dx1, None, nb)
    grad_x = dx0.reshape(nb, s, d)

    dmod =[jnp.concatenate([a.reshape(nb, d) for a in grp], axis=1) for grp in
            ([dsh1, dsc1, dg1, dsh2, dsc2, dg2], [dsh1b, dsc1b, dg1b, dsh2b, dsc2b, dg2b])]
    dfin = jnp.concatenate([dsh_f.reshape(nb, d), dsc_f.reshape(nb, d)], axis=1)
    dmodfin = jnp.concatenate(dmod + [dfin], axis=1)
    ssm_shapes = [(g_ssm, SSM_P), (g_ssm, SSM_P), (g_ssm,), (g_ssm, SSM_P, SSM_H), (g_ssm, SSM_P, SSM_H),
                  (g_ssm, SSM_H, SSM_P), (g_ssm, SSM_H, SSM_P)]
    small_shapes = [(2, d), (2, d), (d,), (2, N_DEV * c_up), (d,), (d,), (2, N_DEV, 3, c_up)]
    small_partial = _pack([jnp.stack([g_nmix0, g_nmix1]), jnp.stack([g_nffn0, g_nffn1]), g_norm_out,
                           jnp.stack([dcb0, dcb1]), g_dskip, g_bglu, jnp.stack([dcw0, dcw1])])
    rq, dmf8, small8 = _exchange("exchange_last", [gwq8, dmodfin, small_partial], [False, True, True])
    (g_norm_mix, g_norm_ffn, g_norm_out_s, g_conv_b, g_dskip_full, g_bglu_full, g_cw_full) = _unpack(
        _sum_parts("sum_small_grads", small8), small_shapes)
    g_a_re, g_a_im, g_log_dt, g_b_re, g_b_im, g_c_re, g_c_im = _unpack(_sum_parts("sum_ssm_grads", ssm8), ssm_shapes)

    dall = dmf8.reshape(n_seq, 14 * d)
    dmod_loc = jnp.stack([lax.dynamic_slice_in_dim(dall[:, i * 6 * d:(i + 1) * 6 * d], me * cm, cm, axis=1)
                          for i in range(2)])
    dfin_loc = lax.dynamic_slice_in_dim(dall[:, 12 * d:], me * cf, cf, axis=1)
    g_w_mod, g_w_fin, g_bias = _modfin_bwd(c_act.T, dmod_loc, dfin_loc, dall)
    g_b_mod = g_bias[0, :12 * d].reshape(2, 6 * d)
    g_b_fin = g_bias[0, 12 * d:]

    def big(name, parts, w, m, v):
        shp = w.shape
        r2 = lambda a: a.reshape(-1, shp[-1])
        res = _adamw(name, parts.reshape(parts.shape[0], -1, shp[-1]), r2(w), r2(m), r2(v))
        return [a.reshape(shp) for a in res]

    upd = {}
    upd["w_mod"] = big("adamw_w_mod", g_w_mod[None], w_mod, m_w_mod, v_w_mod)
    upd["w_fin"] = big("adamw_w_fin", g_w_fin[None], w_fin, m_w_fin, v_w_fin)
    upd["w_qkv"] = big("adamw_w_qkv", rq, w_qkv, m_w_qkv, v_w_qkv)
    upd["w_o_attn"] = big("adamw_w_o_attn", ro, w_o_attn, m_w_o_attn, v_w_o_attn)
    upd["w_in_ssm"] = big("adamw_w_in_ssm", rin, w_in_ssm, m_w_in_ssm, v_w_in_ssm)
    upd["w_glu"] = big("adamw_w_glu", rglu, w_glu, m_w_glu, v_w_glu)
    upd["w_o_ssm"] = big("adamw_w_o_ssm", ros, w_o_ssm, m_w_o_ssm, v_w_o_ssm)
    up_l = [big(f"adamw_w_up{i}", r, w_up[i], m_w_up[i], v_w_up[i]) for i, r in enumerate((rup0, rup1))]
    upd["w_up"] = [jnp.stack([up_l[0][j], up_l[1][j]]) for j in range(4)]
    dn_l = [big(f"adamw_w_down{i}", r, w_down[i], m_w_down[i], v_w_down[i]) for i, r in enumerate((rd0, rd1))]
    upd["w_down"] = [jnp.stack([dn_l[0][j], dn_l[1][j]]) for j in range(4)]

    g_dskip_loc = lax.dynamic_slice_in_dim(g_dskip_full.reshape(1, d), me * (d // N_DEV), d // N_DEV, axis=1)
    g_bglu_loc = lax.dynamic_slice_in_dim(g_bglu_full.reshape(1, d), me * (d // N_DEV), d // N_DEV, axis=1)
    g_cw_loc = lax.dynamic_slice_in_dim(g_cw_full, me, 1, axis=1).reshape(2, 3, c_up)
    small_names = ["norm_mix", "norm_ffn", "b_mod", "a_re", "a_im", "log_dt", "b_re", "b_im", "c_re", "c_im",
                   "d_skip", "b_glu", "conv_w", "conv_b", "norm_out", "b_fin"]
    small_g = [g_norm_mix, g_norm_ffn, g_b_mod, g_a_re[None], g_a_im[None], g_log_dt[None], g_b_re[None], g_b_im[None],
               g_c_re[None], g_c_im[None], g_dskip_loc, g_bglu_loc, g_cw_loc, g_conv_b, g_norm_out_s, g_b_fin]
    small_w = [norm_mix, norm_ffn, b_mod, a_re, a_im, log_dt, b_re, b_im, c_re, c_im, d_skip, b_glu, conv_w, conv_b,
               norm_out, b_fin]
    small_m = [m_norm_mix, m_norm_ffn, m_b_mod, m_a_re, m_a_im, m_log_dt, m_b_re, m_b_im, m_c_re, m_c_im, m_d_skip,
               m_b_glu, m_conv_w, m_conv_b, m_norm_out, m_b_fin]
    small_v = [v_norm_mix, v_norm_ffn, v_b_mod, v_a_re, v_a_im, v_log_dt, v_b_re, v_b_im, v_c_re, v_c_im, v_d_skip,
               v_b_glu, v_conv_w, v_conv_b, v_norm_out, v_b_fin]
    shapes = [w.shape for w in small_w]
    res = _adamw("adamw_small", _pack(small_g)[None], _pack(small_w), _pack(small_m), _pack(small_v))
    res = [_unpack(r, shapes) for r in res]
    for j, nm in enumerate(small_names):
        upd[nm] = [res[k][j] for k in range(4)]

    order = ["norm_mix", "norm_ffn", "w_mod", "b_mod", "w_qkv", "w_o_attn", "w_in_ssm", "a_re", "a_im", "log_dt",
             "b_re", "b_im", "c_re", "c_im", "d_skip", "w_glu", "b_glu", "w_o_ssm", "w_up", "conv_w", "conv_b",
             "w_down", "norm_out", "w_fin", "b_fin"]
    outs = [loss, grad_x]
    for k in range(4):
        outs += [upd[nm][k] for nm in order]
    return tuple(outs)
```

```python
import functools
import math

import jax
import jax.numpy as jnp
from jax import lax
from jax.experimental import pallas as pl
from jax.experimental.pallas import tpu as pltpu

F32 = jnp.float32
BF16 = jnp.bfloat16
MESH = pl.DeviceIdType.MESH

N_DEV = 8
HEAD_DIM = 64
ATT_BLK = 128
ATT_BQ = 256
ATT_UNROLL = 2
SSM_H = 16
SSM_P = 64
SSM_L = 4
EPS = 1e-6
ADAM_LR, ADAM_B1, ADAM_B2, ADAM_EPS, ADAM_WD, ADAM_STEP = 0.001, 0.9, 0.999, 1e-08, 0.01, 10
V7X_VMEM_LIMIT = 56 * 1024 * 1024
LANE = 128

_NN = (((1,), (0,)), ((), ()))
_NT = (((1,), (1,)), ((), ()))
_TN = (((0,), (0,)), ((), ()))


def _cp(sem):
    return pltpu.CompilerParams(dimension_semantics=sem, vmem_limit_bytes=V7X_VMEM_LIMIT)


def _tile(n, pref):
    if n <= pref:
        return n
    t = pref - pref % 16
    while t >= 16:
        if n % t == 0:
            return t
        t -= 16
    return n


def _mm(name, a, b, grid, a_spec, b_spec, out_spec, out_shape, dims, acc_shape, ex=None):
    nk = grid[-1]
    kax = len(grid) - 1
    ex = ex or _NoExchange()

    def body(*refs):
        a_ref, b_ref = refs[:2]
        ex_ins = refs[2:2 + ex.n]
        o_ref = refs[2 + ex.n]
        ex_outs = refs[3 + ex.n:3 + 2 * ex.n]
        acc_ref = refs[3 + 2 * ex.n]
        sems = refs[4 + 2 * ex.n:]
        first, last = _grid_ends(grid)
        k = pl.program_id(kax)

        @pl.when(first)
        def _():
            ex.start(ex_ins, ex_outs, sems)

        @pl.when(k == 0)
        def _():
            acc_ref[...] = jnp.zeros(acc_shape, F32)

        acc_ref[...] += lax.dot_general(a_ref[...].astype(BF16), b_ref[...].astype(BF16), dims,
                                        preferred_element_type=F32)

        @pl.when(k == nk - 1)
        def _():
            o_ref[...] = acc_ref[...].astype(o_ref.dtype)

        @pl.when(last)
        def _():
            ex.wait(ex_ins, ex_outs, sems)

    sem = ("arbitrary",) * len(grid) if ex.n else ("parallel",) * kax + ("arbitrary",)
    res = pl.pallas_call(
        body, name=name, grid=grid, in_specs=[a_spec, b_spec] + ex.specs, out_specs=[out_spec] + ex.specs,
        out_shape=[out_shape] + ex.out_shape, scratch_shapes=[pltpu.VMEM(acc_shape, F32)] + ex.scratch,
        compiler_params=_cp(sem))(a, b, *ex.arrs)
    return (res[0], list(res[1:])) if ex.n else res[0]


def _mm_nn(name, a, w, out_dtype):
    m, k = a.shape
    n = w.shape[1]
    tm, tn, tk = _tile(m, 512), _tile(n, 1024), _tile(k, 1024)
    return _mm(name, a, w, (m // tm, n // tn, k // tk),
               pl.BlockSpec((tm, tk), lambda i, j, kk: (i, kk)), pl.BlockSpec((tk, tn), lambda i, j, kk: (kk, j)),
               pl.BlockSpec((tm, tn), lambda i, j, kk: (i, j)), jax.ShapeDtypeStruct((m, n), out_dtype), _NN, (tm, tn))


def _mm_nt(name, a, w, out_dtype):
    m, n = a.shape
    k = w.shape[0]
    tm, tko, tn = _tile(m, 512), _tile(k, 1024), _tile(n, 1024)
    return _mm(name, a, w, (m // tm, k // tko, n // tn),
               pl.BlockSpec((tm, tn), lambda i, j, kk: (i, kk)), pl.BlockSpec((tko, tn), lambda i, j, kk: (j, kk)),
               pl.BlockSpec((tm, tko), lambda i, j, kk: (i, j)), jax.ShapeDtypeStruct((m, k), out_dtype), _NT, (tm, tko))


def _mm_tn(name, a, b, out_dtype):
    t, m = a.shape
    n = b.shape[1]
    tm, tn, tk = _tile(m, 512), _tile(n, 1024), _tile(t, 1024)
    return _mm(name, a, b, (m // tm, n // tn, t // tk),
               pl.BlockSpec((tk, tm), lambda i, j, kk: (kk, i)), pl.BlockSpec((tk, tn), lambda i, j, kk: (kk, j)),
               pl.BlockSpec((tm, tn), lambda i, j, kk: (i, j)), jax.ShapeDtypeStruct((m, n), out_dtype), _TN, (tm, tn))


def _norm_mod_fwd(name, x, g, shift, scale, nb):
    t, d = x.shape
    s = t // nb
    tr = _tile(s, 512)
    nt = s // tr

    def body(x_ref, g_ref, sh_ref, sc_ref, h_ref):
        xv = x_ref[...]
        r = lax.rsqrt(jnp.mean(xv * xv, axis=-1, keepdims=True) + EPS)
        y = xv * r * g_ref[...]
        h_ref[...] = (y * (1.0 + sc_ref[...]) + sh_ref[...]).astype(h_ref.dtype)

    row = pl.BlockSpec((tr, d), lambda b, i: (b * nt + i, 0))
    vec = pl.BlockSpec((None, 1, d), lambda b, i: (b, 0, 0))
    return pl.pallas_call(body, name=name, grid=(nb, nt),
                          in_specs=[row, pl.BlockSpec((1, d), lambda b, i: (0, 0)), vec, vec],
                          out_specs=row, out_shape=jax.ShapeDtypeStruct((t, d), BF16),
                          compiler_params=_cp(("parallel", "parallel")))(x, g, shift, scale)


def _norm_mod_bwd(name, dh, x, g, shift, scale, dres, target, nb):
    t, d = x.shape
    s = t // nb
    tr = _tile(s, 256)
    nt = s // tr
    final = target is not None

    def body(*refs):
        if final:
            x_ref, g_ref, sh_ref, sc_ref, tg_ref, dx_ref, dg_ref, dsh_ref, dsc_ref, loss_ref = refs
        else:
            dh_ref, x_ref, g_ref, sc_ref, dres_ref, dx_ref, dg_ref, dsh_ref, dsc_ref = refs
        b, i = pl.program_id(0), pl.program_id(1)
        xv = x_ref[...]
        gv = g_ref[...]
        r = lax.rsqrt(jnp.mean(xv * xv, axis=-1, keepdims=True) + EPS)
        nrm = xv * r
        y = nrm * gv
        one_sc = 1.0 + sc_ref[...]
        if final:
            err = y * one_sc + sh_ref[...] - tg_ref[...]
            dhv = err * (1.0 / d)
        else:
            dhv = dh_ref[...].astype(F32)
        dy = dhv * one_sc
        dn = dy * gv
        dxv = r * (dn - nrm * jnp.mean(dn * nrm, axis=-1, keepdims=True))
        if final:
            dx_ref[...] = dxv
        else:
            dx_ref[...] = dres_ref[...] + dxv

        @pl.when(i == 0)
        def _():
            dsh_ref[...] = jnp.zeros_like(dsh_ref)
            dsc_ref[...] = jnp.zeros_like(dsc_ref)

        @pl.when((i == 0) & (b == 0))
        def _():
            dg_ref[...] = jnp.zeros_like(dg_ref)
            if final:
                loss_ref[...] = jnp.zeros_like(loss_ref)

        dsh_ref[...] += jnp.sum(dhv, axis=0, keepdims=True)
        dsc_ref[...] += jnp.sum(dhv * y, axis=0, keepdims=True)
        dg_ref[...] += jnp.sum(dy * nrm, axis=0, keepdims=True)
        if final:
            loss_ref[...] += (0.5 / d) * jnp.sum(err * err)

    row = pl.BlockSpec((tr, d), lambda b, i: (b * nt + i, 0))
    vec = pl.BlockSpec((None, 1, d), lambda b, i: (b, 0, 0))
    gsp = pl.BlockSpec((1, d), lambda b, i: (0, 0))
    out_specs = [row, gsp, vec, vec]
    out_shape = [jax.ShapeDtypeStruct((t, d), F32), jax.ShapeDtypeStruct((1, d), F32),
                 jax.ShapeDtypeStruct((nb, 1, d), F32), jax.ShapeDtypeStruct((nb, 1, d), F32)]
    if final:
        ins, in_specs = [x, g, shift, scale, target], [row, gsp, vec, vec, row]
        out_specs.append(pl.BlockSpec((8, LANE), lambda b, i: (0, 0)))
        out_shape.append(jax.ShapeDtypeStruct((8, LANE), F32))
    else:
        ins, in_specs = [dh, x, g, scale, dres], [row, row, gsp, vec, row]
    return pl.pallas_call(body, name=name, grid=(nb, nt), in_specs=in_specs, out_specs=out_specs,
                          out_shape=out_shape, compiler_params=_cp(("arbitrary", "arbitrary")))(*ins)


def _gate_add(name, x, y, gate, nb):
    t, d = x.shape
    s = t // nb
    tr = _tile(s, 512)
    nt = s // tr

    def body(x_ref, y_ref, g_ref, o_ref):
        o_ref[...] = x_ref[...] + g_ref[...] * y_ref[...]

    row = pl.BlockSpec((tr, d), lambda b, i: (b * nt + i, 0))
    vec = pl.BlockSpec((None, 1, d), lambda b, i: (b, 0, 0))
    return pl.pallas_call(body, name=name, grid=(nb, nt), in_specs=[row, row, vec], out_specs=row,
                          out_shape=jax.ShapeDtypeStruct((t, d), F32),
                          compiler_params=_cp(("parallel", "parallel")))(x, y, gate)


def _gate_bwd(name, dx, y, gate, nb):
    t, d = dx.shape
    s = t // nb
    tr = _tile(s, 512)
    nt = s // tr

    def body(dx_ref, y_ref, g_ref, dy_ref, dg_ref):
        dxv = dx_ref[...]
        dy_ref[...] = (g_ref[...] * dxv).astype(dy_ref.dtype)

        @pl.when(pl.program_id(1) == 0)
        def _():
            dg_ref[...] = jnp.zeros_like(dg_ref)

        dg_ref[...] += jnp.sum(dxv * y_ref[...], axis=0, keepdims=True)

    row = pl.BlockSpec((tr, d), lambda b, i: (b * nt + i, 0))
    vec = pl.BlockSpec((None, 1, d), lambda b, i: (b, 0, 0))
    return pl.pallas_call(body, name=name, grid=(nb, nt), in_specs=[row, row, vec], out_specs=[row, vec],
                          out_shape=[jax.ShapeDtypeStruct((t, d), BF16), jax.ShapeDtypeStruct((nb, 1, d), F32)],
                          compiler_params=_cp(("parallel", "arbitrary")))(dx, y, gate)


def _log_sigmoid(z):
    return jnp.minimum(z, 0.0) - jnp.log(1.0 + jnp.exp(-jnp.abs(z)))


def _split_dot(v, tri):
    hi = v.astype(BF16)
    lo = (v - hi.astype(F32)).astype(BF16)
    return (jnp.dot(hi, tri, preferred_element_type=F32) + jnp.dot(lo, tri, preferred_element_type=F32))


def _grid_ends(grid):
    ids = [pl.program_id(a) for a in range(len(grid))]
    first = functools.reduce(lambda u, w: u & w, [i == 0 for i in ids])
    last = functools.reduce(lambda u, w: u & w, [i == n - 1 for i, n in zip(ids, grid)])
    return first, last


def _attn_fwd(qkv, nb, s, d, ex):
    t = nb * s
    npair = d // LANE
    bk = ATT_BLK
    bq = min(ATT_BQ, s)
    nq = s // bq
    kpq = bq // bk
    nheads = LANE // HEAD_DIM
    scale = HEAD_DIM ** -0.5
    grid = (nb, npair, nq)
    assert s // bk <= HEAD_DIM, "one carry lane per key block and head"
    assert bk == LANE, "the running sums are kept one 128-lane tile wide"
    assert kpq == ATT_UNROLL, "one loop trip covers exactly the key blocks under a query block's diagonal"

    def body(*refs):
        q_ref, k_ref, v_ref = refs[:3]
        ex_ins = refs[3:3 + ex.n]
        o_ref, car_ref = refs[3 + ex.n:5 + ex.n]
        ex_outs = refs[5 + ex.n:5 + 2 * ex.n]
        acc_s, run_s, z_s, arg_s = refs[5 + 2 * ex.n:9 + 2 * ex.n]
        sems = refs[9 + 2 * ex.n:]
        first, last = _grid_ends(grid)

        @pl.when(first)
        def _():
            ex.start(ex_ins, ex_outs, sems)

        qi = pl.program_id(2)
        q = q_ref[...]
        lane = lax.broadcasted_iota(jnp.int32, (1, LANE), 1)
        row = lax.broadcasted_iota(jnp.int32, (bq, bk), 0)
        col = lax.broadcasted_iota(jnp.int32, (bq, bk), 1)
        trow = lax.broadcasted_iota(jnp.int32, (bk, bk), 0)
        tcol = lax.broadcasted_iota(jnp.int32, (bk, bk), 1)
        tri = (trow > tcol).astype(BF16)
        hms = [(lane // HEAD_DIM) == hh for hh in range(nheads)]
        qhs = [jnp.where(hm, q, jnp.zeros_like(q)) * scale for hm in hms]
        car_ref[...] = jnp.zeros((bq, LANE), F32)
        acc_s[...] = jnp.zeros_like(acc_s)
        run_s[...] = jnp.zeros_like(run_s)
        nkb = (qi + 1) * kpq

        ntrip = nkb // ATT_UNROLL

        def k0_of(jj, u):
            return pl.multiple_of((nkb - 1 - (ATT_UNROLL * jj + u)) * bk, bk)

        def scores(jj):
            jc = jnp.minimum(jj, ntrip - 1)
            for u in range(ATT_UNROLL):
                kj = k_ref[pl.ds(k0_of(jc, u), bk), :]
                for hh in range(nheads):
                    z_s[hh, u] = lax.dot_general(qhs[hh], kj, _NT, preferred_element_type=F32)

        def exponents(jj, masked):
            car = car_ref[...]
            for hh in range(nheads):
                run = run_s[hh]
                for u in range(ATT_UNROLL):
                    z = z_s[hh, u]
                    lb = _log_sigmoid(z)
                    l1 = lb - z
                    if masked:
                        mask = (k0_of(jj, u) + col) < (qi * bq + row)
                        l1 = jnp.where(mask, l1, 0.0)
                    arg = lb + (_split_dot(l1, tri) + run)
                    arg_s[hh, u] = jnp.where(mask, arg, -1e30) if masked else arg
                    car = jnp.where(lane == hh * HEAD_DIM + (nkb - 1 - (ATT_UNROLL * jj + u)), run, car)
                    run = run + jnp.sum(l1, axis=1, keepdims=True)
                run_s[hh] = run
            car_ref[...] = car

        def weigh(jj):
            for hh in range(nheads):
                acc = None
                for u in range(ATT_UNROLL):
                    vj = v_ref[pl.ds(k0_of(jj, u), bk), :]
                    pv = jnp.dot(jnp.exp(arg_s[hh, u]).astype(BF16), vj, preferred_element_type=F32)
                    acc = pv if acc is None else acc + pv
                acc_s[hh] += acc

        def step(jj, carry):
            zn = [[lax.dot_general(qhs[hh], k_ref[pl.ds(k0_of(jnp.minimum(jj + 1, ntrip - 1), u), bk), :], _NT,
                                   preferred_element_type=F32) for u in range(ATT_UNROLL)] for hh in range(nheads)]
            weigh(jj - 1)
            exponents(jj, False)
            for hh in range(nheads):
                for u in range(ATT_UNROLL):
                    z_s[hh, u] = zn[hh][u]
            return carry

        scores(0)
        exponents(0, True)
        scores(1)
        lax.fori_loop(1, ntrip, step, 0)
        weigh(ntrip - 1)
        out = acc_s[0]
        for hh in range(1, nheads):
            out = jnp.where(hms[hh], acc_s[hh], out)
        o_ref[...] = out.astype(o_ref.dtype)

        @pl.when(last)
        def _():
            ex.wait(ex_ins, ex_outs, sems)

    qspec = pl.BlockSpec((bq, LANE), lambda b, p, i: (b * nq + i, p))
    res = pl.pallas_call(
        body, name="attn_fwd", grid=grid,
        in_specs=[qspec,
                  pl.BlockSpec((s, LANE), lambda b, p, i: (b, npair + p)),
                  pl.BlockSpec((s, LANE), lambda b, p, i: (b, 2 * npair + p))] + ex.specs,
        out_specs=[qspec, qspec] + ex.specs,
        out_shape=[jax.ShapeDtypeStruct((t, d), BF16), jax.ShapeDtypeStruct((t, d), F32)] + ex.out_shape,
        scratch_shapes=[pltpu.VMEM((nheads, bq, LANE), F32), pltpu.VMEM((nheads, bq, LANE), F32),
                        pltpu.VMEM((nheads, ATT_UNROLL, bq, bk), F32),
                        pltpu.VMEM((nheads, ATT_UNROLL, bq, bk), F32)] + ex.scratch,
        compiler_params=_cp(("arbitrary", "arbitrary", "arbitrary")))(qkv, qkv, qkv, *ex.arrs)
    return res[0], res[1], list(res[2:])


def _attn_bwd(qkv, car, do, nb, s, d, ex):
    t = nb * s
    npair = d // LANE
    bk = ATT_BLK
    bq = min(ATT_BQ, s)
    nq = s // bq
    kpq = bq // bk
    nheads = LANE // HEAD_DIM
    scale = HEAD_DIM ** -0.5
    grid = (nb, npair, nq)

    def body(*refs):
        q_ref, k_ref, v_ref, car_ref, do_ref = refs[:5]
        ex_ins = refs[5:5 + ex.n]
        dq_ref, dk_ref, dv_ref = refs[5 + ex.n:8 + ex.n]
        ex_outs = refs[8 + ex.n:8 + 2 * ex.n]
        dk_acc, dv_acc, dq_s, rune_s, z_s, da_s, dz_s, a_s = refs[8 + 2 * ex.n:16 + 2 * ex.n]
        sems = refs[16 + 2 * ex.n:]
        first, last = _grid_ends(grid)

        @pl.when(first)
        def _():
            ex.start(ex_ins, ex_outs, sems)

        qi = pl.program_id(2)

        @pl.when(qi == 0)
        def _():
            dk_acc[...] = jnp.zeros_like(dk_acc)
            dv_acc[...] = jnp.zeros_like(dv_acc)

        q = q_ref[...]
        dov = do_ref[...]
        lane = lax.broadcasted_iota(jnp.int32, (1, LANE), 1)
        row = lax.broadcasted_iota(jnp.int32, (bq, bk), 0)
        col = lax.broadcasted_iota(jnp.int32, (bq, bk), 1)
        trow = lax.broadcasted_iota(jnp.int32, (bk, bk), 0)
        tcol = lax.broadcasted_iota(jnp.int32, (bk, bk), 1)
        tri_suf = (trow > tcol).astype(BF16)
        tri_pre = (trow < tcol).astype(BF16)
        hms = [(lane // HEAD_DIM) == hh for hh in range(nheads)]
        qhs = [jnp.where(hm, q, jnp.zeros_like(q)) * scale for hm in hms]
        dohs = [jnp.where(hm, dov, jnp.zeros_like(dov)) for hm in hms]
        dq_s[...] = jnp.zeros_like(dq_s)
        rune_s[...] = jnp.zeros_like(rune_s)
        dz_s[...] = jnp.zeros_like(dz_s)
        a_s[...] = jnp.zeros_like(a_s)
        ntrip = (qi + 1) * kpq // ATT_UNROLL

        def k0_of(jj, u):
            return pl.multiple_of((ATT_UNROLL * jnp.maximum(jj, 0) + u) * bk, bk)

        def products(jj):
            ks = [k_ref[pl.ds(k0_of(jj, u), bk), :] for u in range(ATT_UNROLL)]
            vs = [v_ref[pl.ds(k0_of(jj, u), bk), :] for u in range(ATT_UNROLL)]
            zn = [[lax.dot_general(qhs[hh], kj, _NT, preferred_element_type=F32) for kj in ks] for hh in range(nheads)]
            dn = [[lax.dot_general(dohs[hh], vj, _NT, preferred_element_type=F32) for vj in vs] for hh in range(nheads)]
            return zn, dn

        def keep(zn, dn):
            for hh in range(nheads):
                for u in range(ATT_UNROLL):
                    z_s[hh, u] = zn[hh][u]
                    da_s[hh, u] = dn[hh][u]

        def middle(jj, masked):
            car = car_ref[...]
            for hh in range(nheads):
                run_e = rune_s[hh]
                for u in range(ATT_UNROLL):
                    z = z_s[hh, u]
                    lb = _log_sigmoid(z)
                    l1u = lb - z
                    if masked:
                        mask = (k0_of(jj, u) + col) < (qi * bq + row)
                    l1 = jnp.where(mask, l1u, 0.0) if masked else l1u
                    run = jnp.sum(jnp.where(lane == hh * HEAD_DIM + ATT_UNROLL * jj + u, car, 0.0), axis=1,
                                  keepdims=True)
                    a = jnp.exp(lb + (_split_dot(l1, tri_suf) + run))
                    if masked:
                        a = jnp.where(mask, a, 0.0)
                    e = da_s[hh, u] * a
                    dz = e * jnp.exp(l1u) - (_split_dot(e, tri_pre) + run_e) * jnp.exp(lb)
                    if masked:
                        dz = jnp.where(mask, dz, 0.0)
                    dz_s[hh, u] = dz.astype(BF16)
                    a_s[hh, u] = a.astype(BF16)
                    run_e = run_e + jnp.sum(e, axis=1, keepdims=True)
                rune_s[hh] = run_e

        def grads(jj):
            for u in range(ATT_UNROLL):
                k0 = k0_of(jj, u)
                kj = k_ref[pl.ds(k0, bk), :]
                for hh in range(nheads):
                    dzb = dz_s[hh, u]
                    dq_s[hh] += jnp.dot(dzb, kj, preferred_element_type=F32)
                    dkh = lax.dot_general(dzb, qhs[hh], _TN, preferred_element_type=F32)
                    dvh = lax.dot_general(a_s[hh, u], dohs[hh], _TN, preferred_element_type=F32)
                    dk_blk = dkh if hh == 0 else dk_blk + dkh
                    dv_blk = dvh if hh == 0 else dv_blk + dvh
                dk_acc[pl.ds(k0, bk), :] += dk_blk
                dv_acc[pl.ds(k0, bk), :] += dv_blk

        def step(jj, carry):
            zn, dn = products(jj + 1)
            grads(jj - 1)
            middle(jj, False)
            keep(zn, dn)
            return carry

        keep(*products(0))
        lax.fori_loop(0, ntrip - 1, step, 0)
        grads(ntrip - 2)
        middle(ntrip - 1, True)
        grads(ntrip - 1)
        dq_out = dq_s[0]
        for hh in range(1, nheads):
            dq_out = jnp.where(hms[hh], dq_s[hh], dq_out)
        dq_ref[...] = (dq_out * scale).astype(dq_ref.dtype)

        @pl.when(qi == nq - 1)
        def _():
            dk_ref[...] = dk_acc[...].astype(dk_ref.dtype)
            dv_ref[...] = dv_acc[...].astype(dv_ref.dtype)

        @pl.when(last)
        def _():
            ex.wait(ex_ins, ex_outs, sems)

    qspec = pl.BlockSpec((bq, LANE), lambda b, p, i: (b * nq + i, p))
    kvout = pl.BlockSpec((s, LANE), lambda b, p, i: (b, p))
    sds = jax.ShapeDtypeStruct((t, d), BF16)
    res = pl.pallas_call(
        body, name="attn_bwd", grid=grid,
        in_specs=[qspec,
                  pl.BlockSpec((s, LANE), lambda b, p, i: (b, npair + p)),
                  pl.BlockSpec((s, LANE), lambda b, p, i: (b, 2 * npair + p)),
                  qspec, qspec] + ex.specs,
        out_specs=[qspec, kvout, kvout] + ex.specs, out_shape=[sds, sds, sds] + ex.out_shape,
        scratch_shapes=[pltpu.VMEM((s, LANE), F32), pltpu.VMEM((s, LANE), F32),
                        pltpu.VMEM((nheads, bq, LANE), F32), pltpu.VMEM((nheads, bq, LANE), F32),
                        pltpu.VMEM((nheads, ATT_UNROLL, bq, bk), F32), pltpu.VMEM((nheads, ATT_UNROLL, bq, bk), F32),
                        pltpu.VMEM((nheads, ATT_UNROLL, bq, bk), BF16),
                        pltpu.VMEM((nheads, ATT_UNROLL, bq, bk), BF16)] + ex.scratch,
        compiler_params=_cp(("arbitrary", "arbitrary", "arbitrary")))(qkv, qkv, qkv, car, do, *ex.arrs)
    return res[0], res[1], res[2], list(res[3:])


def _conv3(u_ref, w, bias, c, r0, rc):
    x = u_ref[pl.ds(r0, rc), :].astype(F32)
    p0 = pl.multiple_of(jnp.maximum(r0 - 16, 0), 16)
    prev = u_ref[pl.ds(p0, 16), :].astype(F32)
    prev = jnp.where(c > 0, prev, 0.0)
    row = lax.broadcasted_iota(jnp.int32, (rc, 1), 0)
    s1 = jnp.where(row == 0, prev[15:16, :], pltpu.roll(x, 1, 0))
    s2 = jnp.where(row == 0, prev[14:15, :], jnp.where(row == 1, prev[15:16, :], pltpu.roll(x, 2, 0)))
    cv = w[2:3, :] * x + w[1:2, :] * s1 + w[0:1, :] * s2 + bias
    return cv, x, s1, s2


def _sigmoid(x):
    return 1.0 / (1.0 + jnp.exp(-x))


def _ffn_act_fwd(name, up8, cw8, cb8, nb, s):
    _, t, c_w = up8.shape
    rc = _tile(s, 256)
    nch = s // rc
    half = N_DEV // 2

    def body(ug_ref, uv_ref, wg_ref, wv_ref, bg_ref, bv_ref, act_ref):
        wg, wv, bg, bv = wg_ref[...], wv_ref[...], bg_ref[...], bv_ref[...]

        def chunk(c, carry):
            r0 = pl.multiple_of(c * rc, rc)
            cg = _conv3(ug_ref, wg, bg, c, r0, rc)[0]
            cv = _conv3(uv_ref, wv, bv, c, r0, rc)[0]
            act_ref[pl.ds(r0, rc), :] = (cg * _sigmoid(cg) * cv).astype(act_ref.dtype)
            return carry

        lax.fori_loop(0, nch, chunk, 0)

    def slab(off):
        return pl.BlockSpec((None, s, c_w), lambda k, b: (k + off, b, 0))

    def par(rows, off):
        return pl.BlockSpec((None, rows, c_w), lambda k, b: (k + off, 0, 0))

    return pl.pallas_call(
        body, name=name, grid=(half, nb),
        in_specs=[slab(0), slab(half), par(3, 0), par(3, half), par(1, 0), par(1, half)],
        out_specs=pl.BlockSpec((None, s, c_w), lambda k, b: (k, b, 0)),
        out_shape=jax.ShapeDtypeStruct((half, t, c_w), BF16),
        compiler_params=_cp(("parallel", "parallel")))(up8, up8, cw8, cw8, cb8, cb8)


def _ffn_act_bwd(name, up8, dact4, cw8, cb8, nb, s):
    _, t, c_w = up8.shape
    rc = _tile(s, 256)
    nch = s // rc
    half = N_DEV // 2

    def body(u_ref, da_ref, w_ref, b_ref, dup_ref, dcw_ref, dcb_ref):
        w2, b2 = w_ref[...], b_ref[...]
        row = lax.broadcasted_iota(jnp.int32, (rc, 1), 0)

        @pl.when(pl.program_id(1) == 0)
        def _():
            dcw_ref[...] = jnp.zeros_like(dcw_ref)
            dcb_ref[...] = jnp.zeros_like(dcb_ref)

        def chunk(i, carry):
            c = nch - 1 - i
            r0 = pl.multiple_of(c * rc, rc)
            convs = [_conv3(u_ref.at[h], w2[h], b2[h], c, r0, rc) for h in range(2)]
            gt, vl = convs[0][0], convs[1][0]
            da = da_ref[pl.ds(r0, rc), :].astype(F32)
            sg = _sigmoid(gt)
            dcvs = [da * vl * sg * (1.0 + gt * (1.0 - sg)), da * gt * sg]
            out = []
            for h in range(2):
                n0, n1, a0, a1, a2, ab = carry[6 * h:6 * h + 6]
                dcv, (_, x, s1, s2), w = dcvs[h], convs[h], w2[h]
                t1 = jnp.where(row == rc - 1, n0, pltpu.roll(dcv, rc - 1, 0))
                t2 = jnp.where(row == rc - 2, n0, jnp.where(row == rc - 1, n1, pltpu.roll(dcv, rc - 2, 0)))
                dup = w[2:3, :] * dcv + w[1:2, :] * t1 + w[0:1, :] * t2
                dup_ref[h, pl.ds(r0, rc), :] = dup.astype(dup_ref.dtype)
                out += [dcv[0:1, :], dcv[1:2, :],
                        a0 + jnp.sum(dcv * s2, axis=0, keepdims=True), a1 + jnp.sum(dcv * s1, axis=0, keepdims=True),
                        a2 + jnp.sum(dcv * x, axis=0, keepdims=True), ab + jnp.sum(dcv, axis=0, keepdims=True)]
            return tuple(out)

        z = jnp.zeros((1, c_w), F32)
        fin = lax.fori_loop(0, nch, chunk, (z,) * 12)
        for h in range(2):
            _, _, a0, a1, a2, ab = fin[6 * h:6 * h + 6]
            dcw_ref[h, 0:1, :] += a0
            dcw_ref[h, 1:2, :] += a1
            dcw_ref[h, 2:3, :] += a2
            dcb_ref[h] += ab

    def pair(rows, per_seq):
        return pl.BlockSpec((2, None, rows, c_w), (lambda k, b: (0, k, b, 0)) if per_seq else (lambda k, b: (0, k, 0, 0)))

    four = lambda a: a.reshape((2, half) + a.shape[1:])
    dup, dcw, dcb = pl.pallas_call(
        body, name=name, grid=(half, nb),
        in_specs=[pair(s, True), pl.BlockSpec((None, s, c_w), lambda k, b: (k, b, 0)), pair(3, False), pair(1, False)],
        out_specs=[pair(s, True), pair(3, False), pair(1, False)],
        out_shape=[jax.ShapeDtypeStruct((2, half, t, c_w), BF16), jax.ShapeDtypeStruct((2, half, 3, c_w), F32),
                   jax.ShapeDtypeStruct((2, half, 1, c_w), F32)],
        compiler_params=_cp(("parallel", "arbitrary")))(four(up8), dact4, four(cw8), four(cb8))
    return dup.reshape(N_DEV, t, c_w), dcw.reshape(N_DEV, 3, c_w), dcb.reshape(N_DEV, 1, c_w)


_GELU_C0 = math.sqrt(2.0 / math.pi)
_GELU_C1 = 0.044715


def _rowwise(name, body, ins, in_kinds, out_kinds, t, d, tr_pref=512):
    tr = _tile(t, tr_pref)
    row = pl.BlockSpec((tr, d), lambda i: (i, 0))
    vec = pl.BlockSpec((1, d), lambda i: (0, 0))
    in_specs = [row if k == "row" else vec for k in in_kinds]
    out_specs = [row if k[0] == "row" else vec for k in out_kinds]
    out_shape = [jax.ShapeDtypeStruct((t, d) if k[0] == "row" else (1, d), k[1]) for k in out_kinds]
    has_acc = any(k[0] == "acc" for k in out_kinds)
    return pl.pallas_call(body, name=name, grid=(t // tr,), in_specs=in_specs, out_specs=out_specs,
                          out_shape=out_shape,
                          compiler_params=_cp(("arbitrary",) if has_acc else ("parallel",)))(*ins)


def _ssm_post_fwd(ys, u, dskip):
    t, d = ys.shape

    def body(ys_ref, u_ref, ds_ref, y_ref, z_ref):
        y = ys_ref[...].astype(F32) + ds_ref[...] * u_ref[...].astype(F32)
        y_ref[...] = y
        th = jnp.tanh(_GELU_C0 * (y + _GELU_C1 * y * y * y))
        z_ref[...] = (0.5 * y * (1.0 + th)).astype(z_ref.dtype)

    return _rowwise("ssm_post_fwd", body, [ys, u, dskip], ["row", "row", "vec"],
                    [("row", F32), ("row", BF16)], t, d)


def _glu_fwd(z, gl, bglu):
    t, d = z.shape

    def body(z_ref, gl_ref, b_ref, o_ref):
        o_ref[...] = (z_ref[...].astype(F32) * _sigmoid(gl_ref[...] + b_ref[...])).astype(o_ref.dtype)

    return _rowwise("glu_fwd", body, [z, gl, bglu], ["row", "row", "vec"], [("row", BF16)], t, d)[0]


def _glu_bwd(dgg, z, gl, bglu):
    t, d = z.shape

    def body(dg_ref, z_ref, gl_ref, b_ref, dgl_ref, dz_ref, db_ref):
        sg = _sigmoid(gl_ref[...] + b_ref[...])
        dg = dg_ref[...]
        dgl = dg * z_ref[...].astype(F32) * sg * (1.0 - sg)
        dgl_ref[...] = dgl.astype(dgl_ref.dtype)
        dz_ref[...] = dg * sg

        @pl.when(pl.program_id(0) == 0)
        def _():
            db_ref[...] = jnp.zeros_like(db_ref)

        db_ref[...] += jnp.sum(dgl, axis=0, keepdims=True)

    return _rowwise("glu_bwd", body, [dgg, z, gl, bglu], ["row", "row", "row", "vec"],
                    [("row", BF16), ("row", F32), ("acc", F32)], t, d)


def _ssm_post_bwd(dz1, dz2, y, u, dskip):
    t, d = y.shape

    def body(a_ref, b_ref, y_ref, u_ref, ds_ref, dy_ref, du_ref, dd_ref):
        yv = y_ref[...]
        inner = _GELU_C0 * (yv + _GELU_C1 * yv * yv * yv)
        th = jnp.tanh(inner)
        dgelu = 0.5 * (1.0 + th) + 0.5 * yv * (1.0 - th * th) * _GELU_C0 * (1.0 + 3.0 * _GELU_C1 * yv * yv)
        dy = (a_ref[...] + b_ref[...]) * dgelu
        dy_ref[...] = dy.astype(dy_ref.dtype)
        du_ref[...] = dy * ds_ref[...]

        @pl.when(pl.program_id(0) == 0)
        def _():
            dd_ref[...] = jnp.zeros_like(dd_ref)

        dd_ref[...] += jnp.sum(dy * u_ref[...].astype(F32), axis=0, keepdims=True)

    return _rowwise("ssm_post_bwd", body, [dz1, dz2, y, u, dskip], ["row", "row", "row", "row", "vec"],
                    [("row", BF16), ("row", F32), ("acc", F32)], t, d)


def _add_cast(a, b):
    t, d = a.shape

    def body(a_ref, b_ref, o_ref):
        o_ref[...] = (a_ref[...].astype(F32) + b_ref[...].astype(F32)).astype(o_ref.dtype)

    return _rowwise("add_cast", body, [a, b], ["row", "row"], [("row", BF16)], t, d)[0]


def _ssm_scan(e_re, e_im, lam_re, lam_im, nb):
    r, n = e_re.shape
    nc = r // nb
    cb = _tile(n, 512)

    def body(er_ref, ei_ref, lr_ref, li_ref, xr_ref, xi_ref):
        lr, li = lr_ref[...], li_ref[...]
        rid = lax.broadcasted_iota(jnp.int32, (8, 1), 0)

        def tile(i, carry):
            out = []
            for b in range(nb):
                xr, xi = carry[2 * b:2 * b + 2]
                r0 = pl.multiple_of(b * nc + i * 8, 8)
                er, ei = er_ref[pl.ds(r0, 8), :], ei_ref[pl.ds(r0, 8), :]
                outr, outi = jnp.zeros((8, cb), F32), jnp.zeros((8, cb), F32)
                for j in range(8):
                    outr = jnp.where(rid == j, xr, outr)
                    outi = jnp.where(rid == j, xi, outi)
                    xr, xi = lr * xr - li * xi + er[j:j + 1, :], li * xr + lr * xi + ei[j:j + 1, :]
                xr_ref[pl.ds(r0, 8), :] = outr
                xi_ref[pl.ds(r0, 8), :] = outi
                out += [xr, xi]
            return tuple(out)

        lax.fori_loop(0, nc // 8, tile, (jnp.zeros((1, cb), F32),) * (2 * nb))

    mat = pl.BlockSpec((r, cb), lambda j: (0, j))
    vec = pl.BlockSpec((1, cb), lambda j: (0, j))
    sds = jax.ShapeDtypeStruct((r, n), F32)
    return pl.pallas_call(body, name="ssm_scan", grid=(n // cb,), in_specs=[mat, mat, vec, vec],
                          out_specs=[mat, mat], out_shape=[sds, sds],
                          compiler_params=_cp(("parallel",)))(e_re, e_im, lam_re, lam_im)


def _ssm_scan_bwd(dxp_re, dxp_im, xp_re, xp_im, lam_re, lam_im, nb):
    r, n = dxp_re.shape
    nc = r // nb
    cb = _tile(n, 512)

    def body(dr_ref, di_ref, xr_ref, xi_ref, lr_ref, li_ref, er_ref, ei_ref, dlr_ref, dli_ref):
        lr, li = lr_ref[...], li_ref[...]
        rid = lax.broadcasted_iota(jnp.int32, (8, 1), 0)

        def tile(i, carry):
            out = []
            for b in range(nb):
                gr, gi, alr, ali = carry[4 * b:4 * b + 4]
                r0 = pl.multiple_of(b * nc + (nc // 8 - 1 - i) * 8, 8)
                dr, di = dr_ref[pl.ds(r0, 8), :], di_ref[pl.ds(r0, 8), :]
                xr, xi = xr_ref[pl.ds(r0, 8), :], xi_ref[pl.ds(r0, 8), :]
                outr, outi = jnp.zeros((8, cb), F32), jnp.zeros((8, cb), F32)
                for j in range(7, -1, -1):
                    outr = jnp.where(rid == j, gr, outr)
                    outi = jnp.where(rid == j, gi, outi)
                    xrj, xij = xr[j:j + 1, :], xi[j:j + 1, :]
                    alr = alr + (gr * xrj + gi * xij)
                    ali = ali + (gi * xrj - gr * xij)
                    gr, gi = dr[j:j + 1, :] + lr * gr + li * gi, di[j:j + 1, :] + lr * gi - li * gr
                er_ref[pl.ds(r0, 8), :] = outr
                ei_ref[pl.ds(r0, 8), :] = outi
                out += [gr, gi, alr, ali]
            return tuple(out)

        fin = lax.fori_loop(0, nc // 8, tile, (jnp.zeros((1, cb), F32),) * (4 * nb))
        dlr_ref[...] = functools.reduce(lambda u, w: u + w, [fin[4 * b + 2] for b in range(nb)])
        dli_ref[...] = functools.reduce(lambda u, w: u + w, [fin[4 * b + 3] for b in range(nb)])

    mat = pl.BlockSpec((r, cb), lambda j: (0, j))
    vec = pl.BlockSpec((1, cb), lambda j: (0, j))
    sds = jax.ShapeDtypeStruct((r, n), F32)
    vds = jax.ShapeDtypeStruct((1, n), F32)
    return pl.pallas_call(body, name="ssm_scan_bwd", grid=(n // cb,), in_specs=[mat, mat, mat, mat, vec, vec],
                          out_specs=[mat, mat, vec, vec], out_shape=[sds, sds, vds, vds],
                          compiler_params=_cp(("parallel",)))(dxp_re, dxp_im, xp_re, xp_im, lam_re, lam_im)


def _ssm_operators(a_re, a_im, log_dt, b_re, b_im, c_re, c_im):
    g, p = a_re.shape
    h = b_re.shape[-1]
    ln = SSM_L
    sg = LANE // h
    na = g // sg
    hp = lax.Precision.HIGHEST
    lam = lax.complex(a_re, a_im)
    ldt = lam * jnp.exp(log_dt)[:, None]
    lam_bar = jnp.exp(ldt)
    bbar = ((lam_bar - 1.0) / lam)[..., None] * lax.complex(b_re, b_im)
    cm = lax.complex(c_re, c_im)
    steps = jnp.arange(ln + 1, dtype=F32)
    pw = jnp.exp(ldt[:, None, :] * steps[None, :, None])
    kd = jnp.einsum("ghp,gdp,gpk->gdhk", cm, pw[:, :ln], bbar, precision=hp).real

    def blockdiag(compact, rows_per, cols_per):
        spread = jnp.tile(jnp.eye(cols_per, dtype=F32), (1, sg))
        same = (jnp.arange(sg * rows_per)[:, None] // rows_per == jnp.arange(sg * cols_per)[None, :] // cols_per)
        return jnp.einsum("...rc,cl->...rl", compact, spread, precision=hp) * same.astype(F32)

    def by_supergroup(x, rows_per, cols_per):
        return x.reshape(na, sg, ln, rows_per, cols_per).transpose(0, 2, 1, 3, 4).reshape(na, ln, sg * rows_per, cols_per)

    kt = blockdiag(by_supergroup(kd.transpose(0, 1, 3, 2), h, h), h, h)
    zero = jnp.zeros_like(kt[:, 0])
    tm = jnp.concatenate([jnp.concatenate([kt[:, tau - sig] if tau >= sig else zero for tau in range(ln)], axis=2)
                          for sig in range(ln)], axis=1)
    wxc = (pw[:, ln - 1 - jnp.arange(ln)][:, :, :, None] * bbar[:, None]).transpose(0, 1, 3, 2)
    wx = lambda part: blockdiag(by_supergroup(part, h, p), h, p).reshape(na, ln * LANE, sg * p)
    cpc = (cm[:, None] * pw[:, 1:ln + 1][:, :, None, :]).transpose(0, 1, 3, 2)

    def wy(part):
        big = blockdiag(by_supergroup(part, p, h), p, h)
        return jnp.concatenate([big[:, tau] for tau in range(ln)], axis=2)

    lam_l = pw[:, ln]
    return (tm, wx(wxc.real), wx(wxc.imag), wy(cpc.real), wy(-cpc.imag),
            lam_l.real.reshape(1, g * p), lam_l.imag.reshape(1, g * p))


def _chunk_view(a):
    t, d = a.shape
    return a.reshape(t // SSM_L, SSM_L * d)


def _sg_specs(r4, d, wst):
    nblk = d // LANE
    cat = [pl.BlockSpec((r4, LANE), functools.partial(lambda j, tau: (0, tau * nblk + j), tau=tau))
           for tau in range(SSM_L)]
    plane = pl.BlockSpec((r4, wst), lambda j: (0, j))
    mat = lambda rows, cols: pl.BlockSpec((None, rows, cols), lambda j: (j, 0, 0))
    piece = pl.BlockSpec((r4, LANE), lambda j: (0, j))
    return cat, plane, mat, piece


def _lane_cat(refs):
    return jnp.concatenate([r[...] for r in refs], axis=1)


def _bdot(a, b, dims):
    return lax.dot_general(a.astype(BF16), b.astype(BF16), dims, preferred_element_type=F32)


def _ssm_core_fwd(u, ops, nb):
    tm, wxr, wxi, wyr, wyi, lam_re, lam_im = ops
    t, d = u.shape
    ln, na, wch, wst = SSM_L, tm.shape[0], tm.shape[1], wxr.shape[2]
    r4 = t // ln
    n = na * wst
    u4 = _chunk_view(u)
    cat, plane, mat, piece = _sg_specs(r4, d, wst)
    pds = jax.ShapeDtypeStruct((r4, n), F32)

    def states(*refs):
        ucat = _lane_cat(refs[:ln])
        wr_ref, wi_ref, er_ref, ei_ref = refs[ln:]
        er_ref[...] = _bdot(ucat, wr_ref[...], _NN)
        ei_ref[...] = _bdot(ucat, wi_ref[...], _NN)

    e_re, e_im = pl.pallas_call(
        states, name="ssm_states", grid=(na,), in_specs=cat + [mat(wch, wst)] * 2, out_specs=[plane, plane],
        out_shape=[pds, pds], compiler_params=_cp(("parallel",)))(*([u4] * ln), wxr, wxi)
    xp_re, xp_im = _ssm_scan(e_re, e_im, lam_re, lam_im, nb)

    def outputs(*refs):
        ucat = _lane_cat(refs[:ln])
        tm_ref, xr_ref, xi_ref, wr_ref, wi_ref = refs[ln:ln + 5]
        y = (_bdot(ucat, tm_ref[...], _NN) + _bdot(xr_ref[...], wr_ref[...], _NN)
             + _bdot(xi_ref[...], wi_ref[...], _NN))
        for tau, o_ref in enumerate(refs[ln + 5:]):
            o_ref[...] = y[:, tau * LANE:(tau + 1) * LANE].astype(o_ref.dtype)

    ys = pl.pallas_call(
        outputs, name="ssm_y", grid=(na,),
        in_specs=cat + [mat(wch, wch), plane, plane, mat(wst, wch), mat(wst, wch)], out_specs=[piece] * ln,
        out_shape=[jax.ShapeDtypeStruct((r4, d), BF16)] * ln,
        compiler_params=_cp(("parallel",)))(*([u4] * ln), tm, xp_re, xp_im, wyr, wyi)
    return jnp.concatenate(ys, axis=1).reshape(t, d), xp_re, xp_im


def _ssm_core_bwd(dy, u, xp_re, xp_im, ops, nb):
    tm, wxr, wxi, wyr, wyi, lam_re, lam_im = ops
    t, d = u.shape
    ln, na, wch, wst = SSM_L, tm.shape[0], tm.shape[1], wxr.shape[2]
    r4 = t // ln
    n = na * wst
    u4, dy4 = _chunk_view(u), _chunk_view(dy)
    cat, plane, mat, piece = _sg_specs(r4, d, wst)
    pds = jax.ShapeDtypeStruct((r4, n), F32)

    def dstates(*refs):
        dycat = _lane_cat(refs[:ln])
        wr_ref, wi_ref, dr_ref, di_ref = refs[ln:]
        dr_ref[...] = _bdot(dycat, wr_ref[...], _NT)
        di_ref[...] = _bdot(dycat, wi_ref[...], _NT)

    dxp_re, dxp_im = pl.pallas_call(
        dstates, name="ssm_dxp", grid=(na,), in_specs=cat + [mat(wst, wch)] * 2, out_specs=[plane, plane],
        out_shape=[pds, pds], compiler_params=_cp(("parallel",)))(*([dy4] * ln), wyr, wyi)
    de_re, de_im, dlam_re, dlam_im = _ssm_scan_bwd(dxp_re, dxp_im, xp_re, xp_im, lam_re, lam_im, nb)

    def dinputs(*refs):
        dycat = _lane_cat(refs[:ln])
        tm_ref, er_ref, ei_ref, wr_ref, wi_ref = refs[ln:ln + 5]
        du = (_bdot(dycat, tm_ref[...], _NT) + _bdot(er_ref[...], wr_ref[...], _NT)
              + _bdot(ei_ref[...], wi_ref[...], _NT))
        for tau, o_ref in enumerate(refs[ln + 5:]):
            o_ref[...] = du[:, tau * LANE:(tau + 1) * LANE].astype(o_ref.dtype)

    dus = pl.pallas_call(
        dinputs, name="ssm_du", grid=(na,),
        in_specs=cat + [mat(wch, wch), plane, plane, mat(wch, wst), mat(wch, wst)], out_specs=[piece] * ln,
        out_shape=[jax.ShapeDtypeStruct((r4, d), BF16)] * ln,
        compiler_params=_cp(("parallel",)))(*([dy4] * ln), tm, de_re, de_im, wxr, wxi)

    def doperators(*refs):
        ucat, dycat = _lane_cat(refs[:ln]), _lane_cat(refs[ln:2 * ln])
        er_ref, ei_ref, xr_ref, xi_ref, dtm_ref, dwxr_ref, dwxi_ref, dwyr_ref, dwyi_ref = refs[2 * ln:]
        dtm_ref[...] = _bdot(ucat, dycat, _TN)
        dwxr_ref[...] = _bdot(ucat, er_ref[...], _TN)
        dwxi_ref[...] = _bdot(ucat, ei_ref[...], _TN)
        dwyr_ref[...] = _bdot(xr_ref[...], dycat, _TN)
        dwyi_ref[...] = _bdot(xi_ref[...], dycat, _TN)

    mds = lambda rows, cols: jax.ShapeDtypeStruct((na, rows, cols), F32)
    d_ops = pl.pallas_call(
        doperators, name="ssm_dops", grid=(na,), in_specs=cat + cat + [plane] * 4,
        out_specs=[mat(wch, wch), mat(wch, wst), mat(wch, wst), mat(wst, wch), mat(wst, wch)],
        out_shape=[mds(wch, wch), mds(wch, wst), mds(wch, wst), mds(wst, wch), mds(wst, wch)],
        compiler_params=_cp(("parallel",)))(*([u4] * ln), *([dy4] * ln), de_re, de_im, xp_re, xp_im)
    return jnp.concatenate(dus, axis=1).reshape(t, d), (*d_ops, dlam_re, dlam_im)


def _modfin_fwd(c_all, w_mod, w_fin):
    n, d = c_all.shape
    nl, _, cm = w_mod.shape
    cf = w_fin.shape[1]
    width = nl * cm + cf
    hp = lax.Precision.HIGHEST

    def body(c_ref, wm_ref, wf_ref, act_ref, out_ref):
        cv = c_ref[...]
        act = cv * _sigmoid(cv)
        act_ref[...] = act
        for i in range(nl):
            out_ref[:, i * cm:(i + 1) * cm] = jnp.dot(act, wm_ref[i], preferred_element_type=F32, precision=hp)
        out_ref[:, nl * cm:] = jnp.dot(act, wf_ref[...], preferred_element_type=F32, precision=hp)

    return pl.pallas_call(body, name="modfin_fwd",
                          out_shape=[jax.ShapeDtypeStruct((n, d), F32), jax.ShapeDtypeStruct((n, width), F32)],
                          compiler_params=_cp(None))(c_all, w_mod, w_fin)


def _modfin_bwd(c_act_t, dmod_loc, dfin_loc, dall):
    d, n = c_act_t.shape
    nl, _, cm = dmod_loc.shape
    cf = dfin_loc.shape[1]
    hp = lax.Precision.HIGHEST

    def body(ct_ref, dm_ref, df_ref, da_ref, gwm_ref, gwf_ref, gb_ref):
        ct = ct_ref[...]
        for i in range(nl):
            gwm_ref[i] = jnp.dot(ct, dm_ref[i], preferred_element_type=F32, precision=hp)
        gwf_ref[...] = jnp.dot(ct, df_ref[...], preferred_element_type=F32, precision=hp)
        gb_ref[...] = jnp.sum(da_ref[...], axis=0, keepdims=True)

    return pl.pallas_call(body, name="modfin_bwd",
                          out_shape=[jax.ShapeDtypeStruct((nl, d, cm), F32), jax.ShapeDtypeStruct((d, cf), F32),
                                     jax.ShapeDtypeStruct((1, dall.shape[1]), F32)],
                          compiler_params=_cp(None))(c_act_t, dmod_loc, dfin_loc, dall)


def _adamw(name, gparts, w, m, v):
    n, r, c = gparts.shape
    tr = _tile(r, 256)

    def body(gp_ref, w_ref, m_ref, v_ref, g_ref, d_ref, mo_ref, vo_ref):
        _adamw_step(gp_ref, w_ref, m_ref, v_ref, g_ref, d_ref, mo_ref, vo_ref)

    mat = pl.BlockSpec((tr, c), lambda i: (i, 0))
    sds = jax.ShapeDtypeStruct((r, c), F32)
    return pl.pallas_call(body, name=name, grid=(r // tr,),
                          in_specs=[pl.BlockSpec((n, tr, c), lambda i: (0, i, 0)), mat, mat, mat],
                          out_specs=[mat] * 4, out_shape=[sds] * 4,
                          compiler_params=_cp(("parallel",)))(gparts, w, m, v)


def _adamw_step(gp_ref, w_ref, m_ref, v_ref, g_ref, d_ref, mo_ref, vo_ref):
    gsum = gp_ref[0].astype(F32)
    for j in range(1, gp_ref.shape[0]):
        gsum = gsum + gp_ref[j].astype(F32)
    mn = ADAM_B1 * m_ref[...] + (1.0 - ADAM_B1) * gsum
    vn = ADAM_B2 * v_ref[...] + (1.0 - ADAM_B2) * (gsum * gsum)
    g_ref[...] = gsum
    mo_ref[...] = mn
    vo_ref[...] = vn
    m_hat = mn * (1.0 / (1.0 - ADAM_B1 ** ADAM_STEP))
    v_hat = vn * (1.0 / (1.0 - ADAM_B2 ** ADAM_STEP))
    d_ref[...] = -ADAM_LR * (m_hat / (jnp.sqrt(v_hat) + ADAM_EPS) + ADAM_WD * w_ref[...])


def _adamw_many(name, entries):
    k = len(entries)

    def body(*refs):
        for i in range(k):
            _adamw_step(*refs[4 * i:4 * i + 4], *refs[4 * k + 4 * i:4 * k + 4 * i + 4])

    ops = [a for e in entries for a in e]
    out_shape = [jax.ShapeDtypeStruct(e[1].shape, F32) for e in entries for _ in range(4)]
    return pl.pallas_call(body, name=name, out_shape=out_shape, compiler_params=_cp(None))(*ops)


class _Exchange:
    def __init__(self, arrs, gathers):
        self.arrs = [pltpu.with_memory_space_constraint(a, pltpu.HBM) for a in arrs]
        self.gathers = list(gathers)
        self.n = len(arrs)
        self.out_shape = [pltpu.HBM(((N_DEV,) + a.shape) if g else a.shape, a.dtype)
                          for a, g in zip(arrs, self.gathers)]
        self.specs = [pl.BlockSpec(memory_space=pltpu.HBM)] * self.n
        self.scratch = [pltpu.SemaphoreType.DMA((self.n, N_DEV - 1)), pltpu.SemaphoreType.DMA((self.n, N_DEV - 1)),
                        pltpu.SemaphoreType.DMA((self.n,))]

    def _copies(self, ins, outs, sems):
        send_sems, recv_sems, local_sems = sems
        x, y, c = lax.axis_index("x"), lax.axis_index("y"), lax.axis_index("c")
        me = 4 * x + 2 * y + c
        local, sends, recvs = [], [], []
        for i in range(self.n):
            src_me = ins[i] if self.gathers[i] else ins[i].at[me]
            local.append(pltpu.make_async_copy(src_me, outs[i].at[me], local_sems.at[i]))
        for dd in range(1, N_DEV):
            px = jnp.bitwise_xor(x, dd >> 2)
            py = jnp.bitwise_xor(y, (dd >> 1) & 1)
            pc = jnp.bitwise_xor(c, dd & 1)
            pid = 4 * px + 2 * py + pc
            for i in range(self.n):
                src = ins[i] if self.gathers[i] else ins[i].at[pid]
                sems_i = dict(send_sem=send_sems.at[i, dd - 1], recv_sem=recv_sems.at[i, dd - 1],
                              device_id=(px, py, pc), device_id_type=MESH)
                sends.append(pltpu.make_async_remote_copy(src_ref=src, dst_ref=outs[i].at[me], **sems_i))
                recvs.append(pltpu.make_async_remote_copy(src_ref=src, dst_ref=outs[i].at[pid], **sems_i))
        return local, sends, recvs

    def start(self, ins, outs, sems):
        local, sends, _ = self._copies(ins, outs, sems)
        for cp in local + sends:
            cp.start()

    def wait(self, ins, outs, sems):
        local, sends, recvs = self._copies(ins, outs, sems)
        for cp in recvs:
            cp.wait_recv()
        for cp in sends:
            cp.wait_send()
        for cp in local:
            cp.wait()


class _NoExchange:
    n, arrs, specs, out_shape, scratch = 0, [], [], [], []

    def start(self, ins, outs, sems):
        pass

    def wait(self, ins, outs, sems):
        pass


def _exchange(name, arrs, gathers):
    ex = _Exchange(arrs, gathers)
    n = ex.n

    def body(*refs):
        ins, outs, sems = refs[:n], refs[n:2 * n], refs[2 * n:]
        ex.start(ins, outs, sems)
        ex.wait(ins, outs, sems)

    outs = pl.pallas_call(body, name=name, in_specs=ex.specs, out_specs=ex.specs, out_shape=ex.out_shape,
                          scratch_shapes=ex.scratch)(*ex.arrs)
    return list(outs)


def kernel(x, c, norm_mix, norm_ffn, w_mod, b_mod, w_qkv, w_o_attn, w_in_ssm, a_re, a_im, log_dt, b_re, b_im, c_re, c_im, d_skip, w_glu, b_glu, w_o_ssm, w_up, conv_w, conv_b, w_down, norm_out, w_fin, b_fin, loss_target, m_norm_mix, m_norm_ffn, m_w_mod, m_b_mod, m_w_qkv, m_w_o_attn, m_w_in_ssm, m_a_re, m_a_im, m_log_dt, m_b_re, m_b_im, m_c_re, m_c_im, m_d_skip, m_w_glu, m_b_glu, m_w_o_ssm, m_w_up, m_conv_w, m_conv_b, m_w_down, m_norm_out, m_w_fin, m_b_fin, v_norm_mix, v_norm_ffn, v_w_mod, v_b_mod, v_w_qkv, v_w_o_attn, v_w_in_ssm, v_a_re, v_a_im, v_log_dt, v_b_re, v_b_im, v_c_re, v_c_im, v_d_skip, v_w_glu, v_b_glu, v_w_o_ssm, v_w_up, v_conv_w, v_conv_b, v_w_down, v_norm_out, v_w_fin, v_b_fin):
    nb, s, d = x.shape
    t = nb * s
    n_seq = nb * N_DEV
    me = 4 * lax.axis_index("x") + 2 * lax.axis_index("y") + lax.axis_index("c")
    cm = w_mod.shape[2]
    cf = w_fin.shape[1]
    c_up = w_up.shape[2]
    r_dn = w_down.shape[1]
    g_ssm = d // SSM_H

    wq8, c8 = _exchange("gather_first", [w_qkv[0].astype(BF16), c], [True, True])
    later = _Exchange([w_o_attn[0].astype(BF16), w_in_ssm[0].astype(BF16), w_glu[0].astype(BF16),
                       w_o_ssm[0].astype(BF16), w_up[0].astype(BF16), w_up[1].astype(BF16),
                       w_down[0].astype(BF16), w_down[1].astype(BF16), conv_w, d_skip, b_glu], [True] * 11)
    half = N_DEV // 2
    cb_l = [conv_b[i].reshape(N_DEV, 1, c_up) for i in range(2)]
    c_all = c8.reshape(n_seq, d)

    c_act, modloc = _modfin_fwd(c_all, w_mod, w_fin)
    (mod8,) = _exchange("gather_mod", [modloc], [True])
    mine = lax.dynamic_slice_in_dim(mod8, me * nb, nb, axis=1)
    mods = []
    for i in range(2):
        mi = mine[:, :, i * cm:(i + 1) * cm].transpose(1, 0, 2).reshape(nb, N_DEV * cm) + b_mod[i]
        mods.append([mi[:, j * d:(j + 1) * d].reshape(nb, 1, d) for j in range(6)])
    fin = mine[:, :, 2 * cm:].transpose(1, 0, 2).reshape(nb, N_DEV * cf) + b_fin
    sh_f, sc_f = fin[:, :d].reshape(nb, 1, d), fin[:, d:].reshape(nb, 1, d)

    row = lambda a: a.reshape(1, -1)
    x0 = x.reshape(t, d)

    def ffn_fwd(i, xin, sh, sc, gate):
        h = _norm_mod_fwd(f"ffn{i}_norm", xin, row(norm_ffn[i]), sh, sc, nb)
        up = _mm(f"ffn{i}_up", h, wup8[i], (t // tm_, N_DEV, 1),
                 pl.BlockSpec((tm_, d), lambda a, b, k: (a, 0)), pl.BlockSpec((None, d, c_up), lambda a, b, k: (b, 0, 0)),
                 pl.BlockSpec((None, tm_, c_up), lambda a, b, k: (b, a, 0)),
                 jax.ShapeDtypeStruct((N_DEV, t, c_up), BF16), _NN, (tm_, c_up))
        act = _ffn_act_fwd(f"ffn{i}_act", up, cw_l[i], cb_l[i], nb, s)
        yf = _mm(f"ffn{i}_down", act, wd4[i], (t // tm_, 1, half),
                 pl.BlockSpec((None, tm_, c_up), lambda a, b, k: (k, a, 0)),
                 pl.BlockSpec((None, c_up, d), lambda a, b, k: (k, 0, 0)),
                 pl.BlockSpec((tm_, d), lambda a, b, k: (a, 0)), jax.ShapeDtypeStruct((t, d), F32), _NN, (tm_, d))
        xout = _gate_add(f"ffn{i}_res", xin, yf, gate, nb)
        return xout, (h, up, act, yf)

    tm_ = _tile(t, 2048)
    sh1, sc1, g1, sh2, sc2, g2 = mods[0]
    h1 = _norm_mod_fwd("attn_norm", x0, row(norm_mix[0]), sh1, sc1, nb)
    cq = wq8.shape[2]
    qkv = _mm("attn_qkv", h1, wq8, (t // tm_, N_DEV, 1),
              pl.BlockSpec((tm_, d), lambda a, b, k: (a, 0)), pl.BlockSpec((None, d, cq), lambda a, b, k: (b, 0, 0)),
              pl.BlockSpec((tm_, cq), lambda a, b, k: (a, b)), jax.ShapeDtypeStruct((t, 3 * d), BF16), _NN, (tm_, cq))
    o_att, car_att, (wo8, win8, wglu8, wos8, wup8_0, wup8_1, wd8_0, wd8_1, cw8, dskip8, bglu8) = _attn_fwd(
        qkv, nb, s, d, later)
    wo = wo8.reshape(d, d)
    win = win8.reshape(d, d)
    wglu = wglu8.reshape(d, d)
    wos = wos8.reshape(d, d)
    wup8 = [wup8_0, wup8_1]
    wd4 = [wd8_0.reshape(half, 2 * r_dn, d), wd8_1.reshape(half, 2 * r_dn, d)]
    cw_l = [cw8[:, 0], cw8[:, 1]]
    dskip_f = dskip8.reshape(1, d)
    bglu_f = bglu8.reshape(1, d)
    ya = _mm_nn("attn_out", o_att, wo, F32)
    x1 = _gate_add("attn_res", x0, ya, g1, nb)
    x2, ffn0_saved = ffn_fwd(0, x1, sh2, sc2, g2)

    sh1b, sc1b, g1b, sh2b, sc2b, g2b = mods[1]
    ops = _ssm_operators(a_re[0], a_im[0], log_dt[0], b_re[0], b_im[0], c_re[0], c_im[0])
    h3 = _norm_mod_fwd("ssm_norm", x2, row(norm_mix[1]), sh1b, sc1b, nb)
    u = _mm_nn("ssm_in", h3, win, BF16)
    ys_core, xp_re, xp_im = _ssm_core_fwd(u, ops, nb)
    y_ssm, z_ssm = _ssm_post_fwd(ys_core, u, dskip_f)
    gl = _mm_nn("ssm_glu", z_ssm, wglu, F32)
    gg = _glu_fwd(z_ssm, gl, bglu_f)
    ys2 = _mm_nn("ssm_out", gg, wos, F32)
    x3 = _gate_add("ssm_res", x2, ys2, g1b, nb)
    x4, ffn1_saved = ffn_fwd(1, x3, sh2b, sc2b, g2b)

    dx4, g_norm_out, dsh_f, dsc_f, loss_blk = _norm_mod_bwd(
        "final_norm", None, x4, row(norm_out), sh_f, sc_f, None, loss_target.reshape(t, d), nb)
    loss = lax.psum(loss_blk[0, 0], ("x", "y", "c"))

    def ffn_bwd(i, dxo, xin, sc, gate, saved):
        h, up, act, yf = saved
        dyf, dgate = _gate_bwd(f"ffn{i}_res_bwd", dxo, yf, gate, nb)
        dact = _mm(f"ffn{i}_down_dx", dyf, wd4[i], (t // tm_, half, 1),
                   pl.BlockSpec((tm_, d), lambda a, b, k: (a, 0)), pl.BlockSpec((None, c_up, d), lambda a, b, k: (b, 0, 0)),
                   pl.BlockSpec((None, tm_, c_up), lambda a, b, k: (b, a, 0)),
                   jax.ShapeDtypeStruct((half, t, c_up), BF16), _NT, (tm_, c_up))
        tk = _tile(t, 1024)
        gwd = _mm(f"ffn{i}_down_dw", act, dyf, (half, 1, t // tk),
                  pl.BlockSpec((None, tk, c_up), lambda a, b, k: (a, k, 0)), pl.BlockSpec((tk, d), lambda a, b, k: (k, 0)),
                  pl.BlockSpec((None, c_up, d), lambda a, b, k: (a, 0, 0)),
                  jax.ShapeDtypeStruct((half, c_up, d), BF16), _TN, (c_up, d))
        dup, dcw, dcb = _ffn_act_bwd(f"ffn{i}_act_bwd", up, dact, cw_l[i], cb_l[i], nb, s)
        dh = _mm(f"ffn{i}_up_dx", dup, wup8[i], (t // tm_, 1, N_DEV),
                 pl.BlockSpec((None, tm_, c_up), lambda a, b, k: (k, a, 0)),
                 pl.BlockSpec((None, d, c_up), lambda a, b, k: (k, 0, 0)),
                 pl.BlockSpec((tm_, d), lambda a, b, k: (a, 0)), jax.ShapeDtypeStruct((t, d), F32), _NT, (tm_, d))
        gwup = _mm(f"ffn{i}_up_dw", h, dup, (1, N_DEV, t // tk),
                   pl.BlockSpec((tk, d), lambda a, b, k: (k, 0)), pl.BlockSpec((None, tk, c_up), lambda a, b, k: (b, k, 0)),
                   pl.BlockSpec((None, d, c_up), lambda a, b, k: (b, 0, 0)),
                   jax.ShapeDtypeStruct((N_DEV, d, c_up), BF16), _TN, (d, c_up))
        dxi, g_norm, dsh, dsc = _norm_mod_bwd(f"ffn{i}_norm_bwd", dh, xin, row(norm_ffn[i]), None, sc, dxo, None, nb)
        return dxi, (gwup, gwd.reshape(N_DEV, r_dn, d), dcw, dcb, g_norm, dsh, dsc, dgate)

    dx3, (gwup1, gwd1, dcw1, dcb1, g_nffn1, dsh2b, dsc2b, dg2b) = ffn_bwd(1, dx4, x3, sc2b, g2b, ffn1_saved)

    dys2, dg1b = _gate_bwd("ssm_res_bwd", dx3, ys2, g1b, nb)
    dgg = _mm_nt("ssm_out_dx", dys2, wos, F32)
    gwos = _mm_tn("ssm_out_dw", gg, dys2, BF16)
    dgl, dz1, g_bglu = _glu_bwd(dgg, z_ssm, gl, bglu_f)
    dz2 = _mm_nt("ssm_glu_dx", dgl, wglu, F32)
    gwglu = _mm_tn("ssm_glu_dw", z_ssm, dgl, BF16)
    dy_ssm, du_skip, g_dskip = _ssm_post_bwd(dz1, dz2, y_ssm, u, dskip_f)
    du_core, d_ops = _ssm_core_bwd(dy_ssm, u, xp_re, xp_im, ops, nb)
    du = _add_cast(du_core, du_skip)
    dh3 = _mm_nt("ssm_in_dx", du, win, F32)
    gwin = _mm_tn("ssm_in_dw", h3, du, BF16)
    dx2, g_nmix1, dsh1b, dsc1b = _norm_mod_bwd("ssm_norm_bwd", dh3, x2, row(norm_mix[1]), None, sc1b, dx3, None, nb)
    _, ops_vjp = jax.vjp(_ssm_operators, a_re[0], a_im[0], log_dt[0], b_re[0], b_im[0], c_re[0], c_im[0])
    g_ssm_params = ops_vjp(d_ops)

    dx1, (gwup0, gwd0, dcw0, dcb0, g_nffn0, dsh2, dsc2, dg2) = ffn_bwd(0, dx2, x1, sc2, g2, ffn0_saved)

    dya, dg1 = _gate_bwd("attn_res_bwd", dx1, ya, g1, nb)
    do_att = _mm_nt("attn_out_dx", dya, wo, BF16)
    gwo = _mm_tn("attn_out_dw", o_att, dya, BF16)
    rows8 = lambda a: a.reshape(N_DEV, d // N_DEV, d)
    def two_d(w):
        shp = w.shape
        if len(shp) == 1:
            return (1, shp[0])
        if len(shp) == 2:
            return shp
        return (shp[0] * shp[1], math.prod(shp[2:]))

    ssm_w = [a_re, a_im, log_dt, b_re, b_im, c_re, c_im]
    ssm_partial = [g.reshape(two_d(w)) for g, w in zip(g_ssm_params, ssm_w)]
    early = _Exchange([rows8(gwo), rows8(gwin), rows8(gwglu), rows8(gwos), gwup0, gwup1, gwd0, gwd1] + ssm_partial,
                      [False] * 8 + [True] * 7)
    dq, dk, dv, early_res = _attn_bwd(qkv, car_att, do_att, nb, s, d, early)
    ro, rin, rglu, ros, rup0, rup1, rd0, rd1 = early_res[:8]
    ssm8 = early_res[8:]
    dqkv = jnp.concatenate([dq, dk, dv], axis=1)
    tk = _tile(t, 1024)
    gwq8 = _mm("attn_qkv_dw", h1, dqkv, (1, N_DEV, t // tk),
               pl.BlockSpec((tk, d), lambda a, b, k: (k, 0)),
               pl.BlockSpec((tk, cq), lambda a, b, k: (k, b)),
               pl.BlockSpec((None, d, cq), lambda a, b, k: (b, 0, 0)),
               jax.ShapeDtypeStruct((N_DEV, d, cq), BF16), _TN, (d, cq))
    dh1, (rq,) = _mm("attn_qkv_dx", dqkv, wq8, (t // tm_, 1, N_DEV),
                     pl.BlockSpec((tm_, cq), lambda a, b, k: (a, k)),
                     pl.BlockSpec((None, d, cq), lambda a, b, k: (k, 0, 0)),
                     pl.BlockSpec((tm_, d), lambda a, b, k: (a, 0)), jax.ShapeDtypeStruct((t, d), F32), _NT, (tm_, d),
                     ex=_Exchange([gwq8], [False]))
    dx0, g_nmix0, dsh1, dsc1 = _norm_mod_bwd("attn_norm_bwd", dh1, x0, row(norm_mix[0]), None, sc1, dx1, None, nb)
    grad_x = dx0.reshape(nb, s, d)

    dmod = [jnp.concatenate([a.reshape(nb, d) for a in grp], axis=1) for grp in
            ([dsh1, dsc1, dg1, dsh2, dsc2, dg2], [dsh1b, dsc1b, dg1b, dsh2b, dsc2b, dg2b])]
    dfin = jnp.concatenate([dsh_f.reshape(nb, d), dsc_f.reshape(nb, d)], axis=1)
    dmodfin = jnp.concatenate(dmod + [dfin], axis=1)
    flat = lambda a: a.reshape(1, -1)
    dmf8, nmix8, nffn8, nout8, cb8, dskip_g8, bglu_g8, cw_g8 = _exchange(
        "exchange_last",
        [dmodfin, jnp.concatenate([g_nmix0, g_nmix1]), jnp.concatenate([g_nffn0, g_nffn1]), g_norm_out,
         jnp.concatenate([flat(dcb0), flat(dcb1)]), g_dskip, g_bglu, jnp.stack([dcw0, dcw1])], [True] * 8)
    shard = d // N_DEV
    dskip_g8 = lax.dynamic_slice_in_dim(dskip_g8, me * shard, shard, axis=2)
    bglu_g8 = lax.dynamic_slice_in_dim(bglu_g8, me * shard, shard, axis=2)
    cw_g8 = lax.dynamic_slice_in_dim(cw_g8, me, 1, axis=2).reshape(N_DEV, 2 * 3, c_up)

    dall = dmf8.reshape(n_seq, 14 * d)
    dmod_loc = jnp.stack([lax.dynamic_slice_in_dim(dall[:, i * 6 * d:(i + 1) * 6 * d], me * cm, cm, axis=1)
                          for i in range(2)])
    dfin_loc = lax.dynamic_slice_in_dim(dall[:, 12 * d:], me * cf, cf, axis=1)
    g_w_mod, g_w_fin, g_bias = _modfin_bwd(c_act.T, dmod_loc, dfin_loc, dall)
    g_b_mod = g_bias[0, :12 * d].reshape(2, 6 * d)
    g_b_fin = g_bias[0, 12 * d:]

    def big(name, parts, w, m, v):
        shp = w.shape
        r2 = lambda a: a.reshape(-1, shp[-1])
        res = _adamw(name, parts.reshape(parts.shape[0], -1, shp[-1]), r2(w), r2(m), r2(v))
        return [a.reshape(shp) for a in res]

    upd = {}
    upd["w_mod"] = big("adamw_w_mod", g_w_mod[None], w_mod, m_w_mod, v_w_mod)
    upd["w_fin"] = big("adamw_w_fin", g_w_fin[None], w_fin, m_w_fin, v_w_fin)
    upd["w_qkv"] = big("adamw_w_qkv", rq, w_qkv, m_w_qkv, v_w_qkv)
    upd["w_o_attn"] = big("adamw_w_o_attn", ro, w_o_attn, m_w_o_attn, v_w_o_attn)
    upd["w_in_ssm"] = big("adamw_w_in_ssm", rin, w_in_ssm, m_w_in_ssm, v_w_in_ssm)
    upd["w_glu"] = big("adamw_w_glu", rglu, w_glu, m_w_glu, v_w_glu)
    upd["w_o_ssm"] = big("adamw_w_o_ssm", ros, w_o_ssm, m_w_o_ssm, v_w_o_ssm)
    up_l = [big(f"adamw_w_up{i}", r, w_up[i], m_w_up[i], v_w_up[i]) for i, r in enumerate((rup0, rup1))]
    upd["w_up"] = [jnp.stack([up_l[0][j], up_l[1][j]]) for j in range(4)]
    dn_l = [big(f"adamw_w_down{i}", r, w_down[i], m_w_down[i], v_w_down[i]) for i, r in enumerate((rd0, rd1))]
    upd["w_down"] = [jnp.stack([dn_l[0][j], dn_l[1][j]]) for j in range(4)]

    small_names = ["norm_mix", "norm_ffn", "b_mod", "a_re", "a_im", "log_dt", "b_re", "b_im", "c_re", "c_im",
                   "d_skip", "b_glu", "conv_w", "conv_b", "norm_out", "b_fin"]
    small_g = [nmix8, nffn8, g_b_mod[None], *ssm8, dskip_g8, bglu_g8, cw_g8, cb8, nout8, g_b_fin[None]]
    small_w = [norm_mix, norm_ffn, b_mod, a_re, a_im, log_dt, b_re, b_im, c_re, c_im, d_skip, b_glu, conv_w, conv_b,
               norm_out, b_fin]
    small_m = [m_norm_mix, m_norm_ffn, m_b_mod, m_a_re, m_a_im, m_log_dt, m_b_re, m_b_im, m_c_re, m_c_im, m_d_skip,
               m_b_glu, m_conv_w, m_conv_b, m_norm_out, m_b_fin]
    small_v = [v_norm_mix, v_norm_ffn, v_b_mod, v_a_re, v_a_im, v_log_dt, v_b_re, v_b_im, v_c_re, v_c_im, v_d_skip,
               v_b_glu, v_conv_w, v_conv_b, v_norm_out, v_b_fin]
    entries = [(gp.reshape((gp.shape[0],) + two_d(w)), w.reshape(two_d(w)), m.reshape(two_d(w)), v.reshape(two_d(w)))
               for gp, w, m, v in zip(small_g, small_w, small_m, small_v)]
    res = _adamw_many("adamw_small", entries)
    for j, (nm, w) in enumerate(zip(small_names, small_w)):
        upd[nm] = [res[4 * j + k].reshape(w.shape) for k in range(4)]

    order = ["norm_mix", "norm_ffn", "w_mod", "b_mod", "w_qkv", "w_o_attn", "w_in_ssm", "a_re", "a_im", "log_dt",
             "b_re", "b_im", "c_re", "c_im", "d_skip", "w_glu", "b_glu", "w_o_ssm", "w_up", "conv_w", "conv_b",
             "w_down", "norm_out", "w_fin", "b_fin"]
    outs = [loss, grad_x]
    for k in range(4):
        outs += [upd[nm][k] for nm in order]
    return tuple(outs)
```

```python
import functools
import math

import jax
import jax.numpy as jnp
from jax import lax
from jax.experimental import pallas as pl
from jax.experimental.pallas import tpu as pltpu

F32 = jnp.float32
BF16 = jnp.bfloat16
MESH = pl.DeviceIdType.MESH

N_DEV = 8
HEAD_DIM = 64
ATT_BLK = 128
ATT_BQ = 256
ATT_UNROLL = 2
SSM_H = 16
SSM_P = 64
SSM_L = 4
EPS = 1e-6
ADAM_LR, ADAM_B1, ADAM_B2, ADAM_EPS, ADAM_WD, ADAM_STEP = 0.001, 0.9, 0.999, 1e-08, 0.01, 10
V7X_VMEM_LIMIT = 56 * 1024 * 1024
LANE = 128

_NN = (((1,), (0,)), ((), ()))
_NT = (((1,), (1,)), ((), ()))
_TN = (((0,), (0,)), ((), ()))


def _cp(sem):
    return pltpu.CompilerParams(dimension_semantics=sem, vmem_limit_bytes=V7X_VMEM_LIMIT)


def _tile(n, pref):
    if n <= pref:
        return n
    t = pref - pref % 16
    while t >= 16:
        if n % t == 0:
            return t
        t -= 16
    return n


def _mm(name, a, b, grid, a_spec, b_spec, out_spec, out_shape, dims, acc_shape, ex=None):
    nk = grid[-1]
    kax = len(grid) - 1
    ex = ex or _NoExchange()

    def body(*refs):
        a_ref, b_ref = refs[:2]
        ex_ins = refs[2:2 + ex.n]
        o_ref = refs[2 + ex.n]
        ex_outs = refs[3 + ex.n:3 + 2 * ex.n]
        acc_ref = refs[3 + 2 * ex.n]
        sems = refs[4 + 2 * ex.n:]
        first, last = _grid_ends(grid)
        k = pl.program_id(kax)

        @pl.when(first)
        def _():
            ex.start(ex_ins, ex_outs, sems)

        @pl.when(k == 0)
        def _():
            acc_ref[...] = jnp.zeros(acc_shape, F32)

        acc_ref[...] += lax.dot_general(a_ref[...].astype(BF16), b_ref[...].astype(BF16), dims,
                                        preferred_element_type=F32)

        @pl.when(k == nk - 1)
        def _():
            o_ref[...] = acc_ref[...].astype(o_ref.dtype)

        @pl.when(last)
        def _():
            ex.wait(ex_ins, ex_outs, sems)

    sem = ("arbitrary",) * len(grid) if ex.n else ("parallel",) * kax + ("arbitrary",)
    res = pl.pallas_call(
        body, name=name, grid=grid, in_specs=[a_spec, b_spec] + ex.specs, out_specs=[out_spec] + ex.specs,
        out_shape=[out_shape] + ex.out_shape, scratch_shapes=[pltpu.VMEM(acc_shape, F32)] + ex.scratch,
        compiler_params=_cp(sem))(a, b, *ex.arrs)
    return (res[0], list(res[1:])) if ex.n else res[0]


def _mm_nn(name, a, w, out_dtype):
    m, k = a.shape
    n = w.shape[1]
    tm, tn, tk = _tile(m, 512), _tile(n, 1024), _tile(k, 1024)
    return _mm(name, a, w, (m // tm, n // tn, k // tk),
               pl.BlockSpec((tm, tk), lambda i, j, kk: (i, kk)), pl.BlockSpec((tk, tn), lambda i, j, kk: (kk, j)),
               pl.BlockSpec((tm, tn), lambda i, j, kk: (i, j)), jax.ShapeDtypeStruct((m, n), out_dtype), _NN, (tm, tn))


def _mm_nt(name, a, w, out_dtype):
    m, n = a.shape
    k = w.shape[0]
    tm, tko, tn = _tile(m, 512), _tile(k, 1024), _tile(n, 1024)
    return _mm(name, a, w, (m // tm, k // tko, n // tn),
               pl.BlockSpec((tm, tn), lambda i, j, kk: (i, kk)), pl.BlockSpec((tko, tn), lambda i, j, kk: (j, kk)),
               pl.BlockSpec((tm, tko), lambda i, j, kk: (i, j)), jax.ShapeDtypeStruct((m, k), out_dtype), _NT, (tm, tko))


def _mm_tn(name, a, b, out_dtype):
    t, m = a.shape
    n = b.shape[1]
    tm, tn, tk = _tile(m, 512), _tile(n, 1024), _tile(t, 1024)
    return _mm(name, a, b, (m // tm, n // tn, t // tk),
               pl.BlockSpec((tk, tm), lambda i, j, kk: (kk, i)), pl.BlockSpec((tk, tn), lambda i, j, kk: (kk, j)),
               pl.BlockSpec((tm, tn), lambda i, j, kk: (i, j)), jax.ShapeDtypeStruct((m, n), out_dtype), _TN, (tm, tn))


def _norm_mod_fwd(name, x, g, shift, scale, nb):
    t, d = x.shape
    s = t // nb
    tr = _tile(s, 512)
    nt = s // tr

    def body(x_ref, g_ref, sh_ref, sc_ref, h_ref):
        xv = x_ref[...]
        r = lax.rsqrt(jnp.mean(xv * xv, axis=-1, keepdims=True) + EPS)
        y = xv * r * g_ref[...]
        h_ref[...] = (y * (1.0 + sc_ref[...]) + sh_ref[...]).astype(h_ref.dtype)

    row = pl.BlockSpec((tr, d), lambda b, i: (b * nt + i, 0))
    vec = pl.BlockSpec((None, 1, d), lambda b, i: (b, 0, 0))
    return pl.pallas_call(body, name=name, grid=(nb, nt),
                          in_specs=[row, pl.BlockSpec((1, d), lambda b, i: (0, 0)), vec, vec],
                          out_specs=row, out_shape=jax.ShapeDtypeStruct((t, d), BF16),
                          compiler_params=_cp(("parallel", "parallel")))(x, g, shift, scale)


def _norm_mod_bwd(name, dh, x, g, shift, scale, dres, target, nb):
    t, d = x.shape
    s = t // nb
    tr = _tile(s, 256)
    nt = s // tr
    final = target is not None

    def body(*refs):
        if final:
            x_ref, g_ref, sh_ref, sc_ref, tg_ref, dx_ref, dg_ref, dsh_ref, dsc_ref, loss_ref = refs
        else:
            dh_ref, x_ref, g_ref, sc_ref, dres_ref, dx_ref, dg_ref, dsh_ref, dsc_ref = refs
        b, i = pl.program_id(0), pl.program_id(1)
        xv = x_ref[...]
        gv = g_ref[...]
        r = lax.rsqrt(jnp.mean(xv * xv, axis=-1, keepdims=True) + EPS)
        nrm = xv * r
        y = nrm * gv
        one_sc = 1.0 + sc_ref[...]
        if final:
            err = y * one_sc + sh_ref[...] - tg_ref[...]
            dhv = err * (1.0 / d)
        else:
            dhv = dh_ref[...].astype(F32)
        dy = dhv * one_sc
        dn = dy * gv
        dxv = r * (dn - nrm * jnp.mean(dn * nrm, axis=-1, keepdims=True))
        if final:
            dx_ref[...] = dxv
        else:
            dx_ref[...] = dres_ref[...] + dxv

        @pl.when(i == 0)
        def _():
            dsh_ref[...] = jnp.zeros_like(dsh_ref)
            dsc_ref[...] = jnp.zeros_like(dsc_ref)

        @pl.when((i == 0) & (b == 0))
        def _():
            dg_ref[...] = jnp.zeros_like(dg_ref)
            if final:
                loss_ref[...] = jnp.zeros_like(loss_ref)

        dsh_ref[...] += jnp.sum(dhv, axis=0, keepdims=True)
        dsc_ref[...] += jnp.sum(dhv * y, axis=0, keepdims=True)
        dg_ref[...] += jnp.sum(dy * nrm, axis=0, keepdims=True)
        if final:
            loss_ref[...] += (0.5 / d) * jnp.sum(err * err)

    row = pl.BlockSpec((tr, d), lambda b, i: (b * nt + i, 0))
    vec = pl.BlockSpec((None, 1, d), lambda b, i: (b, 0, 0))
    gsp = pl.BlockSpec((1, d), lambda b, i: (0, 0))
    out_specs = [row, gsp, vec, vec]
    out_shape = [jax.ShapeDtypeStruct((t, d), F32), jax.ShapeDtypeStruct((1, d), F32),
                 jax.ShapeDtypeStruct((nb, 1, d), F32), jax.ShapeDtypeStruct((nb, 1, d), F32)]
    if final:
        ins, in_specs = [x, g, shift, scale, target], [row, gsp, vec, vec, row]
        out_specs.append(pl.BlockSpec((8, LANE), lambda b, i: (0, 0)))
        out_shape.append(jax.ShapeDtypeStruct((8, LANE), F32))
    else:
        ins, in_specs = [dh, x, g, scale, dres], [row, row, gsp, vec, row]
    return pl.pallas_call(body, name=name, grid=(nb, nt), in_specs=in_specs, out_specs=out_specs,
                          out_shape=out_shape, compiler_params=_cp(("arbitrary", "arbitrary")))(*ins)


def _gate_add(name, x, y, gate, nb):
    t, d = x.shape
    s = t // nb
    tr = _tile(s, 512)
    nt = s // tr

    def body(x_ref, y_ref, g_ref, o_ref):
        o_ref[...] = x_ref[...] + g_ref[...] * y_ref[...]

    row = pl.BlockSpec((tr, d), lambda b, i: (b * nt + i, 0))
    vec = pl.BlockSpec((None, 1, d), lambda b, i: (b, 0, 0))
    return pl.pallas_call(body, name=name, grid=(nb, nt), in_specs=[row, row, vec], out_specs=row,
                          out_shape=jax.ShapeDtypeStruct((t, d), F32),
                          compiler_params=_cp(("parallel", "parallel")))(x, y, gate)


def _gate_bwd(name, dx, y, gate, nb):
    t, d = dx.shape
    s = t // nb
    tr = _tile(s, 512)
    nt = s // tr

    def body(dx_ref, y_ref, g_ref, dy_ref, dg_ref):
        dxv = dx_ref[...]
        dy_ref[...] = (g_ref[...] * dxv).astype(dy_ref.dtype)

        @pl.when(pl.program_id(1) == 0)
        def _():
            dg_ref[...] = jnp.zeros_like(dg_ref)

        dg_ref[...] += jnp.sum(dxv * y_ref[...], axis=0, keepdims=True)

    row = pl.BlockSpec((tr, d), lambda b, i: (b * nt + i, 0))
    vec = pl.BlockSpec((None, 1, d), lambda b, i: (b, 0, 0))
    return pl.pallas_call(body, name=name, grid=(nb, nt), in_specs=[row, row, vec], out_specs=[row, vec],
                          out_shape=[jax.ShapeDtypeStruct((t, d), BF16), jax.ShapeDtypeStruct((nb, 1, d), F32)],
                          compiler_params=_cp(("parallel", "arbitrary")))(dx, y, gate)


def _log_sigmoid(z):
    return jnp.minimum(z, 0.0) - jnp.log(1.0 + jnp.exp(-jnp.abs(z)))


def _split_dot(v, tri):
    hi = v.astype(BF16)
    lo = (v - hi.astype(F32)).astype(BF16)
    return (jnp.dot(hi, tri, preferred_element_type=F32) + jnp.dot(lo, tri, preferred_element_type=F32))


def _grid_ends(grid):
    ids = [pl.program_id(a) for a in range(len(grid))]
    first = functools.reduce(lambda u, w: u & w, [i == 0 for i in ids])
    last = functools.reduce(lambda u, w: u & w, [i == n - 1 for i, n in zip(ids, grid)])
    return first, last


def _attn_trips(nq):
    return nq * (nq + 1) // 2


def _next_trip(qi, jj, nq):
    wrap = jj >= qi
    nqi = jnp.where(wrap, jnp.minimum(qi + 1, nq - 1), qi)
    njj = jnp.where(wrap, jnp.where(qi + 1 < nq, 0, jj), jj + 1)
    return nqi, njj


def _attn_fwd(qkv, nb, s, d, ex):
    t = nb * s
    npair = d // LANE
    bk = ATT_BLK
    bq = min(ATT_BQ, s)
    nq = s // bq
    kpq = bq // bk
    nheads = LANE // HEAD_DIM
    scale = HEAD_DIM ** -0.5
    grid = (nb, npair)
    assert s // bk <= HEAD_DIM, "one carry lane per key block and head"
    assert bk == LANE, "the running sums are kept one 128-lane tile wide"
    assert kpq == ATT_UNROLL, "query block qi has exactly qi + 1 trips"

    def body(*refs):
        q_ref, k_ref, v_ref = refs[:3]
        ex_ins = refs[3:3 + ex.n]
        o_ref, car_ref = refs[3 + ex.n:5 + ex.n]
        ex_outs = refs[5 + ex.n:5 + 2 * ex.n]
        acc_s, run_s, z_s, arg_s = refs[5 + 2 * ex.n:9 + 2 * ex.n]
        sems = refs[9 + 2 * ex.n:]
        first, last = _grid_ends(grid)

        @pl.when(first)
        def _():
            ex.start(ex_ins, ex_outs, sems)

        lane = lax.broadcasted_iota(jnp.int32, (1, LANE), 1)
        row = lax.broadcasted_iota(jnp.int32, (bq, bk), 0)
        col = lax.broadcasted_iota(jnp.int32, (bq, bk), 1)
        trow = lax.broadcasted_iota(jnp.int32, (bk, bk), 0)
        tcol = lax.broadcasted_iota(jnp.int32, (bk, bk), 1)
        tri = (trow > tcol).astype(BF16)
        hms = [(lane // HEAD_DIM) == hh for hh in range(nheads)]

        def q0_of(qi):
            return pl.multiple_of(qi * bq, bq)

        def kblk_of(qi, jj, u):
            return (qi + 1) * kpq - 1 - (ATT_UNROLL * jj + u)

        def scores(qi, jj):
            q = q_ref[pl.ds(q0_of(qi), bq), :]
            qhs = [jnp.where(hm, q, jnp.zeros_like(q)) * scale for hm in hms]
            ks = [k_ref[pl.ds(pl.multiple_of(kblk_of(qi, jj, u) * bk, bk), bk), :] for u in range(ATT_UNROLL)]
            return [[lax.dot_general(qhs[hh], kj, _NT, preferred_element_type=F32) for kj in ks]
                    for hh in range(nheads)]

        def keep(zn):
            for hh in range(nheads):
                for u in range(ATT_UNROLL):
                    z_s[hh, u] = zn[hh][u]

        def exponents(qi, jj):
            q0 = q0_of(qi)
            car = car_ref[pl.ds(q0, bq), :]
            for hh in range(nheads):
                run = jnp.where(jj == 0, 0.0, run_s[hh])
                for u in range(ATT_UNROLL):
                    j = kblk_of(qi, jj, u)
                    mask = (j * bk + col) < (q0 + row)
                    z = z_s[hh, u]
                    lb = _log_sigmoid(z)
                    l1 = jnp.where(mask, lb - z, 0.0)
                    arg_s[hh, u] = jnp.where(mask, lb + (_split_dot(l1, tri) + run), -1e30)
                    car = jnp.where(lane == hh * HEAD_DIM + j, run, car)
                    run = run + jnp.sum(l1, axis=1, keepdims=True)
                run_s[hh] = run
            car_ref[pl.ds(q0, bq), :] = car

        def weigh(qi, jj):
            q0 = q0_of(qi)
            for hh in range(nheads):
                acc = None
                for u in range(ATT_UNROLL):
                    vj = v_ref[pl.ds(pl.multiple_of(kblk_of(qi, jj, u) * bk, bk), bk), :]
                    pv = jnp.dot(jnp.exp(arg_s[hh, u]).astype(BF16), vj, preferred_element_type=F32)
                    acc = pv if acc is None else acc + pv
                acc_s[hh, pl.ds(q0, bq), :] += acc

        def step(n, carry):
            qi, jj, pqi, pjj = carry
            nqi, njj = _next_trip(qi, jj, nq)
            zn = scores(nqi, njj)
            weigh(pqi, pjj)
            exponents(qi, jj)
            keep(zn)
            return nqi, njj, qi, jj

        acc_s[...] = jnp.zeros_like(acc_s)
        run_s[...] = jnp.zeros_like(run_s)
        car_ref[...] = jnp.zeros_like(car_ref)
        arg_s[...] = jnp.full(arg_s.shape, -1e30, F32)
        zero = jnp.int32(0)
        keep(scores(zero, zero))
        _, _, lqi, ljj = lax.fori_loop(0, _attn_trips(nq), step, (zero, zero, zero, zero))
        weigh(lqi, ljj)
        out = acc_s[0]
        for hh in range(1, nheads):
            out = jnp.where(hms[hh], acc_s[hh], out)
        o_ref[...] = out.astype(o_ref.dtype)

        @pl.when(last)
        def _():
            ex.wait(ex_ins, ex_outs, sems)

    seq = lambda off: pl.BlockSpec((s, LANE), lambda b, p: (b, off + p))
    res = pl.pallas_call(
        body, name="attn_fwd", grid=grid,
        in_specs=[seq(0), seq(npair), seq(2 * npair)] + ex.specs,
        out_specs=[seq(0), seq(0)] + ex.specs,
        out_shape=[jax.ShapeDtypeStruct((t, d), BF16), jax.ShapeDtypeStruct((t, d), F32)] + ex.out_shape,
        scratch_shapes=[pltpu.VMEM((nheads, s, LANE), F32), pltpu.VMEM((nheads, bq, LANE), F32),
                        pltpu.VMEM((nheads, ATT_UNROLL, bq, bk), F32),
                        pltpu.VMEM((nheads, ATT_UNROLL, bq, bk), F32)] + ex.scratch,
        compiler_params=_cp(("arbitrary", "arbitrary")))(qkv, qkv, qkv, *ex.arrs)
    return res[0], res[1], list(res[2:])


def _attn_bwd(qkv, car, do, nb, s, d, ex):
    t = nb * s
    npair = d // LANE
    bk = ATT_BLK
    bq = min(ATT_BQ, s)
    nq = s // bq
    kpq = bq // bk
    nheads = LANE // HEAD_DIM
    scale = HEAD_DIM ** -0.5
    grid = (nb, npair)
    assert kpq == ATT_UNROLL, "query block qi has exactly qi + 1 trips"

    def body(*refs):
        q_ref, k_ref, v_ref, car_ref, do_ref = refs[:5]
        ex_ins = refs[5:5 + ex.n]
        dq_ref, dk_ref, dv_ref = refs[5 + ex.n:8 + ex.n]
        ex_outs = refs[8 + ex.n:8 + 2 * ex.n]
        dk_acc, dv_acc, dq_s, rune_s, z_s, da_s, dz_s, a_s = refs[8 + 2 * ex.n:16 + 2 * ex.n]
        sems = refs[16 + 2 * ex.n:]
        first, last = _grid_ends(grid)

        @pl.when(first)
        def _():
            ex.start(ex_ins, ex_outs, sems)

        lane = lax.broadcasted_iota(jnp.int32, (1, LANE), 1)
        row = lax.broadcasted_iota(jnp.int32, (bq, bk), 0)
        col = lax.broadcasted_iota(jnp.int32, (bq, bk), 1)
        trow = lax.broadcasted_iota(jnp.int32, (bk, bk), 0)
        tcol = lax.broadcasted_iota(jnp.int32, (bk, bk), 1)
        tri_suf = (trow > tcol).astype(BF16)
        tri_pre = (trow < tcol).astype(BF16)
        hms = [(lane // HEAD_DIM) == hh for hh in range(nheads)]

        def q0_of(qi):
            return pl.multiple_of(qi * bq, bq)

        def k0_of(jj, u):
            return pl.multiple_of((ATT_UNROLL * jj + u) * bk, bk)

        def heads_of(ref, qi, factor):
            x = ref[pl.ds(q0_of(qi), bq), :]
            return [jnp.where(hm, x, jnp.zeros_like(x)) * factor for hm in hms]

        def products(qi, jj):
            qhs, dohs = heads_of(q_ref, qi, scale), heads_of(do_ref, qi, 1.0)
            ks = [k_ref[pl.ds(k0_of(jj, u), bk), :] for u in range(ATT_UNROLL)]
            vs = [v_ref[pl.ds(k0_of(jj, u), bk), :] for u in range(ATT_UNROLL)]
            zn = [[lax.dot_general(qhs[hh], kj, _NT, preferred_element_type=F32) for kj in ks] for hh in range(nheads)]
            dn = [[lax.dot_general(dohs[hh], vj, _NT, preferred_element_type=F32) for vj in vs] for hh in range(nheads)]
            return zn, dn

        def keep(zn, dn):
            for hh in range(nheads):
                for u in range(ATT_UNROLL):
                    z_s[hh, u] = zn[hh][u]
                    da_s[hh, u] = dn[hh][u]

        def middle(qi, jj):
            q0 = q0_of(qi)
            car = car_ref[pl.ds(q0, bq), :]
            for hh in range(nheads):
                run_e = jnp.where(jj == 0, 0.0, rune_s[hh])
                for u in range(ATT_UNROLL):
                    j = ATT_UNROLL * jj + u
                    mask = (j * bk + col) < (q0 + row)
                    z = z_s[hh, u]
                    lb = _log_sigmoid(z)
                    l1u = lb - z
                    l1 = jnp.where(mask, l1u, 0.0)
                    run = jnp.sum(jnp.where(lane == hh * HEAD_DIM + j, car, 0.0), axis=1, keepdims=True)
                    a = jnp.where(mask, jnp.exp(lb + (_split_dot(l1, tri_suf) + run)), 0.0)
                    e = da_s[hh, u] * a
                    dz = e * jnp.exp(l1u) - (_split_dot(e, tri_pre) + run_e) * jnp.exp(lb)
                    dz_s[hh, u] = jnp.where(mask, dz, 0.0).astype(BF16)
                    a_s[hh, u] = a.astype(BF16)
                    run_e = run_e + jnp.sum(e, axis=1, keepdims=True)
                rune_s[hh] = run_e

        def grads(qi, jj):
            q0 = q0_of(qi)
            qhs, dohs = heads_of(q_ref, qi, scale), heads_of(do_ref, qi, 1.0)
            dqs = [None] * nheads
            for u in range(ATT_UNROLL):
                k0 = k0_of(jj, u)
                kj = k_ref[pl.ds(k0, bk), :]
                for hh in range(nheads):
                    dzb = dz_s[hh, u]
                    dqu = jnp.dot(dzb, kj, preferred_element_type=F32)
                    dqs[hh] = dqu if dqs[hh] is None else dqs[hh] + dqu
                    dkh = lax.dot_general(dzb, qhs[hh], _TN, preferred_element_type=F32)
                    dvh = lax.dot_general(a_s[hh, u], dohs[hh], _TN, preferred_element_type=F32)
                    dk_blk = dkh if hh == 0 else dk_blk + dkh
                    dv_blk = dvh if hh == 0 else dv_blk + dvh
                dk_acc[pl.ds(k0, bk), :] += dk_blk
                dv_acc[pl.ds(k0, bk), :] += dv_blk
            for hh in range(nheads):
                dq_s[hh, pl.ds(q0, bq), :] += dqs[hh]

        def step(n, carry):
            qi, jj, pqi, pjj = carry
            nqi, njj = _next_trip(qi, jj, nq)
            zn, dn = products(nqi, njj)
            grads(pqi, pjj)
            middle(qi, jj)
            keep(zn, dn)
            return nqi, njj, qi, jj

        dk_acc[...] = jnp.zeros_like(dk_acc)
        dv_acc[...] = jnp.zeros_like(dv_acc)
        dq_s[...] = jnp.zeros_like(dq_s)
        rune_s[...] = jnp.zeros_like(rune_s)
        dz_s[...] = jnp.zeros_like(dz_s)
        a_s[...] = jnp.zeros_like(a_s)
        zero = jnp.int32(0)
        keep(*products(zero, zero))
        _, _, lqi, ljj = lax.fori_loop(0, _attn_trips(nq), step, (zero, zero, zero, zero))
        grads(lqi, ljj)
        dq_out = dq_s[0]
        for hh in range(1, nheads):
            dq_out = jnp.where(hms[hh], dq_s[hh], dq_out)
        dq_ref[...] = (dq_out * scale).astype(dq_ref.dtype)
        dk_ref[...] = dk_acc[...].astype(dk_ref.dtype)
        dv_ref[...] = dv_acc[...].astype(dv_ref.dtype)

        @pl.when(last)
        def _():
            ex.wait(ex_ins, ex_outs, sems)

    seq = lambda off: pl.BlockSpec((s, LANE), lambda b, p: (b, off + p))
    sds = jax.ShapeDtypeStruct((t, d), BF16)
    res = pl.pallas_call(
        body, name="attn_bwd", grid=grid,
        in_specs=[seq(0), seq(npair), seq(2 * npair), seq(0), seq(0)] + ex.specs,
        out_specs=[seq(0), seq(0), seq(0)] + ex.specs, out_shape=[sds, sds, sds] + ex.out_shape,
        scratch_shapes=[pltpu.VMEM((s, LANE), F32), pltpu.VMEM((s, LANE), F32),
                        pltpu.VMEM((nheads, s, LANE), F32), pltpu.VMEM((nheads, bq, LANE), F32),
                        pltpu.VMEM((nheads, ATT_UNROLL, bq, bk), F32), pltpu.VMEM((nheads, ATT_UNROLL, bq, bk), F32),
                        pltpu.VMEM((nheads, ATT_UNROLL, bq, bk), BF16),
                        pltpu.VMEM((nheads, ATT_UNROLL, bq, bk), BF16)] + ex.scratch,
        compiler_params=_cp(("arbitrary", "arbitrary")))(qkv, qkv, qkv, car, do, *ex.arrs)
    return res[0], res[1], res[2], list(res[3:])


def _conv3(u_ref, w, bias, c, r0, rc):
    x = u_ref[pl.ds(r0, rc), :].astype(F32)
    p0 = pl.multiple_of(jnp.maximum(r0 - 16, 0), 16)
    prev = u_ref[pl.ds(p0, 16), :].astype(F32)
    prev = jnp.where(c > 0, prev, 0.0)
    row = lax.broadcasted_iota(jnp.int32, (rc, 1), 0)
    s1 = jnp.where(row == 0, prev[15:16, :], pltpu.roll(x, 1, 0))
    s2 = jnp.where(row == 0, prev[14:15, :], jnp.where(row == 1, prev[15:16, :], pltpu.roll(x, 2, 0)))
    cv = w[2:3, :] * x + w[1:2, :] * s1 + w[0:1, :] * s2 + bias
    return cv, x, s1, s2


def _sigmoid(x):
    return 1.0 / (1.0 + jnp.exp(-x))


def _ffn_act_fwd(name, up8, cw8, cb8, nb, s):
    _, t, c_w = up8.shape
    rc = _tile(s, 256)
    nch = s // rc
    half = N_DEV // 2

    def body(ug_ref, uv_ref, wg_ref, wv_ref, bg_ref, bv_ref, act_ref):
        wg, wv, bg, bv = wg_ref[...], wv_ref[...], bg_ref[...], bv_ref[...]

        def chunk(c, carry):
            r0 = pl.multiple_of(c * rc, rc)
            cg = _conv3(ug_ref, wg, bg, c, r0, rc)[0]
            cv = _conv3(uv_ref, wv, bv, c, r0, rc)[0]
            act_ref[pl.ds(r0, rc), :] = (cg * _sigmoid(cg) * cv).astype(act_ref.dtype)
            return carry

        lax.fori_loop(0, nch, chunk, 0)

    def slab(off):
        return pl.BlockSpec((None, s, c_w), lambda k, b: (k + off, b, 0))

    def par(rows, off):
        return pl.BlockSpec((None, rows, c_w), lambda k, b: (k + off, 0, 0))

    return pl.pallas_call(
        body, name=name, grid=(half, nb),
        in_specs=[slab(0), slab(half), par(3, 0), par(3, half), par(1, 0), par(1, half)],
        out_specs=pl.BlockSpec((None, s, c_w), lambda k, b: (k, b, 0)),
        out_shape=jax.ShapeDtypeStruct((half, t, c_w), BF16),
        compiler_params=_cp(("parallel", "parallel")))(up8, up8, cw8, cw8, cb8, cb8)


def _ffn_act_bwd(name, up8, dact4, cw8, cb8, nb, s):
    _, t, c_w = up8.shape
    rc = _tile(s, 256)
    nch = s // rc
    half = N_DEV // 2

    def body(u_ref, da_ref, w_ref, b_ref, dup_ref, dcw_ref, dcb_ref):
        w2, b2 = w_ref[...], b_ref[...]
        row = lax.broadcasted_iota(jnp.int32, (rc, 1), 0)

        @pl.when(pl.program_id(1) == 0)
        def _():
            dcw_ref[...] = jnp.zeros_like(dcw_ref)
            dcb_ref[...] = jnp.zeros_like(dcb_ref)

        def chunk(i, carry):
            c = nch - 1 - i
            r0 = pl.multiple_of(c * rc, rc)
            convs = [_conv3(u_ref.at[h], w2[h], b2[h], c, r0, rc) for h in range(2)]
            gt, vl = convs[0][0], convs[1][0]
            da = da_ref[pl.ds(r0, rc), :].astype(F32)
            sg = _sigmoid(gt)
            dcvs = [da * vl * sg * (1.0 + gt * (1.0 - sg)), da * gt * sg]
            out = []
            for h in range(2):
                n0, n1, a0, a1, a2, ab = carry[6 * h:6 * h + 6]
                dcv, (_, x, s1, s2), w = dcvs[h], convs[h], w2[h]
                t1 = jnp.where(row == rc - 1, n0, pltpu.roll(dcv, rc - 1, 0))
                t2 = jnp.where(row == rc - 2, n0, jnp.where(row == rc - 1, n1, pltpu.roll(dcv, rc - 2, 0)))
                dup = w[2:3, :] * dcv + w[1:2, :] * t1 + w[0:1, :] * t2
                dup_ref[h, pl.ds(r0, rc), :] = dup.astype(dup_ref.dtype)
                out += [dcv[0:1, :], dcv[1:2, :],
                        a0 + jnp.sum(dcv * s2, axis=0, keepdims=True), a1 + jnp.sum(dcv * s1, axis=0, keepdims=True),
                        a2 + jnp.sum(dcv * x, axis=0, keepdims=True), ab + jnp.sum(dcv, axis=0, keepdims=True)]
            return tuple(out)

        z = jnp.zeros((1, c_w), F32)
        fin = lax.fori_loop(0, nch, chunk, (z,) * 12)
        for h in range(2):
            _, _, a0, a1, a2, ab = fin[6 * h:6 * h + 6]
            dcw_ref[h, 0:1, :] += a0
            dcw_ref[h, 1:2, :] += a1
            dcw_ref[h, 2:3, :] += a2
            dcb_ref[h] += ab

    def pair(rows, per_seq):
        return pl.BlockSpec((2, None, rows, c_w), (lambda k, b: (0, k, b, 0)) if per_seq else (lambda k, b: (0, k, 0, 0)))

    four = lambda a: a.reshape((2, half) + a.shape[1:])
    dup, dcw, dcb = pl.pallas_call(
        body, name=name, grid=(half, nb),
        in_specs=[pair(s, True), pl.BlockSpec((None, s, c_w), lambda k, b: (k, b, 0)), pair(3, False), pair(1, False)],
        out_specs=[pair(s, True), pair(3, False), pair(1, False)],
        out_shape=[jax.ShapeDtypeStruct((2, half, t, c_w), BF16), jax.ShapeDtypeStruct((2, half, 3, c_w), F32),
                   jax.ShapeDtypeStruct((2, half, 1, c_w), F32)],
        compiler_params=_cp(("parallel", "arbitrary")))(four(up8), dact4, four(cw8), four(cb8))
    return dup.reshape(N_DEV, t, c_w), dcw.reshape(N_DEV, 3, c_w), dcb.reshape(N_DEV, 1, c_w)


_GELU_C0 = math.sqrt(2.0 / math.pi)
_GELU_C1 = 0.044715


def _rowwise(name, body, ins, in_kinds, out_kinds, t, d, tr_pref=512):
    tr = _tile(t, tr_pref)
    row = pl.BlockSpec((tr, d), lambda i: (i, 0))
    vec = pl.BlockSpec((1, d), lambda i: (0, 0))
    in_specs = [row if k == "row" else vec for k in in_kinds]
    out_specs = [row if k[0] == "row" else vec for k in out_kinds]
    out_shape = [jax.ShapeDtypeStruct((t, d) if k[0] == "row" else (1, d), k[1]) for k in out_kinds]
    has_acc = any(k[0] == "acc" for k in out_kinds)
    return pl.pallas_call(body, name=name, grid=(t // tr,), in_specs=in_specs, out_specs=out_specs,
                          out_shape=out_shape,
                          compiler_params=_cp(("arbitrary",) if has_acc else ("parallel",)))(*ins)


def _ssm_post_fwd(ys, u, dskip):
    t, d = ys.shape

    def body(ys_ref, u_ref, ds_ref, y_ref, z_ref):
        y = ys_ref[...].astype(F32) + ds_ref[...] * u_ref[...].astype(F32)
        y_ref[...] = y
        th = jnp.tanh(_GELU_C0 * (y + _GELU_C1 * y * y * y))
        z_ref[...] = (0.5 * y * (1.0 + th)).astype(z_ref.dtype)

    return _rowwise("ssm_post_fwd", body, [ys, u, dskip], ["row", "row", "vec"],
                    [("row", F32), ("row", BF16)], t, d)


def _glu_fwd(z, gl, bglu):
    t, d = z.shape

    def body(z_ref, gl_ref, b_ref, o_ref):
        o_ref[...] = (z_ref[...].astype(F32) * _sigmoid(gl_ref[...] + b_ref[...])).astype(o_ref.dtype)

    return _rowwise("glu_fwd", body, [z, gl, bglu], ["row", "row", "vec"], [("row", BF16)], t, d)[0]


def _glu_bwd(dgg, z, gl, bglu):
    t, d = z.shape

    def body(dg_ref, z_ref, gl_ref, b_ref, dgl_ref, dz_ref, db_ref):
        sg = _sigmoid(gl_ref[...] + b_ref[...])
        dg = dg_ref[...]
        dgl = dg * z_ref[...].astype(F32) * sg * (1.0 - sg)
        dgl_ref[...] = dgl.astype(dgl_ref.dtype)
        dz_ref[...] = dg * sg

        @pl.when(pl.program_id(0) == 0)
        def _():
            db_ref[...] = jnp.zeros_like(db_ref)

        db_ref[...] += jnp.sum(dgl, axis=0, keepdims=True)

    return _rowwise("glu_bwd", body, [dgg, z, gl, bglu], ["row", "row", "row", "vec"],
                    [("row", BF16), ("row", F32), ("acc", F32)], t, d)


def _ssm_post_bwd(dz1, dz2, y, u, dskip):
    t, d = y.shape

    def body(a_ref, b_ref, y_ref, u_ref, ds_ref, dy_ref, du_ref, dd_ref):
        yv = y_ref[...]
        inner = _GELU_C0 * (yv + _GELU_C1 * yv * yv * yv)
        th = jnp.tanh(inner)
        dgelu = 0.5 * (1.0 + th) + 0.5 * yv * (1.0 - th * th) * _GELU_C0 * (1.0 + 3.0 * _GELU_C1 * yv * yv)
        dy = (a_ref[...] + b_ref[...]) * dgelu
        dy_ref[...] = dy.astype(dy_ref.dtype)
        du_ref[...] = dy * ds_ref[...]

        @pl.when(pl.program_id(0) == 0)
        def _():
            dd_ref[...] = jnp.zeros_like(dd_ref)

        dd_ref[...] += jnp.sum(dy * u_ref[...].astype(F32), axis=0, keepdims=True)

    return _rowwise("ssm_post_bwd", body, [dz1, dz2, y, u, dskip], ["row", "row", "row", "row", "vec"],
                    [("row", BF16), ("row", F32), ("acc", F32)], t, d)


def _add_cast(a, b):
    t, d = a.shape

    def body(a_ref, b_ref, o_ref):
        o_ref[...] = (a_ref[...].astype(F32) + b_ref[...].astype(F32)).astype(o_ref.dtype)

    return _rowwise("add_cast", body, [a, b], ["row", "row"], [("row", BF16)], t, d)[0]


def _ssm_scan(e_re, e_im, lam_re, lam_im, nb):
    r, n = e_re.shape
    nc = r // nb
    cb = _tile(n, 512)

    def body(er_ref, ei_ref, lr_ref, li_ref, xr_ref, xi_ref):
        lr, li = lr_ref[...], li_ref[...]
        rid = lax.broadcasted_iota(jnp.int32, (8, 1), 0)

        def tile(i, carry):
            out = []
            for b in range(nb):
                xr, xi = carry[2 * b:2 * b + 2]
                r0 = pl.multiple_of(b * nc + i * 8, 8)
                er, ei = er_ref[pl.ds(r0, 8), :], ei_ref[pl.ds(r0, 8), :]
                outr, outi = jnp.zeros((8, cb), F32), jnp.zeros((8, cb), F32)
                for j in range(8):
                    outr = jnp.where(rid == j, xr, outr)
                    outi = jnp.where(rid == j, xi, outi)
                    xr, xi = lr * xr - li * xi + er[j:j + 1, :], li * xr + lr * xi + ei[j:j + 1, :]
                xr_ref[pl.ds(r0, 8), :] = outr
                xi_ref[pl.ds(r0, 8), :] = outi
                out += [xr, xi]
            return tuple(out)

        lax.fori_loop(0, nc // 8, tile, (jnp.zeros((1, cb), F32),) * (2 * nb))

    mat = pl.BlockSpec((r, cb), lambda j: (0, j))
    vec = pl.BlockSpec((1, cb), lambda j: (0, j))
    sds = jax.ShapeDtypeStruct((r, n), F32)
    return pl.pallas_call(body, name="ssm_scan", grid=(n // cb,), in_specs=[mat, mat, vec, vec],
                          out_specs=[mat, mat], out_shape=[sds, sds],
                          compiler_params=_cp(("parallel",)))(e_re, e_im, lam_re, lam_im)


def _ssm_scan_bwd(dxp_re, dxp_im, lam_re, lam_im, nb):
    r, n = dxp_re.shape
    nc = r // nb
    cb = _tile(n, 512)

    def body(dr_ref, di_ref, lr_ref, li_ref, er_ref, ei_ref):
        lr, li = lr_ref[...], li_ref[...]
        rid = lax.broadcasted_iota(jnp.int32, (8, 1), 0)

        def tile(i, carry):
            out = []
            for b in range(nb):
                gr, gi = carry[2 * b:2 * b + 2]
                r0 = pl.multiple_of(b * nc + (nc // 8 - 1 - i) * 8, 8)
                dr, di = dr_ref[pl.ds(r0, 8), :], di_ref[pl.ds(r0, 8), :]
                outr, outi = jnp.zeros((8, cb), F32), jnp.zeros((8, cb), F32)
                for j in range(7, -1, -1):
                    outr = jnp.where(rid == j, gr, outr)
                    outi = jnp.where(rid == j, gi, outi)
                    gr, gi = dr[j:j + 1, :] + lr * gr + li * gi, di[j:j + 1, :] + lr * gi - li * gr
                er_ref[pl.ds(r0, 8), :] = outr
                ei_ref[pl.ds(r0, 8), :] = outi
                out += [gr, gi]
            return tuple(out)

        lax.fori_loop(0, nc // 8, tile, (jnp.zeros((1, cb), F32),) * (2 * nb))

    mat = pl.BlockSpec((r, cb), lambda j: (0, j))
    vec = pl.BlockSpec((1, cb), lambda j: (0, j))
    sds = jax.ShapeDtypeStruct((r, n), F32)
    return pl.pallas_call(body, name="ssm_scan_bwd", grid=(n // cb,), in_specs=[mat, mat, vec, vec],
                          out_specs=[mat, mat], out_shape=[sds, sds],
                          compiler_params=_cp(("parallel",)))(dxp_re, dxp_im, lam_re, lam_im)


def _ssm_operators(a_re, a_im, log_dt, b_re, b_im, c_re, c_im):
    g, p = a_re.shape
    h = b_re.shape[-1]
    ln = SSM_L
    sg = LANE // h
    na = g // sg
    hp = lax.Precision.HIGHEST
    lam = lax.complex(a_re, a_im)
    ldt = lam * jnp.exp(log_dt)[:, None]
    lam_bar = jnp.exp(ldt)
    bbar = ((lam_bar - 1.0) / lam)[..., None] * lax.complex(b_re, b_im)
    cm = lax.complex(c_re, c_im)
    steps = jnp.arange(ln + 1, dtype=F32)
    pw = jnp.exp(ldt[:, None, :] * steps[None, :, None])
    kd = jnp.einsum("ghp,gdp,gpk->gdhk", cm, pw[:, :ln], bbar, precision=hp).real

    def blockdiag(compact, rows_per, cols_per):
        spread = jnp.tile(jnp.eye(cols_per, dtype=F32), (1, sg))
        same = (jnp.arange(sg * rows_per)[:, None] // rows_per == jnp.arange(sg * cols_per)[None, :] // cols_per)
        return jnp.einsum("...rc,cl->...rl", compact, spread, precision=hp) * same.astype(F32)

    def by_supergroup(x, rows_per, cols_per):
        return x.reshape(na, sg, ln, rows_per, cols_per).transpose(0, 2, 1, 3, 4).reshape(na, ln, sg * rows_per, cols_per)

    kt = blockdiag(by_supergroup(kd.transpose(0, 1, 3, 2), h, h), h, h)
    zero = jnp.zeros_like(kt[:, 0])
    tm = jnp.concatenate([jnp.concatenate([kt[:, tau - sig] if tau >= sig else zero for tau in range(ln)], axis=2)
                          for sig in range(ln)], axis=1)
    wxc = (pw[:, ln - 1 - jnp.arange(ln)][:, :, :, None] * bbar[:, None]).transpose(0, 1, 3, 2)
    wx = lambda part: blockdiag(by_supergroup(part, h, p), h, p).reshape(na, ln * LANE, sg * p)
    cpc = (cm[:, None] * pw[:, 1:ln + 1][:, :, None, :]).transpose(0, 1, 3, 2)

    def wy(part):
        big = blockdiag(by_supergroup(part, p, h), p, h)
        return jnp.concatenate([big[:, tau] for tau in range(ln)], axis=2)

    lam_l = pw[:, ln]
    return (tm, wx(wxc.real), wx(wxc.imag), wy(cpc.real), wy(-cpc.imag),
            lam_l.real.reshape(1, g * p), lam_l.imag.reshape(1, g * p))


def _chunk_view(a):
    t, d = a.shape
    return a.reshape(t // SSM_L, SSM_L * d)


def _sg_specs(r4, d, wst):
    nblk = d // LANE
    cat = [pl.BlockSpec((r4, LANE), functools.partial(lambda j, tau: (0, tau * nblk + j), tau=tau))
           for tau in range(SSM_L)]
    plane = pl.BlockSpec((r4, wst), lambda j: (0, j))
    mat = lambda rows, cols: pl.BlockSpec((None, rows, cols), lambda j: (j, 0, 0))
    piece = pl.BlockSpec((r4, LANE), lambda j: (0, j))
    return cat, plane, mat, piece


def _lane_cat(refs):
    return jnp.concatenate([r[...] for r in refs], axis=1)


def _bdot(a, b, dims):
    return lax.dot_general(a.astype(BF16), b.astype(BF16), dims, preferred_element_type=F32)


def _ssm_core_fwd(u, ops, nb):
    tm, wxr, wxi, wyr, wyi, lam_re, lam_im = ops
    t, d = u.shape
    ln, na, wch, wst = SSM_L, tm.shape[0], tm.shape[1], wxr.shape[2]
    r4 = t // ln
    n = na * wst
    u4 = _chunk_view(u)
    cat, plane, mat, piece = _sg_specs(r4, d, wst)
    pds = jax.ShapeDtypeStruct((r4, n), F32)

    def states(*refs):
        ucat = _lane_cat(refs[:ln])
        wr_ref, wi_ref, er_ref, ei_ref = refs[ln:]
        er_ref[...] = _bdot(ucat, wr_ref[...], _NN)
        ei_ref[...] = _bdot(ucat, wi_ref[...], _NN)

    e_re, e_im = pl.pallas_call(
        states, name="ssm_states", grid=(na,), in_specs=cat + [mat(wch, wst)] * 2, out_specs=[plane, plane],
        out_shape=[pds, pds], compiler_params=_cp(("parallel",)))(*([u4] * ln), wxr, wxi)
    xp_re, xp_im = _ssm_scan(e_re, e_im, lam_re, lam_im, nb)

    def outputs(*refs):
        ucat = _lane_cat(refs[:ln])
        tm_ref, xr_ref, xi_ref, wr_ref, wi_ref = refs[ln:ln + 5]
        y = (_bdot(ucat, tm_ref[...], _NN) + _bdot(xr_ref[...], wr_ref[...], _NN)
             + _bdot(xi_ref[...], wi_ref[...], _NN))
        for tau, o_ref in enumerate(refs[ln + 5:]):
            o_ref[...] = y[:, tau * LANE:(tau + 1) * LANE].astype(o_ref.dtype)

    ys = pl.pallas_call(
        outputs, name="ssm_y", grid=(na,),
        in_specs=cat + [mat(wch, wch), plane, plane, mat(wst, wch), mat(wst, wch)], out_specs=[piece] * ln,
        out_shape=[jax.ShapeDtypeStruct((r4, d), BF16)] * ln,
        compiler_params=_cp(("parallel",)))(*([u4] * ln), tm, xp_re, xp_im, wyr, wyi)
    return jnp.concatenate(ys, axis=1).reshape(t, d), xp_re, xp_im


def _ssm_core_bwd(dy, u, xp_re, xp_im, ops, nb):
    tm, wxr, wxi, wyr, wyi, lam_re, lam_im = ops
    t, d = u.shape
    ln, na, wch, wst = SSM_L, tm.shape[0], tm.shape[1], wxr.shape[2]
    r4 = t // ln
    n = na * wst
    u4, dy4 = _chunk_view(u), _chunk_view(dy)
    cat, plane, mat, piece = _sg_specs(r4, d, wst)
    pds = jax.ShapeDtypeStruct((r4, n), F32)

    def dstates(*refs):
        dycat = _lane_cat(refs[:ln])
        wr_ref, wi_ref, dr_ref, di_ref = refs[ln:]
        dr_ref[...] = _bdot(dycat, wr_ref[...], _NT)
        di_ref[...] = _bdot(dycat, wi_ref[...], _NT)

    dxp_re, dxp_im = pl.pallas_call(
        dstates, name="ssm_dxp", grid=(na,), in_specs=cat + [mat(wst, wch)] * 2, out_specs=[plane, plane],
        out_shape=[pds, pds], compiler_params=_cp(("parallel",)))(*([dy4] * ln), wyr, wyi)
    de_re, de_im = _ssm_scan_bwd(dxp_re, dxp_im, lam_re, lam_im, nb)

    def dinputs(*refs):
        dycat = _lane_cat(refs[:ln])
        tm_ref, er_ref, ei_ref, wr_ref, wi_ref = refs[ln:ln + 5]
        du = (_bdot(dycat, tm_ref[...], _NT) + _bdot(er_ref[...], wr_ref[...], _NT)
              + _bdot(ei_ref[...], wi_ref[...], _NT))
        for tau, o_ref in enumerate(refs[ln + 5:]):
            o_ref[...] = du[:, tau * LANE:(tau + 1) * LANE].astype(o_ref.dtype)

    dus = pl.pallas_call(
        dinputs, name="ssm_du", grid=(na,),
        in_specs=cat + [mat(wch, wch), plane, plane, mat(wch, wst), mat(wch, wst)], out_specs=[piece] * ln,
        out_shape=[jax.ShapeDtypeStruct((r4, d), BF16)] * ln,
        compiler_params=_cp(("parallel",)))(*([dy4] * ln), tm, de_re, de_im, wxr, wxi)

    def doperators(*refs):
        ucat, dycat = _lane_cat(refs[:ln]), _lane_cat(refs[ln:2 * ln])
        (er_ref, ei_ref, xr_ref, xi_ref, dtm_ref, dwxr_ref, dwxi_ref, dwyr_ref, dwyi_ref,
         dlr_ref, dli_ref) = refs[2 * ln:]
        er, ei, xr, xi = er_ref[...], ei_ref[...], xr_ref[...], xi_ref[...]
        dtm_ref[...] = _bdot(ucat, dycat, _TN)
        dwxr_ref[...] = _bdot(ucat, er, _TN)
        dwxi_ref[...] = _bdot(ucat, ei, _TN)
        dwyr_ref[...] = _bdot(xr, dycat, _TN)
        dwyi_ref[...] = _bdot(xi, dycat, _TN)
        dlr_ref[...] = jnp.sum(er * xr + ei * xi, axis=0, keepdims=True)
        dli_ref[...] = jnp.sum(ei * xr - er * xi, axis=0, keepdims=True)

    mds = lambda rows, cols: jax.ShapeDtypeStruct((na, rows, cols), F32)
    vec = pl.BlockSpec((1, wst), lambda j: (0, j))
    vds = jax.ShapeDtypeStruct((1, n), F32)
    d_ops = pl.pallas_call(
        doperators, name="ssm_dops", grid=(na,), in_specs=cat + cat + [plane] * 4,
        out_specs=[mat(wch, wch), mat(wch, wst), mat(wch, wst), mat(wst, wch), mat(wst, wch), vec, vec],
        out_shape=[mds(wch, wch), mds(wch, wst), mds(wch, wst), mds(wst, wch), mds(wst, wch), vds, vds],
        compiler_params=_cp(("parallel",)))(*([u4] * ln), *([dy4] * ln), de_re, de_im, xp_re, xp_im)
    return jnp.concatenate(dus, axis=1).reshape(t, d), tuple(d_ops)


def _modfin_fwd(c_all, w_mod, w_fin):
    n, d = c_all.shape
    nl, _, cm = w_mod.shape
    cf = w_fin.shape[1]
    width = nl * cm + cf
    hp = lax.Precision.HIGHEST

    def body(c_ref, wm_ref, wf_ref, act_ref, out_ref):
        cv = c_ref[...]
        act = cv * _sigmoid(cv)
        act_ref[...] = act
        for i in range(nl):
            out_ref[:, i * cm:(i + 1) * cm] = jnp.dot(act, wm_ref[i], preferred_element_type=F32, precision=hp)
        out_ref[:, nl * cm:] = jnp.dot(act, wf_ref[...], preferred_element_type=F32, precision=hp)

    return pl.pallas_call(body, name="modfin_fwd",
                          out_shape=[jax.ShapeDtypeStruct((n, d), F32), jax.ShapeDtypeStruct((n, width), F32)],
                          compiler_params=_cp(None))(c_all, w_mod, w_fin)


def _modfin_bwd(c_act_t, dmod_loc, dfin_loc, dall):
    d, n = c_act_t.shape
    nl, _, cm = dmod_loc.shape
    cf = dfin_loc.shape[1]
    hp = lax.Precision.HIGHEST

    def body(ct_ref, dm_ref, df_ref, da_ref, gwm_ref, gwf_ref, gb_ref):
        ct = ct_ref[...]
        for i in range(nl):
            gwm_ref[i] = jnp.dot(ct, dm_ref[i], preferred_element_type=F32, precision=hp)
        gwf_ref[...] = jnp.dot(ct, df_ref[...], preferred_element_type=F32, precision=hp)
        gb_ref[...] = jnp.sum(da_ref[...], axis=0, keepdims=True)

    return pl.pallas_call(body, name="modfin_bwd",
                          out_shape=[jax.ShapeDtypeStruct((nl, d, cm), F32), jax.ShapeDtypeStruct((d, cf), F32),
                                     jax.ShapeDtypeStruct((1, dall.shape[1]), F32)],
                          compiler_params=_cp(None))(c_act_t, dmod_loc, dfin_loc, dall)


def _adamw(name, gparts, w, m, v):
    n, r, c = gparts.shape
    tr = _tile(r, 256)

    def body(gp_ref, w_ref, m_ref, v_ref, g_ref, d_ref, mo_ref, vo_ref):
        _adamw_step(gp_ref, w_ref, m_ref, v_ref, g_ref, d_ref, mo_ref, vo_ref)

    mat = pl.BlockSpec((tr, c), lambda i: (i, 0))
    sds = jax.ShapeDtypeStruct((r, c), F32)
    return pl.pallas_call(body, name=name, grid=(r // tr,),
                          in_specs=[pl.BlockSpec((n, tr, c), lambda i: (0, i, 0)), mat, mat, mat],
                          out_specs=[mat] * 4, out_shape=[sds] * 4,
                          compiler_params=_cp(("parallel",)))(gparts, w, m, v)


def _adamw_step(gp_ref, w_ref, m_ref, v_ref, g_ref, d_ref, mo_ref, vo_ref):
    gsum = gp_ref[0].astype(F32)
    for j in range(1, gp_ref.shape[0]):
        gsum = gsum + gp_ref[j].astype(F32)
    mn = ADAM_B1 * m_ref[...] + (1.0 - ADAM_B1) * gsum
    vn = ADAM_B2 * v_ref[...] + (1.0 - ADAM_B2) * (gsum * gsum)
    g_ref[...] = gsum
    mo_ref[...] = mn
    vo_ref[...] = vn
    m_hat = mn * (1.0 / (1.0 - ADAM_B1 ** ADAM_STEP))
    v_hat = vn * (1.0 / (1.0 - ADAM_B2 ** ADAM_STEP))
    d_ref[...] = -ADAM_LR * (m_hat / (jnp.sqrt(v_hat) + ADAM_EPS) + ADAM_WD * w_ref[...])


def _adamw_many(name, entries):
    k = len(entries)

    def body(*refs):
        for i in range(k):
            _adamw_step(*refs[4 * i:4 * i + 4], *refs[4 * k + 4 * i:4 * k + 4 * i + 4])

    ops = [a for e in entries for a in e]
    out_shape = [jax.ShapeDtypeStruct(e[1].shape, F32) for e in entries for _ in range(4)]
    return pl.pallas_call(body, name=name, out_shape=out_shape, compiler_params=_cp(None))(*ops)


class _Exchange:
    def __init__(self, arrs, gathers):
        self.arrs = [pltpu.with_memory_space_constraint(a, pltpu.HBM) for a in arrs]
        self.gathers = list(gathers)
        self.n = len(arrs)
        self.out_shape = [pltpu.HBM(((N_DEV,) + a.shape) if g else a.shape, a.dtype)
                          for a, g in zip(arrs, self.gathers)]
        self.specs = [pl.BlockSpec(memory_space=pltpu.HBM)] * self.n
        self.scratch = [pltpu.SemaphoreType.DMA((self.n, N_DEV - 1)), pltpu.SemaphoreType.DMA((self.n, N_DEV - 1)),
                        pltpu.SemaphoreType.DMA((self.n,))]

    def _copies(self, ins, outs, sems):
        send_sems, recv_sems, local_sems = sems
        x, y, c = lax.axis_index("x"), lax.axis_index("y"), lax.axis_index("c")
        me = 4 * x + 2 * y + c
        local, sends, recvs = [], [], []
        for i in range(self.n):
            src_me = ins[i] if self.gathers[i] else ins[i].at[me]
            local.append(pltpu.make_async_copy(src_me, outs[i].at[me], local_sems.at[i]))
        for dd in range(1, N_DEV):
            px = jnp.bitwise_xor(x, dd >> 2)
            py = jnp.bitwise_xor(y, (dd >> 1) & 1)
            pc = jnp.bitwise_xor(c, dd & 1)
            pid = 4 * px + 2 * py + pc
            for i in range(self.n):
                src = ins[i] if self.gathers[i] else ins[i].at[pid]
                sems_i = dict(send_sem=send_sems.at[i, dd - 1], recv_sem=recv_sems.at[i, dd - 1],
                              device_id=(px, py, pc), device_id_type=MESH)
                sends.append(pltpu.make_async_remote_copy(src_ref=src, dst_ref=outs[i].at[me], **sems_i))
                recvs.append(pltpu.make_async_remote_copy(src_ref=src, dst_ref=outs[i].at[pid], **sems_i))
        return local, sends, recvs

    def start(self, ins, outs, sems):
        local, sends, _ = self._copies(ins, outs, sems)
        for cp in local + sends:
            cp.start()

    def wait(self, ins, outs, sems):
        local, sends, recvs = self._copies(ins, outs, sems)
        for cp in recvs:
            cp.wait_recv()
        for cp in sends:
            cp.wait_send()
        for cp in local:
            cp.wait()


class _NoExchange:
    n, arrs, specs, out_shape, scratch = 0, [], [], [], []

    def start(self, ins, outs, sems):
        pass

    def wait(self, ins, outs, sems):
        pass


def _exchange(name, arrs, gathers):
    ex = _Exchange(arrs, gathers)
    n = ex.n

    def body(*refs):
        ins, outs, sems = refs[:n], refs[n:2 * n], refs[2 * n:]
        ex.start(ins, outs, sems)
        ex.wait(ins, outs, sems)

    outs = pl.pallas_call(body, name=name, in_specs=ex.specs, out_specs=ex.specs, out_shape=ex.out_shape,
                          scratch_shapes=ex.scratch)(*ex.arrs)
    return list(outs)


def kernel(x, c, norm_mix, norm_ffn, w_mod, b_mod, w_qkv, w_o_attn, w_in_ssm, a_re, a_im, log_dt, b_re, b_im, c_re, c_im, d_skip, w_glu, b_glu, w_o_ssm, w_up, conv_w, conv_b, w_down, norm_out, w_fin, b_fin, loss_target, m_norm_mix, m_norm_ffn, m_w_mod, m_b_mod, m_w_qkv, m_w_o_attn, m_w_in_ssm, m_a_re, m_a_im, m_log_dt, m_b_re, m_b_im, m_c_re, m_c_im, m_d_skip, m_w_glu, m_b_glu, m_w_o_ssm, m_w_up, m_conv_w, m_conv_b, m_w_down, m_norm_out, m_w_fin, m_b_fin, v_norm_mix, v_norm_ffn, v_w_mod, v_b_mod, v_w_qkv, v_w_o_attn, v_w_in_ssm, v_a_re, v_a_im, v_log_dt, v_b_re, v_b_im, v_c_re, v_c_im, v_d_skip, v_w_glu, v_b_glu, v_w_o_ssm, v_w_up, v_conv_w, v_conv_b, v_w_down, v_norm_out, v_w_fin, v_b_fin):
    nb, s, d = x.shape
    t = nb * s
    n_seq = nb * N_DEV
    me = 4 * lax.axis_index("x") + 2 * lax.axis_index("y") + lax.axis_index("c")
    cm = w_mod.shape[2]
    cf = w_fin.shape[1]
    c_up = w_up.shape[2]
    r_dn = w_down.shape[1]
    g_ssm = d // SSM_H

    wq8, c8 = _exchange("gather_first", [w_qkv[0].astype(BF16), c], [True, True])
    later = _Exchange([w_o_attn[0].astype(BF16), w_in_ssm[0].astype(BF16), w_glu[0].astype(BF16),
                       w_o_ssm[0].astype(BF16), w_up[0].astype(BF16), w_up[1].astype(BF16),
                       w_down[0].astype(BF16), w_down[1].astype(BF16), conv_w, d_skip, b_glu], [True] * 11)
    half = N_DEV // 2
    cb_l = [conv_b[i].reshape(N_DEV, 1, c_up) for i in range(2)]
    c_all = c8.reshape(n_seq, d)

    c_act, modloc = _modfin_fwd(c_all, w_mod, w_fin)
    (mod8,) = _exchange("gather_mod", [modloc], [True])
    mine = lax.dynamic_slice_in_dim(mod8, me * nb, nb, axis=1)
    mods = []
    for i in range(2):
        mi = mine[:, :, i * cm:(i + 1) * cm].transpose(1, 0, 2).reshape(nb, N_DEV * cm) + b_mod[i]
        mods.append([mi[:, j * d:(j + 1) * d].reshape(nb, 1, d) for j in range(6)])
    fin = mine[:, :, 2 * cm:].transpose(1, 0, 2).reshape(nb, N_DEV * cf) + b_fin
    sh_f, sc_f = fin[:, :d].reshape(nb, 1, d), fin[:, d:].reshape(nb, 1, d)

    row = lambda a: a.reshape(1, -1)
    x0 = x.reshape(t, d)

    def ffn_fwd(i, xin, sh, sc, gate):
        h = _norm_mod_fwd(f"ffn{i}_norm", xin, row(norm_ffn[i]), sh, sc, nb)
        up = _mm(f"ffn{i}_up", h, wup8[i], (t // tm_, N_DEV, 1),
                 pl.BlockSpec((tm_, d), lambda a, b, k: (a, 0)), pl.BlockSpec((None, d, c_up), lambda a, b, k: (b, 0, 0)),
                 pl.BlockSpec((None, tm_, c_up), lambda a, b, k: (b, a, 0)),
                 jax.ShapeDtypeStruct((N_DEV, t, c_up), BF16), _NN, (tm_, c_up))
        act = _ffn_act_fwd(f"ffn{i}_act", up, cw_l[i], cb_l[i], nb, s)
        yf = _mm(f"ffn{i}_down", act, wd4[i], (t // tm_, 1, half),
                 pl.BlockSpec((None, tm_, c_up), lambda a, b, k: (k, a, 0)),
                 pl.BlockSpec((None, c_up, d), lambda a, b, k: (k, 0, 0)),
                 pl.BlockSpec((tm_, d), lambda a, b, k: (a, 0)), jax.ShapeDtypeStruct((t, d), F32), _NN, (tm_, d))
        xout = _gate_add(f"ffn{i}_res", xin, yf, gate, nb)
        return xout, (h, up, act, yf)

    tm_ = _tile(t, 2048)
    sh1, sc1, g1, sh2, sc2, g2 = mods[0]
    h1 = _norm_mod_fwd("attn_norm", x0, row(norm_mix[0]), sh1, sc1, nb)
    cq = wq8.shape[2]
    qkv = _mm("attn_qkv", h1, wq8, (t // tm_, N_DEV, 1),
              pl.BlockSpec((tm_, d), lambda a, b, k: (a, 0)), pl.BlockSpec((None, d, cq), lambda a, b, k: (b, 0, 0)),
              pl.BlockSpec((tm_, cq), lambda a, b, k: (a, b)), jax.ShapeDtypeStruct((t, 3 * d), BF16), _NN, (tm_, cq))
    o_att, car_att, (wo8, win8, wglu8, wos8, wup8_0, wup8_1, wd8_0, wd8_1, cw8, dskip8, bglu8) = _attn_fwd(
        qkv, nb, s, d, later)
    wo = wo8.reshape(d, d)
    win = win8.reshape(d, d)
    wglu = wglu8.reshape(d, d)
    wos = wos8.reshape(d, d)
    wup8 = [wup8_0, wup8_1]
    wd4 = [wd8_0.reshape(half, 2 * r_dn, d), wd8_1.reshape(half, 2 * r_dn, d)]
    cw_l = [cw8[:, 0], cw8[:, 1]]
    dskip_f = dskip8.reshape(1, d)
    bglu_f = bglu8.reshape(1, d)
    ya = _mm_nn("attn_out", o_att, wo, F32)
    x1 = _gate_add("attn_res", x0, ya, g1, nb)
    x2, ffn0_saved = ffn_fwd(0, x1, sh2, sc2, g2)

    sh1b, sc1b, g1b, sh2b, sc2b, g2b = mods[1]
    ops = _ssm_operators(a_re[0], a_im[0], log_dt[0], b_re[0], b_im[0], c_re[0], c_im[0])
    h3 = _norm_mod_fwd("ssm_norm", x2, row(norm_mix[1]), sh1b, sc1b, nb)
    u = _mm_nn("ssm_in", h3, win, BF16)
    ys_core, xp_re, xp_im = _ssm_core_fwd(u, ops, nb)
    y_ssm, z_ssm = _ssm_post_fwd(ys_core, u, dskip_f)
    gl = _mm_nn("ssm_glu", z_ssm, wglu, F32)
    gg = _glu_fwd(z_ssm, gl, bglu_f)
    ys2 = _mm_nn("ssm_out", gg, wos, F32)
    x3 = _gate_add("ssm_res", x2, ys2, g1b, nb)
    x4, ffn1_saved = ffn_fwd(1, x3, sh2b, sc2b, g2b)

    dx4, g_norm_out, dsh_f, dsc_f, loss_blk = _norm_mod_bwd(
        "final_norm", None, x4, row(norm_out), sh_f, sc_f, None, loss_target.reshape(t, d), nb)
    loss = lax.psum(loss_blk[0, 0], ("x", "y", "c"))

    def ffn_bwd(i, dxo, xin, sc, gate, saved):
        h, up, act, yf = saved
        dyf, dgate = _gate_bwd(f"ffn{i}_res_bwd", dxo, yf, gate, nb)
        dact = _mm(f"ffn{i}_down_dx", dyf, wd4[i], (t // tm_, half, 1),
                   pl.BlockSpec((tm_, d), lambda a, b, k: (a, 0)), pl.BlockSpec((None, c_up, d), lambda a, b, k: (b, 0, 0)),
                   pl.BlockSpec((None, tm_, c_up), lambda a, b, k: (b, a, 0)),
                   jax.ShapeDtypeStruct((half, t, c_up), BF16), _NT, (tm_, c_up))
        tk = _tile(t, 1024)
        gwd = _mm(f"ffn{i}_down_dw", act, dyf, (half, 1, t // tk),
                  pl.BlockSpec((None, tk, c_up), lambda a, b, k: (a, k, 0)), pl.BlockSpec((tk, d), lambda a, b, k: (k, 0)),
                  pl.BlockSpec((None, c_up, d), lambda a, b, k: (a, 0, 0)),
                  jax.ShapeDtypeStruct((half, c_up, d), BF16), _TN, (c_up, d))
        dup, dcw, dcb = _ffn_act_bwd(f"ffn{i}_act_bwd", up, dact, cw_l[i], cb_l[i], nb, s)
        dh = _mm(f"ffn{i}_up_dx", dup, wup8[i], (t // tm_, 1, N_DEV),
                 pl.BlockSpec((None, tm_, c_up), lambda a, b, k: (k, a, 0)),
                 pl.BlockSpec((None, d, c_up), lambda a, b, k: (k, 0, 0)),
                 pl.BlockSpec((tm_, d), lambda a, b, k: (a, 0)), jax.ShapeDtypeStruct((t, d), F32), _NT, (tm_, d))
        gwup = _mm(f"ffn{i}_up_dw", h, dup, (1, N_DEV, t // tk),
                   pl.BlockSpec((tk, d), lambda a, b, k: (k, 0)), pl.BlockSpec((None, tk, c_up), lambda a, b, k: (b, k, 0)),
                   pl.BlockSpec((None, d, c_up), lambda a, b, k: (b, 0, 0)),
                   jax.ShapeDtypeStruct((N_DEV, d, c_up), BF16), _TN, (d, c_up))
        dxi, g_norm, dsh, dsc = _norm_mod_bwd(f"ffn{i}_norm_bwd", dh, xin, row(norm_ffn[i]), None, sc, dxo, None, nb)
        return dxi, (gwup, gwd.reshape(N_DEV, r_dn, d), dcw, dcb, g_norm, dsh, dsc, dgate)

    dx3, (gwup1, gwd1, dcw1, dcb1, g_nffn1, dsh2b, dsc2b, dg2b) = ffn_bwd(1, dx4, x3, sc2b, g2b, ffn1_saved)

    dys2, dg1b = _gate_bwd("ssm_res_bwd", dx3, ys2, g1b, nb)
    dgg = _mm_nt("ssm_out_dx", dys2, wos, F32)
    gwos = _mm_tn("ssm_out_dw", gg, dys2, BF16)
    dgl, dz1, g_bglu = _glu_bwd(dgg, z_ssm, gl, bglu_f)
    dz2 = _mm_nt("ssm_glu_dx", dgl, wglu, F32)
    gwglu = _mm_tn("ssm_glu_dw", z_ssm, dgl, BF16)
    dy_ssm, du_skip, g_dskip = _ssm_post_bwd(dz1, dz2, y_ssm, u, dskip_f)
    du_core, d_ops = _ssm_core_bwd(dy_ssm, u, xp_re, xp_im, ops, nb)
    du = _add_cast(du_core, du_skip)
    dh3 = _mm_nt("ssm_in_dx", du, win, F32)
    gwin = _mm_tn("ssm_in_dw", h3, du, BF16)
    dx2, g_nmix1, dsh1b, dsc1b = _norm_mod_bwd("ssm_norm_bwd", dh3, x2, row(norm_mix[1]), None, sc1b, dx3, None, nb)
    _, ops_vjp = jax.vjp(_ssm_operators, a_re[0], a_im[0], log_dt[0], b_re[0], b_im[0], c_re[0], c_im[0])
    g_ssm_params = ops_vjp(d_ops)

    dx1, (gwup0, gwd0, dcw0, dcb0, g_nffn0, dsh2, dsc2, dg2) = ffn_bwd(0, dx2, x1, sc2, g2, ffn0_saved)

    dya, dg1 = _gate_bwd("attn_res_bwd", dx1, ya, g1, nb)
    do_att = _mm_nt("attn_out_dx", dya, wo, BF16)
    gwo = _mm_tn("attn_out_dw", o_att, dya, BF16)
    rows8 = lambda a: a.reshape(N_DEV, d // N_DEV, d)
    def two_d(w):
        shp = w.shape
        if len(shp) == 1:
            return (1, shp[0])
        if len(shp) == 2:
            return shp
        return (shp[0] * shp[1], math.prod(shp[2:]))

    ssm_w = [a_re, a_im, log_dt, b_re, b_im, c_re, c_im]
    ssm_partial = [g.reshape(two_d(w)) for g, w in zip(g_ssm_params, ssm_w)]
    early = _Exchange([rows8(gwo), rows8(gwin), rows8(gwglu), rows8(gwos), gwup0, gwup1, gwd0, gwd1] + ssm_partial,
                      [False] * 8 + [True] * 7)
    dq, dk, dv, early_res = _attn_bwd(qkv, car_att, do_att, nb, s, d, early)
    ro, rin, rglu, ros, rup0, rup1, rd0, rd1 = early_res[:8]
    ssm8 = early_res[8:]
    dqkv = jnp.concatenate([dq, dk, dv], axis=1)
    tk = _tile(t, 1024)
    gwq8 = _mm("attn_qkv_dw", h1, dqkv, (1, N_DEV, t // tk),
               pl.BlockSpec((tk, d), lambda a, b, k: (k, 0)),
               pl.BlockSpec((tk, cq), lambda a, b, k: (k, b)),
               pl.BlockSpec((None, d, cq), lambda a, b, k: (b, 0, 0)),
               jax.ShapeDtypeStruct((N_DEV, d, cq), BF16), _TN, (d, cq))
    dh1, (rq,) = _mm("attn_qkv_dx", dqkv, wq8, (t // tm_, 1, N_DEV),
                     pl.BlockSpec((tm_, cq), lambda a, b, k: (a, k)),
                     pl.BlockSpec((None, d, cq), lambda a, b, k: (k, 0, 0)),
                     pl.BlockSpec((tm_, d), lambda a, b, k: (a, 0)), jax.ShapeDtypeStruct((t, d), F32), _NT, (tm_, d),
                     ex=_Exchange([gwq8], [False]))
    dx0, g_nmix0, dsh1, dsc1 = _norm_mod_bwd("attn_norm_bwd", dh1, x0, row(norm_mix[0]), None, sc1, dx1, None, nb)
    grad_x = dx0.reshape(nb, s, d)

    dmod = [jnp.concatenate([a.reshape(nb, d) for a in grp], axis=1) for grp in
            ([dsh1, dsc1, dg1, dsh2, dsc2, dg2], [dsh1b, dsc1b, dg1b, dsh2b, dsc2b, dg2b])]
    dfin = jnp.concatenate([dsh_f.reshape(nb, d), dsc_f.reshape(nb, d)], axis=1)
    dmodfin = jnp.concatenate(dmod + [dfin], axis=1)
    flat = lambda a: a.reshape(1, -1)
    dmf8, nmix8, nffn8, nout8, cb8, dskip_g8, bglu_g8, cw_g8 = _exchange(
        "exchange_last",
        [dmodfin, jnp.concatenate([g_nmix0, g_nmix1]), jnp.concatenate([g_nffn0, g_nffn1]), g_norm_out,
         jnp.concatenate([flat(dcb0), flat(dcb1)]), g_dskip, g_bglu, jnp.stack([dcw0, dcw1])], [True] * 8)
    shard = d // N_DEV
    dskip_g8 = lax.dynamic_slice_in_dim(dskip_g8, me * shard, shard, axis=2)
    bglu_g8 = lax.dynamic_slice_in_dim(bglu_g8, me * shard, shard, axis=2)
    cw_g8 = lax.dynamic_slice_in_dim(cw_g8, me, 1, axis=2).reshape(N_DEV, 2 * 3, c_up)

    dall = dmf8.reshape(n_seq, 14 * d)
    dmod_loc = jnp.stack([lax.dynamic_slice_in_dim(dall[:, i * 6 * d:(i + 1) * 6 * d], me * cm, cm, axis=1)
                          for i in range(2)])
    dfin_loc = lax.dynamic_slice_in_dim(dall[:, 12 * d:], me * cf, cf, axis=1)
    g_w_mod, g_w_fin, g_bias = _modfin_bwd(c_act.T, dmod_loc, dfin_loc, dall)
    g_b_mod = g_bias[0, :12 * d].reshape(2, 6 * d)
    g_b_fin = g_bias[0, 12 * d:]

    def big(name, parts, w, m, v):
        shp = w.shape
        r2 = lambda a: a.reshape(-1, shp[-1])
        res = _adamw(name, parts.reshape(parts.shape[0], -1, shp[-1]), r2(w), r2(m), r2(v))
        return [a.reshape(shp) for a in res]

    upd = {}
    upd["w_mod"] = big("adamw_w_mod", g_w_mod[None], w_mod, m_w_mod, v_w_mod)
    upd["w_fin"] = big("adamw_w_fin", g_w_fin[None], w_fin, m_w_fin, v_w_fin)
    upd["w_qkv"] = big("adamw_w_qkv", rq, w_qkv, m_w_qkv, v_w_qkv)
    upd["w_o_attn"] = big("adamw_w_o_attn", ro, w_o_attn, m_w_o_attn, v_w_o_attn)
    upd["w_in_ssm"] = big("adamw_w_in_ssm", rin, w_in_ssm, m_w_in_ssm, v_w_in_ssm)
    upd["w_glu"] = big("adamw_w_glu", rglu, w_glu, m_w_glu, v_w_glu)
    upd["w_o_ssm"] = big("adamw_w_o_ssm", ros, w_o_ssm, m_w_o_ssm, v_w_o_ssm)
    up_l = [big(f"adamw_w_up{i}", r, w_up[i], m_w_up[i], v_w_up[i]) for i, r in enumerate((rup0, rup1))]
    upd["w_up"] = [jnp.stack([up_l[0][j], up_l[1][j]]) for j in range(4)]
    dn_l = [big(f"adamw_w_down{i}", r, w_down[i], m_w_down[i], v_w_down[i]) for i, r in enumerate((rd0, rd1))]
    upd["w_down"] = [jnp.stack([dn_l[0][j], dn_l[1][j]]) for j in range(4)]

    small_names = ["norm_mix", "norm_ffn", "b_mod", "a_re", "a_im", "log_dt", "b_re", "b_im", "c_re", "c_im",
                   "d_skip", "b_glu", "conv_w", "conv_b", "norm_out", "b_fin"]
    small_g = [nmix8, nffn8, g_b_mod[None], *ssm8, dskip_g8, bglu_g8, cw_g8, cb8, nout8, g_b_fin[None]]
    small_w = [norm_mix, norm_ffn, b_mod, a_re, a_im, log_dt, b_re, b_im, c_re, c_im, d_skip, b_glu, conv_w, conv_b,
               norm_out, b_fin]
    small_m = [m_norm_mix, m_norm_ffn, m_b_mod, m_a_re, m_a_im, m_log_dt, m_b_re, m_b_im, m_c_re, m_c_im, m_d_skip,
               m_b_glu, m_conv_w, m_conv_b, m_norm_out, m_b_fin]
    small_v = [v_norm_mix, v_norm_ffn, v_b_mod, v_a_re, v_a_im, v_log_dt, v_b_re, v_b_im, v_c_re, v_c_im, v_d_skip,
               v_b_glu, v_conv_w, v_conv_b, v_norm_out, v_b_fin]
    entries = [(gp.reshape((gp.shape[0],) + two_d(w)), w.reshape(two_d(w)), m.reshape(two_d(w)), v.reshape(two_d(w)))
               for gp, w, m, v in zip(small_g, small_w, small_m, small_v)]
    res = _adamw_many("adamw_small", entries)
    for j, (nm, w) in enumerate(zip(small_names, small_w)):
        upd[nm] = [res[4 * j + k].reshape(w.shape) for k in range(4)]

    order = ["norm_mix", "norm_ffn", "w_mod", "b_mod", "w_qkv", "w_o_attn", "w_in_ssm", "a_re", "a_im", "log_dt",
             "b_re", "b_im", "c_re", "c_im", "d_skip", "w_glu", "b_glu", "w_o_ssm", "w_up", "conv_w", "conv_b",
             "w_down", "norm_out", "w_fin", "b_fin"]
    outs = [loss, grad_x]
    for k in range(4):
        outs += [upd[nm][k] for nm in order]
    return tuple(outs)
```

```python
import functools
import math

import jax
import jax.numpy as jnp
from jax import lax
from jax.experimental import pallas as pl
from jax.experimental.pallas import tpu as pltpu

F32 = jnp.float32
BF16 = jnp.bfloat16
MESH = pl.DeviceIdType.MESH

N_DEV = 8
HEAD_DIM = 64
ATT_BLK = 128
ATT_BQ = 256
ATT_UNROLL = 2
SSM_H = 16
SSM_P = 64
SSM_L = 4
EPS = 1e-6
ADAM_LR, ADAM_B1, ADAM_B2, ADAM_EPS, ADAM_WD, ADAM_STEP = 0.001, 0.9, 0.999, 1e-08, 0.01, 10
V7X_VMEM_LIMIT = 56 * 1024 * 1024
LANE = 128

_NN = (((1,), (0,)), ((), ()))
_NT = (((1,), (1,)), ((), ()))
_TN = (((0,), (0,)), ((), ()))


def _cp(sem):
    return pltpu.CompilerParams(dimension_semantics=sem, vmem_limit_bytes=V7X_VMEM_LIMIT)


def _tile(n, pref):
    if n <= pref:
        return n
    t = pref - pref % 16
    while t >= 16:
        if n % t == 0:
            return t
        t -= 16
    return n


def _mm(name, a, b, grid, a_spec, b_spec, out_spec, out_shape, dims, acc_shape, ex=None):
    nk = grid[-1]
    kax = len(grid) - 1
    ex = ex or _NoExchange()

    def body(*refs):
        a_ref, b_ref = refs[:2]
        ex_ins = refs[2:2 + ex.n]
        o_ref = refs[2 + ex.n]
        ex_outs = refs[3 + ex.n:3 + 2 * ex.n]
        acc_ref = refs[3 + 2 * ex.n]
        sems = refs[4 + 2 * ex.n:]
        first, last = _grid_ends(grid)
        k = pl.program_id(kax)

        @pl.when(first)
        def _():
            ex.start(ex_ins, ex_outs, sems)

        @pl.when(k == 0)
        def _():
            acc_ref[...] = jnp.zeros(acc_shape, F32)

        acc_ref[...] += lax.dot_general(a_ref[...].astype(BF16), b_ref[...].astype(BF16), dims,
                                        preferred_element_type=F32)

        @pl.when(k == nk - 1)
        def _():
            o_ref[...] = acc_ref[...].astype(o_ref.dtype)

        @pl.when(last)
        def _():
            ex.wait(ex_ins, ex_outs, sems)

    sem = ("arbitrary",) * len(grid) if ex.n else ("parallel",) * kax + ("arbitrary",)
    res = pl.pallas_call(
        body, name=name, grid=grid, in_specs=[a_spec, b_spec] + ex.specs, out_specs=[out_spec] + ex.specs,
        out_shape=[out_shape] + ex.out_shape, scratch_shapes=[pltpu.VMEM(acc_shape, F32)] + ex.scratch,
        compiler_params=_cp(sem))(a, b, *ex.arrs)
    return (res[0], list(res[1:])) if ex.n else res[0]


def _mm_nn(name, a, w, out_dtype):
    m, k = a.shape
    n = w.shape[1]
    tm, tn, tk = _tile(m, 512), _tile(n, 1024), _tile(k, 1024)
    return _mm(name, a, w, (m // tm, n // tn, k // tk),
               pl.BlockSpec((tm, tk), lambda i, j, kk: (i, kk)), pl.BlockSpec((tk, tn), lambda i, j, kk: (kk, j)),
               pl.BlockSpec((tm, tn), lambda i, j, kk: (i, j)), jax.ShapeDtypeStruct((m, n), out_dtype), _NN, (tm, tn))


def _mm_nt(name, a, w, out_dtype):
    m, n = a.shape
    k = w.shape[0]
    tm, tko, tn = _tile(m, 512), _tile(k, 1024), _tile(n, 1024)
    return _mm(name, a, w, (m // tm, k // tko, n // tn),
               pl.BlockSpec((tm, tn), lambda i, j, kk: (i, kk)), pl.BlockSpec((tko, tn), lambda i, j, kk: (j, kk)),
               pl.BlockSpec((tm, tko), lambda i, j, kk: (i, j)), jax.ShapeDtypeStruct((m, k), out_dtype), _NT, (tm, tko))


def _mm_tn(name, a, b, out_dtype):
    t, m = a.shape
    n = b.shape[1]
    tm, tn, tk = _tile(m, 512), _tile(n, 1024), _tile(t, 1024)
    return _mm(name, a, b, (m // tm, n // tn, t // tk),
               pl.BlockSpec((tk, tm), lambda i, j, kk: (kk, i)), pl.BlockSpec((tk, tn), lambda i, j, kk: (kk, j)),
               pl.BlockSpec((tm, tn), lambda i, j, kk: (i, j)), jax.ShapeDtypeStruct((m, n), out_dtype), _TN, (tm, tn))


def _norm_mod_fwd(name, x, g, shift, scale, nb):
    t, d = x.shape
    s = t // nb
    tr = _tile(s, 512)
    nt = s // tr

    def body(x_ref, g_ref, sh_ref, sc_ref, h_ref):
        xv = x_ref[...]
        r = lax.rsqrt(jnp.mean(xv * xv, axis=-1, keepdims=True) + EPS)
        y = xv * r * g_ref[...]
        h_ref[...] = (y * (1.0 + sc_ref[...]) + sh_ref[...]).astype(h_ref.dtype)

    row = pl.BlockSpec((tr, d), lambda b, i: (b * nt + i, 0))
    vec = pl.BlockSpec((None, 1, d), lambda b, i: (b, 0, 0))
    return pl.pallas_call(body, name=name, grid=(nb, nt),
                          in_specs=[row, pl.BlockSpec((1, d), lambda b, i: (0, 0)), vec, vec],
                          out_specs=row, out_shape=jax.ShapeDtypeStruct((t, d), BF16),
                          compiler_params=_cp(("parallel", "parallel")))(x, g, shift, scale)


def _norm_mod_bwd(name, dh, x, g, shift, scale, dres, target, nb):
    t, d = x.shape
    s = t // nb
    tr = _tile(s, 256)
    nt = s // tr
    final = target is not None

    def body(*refs):
        if final:
            x_ref, g_ref, sh_ref, sc_ref, tg_ref, dx_ref, dg_ref, dsh_ref, dsc_ref, loss_ref = refs
        else:
            dh_ref, x_ref, g_ref, sc_ref, dres_ref, dx_ref, dg_ref, dsh_ref, dsc_ref = refs
        b, i = pl.program_id(0), pl.program_id(1)
        xv = x_ref[...]
        gv = g_ref[...]
        r = lax.rsqrt(jnp.mean(xv * xv, axis=-1, keepdims=True) + EPS)
        nrm = xv * r
        y = nrm * gv
        one_sc = 1.0 + sc_ref[...]
        if final:
            err = y * one_sc + sh_ref[...] - tg_ref[...]
            dhv = err * (1.0 / d)
        else:
            dhv = dh_ref[...].astype(F32)
        dy = dhv * one_sc
        dn = dy * gv
        dxv = r * (dn - nrm * jnp.mean(dn * nrm, axis=-1, keepdims=True))
        if final:
            dx_ref[...] = dxv
        else:
            dx_ref[...] = dres_ref[...] + dxv

        @pl.when(i == 0)
        def _():
            dsh_ref[...] = jnp.zeros_like(dsh_ref)
            dsc_ref[...] = jnp.zeros_like(dsc_ref)

        @pl.when((i == 0) & (b == 0))
        def _():
            dg_ref[...] = jnp.zeros_like(dg_ref)
            if final:
                loss_ref[...] = jnp.zeros_like(loss_ref)

        dsh_ref[...] += jnp.sum(dhv, axis=0, keepdims=True)
        dsc_ref[...] += jnp.sum(dhv * y, axis=0, keepdims=True)
        dg_ref[...] += jnp.sum(dy * nrm, axis=0, keepdims=True)
        if final:
            loss_ref[...] += (0.5 / d) * jnp.sum(err * err)

    row = pl.BlockSpec((tr, d), lambda b, i: (b * nt + i, 0))
    vec = pl.BlockSpec((None, 1, d), lambda b, i: (b, 0, 0))
    gsp = pl.BlockSpec((1, d), lambda b, i: (0, 0))
    out_specs = [row, gsp, vec, vec]
    out_shape = [jax.ShapeDtypeStruct((t, d), F32), jax.ShapeDtypeStruct((1, d), F32),
                 jax.ShapeDtypeStruct((nb, 1, d), F32), jax.ShapeDtypeStruct((nb, 1, d), F32)]
    if final:
        ins, in_specs = [x, g, shift, scale, target], [row, gsp, vec, vec, row]
        out_specs.append(pl.BlockSpec((8, LANE), lambda b, i: (0, 0)))
        out_shape.append(jax.ShapeDtypeStruct((8, LANE), F32))
    else:
        ins, in_specs = [dh, x, g, scale, dres], [row, row, gsp, vec, row]
    return pl.pallas_call(body, name=name, grid=(nb, nt), in_specs=in_specs, out_specs=out_specs,
                          out_shape=out_shape, compiler_params=_cp(("arbitrary", "arbitrary")))(*ins)


def _gate_add(name, x, y, gate, nb):
    t, d = x.shape
    s = t // nb
    tr = _tile(s, 512)
    nt = s // tr

    def body(x_ref, y_ref, g_ref, o_ref):
        o_ref[...] = x_ref[...] + g_ref[...] * y_ref[...]

    row = pl.BlockSpec((tr, d), lambda b, i: (b * nt + i, 0))
    vec = pl.BlockSpec((None, 1, d), lambda b, i: (b, 0, 0))
    return pl.pallas_call(body, name=name, grid=(nb, nt), in_specs=[row, row, vec], out_specs=row,
                          out_shape=jax.ShapeDtypeStruct((t, d), F32),
                          compiler_params=_cp(("parallel", "parallel")))(x, y, gate)


def _gate_bwd(name, dx, y, gate, nb):
    t, d = dx.shape
    s = t // nb
    tr = _tile(s, 512)
    nt = s // tr

    def body(dx_ref, y_ref, g_ref, dy_ref, dg_ref):
        dxv = dx_ref[...]
        dy_ref[...] = (g_ref[...] * dxv).astype(dy_ref.dtype)

        @pl.when(pl.program_id(1) == 0)
        def _():
            dg_ref[...] = jnp.zeros_like(dg_ref)

        dg_ref[...] += jnp.sum(dxv * y_ref[...], axis=0, keepdims=True)

    row = pl.BlockSpec((tr, d), lambda b, i: (b * nt + i, 0))
    vec = pl.BlockSpec((None, 1, d), lambda b, i: (b, 0, 0))
    return pl.pallas_call(body, name=name, grid=(nb, nt), in_specs=[row, row, vec], out_specs=[row, vec],
                          out_shape=[jax.ShapeDtypeStruct((t, d), BF16), jax.ShapeDtypeStruct((nb, 1, d), F32)],
                          compiler_params=_cp(("parallel", "arbitrary")))(dx, y, gate)


def _log_sigmoid(z):
    return jnp.minimum(z, 0.0) - jnp.log(1.0 + jnp.exp(-jnp.abs(z)))


def _split_dot(v, tri):
    hi = v.astype(BF16)
    lo = (v - hi.astype(F32)).astype(BF16)
    return (jnp.dot(hi, tri, preferred_element_type=F32) + jnp.dot(lo, tri, preferred_element_type=F32))


def _grid_ends(grid):
    ids = [pl.program_id(a) for a in range(len(grid))]
    first = functools.reduce(lambda u, w: u & w, [i == 0 for i in ids])
    last = functools.reduce(lambda u, w: u & w, [i == n - 1 for i, n in zip(ids, grid)])
    return first, last


def _attn_trips(nq):
    return nq * (nq + 1) // 2


def _next_trip(qi, jj, nq):
    wrap = jj >= qi
    nqi = jnp.where(wrap, jnp.minimum(qi + 1, nq - 1), qi)
    njj = jnp.where(wrap, jnp.where(qi + 1 < nq, 0, jj), jj + 1)
    return nqi, njj


def _attn_fwd(qkv, nb, s, d, ex):
    t = nb * s
    npair = d // LANE
    bk = ATT_BLK
    bq = min(ATT_BQ, s)
    nq = s // bq
    kpq = bq // bk
    nheads = LANE // HEAD_DIM
    scale = HEAD_DIM ** -0.5
    grid = (nb, npair)
    assert s // bk <= HEAD_DIM, "one carry lane per key block and head"
    assert bk == LANE, "the running sums are kept one 128-lane tile wide"
    assert kpq == ATT_UNROLL, "query block qi has exactly qi + 1 trips"

    def body(*refs):
        q_ref, k_ref, v_ref = refs[:3]
        ex_ins = refs[3:3 + ex.n]
        o_ref, car_ref = refs[3 + ex.n:5 + ex.n]
        ex_outs = refs[5 + ex.n:5 + 2 * ex.n]
        acc_s, run_s, z_s, arg_s = refs[5 + 2 * ex.n:9 + 2 * ex.n]
        sems = refs[9 + 2 * ex.n:]
        first, last = _grid_ends(grid)

        @pl.when(first)
        def _():
            ex.start(ex_ins, ex_outs, sems)

        lane = lax.broadcasted_iota(jnp.int32, (1, LANE), 1)
        row = lax.broadcasted_iota(jnp.int32, (bq, bk), 0)
        col = lax.broadcasted_iota(jnp.int32, (bq, bk), 1)
        trow = lax.broadcasted_iota(jnp.int32, (bk, bk), 0)
        tcol = lax.broadcasted_iota(jnp.int32, (bk, bk), 1)
        tri = (trow > tcol).astype(BF16)
        hms = [(lane // HEAD_DIM) == hh for hh in range(nheads)]

        def q0_of(qi):
            return pl.multiple_of(qi * bq, bq)

        def kblk_of(qi, jj, u):
            return (qi + 1) * kpq - 1 - (ATT_UNROLL * jj + u)

        def scores(qi, jj):
            q = q_ref[pl.ds(q0_of(qi), bq), :]
            qhs = [jnp.where(hm, q, jnp.zeros_like(q)) * scale for hm in hms]
            ks = [k_ref[pl.ds(pl.multiple_of(kblk_of(qi, jj, u) * bk, bk), bk), :] for u in range(ATT_UNROLL)]
            return [[lax.dot_general(qhs[hh], kj, _NT, preferred_element_type=F32) for kj in ks]
                    for hh in range(nheads)]

        def keep(zn):
            for hh in range(nheads):
                for u in range(ATT_UNROLL):
                    z_s[hh, u] = zn[hh][u]

        def exponents(qi, jj):
            q0 = q0_of(qi)
            car = car_ref[pl.ds(q0, bq), :]
            for hh in range(nheads):
                run = jnp.where(jj == 0, 0.0, run_s[hh])
                for u in range(ATT_UNROLL):
                    j = kblk_of(qi, jj, u)
                    mask = (j * bk + col) < (q0 + row)
                    z = z_s[hh, u]
                    lb = _log_sigmoid(z)
                    l1 = jnp.where(mask, lb - z, 0.0)
                    arg_s[hh, u] = jnp.where(mask, lb + (_split_dot(l1, tri) + run), -1e30)
                    car = jnp.where(lane == hh * HEAD_DIM + j, run, car)
                    run = run + jnp.sum(l1, axis=1, keepdims=True)
                run_s[hh] = run
            car_ref[pl.ds(q0, bq), :] = car

        def weigh(qi, jj):
            q0 = q0_of(qi)
            for hh in range(nheads):
                acc = None
                for u in range(ATT_UNROLL):
                    vj = v_ref[pl.ds(pl.multiple_of(kblk_of(qi, jj, u) * bk, bk), bk), :]
                    pv = jnp.dot(jnp.exp(arg_s[hh, u]).astype(BF16), vj, preferred_element_type=F32)
                    acc = pv if acc is None else acc + pv
                acc_s[hh, pl.ds(q0, bq), :] += acc

        def step(n, carry):
            qi, jj, pqi, pjj = carry
            nqi, njj = _next_trip(qi, jj, nq)
            zn = scores(nqi, njj)
            weigh(pqi, pjj)
            exponents(qi, jj)
            keep(zn)
            return nqi, njj, qi, jj

        acc_s[...] = jnp.zeros_like(acc_s)
        run_s[...] = jnp.zeros_like(run_s)
        car_ref[...] = jnp.zeros_like(car_ref)
        arg_s[...] = jnp.full(arg_s.shape, -1e30, F32)
        zero = jnp.int32(0)
        keep(scores(zero, zero))
        _, _, lqi, ljj = lax.fori_loop(0, _attn_trips(nq), step, (zero, zero, zero, zero))
        weigh(lqi, ljj)
        out = acc_s[0]
        for hh in range(1, nheads):
            out = jnp.where(hms[hh], acc_s[hh], out)
        o_ref[...] = out.astype(o_ref.dtype)

        @pl.when(last)
        def _():
            ex.wait(ex_ins, ex_outs, sems)

    seq = lambda off: pl.BlockSpec((s, LANE), lambda b, p: (b, off + p))
    res = pl.pallas_call(
        body, name="attn_fwd", grid=grid,
        in_specs=[seq(0), seq(npair), seq(2 * npair)] + ex.specs,
        out_specs=[seq(0), seq(0)] + ex.specs,
        out_shape=[jax.ShapeDtypeStruct((t, d), BF16), jax.ShapeDtypeStruct((t, d), F32)] + ex.out_shape,
        scratch_shapes=[pltpu.VMEM((nheads, s, LANE), F32), pltpu.VMEM((nheads, bq, LANE), F32),
                        pltpu.VMEM((nheads, ATT_UNROLL, bq, bk), F32),
                        pltpu.VMEM((nheads, ATT_UNROLL, bq, bk), F32)] + ex.scratch,
        compiler_params=_cp(("arbitrary", "arbitrary")))(qkv, qkv, qkv, *ex.arrs)
    return res[0], res[1], list(res[2:])


def _attn_bwd(qkv, car, do, nb, s, d, ex):
    t = nb * s
    npair = d // LANE
    bk = ATT_BLK
    bq = min(ATT_BQ, s)
    nq = s // bq
    kpq = bq // bk
    nheads = LANE // HEAD_DIM
    scale = HEAD_DIM ** -0.5
    grid = (nb, npair)
    assert kpq == ATT_UNROLL, "query block qi has exactly qi + 1 trips"

    def body(*refs):
        q_ref, k_ref, v_ref, car_ref, do_ref = refs[:5]
        ex_ins = refs[5:5 + ex.n]
        dq_ref, dk_ref, dv_ref = refs[5 + ex.n:8 + ex.n]
        ex_outs = refs[8 + ex.n:8 + 2 * ex.n]
        dk_acc, dv_acc, dq_s, rune_s, z_s, da_s, dz_s, a_s = refs[8 + 2 * ex.n:16 + 2 * ex.n]
        sems = refs[16 + 2 * ex.n:]
        first, last = _grid_ends(grid)

        @pl.when(first)
        def _():
            ex.start(ex_ins, ex_outs, sems)

        lane = lax.broadcasted_iota(jnp.int32, (1, LANE), 1)
        row = lax.broadcasted_iota(jnp.int32, (bq, bk), 0)
        col = lax.broadcasted_iota(jnp.int32, (bq, bk), 1)
        trow = lax.broadcasted_iota(jnp.int32, (bk, bk), 0)
        tcol = lax.broadcasted_iota(jnp.int32, (bk, bk), 1)
        tri_suf = (trow > tcol).astype(BF16)
        tri_pre = (trow < tcol).astype(BF16)
        hms = [(lane // HEAD_DIM) == hh for hh in range(nheads)]

        def q0_of(qi):
            return pl.multiple_of(qi * bq, bq)

        def k0_of(jj, u):
            return pl.multiple_of((ATT_UNROLL * jj + u) * bk, bk)

        def heads_of(ref, qi, factor):
            x = ref[pl.ds(q0_of(qi), bq), :]
            return [jnp.where(hm, x, jnp.zeros_like(x)) * factor for hm in hms]

        def products(qi, jj):
            qhs, dohs = heads_of(q_ref, qi, scale), heads_of(do_ref, qi, 1.0)
            ks = [k_ref[pl.ds(k0_of(jj, u), bk), :] for u in range(ATT_UNROLL)]
            vs = [v_ref[pl.ds(k0_of(jj, u), bk), :] for u in range(ATT_UNROLL)]
            zn = [[lax.dot_general(qhs[hh], kj, _NT, preferred_element_type=F32) for kj in ks] for hh in range(nheads)]
            dn = [[lax.dot_general(dohs[hh], vj, _NT, preferred_element_type=F32) for vj in vs] for hh in range(nheads)]
            return zn, dn

        def keep(zn, dn):
            for hh in range(nheads):
                for u in range(ATT_UNROLL):
                    z_s[hh, u] = zn[hh][u]
                    da_s[hh, u] = dn[hh][u]

        def middle(qi, jj):
            q0 = q0_of(qi)
            car = car_ref[pl.ds(q0, bq), :]
            for hh in range(nheads):
                run_e = jnp.where(jj == 0, 0.0, rune_s[hh])
                for u in range(ATT_UNROLL):
                    j = ATT_UNROLL * jj + u
                    mask = (j * bk + col) < (q0 + row)
                    z = z_s[hh, u]
                    lb = _log_sigmoid(z)
                    l1u = lb - z
                    l1 = jnp.where(mask, l1u, 0.0)
                    run = jnp.sum(jnp.where(lane == hh * HEAD_DIM + j, car, 0.0), axis=1, keepdims=True)
                    a = jnp.where(mask, jnp.exp(lb + (_split_dot(l1, tri_suf) + run)), 0.0)
                    e = da_s[hh, u] * a
                    dz = e * jnp.exp(l1u) - (_split_dot(e, tri_pre) + run_e) * jnp.exp(lb)
                    dz_s[hh, u] = jnp.where(mask, dz, 0.0).astype(BF16)
                    a_s[hh, u] = a.astype(BF16)
                    run_e = run_e + jnp.sum(e, axis=1, keepdims=True)
                rune_s[hh] = run_e

        def grads(qi, jj):
            q0 = q0_of(qi)
            qhs, dohs = heads_of(q_ref, qi, scale), heads_of(do_ref, qi, 1.0)
            dqs = [None] * nheads
            for u in range(ATT_UNROLL):
                k0 = k0_of(jj, u)
                kj = k_ref[pl.ds(k0, bk), :]
                for hh in range(nheads):
                    dzb = dz_s[hh, u]
                    dqu = jnp.dot(dzb, kj, preferred_element_type=F32)
                    dqs[hh] = dqu if dqs[hh] is None else dqs[hh] + dqu
                    dkh = lax.dot_general(dzb, qhs[hh], _TN, preferred_element_type=F32)
                    dvh = lax.dot_general(a_s[hh, u], dohs[hh], _TN, preferred_element_type=F32)
                    dk_blk = dkh if hh == 0 else dk_blk + dkh
                    dv_blk = dvh if hh == 0 else dv_blk + dvh
                dk_acc[pl.ds(k0, bk), :] += dk_blk
                dv_acc[pl.ds(k0, bk), :] += dv_blk
            for hh in range(nheads):
                dq_s[hh, pl.ds(q0, bq), :] += dqs[hh]

        def step(n, carry):
            qi, jj, pqi, pjj = carry
            nqi, njj = _next_trip(qi, jj, nq)
            zn, dn = products(nqi, njj)
            grads(pqi, pjj)
            middle(qi, jj)
            keep(zn, dn)
            return nqi, njj, qi, jj

        dk_acc[...] = jnp.zeros_like(dk_acc)
        dv_acc[...] = jnp.zeros_like(dv_acc)
        dq_s[...] = jnp.zeros_like(dq_s)
        rune_s[...] = jnp.zeros_like(rune_s)
        dz_s[...] = jnp.zeros_like(dz_s)
        a_s[...] = jnp.zeros_like(a_s)
        zero = jnp.int32(0)
        keep(*products(zero, zero))
        _, _, lqi, ljj = lax.fori_loop(0, _attn_trips(nq), step, (zero, zero, zero, zero))
        grads(lqi, ljj)
        dq_out = dq_s[0]
        for hh in range(1, nheads):
            dq_out = jnp.where(hms[hh], dq_s[hh], dq_out)
        dq_ref[...] = (dq_out * scale).astype(dq_ref.dtype)
        dk_ref[...] = dk_acc[...].astype(dk_ref.dtype)
        dv_ref[...] = dv_acc[...].astype(dv_ref.dtype)

        @pl.when(last)
        def _():
            ex.wait(ex_ins, ex_outs, sems)

    seq = lambda off: pl.BlockSpec((s, LANE), lambda b, p: (b, off + p))
    sds = jax.ShapeDtypeStruct((t, d), BF16)
    res = pl.pallas_call(
        body, name="attn_bwd", grid=grid,
        in_specs=[seq(0), seq(npair), seq(2 * npair), seq(0), seq(0)] + ex.specs,
        out_specs=[seq(0), seq(0), seq(0)] + ex.specs, out_shape=[sds, sds, sds] + ex.out_shape,
        scratch_shapes=[pltpu.VMEM((s, LANE), F32), pltpu.VMEM((s, LANE), F32),
                        pltpu.VMEM((nheads, s, LANE), F32), pltpu.VMEM((nheads, bq, LANE), F32),
                        pltpu.VMEM((nheads, ATT_UNROLL, bq, bk), F32), pltpu.VMEM((nheads, ATT_UNROLL, bq, bk), F32),
                        pltpu.VMEM((nheads, ATT_UNROLL, bq, bk), BF16),
                        pltpu.VMEM((nheads, ATT_UNROLL, bq, bk), BF16)] + ex.scratch,
        compiler_params=_cp(("arbitrary", "arbitrary")))(qkv, qkv, qkv, car, do, *ex.arrs)
    return res[0], res[1], res[2], list(res[3:])


def _conv3(u_ref, w, bias, c, r0, rc):
    x = u_ref[pl.ds(r0, rc), :].astype(F32)
    p0 = pl.multiple_of(jnp.maximum(r0 - 16, 0), 16)
    prev = u_ref[pl.ds(p0, 16), :].astype(F32)
    prev = jnp.where(c > 0, prev, 0.0)
    row = lax.broadcasted_iota(jnp.int32, (rc, 1), 0)
    s1 = jnp.where(row == 0, prev[15:16, :], pltpu.roll(x, 1, 0))
    s2 = jnp.where(row == 0, prev[14:15, :], jnp.where(row == 1, prev[15:16, :], pltpu.roll(x, 2, 0)))
    cv = w[2:3, :] * x + w[1:2, :] * s1 + w[0:1, :] * s2 + bias
    return cv, x, s1, s2


def _sigmoid(x):
    return 1.0 / (1.0 + jnp.exp(-x))


def _ffn_act_fwd(name, up8, cw8, cb8, nb, s):
    _, t, c_w = up8.shape
    rc = _tile(s, 256)
    nch = s // rc
    half = N_DEV // 2

    def body(ug_ref, uv_ref, wg_ref, wv_ref, bg_ref, bv_ref, act_ref):
        wg, wv, bg, bv = wg_ref[...], wv_ref[...], bg_ref[...], bv_ref[...]

        def chunk(c, carry):
            r0 = pl.multiple_of(c * rc, rc)
            cg = _conv3(ug_ref, wg, bg, c, r0, rc)[0]
            cv = _conv3(uv_ref, wv, bv, c, r0, rc)[0]
            act_ref[pl.ds(r0, rc), :] = (cg * _sigmoid(cg) * cv).astype(act_ref.dtype)
            return carry

        lax.fori_loop(0, nch, chunk, 0)

    def slab(off):
        return pl.BlockSpec((None, s, c_w), lambda k, b: (k + off, b, 0))

    def par(rows, off):
        return pl.BlockSpec((None, rows, c_w), lambda k, b: (k + off, 0, 0))

    return pl.pallas_call(
        body, name=name, grid=(half, nb),
        in_specs=[slab(0), slab(half), par(3, 0), par(3, half), par(1, 0), par(1, half)],
        out_specs=pl.BlockSpec((None, s, c_w), lambda k, b: (k, b, 0)),
        out_shape=jax.ShapeDtypeStruct((half, t, c_w), BF16),
        compiler_params=_cp(("parallel", "parallel")))(up8, up8, cw8, cw8, cb8, cb8)


def _ffn_act_bwd(name, up8, dact4, cw8, cb8, nb, s):
    _, t, c_w = up8.shape
    rc = _tile(s, 256)
    nch = s // rc
    half = N_DEV // 2

    def body(u_ref, da_ref, w_ref, b_ref, dup_ref, dcw_ref, dcb_ref):
        w2, b2 = w_ref[...], b_ref[...]
        row = lax.broadcasted_iota(jnp.int32, (rc, 1), 0)

        @pl.when(pl.program_id(1) == 0)
        def _():
            dcw_ref[...] = jnp.zeros_like(dcw_ref)
            dcb_ref[...] = jnp.zeros_like(dcb_ref)

        def chunk(i, carry):
            c = nch - 1 - i
            r0 = pl.multiple_of(c * rc, rc)
            convs = [_conv3(u_ref.at[h], w2[h], b2[h], c, r0, rc) for h in range(2)]
            gt, vl = convs[0][0], convs[1][0]
            da = da_ref[pl.ds(r0, rc), :].astype(F32)
            sg = _sigmoid(gt)
            dcvs = [da * vl * sg * (1.0 + gt * (1.0 - sg)), da * gt * sg]
            out = []
            for h in range(2):
                n0, n1, a0, a1, a2, ab = carry[6 * h:6 * h + 6]
                dcv, (_, x, s1, s2), w = dcvs[h], convs[h], w2[h]
                t1 = jnp.where(row == rc - 1, n0, pltpu.roll(dcv, rc - 1, 0))
                t2 = jnp.where(row == rc - 2, n0, jnp.where(row == rc - 1, n1, pltpu.roll(dcv, rc - 2, 0)))
                dup = w[2:3, :] * dcv + w[1:2, :] * t1 + w[0:1, :] * t2
                dup_ref[h, pl.ds(r0, rc), :] = dup.astype(dup_ref.dtype)
                out += [dcv[0:1, :], dcv[1:2, :],
                        a0 + jnp.sum(dcv * s2, axis=0, keepdims=True), a1 + jnp.sum(dcv * s1, axis=0, keepdims=True),
                        a2 + jnp.sum(dcv * x, axis=0, keepdims=True), ab + jnp.sum(dcv, axis=0, keepdims=True)]
            return tuple(out)

        z = jnp.zeros((1, c_w), F32)
        fin = lax.fori_loop(0, nch, chunk, (z,) * 12)
        for h in range(2):
            _, _, a0, a1, a2, ab = fin[6 * h:6 * h + 6]
            dcw_ref[h, 0:1, :] += a0
            dcw_ref[h, 1:2, :] += a1
            dcw_ref[h, 2:3, :] += a2
            dcb_ref[h] += ab

    def pair(rows, per_seq):
        return pl.BlockSpec((2, None, rows, c_w), (lambda k, b: (0, k, b, 0)) if per_seq else (lambda k, b: (0, k, 0, 0)))

    four = lambda a: a.reshape((2, half) + a.shape[1:])
    dup, dcw, dcb = pl.pallas_call(
        body, name=name, grid=(half, nb),
        in_specs=[pair(s, True), pl.BlockSpec((None, s, c_w), lambda k, b: (k, b, 0)), pair(3, False), pair(1, False)],
        out_specs=[pair(s, True), pair(3, False), pair(1, False)],
        out_shape=[jax.ShapeDtypeStruct((2, half, t, c_w), BF16), jax.ShapeDtypeStruct((2, half, 3, c_w), F32),
                   jax.ShapeDtypeStruct((2, half, 1, c_w), F32)],
        compiler_params=_cp(("parallel", "arbitrary")))(four(up8), dact4, four(cw8), four(cb8))
    return dup.reshape(N_DEV, t, c_w), dcw.reshape(N_DEV, 3, c_w), dcb.reshape(N_DEV, 1, c_w)


_GELU_C0 = math.sqrt(2.0 / math.pi)
_GELU_C1 = 0.044715


def _rowwise(name, body, ins, in_kinds, out_kinds, t, d, tr_pref=512):
    tr = _tile(t, tr_pref)
    row = pl.BlockSpec((tr, d), lambda i: (i, 0))
    vec = pl.BlockSpec((1, d), lambda i: (0, 0))
    in_specs = [row if k == "row" else vec for k in in_kinds]
    out_specs = [row if k[0] == "row" else vec for k in out_kinds]
    out_shape = [jax.ShapeDtypeStruct((t, d) if k[0] == "row" else (1, d), k[1]) for k in out_kinds]
    has_acc = any(k[0] == "acc" for k in out_kinds)
    return pl.pallas_call(body, name=name, grid=(t // tr,), in_specs=in_specs, out_specs=out_specs,
                          out_shape=out_shape,
                          compiler_params=_cp(("arbitrary",) if has_acc else ("parallel",)))(*ins)


def _ssm_post_fwd(ys, u, dskip):
    t, d = ys.shape

    def body(ys_ref, u_ref, ds_ref, y_ref, z_ref):
        y = ys_ref[...].astype(F32) + ds_ref[...] * u_ref[...].astype(F32)
        y_ref[...] = y
        th = jnp.tanh(_GELU_C0 * (y + _GELU_C1 * y * y * y))
        z_ref[...] = (0.5 * y * (1.0 + th)).astype(z_ref.dtype)

    return _rowwise("ssm_post_fwd", body, [ys, u, dskip], ["row", "row", "vec"],
                    [("row", F32), ("row", BF16)], t, d)


def _glu_fwd(z, gl, bglu):
    t, d = z.shape

    def body(z_ref, gl_ref, b_ref, o_ref):
        o_ref[...] = (z_ref[...].astype(F32) * _sigmoid(gl_ref[...] + b_ref[...])).astype(o_ref.dtype)

    return _rowwise("glu_fwd", body, [z, gl, bglu], ["row", "row", "vec"], [("row", BF16)], t, d)[0]


def _glu_bwd(dgg, z, gl, bglu):
    t, d = z.shape

    def body(dg_ref, z_ref, gl_ref, b_ref, dgl_ref, dz_ref, db_ref):
        sg = _sigmoid(gl_ref[...] + b_ref[...])
        dg = dg_ref[...]
        dgl = dg * z_ref[...].astype(F32) * sg * (1.0 - sg)
        dgl_ref[...] = dgl.astype(dgl_ref.dtype)
        dz_ref[...] = dg * sg

        @pl.when(pl.program_id(0) == 0)
        def _():
            db_ref[...] = jnp.zeros_like(db_ref)

        db_ref[...] += jnp.sum(dgl, axis=0, keepdims=True)

    return _rowwise("glu_bwd", body, [dgg, z, gl, bglu], ["row", "row", "row", "vec"],
                    [("row", BF16), ("row", F32), ("acc", F32)], t, d)


def _ssm_post_bwd(dz1, dz2, y, u, dskip):
    t, d = y.shape

    def body(a_ref, b_ref, y_ref, u_ref, ds_ref, dy_ref, du_ref, dd_ref):
        yv = y_ref[...]
        inner = _GELU_C0 * (yv + _GELU_C1 * yv * yv * yv)
        th = jnp.tanh(inner)
        dgelu = 0.5 * (1.0 + th) + 0.5 * yv * (1.0 - th * th) * _GELU_C0 * (1.0 + 3.0 * _GELU_C1 * yv * yv)
        dy = (a_ref[...] + b_ref[...]) * dgelu
        dy_ref[...] = dy.astype(dy_ref.dtype)
        du_ref[...] = dy * ds_ref[...]

        @pl.when(pl.program_id(0) == 0)
        def _():
            dd_ref[...] = jnp.zeros_like(dd_ref)

        dd_ref[...] += jnp.sum(dy * u_ref[...].astype(F32), axis=0, keepdims=True)

    return _rowwise("ssm_post_bwd", body, [dz1, dz2, y, u, dskip], ["row", "row", "row", "row", "vec"],
                    [("row", BF16), ("row", F32), ("acc", F32)], t, d)


def _add_cast(a, b):
    t, d = a.shape

    def body(a_ref, b_ref, o_ref):
        o_ref[...] = (a_ref[...].astype(F32) + b_ref[...].astype(F32)).astype(o_ref.dtype)

    return _rowwise("add_cast", body, [a, b], ["row", "row"], [("row", BF16)], t, d)[0]


def _ssm_scan(e_re, e_im, lam_re, lam_im, nb):
    r, n = e_re.shape
    nc = r // nb
    cb = _tile(n, 512)

    def body(er_ref, ei_ref, lr_ref, li_ref, xr_ref, xi_ref):
        lr, li = lr_ref[...], li_ref[...]
        rid = lax.broadcasted_iota(jnp.int32, (8, 1), 0)

        def tile(i, carry):
            out = []
            for b in range(nb):
                xr, xi = carry[2 * b:2 * b + 2]
                r0 = pl.multiple_of(b * nc + i * 8, 8)
                er, ei = er_ref[pl.ds(r0, 8), :], ei_ref[pl.ds(r0, 8), :]
                outr, outi = jnp.zeros((8, cb), F32), jnp.zeros((8, cb), F32)
                for j in range(8):
                    outr = jnp.where(rid == j, xr, outr)
                    outi = jnp.where(rid == j, xi, outi)
                    xr, xi = lr * xr - li * xi + er[j:j + 1, :], li * xr + lr * xi + ei[j:j + 1, :]
                xr_ref[pl.ds(r0, 8), :] = outr
                xi_ref[pl.ds(r0, 8), :] = outi
                out += [xr, xi]
            return tuple(out)

        lax.fori_loop(0, nc // 8, tile, (jnp.zeros((1, cb), F32),) * (2 * nb))

    mat = pl.BlockSpec((r, cb), lambda j: (0, j))
    vec = pl.BlockSpec((1, cb), lambda j: (0, j))
    sds = jax.ShapeDtypeStruct((r, n), F32)
    return pl.pallas_call(body, name="ssm_scan", grid=(n // cb,), in_specs=[mat, mat, vec, vec],
                          out_specs=[mat, mat], out_shape=[sds, sds],
                          compiler_params=_cp(("parallel",)))(e_re, e_im, lam_re, lam_im)


def _ssm_scan_bwd(dxp_re, dxp_im, lam_re, lam_im, nb):
    r, n = dxp_re.shape
    nc = r // nb
    cb = _tile(n, 512)

    def body(dr_ref, di_ref, lr_ref, li_ref, er_ref, ei_ref):
        lr, li = lr_ref[...], li_ref[...]
        rid = lax.broadcasted_iota(jnp.int32, (8, 1), 0)

        def tile(i, carry):
            out = []
            for b in range(nb):
                gr, gi = carry[2 * b:2 * b + 2]
                r0 = pl.multiple_of(b * nc + (nc // 8 - 1 - i) * 8, 8)
                dr, di = dr_ref[pl.ds(r0, 8), :], di_ref[pl.ds(r0, 8), :]
                outr, outi = jnp.zeros((8, cb), F32), jnp.zeros((8, cb), F32)
                for j in range(7, -1, -1):
                    outr = jnp.where(rid == j, gr, outr)
                    outi = jnp.where(rid == j, gi, outi)
                    gr, gi = dr[j:j + 1, :] + lr * gr + li * gi, di[j:j + 1, :] + lr * gi - li * gr
                er_ref[pl.ds(r0, 8), :] = outr
                ei_ref[pl.ds(r0, 8), :] = outi
                out += [gr, gi]
            return tuple(out)

        lax.fori_loop(0, nc // 8, tile, (jnp.zeros((1, cb), F32),) * (2 * nb))

    mat = pl.BlockSpec((r, cb), lambda j: (0, j))
    vec = pl.BlockSpec((1, cb), lambda j: (0, j))
    sds = jax.ShapeDtypeStruct((r, n), F32)
    return pl.pallas_call(body, name="ssm_scan_bwd", grid=(n // cb,), in_specs=[mat, mat, vec, vec],
                          out_specs=[mat, mat], out_shape=[sds, sds],
                          compiler_params=_cp(("parallel",)))(dxp_re, dxp_im, lam_re, lam_im)


def _ssm_compact(a_re, a_im, log_dt, b_re, b_im, c_re, c_im):
    g, p = a_re.shape
    h = b_re.shape[-1]
    ln = SSM_L
    sg = LANE // h
    na = g // sg
    hp = lax.Precision.HIGHEST
    lam = lax.complex(a_re, a_im)
    ldt = lam * jnp.exp(log_dt)[:, None]
    lam_bar = jnp.exp(ldt)
    bbar = ((lam_bar - 1.0) / lam)[..., None] * lax.complex(b_re, b_im)
    cm = lax.complex(c_re, c_im)
    steps = jnp.arange(ln + 1, dtype=F32)
    pw = jnp.exp(ldt[:, None, :] * steps[None, :, None])
    kd = jnp.einsum("ghp,gdp,gpk->gdhk", cm, pw[:, :ln], bbar, precision=hp).real

    def stacked(x, rows_per, cols_per):
        x = x.reshape(na, sg, ln, rows_per, cols_per).transpose(0, 2, 1, 3, 4).reshape(na, ln, sg * rows_per, cols_per)
        return jnp.pad(x, ((0, 0), (0, 0), (0, 0), (0, LANE - cols_per)))

    wxc = (pw[:, ln - 1 - jnp.arange(ln)][:, :, :, None] * bbar[:, None]).transpose(0, 1, 3, 2)
    cpc = (cm[:, None] * pw[:, 1:ln + 1][:, :, None, :]).transpose(0, 1, 3, 2)
    lam_l = pw[:, ln]
    return (stacked(kd.transpose(0, 1, 3, 2), h, h), stacked(wxc.real, h, p), stacked(wxc.imag, h, p),
            stacked(cpc.real, p, h), stacked(-cpc.imag, p, h),
            lam_l.real.reshape(1, g * p), lam_l.imag.reshape(1, g * p))


def _ssm_masks(h, p):
    sg = LANE // h
    r128 = lax.broadcasted_iota(jnp.int32, (LANE, LANE), 0)
    c128 = lax.broadcasted_iota(jnp.int32, (LANE, LANE), 1)
    rx = lax.broadcasted_iota(jnp.int32, (LANE, sg * p), 0)
    cx = lax.broadcasted_iota(jnp.int32, (LANE, sg * p), 1)
    ry = lax.broadcasted_iota(jnp.int32, (sg * p, LANE), 0)
    cy = lax.broadcasted_iota(jnp.int32, (sg * p, LANE), 1)
    f = lambda m: m.astype(F32)
    return dict(
        spread_h=f((r128 < h) & (c128 % h == r128)),
        spread_p=f((rx < p) & (cx % p == rx)),
        gather_h=f((c128 < h) & (r128 % h == c128)),
        gather_p=f((cy < p) & (ry % p == cy)),
        same_t=f(r128 // h == c128 // h), same_x=f(rx // h == cx // p), same_y=f(ry // p == cy // h))


def _hdot(a, b):
    return jnp.dot(a, b, preferred_element_type=F32, precision=lax.Precision.HIGHEST)


def _ssm_expand(compact):
    kt, wxr, wxi, wyr, wyi = compact
    na, ln = kt.shape[:2]
    wst = wyr.shape[2]
    h, p = SSM_H, SSM_P

    def body(kt_ref, wxr_ref, wxi_ref, wyr_ref, wyi_ref, tm_ref, xr_ref, xi_ref, yr_ref, yi_ref):
        m = _ssm_masks(h, p)
        ktb = [_hdot(kt_ref[lag], m["spread_h"]) * m["same_t"] for lag in range(ln)]
        zero = jnp.zeros((LANE, LANE), F32)
        for sig in range(ln):
            rows = slice(sig * LANE, (sig + 1) * LANE)
            tm_ref[rows, :] = jnp.concatenate([ktb[tau - sig] if tau >= sig else zero for tau in range(ln)],
                                              axis=1).astype(tm_ref.dtype)
            xr_ref[rows, :] = (_hdot(wxr_ref[sig], m["spread_p"]) * m["same_x"]).astype(xr_ref.dtype)
            xi_ref[rows, :] = (_hdot(wxi_ref[sig], m["spread_p"]) * m["same_x"]).astype(xi_ref.dtype)
        for tau in range(ln):
            cols = slice(tau * LANE, (tau + 1) * LANE)
            yr_ref[:, cols] = (_hdot(wyr_ref[tau], m["spread_h"]) * m["same_y"]).astype(yr_ref.dtype)
            yi_ref[:, cols] = (_hdot(wyi_ref[tau], m["spread_h"]) * m["same_y"]).astype(yi_ref.dtype)

    blk = lambda rows: pl.BlockSpec((None, ln, rows, LANE), lambda j: (j, 0, 0, 0))
    mat = lambda rows, cols: pl.BlockSpec((None, rows, cols), lambda j: (j, 0, 0))
    sds = lambda rows, cols: jax.ShapeDtypeStruct((na, rows, cols), BF16)
    wch = ln * LANE
    return pl.pallas_call(
        body, name="ssm_expand", grid=(na,), in_specs=[blk(LANE), blk(LANE), blk(LANE), blk(wst), blk(wst)],
        out_specs=[mat(wch, wch), mat(wch, wst), mat(wch, wst), mat(wst, wch), mat(wst, wch)],
        out_shape=[sds(wch, wch), sds(wch, wst), sds(wch, wst), sds(wst, wch), sds(wst, wch)],
        compiler_params=_cp(("parallel",)))(kt, wxr, wxi, wyr, wyi)


def _chunk_view(a):
    t, d = a.shape
    return a.reshape(t // SSM_L, SSM_L * d)


def _sg_specs(r4, d, wst):
    nblk = d // LANE
    cat = [pl.BlockSpec((r4, LANE), functools.partial(lambda j, tau: (0, tau * nblk + j), tau=tau))
           for tau in range(SSM_L)]
    plane = pl.BlockSpec((r4, wst), lambda j: (0, j))
    mat = lambda rows, cols: pl.BlockSpec((None, rows, cols), lambda j: (j, 0, 0))
    piece = pl.BlockSpec((r4, LANE), lambda j: (0, j))
    return cat, plane, mat, piece


def _lane_cat(refs):
    return jnp.concatenate([r[...] for r in refs], axis=1)


def _bdot(a, b, dims):
    return lax.dot_general(a.astype(BF16), b.astype(BF16), dims, preferred_element_type=F32)


def _ssm_core_fwd(u, ops, nb):
    tm, wxr, wxi, wyr, wyi, lam_re, lam_im = ops
    t, d = u.shape
    ln, na, wch, wst = SSM_L, tm.shape[0], tm.shape[1], wxr.shape[2]
    r4 = t // ln
    n = na * wst
    u4 = _chunk_view(u)
    cat, plane, mat, piece = _sg_specs(r4, d, wst)
    pds = jax.ShapeDtypeStruct((r4, n), F32)

    def states(*refs):
        ucat = _lane_cat(refs[:ln])
        wr_ref, wi_ref, er_ref, ei_ref = refs[ln:]
        er_ref[...] = _bdot(ucat, wr_ref[...], _NN)
        ei_ref[...] = _bdot(ucat, wi_ref[...], _NN)

    e_re, e_im = pl.pallas_call(
        states, name="ssm_states", grid=(na,), in_specs=cat + [mat(wch, wst)] * 2, out_specs=[plane, plane],
        out_shape=[pds, pds], compiler_params=_cp(("parallel",)))(*([u4] * ln), wxr, wxi)
    xp_re, xp_im = _ssm_scan(e_re, e_im, lam_re, lam_im, nb)

    def outputs(*refs):
        ucat = _lane_cat(refs[:ln])
        tm_ref, xr_ref, xi_ref, wr_ref, wi_ref = refs[ln:ln + 5]
        y = (_bdot(ucat, tm_ref[...], _NN) + _bdot(xr_ref[...], wr_ref[...], _NN)
             + _bdot(xi_ref[...], wi_ref[...], _NN))
        for tau, o_ref in enumerate(refs[ln + 5:]):
            o_ref[...] = y[:, tau * LANE:(tau + 1) * LANE].astype(o_ref.dtype)

    ys = pl.pallas_call(
        outputs, name="ssm_y", grid=(na,),
        in_specs=cat + [mat(wch, wch), plane, plane, mat(wst, wch), mat(wst, wch)], out_specs=[piece] * ln,
        out_shape=[jax.ShapeDtypeStruct((r4, d), BF16)] * ln,
        compiler_params=_cp(("parallel",)))(*([u4] * ln), tm, xp_re, xp_im, wyr, wyi)
    return jnp.concatenate(ys, axis=1).reshape(t, d), xp_re, xp_im


def _ssm_core_bwd(dy, u, xp_re, xp_im, ops, nb):
    tm, wxr, wxi, wyr, wyi, lam_re, lam_im = ops
    t, d = u.shape
    ln, na, wch, wst = SSM_L, tm.shape[0], tm.shape[1], wxr.shape[2]
    r4 = t // ln
    n = na * wst
    u4, dy4 = _chunk_view(u), _chunk_view(dy)
    cat, plane, mat, piece = _sg_specs(r4, d, wst)
    pds = jax.ShapeDtypeStruct((r4, n), F32)

    def dstates(*refs):
        dycat = _lane_cat(refs[:ln])
        wr_ref, wi_ref, dr_ref, di_ref = refs[ln:]
        dr_ref[...] = _bdot(dycat, wr_ref[...], _NT)
        di_ref[...] = _bdot(dycat, wi_ref[...], _NT)

    dxp_re, dxp_im = pl.pallas_call(
        dstates, name="ssm_dxp", grid=(na,), in_specs=cat + [mat(wst, wch)] * 2, out_specs=[plane, plane],
        out_shape=[pds, pds], compiler_params=_cp(("parallel",)))(*([dy4] * ln), wyr, wyi)
    de_re, de_im = _ssm_scan_bwd(dxp_re, dxp_im, lam_re, lam_im, nb)

    def dinputs(*refs):
        dycat = _lane_cat(refs[:ln])
        tm_ref, er_ref, ei_ref, wr_ref, wi_ref = refs[ln:ln + 5]
        du = (_bdot(dycat, tm_ref[...], _NT) + _bdot(er_ref[...], wr_ref[...], _NT)
              + _bdot(ei_ref[...], wi_ref[...], _NT))
        for tau, o_ref in enumerate(refs[ln + 5:]):
            o_ref[...] = du[:, tau * LANE:(tau + 1) * LANE].astype(o_ref.dtype)

    dus = pl.pallas_call(
        dinputs, name="ssm_du", grid=(na,),
        in_specs=cat + [mat(wch, wch), plane, plane, mat(wch, wst), mat(wch, wst)], out_specs=[piece] * ln,
        out_shape=[jax.ShapeDtypeStruct((r4, d), BF16)] * ln,
        compiler_params=_cp(("parallel",)))(*([dy4] * ln), tm, de_re, de_im, wxr, wxi)

    def doperators(*refs):
        ucat, dycat = _lane_cat(refs[:ln]), _lane_cat(refs[ln:2 * ln])
        (er_ref, ei_ref, xr_ref, xi_ref, dtm_ref, dwxr_ref, dwxi_ref, dwyr_ref, dwyi_ref,
         dlr_ref, dli_ref) = refs[2 * ln:]
        er, ei, xr, xi = er_ref[...], ei_ref[...], xr_ref[...], xi_ref[...]
        m = _ssm_masks(SSM_H, SSM_P)
        blk = lambda i: slice(i * LANE, (i + 1) * LANE)
        dtm = _bdot(ucat, dycat, _TN)
        for lag in range(ln):
            acc = dtm[blk(0), blk(lag)]
            for sig in range(1, ln - lag):
                acc = acc + dtm[blk(sig), blk(sig + lag)]
            dtm_ref[lag] = _hdot(acc * m["same_t"], m["gather_h"])
        for src, dst in ((er, dwxr_ref), (ei, dwxi_ref)):
            dwx = _bdot(ucat, src, _TN)
            for sig in range(ln):
                dst[sig] = _hdot(dwx[blk(sig), :] * m["same_x"], m["gather_p"])
        for src, dst in ((xr, dwyr_ref), (xi, dwyi_ref)):
            dwy = _bdot(src, dycat, _TN)
            for tau in range(ln):
                dst[tau] = _hdot(dwy[:, blk(tau)] * m["same_y"], m["gather_h"])
        dlr_ref[...] = jnp.sum(er * xr + ei * xi, axis=0, keepdims=True)
        dli_ref[...] = jnp.sum(ei * xr - er * xi, axis=0, keepdims=True)

    cblk = lambda rows: pl.BlockSpec((None, ln, rows, LANE), lambda j: (j, 0, 0, 0))
    cds = lambda rows: jax.ShapeDtypeStruct((na, ln, rows, LANE), F32)
    vec = pl.BlockSpec((1, wst), lambda j: (0, j))
    vds = jax.ShapeDtypeStruct((1, n), F32)
    d_compact = pl.pallas_call(
        doperators, name="ssm_dops", grid=(na,), in_specs=cat + cat + [plane] * 4,
        out_specs=[cblk(LANE), cblk(LANE), cblk(LANE), cblk(wst), cblk(wst), vec, vec],
        out_shape=[cds(LANE), cds(LANE), cds(LANE), cds(wst), cds(wst), vds, vds],
        compiler_params=_cp(("parallel",)))(*([u4] * ln), *([dy4] * ln), de_re, de_im, xp_re, xp_im)
    return jnp.concatenate(dus, axis=1).reshape(t, d), tuple(d_compact)


def _modfin_fwd(c_all, w_mod, w_fin):
    n, d = c_all.shape
    nl, _, cm = w_mod.shape
    cf = w_fin.shape[1]
    width = nl * cm + cf
    hp = lax.Precision.HIGHEST

    def body(c_ref, wm_ref, wf_ref, act_ref, out_ref):
        cv = c_ref[...]
        act = cv * _sigmoid(cv)
        act_ref[...] = act
        for i in range(nl):
            out_ref[:, i * cm:(i + 1) * cm] = jnp.dot(act, wm_ref[i], preferred_element_type=F32, precision=hp)
        out_ref[:, nl * cm:] = jnp.dot(act, wf_ref[...], preferred_element_type=F32, precision=hp)

    return pl.pallas_call(body, name="modfin_fwd",
                          out_shape=[jax.ShapeDtypeStruct((n, d), F32), jax.ShapeDtypeStruct((n, width), F32)],
                          compiler_params=_cp(None))(c_all, w_mod, w_fin)


def _modfin_bwd(c_act_t, dmod_loc, dfin_loc, dall):
    d, n = c_act_t.shape
    nl, _, cm = dmod_loc.shape
    cf = dfin_loc.shape[1]
    hp = lax.Precision.HIGHEST

    def body(ct_ref, dm_ref, df_ref, da_ref, gwm_ref, gwf_ref, gb_ref):
        ct = ct_ref[...]
        for i in range(nl):
            gwm_ref[i] = jnp.dot(ct, dm_ref[i], preferred_element_type=F32, precision=hp)
        gwf_ref[...] = jnp.dot(ct, df_ref[...], preferred_element_type=F32, precision=hp)
        gb_ref[...] = jnp.sum(da_ref[...], axis=0, keepdims=True)

    return pl.pallas_call(body, name="modfin_bwd",
                          out_shape=[jax.ShapeDtypeStruct((nl, d, cm), F32), jax.ShapeDtypeStruct((d, cf), F32),
                                     jax.ShapeDtypeStruct((1, dall.shape[1]), F32)],
                          compiler_params=_cp(None))(c_act_t, dmod_loc, dfin_loc, dall)


def _adamw(name, gparts, w, m, v):
    n, r, c = gparts.shape
    tr = _tile(r, 256)

    def body(gp_ref, w_ref, m_ref, v_ref, g_ref, d_ref, mo_ref, vo_ref):
        _adamw_step(gp_ref, w_ref, m_ref, v_ref, g_ref, d_ref, mo_ref, vo_ref)

    mat = pl.BlockSpec((tr, c), lambda i: (i, 0))
    sds = jax.ShapeDtypeStruct((r, c), F32)
    return pl.pallas_call(body, name=name, grid=(r // tr,),
                          in_specs=[pl.BlockSpec((n, tr, c), lambda i: (0, i, 0)), mat, mat, mat],
                          out_specs=[mat] * 4, out_shape=[sds] * 4,
                          compiler_params=_cp(("parallel",)))(gparts, w, m, v)


def _adamw_step(gp_ref, w_ref, m_ref, v_ref, g_ref, d_ref, mo_ref, vo_ref):
    gsum = gp_ref[0].astype(F32)
    for j in range(1, gp_ref.shape[0]):
        gsum = gsum + gp_ref[j].astype(F32)
    mn = ADAM_B1 * m_ref[...] + (1.0 - ADAM_B1) * gsum
    vn = ADAM_B2 * v_ref[...] + (1.0 - ADAM_B2) * (gsum * gsum)
    g_ref[...] = gsum
    mo_ref[...] = mn
    vo_ref[...] = vn
    m_hat = mn * (1.0 / (1.0 - ADAM_B1 ** ADAM_STEP))
    v_hat = vn * (1.0 / (1.0 - ADAM_B2 ** ADAM_STEP))
    d_ref[...] = -ADAM_LR * (m_hat / (jnp.sqrt(v_hat) + ADAM_EPS) + ADAM_WD * w_ref[...])


def _adamw_many(name, entries):
    k = len(entries)

    def body(*refs):
        for i in range(k):
            _adamw_step(*refs[4 * i:4 * i + 4], *refs[4 * k + 4 * i:4 * k + 4 * i + 4])

    ops = [a for e in entries for a in e]
    out_shape = [jax.ShapeDtypeStruct(e[1].shape, F32) for e in entries for _ in range(4)]
    return pl.pallas_call(body, name=name, out_shape=out_shape, compiler_params=_cp(None))(*ops)


class _Exchange:
    def __init__(self, arrs, gathers):
        self.arrs = [pltpu.with_memory_space_constraint(a, pltpu.HBM) for a in arrs]
        self.gathers = list(gathers)
        self.n = len(arrs)
        self.out_shape = [pltpu.HBM(((N_DEV,) + a.shape) if g else a.shape, a.dtype)
                          for a, g in zip(arrs, self.gathers)]
        self.specs = [pl.BlockSpec(memory_space=pltpu.HBM)] * self.n
        self.scratch = [pltpu.SemaphoreType.DMA((self.n, N_DEV - 1)), pltpu.SemaphoreType.DMA((self.n, N_DEV - 1)),
                        pltpu.SemaphoreType.DMA((self.n,))]

    def _copies(self, ins, outs, sems):
        send_sems, recv_sems, local_sems = sems
        x, y, c = lax.axis_index("x"), lax.axis_index("y"), lax.axis_index("c")
        me = 4 * x + 2 * y + c
        local, sends, recvs = [], [], []
        for i in range(self.n):
            src_me = ins[i] if self.gathers[i] else ins[i].at[me]
            local.append(pltpu.make_async_copy(src_me, outs[i].at[me], local_sems.at[i]))
        for dd in range(1, N_DEV):
            px = jnp.bitwise_xor(x, dd >> 2)
            py = jnp.bitwise_xor(y, (dd >> 1) & 1)
            pc = jnp.bitwise_xor(c, dd & 1)
            pid = 4 * px + 2 * py + pc
            for i in range(self.n):
                src = ins[i] if self.gathers[i] else ins[i].at[pid]
                sems_i = dict(send_sem=send_sems.at[i, dd - 1], recv_sem=recv_sems.at[i, dd - 1],
                              device_id=(px, py, pc), device_id_type=MESH)
                sends.append(pltpu.make_async_remote_copy(src_ref=src, dst_ref=outs[i].at[me], **sems_i))
                recvs.append(pltpu.make_async_remote_copy(src_ref=src, dst_ref=outs[i].at[pid], **sems_i))
        return local, sends, recvs

    def start(self, ins, outs, sems):
        local, sends, _ = self._copies(ins, outs, sems)
        for cp in local + sends:
            cp.start()

    def wait(self, ins, outs, sems):
        local, sends, recvs = self._copies(ins, outs, sems)
        for cp in recvs:
            cp.wait_recv()
        for cp in sends:
            cp.wait_send()
        for cp in local:
            cp.wait()


class _NoExchange:
    n, arrs, specs, out_shape, scratch = 0, [], [], [], []

    def start(self, ins, outs, sems):
        pass

    def wait(self, ins, outs, sems):
        pass


def _exchange(name, arrs, gathers):
    ex = _Exchange(arrs, gathers)
    n = ex.n

    def body(*refs):
        ins, outs, sems = refs[:n], refs[n:2 * n], refs[2 * n:]
        ex.start(ins, outs, sems)
        ex.wait(ins, outs, sems)

    outs = pl.pallas_call(body, name=name, in_specs=ex.specs, out_specs=ex.specs, out_shape=ex.out_shape,
                          scratch_shapes=ex.scratch)(*ex.arrs)
    return list(outs)


def kernel(x, c, norm_mix, norm_ffn, w_mod, b_mod, w_qkv, w_o_attn, w_in_ssm, a_re, a_im, log_dt, b_re, b_im, c_re, c_im, d_skip, w_glu, b_glu, w_o_ssm, w_up, conv_w, conv_b, w_down, norm_out, w_fin, b_fin, loss_target, m_norm_mix, m_norm_ffn, m_w_mod, m_b_mod, m_w_qkv, m_w_o_attn, m_w_in_ssm, m_a_re, m_a_im, m_log_dt, m_b_re, m_b_im, m_c_re, m_c_im, m_d_skip, m_w_glu, m_b_glu, m_w_o_ssm, m_w_up, m_conv_w, m_conv_b, m_w_down, m_norm_out, m_w_fin, m_b_fin, v_norm_mix, v_norm_ffn, v_w_mod, v_b_mod, v_w_qkv, v_w_o_attn, v_w_in_ssm, v_a_re, v_a_im, v_log_dt, v_b_re, v_b_im, v_c_re, v_c_im, v_d_skip, v_w_glu, v_b_glu, v_w_o_ssm, v_w_up, v_conv_w, v_conv_b, v_w_down, v_norm_out, v_w_fin, v_b_fin):
    nb, s, d = x.shape
    t = nb * s
    n_seq = nb * N_DEV
    me = 4 * lax.axis_index("x") + 2 * lax.axis_index("y") + lax.axis_index("c")
    cm = w_mod.shape[2]
    cf = w_fin.shape[1]
    c_up = w_up.shape[2]
    r_dn = w_down.shape[1]
    g_ssm = d // SSM_H

    wq8, c8 = _exchange("gather_first", [w_qkv[0].astype(BF16), c], [True, True])
    later = _Exchange([w_o_attn[0].astype(BF16), w_in_ssm[0].astype(BF16), w_glu[0].astype(BF16),
                       w_o_ssm[0].astype(BF16), w_up[0].astype(BF16), w_up[1].astype(BF16),
                       w_down[0].astype(BF16), w_down[1].astype(BF16), conv_w, d_skip, b_glu], [True] * 11)
    half = N_DEV // 2
    cb_l = [conv_b[i].reshape(N_DEV, 1, c_up) for i in range(2)]
    c_all = c8.reshape(n_seq, d)

    c_act, modloc = _modfin_fwd(c_all, w_mod, w_fin)
    (mod8,) = _exchange("gather_mod", [modloc], [True])
    mine = lax.dynamic_slice_in_dim(mod8, me * nb, nb, axis=1)
    mods = []
    for i in range(2):
        mi = mine[:, :, i * cm:(i + 1) * cm].transpose(1, 0, 2).reshape(nb, N_DEV * cm) + b_mod[i]
        mods.append([mi[:, j * d:(j + 1) * d].reshape(nb, 1, d) for j in range(6)])
    fin = mine[:, :, 2 * cm:].transpose(1, 0, 2).reshape(nb, N_DEV * cf) + b_fin
    sh_f, sc_f = fin[:, :d].reshape(nb, 1, d), fin[:, d:].reshape(nb, 1, d)

    row = lambda a: a.reshape(1, -1)
    x0 = x.reshape(t, d)

    def ffn_fwd(i, xin, sh, sc, gate):
        h = _norm_mod_fwd(f"ffn{i}_norm", xin, row(norm_ffn[i]), sh, sc, nb)
        up = _mm(f"ffn{i}_up", h, wup8[i], (t // tm_, N_DEV, 1),
                 pl.BlockSpec((tm_, d), lambda a, b, k: (a, 0)), pl.BlockSpec((None, d, c_up), lambda a, b, k: (b, 0, 0)),
                 pl.BlockSpec((None, tm_, c_up), lambda a, b, k: (b, a, 0)),
                 jax.ShapeDtypeStruct((N_DEV, t, c_up), BF16), _NN, (tm_, c_up))
        act = _ffn_act_fwd(f"ffn{i}_act", up, cw_l[i], cb_l[i], nb, s)
        yf = _mm(f"ffn{i}_down", act, wd4[i], (t // tm_, 1, half),
                 pl.BlockSpec((None, tm_, c_up), lambda a, b, k: (k, a, 0)),
                 pl.BlockSpec((None, c_up, d), lambda a, b, k: (k, 0, 0)),
                 pl.BlockSpec((tm_, d), lambda a, b, k: (a, 0)), jax.ShapeDtypeStruct((t, d), F32), _NN, (tm_, d))
        xout = _gate_add(f"ffn{i}_res", xin, yf, gate, nb)
        return xout, (h, up, act, yf)

    tm_ = _tile(t, 2048)
    sh1, sc1, g1, sh2, sc2, g2 = mods[0]
    h1 = _norm_mod_fwd("attn_norm", x0, row(norm_mix[0]), sh1, sc1, nb)
    cq = wq8.shape[2]
    qkv = _mm("attn_qkv", h1, wq8, (t // tm_, N_DEV, 1),
              pl.BlockSpec((tm_, d), lambda a, b, k: (a, 0)), pl.BlockSpec((None, d, cq), lambda a, b, k: (b, 0, 0)),
              pl.BlockSpec((tm_, cq), lambda a, b, k: (a, b)), jax.ShapeDtypeStruct((t, 3 * d), BF16), _NN, (tm_, cq))
    o_att, car_att, (wo8, win8, wglu8, wos8, wup8_0, wup8_1, wd8_0, wd8_1, cw8, dskip8, bglu8) = _attn_fwd(
        qkv, nb, s, d, later)
    wo = wo8.reshape(d, d)
    win = win8.reshape(d, d)
    wglu = wglu8.reshape(d, d)
    wos = wos8.reshape(d, d)
    wup8 = [wup8_0, wup8_1]
    wd4 = [wd8_0.reshape(half, 2 * r_dn, d), wd8_1.reshape(half, 2 * r_dn, d)]
    cw_l = [cw8[:, 0], cw8[:, 1]]
    dskip_f = dskip8.reshape(1, d)
    bglu_f = bglu8.reshape(1, d)
    ya = _mm_nn("attn_out", o_att, wo, F32)
    x1 = _gate_add("attn_res", x0, ya, g1, nb)
    x2, ffn0_saved = ffn_fwd(0, x1, sh2, sc2, g2)

    sh1b, sc1b, g1b, sh2b, sc2b, g2b = mods[1]
    ssm_params = (a_re[0], a_im[0], log_dt[0], b_re[0], b_im[0], c_re[0], c_im[0])
    compact, ops_vjp = jax.vjp(_ssm_compact, *ssm_params)
    ops = (*_ssm_expand(compact[:5]), compact[5], compact[6])
    h3 = _norm_mod_fwd("ssm_norm", x2, row(norm_mix[1]), sh1b, sc1b, nb)
    u = _mm_nn("ssm_in", h3, win, BF16)
    ys_core, xp_re, xp_im = _ssm_core_fwd(u, ops, nb)
    y_ssm, z_ssm = _ssm_post_fwd(ys_core, u, dskip_f)
    gl = _mm_nn("ssm_glu", z_ssm, wglu, F32)
    gg = _glu_fwd(z_ssm, gl, bglu_f)
    ys2 = _mm_nn("ssm_out", gg, wos, F32)
    x3 = _gate_add("ssm_res", x2, ys2, g1b, nb)
    x4, ffn1_saved = ffn_fwd(1, x3, sh2b, sc2b, g2b)

    dx4, g_norm_out, dsh_f, dsc_f, loss_blk = _norm_mod_bwd(
        "final_norm", None, x4, row(norm_out), sh_f, sc_f, None, loss_target.reshape(t, d), nb)
    loss = lax.psum(loss_blk[0, 0], ("x", "y", "c"))

    def ffn_bwd(i, dxo, xin, sc, gate, saved):
        h, up, act, yf = saved
        dyf, dgate = _gate_bwd(f"ffn{i}_res_bwd", dxo, yf, gate, nb)
        dact = _mm(f"ffn{i}_down_dx", dyf, wd4[i], (t // tm_, half, 1),
                   pl.BlockSpec((tm_, d), lambda a, b, k: (a, 0)), pl.BlockSpec((None, c_up, d), lambda a, b, k: (b, 0, 0)),
                   pl.BlockSpec((None, tm_, c_up), lambda a, b, k: (b, a, 0)),
                   jax.ShapeDtypeStruct((half, t, c_up), BF16), _NT, (tm_, c_up))
        tk = _tile(t, 1024)
        gwd = _mm(f"ffn{i}_down_dw", act, dyf, (half, 1, t // tk),
                  pl.BlockSpec((None, tk, c_up), lambda a, b, k: (a, k, 0)), pl.BlockSpec((tk, d), lambda a, b, k: (k, 0)),
                  pl.BlockSpec((None, c_up, d), lambda a, b, k: (a, 0, 0)),
                  jax.ShapeDtypeStruct((half, c_up, d), BF16), _TN, (c_up, d))
        dup, dcw, dcb = _ffn_act_bwd(f"ffn{i}_act_bwd", up, dact, cw_l[i], cb_l[i], nb, s)
        dh = _mm(f"ffn{i}_up_dx", dup, wup8[i], (t // tm_, 1, N_DEV),
                 pl.BlockSpec((None, tm_, c_up), lambda a, b, k: (k, a, 0)),
                 pl.BlockSpec((None, d, c_up), lambda a, b, k: (k, 0, 0)),
                 pl.BlockSpec((tm_, d), lambda a, b, k: (a, 0)), jax.ShapeDtypeStruct((t, d), F32), _NT, (tm_, d))
        gwup = _mm(f"ffn{i}_up_dw", h, dup, (1, N_DEV, t // tk),
                   pl.BlockSpec((tk, d), lambda a, b, k: (k, 0)), pl.BlockSpec((None, tk, c_up), lambda a, b, k: (b, k, 0)),
                   pl.BlockSpec((None, d, c_up), lambda a, b, k: (b, 0, 0)),
                   jax.ShapeDtypeStruct((N_DEV, d, c_up), BF16), _TN, (d, c_up))
        dxi, g_norm, dsh, dsc = _norm_mod_bwd(f"ffn{i}_norm_bwd", dh, xin, row(norm_ffn[i]), None, sc, dxo, None, nb)
        return dxi, (gwup, gwd.reshape(N_DEV, r_dn, d), dcw, dcb, g_norm, dsh, dsc, dgate)

    dx3, (gwup1, gwd1, dcw1, dcb1, g_nffn1, dsh2b, dsc2b, dg2b) = ffn_bwd(1, dx4, x3, sc2b, g2b, ffn1_saved)

    dys2, dg1b = _gate_bwd("ssm_res_bwd", dx3, ys2, g1b, nb)
    dgg = _mm_nt("ssm_out_dx", dys2, wos, F32)
    gwos = _mm_tn("ssm_out_dw", gg, dys2, BF16)
    dgl, dz1, g_bglu = _glu_bwd(dgg, z_ssm, gl, bglu_f)
    dz2 = _mm_nt("ssm_glu_dx", dgl, wglu, F32)
    gwglu = _mm_tn("ssm_glu_dw", z_ssm, dgl, BF16)
    dy_ssm, du_skip, g_dskip = _ssm_post_bwd(dz1, dz2, y_ssm, u, dskip_f)
    du_core, d_ops = _ssm_core_bwd(dy_ssm, u, xp_re, xp_im, ops, nb)
    du = _add_cast(du_core, du_skip)
    dh3 = _mm_nt("ssm_in_dx", du, win, F32)
    gwin = _mm_tn("ssm_in_dw", h3, du, BF16)
    dx2, g_nmix1, dsh1b, dsc1b = _norm_mod_bwd("ssm_norm_bwd", dh3, x2, row(norm_mix[1]), None, sc1b, dx3, None, nb)
    g_ssm_params = ops_vjp(d_ops)

    dx1, (gwup0, gwd0, dcw0, dcb0, g_nffn0, dsh2, dsc2, dg2) = ffn_bwd(0, dx2, x1, sc2, g2, ffn0_saved)

    dya, dg1 = _gate_bwd("attn_res_bwd", dx1, ya, g1, nb)
    do_att = _mm_nt("attn_out_dx", dya, wo, BF16)
    gwo = _mm_tn("attn_out_dw", o_att, dya, BF16)
    rows8 = lambda a: a.reshape(N_DEV, d // N_DEV, d)
    def two_d(w):
        shp = w.shape
        if len(shp) == 1:
            return (1, shp[0])
        if len(shp) == 2:
            return shp
        return (shp[0] * shp[1], math.prod(shp[2:]))

    ssm_w = [a_re, a_im, log_dt, b_re, b_im, c_re, c_im]
    ssm_partial = [g.reshape(two_d(w)) for g, w in zip(g_ssm_params, ssm_w)]
    early = _Exchange([rows8(gwo), rows8(gwin), rows8(gwglu), rows8(gwos), gwup0, gwup1, gwd0, gwd1] + ssm_partial,
                      [False] * 8 + [True] * 7)
    dq, dk, dv, early_res = _attn_bwd(qkv, car_att, do_att, nb, s, d, early)
    ro, rin, rglu, ros, rup0, rup1, rd0, rd1 = early_res[:8]
    ssm8 = early_res[8:]
    dqkv = jnp.concatenate([dq, dk, dv], axis=1)
    tk = _tile(t, 1024)
    gwq8 = _mm("attn_qkv_dw", h1, dqkv, (1, N_DEV, t // tk),
               pl.BlockSpec((tk, d), lambda a, b, k: (k, 0)),
               pl.BlockSpec((tk, cq), lambda a, b, k: (k, b)),
               pl.BlockSpec((None, d, cq), lambda a, b, k: (b, 0, 0)),
               jax.ShapeDtypeStruct((N_DEV, d, cq), BF16), _TN, (d, cq))
    dh1, (rq,) = _mm("attn_qkv_dx", dqkv, wq8, (t // tm_, 1, N_DEV),
                     pl.BlockSpec((tm_, cq), lambda a, b, k: (a, k)),
                     pl.BlockSpec((None, d, cq), lambda a, b, k: (k, 0, 0)),
                     pl.BlockSpec((tm_, d), lambda a, b, k: (a, 0)), jax.ShapeDtypeStruct((t, d), F32), _NT, (tm_, d),
                     ex=_Exchange([gwq8], [False]))
    dx0, g_nmix0, dsh1, dsc1 = _norm_mod_bwd("attn_norm_bwd", dh1, x0, row(norm_mix[0]), None, sc1, dx1, None, nb)
    grad_x = dx0.reshape(nb, s, d)

    dmod = [jnp.concatenate([a.reshape(nb, d) for a in grp], axis=1) for grp in
            ([dsh1, dsc1, dg1, dsh2, dsc2, dg2], [dsh1b, dsc1b, dg1b, dsh2b, dsc2b, dg2b])]
    dfin = jnp.concatenate([dsh_f.reshape(nb, d), dsc_f.reshape(nb, d)], axis=1)
    dmodfin = jnp.concatenate(dmod + [dfin], axis=1)
    flat = lambda a: a.reshape(1, -1)
    dmf8, nmix8, nffn8, nout8, cb8, dskip_g8, bglu_g8, cw_g8 = _exchange(
        "exchange_last",
        [dmodfin, jnp.concatenate([g_nmix0, g_nmix1]), jnp.concatenate([g_nffn0, g_nffn1]), g_norm_out,
         jnp.concatenate([flat(dcb0), flat(dcb1)]), g_dskip, g_bglu, jnp.stack([dcw0, dcw1])], [True] * 8)
    shard = d // N_DEV
    dskip_g8 = lax.dynamic_slice_in_dim(dskip_g8, me * shard, shard, axis=2)
    bglu_g8 = lax.dynamic_slice_in_dim(bglu_g8, me * shard, shard, axis=2)
    cw_g8 = lax.dynamic_slice_in_dim(cw_g8, me, 1, axis=2).reshape(N_DEV, 2 * 3, c_up)

    dall = dmf8.reshape(n_seq, 14 * d)
    dmod_loc = jnp.stack([lax.dynamic_slice_in_dim(dall[:, i * 6 * d:(i + 1) * 6 * d], me * cm, cm, axis=1)
                          for i in range(2)])
    dfin_loc = lax.dynamic_slice_in_dim(dall[:, 12 * d:], me * cf, cf, axis=1)
    g_w_mod, g_w_fin, g_bias = _modfin_bwd(c_act.T, dmod_loc, dfin_loc, dall)
    g_b_mod = g_bias[0, :12 * d].reshape(2, 6 * d)
    g_b_fin = g_bias[0, 12 * d:]

    def big(name, parts, w, m, v):
        shp = w.shape
        r2 = lambda a: a.reshape(-1, shp[-1])
        res = _adamw(name, parts.reshape(parts.shape[0], -1, shp[-1]), r2(w), r2(m), r2(v))
        return [a.reshape(shp) for a in res]

    upd = {}
    upd["w_mod"] = big("adamw_w_mod", g_w_mod[None], w_mod, m_w_mod, v_w_mod)
    upd["w_fin"] = big("adamw_w_fin", g_w_fin[None], w_fin, m_w_fin, v_w_fin)
    upd["w_qkv"] = big("adamw_w_qkv", rq, w_qkv, m_w_qkv, v_w_qkv)
    upd["w_o_attn"] = big("adamw_w_o_attn", ro, w_o_attn, m_w_o_attn, v_w_o_attn)
    upd["w_in_ssm"] = big("adamw_w_in_ssm", rin, w_in_ssm, m_w_in_ssm, v_w_in_ssm)
    upd["w_glu"] = big("adamw_w_glu", rglu, w_glu, m_w_glu, v_w_glu)
    upd["w_o_ssm"] = big("adamw_w_o_ssm", ros, w_o_ssm, m_w_o_ssm, v_w_o_ssm)
    up_l = [big(f"adamw_w_up{i}", r, w_up[i], m_w_up[i], v_w_up[i]) for i, r in enumerate((rup0, rup1))]
    upd["w_up"] = [jnp.stack([up_l[0][j], up_l[1][j]]) for j in range(4)]
    dn_l = [big(f"adamw_w_down{i}", r, w_down[i], m_w_down[i], v_w_down[i]) for i, r in enumerate((rd0, rd1))]
    upd["w_down"] = [jnp.stack([dn_l[0][j], dn_l[1][j]]) for j in range(4)]

    small_names = ["norm_mix", "norm_ffn", "b_mod", "a_re", "a_im", "log_dt", "b_re", "b_im", "c_re", "c_im",
                   "d_skip", "b_glu", "conv_w", "conv_b", "norm_out", "b_fin"]
    small_g = [nmix8, nffn8, g_b_mod[None], *ssm8, dskip_g8, bglu_g8, cw_g8, cb8, nout8, g_b_fin[None]]
    small_w = [norm_mix, norm_ffn, b_mod, a_re, a_im, log_dt, b_re, b_im, c_re, c_im, d_skip, b_glu, conv_w, conv_b,
               norm_out, b_fin]
    small_m = [m_norm_mix, m_norm_ffn, m_b_mod, m_a_re, m_a_im, m_log_dt, m_b_re, m_b_im, m_c_re, m_c_im, m_d_skip,
               m_b_glu, m_conv_w, m_conv_b, m_norm_out, m_b_fin]
    small_v = [v_norm_mix, v_norm_ffn, v_b_mod, v_a_re, v_a_im, v_log_dt, v_b_re, v_b_im, v_c_re, v_c_im, v_d_skip,
               v_b_glu, v_conv_w, v_conv_b, v_norm_out, v_b_fin]
    entries = [(gp.reshape((gp.shape[0],) + two_d(w)), w.reshape(two_d(w)), m.reshape(two_d(w)), v.reshape(two_d(w)))
               for gp, w, m, v in zip(small_g, small_w, small_m, small_v)]
    res = _adamw_many("adamw_small", entries)
    for j, (nm, w) in enumerate(zip(small_names, small_w)):
        upd[nm] = [res[4 * j + k].reshape(w.shape) for k in range(4)]

    order = ["norm_mix", "norm_ffn", "w_mod", "b_mod", "w_qkv", "w_o_attn", "w_in_ssm", "a_re", "a_im", "log_dt",
             "b_re", "b_im", "c_re", "c_im", "d_skip", "w_glu", "b_glu", "w_o_ssm", "w_up", "conv_w", "conv_b",
             "w_down", "norm_out", "w_fin", "b_fin"]
    outs = [loss, grad_x]
    for k in range(4):
        outs += [upd[nm][k] for nm in order]
    return tuple(outs)
```

```python
import functools
import math

import jax
import jax.numpy as jnp
from jax import lax
from jax.experimental import pallas as pl
from jax.experimental.pallas import tpu as pltpu

F32 = jnp.float32
BF16 = jnp.bfloat16
MESH = pl.DeviceIdType.MESH

N_DEV = 8
HEAD_DIM = 64
ATT_BLK = 128
ATT_BQ = 256
ATT_UNROLL = 2
SSM_H = 16
SSM_P = 64
SSM_L = 4
EPS = 1e-6
ADAM_LR, ADAM_B1, ADAM_B2, ADAM_EPS, ADAM_WD, ADAM_STEP = 0.001, 0.9, 0.999, 1e-08, 0.01, 10
V7X_VMEM_LIMIT = 56 * 1024 * 1024
LANE = 128

_NN = (((1,), (0,)), ((), ()))
_NT = (((1,), (1,)), ((), ()))
_TN = (((0,), (0,)), ((), ()))


def _cp(sem):
    return pltpu.CompilerParams(dimension_semantics=sem, vmem_limit_bytes=V7X_VMEM_LIMIT)


def _tile(n, pref):
    if n <= pref:
        return n
    t = pref - pref % 16
    while t >= 16:
        if n % t == 0:
            return t
        t -= 16
    return n


def _mm(name, a, b, grid, a_spec, b_spec, out_spec, out_shape, dims, acc_shape, ex=None):
    nk = grid[-1]
    kax = len(grid) - 1
    ex = ex or _NoExchange()

    def body(*refs):
        a_ref, b_ref = refs[:2]
        ex_ins = refs[2:2 + ex.n]
        o_ref = refs[2 + ex.n]
        ex_outs = refs[3 + ex.n:3 + 2 * ex.n]
        acc_ref = refs[3 + 2 * ex.n]
        sems = refs[4 + 2 * ex.n:]
        first, last = _grid_ends(grid)
        k = pl.program_id(kax)

        @pl.when(first)
        def _():
            ex.start(ex_ins, ex_outs, sems)

        @pl.when(k == 0)
        def _():
            acc_ref[...] = jnp.zeros(acc_shape, F32)

        acc_ref[...] += lax.dot_general(a_ref[...].astype(BF16), b_ref[...].astype(BF16), dims,
                                        preferred_element_type=F32)

        @pl.when(k == nk - 1)
        def _():
            o_ref[...] = acc_ref[...].astype(o_ref.dtype)

        @pl.when(last)
        def _():
            ex.wait(ex_ins, ex_outs, sems)

    sem = ("arbitrary",) * len(grid) if ex.n else ("parallel",) * kax + ("arbitrary",)
    res = pl.pallas_call(
        body, name=name, grid=grid, in_specs=[a_spec, b_spec] + ex.specs, out_specs=[out_spec] + ex.specs,
        out_shape=[out_shape] + ex.out_shape, scratch_shapes=[pltpu.VMEM(acc_shape, F32)] + ex.scratch,
        compiler_params=_cp(sem))(a, b, *ex.arrs)
    return (res[0], list(res[1:])) if ex.n else res[0]


def _mm_nn(name, a, w, out_dtype):
    m, k = a.shape
    n = w.shape[1]
    tm, tn, tk = _tile(m, 512), _tile(n, 1024), _tile(k, 1024)
    return _mm(name, a, w, (m // tm, n // tn, k // tk),
               pl.BlockSpec((tm, tk), lambda i, j, kk: (i, kk)), pl.BlockSpec((tk, tn), lambda i, j, kk: (kk, j)),
               pl.BlockSpec((tm, tn), lambda i, j, kk: (i, j)), jax.ShapeDtypeStruct((m, n), out_dtype), _NN, (tm, tn))


def _mm_nt(name, a, w, out_dtype):
    m, n = a.shape
    k = w.shape[0]
    tm, tko, tn = _tile(m, 512), _tile(k, 1024), _tile(n, 1024)
    return _mm(name, a, w, (m // tm, k // tko, n // tn),
               pl.BlockSpec((tm, tn), lambda i, j, kk: (i, kk)), pl.BlockSpec((tko, tn), lambda i, j, kk: (j, kk)),
               pl.BlockSpec((tm, tko), lambda i, j, kk: (i, j)), jax.ShapeDtypeStruct((m, k), out_dtype), _NT, (tm, tko))


def _mm_tn(name, a, b, out_dtype):
    t, m = a.shape
    n = b.shape[1]
    tm, tn, tk = _tile(m, 512), _tile(n, 1024), _tile(t, 1024)
    return _mm(name, a, b, (m // tm, n // tn, t // tk),
               pl.BlockSpec((tk, tm), lambda i, j, kk: (kk, i)), pl.BlockSpec((tk, tn), lambda i, j, kk: (kk, j)),
               pl.BlockSpec((tm, tn), lambda i, j, kk: (i, j)), jax.ShapeDtypeStruct((m, n), out_dtype), _TN, (tm, tn))


def _norm_mod_fwd(name, x, g, shift, scale, nb):
    t, d = x.shape
    s = t // nb
    tr = _tile(s, 512)
    nt = s // tr

    def body(x_ref, g_ref, sh_ref, sc_ref, h_ref):
        xv = x_ref[...]
        r = lax.rsqrt(jnp.mean(xv * xv, axis=-1, keepdims=True) + EPS)
        y = xv * r * g_ref[...]
        h_ref[...] = (y * (1.0 + sc_ref[...]) + sh_ref[...]).astype(h_ref.dtype)

    row = pl.BlockSpec((tr, d), lambda b, i: (b * nt + i, 0))
    vec = pl.BlockSpec((None, 1, d), lambda b, i: (b, 0, 0))
    return pl.pallas_call(body, name=name, grid=(nb, nt),
                          in_specs=[row, pl.BlockSpec((1, d), lambda b, i: (0, 0)), vec, vec],
                          out_specs=row, out_shape=jax.ShapeDtypeStruct((t, d), BF16),
                          compiler_params=_cp(("parallel", "parallel")))(x, g, shift, scale)


def _norm_mod_bwd(name, dh, x, g, shift, scale, dres, target, nb):
    t, d = x.shape
    s = t // nb
    tr = _tile(s, 256)
    nt = s // tr
    final = target is not None

    def body(*refs):
        if final:
            x_ref, g_ref, sh_ref, sc_ref, tg_ref, dx_ref, dg_ref, dsh_ref, dsc_ref, loss_ref = refs
        else:
            dh_ref, x_ref, g_ref, sc_ref, dres_ref, dx_ref, dg_ref, dsh_ref, dsc_ref = refs
        b, i = pl.program_id(0), pl.program_id(1)
        xv = x_ref[...]
        gv = g_ref[...]
        r = lax.rsqrt(jnp.mean(xv * xv, axis=-1, keepdims=True) + EPS)
        nrm = xv * r
        y = nrm * gv
        one_sc = 1.0 + sc_ref[...]
        if final:
            err = y * one_sc + sh_ref[...] - tg_ref[...]
            dhv = err * (1.0 / d)
        else:
            dhv = dh_ref[...].astype(F32)
        dy = dhv * one_sc
        dn = dy * gv
        dxv = r * (dn - nrm * jnp.mean(dn * nrm, axis=-1, keepdims=True))
        if final:
            dx_ref[...] = dxv
        else:
            dx_ref[...] = dres_ref[...] + dxv

        @pl.when(i == 0)
        def _():
            dsh_ref[...] = jnp.zeros_like(dsh_ref)
            dsc_ref[...] = jnp.zeros_like(dsc_ref)

        @pl.when((i == 0) & (b == 0))
        def _():
            dg_ref[...] = jnp.zeros_like(dg_ref)
            if final:
                loss_ref[...] = jnp.zeros_like(loss_ref)

        dsh_ref[...] += jnp.sum(dhv, axis=0, keepdims=True)
        dsc_ref[...] += jnp.sum(dhv * y, axis=0, keepdims=True)
        dg_ref[...] += jnp.sum(dy * nrm, axis=0, keepdims=True)
        if final:
            loss_ref[...] += (0.5 / d) * jnp.sum(err * err)

    row = pl.BlockSpec((tr, d), lambda b, i: (b * nt + i, 0))
    vec = pl.BlockSpec((None, 1, d), lambda b, i: (b, 0, 0))
    gsp = pl.BlockSpec((1, d), lambda b, i: (0, 0))
    out_specs = [row, gsp, vec, vec]
    out_shape = [jax.ShapeDtypeStruct((t, d), F32), jax.ShapeDtypeStruct((1, d), F32),
                 jax.ShapeDtypeStruct((nb, 1, d), F32), jax.ShapeDtypeStruct((nb, 1, d), F32)]
    if final:
        ins, in_specs = [x, g, shift, scale, target], [row, gsp, vec, vec, row]
        out_specs.append(pl.BlockSpec((8, LANE), lambda b, i: (0, 0)))
        out_shape.append(jax.ShapeDtypeStruct((8, LANE), F32))
    else:
        ins, in_specs = [dh, x, g, scale, dres], [row, row, gsp, vec, row]
    return pl.pallas_call(body, name=name, grid=(nb, nt), in_specs=in_specs, out_specs=out_specs,
                          out_shape=out_shape, compiler_params=_cp(("arbitrary", "arbitrary")))(*ins)


def _gate_add(name, x, y, gate, nb):
    t, d = x.shape
    s = t // nb
    tr = _tile(s, 512)
    nt = s // tr

    def body(x_ref, y_ref, g_ref, o_ref):
        o_ref[...] = x_ref[...] + g_ref[...] * y_ref[...]

    row = pl.BlockSpec((tr, d), lambda b, i: (b * nt + i, 0))
    vec = pl.BlockSpec((None, 1, d), lambda b, i: (b, 0, 0))
    return pl.pallas_call(body, name=name, grid=(nb, nt), in_specs=[row, row, vec], out_specs=row,
                          out_shape=jax.ShapeDtypeStruct((t, d), F32),
                          compiler_params=_cp(("parallel", "parallel")))(x, y, gate)


def _gate_bwd(name, dx, y, gate, nb):
    t, d = dx.shape
    s = t // nb
    tr = _tile(s, 512)
    nt = s // tr

    def body(dx_ref, y_ref, g_ref, dy_ref, dg_ref):
        dxv = dx_ref[...]
        dy_ref[...] = (g_ref[...] * dxv).astype(dy_ref.dtype)

        @pl.when(pl.program_id(1) == 0)
        def _():
            dg_ref[...] = jnp.zeros_like(dg_ref)

        dg_ref[...] += jnp.sum(dxv * y_ref[...], axis=0, keepdims=True)

    row = pl.BlockSpec((tr, d), lambda b, i: (b * nt + i, 0))
    vec = pl.BlockSpec((None, 1, d), lambda b, i: (b, 0, 0))
    return pl.pallas_call(body, name=name, grid=(nb, nt), in_specs=[row, row, vec], out_specs=[row, vec],
                          out_shape=[jax.ShapeDtypeStruct((t, d), BF16), jax.ShapeDtypeStruct((nb, 1, d), F32)],
                          compiler_params=_cp(("parallel", "arbitrary")))(dx, y, gate)


def _log_sigmoid(z):
    return jnp.minimum(z, 0.0) - jnp.log(1.0 + jnp.exp(-jnp.abs(z)))


def _split_dot(v, tri):
    hi = v.astype(BF16)
    lo = (v - hi.astype(F32)).astype(BF16)
    return (jnp.dot(hi, tri, preferred_element_type=F32) + jnp.dot(lo, tri, preferred_element_type=F32))


def _grid_ends(grid):
    ids = [pl.program_id(a) for a in range(len(grid))]
    first = functools.reduce(lambda u, w: u & w, [i == 0 for i in ids])
    last = functools.reduce(lambda u, w: u & w, [i == n - 1 for i, n in zip(ids, grid)])
    return first, last


def _attn_trips(nq):
    return nq * (nq + 1) // 2


def _next_trip(qi, jj, nq):
    wrap = jj >= qi
    nqi = jnp.where(wrap, jnp.minimum(qi + 1, nq - 1), qi)
    njj = jnp.where(wrap, jnp.where(qi + 1 < nq, 0, jj), jj + 1)
    return nqi, njj


def _attn_fwd(qkv, nb, s, d, ex):
    t = nb * s
    npair = d // LANE
    bk = ATT_BLK
    bq = min(ATT_BQ, s)
    nq = s // bq
    kpq = bq // bk
    nheads = LANE // HEAD_DIM
    scale = HEAD_DIM ** -0.5
    grid = (nb, npair)
    assert s // bk <= HEAD_DIM, "one carry lane per key block and head"
    assert bk == LANE, "the running sums are kept one 128-lane tile wide"
    assert kpq == ATT_UNROLL, "query block qi has exactly qi + 1 trips"

    def body(*refs):
        q_ref, k_ref, v_ref = refs[:3]
        ex_ins = refs[3:3 + ex.n]
        o_ref, car_ref = refs[3 + ex.n:5 + ex.n]
        ex_outs = refs[5 + ex.n:5 + 2 * ex.n]
        acc_s, run_s, z_s, arg_s = refs[5 + 2 * ex.n:9 + 2 * ex.n]
        sems = refs[9 + 2 * ex.n:]
        first, last = _grid_ends(grid)

        @pl.when(first)
        def _():
            ex.start(ex_ins, ex_outs, sems)

        lane = lax.broadcasted_iota(jnp.int32, (1, LANE), 1)
        row = lax.broadcasted_iota(jnp.int32, (bq, bk), 0)
        col = lax.broadcasted_iota(jnp.int32, (bq, bk), 1)
        trow = lax.broadcasted_iota(jnp.int32, (bk, bk), 0)
        tcol = lax.broadcasted_iota(jnp.int32, (bk, bk), 1)
        tri = (trow > tcol).astype(BF16)
        hms = [(lane // HEAD_DIM) == hh for hh in range(nheads)]

        def q0_of(qi):
            return pl.multiple_of(qi * bq, bq)

        def kblk_of(qi, jj, u):
            return (qi + 1) * kpq - 1 - (ATT_UNROLL * jj + u)

        def scores(qi, jj):
            q = q_ref[pl.ds(q0_of(qi), bq), :]
            qhs = [jnp.where(hm, q, jnp.zeros_like(q)) * scale for hm in hms]
            ks = [k_ref[pl.ds(pl.multiple_of(kblk_of(qi, jj, u) * bk, bk), bk), :] for u in range(ATT_UNROLL)]
            return [[lax.dot_general(qhs[hh], kj, _NT, preferred_element_type=F32) for kj in ks]
                    for hh in range(nheads)]

        def keep(zn):
            for hh in range(nheads):
                for u in range(ATT_UNROLL):
                    z_s[hh, u] = zn[hh][u]

        def exponents(qi, jj):
            q0 = q0_of(qi)
            car = car_ref[pl.ds(q0, bq), :]
            for hh in range(nheads):
                run = jnp.where(jj == 0, 0.0, run_s[hh])
                for u in range(ATT_UNROLL):
                    j = kblk_of(qi, jj, u)
                    mask = (j * bk + col) < (q0 + row)
                    z = z_s[hh, u]
                    lb = _log_sigmoid(z)
                    l1 = jnp.where(mask, lb - z, 0.0)
                    arg_s[hh, u] = jnp.where(mask, lb + (_split_dot(l1, tri) + run), -1e30)
                    car = jnp.where(lane == hh * HEAD_DIM + j, run, car)
                    run = run + jnp.sum(l1, axis=1, keepdims=True)
                run_s[hh] = run
            car_ref[pl.ds(q0, bq), :] = car

        def weigh(qi, jj):
            q0 = q0_of(qi)
            for hh in range(nheads):
                acc = None
                for u in range(ATT_UNROLL):
                    vj = v_ref[pl.ds(pl.multiple_of(kblk_of(qi, jj, u) * bk, bk), bk), :]
                    pv = jnp.dot(jnp.exp(arg_s[hh, u]).astype(BF16), vj, preferred_element_type=F32)
                    acc = pv if acc is None else acc + pv
                acc_s[hh, pl.ds(q0, bq), :] += acc

        def step(n, carry):
            qi, jj, pqi, pjj = carry
            nqi, njj = _next_trip(qi, jj, nq)
            zn = scores(nqi, njj)
            weigh(pqi, pjj)
            exponents(qi, jj)
            keep(zn)
            return nqi, njj, qi, jj

        acc_s[...] = jnp.zeros_like(acc_s)
        run_s[...] = jnp.zeros_like(run_s)
        car_ref[...] = jnp.zeros_like(car_ref)
        arg_s[...] = jnp.full(arg_s.shape, -1e30, F32)
        zero = jnp.int32(0)
        keep(scores(zero, zero))
        _, _, lqi, ljj = lax.fori_loop(0, _attn_trips(nq), step, (zero, zero, zero, zero))
        weigh(lqi, ljj)
        out = acc_s[0]
        for hh in range(1, nheads):
            out = jnp.where(hms[hh], acc_s[hh], out)
        o_ref[...] = out.astype(o_ref.dtype)

        @pl.when(last)
        def _():
            ex.wait(ex_ins, ex_outs, sems)

    seq = lambda off: pl.BlockSpec((s, LANE), lambda b, p: (b, off + p))
    res = pl.pallas_call(
        body, name="attn_fwd", grid=grid,
        in_specs=[seq(0), seq(npair), seq(2 * npair)] + ex.specs,
        out_specs=[seq(0), seq(0)] + ex.specs,
        out_shape=[jax.ShapeDtypeStruct((t, d), BF16), jax.ShapeDtypeStruct((t, d), F32)] + ex.out_shape,
        scratch_shapes=[pltpu.VMEM((nheads, s, LANE), F32), pltpu.VMEM((nheads, bq, LANE), F32),
                        pltpu.VMEM((nheads, ATT_UNROLL, bq, bk), F32),
                        pltpu.VMEM((nheads, ATT_UNROLL, bq, bk), F32)] + ex.scratch,
        compiler_params=_cp(("arbitrary", "arbitrary")))(qkv, qkv, qkv, *ex.arrs)
    return res[0], res[1], list(res[2:])


def _attn_bwd(qkv, car, do, nb, s, d, ex):
    t = nb * s
    npair = d // LANE
    bk = ATT_BLK
    bq = min(ATT_BQ, s)
    nq = s // bq
    kpq = bq // bk
    nheads = LANE // HEAD_DIM
    scale = HEAD_DIM ** -0.5
    grid = (nb, npair)
    assert kpq == ATT_UNROLL, "query block qi has exactly qi + 1 trips"

    def body(*refs):
        q_ref, k_ref, v_ref, car_ref, do_ref = refs[:5]
        ex_ins = refs[5:5 + ex.n]
        dq_ref, dk_ref, dv_ref = refs[5 + ex.n:8 + ex.n]
        ex_outs = refs[8 + ex.n:8 + 2 * ex.n]
        dk_acc, dv_acc, dq_s, rune_s, z_s, da_s, dz_s, a_s = refs[8 + 2 * ex.n:16 + 2 * ex.n]
        sems = refs[16 + 2 * ex.n:]
        first, last = _grid_ends(grid)

        @pl.when(first)
        def _():
            ex.start(ex_ins, ex_outs, sems)

        lane = lax.broadcasted_iota(jnp.int32, (1, LANE), 1)
        row = lax.broadcasted_iota(jnp.int32, (bq, bk), 0)
        col = lax.broadcasted_iota(jnp.int32, (bq, bk), 1)
        trow = lax.broadcasted_iota(jnp.int32, (bk, bk), 0)
        tcol = lax.broadcasted_iota(jnp.int32, (bk, bk), 1)
        tri_suf = (trow > tcol).astype(BF16)
        tri_pre = (trow < tcol).astype(BF16)
        hms = [(lane // HEAD_DIM) == hh for hh in range(nheads)]

        def q0_of(qi):
            return pl.multiple_of(qi * bq, bq)

        def k0_of(jj, u):
            return pl.multiple_of((ATT_UNROLL * jj + u) * bk, bk)

        def heads_of(ref, qi, factor):
            x = ref[pl.ds(q0_of(qi), bq), :]
            return [jnp.where(hm, x, jnp.zeros_like(x)) * factor for hm in hms]

        def products(qi, jj):
            qhs, dohs = heads_of(q_ref, qi, scale), heads_of(do_ref, qi, 1.0)
            ks = [k_ref[pl.ds(k0_of(jj, u), bk), :] for u in range(ATT_UNROLL)]
            vs = [v_ref[pl.ds(k0_of(jj, u), bk), :] for u in range(ATT_UNROLL)]
            zn = [[lax.dot_general(qhs[hh], kj, _NT, preferred_element_type=F32) for kj in ks] for hh in range(nheads)]
            dn = [[lax.dot_general(dohs[hh], vj, _NT, preferred_element_type=F32) for vj in vs] for hh in range(nheads)]
            return zn, dn

        def keep(zn, dn):
            for hh in range(nheads):
                for u in range(ATT_UNROLL):
                    z_s[hh, u] = zn[hh][u]
                    da_s[hh, u] = dn[hh][u]

        def middle(qi, jj):
            q0 = q0_of(qi)
            car = car_ref[pl.ds(q0, bq), :]
            for hh in range(nheads):
                run_e = jnp.where(jj == 0, 0.0, rune_s[hh])
                for u in range(ATT_UNROLL):
                    j = ATT_UNROLL * jj + u
                    mask = (j * bk + col) < (q0 + row)
                    z = z_s[hh, u]
                    lb = _log_sigmoid(z)
                    l1u = lb - z
                    l1 = jnp.where(mask, l1u, 0.0)
                    run = jnp.sum(jnp.where(lane == hh * HEAD_DIM + j, car, 0.0), axis=1, keepdims=True)
                    a = jnp.where(mask, jnp.exp(lb + (_split_dot(l1, tri_suf) + run)), 0.0)
                    e = da_s[hh, u] * a
                    dz = e * jnp.exp(l1u) - (_split_dot(e, tri_pre) + run_e) * jnp.exp(lb)
                    dz_s[hh, u] = jnp.where(mask, dz, 0.0).astype(BF16)
                    a_s[hh, u] = a.astype(BF16)
                    run_e = run_e + jnp.sum(e, axis=1, keepdims=True)
                rune_s[hh] = run_e

        def grads(qi, jj):
            q0 = q0_of(qi)
            qhs, dohs = heads_of(q_ref, qi, scale), heads_of(do_ref, qi, 1.0)
            dqs = [None] * nheads
            for u in range(ATT_UNROLL):
                k0 = k0_of(jj, u)
                kj = k_ref[pl.ds(k0, bk), :]
                for hh in range(nheads):
                    dzb = dz_s[hh, u]
                    dqu = jnp.dot(dzb, kj, preferred_element_type=F32)
                    dqs[hh] = dqu if dqs[hh] is None else dqs[hh] + dqu
                    dkh = lax.dot_general(dzb, qhs[hh], _TN, preferred_element_type=F32)
                    dvh = lax.dot_general(a_s[hh, u], dohs[hh], _TN, preferred_element_type=F32)
                    dk_blk = dkh if hh == 0 else dk_blk + dkh
                    dv_blk = dvh if hh == 0 else dv_blk + dvh
                dk_acc[pl.ds(k0, bk), :] += dk_blk
                dv_acc[pl.ds(k0, bk), :] += dv_blk
            for hh in range(nheads):
                dq_s[hh, pl.ds(q0, bq), :] += dqs[hh]

        def step(n, carry):
            qi, jj, pqi, pjj = carry
            nqi, njj = _next_trip(qi, jj, nq)
            zn, dn = products(nqi, njj)
            grads(pqi, pjj)
            middle(qi, jj)
            keep(zn, dn)
            return nqi, njj, qi, jj

        dk_acc[...] = jnp.zeros_like(dk_acc)
        dv_acc[...] = jnp.zeros_like(dv_acc)
        dq_s[...] = jnp.zeros_like(dq_s)
        rune_s[...] = jnp.zeros_like(rune_s)
        dz_s[...] = jnp.zeros_like(dz_s)
        a_s[...] = jnp.zeros_like(a_s)
        zero = jnp.int32(0)
        keep(*products(zero, zero))
        _, _, lqi, ljj = lax.fori_loop(0, _attn_trips(nq), step, (zero, zero, zero, zero))
        grads(lqi, ljj)
        dq_out = dq_s[0]
        for hh in range(1, nheads):
            dq_out = jnp.where(hms[hh], dq_s[hh], dq_out)
        dq_ref[...] = (dq_out * scale).astype(dq_ref.dtype)
        dk_ref[...] = dk_acc[...].astype(dk_ref.dtype)
        dv_ref[...] = dv_acc[...].astype(dv_ref.dtype)

        @pl.when(last)
        def _():
            ex.wait(ex_ins, ex_outs, sems)

    seq = lambda off: pl.BlockSpec((s, LANE), lambda b, p: (b, off + p))
    sds = jax.ShapeDtypeStruct((t, d), BF16)
    res = pl.pallas_call(
        body, name="attn_bwd", grid=grid,
        in_specs=[seq(0), seq(npair), seq(2 * npair), seq(0), seq(0)] + ex.specs,
        out_specs=[seq(0), seq(0), seq(0)] + ex.specs, out_shape=[sds, sds, sds] + ex.out_shape,
        scratch_shapes=[pltpu.VMEM((s, LANE), F32), pltpu.VMEM((s, LANE), F32),
                        pltpu.VMEM((nheads, s, LANE), F32), pltpu.VMEM((nheads, bq, LANE), F32),
                        pltpu.VMEM((nheads, ATT_UNROLL, bq, bk), F32), pltpu.VMEM((nheads, ATT_UNROLL, bq, bk), F32),
                        pltpu.VMEM((nheads, ATT_UNROLL, bq, bk), BF16),
                        pltpu.VMEM((nheads, ATT_UNROLL, bq, bk), BF16)] + ex.scratch,
        compiler_params=_cp(("arbitrary", "arbitrary")))(qkv, qkv, qkv, car, do, *ex.arrs)
    return res[0], res[1], res[2], list(res[3:])


def _conv3(u_ref, w, bias, c, r0, rc):
    x = u_ref[pl.ds(r0, rc), :].astype(F32)
    p0 = pl.multiple_of(jnp.maximum(r0 - 16, 0), 16)
    prev = u_ref[pl.ds(p0, 16), :].astype(F32)
    prev = jnp.where(c > 0, prev, 0.0)
    row = lax.broadcasted_iota(jnp.int32, (rc, 1), 0)
    s1 = jnp.where(row == 0, prev[15:16, :], pltpu.roll(x, 1, 0))
    s2 = jnp.where(row == 0, prev[14:15, :], jnp.where(row == 1, prev[15:16, :], pltpu.roll(x, 2, 0)))
    cv = w[2:3, :] * x + w[1:2, :] * s1 + w[0:1, :] * s2 + bias
    return cv, x, s1, s2


def _sigmoid(x):
    return 1.0 / (1.0 + jnp.exp(-x))


def _ffn_act_fwd(name, up8, cw8, cb8, nb, s):
    _, t, c_w = up8.shape
    rc = _tile(s, 256)
    nch = s // rc
    half = N_DEV // 2

    def body(ug_ref, uv_ref, wg_ref, wv_ref, bg_ref, bv_ref, act_ref):
        wg, wv, bg, bv = wg_ref[...], wv_ref[...], bg_ref[...], bv_ref[...]

        def chunk(c, carry):
            r0 = pl.multiple_of(c * rc, rc)
            cg = _conv3(ug_ref, wg, bg, c, r0, rc)[0]
            cv = _conv3(uv_ref, wv, bv, c, r0, rc)[0]
            act_ref[pl.ds(r0, rc), :] = (cg * _sigmoid(cg) * cv).astype(act_ref.dtype)
            return carry

        lax.fori_loop(0, nch, chunk, 0)

    def slab(off):
        return pl.BlockSpec((None, s, c_w), lambda k, b: (k + off, b, 0))

    def par(rows, off):
        return pl.BlockSpec((None, rows, c_w), lambda k, b: (k + off, 0, 0))

    return pl.pallas_call(
        body, name=name, grid=(half, nb),
        in_specs=[slab(0), slab(half), par(3, 0), par(3, half), par(1, 0), par(1, half)],
        out_specs=pl.BlockSpec((None, s, c_w), lambda k, b: (k, b, 0)),
        out_shape=jax.ShapeDtypeStruct((half, t, c_w), BF16),
        compiler_params=_cp(("parallel", "parallel")))(up8, up8, cw8, cw8, cb8, cb8)


def _ffn_act_bwd(name, up8, dact4, cw8, cb8, nb, s):
    _, t, c_w = up8.shape
    rc = _tile(s, 256)
    nch = s // rc
    half = N_DEV // 2

    def body(u_ref, da_ref, w_ref, b_ref, dup_ref, dcw_ref, dcb_ref):
        w2, b2 = w_ref[...], b_ref[...]
        row = lax.broadcasted_iota(jnp.int32, (rc, 1), 0)

        @pl.when(pl.program_id(1) == 0)
        def _():
            dcw_ref[...] = jnp.zeros_like(dcw_ref)
            dcb_ref[...] = jnp.zeros_like(dcb_ref)

        def chunk(i, carry):
            c = nch - 1 - i
            r0 = pl.multiple_of(c * rc, rc)
            convs = [_conv3(u_ref.at[h], w2[h], b2[h], c, r0, rc) for h in range(2)]
            gt, vl = convs[0][0], convs[1][0]
            da = da_ref[pl.ds(r0, rc), :].astype(F32)
            sg = _sigmoid(gt)
            dcvs = [da * vl * sg * (1.0 + gt * (1.0 - sg)), da * gt * sg]
            out = []
            for h in range(2):
                n0, n1, a0, a1, a2, ab = carry[6 * h:6 * h + 6]
                dcv, (_, x, s1, s2), w = dcvs[h], convs[h], w2[h]
                t1 = jnp.where(row == rc - 1, n0, pltpu.roll(dcv, rc - 1, 0))
                t2 = jnp.where(row == rc - 2, n0, jnp.where(row == rc - 1, n1, pltpu.roll(dcv, rc - 2, 0)))
                dup = w[2:3, :] * dcv + w[1:2, :] * t1 + w[0:1, :] * t2
                dup_ref[h, pl.ds(r0, rc), :] = dup.astype(dup_ref.dtype)
                out += [dcv[0:1, :], dcv[1:2, :],
                        a0 + jnp.sum(dcv * s2, axis=0, keepdims=True), a1 + jnp.sum(dcv * s1, axis=0, keepdims=True),
                        a2 + jnp.sum(dcv * x, axis=0, keepdims=True), ab + jnp.sum(dcv, axis=0, keepdims=True)]
            return tuple(out)

        z = jnp.zeros((1, c_w), F32)
        fin = lax.fori_loop(0, nch, chunk, (z,) * 12)
        for h in range(2):
            _, _, a0, a1, a2, ab = fin[6 * h:6 * h + 6]
            dcw_ref[h, 0:1, :] += a0
            dcw_ref[h, 1:2, :] += a1
            dcw_ref[h, 2:3, :] += a2
            dcb_ref[h] += ab

    def pair(rows, per_seq):
        return pl.BlockSpec((2, None, rows, c_w), (lambda k, b: (0, k, b, 0)) if per_seq else (lambda k, b: (0, k, 0, 0)))

    four = lambda a: a.reshape((2, half) + a.shape[1:])
    dup, dcw, dcb = pl.pallas_call(
        body, name=name, grid=(half, nb),
        in_specs=[pair(s, True), pl.BlockSpec((None, s, c_w), lambda k, b: (k, b, 0)), pair(3, False), pair(1, False)],
        out_specs=[pair(s, True), pair(3, False), pair(1, False)],
        out_shape=[jax.ShapeDtypeStruct((2, half, t, c_w), BF16), jax.ShapeDtypeStruct((2, half, 3, c_w), F32),
                   jax.ShapeDtypeStruct((2, half, 1, c_w), F32)],
        compiler_params=_cp(("parallel", "arbitrary")))(four(up8), dact4, four(cw8), four(cb8))
    return dup.reshape(N_DEV, t, c_w), dcw.reshape(N_DEV, 3, c_w), dcb.reshape(N_DEV, 1, c_w)


_GELU_C0 = math.sqrt(2.0 / math.pi)
_GELU_C1 = 0.044715


def _rowwise(name, body, ins, in_kinds, out_kinds, t, d, tr_pref=512):
    tr = _tile(t, tr_pref)
    row = pl.BlockSpec((tr, d), lambda i: (i, 0))
    vec = pl.BlockSpec((1, d), lambda i: (0, 0))
    in_specs = [row if k == "row" else vec for k in in_kinds]
    out_specs = [row if k[0] == "row" else vec for k in out_kinds]
    out_shape = [jax.ShapeDtypeStruct((t, d) if k[0] == "row" else (1, d), k[1]) for k in out_kinds]
    has_acc = any(k[0] == "acc" for k in out_kinds)
    return pl.pallas_call(body, name=name, grid=(t // tr,), in_specs=in_specs, out_specs=out_specs,
                          out_shape=out_shape,
                          compiler_params=_cp(("arbitrary",) if has_acc else ("parallel",)))(*ins)


def _ssm_post_fwd(ys, u, dskip):
    t, d = ys.shape

    def body(ys_ref, u_ref, ds_ref, y_ref, z_ref):
        y = ys_ref[...].astype(F32) + ds_ref[...] * u_ref[...].astype(F32)
        y_ref[...] = y
        th = jnp.tanh(_GELU_C0 * (y + _GELU_C1 * y * y * y))
        z_ref[...] = (0.5 * y * (1.0 + th)).astype(z_ref.dtype)

    return _rowwise("ssm_post_fwd", body, [ys, u, dskip], ["row", "row", "vec"],
                    [("row", F32), ("row", BF16)], t, d)


def _glu_fwd(z, gl, bglu):
    t, d = z.shape

    def body(z_ref, gl_ref, b_ref, o_ref):
        o_ref[...] = (z_ref[...].astype(F32) * _sigmoid(gl_ref[...] + b_ref[...])).astype(o_ref.dtype)

    return _rowwise("glu_fwd", body, [z, gl, bglu], ["row", "row", "vec"], [("row", BF16)], t, d)[0]


def _glu_bwd(dgg, z, gl, bglu):
    t, d = z.shape

    def body(dg_ref, z_ref, gl_ref, b_ref, dgl_ref, dz_ref, db_ref):
        sg = _sigmoid(gl_ref[...] + b_ref[...])
        dg = dg_ref[...]
        dgl = dg * z_ref[...].astype(F32) * sg * (1.0 - sg)
        dgl_ref[...] = dgl.astype(dgl_ref.dtype)
        dz_ref[...] = dg * sg

        @pl.when(pl.program_id(0) == 0)
        def _():
            db_ref[...] = jnp.zeros_like(db_ref)

        db_ref[...] += jnp.sum(dgl, axis=0, keepdims=True)

    return _rowwise("glu_bwd", body, [dgg, z, gl, bglu], ["row", "row", "row", "vec"],
                    [("row", BF16), ("row", F32), ("acc", F32)], t, d)


def _ssm_post_bwd(dz1, dz2, y, u, dskip):
    t, d = y.shape

    def body(a_ref, b_ref, y_ref, u_ref, ds_ref, dy_ref, du_ref, dd_ref):
        yv = y_ref[...]
        inner = _GELU_C0 * (yv + _GELU_C1 * yv * yv * yv)
        th = jnp.tanh(inner)
        dgelu = 0.5 * (1.0 + th) + 0.5 * yv * (1.0 - th * th) * _GELU_C0 * (1.0 + 3.0 * _GELU_C1 * yv * yv)
        dy = (a_ref[...] + b_ref[...]) * dgelu
        dy_ref[...] = dy.astype(dy_ref.dtype)
        du_ref[...] = dy * ds_ref[...]

        @pl.when(pl.program_id(0) == 0)
        def _():
            dd_ref[...] = jnp.zeros_like(dd_ref)

        dd_ref[...] += jnp.sum(dy * u_ref[...].astype(F32), axis=0, keepdims=True)

    return _rowwise("ssm_post_bwd", body, [dz1, dz2, y, u, dskip], ["row", "row", "row", "row", "vec"],
                    [("row", BF16), ("row", F32), ("acc", F32)], t, d)


def _add_cast(a, b):
    t, d = a.shape

    def body(a_ref, b_ref, o_ref):
        o_ref[...] = (a_ref[...].astype(F32) + b_ref[...].astype(F32)).astype(o_ref.dtype)

    return _rowwise("add_cast", body, [a, b], ["row", "row"], [("row", BF16)], t, d)[0]


def _ssm_scan(e_re, e_im, lam_re, lam_im, nb):
    r, n = e_re.shape
    nc = r // nb
    cb = _tile(n, 512)

    def body(er_ref, ei_ref, lr_ref, li_ref, xr_ref, xi_ref):
        lr, li = lr_ref[...], li_ref[...]
        rid = lax.broadcasted_iota(jnp.int32, (8, 1), 0)

        def tile(i, carry):
            out = []
            for b in range(nb):
                xr, xi = carry[2 * b:2 * b + 2]
                r0 = pl.multiple_of(b * nc + i * 8, 8)
                er, ei = er_ref[pl.ds(r0, 8), :], ei_ref[pl.ds(r0, 8), :]
                outr, outi = jnp.zeros((8, cb), F32), jnp.zeros((8, cb), F32)
                for j in range(8):
                    outr = jnp.where(rid == j, xr, outr)
                    outi = jnp.where(rid == j, xi, outi)
                    xr, xi = lr * xr - li * xi + er[j:j + 1, :], li * xr + lr * xi + ei[j:j + 1, :]
                xr_ref[pl.ds(r0, 8), :] = outr
                xi_ref[pl.ds(r0, 8), :] = outi
                out += [xr, xi]
            return tuple(out)

        lax.fori_loop(0, nc // 8, tile, (jnp.zeros((1, cb), F32),) * (2 * nb))

    mat = pl.BlockSpec((r, cb), lambda j: (0, j))
    vec = pl.BlockSpec((1, cb), lambda j: (0, j))
    sds = jax.ShapeDtypeStruct((r, n), F32)
    return pl.pallas_call(body, name="ssm_scan", grid=(n // cb,), in_specs=[mat, mat, vec, vec],
                          out_specs=[mat, mat], out_shape=[sds, sds],
                          compiler_params=_cp(("parallel",)))(e_re, e_im, lam_re, lam_im)


def _ssm_scan_bwd(dxp_re, dxp_im, lam_re, lam_im, nb):
    r, n = dxp_re.shape
    nc = r // nb
    cb = _tile(n, 512)

    def body(dr_ref, di_ref, lr_ref, li_ref, er_ref, ei_ref):
        lr, li = lr_ref[...], li_ref[...]
        rid = lax.broadcasted_iota(jnp.int32, (8, 1), 0)

        def tile(i, carry):
            out = []
            for b in range(nb):
                gr, gi = carry[2 * b:2 * b + 2]
                r0 = pl.multiple_of(b * nc + (nc // 8 - 1 - i) * 8, 8)
                dr, di = dr_ref[pl.ds(r0, 8), :], di_ref[pl.ds(r0, 8), :]
                outr, outi = jnp.zeros((8, cb), F32), jnp.zeros((8, cb), F32)
                for j in range(7, -1, -1):
                    outr = jnp.where(rid == j, gr, outr)
                    outi = jnp.where(rid == j, gi, outi)
                    gr, gi = dr[j:j + 1, :] + lr * gr + li * gi, di[j:j + 1, :] + lr * gi - li * gr
                er_ref[pl.ds(r0, 8), :] = outr
                ei_ref[pl.ds(r0, 8), :] = outi
                out += [gr, gi]
            return tuple(out)

        lax.fori_loop(0, nc // 8, tile, (jnp.zeros((1, cb), F32),) * (2 * nb))

    mat = pl.BlockSpec((r, cb), lambda j: (0, j))
    vec = pl.BlockSpec((1, cb), lambda j: (0, j))
    sds = jax.ShapeDtypeStruct((r, n), F32)
    return pl.pallas_call(body, name="ssm_scan_bwd", grid=(n // cb,), in_specs=[mat, mat, vec, vec],
                          out_specs=[mat, mat], out_shape=[sds, sds],
                          compiler_params=_cp(("parallel",)))(dxp_re, dxp_im, lam_re, lam_im)


def _ssm_compact(a_re, a_im, log_dt, b_re, b_im, c_re, c_im):
    g, p = a_re.shape
    h = b_re.shape[-1]
    ln = SSM_L
    sg = LANE // h
    na = g // sg
    hp = lax.Precision.HIGHEST
    lam = lax.complex(a_re, a_im)
    ldt = lam * jnp.exp(log_dt)[:, None]
    lam_bar = jnp.exp(ldt)
    bbar = ((lam_bar - 1.0) / lam)[..., None] * lax.complex(b_re, b_im)
    cm = lax.complex(c_re, c_im)
    steps = jnp.arange(ln + 1, dtype=F32)
    pw = jnp.exp(ldt[:, None, :] * steps[None, :, None])
    kd = jnp.einsum("ghp,gdp,gpk->gdhk", cm, pw[:, :ln], bbar, precision=hp).real

    def stacked(x, rows_per, cols_per):
        x = x.reshape(na, sg, ln, rows_per, cols_per).transpose(0, 2, 1, 3, 4).reshape(na, ln, sg * rows_per, cols_per)
        return jnp.pad(x, ((0, 0), (0, 0), (0, 0), (0, LANE - cols_per)))

    wxc = (pw[:, ln - 1 - jnp.arange(ln)][:, :, :, None] * bbar[:, None]).transpose(0, 1, 3, 2)
    cpc = (cm[:, None] * pw[:, 1:ln + 1][:, :, None, :]).transpose(0, 1, 3, 2)
    lam_l = pw[:, ln]
    return (stacked(kd.transpose(0, 1, 3, 2), h, h), stacked(wxc.real, h, p), stacked(wxc.imag, h, p),
            stacked(cpc.real, p, h), stacked(-cpc.imag, p, h),
            lam_l.real.reshape(1, g * p), lam_l.imag.reshape(1, g * p))


def _ssm_masks(h, p):
    sg = LANE // h
    r128 = lax.broadcasted_iota(jnp.int32, (LANE, LANE), 0)
    c128 = lax.broadcasted_iota(jnp.int32, (LANE, LANE), 1)
    rx = lax.broadcasted_iota(jnp.int32, (LANE, sg * p), 0)
    cx = lax.broadcasted_iota(jnp.int32, (LANE, sg * p), 1)
    ry = lax.broadcasted_iota(jnp.int32, (sg * p, LANE), 0)
    cy = lax.broadcasted_iota(jnp.int32, (sg * p, LANE), 1)
    f = lambda m: m.astype(F32)
    return dict(
        spread_h=f((r128 < h) & (c128 % h == r128)),
        spread_p=f((rx < p) & (cx % p == rx)),
        gather_h=f((c128 < h) & (r128 % h == c128)),
        gather_p=f((cy < p) & (ry % p == cy)),
        same_t=f(r128 // h == c128 // h), same_x=f(rx // h == cx // p), same_y=f(ry // p == cy // h))


def _place(a, spread):
    return jnp.dot(a.astype(BF16), spread.astype(BF16), preferred_element_type=F32)


def _ssm_expand(compact):
    kt, wxr, wxi, wyr, wyi = compact
    na, ln = kt.shape[:2]
    wst = wyr.shape[2]
    h, p = SSM_H, SSM_P

    def body(kt_ref, wxr_ref, wxi_ref, wyr_ref, wyi_ref, tm_ref, xr_ref, xi_ref, yr_ref, yi_ref):
        m = _ssm_masks(h, p)
        ktb = [_place(kt_ref[lag], m["spread_h"]) * m["same_t"] for lag in range(ln)]
        zero = jnp.zeros((LANE, LANE), F32)
        for sig in range(ln):
            rows = slice(sig * LANE, (sig + 1) * LANE)
            tm_ref[rows, :] = jnp.concatenate([ktb[tau - sig] if tau >= sig else zero for tau in range(ln)],
                                              axis=1).astype(tm_ref.dtype)
            xr_ref[rows, :] = (_place(wxr_ref[sig], m["spread_p"]) * m["same_x"]).astype(xr_ref.dtype)
            xi_ref[rows, :] = (_place(wxi_ref[sig], m["spread_p"]) * m["same_x"]).astype(xi_ref.dtype)
        for tau in range(ln):
            cols = slice(tau * LANE, (tau + 1) * LANE)
            yr_ref[:, cols] = (_place(wyr_ref[tau], m["spread_h"]) * m["same_y"]).astype(yr_ref.dtype)
            yi_ref[:, cols] = (_place(wyi_ref[tau], m["spread_h"]) * m["same_y"]).astype(yi_ref.dtype)

    blk = lambda rows: pl.BlockSpec((None, ln, rows, LANE), lambda j: (j, 0, 0, 0))
    mat = lambda rows, cols: pl.BlockSpec((None, rows, cols), lambda j: (j, 0, 0))
    sds = lambda rows, cols: jax.ShapeDtypeStruct((na, rows, cols), BF16)
    wch = ln * LANE
    return pl.pallas_call(
        body, name="ssm_expand", grid=(na,), in_specs=[blk(LANE), blk(LANE), blk(LANE), blk(wst), blk(wst)],
        out_specs=[mat(wch, wch), mat(wch, wst), mat(wch, wst), mat(wst, wch), mat(wst, wch)],
        out_shape=[sds(wch, wch), sds(wch, wst), sds(wch, wst), sds(wst, wch), sds(wst, wch)],
        compiler_params=_cp(("parallel",)))(kt, wxr, wxi, wyr, wyi)


def _chunk_view(a):
    t, d = a.shape
    return a.reshape(t // SSM_L, SSM_L * d)


def _sg_specs(r4, d, wst):
    nblk = d // LANE
    cat = [pl.BlockSpec((r4, LANE), functools.partial(lambda j, tau: (0, tau * nblk + j), tau=tau))
           for tau in range(SSM_L)]
    plane = pl.BlockSpec((r4, wst), lambda j: (0, j))
    mat = lambda rows, cols: pl.BlockSpec((None, rows, cols), lambda j: (j, 0, 0))
    piece = pl.BlockSpec((r4, LANE), lambda j: (0, j))
    return cat, plane, mat, piece


def _lane_cat(refs):
    return jnp.concatenate([r[...] for r in refs], axis=1)


def _bdot(a, b, dims):
    return lax.dot_general(a.astype(BF16), b.astype(BF16), dims, preferred_element_type=F32)


def _ssm_core_fwd(u, ops, nb):
    tm, wxr, wxi, wyr, wyi, lam_re, lam_im = ops
    t, d = u.shape
    ln, na, wch, wst = SSM_L, tm.shape[0], tm.shape[1], wxr.shape[2]
    r4 = t // ln
    n = na * wst
    u4 = _chunk_view(u)
    cat, plane, mat, piece = _sg_specs(r4, d, wst)
    pds = jax.ShapeDtypeStruct((r4, n), F32)

    def states(*refs):
        ucat = _lane_cat(refs[:ln])
        wr_ref, wi_ref, er_ref, ei_ref = refs[ln:]
        er_ref[...] = _bdot(ucat, wr_ref[...], _NN)
        ei_ref[...] = _bdot(ucat, wi_ref[...], _NN)

    e_re, e_im = pl.pallas_call(
        states, name="ssm_states", grid=(na,), in_specs=cat + [mat(wch, wst)] * 2, out_specs=[plane, plane],
        out_shape=[pds, pds], compiler_params=_cp(("parallel",)))(*([u4] * ln), wxr, wxi)
    xp_re, xp_im = _ssm_scan(e_re, e_im, lam_re, lam_im, nb)

    def outputs(*refs):
        ucat = _lane_cat(refs[:ln])
        tm_ref, xr_ref, xi_ref, wr_ref, wi_ref = refs[ln:ln + 5]
        y = (_bdot(ucat, tm_ref[...], _NN) + _bdot(xr_ref[...], wr_ref[...], _NN)
             + _bdot(xi_ref[...], wi_ref[...], _NN))
        for tau, o_ref in enumerate(refs[ln + 5:]):
            o_ref[...] = y[:, tau * LANE:(tau + 1) * LANE].astype(o_ref.dtype)

    ys = pl.pallas_call(
        outputs, name="ssm_y", grid=(na,),
        in_specs=cat + [mat(wch, wch), plane, plane, mat(wst, wch), mat(wst, wch)], out_specs=[piece] * ln,
        out_shape=[jax.ShapeDtypeStruct((r4, d), BF16)] * ln,
        compiler_params=_cp(("parallel",)))(*([u4] * ln), tm, xp_re, xp_im, wyr, wyi)
    return jnp.concatenate(ys, axis=1).reshape(t, d), xp_re, xp_im


def _ssm_core_bwd(dy, u, xp_re, xp_im, ops, nb):
    tm, wxr, wxi, wyr, wyi, lam_re, lam_im = ops
    t, d = u.shape
    ln, na, wch, wst = SSM_L, tm.shape[0], tm.shape[1], wxr.shape[2]
    r4 = t // ln
    n = na * wst
    u4, dy4 = _chunk_view(u), _chunk_view(dy)
    cat, plane, mat, piece = _sg_specs(r4, d, wst)
    pds = jax.ShapeDtypeStruct((r4, n), F32)

    def dstates(*refs):
        dycat = _lane_cat(refs[:ln])
        wr_ref, wi_ref, dr_ref, di_ref = refs[ln:]
        dr_ref[...] = _bdot(dycat, wr_ref[...], _NT)
        di_ref[...] = _bdot(dycat, wi_ref[...], _NT)

    dxp_re, dxp_im = pl.pallas_call(
        dstates, name="ssm_dxp", grid=(na,), in_specs=cat + [mat(wst, wch)] * 2, out_specs=[plane, plane],
        out_shape=[pds, pds], compiler_params=_cp(("parallel",)))(*([dy4] * ln), wyr, wyi)
    de_re, de_im = _ssm_scan_bwd(dxp_re, dxp_im, lam_re, lam_im, nb)

    def dinputs(*refs):
        dycat = _lane_cat(refs[:ln])
        tm_ref, er_ref, ei_ref, wr_ref, wi_ref = refs[ln:ln + 5]
        du = (_bdot(dycat, tm_ref[...], _NT) + _bdot(er_ref[...], wr_ref[...], _NT)
              + _bdot(ei_ref[...], wi_ref[...], _NT))
        for tau, o_ref in enumerate(refs[ln + 5:]):
            o_ref[...] = du[:, tau * LANE:(tau + 1) * LANE].astype(o_ref.dtype)

    dus = pl.pallas_call(
        dinputs, name="ssm_du", grid=(na,),
        in_specs=cat + [mat(wch, wch), plane, plane, mat(wch, wst), mat(wch, wst)], out_specs=[piece] * ln,
        out_shape=[jax.ShapeDtypeStruct((r4, d), BF16)] * ln,
        compiler_params=_cp(("parallel",)))(*([dy4] * ln), tm, de_re, de_im, wxr, wxi)

    def doperators(*refs):
        ucat, dycat = _lane_cat(refs[:ln]), _lane_cat(refs[ln:2 * ln])
        (er_ref, ei_ref, xr_ref, xi_ref, dtm_ref, dwxr_ref, dwxi_ref, dwyr_ref, dwyi_ref,
         dlr_ref, dli_ref) = refs[2 * ln:]
        er, ei, xr, xi = er_ref[...], ei_ref[...], xr_ref[...], xi_ref[...]
        m = _ssm_masks(SSM_H, SSM_P)
        gather_h, gather_p = m["gather_h"].astype(BF16), m["gather_p"].astype(BF16)
        blk = lambda i: slice(i * LANE, (i + 1) * LANE)
        dtm = _bdot(ucat, dycat, _TN)
        for lag in range(ln):
            acc = dtm[blk(0), blk(lag)]
            for sig in range(1, ln - lag):
                acc = acc + dtm[blk(sig), blk(sig + lag)]
            dtm_ref[lag] = _split_dot(acc * m["same_t"], gather_h)
        for src, dst in ((er, dwxr_ref), (ei, dwxi_ref)):
            dwx = _bdot(ucat, src, _TN)
            for sig in range(ln):
                dst[sig] = _split_dot(dwx[blk(sig), :] * m["same_x"], gather_p)
        for src, dst in ((xr, dwyr_ref), (xi, dwyi_ref)):
            dwy = _bdot(src, dycat, _TN)
            for tau in range(ln):
                dst[tau] = _split_dot(dwy[:, blk(tau)] * m["same_y"], gather_h)
        dlr_ref[...] = jnp.sum(er * xr + ei * xi, axis=0, keepdims=True)
        dli_ref[...] = jnp.sum(ei * xr - er * xi, axis=0, keepdims=True)

    cblk = lambda rows: pl.BlockSpec((None, ln, rows, LANE), lambda j: (j, 0, 0, 0))
    cds = lambda rows: jax.ShapeDtypeStruct((na, ln, rows, LANE), F32)
    vec = pl.BlockSpec((1, wst), lambda j: (0, j))
    vds = jax.ShapeDtypeStruct((1, n), F32)
    d_compact = pl.pallas_call(
        doperators, name="ssm_dops", grid=(na,), in_specs=cat + cat + [plane] * 4,
        out_specs=[cblk(LANE), cblk(LANE), cblk(LANE), cblk(wst), cblk(wst), vec, vec],
        out_shape=[cds(LANE), cds(LANE), cds(LANE), cds(wst), cds(wst), vds, vds],
        compiler_params=_cp(("parallel",)))(*([u4] * ln), *([dy4] * ln), de_re, de_im, xp_re, xp_im)
    return jnp.concatenate(dus, axis=1).reshape(t, d), tuple(d_compact)


def _modfin_fwd(c_all, w_mod, w_fin):
    n, d = c_all.shape
    nl, _, cm = w_mod.shape
    cf = w_fin.shape[1]
    width = nl * cm + cf
    hp = lax.Precision.HIGHEST

    def body(c_ref, wm_ref, wf_ref, act_ref, out_ref):
        cv = c_ref[...]
        act = cv * _sigmoid(cv)
        act_ref[...] = act
        for i in range(nl):
            out_ref[:, i * cm:(i + 1) * cm] = jnp.dot(act, wm_ref[i], preferred_element_type=F32, precision=hp)
        out_ref[:, nl * cm:] = jnp.dot(act, wf_ref[...], preferred_element_type=F32, precision=hp)

    return pl.pallas_call(body, name="modfin_fwd",
                          out_shape=[jax.ShapeDtypeStruct((n, d), F32), jax.ShapeDtypeStruct((n, width), F32)],
                          compiler_params=_cp(None))(c_all, w_mod, w_fin)


def _modfin_bwd(c_act_t, dmod_loc, dfin_loc, dall):
    d, n = c_act_t.shape
    nl, _, cm = dmod_loc.shape
    cf = dfin_loc.shape[1]
    hp = lax.Precision.HIGHEST

    def body(ct_ref, dm_ref, df_ref, da_ref, gwm_ref, gwf_ref, gb_ref):
        ct = ct_ref[...]
        for i in range(nl):
            gwm_ref[i] = jnp.dot(ct, dm_ref[i], preferred_element_type=F32, precision=hp)
        gwf_ref[...] = jnp.dot(ct, df_ref[...], preferred_element_type=F32, precision=hp)
        gb_ref[...] = jnp.sum(da_ref[...], axis=0, keepdims=True)

    return pl.pallas_call(body, name="modfin_bwd",
                          out_shape=[jax.ShapeDtypeStruct((nl, d, cm), F32), jax.ShapeDtypeStruct((d, cf), F32),
                                     jax.ShapeDtypeStruct((1, dall.shape[1]), F32)],
                          compiler_params=_cp(None))(c_act_t, dmod_loc, dfin_loc, dall)


def _adamw(name, gparts, w, m, v):
    n, r, c = gparts.shape
    tr = _tile(r, 256)

    def body(gp_ref, w_ref, m_ref, v_ref, g_ref, d_ref, mo_ref, vo_ref):
        _adamw_step(gp_ref, w_ref, m_ref, v_ref, g_ref, d_ref, mo_ref, vo_ref)

    mat = pl.BlockSpec((tr, c), lambda i: (i, 0))
    sds = jax.ShapeDtypeStruct((r, c), F32)
    return pl.pallas_call(body, name=name, grid=(r // tr,),
                          in_specs=[pl.BlockSpec((n, tr, c), lambda i: (0, i, 0)), mat, mat, mat],
                          out_specs=[mat] * 4, out_shape=[sds] * 4,
                          compiler_params=_cp(("parallel",)))(gparts, w, m, v)


def _adamw_layers(name, gparts_l, w, m, v):
    nl, r, c = w.shape
    n = gparts_l[0].shape[0]
    tr = _tile(r, 256)
    nt = r // tr

    def body(*refs):
        w_ref, m_ref, v_ref = refs[nl:nl + 3]
        layer = pl.program_id(0)
        for i in range(nl):
            @pl.when(layer == i)
            def _(i=i):
                _adamw_step(refs[i], w_ref, m_ref, v_ref, *refs[nl + 3:])

    def parts(i):
        return pl.BlockSpec((n, tr, c), lambda l, t: (0, jnp.where(l == i, t, jnp.where(l < i, 0, nt - 1)), 0))

    mat = pl.BlockSpec((None, tr, c), lambda l, t: (l, t, 0))
    sds = jax.ShapeDtypeStruct((nl, r, c), F32)
    return pl.pallas_call(body, name=name, grid=(nl, nt), in_specs=[parts(i) for i in range(nl)] + [mat] * 3,
                          out_specs=[mat] * 4, out_shape=[sds] * 4,
                          compiler_params=_cp(("arbitrary", "arbitrary")))(*gparts_l, w, m, v)


def _adamw_step(gp_ref, w_ref, m_ref, v_ref, g_ref, d_ref, mo_ref, vo_ref):
    gsum = gp_ref[0].astype(F32)
    for j in range(1, gp_ref.shape[0]):
        gsum = gsum + gp_ref[j].astype(F32)
    mn = ADAM_B1 * m_ref[...] + (1.0 - ADAM_B1) * gsum
    vn = ADAM_B2 * v_ref[...] + (1.0 - ADAM_B2) * (gsum * gsum)
    g_ref[...] = gsum
    mo_ref[...] = mn
    vo_ref[...] = vn
    m_hat = mn * (1.0 / (1.0 - ADAM_B1 ** ADAM_STEP))
    v_hat = vn * (1.0 / (1.0 - ADAM_B2 ** ADAM_STEP))
    d_ref[...] = -ADAM_LR * (m_hat / (jnp.sqrt(v_hat) + ADAM_EPS) + ADAM_WD * w_ref[...])


def _adamw_many(name, entries):
    k = len(entries)

    def body(*refs):
        for i in range(k):
            _adamw_step(*refs[4 * i:4 * i + 4], *refs[4 * k + 4 * i:4 * k + 4 * i + 4])

    ops = [a for e in entries for a in e]
    out_shape = [jax.ShapeDtypeStruct(e[1].shape, F32) for e in entries for _ in range(4)]
    return pl.pallas_call(body, name=name, out_shape=out_shape, compiler_params=_cp(None))(*ops)


class _Exchange:
    def __init__(self, arrs, gathers):
        self.arrs = [pltpu.with_memory_space_constraint(a, pltpu.HBM) for a in arrs]
        self.gathers = list(gathers)
        self.n = len(arrs)
        self.out_shape = [pltpu.HBM(((N_DEV,) + a.shape) if g else a.shape, a.dtype)
                          for a, g in zip(arrs, self.gathers)]
        self.specs = [pl.BlockSpec(memory_space=pltpu.HBM)] * self.n
        self.scratch = [pltpu.SemaphoreType.DMA((self.n, N_DEV - 1)), pltpu.SemaphoreType.DMA((self.n, N_DEV - 1)),
                        pltpu.SemaphoreType.DMA((self.n,))]

    def _copies(self, ins, outs, sems):
        send_sems, recv_sems, local_sems = sems
        x, y, c = lax.axis_index("x"), lax.axis_index("y"), lax.axis_index("c")
        me = 4 * x + 2 * y + c
        local, sends, recvs = [], [], []
        for i in range(self.n):
            src_me = ins[i] if self.gathers[i] else ins[i].at[me]
            local.append(pltpu.make_async_copy(src_me, outs[i].at[me], local_sems.at[i]))
        for dd in range(1, N_DEV):
            px = jnp.bitwise_xor(x, dd >> 2)
            py = jnp.bitwise_xor(y, (dd >> 1) & 1)
            pc = jnp.bitwise_xor(c, dd & 1)
            pid = 4 * px + 2 * py + pc
            for i in range(self.n):
                src = ins[i] if self.gathers[i] else ins[i].at[pid]
                sems_i = dict(send_sem=send_sems.at[i, dd - 1], recv_sem=recv_sems.at[i, dd - 1],
                              device_id=(px, py, pc), device_id_type=MESH)
                sends.append(pltpu.make_async_remote_copy(src_ref=src, dst_ref=outs[i].at[me], **sems_i))
                recvs.append(pltpu.make_async_remote_copy(src_ref=src, dst_ref=outs[i].at[pid], **sems_i))
        return local, sends, recvs

    def start(self, ins, outs, sems):
        local, sends, _ = self._copies(ins, outs, sems)
        for cp in local + sends:
            cp.start()

    def wait(self, ins, outs, sems):
        local, sends, recvs = self._copies(ins, outs, sems)
        for cp in recvs:
            cp.wait_recv()
        for cp in sends:
            cp.wait_send()
        for cp in local:
            cp.wait()


class _NoExchange:
    n, arrs, specs, out_shape, scratch = 0, [], [], [], []

    def start(self, ins, outs, sems):
        pass

    def wait(self, ins, outs, sems):
        pass


def _exchange(name, arrs, gathers):
    ex = _Exchange(arrs, gathers)
    n = ex.n

    def body(*refs):
        ins, outs, sems = refs[:n], refs[n:2 * n], refs[2 * n:]
        ex.start(ins, outs, sems)
        ex.wait(ins, outs, sems)

    outs = pl.pallas_call(body, name=name, in_specs=ex.specs, out_specs=ex.specs, out_shape=ex.out_shape,
                          scratch_shapes=ex.scratch)(*ex.arrs)
    return list(outs)


def kernel(x, c, norm_mix, norm_ffn, w_mod, b_mod, w_qkv, w_o_attn, w_in_ssm, a_re, a_im, log_dt, b_re, b_im, c_re, c_im, d_skip, w_glu, b_glu, w_o_ssm, w_up, conv_w, conv_b, w_down, norm_out, w_fin, b_fin, loss_target, m_norm_mix, m_norm_ffn, m_w_mod, m_b_mod, m_w_qkv, m_w_o_attn, m_w_in_ssm, m_a_re, m_a_im, m_log_dt, m_b_re, m_b_im, m_c_re, m_c_im, m_d_skip, m_w_glu, m_b_glu, m_w_o_ssm, m_w_up, m_conv_w, m_conv_b, m_w_down, m_norm_out, m_w_fin, m_b_fin, v_norm_mix, v_norm_ffn, v_w_mod, v_b_mod, v_w_qkv, v_w_o_attn, v_w_in_ssm, v_a_re, v_a_im, v_log_dt, v_b_re, v_b_im, v_c_re, v_c_im, v_d_skip, v_w_glu, v_b_glu, v_w_o_ssm, v_w_up, v_conv_w, v_conv_b, v_w_down, v_norm_out, v_w_fin, v_b_fin):
    nb, s, d = x.shape
    t = nb * s
    n_seq = nb * N_DEV
    me = 4 * lax.axis_index("x") + 2 * lax.axis_index("y") + lax.axis_index("c")
    cm = w_mod.shape[2]
    cf = w_fin.shape[1]
    c_up = w_up.shape[2]
    r_dn = w_down.shape[1]
    g_ssm = d // SSM_H

    wq8, c8 = _exchange("gather_first", [w_qkv[0].astype(BF16), c], [True, True])
    later = _Exchange([w_o_attn[0].astype(BF16), w_in_ssm[0].astype(BF16), w_glu[0].astype(BF16),
                       w_o_ssm[0].astype(BF16), w_up[0].astype(BF16), w_up[1].astype(BF16),
                       w_down[0].astype(BF16), w_down[1].astype(BF16), conv_w, d_skip, b_glu], [True] * 11)
    half = N_DEV // 2
    cb_l = [conv_b[i].reshape(N_DEV, 1, c_up) for i in range(2)]
    c_all = c8.reshape(n_seq, d)

    c_act, modloc = _modfin_fwd(c_all, w_mod, w_fin)
    (mod8,) = _exchange("gather_mod", [modloc], [True])
    mine = lax.dynamic_slice_in_dim(mod8, me * nb, nb, axis=1)
    mods = []
    for i in range(2):
        mi = mine[:, :, i * cm:(i + 1) * cm].transpose(1, 0, 2).reshape(nb, N_DEV * cm) + b_mod[i]
        mods.append([mi[:, j * d:(j + 1) * d].reshape(nb, 1, d) for j in range(6)])
    fin = mine[:, :, 2 * cm:].transpose(1, 0, 2).reshape(nb, N_DEV * cf) + b_fin
    sh_f, sc_f = fin[:, :d].reshape(nb, 1, d), fin[:, d:].reshape(nb, 1, d)

    row = lambda a: a.reshape(1, -1)
    x0 = x.reshape(t, d)

    def ffn_fwd(i, xin, sh, sc, gate):
        h = _norm_mod_fwd(f"ffn{i}_norm", xin, row(norm_ffn[i]), sh, sc, nb)
        up = _mm(f"ffn{i}_up", h, wup8[i], (t // tm_, N_DEV, 1),
                 pl.BlockSpec((tm_, d), lambda a, b, k: (a, 0)), pl.BlockSpec((None, d, c_up), lambda a, b, k: (b, 0, 0)),
                 pl.BlockSpec((None, tm_, c_up), lambda a, b, k: (b, a, 0)),
                 jax.ShapeDtypeStruct((N_DEV, t, c_up), BF16), _NN, (tm_, c_up))
        act = _ffn_act_fwd(f"ffn{i}_act", up, cw_l[i], cb_l[i], nb, s)
        yf = _mm(f"ffn{i}_down", act, wd4[i], (t // tm_, 1, half),
                 pl.BlockSpec((None, tm_, c_up), lambda a, b, k: (k, a, 0)),
                 pl.BlockSpec((None, c_up, d), lambda a, b, k: (k, 0, 0)),
                 pl.BlockSpec((tm_, d), lambda a, b, k: (a, 0)), jax.ShapeDtypeStruct((t, d), F32), _NN, (tm_, d))
        xout = _gate_add(f"ffn{i}_res", xin, yf, gate, nb)
        return xout, (h, up, act, yf)

    tm_ = _tile(t, 2048)
    sh1, sc1, g1, sh2, sc2, g2 = mods[0]
    h1 = _norm_mod_fwd("attn_norm", x0, row(norm_mix[0]), sh1, sc1, nb)
    cq = wq8.shape[2]
    qkv = _mm("attn_qkv", h1, wq8, (t // tm_, N_DEV, 1),
              pl.BlockSpec((tm_, d), lambda a, b, k: (a, 0)), pl.BlockSpec((None, d, cq), lambda a, b, k: (b, 0, 0)),
              pl.BlockSpec((tm_, cq), lambda a, b, k: (a, b)), jax.ShapeDtypeStruct((t, 3 * d), BF16), _NN, (tm_, cq))
    o_att, car_att, (wo8, win8, wglu8, wos8, wup8_0, wup8_1, wd8_0, wd8_1, cw8, dskip8, bglu8) = _attn_fwd(
        qkv, nb, s, d, later)
    wo = wo8.reshape(d, d)
    win = win8.reshape(d, d)
    wglu = wglu8.reshape(d, d)
    wos = wos8.reshape(d, d)
    wup8 = [wup8_0, wup8_1]
    wd4 = [wd8_0.reshape(half, 2 * r_dn, d), wd8_1.reshape(half, 2 * r_dn, d)]
    cw_l = [cw8[:, 0], cw8[:, 1]]
    dskip_f = dskip8.reshape(1, d)
    bglu_f = bglu8.reshape(1, d)
    ya = _mm_nn("attn_out", o_att, wo, F32)
    x1 = _gate_add("attn_res", x0, ya, g1, nb)
    x2, ffn0_saved = ffn_fwd(0, x1, sh2, sc2, g2)

    sh1b, sc1b, g1b, sh2b, sc2b, g2b = mods[1]
    ssm_params = (a_re[0], a_im[0], log_dt[0], b_re[0], b_im[0], c_re[0], c_im[0])
    compact, ops_vjp = jax.vjp(_ssm_compact, *ssm_params)
    ops = (*_ssm_expand(compact[:5]), compact[5], compact[6])
    h3 = _norm_mod_fwd("ssm_norm", x2, row(norm_mix[1]), sh1b, sc1b, nb)
    u = _mm_nn("ssm_in", h3, win, BF16)
    ys_core, xp_re, xp_im = _ssm_core_fwd(u, ops, nb)
    y_ssm, z_ssm = _ssm_post_fwd(ys_core, u, dskip_f)
    gl = _mm_nn("ssm_glu", z_ssm, wglu, F32)
    gg = _glu_fwd(z_ssm, gl, bglu_f)
    ys2 = _mm_nn("ssm_out", gg, wos, F32)
    x3 = _gate_add("ssm_res", x2, ys2, g1b, nb)
    x4, ffn1_saved = ffn_fwd(1, x3, sh2b, sc2b, g2b)

    dx4, g_norm_out, dsh_f, dsc_f, loss_blk = _norm_mod_bwd(
        "final_norm", None, x4, row(norm_out), sh_f, sc_f, None, loss_target.reshape(t, d), nb)
    loss = lax.psum(loss_blk[0, 0], ("x", "y", "c"))

    def ffn_bwd(i, dxo, xin, sc, gate, saved):
        h, up, act, yf = saved
        dyf, dgate = _gate_bwd(f"ffn{i}_res_bwd", dxo, yf, gate, nb)
        dact = _mm(f"ffn{i}_down_dx", dyf, wd4[i], (t // tm_, half, 1),
                   pl.BlockSpec((tm_, d), lambda a, b, k: (a, 0)), pl.BlockSpec((None, c_up, d), lambda a, b, k: (b, 0, 0)),
                   pl.BlockSpec((None, tm_, c_up), lambda a, b, k: (b, a, 0)),
                   jax.ShapeDtypeStruct((half, t, c_up), BF16), _NT, (tm_, c_up))
        tk = _tile(t, 1024)
        gwd = _mm(f"ffn{i}_down_dw", act, dyf, (half, 1, t // tk),
                  pl.BlockSpec((None, tk, c_up), lambda a, b, k: (a, k, 0)), pl.BlockSpec((tk, d), lambda a, b, k: (k, 0)),
                  pl.BlockSpec((None, c_up, d), lambda a, b, k: (a, 0, 0)),
                  jax.ShapeDtypeStruct((half, c_up, d), BF16), _TN, (c_up, d))
        dup, dcw, dcb = _ffn_act_bwd(f"ffn{i}_act_bwd", up, dact, cw_l[i], cb_l[i], nb, s)
        dh = _mm(f"ffn{i}_up_dx", dup, wup8[i], (t // tm_, 1, N_DEV),
                 pl.BlockSpec((None, tm_, c_up), lambda a, b, k: (k, a, 0)),
                 pl.BlockSpec((None, d, c_up), lambda a, b, k: (k, 0, 0)),
                 pl.BlockSpec((tm_, d), lambda a, b, k: (a, 0)), jax.ShapeDtypeStruct((t, d), F32), _NT, (tm_, d))
        gwup = _mm(f"ffn{i}_up_dw", h, dup, (1, N_DEV, t // tk),
                   pl.BlockSpec((tk, d), lambda a, b, k: (k, 0)), pl.BlockSpec((None, tk, c_up), lambda a, b, k: (b, k, 0)),
                   pl.BlockSpec((None, d, c_up), lambda a, b, k: (b, 0, 0)),
                   jax.ShapeDtypeStruct((N_DEV, d, c_up), BF16), _TN, (d, c_up))
        dxi, g_norm, dsh, dsc = _norm_mod_bwd(f"ffn{i}_norm_bwd", dh, xin, row(norm_ffn[i]), None, sc, dxo, None, nb)
        return dxi, (gwup, gwd.reshape(N_DEV, r_dn, d), dcw, dcb, g_norm, dsh, dsc, dgate)

    dx3, (gwup1, gwd1, dcw1, dcb1, g_nffn1, dsh2b, dsc2b, dg2b) = ffn_bwd(1, dx4, x3, sc2b, g2b, ffn1_saved)

    dys2, dg1b = _gate_bwd("ssm_res_bwd", dx3, ys2, g1b, nb)
    dgg = _mm_nt("ssm_out_dx", dys2, wos, F32)
    gwos = _mm_tn("ssm_out_dw", gg, dys2, BF16)
    dgl, dz1, g_bglu = _glu_bwd(dgg, z_ssm, gl, bglu_f)
    dz2 = _mm_nt("ssm_glu_dx", dgl, wglu, F32)
    gwglu = _mm_tn("ssm_glu_dw", z_ssm, dgl, BF16)
    dy_ssm, du_skip, g_dskip = _ssm_post_bwd(dz1, dz2, y_ssm, u, dskip_f)
    du_core, d_ops = _ssm_core_bwd(dy_ssm, u, xp_re, xp_im, ops, nb)
    du = _add_cast(du_core, du_skip)
    dh3 = _mm_nt("ssm_in_dx", du, win, F32)
    gwin = _mm_tn("ssm_in_dw", h3, du, BF16)
    dx2, g_nmix1, dsh1b, dsc1b = _norm_mod_bwd("ssm_norm_bwd", dh3, x2, row(norm_mix[1]), None, sc1b, dx3, None, nb)
    g_ssm_params = ops_vjp(d_ops)

    dx1, (gwup0, gwd0, dcw0, dcb0, g_nffn0, dsh2, dsc2, dg2) = ffn_bwd(0, dx2, x1, sc2, g2, ffn0_saved)

    dya, dg1 = _gate_bwd("attn_res_bwd", dx1, ya, g1, nb)
    do_att = _mm_nt("attn_out_dx", dya, wo, BF16)
    gwo = _mm_tn("attn_out_dw", o_att, dya, BF16)
    rows8 = lambda a: a.reshape(N_DEV, d // N_DEV, d)
    def two_d(w):
        shp = w.shape
        if len(shp) == 1:
            return (1, shp[0])
        if len(shp) == 2:
            return shp
        return (shp[0] * shp[1], math.prod(shp[2:]))

    ssm_w = [a_re, a_im, log_dt, b_re, b_im, c_re, c_im]
    ssm_partial = [g.reshape(two_d(w)) for g, w in zip(g_ssm_params, ssm_w)]
    early = _Exchange([rows8(gwo), rows8(gwin), rows8(gwglu), rows8(gwos), gwup0, gwup1, gwd0, gwd1] + ssm_partial,
                      [False] * 8 + [True] * 7)
    dq, dk, dv, early_res = _attn_bwd(qkv, car_att, do_att, nb, s, d, early)
    ro, rin, rglu, ros, rup0, rup1, rd0, rd1 = early_res[:8]
    ssm8 = early_res[8:]
    dqkv = jnp.concatenate([dq, dk, dv], axis=1)
    tk = _tile(t, 1024)
    gwq8 = _mm("attn_qkv_dw", h1, dqkv, (1, N_DEV, t // tk),
               pl.BlockSpec((tk, d), lambda a, b, k: (k, 0)),
               pl.BlockSpec((tk, cq), lambda a, b, k: (k, b)),
               pl.BlockSpec((None, d, cq), lambda a, b, k: (b, 0, 0)),
               jax.ShapeDtypeStruct((N_DEV, d, cq), BF16), _TN, (d, cq))
    dh1, (rq,) = _mm("attn_qkv_dx", dqkv, wq8, (t // tm_, 1, N_DEV),
                     pl.BlockSpec((tm_, cq), lambda a, b, k: (a, k)),
                     pl.BlockSpec((None, d, cq), lambda a, b, k: (k, 0, 0)),
                     pl.BlockSpec((tm_, d), lambda a, b, k: (a, 0)), jax.ShapeDtypeStruct((t, d), F32), _NT, (tm_, d),
                     ex=_Exchange([gwq8], [False]))
    dx0, g_nmix0, dsh1, dsc1 = _norm_mod_bwd("attn_norm_bwd", dh1, x0, row(norm_mix[0]), None, sc1, dx1, None, nb)
    grad_x = dx0.reshape(nb, s, d)

    dmod = [jnp.concatenate([a.reshape(nb, d) for a in grp], axis=1) for grp in
            ([dsh1, dsc1, dg1, dsh2, dsc2, dg2], [dsh1b, dsc1b, dg1b, dsh2b, dsc2b, dg2b])]
    dfin = jnp.concatenate([dsh_f.reshape(nb, d), dsc_f.reshape(nb, d)], axis=1)
    dmodfin = jnp.concatenate(dmod + [dfin], axis=1)
    flat = lambda a: a.reshape(1, -1)
    dmf8, nmix8, nffn8, nout8, cb8, dskip_g8, bglu_g8, cw_g8 = _exchange(
        "exchange_last",
        [dmodfin, jnp.concatenate([g_nmix0, g_nmix1]), jnp.concatenate([g_nffn0, g_nffn1]), g_norm_out,
         jnp.concatenate([flat(dcb0), flat(dcb1)]), g_dskip, g_bglu, jnp.stack([dcw0, dcw1])], [True] * 8)
    shard = d // N_DEV
    dskip_g8 = lax.dynamic_slice_in_dim(dskip_g8, me * shard, shard, axis=2)
    bglu_g8 = lax.dynamic_slice_in_dim(bglu_g8, me * shard, shard, axis=2)
    cw_g8 = lax.dynamic_slice_in_dim(cw_g8, me, 1, axis=2).reshape(N_DEV, 2 * 3, c_up)

    dall = dmf8.reshape(n_seq, 14 * d)
    dmod_loc = jnp.stack([lax.dynamic_slice_in_dim(dall[:, i * 6 * d:(i + 1) * 6 * d], me * cm, cm, axis=1)
                          for i in range(2)])
    dfin_loc = lax.dynamic_slice_in_dim(dall[:, 12 * d:], me * cf, cf, axis=1)
    g_w_mod, g_w_fin, g_bias = _modfin_bwd(c_act.T, dmod_loc, dfin_loc, dall)
    g_b_mod = g_bias[0, :12 * d].reshape(2, 6 * d)
    g_b_fin = g_bias[0, 12 * d:]

    def big(name, parts, w, m, v):
        shp = w.shape
        r2 = lambda a: a.reshape(-1, shp[-1])
        res = _adamw(name, parts.reshape(parts.shape[0], -1, shp[-1]), r2(w), r2(m), r2(v))
        return [a.reshape(shp) for a in res]

    upd = {}
    upd["w_mod"] = big("adamw_w_mod", g_w_mod[None], w_mod, m_w_mod, v_w_mod)
    upd["w_fin"] = big("adamw_w_fin", g_w_fin[None], w_fin, m_w_fin, v_w_fin)
    upd["w_qkv"] = big("adamw_w_qkv", rq, w_qkv, m_w_qkv, v_w_qkv)
    upd["w_o_attn"] = big("adamw_w_o_attn", ro, w_o_attn, m_w_o_attn, v_w_o_attn)
    upd["w_in_ssm"] = big("adamw_w_in_ssm", rin, w_in_ssm, m_w_in_ssm, v_w_in_ssm)
    upd["w_glu"] = big("adamw_w_glu", rglu, w_glu, m_w_glu, v_w_glu)
    upd["w_o_ssm"] = big("adamw_w_o_ssm", ros, w_o_ssm, m_w_o_ssm, v_w_o_ssm)
    upd["w_up"] = _adamw_layers("adamw_w_up", [rup0, rup1], w_up, m_w_up, v_w_up)
    upd["w_down"] = _adamw_layers("adamw_w_down", [rd0, rd1], w_down, m_w_down, v_w_down)

    small_names = ["norm_mix", "norm_ffn", "b_mod", "a_re", "a_im", "log_dt", "b_re", "b_im", "c_re", "c_im",
                   "d_skip", "b_glu", "conv_w", "conv_b", "norm_out", "b_fin"]
    small_g = [nmix8, nffn8, g_b_mod[None], *ssm8, dskip_g8, bglu_g8, cw_g8, cb8, nout8, g_b_fin[None]]
    small_w = [norm_mix, norm_ffn, b_mod, a_re, a_im, log_dt, b_re, b_im, c_re, c_im, d_skip, b_glu, conv_w, conv_b,
               norm_out, b_fin]
    small_m = [m_norm_mix, m_norm_ffn, m_b_mod, m_a_re, m_a_im, m_log_dt, m_b_re, m_b_im, m_c_re, m_c_im, m_d_skip,
               m_b_glu, m_conv_w, m_conv_b, m_norm_out, m_b_fin]
    small_v = [v_norm_mix, v_norm_ffn, v_b_mod, v_a_re, v_a_im, v_log_dt, v_b_re, v_b_im, v_c_re, v_c_im, v_d_skip,
               v_b_glu, v_conv_w, v_conv_b, v_norm_out, v_b_fin]
    entries = [(gp.reshape((gp.shape[0],) + two_d(w)), w.reshape(two_d(w)), m.reshape(two_d(w)), v.reshape(two_d(w)))
               for gp, w, m, v in zip(small_g, small_w, small_m, small_v)]
    res = _adamw_many("adamw_small", entries)
    for j, (nm, w) in enumerate(zip(small_names, small_w)):
        upd[nm] = [res[4 * j + k].reshape(w.shape) for k in range(4)]

    order = ["norm_mix", "norm_ffn", "w_mod", "b_mod", "w_qkv", "w_o_attn", "w_in_ssm", "a_re", "a_im", "log_dt",
             "b_re", "b_im", "c_re", "c_im", "d_skip", "w_glu", "b_glu", "w_o_ssm", "w_up", "conv_w", "conv_b",
             "w_down", "norm_out", "w_fin", "b_fin"]
    outs = [loss, grad_x]
    for k in range(4):
        outs += [upd[nm][k] for nm in order]
    return tuple(outs)
```

```python
import functools
import math

import jax
import jax.numpy as jnp
from jax import lax
from jax.experimental import pallas as pl
from jax.experimental.pallas import tpu as pltpu

F32 = jnp.float32
BF16 = jnp.bfloat16
MESH = pl.DeviceIdType.MESH

N_DEV = 8
HEAD_DIM = 64
ATT_BLK = 128
ATT_BQ = 256
ATT_UNROLL = 2
SSM_H = 16
SSM_P = 64
SSM_L = 4
EPS = 1e-6
ADAM_LR, ADAM_B1, ADAM_B2, ADAM_EPS, ADAM_WD, ADAM_STEP = 0.001, 0.9, 0.999, 1e-08, 0.01, 10
V7X_VMEM_LIMIT = 56 * 1024 * 1024
LANE = 128

_NN = (((1,), (0,)), ((), ()))
_NT = (((1,), (1,)), ((), ()))
_TN = (((0,), (0,)), ((), ()))


def _cp(sem):
    return pltpu.CompilerParams(dimension_semantics=sem, vmem_limit_bytes=V7X_VMEM_LIMIT)


def _tile(n, pref):
    if n <= pref:
        return n
    t = pref - pref % 16
    while t >= 16:
        if n % t == 0:
            return t
        t -= 16
    return n


def _mm(name, a, b, grid, a_spec, b_spec, out_spec, out_shape, dims, acc_shape, ex=None):
    nk = grid[-1]
    kax = len(grid) - 1
    ex = ex or _NoExchange()

    def body(*refs):
        a_ref, b_ref = refs[:2]
        ex_ins = refs[2:2 + ex.n]
        o_ref = refs[2 + ex.n]
        ex_outs = refs[3 + ex.n:3 + 2 * ex.n]
        acc_ref = refs[3 + 2 * ex.n]
        sems = refs[4 + 2 * ex.n:]
        first, last = _grid_ends(grid)
        k = pl.program_id(kax)

        @pl.when(first)
        def _():
            ex.start(ex_ins, ex_outs, sems)

        @pl.when(k == 0)
        def _():
            acc_ref[...] = jnp.zeros(acc_shape, F32)

        acc_ref[...] += lax.dot_general(a_ref[...].astype(BF16), b_ref[...].astype(BF16), dims,
                                        preferred_element_type=F32)

        @pl.when(k == nk - 1)
        def _():
            o_ref[...] = acc_ref[...].astype(o_ref.dtype)

        @pl.when(last)
        def _():
            ex.wait(ex_ins, ex_outs, sems)

    sem = ("arbitrary",) * len(grid) if ex.n else ("parallel",) * kax + ("arbitrary",)
    res = pl.pallas_call(
        body, name=name, grid=grid, in_specs=[a_spec, b_spec] + ex.specs, out_specs=[out_spec] + ex.specs,
        out_shape=[out_shape] + ex.out_shape, scratch_shapes=[pltpu.VMEM(acc_shape, F32)] + ex.scratch,
        compiler_params=_cp(sem))(a, b, *ex.arrs)
    return (res[0], list(res[1:])) if ex.n else res[0]


def _mm_nn(name, a, w, out_dtype):
    m, k = a.shape
    n = w.shape[1]
    tm, tn, tk = _tile(m, 512), _tile(n, 1024), _tile(k, 1024)
    return _mm(name, a, w, (m // tm, n // tn, k // tk),
               pl.BlockSpec((tm, tk), lambda i, j, kk: (i, kk)), pl.BlockSpec((tk, tn), lambda i, j, kk: (kk, j)),
               pl.BlockSpec((tm, tn), lambda i, j, kk: (i, j)), jax.ShapeDtypeStruct((m, n), out_dtype), _NN, (tm, tn))


def _mm_nt(name, a, w, out_dtype):
    m, n = a.shape
    k = w.shape[0]
    tm, tko, tn = _tile(m, 512), _tile(k, 1024), _tile(n, 1024)
    return _mm(name, a, w, (m // tm, k // tko, n // tn),
               pl.BlockSpec((tm, tn), lambda i, j, kk: (i, kk)), pl.BlockSpec((tko, tn), lambda i, j, kk: (j, kk)),
               pl.BlockSpec((tm, tko), lambda i, j, kk: (i, j)), jax.ShapeDtypeStruct((m, k), out_dtype), _NT, (tm, tko))


def _mm_tn(name, a, b, out_dtype):
    t, m = a.shape
    n = b.shape[1]
    tm, tn, tk = _tile(m, 512), _tile(n, 1024), _tile(t, 1024)
    return _mm(name, a, b, (m // tm, n // tn, t // tk),
               pl.BlockSpec((tk, tm), lambda i, j, kk: (kk, i)), pl.BlockSpec((tk, tn), lambda i, j, kk: (kk, j)),
               pl.BlockSpec((tm, tn), lambda i, j, kk: (i, j)), jax.ShapeDtypeStruct((m, n), out_dtype), _TN, (tm, tn))


def _norm_mod_fwd(name, x, g, shift, scale, nb):
    t, d = x.shape
    s = t // nb
    tr = _tile(s, 512)
    nt = s // tr

    def body(x_ref, g_ref, sh_ref, sc_ref, h_ref):
        xv = x_ref[...]
        r = lax.rsqrt(jnp.mean(xv * xv, axis=-1, keepdims=True) + EPS)
        y = xv * r * g_ref[...]
        h_ref[...] = (y * (1.0 + sc_ref[...]) + sh_ref[...]).astype(h_ref.dtype)

    row = pl.BlockSpec((tr, d), lambda b, i: (b * nt + i, 0))
    vec = pl.BlockSpec((None, 1, d), lambda b, i: (b, 0, 0))
    return pl.pallas_call(body, name=name, grid=(nb, nt),
                          in_specs=[row, pl.BlockSpec((1, d), lambda b, i: (0, 0)), vec, vec],
                          out_specs=row, out_shape=jax.ShapeDtypeStruct((t, d), BF16),
                          compiler_params=_cp(("parallel", "parallel")))(x, g, shift, scale)


def _norm_mod_bwd(name, dh, x, g, shift, scale, dres, target, nb, branch=None):
    t, d = x.shape
    s = t // nb
    tr = _tile(s, 256)
    nt = s // tr
    final = target is not None
    n_in = 5 + 2 * (branch is not None)

    def body(*refs):
        if final:
            x_ref, g_ref, sh_ref, sc_ref, tg_ref = refs[:5]
        else:
            dh_ref, x_ref, g_ref, sc_ref, dres_ref = refs[:5]
        dx_ref, dg_ref, dsh_ref, dsc_ref = refs[n_in:n_in + 4]
        if final:
            loss_ref = refs[n_in + 4]
        b, i = pl.program_id(0), pl.program_id(1)
        xv = x_ref[...]
        gv = g_ref[...]
        r = lax.rsqrt(jnp.mean(xv * xv, axis=-1, keepdims=True) + EPS)
        nrm = xv * r
        y = nrm * gv
        one_sc = 1.0 + sc_ref[...]
        if final:
            err = y * one_sc + sh_ref[...] - tg_ref[...]
            dhv = err * (1.0 / d)
        else:
            dhv = dh_ref[...].astype(F32)
        dy = dhv * one_sc
        dn = dy * gv
        dxv = r * (dn - nrm * jnp.mean(dn * nrm, axis=-1, keepdims=True))
        dtot = dxv if final else dres_ref[...] + dxv
        dx_ref[...] = dtot

        @pl.when(i == 0)
        def _():
            dsh_ref[...] = jnp.zeros_like(dsh_ref)
            dsc_ref[...] = jnp.zeros_like(dsc_ref)

        if branch is not None:
            yb_ref, gate_ref = refs[5:7]
            dyb_ref, dgate_ref = refs[-2:]
            dyb_ref[...] = (gate_ref[...] * dtot).astype(dyb_ref.dtype)

            @pl.when(i == 0)
            def _():
                dgate_ref[...] = jnp.zeros_like(dgate_ref)

            dgate_ref[...] += jnp.sum(dtot * yb_ref[...], axis=0, keepdims=True)

        @pl.when((i == 0) & (b == 0))
        def _():
            dg_ref[...] = jnp.zeros_like(dg_ref)
            if final:
                loss_ref[...] = jnp.zeros_like(loss_ref)

        dsh_ref[...] += jnp.sum(dhv, axis=0, keepdims=True)
        dsc_ref[...] += jnp.sum(dhv * y, axis=0, keepdims=True)
        dg_ref[...] += jnp.sum(dy * nrm, axis=0, keepdims=True)
        if final:
            loss_ref[...] += (0.5 / d) * jnp.sum(err * err)

    row = pl.BlockSpec((tr, d), lambda b, i: (b * nt + i, 0))
    vec = pl.BlockSpec((None, 1, d), lambda b, i: (b, 0, 0))
    gsp = pl.BlockSpec((1, d), lambda b, i: (0, 0))
    out_specs = [row, gsp, vec, vec]
    out_shape = [jax.ShapeDtypeStruct((t, d), F32), jax.ShapeDtypeStruct((1, d), F32),
                 jax.ShapeDtypeStruct((nb, 1, d), F32), jax.ShapeDtypeStruct((nb, 1, d), F32)]
    if final:
        ins, in_specs = [x, g, shift, scale, target], [row, gsp, vec, vec, row]
        out_specs.append(pl.BlockSpec((8, LANE), lambda b, i: (0, 0)))
        out_shape.append(jax.ShapeDtypeStruct((8, LANE), F32))
    else:
        ins, in_specs = [dh, x, g, scale, dres], [row, row, gsp, vec, row]
    if branch is not None:
        ins, in_specs = ins + list(branch), in_specs + [row, vec]
        out_specs += [row, vec]
        out_shape += [jax.ShapeDtypeStruct((t, d), BF16), jax.ShapeDtypeStruct((nb, 1, d), F32)]
    return pl.pallas_call(body, name=name, grid=(nb, nt), in_specs=in_specs, out_specs=out_specs,
                          out_shape=out_shape, compiler_params=_cp(("arbitrary", "arbitrary")))(*ins)


def _gate_add(name, x, y, gate, nb):
    t, d = x.shape
    s = t // nb
    tr = _tile(s, 512)
    nt = s // tr

    def body(x_ref, y_ref, g_ref, o_ref):
        o_ref[...] = x_ref[...] + g_ref[...] * y_ref[...]

    row = pl.BlockSpec((tr, d), lambda b, i: (b * nt + i, 0))
    vec = pl.BlockSpec((None, 1, d), lambda b, i: (b, 0, 0))
    return pl.pallas_call(body, name=name, grid=(nb, nt), in_specs=[row, row, vec], out_specs=row,
                          out_shape=jax.ShapeDtypeStruct((t, d), F32),
                          compiler_params=_cp(("parallel", "parallel")))(x, y, gate)


def _res_norm_fwd(name, x, y, gate, g, shift, scale, nb):
    t, d = x.shape
    s = t // nb
    tr = _tile(s, 512)
    nt = s // tr

    def body(x_ref, y_ref, gate_ref, g_ref, sh_ref, sc_ref, xo_ref, h_ref):
        xn = x_ref[...] + gate_ref[...] * y_ref[...]
        xo_ref[...] = xn
        r = lax.rsqrt(jnp.mean(xn * xn, axis=-1, keepdims=True) + EPS)
        h_ref[...] = (xn * r * g_ref[...] * (1.0 + sc_ref[...]) + sh_ref[...]).astype(h_ref.dtype)

    row = pl.BlockSpec((tr, d), lambda b, i: (b * nt + i, 0))
    vec = pl.BlockSpec((None, 1, d), lambda b, i: (b, 0, 0))
    return pl.pallas_call(body, name=name, grid=(nb, nt),
                          in_specs=[row, row, vec, pl.BlockSpec((1, d), lambda b, i: (0, 0)), vec, vec],
                          out_specs=[row, row],
                          out_shape=[jax.ShapeDtypeStruct((t, d), F32), jax.ShapeDtypeStruct((t, d), BF16)],
                          compiler_params=_cp(("parallel", "parallel")))(x, y, gate, g, shift, scale)


def _log_sigmoid(z):
    return jnp.minimum(z, 0.0) - jnp.log(1.0 + jnp.exp(-jnp.abs(z)))


def _split_dot(v, tri):
    hi = v.astype(BF16)
    lo = (v - hi.astype(F32)).astype(BF16)
    return (jnp.dot(hi, tri, preferred_element_type=F32) + jnp.dot(lo, tri, preferred_element_type=F32))


def _grid_ends(grid):
    ids = [pl.program_id(a) for a in range(len(grid))]
    first = functools.reduce(lambda u, w: u & w, [i == 0 for i in ids])
    last = functools.reduce(lambda u, w: u & w, [i == n - 1 for i, n in zip(ids, grid)])
    return first, last


def _attn_trips(nq):
    return nq * (nq + 1) // 2


def _next_trip(qi, jj, nq):
    wrap = jj >= qi
    nqi = jnp.where(wrap, jnp.minimum(qi + 1, nq - 1), qi)
    njj = jnp.where(wrap, jnp.where(qi + 1 < nq, 0, jj), jj + 1)
    return nqi, njj


def _attn_fwd(qkv, nb, s, d, ex):
    t = nb * s
    npair = d // LANE
    bk = ATT_BLK
    bq = min(ATT_BQ, s)
    nq = s // bq
    kpq = bq // bk
    nheads = LANE // HEAD_DIM
    scale = HEAD_DIM ** -0.5
    grid = (nb, npair)
    assert s // bk <= HEAD_DIM, "one carry lane per key block and head"
    assert bk == LANE, "the running sums are kept one 128-lane tile wide"
    assert kpq == ATT_UNROLL, "query block qi has exactly qi + 1 trips"

    def body(*refs):
        q_ref, k_ref, v_ref = refs[:3]
        ex_ins = refs[3:3 + ex.n]
        o_ref, car_ref = refs[3 + ex.n:5 + ex.n]
        ex_outs = refs[5 + ex.n:5 + 2 * ex.n]
        acc_s, run_s, z_s, arg_s = refs[5 + 2 * ex.n:9 + 2 * ex.n]
        sems = refs[9 + 2 * ex.n:]
        first, last = _grid_ends(grid)

        @pl.when(first)
        def _():
            ex.start(ex_ins, ex_outs, sems)

        lane = lax.broadcasted_iota(jnp.int32, (1, LANE), 1)
        row = lax.broadcasted_iota(jnp.int32, (bq, bk), 0)
        col = lax.broadcasted_iota(jnp.int32, (bq, bk), 1)
        trow = lax.broadcasted_iota(jnp.int32, (bk, bk), 0)
        tcol = lax.broadcasted_iota(jnp.int32, (bk, bk), 1)
        tri = (trow > tcol).astype(BF16)
        hms = [(lane // HEAD_DIM) == hh for hh in range(nheads)]

        def q0_of(qi):
            return pl.multiple_of(qi * bq, bq)

        def kblk_of(qi, jj, u):
            return (qi + 1) * kpq - 1 - (ATT_UNROLL * jj + u)

        def scores(qi, jj):
            q = q_ref[pl.ds(q0_of(qi), bq), :]
            qhs = [jnp.where(hm, q, jnp.zeros_like(q)) * scale for hm in hms]
            ks = [k_ref[pl.ds(pl.multiple_of(kblk_of(qi, jj, u) * bk, bk), bk), :] for u in range(ATT_UNROLL)]
            return [[lax.dot_general(qhs[hh], kj, _NT, preferred_element_type=F32) for kj in ks]
                    for hh in range(nheads)]

        def keep(zn):
            for hh in range(nheads):
                for u in range(ATT_UNROLL):
                    z_s[hh, u] = zn[hh][u]

        def exponents(qi, jj):
            q0 = q0_of(qi)
            car = car_ref[pl.ds(q0, bq), :]
            for hh in range(nheads):
                run = jnp.where(jj == 0, 0.0, run_s[hh])
                for u in range(ATT_UNROLL):
                    j = kblk_of(qi, jj, u)
                    mask = (j * bk + col) < (q0 + row)
                    z = z_s[hh, u]
                    lb = _log_sigmoid(z)
                    l1 = jnp.where(mask, lb - z, 0.0)
                    arg_s[hh, u] = jnp.where(mask, lb + (_split_dot(l1, tri) + run), -1e30)
                    car = jnp.where(lane == hh * HEAD_DIM + j, run, car)
                    run = run + jnp.sum(l1, axis=1, keepdims=True)
                run_s[hh] = run
            car_ref[pl.ds(q0, bq), :] = car

        def weigh(qi, jj):
            q0 = q0_of(qi)
            for hh in range(nheads):
                acc = None
                for u in range(ATT_UNROLL):
                    vj = v_ref[pl.ds(pl.multiple_of(kblk_of(qi, jj, u) * bk, bk), bk), :]
                    pv = jnp.dot(jnp.exp(arg_s[hh, u]).astype(BF16), vj, preferred_element_type=F32)
                    acc = pv if acc is None else acc + pv
                acc_s[hh, pl.ds(q0, bq), :] += acc

        def step(n, carry):
            qi, jj, pqi, pjj = carry
            nqi, njj = _next_trip(qi, jj, nq)
            zn = scores(nqi, njj)
            weigh(pqi, pjj)
            exponents(qi, jj)
            keep(zn)
            return nqi, njj, qi, jj

        acc_s[...] = jnp.zeros_like(acc_s)
        run_s[...] = jnp.zeros_like(run_s)
        car_ref[...] = jnp.zeros_like(car_ref)
        arg_s[...] = jnp.full(arg_s.shape, -1e30, F32)
        zero = jnp.int32(0)
        keep(scores(zero, zero))
        _, _, lqi, ljj = lax.fori_loop(0, _attn_trips(nq), step, (zero, zero, zero, zero))
        weigh(lqi, ljj)
        out = acc_s[0]
        for hh in range(1, nheads):
            out = jnp.where(hms[hh], acc_s[hh], out)
        o_ref[...] = out.astype(o_ref.dtype)

        @pl.when(last)
        def _():
            ex.wait(ex_ins, ex_outs, sems)

    seq = lambda off: pl.BlockSpec((s, LANE), lambda b, p: (b, off + p))
    res = pl.pallas_call(
        body, name="attn_fwd", grid=grid,
        in_specs=[seq(0), seq(npair), seq(2 * npair)] + ex.specs,
        out_specs=[seq(0), seq(0)] + ex.specs,
        out_shape=[jax.ShapeDtypeStruct((t, d), BF16), jax.ShapeDtypeStruct((t, d), F32)] + ex.out_shape,
        scratch_shapes=[pltpu.VMEM((nheads, s, LANE), F32), pltpu.VMEM((nheads, bq, LANE), F32),
                        pltpu.VMEM((nheads, ATT_UNROLL, bq, bk), F32),
                        pltpu.VMEM((nheads, ATT_UNROLL, bq, bk), F32)] + ex.scratch,
        compiler_params=_cp(("arbitrary", "arbitrary")))(qkv, qkv, qkv, *ex.arrs)
    return res[0], res[1], list(res[2:])


def _attn_bwd(qkv, car, do, nb, s, d, ex):
    t = nb * s
    npair = d // LANE
    bk = ATT_BLK
    bq = min(ATT_BQ, s)
    nq = s // bq
    kpq = bq // bk
    nheads = LANE // HEAD_DIM
    scale = HEAD_DIM ** -0.5
    grid = (nb, npair)
    assert kpq == ATT_UNROLL, "query block qi has exactly qi + 1 trips"

    def body(*refs):
        q_ref, k_ref, v_ref, car_ref, do_ref = refs[:5]
        ex_ins = refs[5:5 + ex.n]
        dq_ref, dk_ref, dv_ref = refs[5 + ex.n:8 + ex.n]
        ex_outs = refs[8 + ex.n:8 + 2 * ex.n]
        dk_acc, dv_acc, dq_s, rune_s, z_s, da_s, dz_s, a_s = refs[8 + 2 * ex.n:16 + 2 * ex.n]
        sems = refs[16 + 2 * ex.n:]
        first, last = _grid_ends(grid)

        @pl.when(first)
        def _():
            ex.start(ex_ins, ex_outs, sems)

        lane = lax.broadcasted_iota(jnp.int32, (1, LANE), 1)
        row = lax.broadcasted_iota(jnp.int32, (bq, bk), 0)
        col = lax.broadcasted_iota(jnp.int32, (bq, bk), 1)
        trow = lax.broadcasted_iota(jnp.int32, (bk, bk), 0)
        tcol = lax.broadcasted_iota(jnp.int32, (bk, bk), 1)
        tri_suf = (trow > tcol).astype(BF16)
        tri_pre = (trow < tcol).astype(BF16)
        hms = [(lane // HEAD_DIM) == hh for hh in range(nheads)]

        def q0_of(qi):
            return pl.multiple_of(qi * bq, bq)

        def k0_of(jj, u):
            return pl.multiple_of((ATT_UNROLL * jj + u) * bk, bk)

        def heads_of(ref, qi, factor):
            x = ref[pl.ds(q0_of(qi), bq), :]
            return [jnp.where(hm, x, jnp.zeros_like(x)) * factor for hm in hms]

        def products(qi, jj):
            qhs, dohs = heads_of(q_ref, qi, scale), heads_of(do_ref, qi, 1.0)
            ks = [k_ref[pl.ds(k0_of(jj, u), bk), :] for u in range(ATT_UNROLL)]
            vs = [v_ref[pl.ds(k0_of(jj, u), bk), :] for u in range(ATT_UNROLL)]
            zn = [[lax.dot_general(qhs[hh], kj, _NT, preferred_element_type=F32) for kj in ks] for hh in range(nheads)]
            dn = [[lax.dot_general(dohs[hh], vj, _NT, preferred_element_type=F32) for vj in vs] for hh in range(nheads)]
            return zn, dn

        def keep(zn, dn):
            for hh in range(nheads):
                for u in range(ATT_UNROLL):
                    z_s[hh, u] = zn[hh][u]
                    da_s[hh, u] = dn[hh][u]

        def middle(qi, jj):
            q0 = q0_of(qi)
            car = car_ref[pl.ds(q0, bq), :]
            for hh in range(nheads):
                run_e = jnp.where(jj == 0, 0.0, rune_s[hh])
                for u in range(ATT_UNROLL):
                    j = ATT_UNROLL * jj + u
                    mask = (j * bk + col) < (q0 + row)
                    z = z_s[hh, u]
                    lb = _log_sigmoid(z)
                    l1u = lb - z
                    l1 = jnp.where(mask, l1u, 0.0)
                    run = jnp.sum(jnp.where(lane == hh * HEAD_DIM + j, car, 0.0), axis=1, keepdims=True)
                    a = jnp.where(mask, jnp.exp(lb + (_split_dot(l1, tri_suf) + run)), 0.0)
                    e = da_s[hh, u] * a
                    dz = e * jnp.exp(l1u) - (_split_dot(e, tri_pre) + run_e) * jnp.exp(lb)
                    dz_s[hh, u] = jnp.where(mask, dz, 0.0).astype(BF16)
                    a_s[hh, u] = a.astype(BF16)
                    run_e = run_e + jnp.sum(e, axis=1, keepdims=True)
                rune_s[hh] = run_e

        def grads(qi, jj):
            q0 = q0_of(qi)
            qhs, dohs = heads_of(q_ref, qi, scale), heads_of(do_ref, qi, 1.0)
            dqs = [None] * nheads
            for u in range(ATT_UNROLL):
                k0 = k0_of(jj, u)
                kj = k_ref[pl.ds(k0, bk), :]
                for hh in range(nheads):
                    dzb = dz_s[hh, u]
                    dqu = jnp.dot(dzb, kj, preferred_element_type=F32)
                    dqs[hh] = dqu if dqs[hh] is None else dqs[hh] + dqu
                    dkh = lax.dot_general(dzb, qhs[hh], _TN, preferred_element_type=F32)
                    dvh = lax.dot_general(a_s[hh, u], dohs[hh], _TN, preferred_element_type=F32)
                    dk_blk = dkh if hh == 0 else dk_blk + dkh
                    dv_blk = dvh if hh == 0 else dv_blk + dvh
                dk_acc[pl.ds(k0, bk), :] += dk_blk
                dv_acc[pl.ds(k0, bk), :] += dv_blk
            for hh in range(nheads):
                dq_s[hh, pl.ds(q0, bq), :] += dqs[hh]

        def step(n, carry):
            qi, jj, pqi, pjj = carry
            nqi, njj = _next_trip(qi, jj, nq)
            zn, dn = products(nqi, njj)
            grads(pqi, pjj)
            middle(qi, jj)
            keep(zn, dn)
            return nqi, njj, qi, jj

        dk_acc[...] = jnp.zeros_like(dk_acc)
        dv_acc[...] = jnp.zeros_like(dv_acc)
        dq_s[...] = jnp.zeros_like(dq_s)
        rune_s[...] = jnp.zeros_like(rune_s)
        dz_s[...] = jnp.zeros_like(dz_s)
        a_s[...] = jnp.zeros_like(a_s)
        zero = jnp.int32(0)
        keep(*products(zero, zero))
        _, _, lqi, ljj = lax.fori_loop(0, _attn_trips(nq), step, (zero, zero, zero, zero))
        grads(lqi, ljj)
        dq_out = dq_s[0]
        for hh in range(1, nheads):
            dq_out = jnp.where(hms[hh], dq_s[hh], dq_out)
        dq_ref[...] = (dq_out * scale).astype(dq_ref.dtype)
        dk_ref[...] = dk_acc[...].astype(dk_ref.dtype)
        dv_ref[...] = dv_acc[...].astype(dv_ref.dtype)

        @pl.when(last)
        def _():
            ex.wait(ex_ins, ex_outs, sems)

    seq = lambda off: pl.BlockSpec((s, LANE), lambda b, p: (b, off + p))
    sds = jax.ShapeDtypeStruct((t, d), BF16)
    res = pl.pallas_call(
        body, name="attn_bwd", grid=grid,
        in_specs=[seq(0), seq(npair), seq(2 * npair), seq(0), seq(0)] + ex.specs,
        out_specs=[seq(0), seq(0), seq(0)] + ex.specs, out_shape=[sds, sds, sds] + ex.out_shape,
        scratch_shapes=[pltpu.VMEM((s, LANE), F32), pltpu.VMEM((s, LANE), F32),
                        pltpu.VMEM((nheads, s, LANE), F32), pltpu.VMEM((nheads, bq, LANE), F32),
                        pltpu.VMEM((nheads, ATT_UNROLL, bq, bk), F32), pltpu.VMEM((nheads, ATT_UNROLL, bq, bk), F32),
                        pltpu.VMEM((nheads, ATT_UNROLL, bq, bk), BF16),
                        pltpu.VMEM((nheads, ATT_UNROLL, bq, bk), BF16)] + ex.scratch,
        compiler_params=_cp(("arbitrary", "arbitrary")))(qkv, qkv, qkv, car, do, *ex.arrs)
    return res[0], res[1], res[2], list(res[3:])


def _conv3(u_ref, w, bias, c, r0, rc):
    x = u_ref[pl.ds(r0, rc), :].astype(F32)
    p0 = pl.multiple_of(jnp.maximum(r0 - 16, 0), 16)
    prev = u_ref[pl.ds(p0, 16), :].astype(F32)
    prev = jnp.where(c > 0, prev, 0.0)
    row = lax.broadcasted_iota(jnp.int32, (rc, 1), 0)
    s1 = jnp.where(row == 0, prev[15:16, :], pltpu.roll(x, 1, 0))
    s2 = jnp.where(row == 0, prev[14:15, :], jnp.where(row == 1, prev[15:16, :], pltpu.roll(x, 2, 0)))
    cv = w[2:3, :] * x + w[1:2, :] * s1 + w[0:1, :] * s2 + bias
    return cv, x, s1, s2


def _sigmoid(x):
    return 1.0 / (1.0 + jnp.exp(-x))


def _ffn_act_fwd(name, up8, cw8, cb8, nb, s):
    _, t, c_w = up8.shape
    rc = _tile(s, 256)
    nch = s // rc
    half = N_DEV // 2

    def body(ug_ref, uv_ref, wg_ref, wv_ref, bg_ref, bv_ref, act_ref):
        wg, wv, bg, bv = wg_ref[...], wv_ref[...], bg_ref[...], bv_ref[...]

        def chunk(c, carry):
            r0 = pl.multiple_of(c * rc, rc)
            cg = _conv3(ug_ref, wg, bg, c, r0, rc)[0]
            cv = _conv3(uv_ref, wv, bv, c, r0, rc)[0]
            act_ref[pl.ds(r0, rc), :] = (cg * _sigmoid(cg) * cv).astype(act_ref.dtype)
            return carry

        lax.fori_loop(0, nch, chunk, 0)

    def slab(off):
        return pl.BlockSpec((None, s, c_w), lambda k, b: (k + off, b, 0))

    def par(rows, off):
        return pl.BlockSpec((None, rows, c_w), lambda k, b: (k + off, 0, 0))

    return pl.pallas_call(
        body, name=name, grid=(half, nb),
        in_specs=[slab(0), slab(half), par(3, 0), par(3, half), par(1, 0), par(1, half)],
        out_specs=pl.BlockSpec((None, s, c_w), lambda k, b: (k, b, 0)),
        out_shape=jax.ShapeDtypeStruct((half, t, c_w), BF16),
        compiler_params=_cp(("parallel", "parallel")))(up8, up8, cw8, cw8, cb8, cb8)


def _ffn_act_bwd(name, up8, dact4, cw8, cb8, nb, s):
    _, t, c_w = up8.shape
    rc = _tile(s, 256)
    nch = s // rc
    half = N_DEV // 2

    def body(u_ref, da_ref, w_ref, b_ref, dup_ref, dcw_ref, dcb_ref):
        w2, b2 = w_ref[...], b_ref[...]
        row = lax.broadcasted_iota(jnp.int32, (rc, 1), 0)

        @pl.when(pl.program_id(1) == 0)
        def _():
            dcw_ref[...] = jnp.zeros_like(dcw_ref)
            dcb_ref[...] = jnp.zeros_like(dcb_ref)

        def chunk(i, carry):
            c = nch - 1 - i
            r0 = pl.multiple_of(c * rc, rc)
            convs = [_conv3(u_ref.at[h], w2[h], b2[h], c, r0, rc) for h in range(2)]
            gt, vl = convs[0][0], convs[1][0]
            da = da_ref[pl.ds(r0, rc), :].astype(F32)
            sg = _sigmoid(gt)
            dcvs = [da * vl * sg * (1.0 + gt * (1.0 - sg)), da * gt * sg]
            out = []
            for h in range(2):
                n0, n1, a0, a1, a2, ab = carry[6 * h:6 * h + 6]
                dcv, (_, x, s1, s2), w = dcvs[h], convs[h], w2[h]
                t1 = jnp.where(row == rc - 1, n0, pltpu.roll(dcv, rc - 1, 0))
                t2 = jnp.where(row == rc - 2, n0, jnp.where(row == rc - 1, n1, pltpu.roll(dcv, rc - 2, 0)))
                dup = w[2:3, :] * dcv + w[1:2, :] * t1 + w[0:1, :] * t2
                dup_ref[h, pl.ds(r0, rc), :] = dup.astype(dup_ref.dtype)
                out += [dcv[0:1, :], dcv[1:2, :],
                        a0 + jnp.sum(dcv * s2, axis=0, keepdims=True), a1 + jnp.sum(dcv * s1, axis=0, keepdims=True),
                        a2 + jnp.sum(dcv * x, axis=0, keepdims=True), ab + jnp.sum(dcv, axis=0, keepdims=True)]
            return tuple(out)

        z = jnp.zeros((1, c_w), F32)
        fin = lax.fori_loop(0, nch, chunk, (z,) * 12)
        for h in range(2):
            _, _, a0, a1, a2, ab = fin[6 * h:6 * h + 6]
            dcw_ref[h, 0:1, :] += a0
            dcw_ref[h, 1:2, :] += a1
            dcw_ref[h, 2:3, :] += a2
            dcb_ref[h] += ab

    def pair(rows, per_seq):
        return pl.BlockSpec((2, None, rows, c_w), (lambda k, b: (0, k, b, 0)) if per_seq else (lambda k, b: (0, k, 0, 0)))

    four = lambda a: a.reshape((2, half) + a.shape[1:])
    dup, dcw, dcb = pl.pallas_call(
        body, name=name, grid=(half, nb),
        in_specs=[pair(s, True), pl.BlockSpec((None, s, c_w), lambda k, b: (k, b, 0)), pair(3, False), pair(1, False)],
        out_specs=[pair(s, True), pair(3, False), pair(1, False)],
        out_shape=[jax.ShapeDtypeStruct((2, half, t, c_w), BF16), jax.ShapeDtypeStruct((2, half, 3, c_w), F32),
                   jax.ShapeDtypeStruct((2, half, 1, c_w), F32)],
        compiler_params=_cp(("parallel", "arbitrary")))(four(up8), dact4, four(cw8), four(cb8))
    return dup.reshape(N_DEV, t, c_w), dcw.reshape(N_DEV, 3, c_w), dcb.reshape(N_DEV, 1, c_w)


_GELU_C0 = math.sqrt(2.0 / math.pi)
_GELU_C1 = 0.044715


def _rowwise(name, body, ins, in_kinds, out_kinds, t, d, tr_pref=512):
    tr = _tile(t, tr_pref)
    row = pl.BlockSpec((tr, d), lambda i: (i, 0))
    vec = pl.BlockSpec((1, d), lambda i: (0, 0))
    in_specs = [row if k == "row" else vec for k in in_kinds]
    out_specs = [row if k[0] == "row" else vec for k in out_kinds]
    out_shape = [jax.ShapeDtypeStruct((t, d) if k[0] == "row" else (1, d), k[1]) for k in out_kinds]
    has_acc = any(k[0] == "acc" for k in out_kinds)
    return pl.pallas_call(body, name=name, grid=(t // tr,), in_specs=in_specs, out_specs=out_specs,
                          out_shape=out_shape,
                          compiler_params=_cp(("arbitrary",) if has_acc else ("parallel",)))(*ins)


def _ssm_post_fwd(ys, u, dskip):
    t, d = ys.shape

    def body(ys_ref, u_ref, ds_ref, y_ref, z_ref):
        y = ys_ref[...].astype(F32) + ds_ref[...] * u_ref[...].astype(F32)
        y_ref[...] = y
        th = jnp.tanh(_GELU_C0 * (y + _GELU_C1 * y * y * y))
        z_ref[...] = (0.5 * y * (1.0 + th)).astype(z_ref.dtype)

    return _rowwise("ssm_post_fwd", body, [ys, u, dskip], ["row", "row", "vec"],
                    [("row", F32), ("row", BF16)], t, d)


def _glu_fwd(z, gl, bglu):
    t, d = z.shape

    def body(z_ref, gl_ref, b_ref, o_ref):
        o_ref[...] = (z_ref[...].astype(F32) * _sigmoid(gl_ref[...] + b_ref[...])).astype(o_ref.dtype)

    return _rowwise("glu_fwd", body, [z, gl, bglu], ["row", "row", "vec"], [("row", BF16)], t, d)[0]


def _glu_bwd(dgg, z, gl, bglu):
    t, d = z.shape

    def body(dg_ref, z_ref, gl_ref, b_ref, dgl_ref, dz_ref, db_ref):
        sg = _sigmoid(gl_ref[...] + b_ref[...])
        dg = dg_ref[...]
        dgl = dg * z_ref[...].astype(F32) * sg * (1.0 - sg)
        dgl_ref[...] = dgl.astype(dgl_ref.dtype)
        dz_ref[...] = dg * sg

        @pl.when(pl.program_id(0) == 0)
        def _():
            db_ref[...] = jnp.zeros_like(db_ref)

        db_ref[...] += jnp.sum(dgl, axis=0, keepdims=True)

    return _rowwise("glu_bwd", body, [dgg, z, gl, bglu], ["row", "row", "row", "vec"],
                    [("row", BF16), ("row", F32), ("acc", F32)], t, d)


def _ssm_post_bwd(dz1, dz2, y, u, dskip):
    t, d = y.shape

    def body(a_ref, b_ref, y_ref, u_ref, ds_ref, dy_ref, du_ref, dd_ref):
        yv = y_ref[...]
        inner = _GELU_C0 * (yv + _GELU_C1 * yv * yv * yv)
        th = jnp.tanh(inner)
        dgelu = 0.5 * (1.0 + th) + 0.5 * yv * (1.0 - th * th) * _GELU_C0 * (1.0 + 3.0 * _GELU_C1 * yv * yv)
        dy = (a_ref[...] + b_ref[...]) * dgelu
        dy_ref[...] = dy.astype(dy_ref.dtype)
        du_ref[...] = dy * ds_ref[...]

        @pl.when(pl.program_id(0) == 0)
        def _():
            dd_ref[...] = jnp.zeros_like(dd_ref)

        dd_ref[...] += jnp.sum(dy * u_ref[...].astype(F32), axis=0, keepdims=True)

    return _rowwise("ssm_post_bwd", body, [dz1, dz2, y, u, dskip], ["row", "row", "row", "row", "vec"],
                    [("row", BF16), ("row", F32), ("acc", F32)], t, d)


def _add_cast(a, b):
    t, d = a.shape

    def body(a_ref, b_ref, o_ref):
        o_ref[...] = (a_ref[...].astype(F32) + b_ref[...].astype(F32)).astype(o_ref.dtype)

    return _rowwise("add_cast", body, [a, b], ["row", "row"], [("row", BF16)], t, d)[0]


def _ssm_scan(e_re, e_im, lam_re, lam_im, nb):
    r, n = e_re.shape
    nc = r // nb
    cb = _tile(n, 512)

    def body(er_ref, ei_ref, lr_ref, li_ref, xr_ref, xi_ref):
        lr, li = lr_ref[...], li_ref[...]
        rid = lax.broadcasted_iota(jnp.int32, (8, 1), 0)

        def tile(i, carry):
            out = []
            for b in range(nb):
                xr, xi = carry[2 * b:2 * b + 2]
                r0 = pl.multiple_of(b * nc + i * 8, 8)
                er, ei = er_ref[pl.ds(r0, 8), :], ei_ref[pl.ds(r0, 8), :]
                outr, outi = jnp.zeros((8, cb), F32), jnp.zeros((8, cb), F32)
                for j in range(8):
                    outr = jnp.where(rid == j, xr, outr)
                    outi = jnp.where(rid == j, xi, outi)
                    xr, xi = lr * xr - li * xi + er[j:j + 1, :], li * xr + lr * xi + ei[j:j + 1, :]
                xr_ref[pl.ds(r0, 8), :] = outr
                xi_ref[pl.ds(r0, 8), :] = outi
                out += [xr, xi]
            return tuple(out)

        lax.fori_loop(0, nc // 8, tile, (jnp.zeros((1, cb), F32),) * (2 * nb))

    mat = pl.BlockSpec((r, cb), lambda j: (0, j))
    vec = pl.BlockSpec((1, cb), lambda j: (0, j))
    sds = jax.ShapeDtypeStruct((r, n), F32)
    return pl.pallas_call(body, name="ssm_scan", grid=(n // cb,), in_specs=[mat, mat, vec, vec],
                          out_specs=[mat, mat], out_shape=[sds, sds],
                          compiler_params=_cp(("parallel",)))(e_re, e_im, lam_re, lam_im)


def _ssm_scan_bwd(dxp_re, dxp_im, lam_re, lam_im, nb):
    r, n = dxp_re.shape
    nc = r // nb
    cb = _tile(n, 512)

    def body(dr_ref, di_ref, lr_ref, li_ref, er_ref, ei_ref):
        lr, li = lr_ref[...], li_ref[...]
        rid = lax.broadcasted_iota(jnp.int32, (8, 1), 0)

        def tile(i, carry):
            out = []
            for b in range(nb):
                gr, gi = carry[2 * b:2 * b + 2]
                r0 = pl.multiple_of(b * nc + (nc // 8 - 1 - i) * 8, 8)
                dr, di = dr_ref[pl.ds(r0, 8), :], di_ref[pl.ds(r0, 8), :]
                outr, outi = jnp.zeros((8, cb), F32), jnp.zeros((8, cb), F32)
                for j in range(7, -1, -1):
                    outr = jnp.where(rid == j, gr, outr)
                    outi = jnp.where(rid == j, gi, outi)
                    gr, gi = dr[j:j + 1, :] + lr * gr + li * gi, di[j:j + 1, :] + lr * gi - li * gr
                er_ref[pl.ds(r0, 8), :] = outr
                ei_ref[pl.ds(r0, 8), :] = outi
                out += [gr, gi]
            return tuple(out)

        lax.fori_loop(0, nc // 8, tile, (jnp.zeros((1, cb), F32),) * (2 * nb))

    mat = pl.BlockSpec((r, cb), lambda j: (0, j))
    vec = pl.BlockSpec((1, cb), lambda j: (0, j))
    sds = jax.ShapeDtypeStruct((r, n), F32)
    return pl.pallas_call(body, name="ssm_scan_bwd", grid=(n // cb,), in_specs=[mat, mat, vec, vec],
                          out_specs=[mat, mat], out_shape=[sds, sds],
                          compiler_params=_cp(("parallel",)))(dxp_re, dxp_im, lam_re, lam_im)


def _ssm_compact(a_re, a_im, log_dt, b_re, b_im, c_re, c_im):
    g, p = a_re.shape
    h = b_re.shape[-1]
    ln = SSM_L
    sg = LANE // h
    na = g // sg
    hp = lax.Precision.HIGHEST
    lam = lax.complex(a_re, a_im)
    ldt = lam * jnp.exp(log_dt)[:, None]
    lam_bar = jnp.exp(ldt)
    bbar = ((lam_bar - 1.0) / lam)[..., None] * lax.complex(b_re, b_im)
    cm = lax.complex(c_re, c_im)
    steps = jnp.arange(ln + 1, dtype=F32)
    pw = jnp.exp(ldt[:, None, :] * steps[None, :, None])
    kd = jnp.einsum("ghp,gdp,gpk->gdhk", cm, pw[:, :ln], bbar, precision=hp).real

    def stacked(x, rows_per, cols_per):
        x = x.reshape(na, sg, ln, rows_per, cols_per).transpose(0, 2, 1, 3, 4).reshape(na, ln, sg * rows_per, cols_per)
        return jnp.pad(x, ((0, 0), (0, 0), (0, 0), (0, LANE - cols_per)))

    wxc = (pw[:, ln - 1 - jnp.arange(ln)][:, :, :, None] * bbar[:, None]).transpose(0, 1, 3, 2)
    cpc = (cm[:, None] * pw[:, 1:ln + 1][:, :, None, :]).transpose(0, 1, 3, 2)
    lam_l = pw[:, ln]
    return (stacked(kd.transpose(0, 1, 3, 2), h, h), stacked(wxc.real, h, p), stacked(wxc.imag, h, p),
            stacked(cpc.real, p, h), stacked(-cpc.imag, p, h),
            lam_l.real.reshape(1, g * p), lam_l.imag.reshape(1, g * p))


def _ssm_masks(h, p):
    sg = LANE // h
    r128 = lax.broadcasted_iota(jnp.int32, (LANE, LANE), 0)
    c128 = lax.broadcasted_iota(jnp.int32, (LANE, LANE), 1)
    rx = lax.broadcasted_iota(jnp.int32, (LANE, sg * p), 0)
    cx = lax.broadcasted_iota(jnp.int32, (LANE, sg * p), 1)
    ry = lax.broadcasted_iota(jnp.int32, (sg * p, LANE), 0)
    cy = lax.broadcasted_iota(jnp.int32, (sg * p, LANE), 1)
    f = lambda m: m.astype(F32)
    return dict(
        spread_h=f((r128 < h) & (c128 % h == r128)),
        spread_p=f((rx < p) & (cx % p == rx)),
        gather_h=f((c128 < h) & (r128 % h == c128)),
        gather_p=f((cy < p) & (ry % p == cy)),
        same_t=f(r128 // h == c128 // h), same_x=f(rx // h == cx // p), same_y=f(ry // p == cy // h))


def _place(a, spread):
    return jnp.dot(a.astype(BF16), spread.astype(BF16), preferred_element_type=F32)


def _ssm_expand(compact):
    kt, wxr, wxi, wyr, wyi = compact
    na, ln = kt.shape[:2]
    wst = wyr.shape[2]
    h, p = SSM_H, SSM_P

    def body(kt_ref, wxr_ref, wxi_ref, wyr_ref, wyi_ref, tm_ref, xr_ref, xi_ref, yr_ref, yi_ref):
        m = _ssm_masks(h, p)
        ktb = [_place(kt_ref[lag], m["spread_h"]) * m["same_t"] for lag in range(ln)]
        zero = jnp.zeros((LANE, LANE), F32)
        for sig in range(ln):
            rows = slice(sig * LANE, (sig + 1) * LANE)
            tm_ref[rows, :] = jnp.concatenate([ktb[tau - sig] if tau >= sig else zero for tau in range(ln)],
                                              axis=1).astype(tm_ref.dtype)
            xr_ref[rows, :] = (_place(wxr_ref[sig], m["spread_p"]) * m["same_x"]).astype(xr_ref.dtype)
            xi_ref[rows, :] = (_place(wxi_ref[sig], m["spread_p"]) * m["same_x"]).astype(xi_ref.dtype)
        for tau in range(ln):
            cols = slice(tau * LANE, (tau + 1) * LANE)
            yr_ref[:, cols] = (_place(wyr_ref[tau], m["spread_h"]) * m["same_y"]).astype(yr_ref.dtype)
            yi_ref[:, cols] = (_place(wyi_ref[tau], m["spread_h"]) * m["same_y"]).astype(yi_ref.dtype)

    blk = lambda rows: pl.BlockSpec((None, ln, rows, LANE), lambda j: (j, 0, 0, 0))
    mat = lambda rows, cols: pl.BlockSpec((None, rows, cols), lambda j: (j, 0, 0))
    sds = lambda rows, cols: jax.ShapeDtypeStruct((na, rows, cols), BF16)
    wch = ln * LANE
    return pl.pallas_call(
        body, name="ssm_expand", grid=(na,), in_specs=[blk(LANE), blk(LANE), blk(LANE), blk(wst), blk(wst)],
        out_specs=[mat(wch, wch), mat(wch, wst), mat(wch, wst), mat(wst, wch), mat(wst, wch)],
        out_shape=[sds(wch, wch), sds(wch, wst), sds(wch, wst), sds(wst, wch), sds(wst, wch)],
        compiler_params=_cp(("parallel",)))(kt, wxr, wxi, wyr, wyi)


def _chunk_view(a):
    t, d = a.shape
    return a.reshape(t // SSM_L, SSM_L * d)


def _sg_specs(r4, d, wst):
    nblk = d // LANE
    cat = [pl.BlockSpec((r4, LANE), functools.partial(lambda j, tau: (0, tau * nblk + j), tau=tau))
           for tau in range(SSM_L)]
    plane = pl.BlockSpec((r4, wst), lambda j: (0, j))
    mat = lambda rows, cols: pl.BlockSpec((None, rows, cols), lambda j: (j, 0, 0))
    piece = pl.BlockSpec((r4, LANE), lambda j: (0, j))
    return cat, plane, mat, piece


def _lane_cat(refs):
    return jnp.concatenate([r[...] for r in refs], axis=1)


def _bdot(a, b, dims):
    return lax.dot_general(a.astype(BF16), b.astype(BF16), dims, preferred_element_type=F32)


def _ssm_core_fwd(u, ops, nb):
    tm, wxr, wxi, wyr, wyi, lam_re, lam_im = ops
    t, d = u.shape
    ln, na, wch, wst = SSM_L, tm.shape[0], tm.shape[1], wxr.shape[2]
    r4 = t // ln
    n = na * wst
    u4 = _chunk_view(u)
    cat, plane, mat, piece = _sg_specs(r4, d, wst)
    pds = jax.ShapeDtypeStruct((r4, n), F32)

    def states(*refs):
        ucat = _lane_cat(refs[:ln])
        wr_ref, wi_ref, er_ref, ei_ref = refs[ln:]
        er_ref[...] = _bdot(ucat, wr_ref[...], _NN)
        ei_ref[...] = _bdot(ucat, wi_ref[...], _NN)

    e_re, e_im = pl.pallas_call(
        states, name="ssm_states", grid=(na,), in_specs=cat + [mat(wch, wst)] * 2, out_specs=[plane, plane],
        out_shape=[pds, pds], compiler_params=_cp(("parallel",)))(*([u4] * ln), wxr, wxi)
    xp_re, xp_im = _ssm_scan(e_re, e_im, lam_re, lam_im, nb)

    def outputs(*refs):
        ucat = _lane_cat(refs[:ln])
        tm_ref, xr_ref, xi_ref, wr_ref, wi_ref = refs[ln:ln + 5]
        y = (_bdot(ucat, tm_ref[...], _NN) + _bdot(xr_ref[...], wr_ref[...], _NN)
             + _bdot(xi_ref[...], wi_ref[...], _NN))
        for tau, o_ref in enumerate(refs[ln + 5:]):
            o_ref[...] = y[:, tau * LANE:(tau + 1) * LANE].astype(o_ref.dtype)

    ys = pl.pallas_call(
        outputs, name="ssm_y", grid=(na,),
        in_specs=cat + [mat(wch, wch), plane, plane, mat(wst, wch), mat(wst, wch)], out_specs=[piece] * ln,
        out_shape=[jax.ShapeDtypeStruct((r4, d), BF16)] * ln,
        compiler_params=_cp(("parallel",)))(*([u4] * ln), tm, xp_re, xp_im, wyr, wyi)
    return jnp.concatenate(ys, axis=1).reshape(t, d), xp_re, xp_im


def _ssm_core_bwd(dy, u, xp_re, xp_im, ops, nb):
    tm, wxr, wxi, wyr, wyi, lam_re, lam_im = ops
    t, d = u.shape
    ln, na, wch, wst = SSM_L, tm.shape[0], tm.shape[1], wxr.shape[2]
    r4 = t // ln
    n = na * wst
    u4, dy4 = _chunk_view(u), _chunk_view(dy)
    cat, plane, mat, piece = _sg_specs(r4, d, wst)
    pds = jax.ShapeDtypeStruct((r4, n), F32)

    def dstates(*refs):
        dycat = _lane_cat(refs[:ln])
        wr_ref, wi_ref, dr_ref, di_ref = refs[ln:]
        dr_ref[...] = _bdot(dycat, wr_ref[...], _NT)
        di_ref[...] = _bdot(dycat, wi_ref[...], _NT)

    dxp_re, dxp_im = pl.pallas_call(
        dstates, name="ssm_dxp", grid=(na,), in_specs=cat + [mat(wst, wch)] * 2, out_specs=[plane, plane],
        out_shape=[pds, pds], compiler_params=_cp(("parallel",)))(*([dy4] * ln), wyr, wyi)
    de_re, de_im = _ssm_scan_bwd(dxp_re, dxp_im, lam_re, lam_im, nb)

    def dinputs(*refs):
        dycat = _lane_cat(refs[:ln])
        tm_ref, er_ref, ei_ref, wr_ref, wi_ref = refs[ln:ln + 5]
        du = (_bdot(dycat, tm_ref[...], _NT) + _bdot(er_ref[...], wr_ref[...], _NT)
              + _bdot(ei_ref[...], wi_ref[...], _NT))
        for tau, o_ref in enumerate(refs[ln + 5:]):
            o_ref[...] = du[:, tau * LANE:(tau + 1) * LANE].astype(o_ref.dtype)

    dus = pl.pallas_call(
        dinputs, name="ssm_du", grid=(na,),
        in_specs=cat + [mat(wch, wch), plane, plane, mat(wch, wst), mat(wch, wst)], out_specs=[piece] * ln,
        out_shape=[jax.ShapeDtypeStruct((r4, d), BF16)] * ln,
        compiler_params=_cp(("parallel",)))(*([dy4] * ln), tm, de_re, de_im, wxr, wxi)

    def doperators(*refs):
        ucat, dycat = _lane_cat(refs[:ln]), _lane_cat(refs[ln:2 * ln])
        (er_ref, ei_ref, xr_ref, xi_ref, dtm_ref, dwxr_ref, dwxi_ref, dwyr_ref, dwyi_ref,
         dlr_ref, dli_ref) = refs[2 * ln:]
        er, ei, xr, xi = er_ref[...], ei_ref[...], xr_ref[...], xi_ref[...]
        m = _ssm_masks(SSM_H, SSM_P)
        gather_h, gather_p = m["gather_h"].astype(BF16), m["gather_p"].astype(BF16)
        blk = lambda i: slice(i * LANE, (i + 1) * LANE)
        dtm = _bdot(ucat, dycat, _TN)
        for lag in range(ln):
            acc = dtm[blk(0), blk(lag)]
            for sig in range(1, ln - lag):
                acc = acc + dtm[blk(sig), blk(sig + lag)]
            dtm_ref[lag] = _split_dot(acc * m["same_t"], gather_h)
        for src, dst in ((er, dwxr_ref), (ei, dwxi_ref)):
            dwx = _bdot(ucat, src, _TN)
            for sig in range(ln):
                dst[sig] = _split_dot(dwx[blk(sig), :] * m["same_x"], gather_p)
        for src, dst in ((xr, dwyr_ref), (xi, dwyi_ref)):
            dwy = _bdot(src, dycat, _TN)
            for tau in range(ln):
                dst[tau] = _split_dot(dwy[:, blk(tau)] * m["same_y"], gather_h)
        dlr_ref[...] = jnp.sum(er * xr + ei * xi, axis=0, keepdims=True)
        dli_ref[...] = jnp.sum(ei * xr - er * xi, axis=0, keepdims=True)

    cblk = lambda rows: pl.BlockSpec((None, ln, rows, LANE), lambda j: (j, 0, 0, 0))
    cds = lambda rows: jax.ShapeDtypeStruct((na, ln, rows, LANE), F32)
    vec = pl.BlockSpec((1, wst), lambda j: (0, j))
    vds = jax.ShapeDtypeStruct((1, n), F32)
    d_compact = pl.pallas_call(
        doperators, name="ssm_dops", grid=(na,), in_specs=cat + cat + [plane] * 4,
        out_specs=[cblk(LANE), cblk(LANE), cblk(LANE), cblk(wst), cblk(wst), vec, vec],
        out_shape=[cds(LANE), cds(LANE), cds(LANE), cds(wst), cds(wst), vds, vds],
        compiler_params=_cp(("parallel",)))(*([u4] * ln), *([dy4] * ln), de_re, de_im, xp_re, xp_im)
    return jnp.concatenate(dus, axis=1).reshape(t, d), tuple(d_compact)


def _modfin_fwd(c_all, w_mod, w_fin):
    n, d = c_all.shape
    nl, _, cm = w_mod.shape
    cf = w_fin.shape[1]
    width = nl * cm + cf
    hp = lax.Precision.HIGHEST

    def body(c_ref, wm_ref, wf_ref, act_ref, out_ref):
        cv = c_ref[...]
        act = cv * _sigmoid(cv)
        act_ref[...] = act
        for i in range(nl):
            out_ref[:, i * cm:(i + 1) * cm] = jnp.dot(act, wm_ref[i], preferred_element_type=F32, precision=hp)
        out_ref[:, nl * cm:] = jnp.dot(act, wf_ref[...], preferred_element_type=F32, precision=hp)

    return pl.pallas_call(body, name="modfin_fwd",
                          out_shape=[jax.ShapeDtypeStruct((n, d), F32), jax.ShapeDtypeStruct((n, width), F32)],
                          compiler_params=_cp(None))(c_all, w_mod, w_fin)


def _modfin_bwd(c_act_t, dmod_loc, dfin_loc, dall):
    d, n = c_act_t.shape
    nl, _, cm = dmod_loc.shape
    cf = dfin_loc.shape[1]
    hp = lax.Precision.HIGHEST

    def body(ct_ref, dm_ref, df_ref, da_ref, gwm_ref, gwf_ref, gb_ref):
        ct = ct_ref[...]
        for i in range(nl):
            gwm_ref[i] = jnp.dot(ct, dm_ref[i], preferred_element_type=F32, precision=hp)
        gwf_ref[...] = jnp.dot(ct, df_ref[...], preferred_element_type=F32, precision=hp)
        gb_ref[...] = jnp.sum(da_ref[...], axis=0, keepdims=True)

    return pl.pallas_call(body, name="modfin_bwd",
                          out_shape=[jax.ShapeDtypeStruct((nl, d, cm), F32), jax.ShapeDtypeStruct((d, cf), F32),
                                     jax.ShapeDtypeStruct((1, dall.shape[1]), F32)],
                          compiler_params=_cp(None))(c_act_t, dmod_loc, dfin_loc, dall)


def _adamw(name, gparts, w, m, v):
    n, r, c = gparts.shape
    tr = _tile(r, 256)

    def body(gp_ref, w_ref, m_ref, v_ref, g_ref, d_ref, mo_ref, vo_ref):
        _adamw_step(gp_ref, w_ref, m_ref, v_ref, g_ref, d_ref, mo_ref, vo_ref)

    mat = pl.BlockSpec((tr, c), lambda i: (i, 0))
    sds = jax.ShapeDtypeStruct((r, c), F32)
    return pl.pallas_call(body, name=name, grid=(r // tr,),
                          in_specs=[pl.BlockSpec((n, tr, c), lambda i: (0, i, 0)), mat, mat, mat],
                          out_specs=[mat] * 4, out_shape=[sds] * 4,
                          compiler_params=_cp(("parallel",)))(gparts, w, m, v)


def _adamw_layers(name, gparts_l, w, m, v):
    nl, r, c = w.shape
    n = gparts_l[0].shape[0]
    tr = _tile(r, 256)
    nt = r // tr

    def body(*refs):
        w_ref, m_ref, v_ref = refs[nl:nl + 3]
        layer = pl.program_id(0)
        for i in range(nl):
            @pl.when(layer == i)
            def _(i=i):
                _adamw_step(refs[i], w_ref, m_ref, v_ref, *refs[nl + 3:])

    def parts(i):
        return pl.BlockSpec((n, tr, c), lambda l, t: (0, jnp.where(l == i, t, jnp.where(l < i, 0, nt - 1)), 0))

    mat = pl.BlockSpec((None, tr, c), lambda l, t: (l, t, 0))
    sds = jax.ShapeDtypeStruct((nl, r, c), F32)
    return pl.pallas_call(body, name=name, grid=(nl, nt), in_specs=[parts(i) for i in range(nl)] + [mat] * 3,
                          out_specs=[mat] * 4, out_shape=[sds] * 4,
                          compiler_params=_cp(("arbitrary", "arbitrary")))(*gparts_l, w, m, v)


def _adamw_step(gp_ref, w_ref, m_ref, v_ref, g_ref, d_ref, mo_ref, vo_ref):
    gsum = gp_ref[0].astype(F32)
    for j in range(1, gp_ref.shape[0]):
        gsum = gsum + gp_ref[j].astype(F32)
    mn = ADAM_B1 * m_ref[...] + (1.0 - ADAM_B1) * gsum
    vn = ADAM_B2 * v_ref[...] + (1.0 - ADAM_B2) * (gsum * gsum)
    g_ref[...] = gsum
    mo_ref[...] = mn
    vo_ref[...] = vn
    m_hat = mn * (1.0 / (1.0 - ADAM_B1 ** ADAM_STEP))
    v_hat = vn * (1.0 / (1.0 - ADAM_B2 ** ADAM_STEP))
    d_ref[...] = -ADAM_LR * (m_hat / (jnp.sqrt(v_hat) + ADAM_EPS) + ADAM_WD * w_ref[...])


def _adamw_many(name, entries):
    k = len(entries)

    def body(*refs):
        for i in range(k):
            _adamw_step(*refs[4 * i:4 * i + 4], *refs[4 * k + 4 * i:4 * k + 4 * i + 4])

    ops = [a for e in entries for a in e]
    out_shape = [jax.ShapeDtypeStruct(e[1].shape, F32) for e in entries for _ in range(4)]
    return pl.pallas_call(body, name=name, out_shape=out_shape, compiler_params=_cp(None))(*ops)


class _Exchange:
    def __init__(self, arrs, gathers):
        self.arrs = [pltpu.with_memory_space_constraint(a, pltpu.HBM) for a in arrs]
        self.gathers = list(gathers)
        self.n = len(arrs)
        self.out_shape = [pltpu.HBM(((N_DEV,) + a.shape) if g else a.shape, a.dtype)
                          for a, g in zip(arrs, self.gathers)]
        self.specs = [pl.BlockSpec(memory_space=pltpu.HBM)] * self.n
        self.scratch = [pltpu.SemaphoreType.DMA((self.n, N_DEV - 1)), pltpu.SemaphoreType.DMA((self.n, N_DEV - 1)),
                        pltpu.SemaphoreType.DMA((self.n,))]

    def _copies(self, ins, outs, sems):
        send_sems, recv_sems, local_sems = sems
        x, y, c = lax.axis_index("x"), lax.axis_index("y"), lax.axis_index("c")
        me = 4 * x + 2 * y + c
        local, sends, recvs = [], [], []
        for i in range(self.n):
            src_me = ins[i] if self.gathers[i] else ins[i].at[me]
            local.append(pltpu.make_async_copy(src_me, outs[i].at[me], local_sems.at[i]))
        for dd in range(1, N_DEV):
            px = jnp.bitwise_xor(x, dd >> 2)
            py = jnp.bitwise_xor(y, (dd >> 1) & 1)
            pc = jnp.bitwise_xor(c, dd & 1)
            pid = 4 * px + 2 * py + pc
            for i in range(self.n):
                src = ins[i] if self.gathers[i] else ins[i].at[pid]
                sems_i = dict(send_sem=send_sems.at[i, dd - 1], recv_sem=recv_sems.at[i, dd - 1],
                              device_id=(px, py, pc), device_id_type=MESH)
                sends.append(pltpu.make_async_remote_copy(src_ref=src, dst_ref=outs[i].at[me], **sems_i))
                recvs.append(pltpu.make_async_remote_copy(src_ref=src, dst_ref=outs[i].at[pid], **sems_i))
        return local, sends, recvs

    def start(self, ins, outs, sems):
        local, sends, _ = self._copies(ins, outs, sems)
        for cp in local + sends:
            cp.start()

    def wait(self, ins, outs, sems):
        local, sends, recvs = self._copies(ins, outs, sems)
        for cp in recvs:
            cp.wait_recv()
        for cp in sends:
            cp.wait_send()
        for cp in local:
            cp.wait()


class _NoExchange:
    n, arrs, specs, out_shape, scratch = 0, [], [], [], []

    def start(self, ins, outs, sems):
        pass

    def wait(self, ins, outs, sems):
        pass


def _exchange(name, arrs, gathers):
    ex = _Exchange(arrs, gathers)
    n = ex.n

    def body(*refs):
        ins, outs, sems = refs[:n], refs[n:2 * n], refs[2 * n:]
        ex.start(ins, outs, sems)
        ex.wait(ins, outs, sems)

    outs = pl.pallas_call(body, name=name, in_specs=ex.specs, out_specs=ex.specs, out_shape=ex.out_shape,
                          scratch_shapes=ex.scratch)(*ex.arrs)
    return list(outs)


def kernel(x, c, norm_mix, norm_ffn, w_mod, b_mod, w_qkv, w_o_attn, w_in_ssm, a_re, a_im, log_dt, b_re, b_im, c_re, c_im, d_skip, w_glu, b_glu, w_o_ssm, w_up, conv_w, conv_b, w_down, norm_out, w_fin, b_fin, loss_target, m_norm_mix, m_norm_ffn, m_w_mod, m_b_mod, m_w_qkv, m_w_o_attn, m_w_in_ssm, m_a_re, m_a_im, m_log_dt, m_b_re, m_b_im, m_c_re, m_c_im, m_d_skip, m_w_glu, m_b_glu, m_w_o_ssm, m_w_up, m_conv_w, m_conv_b, m_w_down, m_norm_out, m_w_fin, m_b_fin, v_norm_mix, v_norm_ffn, v_w_mod, v_b_mod, v_w_qkv, v_w_o_attn, v_w_in_ssm, v_a_re, v_a_im, v_log_dt, v_b_re, v_b_im, v_c_re, v_c_im, v_d_skip, v_w_glu, v_b_glu, v_w_o_ssm, v_w_up, v_conv_w, v_conv_b, v_w_down, v_norm_out, v_w_fin, v_b_fin):
    nb, s, d = x.shape
    t = nb * s
    n_seq = nb * N_DEV
    me = 4 * lax.axis_index("x") + 2 * lax.axis_index("y") + lax.axis_index("c")
    cm = w_mod.shape[2]
    cf = w_fin.shape[1]
    c_up = w_up.shape[2]
    r_dn = w_down.shape[1]
    g_ssm = d // SSM_H

    wq8, c8 = _exchange("gather_first", [w_qkv[0].astype(BF16), c], [True, True])
    later = _Exchange([w_o_attn[0].astype(BF16), w_in_ssm[0].astype(BF16), w_glu[0].astype(BF16),
                       w_o_ssm[0].astype(BF16), w_up[0].astype(BF16), w_up[1].astype(BF16),
                       w_down[0].astype(BF16), w_down[1].astype(BF16), conv_w, d_skip, b_glu], [True] * 11)
    half = N_DEV // 2
    cb_l = [conv_b[i].reshape(N_DEV, 1, c_up) for i in range(2)]
    c_all = c8.reshape(n_seq, d)

    c_act, modloc = _modfin_fwd(c_all, w_mod, w_fin)
    (mod8,) = _exchange("gather_mod", [modloc], [True])
    mine = lax.dynamic_slice_in_dim(mod8, me * nb, nb, axis=1)
    mods = []
    for i in range(2):
        mi = mine[:, :, i * cm:(i + 1) * cm].transpose(1, 0, 2).reshape(nb, N_DEV * cm) + b_mod[i]
        mods.append([mi[:, j * d:(j + 1) * d].reshape(nb, 1, d) for j in range(6)])
    fin = mine[:, :, 2 * cm:].transpose(1, 0, 2).reshape(nb, N_DEV * cf) + b_fin
    sh_f, sc_f = fin[:, :d].reshape(nb, 1, d), fin[:, d:].reshape(nb, 1, d)

    row = lambda a: a.reshape(1, -1)
    x0 = x.reshape(t, d)

    def ffn_fwd(i, h):
        up = _mm(f"ffn{i}_up", h, wup8[i], (t // tm_, N_DEV, 1),
                 pl.BlockSpec((tm_, d), lambda a, b, k: (a, 0)), pl.BlockSpec((None, d, c_up), lambda a, b, k: (b, 0, 0)),
                 pl.BlockSpec((None, tm_, c_up), lambda a, b, k: (b, a, 0)),
                 jax.ShapeDtypeStruct((N_DEV, t, c_up), BF16), _NN, (tm_, c_up))
        act = _ffn_act_fwd(f"ffn{i}_act", up, cw_l[i], cb_l[i], nb, s)
        yf = _mm(f"ffn{i}_down", act, wd4[i], (t // tm_, 1, half),
                 pl.BlockSpec((None, tm_, c_up), lambda a, b, k: (k, a, 0)),
                 pl.BlockSpec((None, c_up, d), lambda a, b, k: (k, 0, 0)),
                 pl.BlockSpec((tm_, d), lambda a, b, k: (a, 0)), jax.ShapeDtypeStruct((t, d), F32), _NN, (tm_, d))
        return yf, (h, up, act)

    tm_ = _tile(t, 2048)
    sh1, sc1, g1, sh2, sc2, g2 = mods[0]
    h1 = _norm_mod_fwd("attn_norm", x0, row(norm_mix[0]), sh1, sc1, nb)
    cq = wq8.shape[2]
    qkv = _mm("attn_qkv", h1, wq8, (t // tm_, N_DEV, 1),
              pl.BlockSpec((tm_, d), lambda a, b, k: (a, 0)), pl.BlockSpec((None, d, cq), lambda a, b, k: (b, 0, 0)),
              pl.BlockSpec((tm_, cq), lambda a, b, k: (a, b)), jax.ShapeDtypeStruct((t, 3 * d), BF16), _NN, (tm_, cq))
    o_att, car_att, (wo8, win8, wglu8, wos8, wup8_0, wup8_1, wd8_0, wd8_1, cw8, dskip8, bglu8) = _attn_fwd(
        qkv, nb, s, d, later)
    wo = wo8.reshape(d, d)
    win = win8.reshape(d, d)
    wglu = wglu8.reshape(d, d)
    wos = wos8.reshape(d, d)
    wup8 = [wup8_0, wup8_1]
    wd4 = [wd8_0.reshape(half, 2 * r_dn, d), wd8_1.reshape(half, 2 * r_dn, d)]
    cw_l = [cw8[:, 0], cw8[:, 1]]
    dskip_f = dskip8.reshape(1, d)
    bglu_f = bglu8.reshape(1, d)
    ya = _mm_nn("attn_out", o_att, wo, F32)
    sh1b, sc1b, g1b, sh2b, sc2b, g2b = mods[1]
    x1, h2 = _res_norm_fwd("attn_res", x0, ya, g1, row(norm_ffn[0]), sh2, sc2, nb)
    yf0, ffn0_saved = ffn_fwd(0, h2)
    x2, h3 = _res_norm_fwd("ffn0_res", x1, yf0, g2, row(norm_mix[1]), sh1b, sc1b, nb)

    ssm_params = (a_re[0], a_im[0], log_dt[0], b_re[0], b_im[0], c_re[0], c_im[0])
    compact, ops_vjp = jax.vjp(_ssm_compact, *ssm_params)
    ops = (*_ssm_expand(compact[:5]), compact[5], compact[6])
    u = _mm_nn("ssm_in", h3, win, BF16)
    ys_core, xp_re, xp_im = _ssm_core_fwd(u, ops, nb)
    y_ssm, z_ssm = _ssm_post_fwd(ys_core, u, dskip_f)
    gl = _mm_nn("ssm_glu", z_ssm, wglu, F32)
    gg = _glu_fwd(z_ssm, gl, bglu_f)
    ys2 = _mm_nn("ssm_out", gg, wos, F32)
    x3, h4 = _res_norm_fwd("ssm_res", x2, ys2, g1b, row(norm_ffn[1]), sh2b, sc2b, nb)
    yf1, ffn1_saved = ffn_fwd(1, h4)
    x4 = _gate_add("ffn1_res", x3, yf1, g2b, nb)

    dx4, g_norm_out, dsh_f, dsc_f, loss_blk, dyf1, dg2b = _norm_mod_bwd(
        "final_norm", None, x4, row(norm_out), sh_f, sc_f, None, loss_target.reshape(t, d), nb, branch=(yf1, g2b))
    loss = lax.psum(loss_blk[0, 0], ("x", "y", "c"))

    def ffn_bwd(i, dyf, dxo, xin, sc, saved, branch):
        h, up, act = saved
        dact = _mm(f"ffn{i}_down_dx", dyf, wd4[i], (t // tm_, half, 1),
                   pl.BlockSpec((tm_, d), lambda a, b, k: (a, 0)), pl.BlockSpec((None, c_up, d), lambda a, b, k: (b, 0, 0)),
                   pl.BlockSpec((None, tm_, c_up), lambda a, b, k: (b, a, 0)),
                   jax.ShapeDtypeStruct((half, t, c_up), BF16), _NT, (tm_, c_up))
        tk = _tile(t, 1024)
        gwd = _mm(f"ffn{i}_down_dw", act, dyf, (half, 1, t // tk),
                  pl.BlockSpec((None, tk, c_up), lambda a, b, k: (a, k, 0)), pl.BlockSpec((tk, d), lambda a, b, k: (k, 0)),
                  pl.BlockSpec((None, c_up, d), lambda a, b, k: (a, 0, 0)),
                  jax.ShapeDtypeStruct((half, c_up, d), BF16), _TN, (c_up, d))
        dup, dcw, dcb = _ffn_act_bwd(f"ffn{i}_act_bwd", up, dact, cw_l[i], cb_l[i], nb, s)
        dh = _mm(f"ffn{i}_up_dx", dup, wup8[i], (t // tm_, 1, N_DEV),
                 pl.BlockSpec((None, tm_, c_up), lambda a, b, k: (k, a, 0)),
                 pl.BlockSpec((None, d, c_up), lambda a, b, k: (k, 0, 0)),
                 pl.BlockSpec((tm_, d), lambda a, b, k: (a, 0)), jax.ShapeDtypeStruct((t, d), F32), _NT, (tm_, d))
        gwup = _mm(f"ffn{i}_up_dw", h, dup, (1, N_DEV, t // tk),
                   pl.BlockSpec((tk, d), lambda a, b, k: (k, 0)), pl.BlockSpec((None, tk, c_up), lambda a, b, k: (b, k, 0)),
                   pl.BlockSpec((None, d, c_up), lambda a, b, k: (b, 0, 0)),
                   jax.ShapeDtypeStruct((N_DEV, d, c_up), BF16), _TN, (d, c_up))
        dxi, g_norm, dsh, dsc, dy_branch, dgate = _norm_mod_bwd(
            f"ffn{i}_norm_bwd", dh, xin, row(norm_ffn[i]), None, sc, dxo, None, nb, branch=branch)
        return dxi, (gwup, gwd.reshape(N_DEV, r_dn, d), dcw, dcb, g_norm, dsh, dsc), dy_branch, dgate

    dx3, (gwup1, gwd1, dcw1, dcb1, g_nffn1, dsh2b, dsc2b), dys2, dg1b = ffn_bwd(
        1, dyf1, dx4, x3, sc2b, ffn1_saved, (ys2, g1b))
    dgg = _mm_nt("ssm_out_dx", dys2, wos, F32)
    gwos = _mm_tn("ssm_out_dw", gg, dys2, BF16)
    dgl, dz1, g_bglu = _glu_bwd(dgg, z_ssm, gl, bglu_f)
    dz2 = _mm_nt("ssm_glu_dx", dgl, wglu, F32)
    gwglu = _mm_tn("ssm_glu_dw", z_ssm, dgl, BF16)
    dy_ssm, du_skip, g_dskip = _ssm_post_bwd(dz1, dz2, y_ssm, u, dskip_f)
    du_core, d_ops = _ssm_core_bwd(dy_ssm, u, xp_re, xp_im, ops, nb)
    du = _add_cast(du_core, du_skip)
    dh3 = _mm_nt("ssm_in_dx", du, win, F32)
    gwin = _mm_tn("ssm_in_dw", h3, du, BF16)
    dx2, g_nmix1, dsh1b, dsc1b, dyf0, dg2 = _norm_mod_bwd(
        "ssm_norm_bwd", dh3, x2, row(norm_mix[1]), None, sc1b, dx3, None, nb, branch=(yf0, g2))
    g_ssm_params = ops_vjp(d_ops)

    dx1, (gwup0, gwd0, dcw0, dcb0, g_nffn0, dsh2, dsc2), dya, dg1 = ffn_bwd(
        0, dyf0, dx2, x1, sc2, ffn0_saved, (ya, g1))
    do_att = _mm_nt("attn_out_dx", dya, wo, BF16)
    gwo = _mm_tn("attn_out_dw", o_att, dya, BF16)
    rows8 = lambda a: a.reshape(N_DEV, d // N_DEV, d)
    def two_d(w):
        shp = w.shape
        if len(shp) == 1:
            return (1, shp[0])
        if len(shp) == 2:
            return shp
        return (shp[0] * shp[1], math.prod(shp[2:]))

    ssm_w = [a_re, a_im, log_dt, b_re, b_im, c_re, c_im]
    ssm_partial = [g.reshape(two_d(w)) for g, w in zip(g_ssm_params, ssm_w)]
    early = _Exchange([rows8(gwo), rows8(gwin), rows8(gwglu), rows8(gwos), gwup0, gwup1, gwd0, gwd1] + ssm_partial,
                      [False] * 8 + [True] * 7)
    dq, dk, dv, early_res = _attn_bwd(qkv, car_att, do_att, nb, s, d, early)
    ro, rin, rglu, ros, rup0, rup1, rd0, rd1 = early_res[:8]
    ssm8 = early_res[8:]
    dqkv = jnp.concatenate([dq, dk, dv], axis=1)
    tk = _tile(t, 1024)
    gwq8 = _mm("attn_qkv_dw", h1, dqkv, (1, N_DEV, t // tk),
               pl.BlockSpec((tk, d), lambda a, b, k: (k, 0)),
               pl.BlockSpec((tk, cq), lambda a, b, k: (k, b)),
               pl.BlockSpec((None, d, cq), lambda a, b, k: (b, 0, 0)),
               jax.ShapeDtypeStruct((N_DEV, d, cq), BF16), _TN, (d, cq))
    dh1, (rq,) = _mm("attn_qkv_dx", dqkv, wq8, (t // tm_, 1, N_DEV),
                     pl.BlockSpec((tm_, cq), lambda a, b, k: (a, k)),
                     pl.BlockSpec((None, d, cq), lambda a, b, k: (k, 0, 0)),
                     pl.BlockSpec((tm_, d), lambda a, b, k: (a, 0)), jax.ShapeDtypeStruct((t, d), F32), _NT, (tm_, d),
                     ex=_Exchange([gwq8], [False]))
    dx0, g_nmix0, dsh1, dsc1 = _norm_mod_bwd("attn_norm_bwd", dh1, x0, row(norm_mix[0]), None, sc1, dx1, None, nb)
    grad_x = dx0.reshape(nb, s, d)

    dmod = [jnp.concatenate([a.reshape(nb, d) for a in grp], axis=1) for grp in
            ([dsh1, dsc1, dg1, dsh2, dsc2, dg2], [dsh1b, dsc1b, dg1b, dsh2b, dsc2b, dg2b])]
    dfin = jnp.concatenate([dsh_f.reshape(nb, d), dsc_f.reshape(nb, d)], axis=1)
    dmodfin = jnp.concatenate(dmod + [dfin], axis=1)
    flat = lambda a: a.reshape(1, -1)
    dmf8, nmix8, nffn8, nout8, cb8, dskip_g8, bglu_g8, cw_g8 = _exchange(
        "exchange_last",
        [dmodfin, jnp.concatenate([g_nmix0, g_nmix1]), jnp.concatenate([g_nffn0, g_nffn1]), g_norm_out,
         jnp.concatenate([flat(dcb0), flat(dcb1)]), g_dskip, g_bglu, jnp.stack([dcw0, dcw1])], [True] * 8)
    shard = d // N_DEV
    dskip_g8 = lax.dynamic_slice_in_dim(dskip_g8, me * shard, shard, axis=2)
    bglu_g8 = lax.dynamic_slice_in_dim(bglu_g8, me * shard, shard, axis=2)
    cw_g8 = lax.dynamic_slice_in_dim(cw_g8, me, 1, axis=2).reshape(N_DEV, 2 * 3, c_up)

    dall = dmf8.reshape(n_seq, 14 * d)
    dmod_loc = jnp.stack([lax.dynamic_slice_in_dim(dall[:, i * 6 * d:(i + 1) * 6 * d], me * cm, cm, axis=1)
                          for i in range(2)])
    dfin_loc = lax.dynamic_slice_in_dim(dall[:, 12 * d:], me * cf, cf, axis=1)
    g_w_mod, g_w_fin, g_bias = _modfin_bwd(c_act.T, dmod_loc, dfin_loc, dall)
    g_b_mod = g_bias[0, :12 * d].reshape(2, 6 * d)
    g_b_fin = g_bias[0, 12 * d:]

    def big(name, parts, w, m, v):
        shp = w.shape
        r2 = lambda a: a.reshape(-1, shp[-1])
        res = _adamw(name, parts.reshape(parts.shape[0], -1, shp[-1]), r2(w), r2(m), r2(v))
        return [a.reshape(shp) for a in res]

    upd = {}
    upd["w_mod"] = big("adamw_w_mod", g_w_mod[None], w_mod, m_w_mod, v_w_mod)
    upd["w_fin"] = big("adamw_w_fin", g_w_fin[None], w_fin, m_w_fin, v_w_fin)
    upd["w_qkv"] = big("adamw_w_qkv", rq, w_qkv, m_w_qkv, v_w_qkv)
    upd["w_o_attn"] = big("adamw_w_o_attn", ro, w_o_attn, m_w_o_attn, v_w_o_attn)
    upd["w_in_ssm"] = big("adamw_w_in_ssm", rin, w_in_ssm, m_w_in_ssm, v_w_in_ssm)
    upd["w_glu"] = big("adamw_w_glu", rglu, w_glu, m_w_glu, v_w_glu)
    upd["w_o_ssm"] = big("adamw_w_o_ssm", ros, w_o_ssm, m_w_o_ssm, v_w_o_ssm)
    upd["w_up"] = _adamw_layers("adamw_w_up", [rup0, rup1], w_up, m_w_up, v_w_up)
    upd["w_down"] = _adamw_layers("adamw_w_down", [rd0, rd1], w_down, m_w_down, v_w_down)

    small_names = ["norm_mix", "norm_ffn", "b_mod", "a_re", "a_im", "log_dt", "b_re", "b_im", "c_re", "c_im",
                   "d_skip", "b_glu", "conv_w", "conv_b", "norm_out", "b_fin"]
    small_g = [nmix8, nffn8, g_b_mod[None], *ssm8, dskip_g8, bglu_g8, cw_g8, cb8, nout8, g_b_fin[None]]
    small_w = [norm_mix, norm_ffn, b_mod, a_re, a_im, log_dt, b_re, b_im, c_re, c_im, d_skip, b_glu, conv_w, conv_b,
               norm_out, b_fin]
    small_m = [m_norm_mix, m_norm_ffn, m_b_mod, m_a_re, m_a_im, m_log_dt, m_b_re, m_b_im, m_c_re, m_c_im, m_d_skip,
               m_b_glu, m_conv_w, m_conv_b, m_norm_out, m_b_fin]
    small_v = [v_norm_mix, v_norm_ffn, v_b_mod, v_a_re, v_a_im, v_log_dt, v_b_re, v_b_im, v_c_re, v_c_im, v_d_skip,
               v_b_glu, v_conv_w, v_conv_b, v_norm_out, v_b_fin]
    entries = [(gp.reshape((gp.shape[0],) + two_d(w)), w.reshape(two_d(w)), m.reshape(two_d(w)), v.reshape(two_d(w)))
               for gp, w, m, v in zip(small_g, small_w, small_m, small_v)]
    res = _adamw_many("adamw_small", entries)
    for j, (nm, w) in enumerate(zip(small_names, small_w)):
        upd[nm] = [res[4 * j + k].reshape(w.shape) for k in range(4)]

    order = ["norm_mix", "norm_ffn", "w_mod", "b_mod", "w_qkv", "w_o_attn", "w_in_ssm", "a_re", "a_im", "log_dt",
             "b_re", "b_im", "c_re", "c_im", "d_skip", "w_glu", "b_glu", "w_o_ssm", "w_up", "conv_w", "conv_b",
             "w_down", "norm_out", "w_fin", "b_fin"]
    outs = [loss, grad_x]
    for k in range(4):
        outs += [upd[nm][k] for nm in order]
    return tuple(outs)
```

```python
import functools
import math

import jax
import jax.numpy as jnp
from jax import lax
from jax.experimental import pallas as pl
from jax.experimental.pallas import tpu as pltpu

F32 = jnp.float32
BF16 = jnp.bfloat16
MESH = pl.DeviceIdType.MESH

N_DEV = 8
HEAD_DIM = 64
ATT_BLK = 128
ATT_BQ = 256
ATT_UNROLL = 2
SSM_H = 16
SSM_P = 64
SSM_L = 4
EPS = 1e-6
ADAM_LR, ADAM_B1, ADAM_B2, ADAM_EPS, ADAM_WD, ADAM_STEP = 0.001, 0.9, 0.999, 1e-08, 0.01, 10
V7X_VMEM_LIMIT = 56 * 1024 * 1024
LANE = 128

_NN = (((1,), (0,)), ((), ()))
_NT = (((1,), (1,)), ((), ()))
_TN = (((0,), (0,)), ((), ()))


def _cp(sem):
    return pltpu.CompilerParams(dimension_semantics=sem, vmem_limit_bytes=V7X_VMEM_LIMIT)


def _tile(n, pref):
    if n <= pref:
        return n
    t = pref - pref % 16
    while t >= 16:
        if n % t == 0:
            return t
        t -= 16
    return n


def _mm(name, a, b, grid, a_spec, b_spec, out_spec, out_shape, dims, acc_shape, ex=None):
    nk = grid[-1]
    kax = len(grid) - 1
    ex = ex or _NoExchange()

    def body(*refs):
        a_ref, b_ref = refs[:2]
        ex_ins = refs[2:2 + ex.n]
        o_ref = refs[2 + ex.n]
        ex_outs = refs[3 + ex.n:3 + 2 * ex.n]
        acc_ref = refs[3 + 2 * ex.n]
        sems = refs[4 + 2 * ex.n:]
        first, last = _grid_ends(grid)
        k = pl.program_id(kax)

        @pl.when(first)
        def _():
            ex.start(ex_ins, ex_outs, sems)

        @pl.when(k == 0)
        def _():
            acc_ref[...] = jnp.zeros(acc_shape, F32)

        acc_ref[...] += lax.dot_general(a_ref[...].astype(BF16), b_ref[...].astype(BF16), dims,
                                        preferred_element_type=F32)

        @pl.when(k == nk - 1)
        def _():
            o_ref[...] = acc_ref[...].astype(o_ref.dtype)

        @pl.when(last)
        def _():
            ex.wait(ex_ins, ex_outs, sems)

    sem = ("arbitrary",) * len(grid) if ex.n else ("parallel",) * kax + ("arbitrary",)
    res = pl.pallas_call(
        body, name=name, grid=grid, in_specs=[a_spec, b_spec] + ex.specs, out_specs=[out_spec] + ex.specs,
        out_shape=[out_shape] + ex.out_shape, scratch_shapes=[pltpu.VMEM(acc_shape, F32)] + ex.scratch,
        compiler_params=_cp(sem))(a, b, *ex.arrs)
    return (res[0], list(res[1:])) if ex.n else res[0]


def _mm_nn(name, a, w, out_dtype):
    m, k = a.shape
    n = w.shape[1]
    tm, tn, tk = _tile(m, 512), _tile(n, 1024), _tile(k, 1024)
    return _mm(name, a, w, (m // tm, n // tn, k // tk),
               pl.BlockSpec((tm, tk), lambda i, j, kk: (i, kk)), pl.BlockSpec((tk, tn), lambda i, j, kk: (kk, j)),
               pl.BlockSpec((tm, tn), lambda i, j, kk: (i, j)), jax.ShapeDtypeStruct((m, n), out_dtype), _NN, (tm, tn))


def _mm_nt(name, a, w, out_dtype):
    m, n = a.shape
    k = w.shape[0]
    tm, tko, tn = _tile(m, 512), _tile(k, 1024), _tile(n, 1024)
    return _mm(name, a, w, (m // tm, k // tko, n // tn),
               pl.BlockSpec((tm, tn), lambda i, j, kk: (i, kk)), pl.BlockSpec((tko, tn), lambda i, j, kk: (j, kk)),
               pl.BlockSpec((tm, tko), lambda i, j, kk: (i, j)), jax.ShapeDtypeStruct((m, k), out_dtype), _NT, (tm, tko))


def _mm_tn(name, a, b, out_dtype):
    t, m = a.shape
    n = b.shape[1]
    tm, tn, tk = _tile(m, 512), _tile(n, 1024), _tile(t, 1024)
    return _mm(name, a, b, (m // tm, n // tn, t // tk),
               pl.BlockSpec((tk, tm), lambda i, j, kk: (kk, i)), pl.BlockSpec((tk, tn), lambda i, j, kk: (kk, j)),
               pl.BlockSpec((tm, tn), lambda i, j, kk: (i, j)), jax.ShapeDtypeStruct((m, n), out_dtype), _TN, (tm, tn))


def _norm_mod_fwd(name, x, g, shift, scale, nb):
    t, d = x.shape
    s = t // nb
    tr = _tile(s, 512)
    nt = s // tr

    def body(x_ref, g_ref, sh_ref, sc_ref, h_ref):
        xv = x_ref[...]
        r = lax.rsqrt(jnp.mean(xv * xv, axis=-1, keepdims=True) + EPS)
        y = xv * r * g_ref[...]
        h_ref[...] = (y * (1.0 + sc_ref[...]) + sh_ref[...]).astype(h_ref.dtype)

    row = pl.BlockSpec((tr, d), lambda b, i: (b * nt + i, 0))
    vec = pl.BlockSpec((None, 1, d), lambda b, i: (b, 0, 0))
    return pl.pallas_call(body, name=name, grid=(nb, nt),
                          in_specs=[row, pl.BlockSpec((1, d), lambda b, i: (0, 0)), vec, vec],
                          out_specs=row, out_shape=jax.ShapeDtypeStruct((t, d), BF16),
                          compiler_params=_cp(("parallel", "parallel")))(x, g, shift, scale)


def _norm_mod_bwd(name, dh, x, g, shift, scale, dres, target, nb, branch=None):
    t, d = x.shape
    s = t // nb
    tr = _tile(s, 512)
    nt = s // tr
    final = target is not None
    n_in = 5 + 2 * (branch is not None)

    def body(*refs):
        if final:
            x_ref, g_ref, sh_ref, sc_ref, tg_ref = refs[:5]
        else:
            dh_ref, x_ref, g_ref, sc_ref, dres_ref = refs[:5]
        dx_ref, dg_ref, dsh_ref, dsc_ref = refs[n_in:n_in + 4]
        if final:
            loss_ref = refs[n_in + 4]
        b, i = pl.program_id(0), pl.program_id(1)
        xv = x_ref[...]
        gv = g_ref[...]
        r = lax.rsqrt(jnp.mean(xv * xv, axis=-1, keepdims=True) + EPS)
        nrm = xv * r
        y = nrm * gv
        one_sc = 1.0 + sc_ref[...]
        if final:
            err = y * one_sc + sh_ref[...] - tg_ref[...]
            dhv = err * (1.0 / d)
        else:
            dhv = dh_ref[...].astype(F32)
        dy = dhv * one_sc
        dn = dy * gv
        dxv = r * (dn - nrm * jnp.mean(dn * nrm, axis=-1, keepdims=True))
        dtot = dxv if final else dres_ref[...] + dxv
        dx_ref[...] = dtot

        @pl.when(i == 0)
        def _():
            dsh_ref[...] = jnp.zeros_like(dsh_ref)
            dsc_ref[...] = jnp.zeros_like(dsc_ref)

        if branch is not None:
            yb_ref, gate_ref = refs[5:7]
            dyb_ref, dgate_ref = refs[-2:]
            dyb_ref[...] = (gate_ref[...] * dtot).astype(dyb_ref.dtype)

            @pl.when(i == 0)
            def _():
                dgate_ref[...] = jnp.zeros_like(dgate_ref)

            dgate_ref[...] += jnp.sum(dtot * yb_ref[...], axis=0, keepdims=True)

        @pl.when((i == 0) & (b == 0))
        def _():
            dg_ref[...] = jnp.zeros_like(dg_ref)
            if final:
                loss_ref[...] = jnp.zeros_like(loss_ref)

        dsh_ref[...] += jnp.sum(dhv, axis=0, keepdims=True)
        dsc_ref[...] += jnp.sum(dhv * y, axis=0, keepdims=True)
        dg_ref[...] += jnp.sum(dy * nrm, axis=0, keepdims=True)
        if final:
            loss_ref[...] += (0.5 / d) * jnp.sum(err * err)

    row = pl.BlockSpec((tr, d), lambda b, i: (b * nt + i, 0))
    vec = pl.BlockSpec((None, 1, d), lambda b, i: (b, 0, 0))
    gsp = pl.BlockSpec((1, d), lambda b, i: (0, 0))
    out_specs = [row, gsp, vec, vec]
    out_shape = [jax.ShapeDtypeStruct((t, d), F32), jax.ShapeDtypeStruct((1, d), F32),
                 jax.ShapeDtypeStruct((nb, 1, d), F32), jax.ShapeDtypeStruct((nb, 1, d), F32)]
    if final:
        ins, in_specs = [x, g, shift, scale, target], [row, gsp, vec, vec, row]
        out_specs.append(pl.BlockSpec((8, LANE), lambda b, i: (0, 0)))
        out_shape.append(jax.ShapeDtypeStruct((8, LANE), F32))
    else:
        ins, in_specs = [dh, x, g, scale, dres], [row, row, gsp, vec, row]
    if branch is not None:
        ins, in_specs = ins + list(branch), in_specs + [row, vec]
        out_specs += [row, vec]
        out_shape += [jax.ShapeDtypeStruct((t, d), BF16), jax.ShapeDtypeStruct((nb, 1, d), F32)]
    return pl.pallas_call(body, name=name, grid=(nb, nt), in_specs=in_specs, out_specs=out_specs,
                          out_shape=out_shape, compiler_params=_cp(("arbitrary", "arbitrary")))(*ins)


def _gate_add(name, x, y, gate, nb):
    t, d = x.shape
    s = t // nb
    tr = _tile(s, 512)
    nt = s // tr

    def body(x_ref, y_ref, g_ref, o_ref):
        o_ref[...] = x_ref[...] + g_ref[...] * y_ref[...]

    row = pl.BlockSpec((tr, d), lambda b, i: (b * nt + i, 0))
    vec = pl.BlockSpec((None, 1, d), lambda b, i: (b, 0, 0))
    return pl.pallas_call(body, name=name, grid=(nb, nt), in_specs=[row, row, vec], out_specs=row,
                          out_shape=jax.ShapeDtypeStruct((t, d), F32),
                          compiler_params=_cp(("parallel", "parallel")))(x, y, gate)


def _res_norm_fwd(name, x, y, gate, g, shift, scale, nb):
    t, d = x.shape
    s = t // nb
    tr = _tile(s, 512)
    nt = s // tr

    def body(x_ref, y_ref, gate_ref, g_ref, sh_ref, sc_ref, xo_ref, h_ref):
        xn = x_ref[...] + gate_ref[...] * y_ref[...]
        xo_ref[...] = xn
        r = lax.rsqrt(jnp.mean(xn * xn, axis=-1, keepdims=True) + EPS)
        h_ref[...] = (xn * r * g_ref[...] * (1.0 + sc_ref[...]) + sh_ref[...]).astype(h_ref.dtype)

    row = pl.BlockSpec((tr, d), lambda b, i: (b * nt + i, 0))
    vec = pl.BlockSpec((None, 1, d), lambda b, i: (b, 0, 0))
    return pl.pallas_call(body, name=name, grid=(nb, nt),
                          in_specs=[row, row, vec, pl.BlockSpec((1, d), lambda b, i: (0, 0)), vec, vec],
                          out_specs=[row, row],
                          out_shape=[jax.ShapeDtypeStruct((t, d), F32), jax.ShapeDtypeStruct((t, d), BF16)],
                          compiler_params=_cp(("parallel", "parallel")))(x, y, gate, g, shift, scale)


def _log_sigmoid(z):
    return jnp.minimum(z, 0.0) - jnp.log(1.0 + jnp.exp(-jnp.abs(z)))


def _split_dot(v, tri):
    hi = v.astype(BF16)
    lo = (v - hi.astype(F32)).astype(BF16)
    return (jnp.dot(hi, tri, preferred_element_type=F32) + jnp.dot(lo, tri, preferred_element_type=F32))


def _grid_ends(grid):
    ids = [pl.program_id(a) for a in range(len(grid))]
    first = functools.reduce(lambda u, w: u & w, [i == 0 for i in ids])
    last = functools.reduce(lambda u, w: u & w, [i == n - 1 for i, n in zip(ids, grid)])
    return first, last


def _attn_trips(nq):
    return nq * (nq + 1) // 2


def _next_trip(qi, jj, nq):
    wrap = jj >= qi
    nqi = jnp.where(wrap, jnp.minimum(qi + 1, nq - 1), qi)
    njj = jnp.where(wrap, jnp.where(qi + 1 < nq, 0, jj), jj + 1)
    return nqi, njj


def _attn_fwd(qkv, nb, s, d, ex):
    t = nb * s
    npair = d // LANE
    bk = ATT_BLK
    bq = min(ATT_BQ, s)
    nq = s // bq
    kpq = bq // bk
    nheads = LANE // HEAD_DIM
    scale = HEAD_DIM ** -0.5
    grid = (nb, npair)
    assert s // bk <= HEAD_DIM, "one carry lane per key block and head"
    assert bk == LANE, "the running sums are kept one 128-lane tile wide"
    assert kpq == ATT_UNROLL, "query block qi has exactly qi + 1 trips"

    def body(*refs):
        q_ref, k_ref, v_ref = refs[:3]
        ex_ins = refs[3:3 + ex.n]
        o_ref, car_ref = refs[3 + ex.n:5 + ex.n]
        ex_outs = refs[5 + ex.n:5 + 2 * ex.n]
        acc_s, run_s, z_s, arg_s = refs[5 + 2 * ex.n:9 + 2 * ex.n]
        sems = refs[9 + 2 * ex.n:]
        first, last = _grid_ends(grid)

        @pl.when(first)
        def _():
            ex.start(ex_ins, ex_outs, sems)

        lane = lax.broadcasted_iota(jnp.int32, (1, LANE), 1)
        row = lax.broadcasted_iota(jnp.int32, (bq, bk), 0)
        col = lax.broadcasted_iota(jnp.int32, (bq, bk), 1)
        trow = lax.broadcasted_iota(jnp.int32, (bk, bk), 0)
        tcol = lax.broadcasted_iota(jnp.int32, (bk, bk), 1)
        tri = (trow > tcol).astype(BF16)
        hms = [(lane // HEAD_DIM) == hh for hh in range(nheads)]

        def q0_of(qi):
            return pl.multiple_of(qi * bq, bq)

        def kblk_of(qi, jj, u):
            return (qi + 1) * kpq - 1 - (ATT_UNROLL * jj + u)

        def scores(qi, jj):
            q = q_ref[pl.ds(q0_of(qi), bq), :]
            qhs = [jnp.where(hm, q, jnp.zeros_like(q)) * scale for hm in hms]
            ks = [k_ref[pl.ds(pl.multiple_of(kblk_of(qi, jj, u) * bk, bk), bk), :] for u in range(ATT_UNROLL)]
            return [[lax.dot_general(qhs[hh], kj, _NT, preferred_element_type=F32) for kj in ks]
                    for hh in range(nheads)]

        def keep(zn):
            for hh in range(nheads):
                for u in range(ATT_UNROLL):
                    z_s[hh, u] = zn[hh][u]

        def exponents(qi, jj):
            q0 = q0_of(qi)
            car = car_ref[pl.ds(q0, bq), :]
            for hh in range(nheads):
                run = jnp.where(jj == 0, 0.0, run_s[hh])
                for u in range(ATT_UNROLL):
                    j = kblk_of(qi, jj, u)
                    mask = (j * bk + col) < (q0 + row)
                    z = z_s[hh, u]
                    lb = _log_sigmoid(z)
                    l1 = jnp.where(mask, lb - z, 0.0)
                    arg_s[hh, u] = jnp.where(mask, lb + (_split_dot(l1, tri) + run), -1e30)
                    car = jnp.where(lane == hh * HEAD_DIM + j, run, car)
                    run = run + jnp.sum(l1, axis=1, keepdims=True)
                run_s[hh] = run
            car_ref[pl.ds(q0, bq), :] = car

        def weigh(qi, jj):
            q0 = q0_of(qi)
            for hh in range(nheads):
                acc = None
                for u in range(ATT_UNROLL):
                    vj = v_ref[pl.ds(pl.multiple_of(kblk_of(qi, jj, u) * bk, bk), bk), :]
                    pv = jnp.dot(jnp.exp(arg_s[hh, u]).astype(BF16), vj, preferred_element_type=F32)
                    acc = pv if acc is None else acc + pv
                acc_s[hh, pl.ds(q0, bq), :] += acc

        def step(n, carry):
            qi, jj, pqi, pjj = carry
            nqi, njj = _next_trip(qi, jj, nq)
            zn = scores(nqi, njj)
            weigh(pqi, pjj)
            exponents(qi, jj)
            keep(zn)
            return nqi, njj, qi, jj

        acc_s[...] = jnp.zeros_like(acc_s)
        run_s[...] = jnp.zeros_like(run_s)
        car_ref[...] = jnp.zeros_like(car_ref)
        arg_s[...] = jnp.full(arg_s.shape, -1e30, F32)
        zero = jnp.int32(0)
        keep(scores(zero, zero))
        _, _, lqi, ljj = lax.fori_loop(0, _attn_trips(nq), step, (zero, zero, zero, zero))
        weigh(lqi, ljj)
        out = acc_s[0]
        for hh in range(1, nheads):
            out = jnp.where(hms[hh], acc_s[hh], out)
        o_ref[...] = out.astype(o_ref.dtype)

        @pl.when(last)
        def _():
            ex.wait(ex_ins, ex_outs, sems)

    seq = lambda off: pl.BlockSpec((s, LANE), lambda b, p: (b, off + p))
    res = pl.pallas_call(
        body, name="attn_fwd", grid=grid,
        in_specs=[seq(0), seq(npair), seq(2 * npair)] + ex.specs,
        out_specs=[seq(0), seq(0)] + ex.specs,
        out_shape=[jax.ShapeDtypeStruct((t, d), BF16), jax.ShapeDtypeStruct((t, d), F32)] + ex.out_shape,
        scratch_shapes=[pltpu.VMEM((nheads, s, LANE), F32), pltpu.VMEM((nheads, bq, LANE), F32),
                        pltpu.VMEM((nheads, ATT_UNROLL, bq, bk), F32),
                        pltpu.VMEM((nheads, ATT_UNROLL, bq, bk), F32)] + ex.scratch,
        compiler_params=_cp(("arbitrary", "arbitrary")))(qkv, qkv, qkv, *ex.arrs)
    return res[0], res[1], list(res[2:])


def _attn_bwd(qkv, car, do, nb, s, d, ex):
    t = nb * s
    npair = d // LANE
    bk = ATT_BLK
    bq = min(ATT_BQ, s)
    nq = s // bq
    kpq = bq // bk
    nheads = LANE // HEAD_DIM
    scale = HEAD_DIM ** -0.5
    grid = (nb, npair)
    assert kpq == ATT_UNROLL, "query block qi has exactly qi + 1 trips"

    def body(*refs):
        q_ref, k_ref, v_ref, car_ref, do_ref = refs[:5]
        ex_ins = refs[5:5 + ex.n]
        dq_ref, dk_ref, dv_ref = refs[5 + ex.n:8 + ex.n]
        ex_outs = refs[8 + ex.n:8 + 2 * ex.n]
        dk_acc, dv_acc, dq_s, rune_s, z_s, da_s, dz_s, a_s = refs[8 + 2 * ex.n:16 + 2 * ex.n]
        sems = refs[16 + 2 * ex.n:]
        first, last = _grid_ends(grid)

        @pl.when(first)
        def _():
            ex.start(ex_ins, ex_outs, sems)

        lane = lax.broadcasted_iota(jnp.int32, (1, LANE), 1)
        row = lax.broadcasted_iota(jnp.int32, (bq, bk), 0)
        col = lax.broadcasted_iota(jnp.int32, (bq, bk), 1)
        trow = lax.broadcasted_iota(jnp.int32, (bk, bk), 0)
        tcol = lax.broadcasted_iota(jnp.int32, (bk, bk), 1)
        tri_suf = (trow > tcol).astype(BF16)
        tri_pre = (trow < tcol).astype(BF16)
        hms = [(lane // HEAD_DIM) == hh for hh in range(nheads)]

        def q0_of(qi):
            return pl.multiple_of(qi * bq, bq)

        def k0_of(jj, u):
            return pl.multiple_of((ATT_UNROLL * jj + u) * bk, bk)

        def heads_of(ref, qi, factor):
            x = ref[pl.ds(q0_of(qi), bq), :]
            return [jnp.where(hm, x, jnp.zeros_like(x)) * factor for hm in hms]

        def products(qi, jj):
            qhs, dohs = heads_of(q_ref, qi, scale), heads_of(do_ref, qi, 1.0)
            ks = [k_ref[pl.ds(k0_of(jj, u), bk), :] for u in range(ATT_UNROLL)]
            vs = [v_ref[pl.ds(k0_of(jj, u), bk), :] for u in range(ATT_UNROLL)]
            zn = [[lax.dot_general(qhs[hh], kj, _NT, preferred_element_type=F32) for kj in ks] for hh in range(nheads)]
            dn = [[lax.dot_general(dohs[hh], vj, _NT, preferred_element_type=F32) for vj in vs] for hh in range(nheads)]
            return zn, dn

        def keep(zn, dn):
            for hh in range(nheads):
                for u in range(ATT_UNROLL):
                    z_s[hh, u] = zn[hh][u]
                    da_s[hh, u] = dn[hh][u]

        def middle(qi, jj):
            q0 = q0_of(qi)
            car = car_ref[pl.ds(q0, bq), :]
            for hh in range(nheads):
                run_e = jnp.where(jj == 0, 0.0, rune_s[hh])
                for u in range(ATT_UNROLL):
                    j = ATT_UNROLL * jj + u
                    mask = (j * bk + col) < (q0 + row)
                    z = z_s[hh, u]
                    lb = _log_sigmoid(z)
                    l1u = lb - z
                    l1 = jnp.where(mask, l1u, 0.0)
                    run = jnp.sum(jnp.where(lane == hh * HEAD_DIM + j, car, 0.0), axis=1, keepdims=True)
                    a = jnp.where(mask, jnp.exp(lb + (_split_dot(l1, tri_suf) + run)), 0.0)
                    e = da_s[hh, u] * a
                    dz = e * jnp.exp(l1u) - (_split_dot(e, tri_pre) + run_e) * jnp.exp(lb)
                    dz_s[hh, u] = jnp.where(mask, dz, 0.0).astype(BF16)
                    a_s[hh, u] = a.astype(BF16)
                    run_e = run_e + jnp.sum(e, axis=1, keepdims=True)
                rune_s[hh] = run_e

        def grads(qi, jj):
            q0 = q0_of(qi)
            qhs, dohs = heads_of(q_ref, qi, scale), heads_of(do_ref, qi, 1.0)
            dqs = [None] * nheads
            for u in range(ATT_UNROLL):
                k0 = k0_of(jj, u)
                kj = k_ref[pl.ds(k0, bk), :]
                for hh in range(nheads):
                    dzb = dz_s[hh, u]
                    dqu = jnp.dot(dzb, kj, preferred_element_type=F32)
                    dqs[hh] = dqu if dqs[hh] is None else dqs[hh] + dqu
                    dkh = lax.dot_general(dzb, qhs[hh], _TN, preferred_element_type=F32)
                    dvh = lax.dot_general(a_s[hh, u], dohs[hh], _TN, preferred_element_type=F32)
                    dk_blk = dkh if hh == 0 else dk_blk + dkh
                    dv_blk = dvh if hh == 0 else dv_blk + dvh
                dk_acc[pl.ds(k0, bk), :] += dk_blk
                dv_acc[pl.ds(k0, bk), :] += dv_blk
            for hh in range(nheads):
                dq_s[hh, pl.ds(q0, bq), :] += dqs[hh]

        def step(n, carry):
            qi, jj, pqi, pjj = carry
            nqi, njj = _next_trip(qi, jj, nq)
            zn, dn = products(nqi, njj)
            grads(pqi, pjj)
            middle(qi, jj)
            keep(zn, dn)
            return nqi, njj, qi, jj

        dk_acc[...] = jnp.zeros_like(dk_acc)
        dv_acc[...] = jnp.zeros_like(dv_acc)
        dq_s[...] = jnp.zeros_like(dq_s)
        rune_s[...] = jnp.zeros_like(rune_s)
        dz_s[...] = jnp.zeros_like(dz_s)
        a_s[...] = jnp.zeros_like(a_s)
        zero = jnp.int32(0)
        keep(*products(zero, zero))
        _, _, lqi, ljj = lax.fori_loop(0, _attn_trips(nq), step, (zero, zero, zero, zero))
        grads(lqi, ljj)
        dq_out = dq_s[0]
        for hh in range(1, nheads):
            dq_out = jnp.where(hms[hh], dq_s[hh], dq_out)
        dq_ref[...] = (dq_out * scale).astype(dq_ref.dtype)
        dk_ref[...] = dk_acc[...].astype(dk_ref.dtype)
        dv_ref[...] = dv_acc[...].astype(dv_ref.dtype)

        @pl.when(last)
        def _():
            ex.wait(ex_ins, ex_outs, sems)

    seq = lambda off: pl.BlockSpec((s, LANE), lambda b, p: (b, off + p))
    sds = jax.ShapeDtypeStruct((t, d), BF16)
    res = pl.pallas_call(
        body, name="attn_bwd", grid=grid,
        in_specs=[seq(0), seq(npair), seq(2 * npair), seq(0), seq(0)] + ex.specs,
        out_specs=[seq(0), seq(0), seq(0)] + ex.specs, out_shape=[sds, sds, sds] + ex.out_shape,
        scratch_shapes=[pltpu.VMEM((s, LANE), F32), pltpu.VMEM((s, LANE), F32),
                        pltpu.VMEM((nheads, s, LANE), F32), pltpu.VMEM((nheads, bq, LANE), F32),
                        pltpu.VMEM((nheads, ATT_UNROLL, bq, bk), F32), pltpu.VMEM((nheads, ATT_UNROLL, bq, bk), F32),
                        pltpu.VMEM((nheads, ATT_UNROLL, bq, bk), BF16),
                        pltpu.VMEM((nheads, ATT_UNROLL, bq, bk), BF16)] + ex.scratch,
        compiler_params=_cp(("arbitrary", "arbitrary")))(qkv, qkv, qkv, car, do, *ex.arrs)
    return res[0], res[1], res[2], list(res[3:])


def _conv3(u_ref, w, bias, c, r0, rc):
    x = u_ref[pl.ds(r0, rc), :].astype(F32)
    p0 = pl.multiple_of(jnp.maximum(r0 - 16, 0), 16)
    prev = u_ref[pl.ds(p0, 16), :].astype(F32)
    prev = jnp.where(c > 0, prev, 0.0)
    row = lax.broadcasted_iota(jnp.int32, (rc, 1), 0)
    s1 = jnp.where(row == 0, prev[15:16, :], pltpu.roll(x, 1, 0))
    s2 = jnp.where(row == 0, prev[14:15, :], jnp.where(row == 1, prev[15:16, :], pltpu.roll(x, 2, 0)))
    cv = w[2:3, :] * x + w[1:2, :] * s1 + w[0:1, :] * s2 + bias
    return cv, x, s1, s2


def _sigmoid(x):
    return 1.0 / (1.0 + jnp.exp(-x))


def _ffn_act_fwd(name, up8, cw8, cb8, nb, s):
    _, t, c_w = up8.shape
    rc = _tile(s, 256)
    nch = s // rc
    half = N_DEV // 2

    def body(ug_ref, uv_ref, wg_ref, wv_ref, bg_ref, bv_ref, act_ref):
        wg, wv, bg, bv = wg_ref[...], wv_ref[...], bg_ref[...], bv_ref[...]

        def chunk(c, carry):
            r0 = pl.multiple_of(c * rc, rc)
            cg = _conv3(ug_ref, wg, bg, c, r0, rc)[0]
            cv = _conv3(uv_ref, wv, bv, c, r0, rc)[0]
            act_ref[pl.ds(r0, rc), :] = (cg * _sigmoid(cg) * cv).astype(act_ref.dtype)
            return carry

        lax.fori_loop(0, nch, chunk, 0)

    def slab(off):
        return pl.BlockSpec((None, s, c_w), lambda k, b: (k + off, b, 0))

    def par(rows, off):
        return pl.BlockSpec((None, rows, c_w), lambda k, b: (k + off, 0, 0))

    return pl.pallas_call(
        body, name=name, grid=(half, nb),
        in_specs=[slab(0), slab(half), par(3, 0), par(3, half), par(1, 0), par(1, half)],
        out_specs=pl.BlockSpec((None, s, c_w), lambda k, b: (k, b, 0)),
        out_shape=jax.ShapeDtypeStruct((half, t, c_w), BF16),
        compiler_params=_cp(("parallel", "parallel")))(up8, up8, cw8, cw8, cb8, cb8)


def _ffn_act_bwd(name, up8, dact4, cw8, cb8, nb, s):
    _, t, c_w = up8.shape
    rc = _tile(s, 256)
    nch = s // rc
    half = N_DEV // 2

    def body(u_ref, da_ref, w_ref, b_ref, dup_ref, dcw_ref, dcb_ref):
        w2, b2 = w_ref[...], b_ref[...]
        row = lax.broadcasted_iota(jnp.int32, (rc, 1), 0)

        @pl.when(pl.program_id(1) == 0)
        def _():
            dcw_ref[...] = jnp.zeros_like(dcw_ref)
            dcb_ref[...] = jnp.zeros_like(dcb_ref)

        def chunk(i, carry):
            c = nch - 1 - i
            r0 = pl.multiple_of(c * rc, rc)
            convs = [_conv3(u_ref.at[h], w2[h], b2[h], c, r0, rc) for h in range(2)]
            gt, vl = convs[0][0], convs[1][0]
            da = da_ref[pl.ds(r0, rc), :].astype(F32)
            sg = _sigmoid(gt)
            dcvs = [da * vl * sg * (1.0 + gt * (1.0 - sg)), da * gt * sg]
            out = []
            for h in range(2):
                n0, n1, a0, a1, a2, ab = carry[6 * h:6 * h + 6]
                dcv, (_, x, s1, s2), w = dcvs[h], convs[h], w2[h]
                t1 = jnp.where(row == rc - 1, n0, pltpu.roll(dcv, rc - 1, 0))
                t2 = jnp.where(row == rc - 2, n0, jnp.where(row == rc - 1, n1, pltpu.roll(dcv, rc - 2, 0)))
                dup = w[2:3, :] * dcv + w[1:2, :] * t1 + w[0:1, :] * t2
                dup_ref[h, pl.ds(r0, rc), :] = dup.astype(dup_ref.dtype)
                out += [dcv[0:1, :], dcv[1:2, :],
                        a0 + jnp.sum(dcv * s2, axis=0, keepdims=True), a1 + jnp.sum(dcv * s1, axis=0, keepdims=True),
                        a2 + jnp.sum(dcv * x, axis=0, keepdims=True), ab + jnp.sum(dcv, axis=0, keepdims=True)]
            return tuple(out)

        z = jnp.zeros((1, c_w), F32)
        fin = lax.fori_loop(0, nch, chunk, (z,) * 12)
        for h in range(2):
            _, _, a0, a1, a2, ab = fin[6 * h:6 * h + 6]
            dcw_ref[h, 0:1, :] += a0
            dcw_ref[h, 1:2, :] += a1
            dcw_ref[h, 2:3, :] += a2
            dcb_ref[h] += ab

    def pair(rows, per_seq):
        return pl.BlockSpec((2, None, rows, c_w), (lambda k, b: (0, k, b, 0)) if per_seq else (lambda k, b: (0, k, 0, 0)))

    four = lambda a: a.reshape((2, half) + a.shape[1:])
    dup, dcw, dcb = pl.pallas_call(
        body, name=name, grid=(half, nb),
        in_specs=[pair(s, True), pl.BlockSpec((None, s, c_w), lambda k, b: (k, b, 0)), pair(3, False), pair(1, False)],
        out_specs=[pair(s, True), pair(3, False), pair(1, False)],
        out_shape=[jax.ShapeDtypeStruct((2, half, t, c_w), BF16), jax.ShapeDtypeStruct((2, half, 3, c_w), F32),
                   jax.ShapeDtypeStruct((2, half, 1, c_w), F32)],
        compiler_params=_cp(("parallel", "arbitrary")))(four(up8), dact4, four(cw8), four(cb8))
    return dup.reshape(N_DEV, t, c_w), dcw.reshape(N_DEV, 3, c_w), dcb.reshape(N_DEV, 1, c_w)


_GELU_C0 = math.sqrt(2.0 / math.pi)
_GELU_C1 = 0.044715


def _rowwise(name, body, ins, in_kinds, out_kinds, t, d, tr_pref=512):
    tr = _tile(t, tr_pref)
    row = pl.BlockSpec((tr, d), lambda i: (i, 0))
    vec = pl.BlockSpec((1, d), lambda i: (0, 0))
    in_specs = [row if k == "row" else vec for k in in_kinds]
    out_specs = [row if k[0] == "row" else vec for k in out_kinds]
    out_shape = [jax.ShapeDtypeStruct((t, d) if k[0] == "row" else (1, d), k[1]) for k in out_kinds]
    has_acc = any(k[0] == "acc" for k in out_kinds)
    return pl.pallas_call(body, name=name, grid=(t // tr,), in_specs=in_specs, out_specs=out_specs,
                          out_shape=out_shape,
                          compiler_params=_cp(("arbitrary",) if has_acc else ("parallel",)))(*ins)


def _ssm_post_fwd(ys, u, dskip):
    t, d = ys.shape

    def body(ys_ref, u_ref, ds_ref, y_ref, z_ref):
        y = ys_ref[...].astype(F32) + ds_ref[...] * u_ref[...].astype(F32)
        y_ref[...] = y
        th = jnp.tanh(_GELU_C0 * (y + _GELU_C1 * y * y * y))
        z_ref[...] = (0.5 * y * (1.0 + th)).astype(z_ref.dtype)

    return _rowwise("ssm_post_fwd", body, [ys, u, dskip], ["row", "row", "vec"],
                    [("row", F32), ("row", BF16)], t, d)


def _glu_fwd(z, gl, bglu):
    t, d = z.shape

    def body(z_ref, gl_ref, b_ref, o_ref):
        o_ref[...] = (z_ref[...].astype(F32) * _sigmoid(gl_ref[...] + b_ref[...])).astype(o_ref.dtype)

    return _rowwise("glu_fwd", body, [z, gl, bglu], ["row", "row", "vec"], [("row", BF16)], t, d)[0]


def _glu_bwd(dgg, z, gl, bglu):
    t, d = z.shape

    def body(dg_ref, z_ref, gl_ref, b_ref, dgl_ref, dz_ref, db_ref):
        sg = _sigmoid(gl_ref[...] + b_ref[...])
        dg = dg_ref[...]
        dgl = dg * z_ref[...].astype(F32) * sg * (1.0 - sg)
        dgl_ref[...] = dgl.astype(dgl_ref.dtype)
        dz_ref[...] = dg * sg

        @pl.when(pl.program_id(0) == 0)
        def _():
            db_ref[...] = jnp.zeros_like(db_ref)

        db_ref[...] += jnp.sum(dgl, axis=0, keepdims=True)

    return _rowwise("glu_bwd", body, [dgg, z, gl, bglu], ["row", "row", "row", "vec"],
                    [("row", BF16), ("row", F32), ("acc", F32)], t, d)


def _ssm_post_bwd(dz1, dz2, y, u, dskip):
    t, d = y.shape

    def body(a_ref, b_ref, y_ref, u_ref, ds_ref, dy_ref, du_ref, dd_ref):
        yv = y_ref[...]
        inner = _GELU_C0 * (yv + _GELU_C1 * yv * yv * yv)
        th = jnp.tanh(inner)
        dgelu = 0.5 * (1.0 + th) + 0.5 * yv * (1.0 - th * th) * _GELU_C0 * (1.0 + 3.0 * _GELU_C1 * yv * yv)
        dy = (a_ref[...] + b_ref[...]) * dgelu
        dy_ref[...] = dy.astype(dy_ref.dtype)
        du_ref[...] = dy * ds_ref[...]

        @pl.when(pl.program_id(0) == 0)
        def _():
            dd_ref[...] = jnp.zeros_like(dd_ref)

        dd_ref[...] += jnp.sum(dy * u_ref[...].astype(F32), axis=0, keepdims=True)

    return _rowwise("ssm_post_bwd", body, [dz1, dz2, y, u, dskip], ["row", "row", "row", "row", "vec"],
                    [("row", BF16), ("row", F32), ("acc", F32)], t, d)


def _add_cast(a, b):
    t, d = a.shape

    def body(a_ref, b_ref, o_ref):
        o_ref[...] = (a_ref[...].astype(F32) + b_ref[...].astype(F32)).astype(o_ref.dtype)

    return _rowwise("add_cast", body, [a, b], ["row", "row"], [("row", BF16)], t, d)[0]


def _ssm_scan(e_re, e_im, lam_re, lam_im, nb):
    r, n = e_re.shape
    nc = r // nb
    cb = _tile(n, 512)

    def body(er_ref, ei_ref, lr_ref, li_ref, xr_ref, xi_ref):
        lr, li = lr_ref[...], li_ref[...]
        rid = lax.broadcasted_iota(jnp.int32, (8, 1), 0)

        def tile(i, carry):
            out = []
            for b in range(nb):
                xr, xi = carry[2 * b:2 * b + 2]
                r0 = pl.multiple_of(b * nc + i * 8, 8)
                er, ei = er_ref[pl.ds(r0, 8), :], ei_ref[pl.ds(r0, 8), :]
                outr, outi = jnp.zeros((8, cb), F32), jnp.zeros((8, cb), F32)
                for j in range(8):
                    outr = jnp.where(rid == j, xr, outr)
                    outi = jnp.where(rid == j, xi, outi)
                    xr, xi = lr * xr - li * xi + er[j:j + 1, :], li * xr + lr * xi + ei[j:j + 1, :]
                xr_ref[pl.ds(r0, 8), :] = outr
                xi_ref[pl.ds(r0, 8), :] = outi
                out += [xr, xi]
            return tuple(out)

        lax.fori_loop(0, nc // 8, tile, (jnp.zeros((1, cb), F32),) * (2 * nb))

    mat = pl.BlockSpec((r, cb), lambda j: (0, j))
    vec = pl.BlockSpec((1, cb), lambda j: (0, j))
    sds = jax.ShapeDtypeStruct((r, n), F32)
    return pl.pallas_call(body, name="ssm_scan", grid=(n // cb,), in_specs=[mat, mat, vec, vec],
                          out_specs=[mat, mat], out_shape=[sds, sds],
                          compiler_params=_cp(("parallel",)))(e_re, e_im, lam_re, lam_im)


def _ssm_scan_bwd(dxp_re, dxp_im, lam_re, lam_im, nb):
    r, n = dxp_re.shape
    nc = r // nb
    cb = _tile(n, 512)

    def body(dr_ref, di_ref, lr_ref, li_ref, er_ref, ei_ref):
        lr, li = lr_ref[...], li_ref[...]
        rid = lax.broadcasted_iota(jnp.int32, (8, 1), 0)

        def tile(i, carry):
            out = []
            for b in range(nb):
                gr, gi = carry[2 * b:2 * b + 2]
                r0 = pl.multiple_of(b * nc + (nc // 8 - 1 - i) * 8, 8)
                dr, di = dr_ref[pl.ds(r0, 8), :], di_ref[pl.ds(r0, 8), :]
                outr, outi = jnp.zeros((8, cb), F32), jnp.zeros((8, cb), F32)
                for j in range(7, -1, -1):
                    outr = jnp.where(rid == j, gr, outr)
                    outi = jnp.where(rid == j, gi, outi)
                    gr, gi = dr[j:j + 1, :] + lr * gr + li * gi, di[j:j + 1, :] + lr * gi - li * gr
                er_ref[pl.ds(r0, 8), :] = outr
                ei_ref[pl.ds(r0, 8), :] = outi
                out += [gr, gi]
            return tuple(out)

        lax.fori_loop(0, nc // 8, tile, (jnp.zeros((1, cb), F32),) * (2 * nb))

    mat = pl.BlockSpec((r, cb), lambda j: (0, j))
    vec = pl.BlockSpec((1, cb), lambda j: (0, j))
    sds = jax.ShapeDtypeStruct((r, n), F32)
    return pl.pallas_call(body, name="ssm_scan_bwd", grid=(n // cb,), in_specs=[mat, mat, vec, vec],
                          out_specs=[mat, mat], out_shape=[sds, sds],
                          compiler_params=_cp(("parallel",)))(dxp_re, dxp_im, lam_re, lam_im)


def _ssm_compact(a_re, a_im, log_dt, b_re, b_im, c_re, c_im):
    g, p = a_re.shape
    h = b_re.shape[-1]
    ln = SSM_L
    sg = LANE // h
    na = g // sg
    hp = lax.Precision.HIGHEST
    lam = lax.complex(a_re, a_im)
    ldt = lam * jnp.exp(log_dt)[:, None]
    lam_bar = jnp.exp(ldt)
    bbar = ((lam_bar - 1.0) / lam)[..., None] * lax.complex(b_re, b_im)
    cm = lax.complex(c_re, c_im)
    steps = jnp.arange(ln + 1, dtype=F32)
    pw = jnp.exp(ldt[:, None, :] * steps[None, :, None])
    kd = jnp.einsum("ghp,gdp,gpk->gdhk", cm, pw[:, :ln], bbar, precision=hp).real

    def stacked(x, rows_per, cols_per):
        x = x.reshape(na, sg, ln, rows_per, cols_per).transpose(0, 2, 1, 3, 4).reshape(na, ln, sg * rows_per, cols_per)
        return jnp.pad(x, ((0, 0), (0, 0), (0, 0), (0, LANE - cols_per)))

    wxc = (pw[:, ln - 1 - jnp.arange(ln)][:, :, :, None] * bbar[:, None]).transpose(0, 1, 3, 2)
    cpc = (cm[:, None] * pw[:, 1:ln + 1][:, :, None, :]).transpose(0, 1, 3, 2)
    lam_l = pw[:, ln]
    return (stacked(kd.transpose(0, 1, 3, 2), h, h), stacked(wxc.real, h, p), stacked(wxc.imag, h, p),
            stacked(cpc.real, p, h), stacked(-cpc.imag, p, h),
            lam_l.real.reshape(1, g * p), lam_l.imag.reshape(1, g * p))


def _ssm_masks(h, p):
    sg = LANE // h
    r128 = lax.broadcasted_iota(jnp.int32, (LANE, LANE), 0)
    c128 = lax.broadcasted_iota(jnp.int32, (LANE, LANE), 1)
    rx = lax.broadcasted_iota(jnp.int32, (LANE, sg * p), 0)
    cx = lax.broadcasted_iota(jnp.int32, (LANE, sg * p), 1)
    ry = lax.broadcasted_iota(jnp.int32, (sg * p, LANE), 0)
    cy = lax.broadcasted_iota(jnp.int32, (sg * p, LANE), 1)
    f = lambda m: m.astype(F32)
    return dict(
        spread_h=f((r128 < h) & (c128 % h == r128)),
        spread_p=f((rx < p) & (cx % p == rx)),
        gather_h=f((c128 < h) & (r128 % h == c128)),
        gather_p=f((cy < p) & (ry % p == cy)),
        same_t=f(r128 // h == c128 // h), same_x=f(rx // h == cx // p), same_y=f(ry // p == cy // h))


def _place(a, spread):
    return jnp.dot(a.astype(BF16), spread.astype(BF16), preferred_element_type=F32)


def _ssm_expand(compact, ex):
    kt, wxr, wxi, wyr, wyi = compact
    na, ln = kt.shape[:2]
    wst = wyr.shape[2]
    h, p = SSM_H, SSM_P
    grid = (na,)

    def body(*refs):
        kt_ref, wxr_ref, wxi_ref, wyr_ref, wyi_ref = refs[:5]
        ex_ins = refs[5:5 + ex.n]
        tm_ref, xr_ref, xi_ref, yr_ref, yi_ref = refs[5 + ex.n:10 + ex.n]
        ex_outs = refs[10 + ex.n:10 + 2 * ex.n]
        sems = refs[10 + 2 * ex.n:]
        first, last = _grid_ends(grid)

        @pl.when(first)
        def _():
            ex.start(ex_ins, ex_outs, sems)

        m = _ssm_masks(h, p)
        ktb = [_place(kt_ref[lag], m["spread_h"]) * m["same_t"] for lag in range(ln)]
        zero = jnp.zeros((LANE, LANE), F32)
        for sig in range(ln):
            rows = slice(sig * LANE, (sig + 1) * LANE)
            tm_ref[rows, :] = jnp.concatenate([ktb[tau - sig] if tau >= sig else zero for tau in range(ln)],
                                              axis=1).astype(tm_ref.dtype)
            xr_ref[rows, :] = (_place(wxr_ref[sig], m["spread_p"]) * m["same_x"]).astype(xr_ref.dtype)
            xi_ref[rows, :] = (_place(wxi_ref[sig], m["spread_p"]) * m["same_x"]).astype(xi_ref.dtype)
        for tau in range(ln):
            cols = slice(tau * LANE, (tau + 1) * LANE)
            yr_ref[:, cols] = (_place(wyr_ref[tau], m["spread_h"]) * m["same_y"]).astype(yr_ref.dtype)
            yi_ref[:, cols] = (_place(wyi_ref[tau], m["spread_h"]) * m["same_y"]).astype(yi_ref.dtype)

        @pl.when(last)
        def _():
            ex.wait(ex_ins, ex_outs, sems)

    blk = lambda rows: pl.BlockSpec((None, ln, rows, LANE), lambda j: (j, 0, 0, 0))
    mat = lambda rows, cols: pl.BlockSpec((None, rows, cols), lambda j: (j, 0, 0))
    sds = lambda rows, cols: jax.ShapeDtypeStruct((na, rows, cols), BF16)
    wch = ln * LANE
    res = pl.pallas_call(
        body, name="ssm_expand", grid=grid,
        in_specs=[blk(LANE), blk(LANE), blk(LANE), blk(wst), blk(wst)] + ex.specs,
        out_specs=[mat(wch, wch), mat(wch, wst), mat(wch, wst), mat(wst, wch), mat(wst, wch)] + ex.specs,
        out_shape=[sds(wch, wch), sds(wch, wst), sds(wch, wst), sds(wst, wch), sds(wst, wch)] + ex.out_shape,
        scratch_shapes=ex.scratch,
        compiler_params=_cp(("arbitrary",)))(kt, wxr, wxi, wyr, wyi, *ex.arrs)
    return list(res[:5]), list(res[5:])


def _chunk_view(a):
    t, d = a.shape
    return a.reshape(t // SSM_L, SSM_L * d)


def _sg_specs(r4, d, wst):
    nblk = d // LANE
    cat = [pl.BlockSpec((r4, LANE), functools.partial(lambda j, tau: (0, tau * nblk + j), tau=tau))
           for tau in range(SSM_L)]
    plane = pl.BlockSpec((r4, wst), lambda j: (0, j))
    mat = lambda rows, cols: pl.BlockSpec((None, rows, cols), lambda j: (j, 0, 0))
    piece = pl.BlockSpec((r4, LANE), lambda j: (0, j))
    return cat, plane, mat, piece


def _lane_cat(refs):
    return jnp.concatenate([r[...] for r in refs], axis=1)


def _bdot(a, b, dims):
    return lax.dot_general(a.astype(BF16), b.astype(BF16), dims, preferred_element_type=F32)


def _ssm_core_fwd(u, ops, nb):
    tm, wxr, wxi, wyr, wyi, lam_re, lam_im = ops
    t, d = u.shape
    ln, na, wch, wst = SSM_L, tm.shape[0], tm.shape[1], wxr.shape[2]
    r4 = t // ln
    n = na * wst
    u4 = _chunk_view(u)
    cat, plane, mat, piece = _sg_specs(r4, d, wst)
    pds = jax.ShapeDtypeStruct((r4, n), F32)

    def states(*refs):
        ucat = _lane_cat(refs[:ln])
        wr_ref, wi_ref, er_ref, ei_ref = refs[ln:]
        er_ref[...] = _bdot(ucat, wr_ref[...], _NN)
        ei_ref[...] = _bdot(ucat, wi_ref[...], _NN)

    e_re, e_im = pl.pallas_call(
        states, name="ssm_states", grid=(na,), in_specs=cat + [mat(wch, wst)] * 2, out_specs=[plane, plane],
        out_shape=[pds, pds], compiler_params=_cp(("parallel",)))(*([u4] * ln), wxr, wxi)
    xp_re, xp_im = _ssm_scan(e_re, e_im, lam_re, lam_im, nb)

    def outputs(*refs):
        ucat = _lane_cat(refs[:ln])
        tm_ref, xr_ref, xi_ref, wr_ref, wi_ref = refs[ln:ln + 5]
        y = (_bdot(ucat, tm_ref[...], _NN) + _bdot(xr_ref[...], wr_ref[...], _NN)
             + _bdot(xi_ref[...], wi_ref[...], _NN))
        for tau, o_ref in enumerate(refs[ln + 5:]):
            o_ref[...] = y[:, tau * LANE:(tau + 1) * LANE].astype(o_ref.dtype)

    ys = pl.pallas_call(
        outputs, name="ssm_y", grid=(na,),
        in_specs=cat + [mat(wch, wch), plane, plane, mat(wst, wch), mat(wst, wch)], out_specs=[piece] * ln,
        out_shape=[jax.ShapeDtypeStruct((r4, d), BF16)] * ln,
        compiler_params=_cp(("parallel",)))(*([u4] * ln), tm, xp_re, xp_im, wyr, wyi)
    return jnp.concatenate(ys, axis=1).reshape(t, d), xp_re, xp_im


def _ssm_core_bwd(dy, u, xp_re, xp_im, ops, nb):
    tm, wxr, wxi, wyr, wyi, lam_re, lam_im = ops
    t, d = u.shape
    ln, na, wch, wst = SSM_L, tm.shape[0], tm.shape[1], wxr.shape[2]
    r4 = t // ln
    n = na * wst
    u4, dy4 = _chunk_view(u), _chunk_view(dy)
    cat, plane, mat, piece = _sg_specs(r4, d, wst)
    pds = jax.ShapeDtypeStruct((r4, n), F32)

    def dstates(*refs):
        dycat = _lane_cat(refs[:ln])
        wr_ref, wi_ref, dr_ref, di_ref = refs[ln:]
        dr_ref[...] = _bdot(dycat, wr_ref[...], _NT)
        di_ref[...] = _bdot(dycat, wi_ref[...], _NT)

    dxp_re, dxp_im = pl.pallas_call(
        dstates, name="ssm_dxp", grid=(na,), in_specs=cat + [mat(wst, wch)] * 2, out_specs=[plane, plane],
        out_shape=[pds, pds], compiler_params=_cp(("parallel",)))(*([dy4] * ln), wyr, wyi)
    de_re, de_im = _ssm_scan_bwd(dxp_re, dxp_im, lam_re, lam_im, nb)

    def dinputs(*refs):
        dycat = _lane_cat(refs[:ln])
        tm_ref, er_ref, ei_ref, wr_ref, wi_ref = refs[ln:ln + 5]
        du = (_bdot(dycat, tm_ref[...], _NT) + _bdot(er_ref[...], wr_ref[...], _NT)
              + _bdot(ei_ref[...], wi_ref[...], _NT))
        for tau, o_ref in enumerate(refs[ln + 5:]):
            o_ref[...] = du[:, tau * LANE:(tau + 1) * LANE].astype(o_ref.dtype)

    dus = pl.pallas_call(
        dinputs, name="ssm_du", grid=(na,),
        in_specs=cat + [mat(wch, wch), plane, plane, mat(wch, wst), mat(wch, wst)], out_specs=[piece] * ln,
        out_shape=[jax.ShapeDtypeStruct((r4, d), BF16)] * ln,
        compiler_params=_cp(("parallel",)))(*([dy4] * ln), tm, de_re, de_im, wxr, wxi)

    def doperators(*refs):
        ucat, dycat = _lane_cat(refs[:ln]), _lane_cat(refs[ln:2 * ln])
        (er_ref, ei_ref, xr_ref, xi_ref, dtm_ref, dwxr_ref, dwxi_ref, dwyr_ref, dwyi_ref,
         dlr_ref, dli_ref) = refs[2 * ln:]
        er, ei, xr, xi = er_ref[...], ei_ref[...], xr_ref[...], xi_ref[...]
        m = _ssm_masks(SSM_H, SSM_P)
        gather_h, gather_p = m["gather_h"].astype(BF16), m["gather_p"].astype(BF16)
        blk = lambda i: slice(i * LANE, (i + 1) * LANE)
        dtm = _bdot(ucat, dycat, _TN)
        for lag in range(ln):
            acc = dtm[blk(0), blk(lag)]
            for sig in range(1, ln - lag):
                acc = acc + dtm[blk(sig), blk(sig + lag)]
            dtm_ref[lag] = _split_dot(acc * m["same_t"], gather_h)
        for src, dst in ((er, dwxr_ref), (ei, dwxi_ref)):
            dwx = _bdot(ucat, src, _TN)
            for sig in range(ln):
                dst[sig] = _split_dot(dwx[blk(sig), :] * m["same_x"], gather_p)
        for src, dst in ((xr, dwyr_ref), (xi, dwyi_ref)):
            dwy = _bdot(src, dycat, _TN)
            for tau in range(ln):
                dst[tau] = _split_dot(dwy[:, blk(tau)] * m["same_y"], gather_h)
        dlr_ref[...] = jnp.sum(er * xr + ei * xi, axis=0, keepdims=True)
        dli_ref[...] = jnp.sum(ei * xr - er * xi, axis=0, keepdims=True)

    cblk = lambda rows: pl.BlockSpec((None, ln, rows, LANE), lambda j: (j, 0, 0, 0))
    cds = lambda rows: jax.ShapeDtypeStruct((na, ln, rows, LANE), F32)
    vec = pl.BlockSpec((1, wst), lambda j: (0, j))
    vds = jax.ShapeDtypeStruct((1, n), F32)
    d_compact = pl.pallas_call(
        doperators, name="ssm_dops", grid=(na,), in_specs=cat + cat + [plane] * 4,
        out_specs=[cblk(LANE), cblk(LANE), cblk(LANE), cblk(wst), cblk(wst), vec, vec],
        out_shape=[cds(LANE), cds(LANE), cds(LANE), cds(wst), cds(wst), vds, vds],
        compiler_params=_cp(("parallel",)))(*([u4] * ln), *([dy4] * ln), de_re, de_im, xp_re, xp_im)
    return jnp.concatenate(dus, axis=1).reshape(t, d), tuple(d_compact)


def _modfin_fwd(c_all, w_mod, w_fin):
    n, d = c_all.shape
    nl, _, cm = w_mod.shape
    cf = w_fin.shape[1]
    width = nl * cm + cf
    hp = lax.Precision.HIGHEST

    def body(c_ref, wm_ref, wf_ref, act_ref, out_ref):
        cv = c_ref[...]
        act = cv * _sigmoid(cv)
        act_ref[...] = act
        for i in range(nl):
            out_ref[:, i * cm:(i + 1) * cm] = jnp.dot(act, wm_ref[i], preferred_element_type=F32, precision=hp)
        out_ref[:, nl * cm:] = jnp.dot(act, wf_ref[...], preferred_element_type=F32, precision=hp)

    return pl.pallas_call(body, name="modfin_fwd",
                          out_shape=[jax.ShapeDtypeStruct((n, d), F32), jax.ShapeDtypeStruct((n, width), F32)],
                          compiler_params=_cp(None))(c_all, w_mod, w_fin)


def _modfin_bwd(c_act_t, dmod_loc, dfin_loc, dall):
    d, n = c_act_t.shape
    nl, _, cm = dmod_loc.shape
    cf = dfin_loc.shape[1]
    hp = lax.Precision.HIGHEST

    def body(ct_ref, dm_ref, df_ref, da_ref, gwm_ref, gwf_ref, gb_ref):
        ct = ct_ref[...]
        for i in range(nl):
            gwm_ref[i] = jnp.dot(ct, dm_ref[i], preferred_element_type=F32, precision=hp)
        gwf_ref[...] = jnp.dot(ct, df_ref[...], preferred_element_type=F32, precision=hp)
        gb_ref[...] = jnp.sum(da_ref[...], axis=0, keepdims=True)

    return pl.pallas_call(body, name="modfin_bwd",
                          out_shape=[jax.ShapeDtypeStruct((nl, d, cm), F32), jax.ShapeDtypeStruct((d, cf), F32),
                                     jax.ShapeDtypeStruct((1, dall.shape[1]), F32)],
                          compiler_params=_cp(None))(c_act_t, dmod_loc, dfin_loc, dall)


def _adamw(name, gparts, w, m, v):
    n, r, c = gparts.shape
    tr = _tile(r, 256)

    def body(gp_ref, w_ref, m_ref, v_ref, g_ref, d_ref, mo_ref, vo_ref):
        _adamw_step(gp_ref, w_ref, m_ref, v_ref, g_ref, d_ref, mo_ref, vo_ref)

    mat = pl.BlockSpec((tr, c), lambda i: (i, 0))
    sds = jax.ShapeDtypeStruct((r, c), F32)
    return pl.pallas_call(body, name=name, grid=(r // tr,),
                          in_specs=[pl.BlockSpec((n, tr, c), lambda i: (0, i, 0)), mat, mat, mat],
                          out_specs=[mat] * 4, out_shape=[sds] * 4,
                          compiler_params=_cp(("parallel",)))(gparts, w, m, v)


def _adamw_layers(name, gparts_l, w, m, v):
    nl, r, c = w.shape
    n = gparts_l[0].shape[0]
    tr = _tile(r, 256)
    nt = r // tr

    def body(*refs):
        w_ref, m_ref, v_ref = refs[nl:nl + 3]
        layer = pl.program_id(0)
        for i in range(nl):
            @pl.when(layer == i)
            def _(i=i):
                _adamw_step(refs[i], w_ref, m_ref, v_ref, *refs[nl + 3:])

    def parts(i):
        return pl.BlockSpec((n, tr, c), lambda l, t: (0, jnp.where(l == i, t, jnp.where(l < i, 0, nt - 1)), 0))

    mat = pl.BlockSpec((None, tr, c), lambda l, t: (l, t, 0))
    sds = jax.ShapeDtypeStruct((nl, r, c), F32)
    return pl.pallas_call(body, name=name, grid=(nl, nt), in_specs=[parts(i) for i in range(nl)] + [mat] * 3,
                          out_specs=[mat] * 4, out_shape=[sds] * 4,
                          compiler_params=_cp(("arbitrary", "arbitrary")))(*gparts_l, w, m, v)


def _adamw_step(gp_ref, w_ref, m_ref, v_ref, g_ref, d_ref, mo_ref, vo_ref):
    gsum = gp_ref[0].astype(F32)
    for j in range(1, gp_ref.shape[0]):
        gsum = gsum + gp_ref[j].astype(F32)
    mn = ADAM_B1 * m_ref[...] + (1.0 - ADAM_B1) * gsum
    vn = ADAM_B2 * v_ref[...] + (1.0 - ADAM_B2) * (gsum * gsum)
    g_ref[...] = gsum
    mo_ref[...] = mn
    vo_ref[...] = vn
    m_hat = mn * (1.0 / (1.0 - ADAM_B1 ** ADAM_STEP))
    v_hat = vn * (1.0 / (1.0 - ADAM_B2 ** ADAM_STEP))
    d_ref[...] = -ADAM_LR * (m_hat / (jnp.sqrt(v_hat) + ADAM_EPS) + ADAM_WD * w_ref[...])


def _adamw_many(name, entries):
    k = len(entries)

    def body(*refs):
        for i in range(k):
            _adamw_step(*refs[4 * i:4 * i + 4], *refs[4 * k + 4 * i:4 * k + 4 * i + 4])

    ops = [a for e in entries for a in e]
    out_shape = [jax.ShapeDtypeStruct(e[1].shape, F32) for e in entries for _ in range(4)]
    return pl.pallas_call(body, name=name, out_shape=out_shape, compiler_params=_cp(None))(*ops)


class _Exchange:
    def __init__(self, arrs, gathers):
        self.arrs = [pltpu.with_memory_space_constraint(a, pltpu.HBM) for a in arrs]
        self.gathers = list(gathers)
        self.n = len(arrs)
        self.out_shape = [pltpu.HBM(((N_DEV,) + a.shape) if g else a.shape, a.dtype)
                          for a, g in zip(arrs, self.gathers)]
        self.specs = [pl.BlockSpec(memory_space=pltpu.HBM)] * self.n
        self.scratch = [pltpu.SemaphoreType.DMA((self.n, N_DEV - 1)), pltpu.SemaphoreType.DMA((self.n, N_DEV - 1)),
                        pltpu.SemaphoreType.DMA((self.n,))]

    def _copies(self, ins, outs, sems):
        send_sems, recv_sems, local_sems = sems
        x, y, c = lax.axis_index("x"), lax.axis_index("y"), lax.axis_index("c")
        me = 4 * x + 2 * y + c
        local, sends, recvs = [], [], []
        for i in range(self.n):
            src_me = ins[i] if self.gathers[i] else ins[i].at[me]
            local.append(pltpu.make_async_copy(src_me, outs[i].at[me], local_sems.at[i]))
        for dd in range(1, N_DEV):
            px = jnp.bitwise_xor(x, dd >> 2)
            py = jnp.bitwise_xor(y, (dd >> 1) & 1)
            pc = jnp.bitwise_xor(c, dd & 1)
            pid = 4 * px + 2 * py + pc
            for i in range(self.n):
                src = ins[i] if self.gathers[i] else ins[i].at[pid]
                sems_i = dict(send_sem=send_sems.at[i, dd - 1], recv_sem=recv_sems.at[i, dd - 1],
                              device_id=(px, py, pc), device_id_type=MESH)
                sends.append(pltpu.make_async_remote_copy(src_ref=src, dst_ref=outs[i].at[me], **sems_i))
                recvs.append(pltpu.make_async_remote_copy(src_ref=src, dst_ref=outs[i].at[pid], **sems_i))
        return local, sends, recvs

    def start(self, ins, outs, sems):
        local, sends, _ = self._copies(ins, outs, sems)
        for cp in local + sends:
            cp.start()

    def wait(self, ins, outs, sems):
        local, sends, recvs = self._copies(ins, outs, sems)
        for cp in recvs:
            cp.wait_recv()
        for cp in sends:
            cp.wait_send()
        for cp in local:
            cp.wait()


class _NoExchange:
    n, arrs, specs, out_shape, scratch = 0, [], [], [], []

    def start(self, ins, outs, sems):
        pass

    def wait(self, ins, outs, sems):
        pass


def _exchange(name, arrs, gathers):
    ex = _Exchange(arrs, gathers)
    n = ex.n

    def body(*refs):
        ins, outs, sems = refs[:n], refs[n:2 * n], refs[2 * n:]
        ex.start(ins, outs, sems)
        ex.wait(ins, outs, sems)

    outs = pl.pallas_call(body, name=name, in_specs=ex.specs, out_specs=ex.specs, out_shape=ex.out_shape,
                          scratch_shapes=ex.scratch)(*ex.arrs)
    return list(outs)


def kernel(x, c, norm_mix, norm_ffn, w_mod, b_mod, w_qkv, w_o_attn, w_in_ssm, a_re, a_im, log_dt, b_re, b_im, c_re, c_im, d_skip, w_glu, b_glu, w_o_ssm, w_up, conv_w, conv_b, w_down, norm_out, w_fin, b_fin, loss_target, m_norm_mix, m_norm_ffn, m_w_mod, m_b_mod, m_w_qkv, m_w_o_attn, m_w_in_ssm, m_a_re, m_a_im, m_log_dt, m_b_re, m_b_im, m_c_re, m_c_im, m_d_skip, m_w_glu, m_b_glu, m_w_o_ssm, m_w_up, m_conv_w, m_conv_b, m_w_down, m_norm_out, m_w_fin, m_b_fin, v_norm_mix, v_norm_ffn, v_w_mod, v_b_mod, v_w_qkv, v_w_o_attn, v_w_in_ssm, v_a_re, v_a_im, v_log_dt, v_b_re, v_b_im, v_c_re, v_c_im, v_d_skip, v_w_glu, v_b_glu, v_w_o_ssm, v_w_up, v_conv_w, v_conv_b, v_w_down, v_norm_out, v_w_fin, v_b_fin):
    nb, s, d = x.shape
    t = nb * s
    n_seq = nb * N_DEV
    me = 4 * lax.axis_index("x") + 2 * lax.axis_index("y") + lax.axis_index("c")
    cm = w_mod.shape[2]
    cf = w_fin.shape[1]
    c_up = w_up.shape[2]
    r_dn = w_down.shape[1]
    g_ssm = d // SSM_H

    (c8,) = _exchange("gather_first", [c], [True])
    ssm_params = (a_re[0], a_im[0], log_dt[0], b_re[0], b_im[0], c_re[0], c_im[0])
    compact, ops_vjp = jax.vjp(_ssm_compact, *ssm_params)
    ssm_mats, (wq8,) = _ssm_expand(compact[:5], _Exchange([w_qkv[0].astype(BF16)], [True]))
    ops = (*ssm_mats, compact[5], compact[6])
    later = _Exchange([w_o_attn[0].astype(BF16), w_in_ssm[0].astype(BF16), w_glu[0].astype(BF16),
                       w_o_ssm[0].astype(BF16), w_up[0].astype(BF16), w_up[1].astype(BF16),
                       w_down[0].astype(BF16), w_down[1].astype(BF16), conv_w, d_skip, b_glu], [True] * 11)
    half = N_DEV // 2
    cb_l = [conv_b[i].reshape(N_DEV, 1, c_up) for i in range(2)]
    c_all = c8.reshape(n_seq, d)

    c_act, modloc = _modfin_fwd(c_all, w_mod, w_fin)
    (mod8,) = _exchange("gather_mod", [modloc], [True])
    mine = lax.dynamic_slice_in_dim(mod8, me * nb, nb, axis=1)
    mods = []
    for i in range(2):
        mi = mine[:, :, i * cm:(i + 1) * cm].transpose(1, 0, 2).reshape(nb, N_DEV * cm) + b_mod[i]
        mods.append([mi[:, j * d:(j + 1) * d].reshape(nb, 1, d) for j in range(6)])
    fin = mine[:, :, 2 * cm:].transpose(1, 0, 2).reshape(nb, N_DEV * cf) + b_fin
    sh_f, sc_f = fin[:, :d].reshape(nb, 1, d), fin[:, d:].reshape(nb, 1, d)

    row = lambda a: a.reshape(1, -1)
    x0 = x.reshape(t, d)

    def ffn_fwd(i, h):
        up = _mm(f"ffn{i}_up", h, wup8[i], (t // tm_, N_DEV, 1),
                 pl.BlockSpec((tm_, d), lambda a, b, k: (a, 0)), pl.BlockSpec((None, d, c_up), lambda a, b, k: (b, 0, 0)),
                 pl.BlockSpec((None, tm_, c_up), lambda a, b, k: (b, a, 0)),
                 jax.ShapeDtypeStruct((N_DEV, t, c_up), BF16), _NN, (tm_, c_up))
        act = _ffn_act_fwd(f"ffn{i}_act", up, cw_l[i], cb_l[i], nb, s)
        yf = _mm(f"ffn{i}_down", act, wd4[i], (t // tm_, 1, half),
                 pl.BlockSpec((None, tm_, c_up), lambda a, b, k: (k, a, 0)),
                 pl.BlockSpec((None, c_up, d), lambda a, b, k: (k, 0, 0)),
                 pl.BlockSpec((tm_, d), lambda a, b, k: (a, 0)), jax.ShapeDtypeStruct((t, d), F32), _NN, (tm_, d))
        return yf, (h, up, act)

    tm_ = _tile(t, 2048)
    sh1, sc1, g1, sh2, sc2, g2 = mods[0]
    h1 = _norm_mod_fwd("attn_norm", x0, row(norm_mix[0]), sh1, sc1, nb)
    cq = wq8.shape[2]
    qkv = _mm("attn_qkv", h1, wq8, (t // tm_, N_DEV, 1),
              pl.BlockSpec((tm_, d), lambda a, b, k: (a, 0)), pl.BlockSpec((None, d, cq), lambda a, b, k: (b, 0, 0)),
              pl.BlockSpec((tm_, cq), lambda a, b, k: (a, b)), jax.ShapeDtypeStruct((t, 3 * d), BF16), _NN, (tm_, cq))
    o_att, car_att, (wo8, win8, wglu8, wos8, wup8_0, wup8_1, wd8_0, wd8_1, cw8, dskip8, bglu8) = _attn_fwd(
        qkv, nb, s, d, later)
    wo = wo8.reshape(d, d)
    win = win8.reshape(d, d)
    wglu = wglu8.reshape(d, d)
    wos = wos8.reshape(d, d)
    wup8 = [wup8_0, wup8_1]
    wd4 = [wd8_0.reshape(half, 2 * r_dn, d), wd8_1.reshape(half, 2 * r_dn, d)]
    cw_l = [cw8[:, 0], cw8[:, 1]]
    dskip_f = dskip8.reshape(1, d)
    bglu_f = bglu8.reshape(1, d)
    ya = _mm_nn("attn_out", o_att, wo, F32)
    sh1b, sc1b, g1b, sh2b, sc2b, g2b = mods[1]
    x1, h2 = _res_norm_fwd("attn_res", x0, ya, g1, row(norm_ffn[0]), sh2, sc2, nb)
    yf0, ffn0_saved = ffn_fwd(0, h2)
    x2, h3 = _res_norm_fwd("ffn0_res", x1, yf0, g2, row(norm_mix[1]), sh1b, sc1b, nb)

    u = _mm_nn("ssm_in", h3, win, BF16)
    ys_core, xp_re, xp_im = _ssm_core_fwd(u, ops, nb)
    y_ssm, z_ssm = _ssm_post_fwd(ys_core, u, dskip_f)
    gl = _mm_nn("ssm_glu", z_ssm, wglu, F32)
    gg = _glu_fwd(z_ssm, gl, bglu_f)
    ys2 = _mm_nn("ssm_out", gg, wos, F32)
    x3, h4 = _res_norm_fwd("ssm_res", x2, ys2, g1b, row(norm_ffn[1]), sh2b, sc2b, nb)
    yf1, ffn1_saved = ffn_fwd(1, h4)
    x4 = _gate_add("ffn1_res", x3, yf1, g2b, nb)

    dx4, g_norm_out, dsh_f, dsc_f, loss_blk, dyf1, dg2b = _norm_mod_bwd(
        "final_norm", None, x4, row(norm_out), sh_f, sc_f, None, loss_target.reshape(t, d), nb, branch=(yf1, g2b))
    loss = lax.psum(loss_blk[0, 0], ("x", "y", "c"))

    def ffn_bwd(i, dyf, dxo, xin, sc, saved, branch):
        h, up, act = saved
        dact = _mm(f"ffn{i}_down_dx", dyf, wd4[i], (t // tm_, half, 1),
                   pl.BlockSpec((tm_, d), lambda a, b, k: (a, 0)), pl.BlockSpec((None, c_up, d), lambda a, b, k: (b, 0, 0)),
                   pl.BlockSpec((None, tm_, c_up), lambda a, b, k: (b, a, 0)),
                   jax.ShapeDtypeStruct((half, t, c_up), BF16), _NT, (tm_, c_up))
        tk = _tile(t, 1024)
        gwd = _mm(f"ffn{i}_down_dw", act, dyf, (half, 1, t // tk),
                  pl.BlockSpec((None, tk, c_up), lambda a, b, k: (a, k, 0)), pl.BlockSpec((tk, d), lambda a, b, k: (k, 0)),
                  pl.BlockSpec((None, c_up, d), lambda a, b, k: (a, 0, 0)),
                  jax.ShapeDtypeStruct((half, c_up, d), BF16), _TN, (c_up, d))
        dup, dcw, dcb = _ffn_act_bwd(f"ffn{i}_act_bwd", up, dact, cw_l[i], cb_l[i], nb, s)
        dh = _mm(f"ffn{i}_up_dx", dup, wup8[i], (t // tm_, 1, N_DEV),
                 pl.BlockSpec((None, tm_, c_up), lambda a, b, k: (k, a, 0)),
                 pl.BlockSpec((None, d, c_up), lambda a, b, k: (k, 0, 0)),
                 pl.BlockSpec((tm_, d), lambda a, b, k: (a, 0)), jax.ShapeDtypeStruct((t, d), F32), _NT, (tm_, d))
        gwup = _mm(f"ffn{i}_up_dw", h, dup, (1, N_DEV, t // tk),
                   pl.BlockSpec((tk, d), lambda a, b, k: (k, 0)), pl.BlockSpec((None, tk, c_up), lambda a, b, k: (b, k, 0)),
                   pl.BlockSpec((None, d, c_up), lambda a, b, k: (b, 0, 0)),
                   jax.ShapeDtypeStruct((N_DEV, d, c_up), BF16), _TN, (d, c_up))
        dxi, g_norm, dsh, dsc, dy_branch, dgate = _norm_mod_bwd(
            f"ffn{i}_norm_bwd", dh, xin, row(norm_ffn[i]), None, sc, dxo, None, nb, branch=branch)
        return dxi, (gwup, gwd.reshape(N_DEV, r_dn, d), dcw, dcb, g_norm, dsh, dsc), dy_branch, dgate

    dx3, (gwup1, gwd1, dcw1, dcb1, g_nffn1, dsh2b, dsc2b), dys2, dg1b = ffn_bwd(
        1, dyf1, dx4, x3, sc2b, ffn1_saved, (ys2, g1b))
    dgg = _mm_nt("ssm_out_dx", dys2, wos, F32)
    gwos = _mm_tn("ssm_out_dw", gg, dys2, BF16)
    dgl, dz1, g_bglu = _glu_bwd(dgg, z_ssm, gl, bglu_f)
    dz2 = _mm_nt("ssm_glu_dx", dgl, wglu, F32)
    gwglu = _mm_tn("ssm_glu_dw", z_ssm, dgl, BF16)
    dy_ssm, du_skip, g_dskip = _ssm_post_bwd(dz1, dz2, y_ssm, u, dskip_f)
    du_core, d_ops = _ssm_core_bwd(dy_ssm, u, xp_re, xp_im, ops, nb)
    du = _add_cast(du_core, du_skip)
    dh3 = _mm_nt("ssm_in_dx", du, win, F32)
    gwin = _mm_tn("ssm_in_dw", h3, du, BF16)
    dx2, g_nmix1, dsh1b, dsc1b, dyf0, dg2 = _norm_mod_bwd(
        "ssm_norm_bwd", dh3, x2, row(norm_mix[1]), None, sc1b, dx3, None, nb, branch=(yf0, g2))
    g_ssm_params = ops_vjp(d_ops)

    dx1, (gwup0, gwd0, dcw0, dcb0, g_nffn0, dsh2, dsc2), dya, dg1 = ffn_bwd(
        0, dyf0, dx2, x1, sc2, ffn0_saved, (ya, g1))
    do_att = _mm_nt("attn_out_dx", dya, wo, BF16)
    gwo = _mm_tn("attn_out_dw", o_att, dya, BF16)
    rows8 = lambda a: a.reshape(N_DEV, d // N_DEV, d)
    def two_d(w):
        shp = w.shape
        if len(shp) == 1:
            return (1, shp[0])
        if len(shp) == 2:
            return shp
        return (shp[0] * shp[1], math.prod(shp[2:]))

    ssm_w = [a_re, a_im, log_dt, b_re, b_im, c_re, c_im]
    ssm_partial = [g.reshape(two_d(w)) for g, w in zip(g_ssm_params, ssm_w)]
    early = _Exchange([rows8(gwo), rows8(gwin), rows8(gwglu), rows8(gwos), gwup0, gwup1, gwd0, gwd1] + ssm_partial,
                      [False] * 8 + [True] * 7)
    dq, dk, dv, early_res = _attn_bwd(qkv, car_att, do_att, nb, s, d, early)
    ro, rin, rglu, ros, rup0, rup1, rd0, rd1 = early_res[:8]
    ssm8 = early_res[8:]
    dqkv = jnp.concatenate([dq, dk, dv], axis=1)
    tk = _tile(t, 1024)
    gwq8 = _mm("attn_qkv_dw", h1, dqkv, (1, N_DEV, t // tk),
               pl.BlockSpec((tk, d), lambda a, b, k: (k, 0)),
               pl.BlockSpec((tk, cq), lambda a, b, k: (k, b)),
               pl.BlockSpec((None, d, cq), lambda a, b, k: (b, 0, 0)),
               jax.ShapeDtypeStruct((N_DEV, d, cq), BF16), _TN, (d, cq))
    dh1, (rq,) = _mm("attn_qkv_dx", dqkv, wq8, (t // tm_, 1, N_DEV),
                     pl.BlockSpec((tm_, cq), lambda a, b, k: (a, k)),
                     pl.BlockSpec((None, d, cq), lambda a, b, k: (k, 0, 0)),
                     pl.BlockSpec((tm_, d), lambda a, b, k: (a, 0)), jax.ShapeDtypeStruct((t, d), F32), _NT, (tm_, d),
                     ex=_Exchange([gwq8], [False]))
    dx0, g_nmix0, dsh1, dsc1 = _norm_mod_bwd("attn_norm_bwd", dh1, x0, row(norm_mix[0]), None, sc1, dx1, None, nb)
    grad_x = dx0.reshape(nb, s, d)

    dmod = [jnp.concatenate([a.reshape(nb, d) for a in grp], axis=1) for grp in
            ([dsh1, dsc1, dg1, dsh2, dsc2, dg2], [dsh1b, dsc1b, dg1b, dsh2b, dsc2b, dg2b])]
    dfin = jnp.concatenate([dsh_f.reshape(nb, d), dsc_f.reshape(nb, d)], axis=1)
    dmodfin = jnp.concatenate(dmod + [dfin], axis=1)
    flat = lambda a: a.reshape(1, -1)
    dmf8, nmix8, nffn8, nout8, cb8, dskip_g8, bglu_g8, cw_g8 = _exchange(
        "exchange_last",
        [dmodfin, jnp.concatenate([g_nmix0, g_nmix1]), jnp.concatenate([g_nffn0, g_nffn1]), g_norm_out,
         jnp.concatenate([flat(dcb0), flat(dcb1)]), g_dskip, g_bglu, jnp.stack([dcw0, dcw1])], [True] * 8)
    shard = d // N_DEV
    dskip_g8 = lax.dynamic_slice_in_dim(dskip_g8, me * shard, shard, axis=2)
    bglu_g8 = lax.dynamic_slice_in_dim(bglu_g8, me * shard, shard, axis=2)
    cw_g8 = lax.dynamic_slice_in_dim(cw_g8, me, 1, axis=2).reshape(N_DEV, 2 * 3, c_up)

    dall = dmf8.reshape(n_seq, 14 * d)
    dmod_loc = jnp.stack([lax.dynamic_slice_in_dim(dall[:, i * 6 * d:(i + 1) * 6 * d], me * cm, cm, axis=1)
                          for i in range(2)])
    dfin_loc = lax.dynamic_slice_in_dim(dall[:, 12 * d:], me * cf, cf, axis=1)
    g_w_mod, g_w_fin, g_bias = _modfin_bwd(c_act.T, dmod_loc, dfin_loc, dall)
    g_b_mod = g_bias[0, :12 * d].reshape(2, 6 * d)
    g_b_fin = g_bias[0, 12 * d:]

    def big(name, parts, w, m, v):
        shp = w.shape
        r2 = lambda a: a.reshape(-1, shp[-1])
        res = _adamw(name, parts.reshape(parts.shape[0], -1, shp[-1]), r2(w), r2(m), r2(v))
        return [a.reshape(shp) for a in res]

    upd = {}
    upd["w_mod"] = big("adamw_w_mod", g_w_mod[None], w_mod, m_w_mod, v_w_mod)
    upd["w_fin"] = big("adamw_w_fin", g_w_fin[None], w_fin, m_w_fin, v_w_fin)
    upd["w_qkv"] = big("adamw_w_qkv", rq, w_qkv, m_w_qkv, v_w_qkv)
    upd["w_o_attn"] = big("adamw_w_o_attn", ro, w_o_attn, m_w_o_attn, v_w_o_attn)
    upd["w_in_ssm"] = big("adamw_w_in_ssm", rin, w_in_ssm, m_w_in_ssm, v_w_in_ssm)
    upd["w_glu"] = big("adamw_w_glu", rglu, w_glu, m_w_glu, v_w_glu)
    upd["w_o_ssm"] = big("adamw_w_o_ssm", ros, w_o_ssm, m_w_o_ssm, v_w_o_ssm)
    upd["w_up"] = _adamw_layers("adamw_w_up", [rup0, rup1], w_up, m_w_up, v_w_up)
    upd["w_down"] = _adamw_layers("adamw_w_down", [rd0, rd1], w_down, m_w_down, v_w_down)

    small_names = ["norm_mix", "norm_ffn", "b_mod", "a_re", "a_im", "log_dt", "b_re", "b_im", "c_re", "c_im",
                   "d_skip", "b_glu", "conv_w", "conv_b", "norm_out", "b_fin"]
    small_g = [nmix8, nffn8, g_b_mod[None], *ssm8, dskip_g8, bglu_g8, cw_g8, cb8, nout8, g_b_fin[None]]
    small_w = [norm_mix, norm_ffn, b_mod, a_re, a_im, log_dt, b_re, b_im, c_re, c_im, d_skip, b_glu, conv_w, conv_b,
               norm_out, b_fin]
    small_m = [m_norm_mix, m_norm_ffn, m_b_mod, m_a_re, m_a_im, m_log_dt, m_b_re, m_b_im, m_c_re, m_c_im, m_d_skip,
               m_b_glu, m_conv_w, m_conv_b, m_norm_out, m_b_fin]
    small_v = [v_norm_mix, v_norm_ffn, v_b_mod, v_a_re, v_a_im, v_log_dt, v_b_re, v_b_im, v_c_re, v_c_im, v_d_skip,
               v_b_glu, v_conv_w, v_conv_b, v_norm_out, v_b_fin]
    entries = [(gp.reshape((gp.shape[0],) + two_d(w)), w.reshape(two_d(w)), m.reshape(two_d(w)), v.reshape(two_d(w)))
               for gp, w, m, v in zip(small_g, small_w, small_m, small_v)]
    res = _adamw_many("adamw_small", entries)
    for j, (nm, w) in enumerate(zip(small_names, small_w)):
        upd[nm] = [res[4 * j + k].reshape(w.shape) for k in range(4)]

    order = ["norm_mix", "norm_ffn", "w_mod", "b_mod", "w_qkv", "w_o_attn", "w_in_ssm", "a_re", "a_im", "log_dt",
             "b_re", "b_im", "c_re", "c_im", "d_skip", "w_glu", "b_glu", "w_o_ssm", "w_up", "conv_w", "conv_b",
             "w_down", "norm_out", "w_fin", "b_fin"]
    outs = [loss, grad_x]
    for k in range(4):
        outs += [upd[nm][k] for nm in order]
    return tuple(outs)
```

```python
import functools
import math

import jax
import jax.numpy as jnp
from jax import lax
from jax.experimental import pallas as pl
from jax.experimental.pallas import tpu as pltpu

F32 = jnp.float32
BF16 = jnp.bfloat16
MESH = pl.DeviceIdType.MESH

N_DEV = 8
HEAD_DIM = 64
ATT_BLK = 128
ATT_BQ = 256
ATT_UNROLL = 2
SSM_H = 16
SSM_P = 64
SSM_L = 4
EPS = 1e-6
ADAM_LR, ADAM_B1, ADAM_B2, ADAM_EPS, ADAM_WD, ADAM_STEP = 0.001, 0.9, 0.999, 1e-08, 0.01, 10
V7X_VMEM_LIMIT = 56 * 1024 * 1024
LANE = 128

_NN = (((1,), (0,)), ((), ()))
_NT = (((1,), (1,)), ((), ()))
_TN = (((0,), (0,)), ((), ()))


def _cp(sem):
    return pltpu.CompilerParams(dimension_semantics=sem, vmem_limit_bytes=V7X_VMEM_LIMIT)


def _tile(n, pref):
    if n <= pref:
        return n
    t = pref - pref % 16
    while t >= 16:
        if n % t == 0:
            return t
        t -= 16
    return n


def _mm(name, a, b, grid, a_spec, b_spec, out_spec, out_shape, dims, acc_shape, ex=None):
    nk = grid[-1]
    kax = len(grid) - 1
    ex = ex or _NoExchange()

    def body(*refs):
        a_ref, b_ref = refs[:2]
        ex_ins = refs[2:2 + ex.n]
        o_ref = refs[2 + ex.n]
        ex_outs = refs[3 + ex.n:3 + 2 * ex.n]
        acc_ref = refs[3 + 2 * ex.n]
        sems = refs[4 + 2 * ex.n:]
        first, last = _grid_ends(grid)
        k = pl.program_id(kax)

        @pl.when(first)
        def _():
            ex.start(ex_ins, ex_outs, sems)

        @pl.when(k == 0)
        def _():
            acc_ref[...] = jnp.zeros(acc_shape, F32)

        acc_ref[...] += lax.dot_general(a_ref[...].astype(BF16), b_ref[...].astype(BF16), dims,
                                        preferred_element_type=F32)

        @pl.when(k == nk - 1)
        def _():
            o_ref[...] = acc_ref[...].astype(o_ref.dtype)

        @pl.when(last)
        def _():
            ex.wait(ex_ins, ex_outs, sems)

    sem = ("arbitrary",) * len(grid) if ex.n else ("parallel",) * kax + ("arbitrary",)
    res = pl.pallas_call(
        body, name=name, grid=grid, in_specs=[a_spec, b_spec] + ex.specs, out_specs=[out_spec] + ex.specs,
        out_shape=[out_shape] + ex.out_shape, scratch_shapes=[pltpu.VMEM(acc_shape, F32)] + ex.scratch,
        compiler_params=_cp(sem))(a, b, *ex.arrs)
    return (res[0], list(res[1:])) if ex.n else res[0]


def _mm_nn(name, a, w, out_dtype):
    m, k = a.shape
    n = w.shape[1]
    tm, tn, tk = _tile(m, 512), _tile(n, 1024), _tile(k, 1024)
    return _mm(name, a, w, (m // tm, n // tn, k // tk),
               pl.BlockSpec((tm, tk), lambda i, j, kk: (i, kk)), pl.BlockSpec((tk, tn), lambda i, j, kk: (kk, j)),
               pl.BlockSpec((tm, tn), lambda i, j, kk: (i, j)), jax.ShapeDtypeStruct((m, n), out_dtype), _NN, (tm, tn))


def _mm_nt(name, a, w, out_dtype):
    m, n = a.shape
    k = w.shape[0]
    tm, tko, tn = _tile(m, 512), _tile(k, 1024), _tile(n, 1024)
    return _mm(name, a, w, (m // tm, k // tko, n // tn),
               pl.BlockSpec((tm, tn), lambda i, j, kk: (i, kk)), pl.BlockSpec((tko, tn), lambda i, j, kk: (j, kk)),
               pl.BlockSpec((tm, tko), lambda i, j, kk: (i, j)), jax.ShapeDtypeStruct((m, k), out_dtype), _NT, (tm, tko))


def _mm_tn(name, a, b, out_dtype):
    t, m = a.shape
    n = b.shape[1]
    tm, tn, tk = _tile(m, 512), _tile(n, 1024), _tile(t, 1024)
    return _mm(name, a, b, (m // tm, n // tn, t // tk),
               pl.BlockSpec((tk, tm), lambda i, j, kk: (kk, i)), pl.BlockSpec((tk, tn), lambda i, j, kk: (kk, j)),
               pl.BlockSpec((tm, tn), lambda i, j, kk: (i, j)), jax.ShapeDtypeStruct((m, n), out_dtype), _TN, (tm, tn))


def _norm_mod_fwd(name, x, g, shift, scale, nb):
    t, d = x.shape
    s = t // nb
    tr = _tile(s, 512)
    nt = s // tr

    def body(x_ref, g_ref, sh_ref, sc_ref, h_ref):
        xv = x_ref[...]
        r = lax.rsqrt(jnp.mean(xv * xv, axis=-1, keepdims=True) + EPS)
        y = xv * r * g_ref[...]
        h_ref[...] = (y * (1.0 + sc_ref[...]) + sh_ref[...]).astype(h_ref.dtype)

    row = pl.BlockSpec((tr, d), lambda b, i: (b * nt + i, 0))
    vec = pl.BlockSpec((None, 1, d), lambda b, i: (b, 0, 0))
    return pl.pallas_call(body, name=name, grid=(nb, nt),
                          in_specs=[row, pl.BlockSpec((1, d), lambda b, i: (0, 0)), vec, vec],
                          out_specs=row, out_shape=jax.ShapeDtypeStruct((t, d), BF16),
                          compiler_params=_cp(("parallel", "parallel")))(x, g, shift, scale)


def _norm_mod_bwd(name, dh, x, g, shift, scale, dres, target, nb, branch=None):
    t, d = x.shape
    s = t // nb
    tr = _tile(s, 512)
    nt = s // tr
    final = target is not None
    n_in = 5 + 2 * (branch is not None)

    def body(*refs):
        if final:
            x_ref, g_ref, sh_ref, sc_ref, tg_ref = refs[:5]
        else:
            dh_ref, x_ref, g_ref, sc_ref, dres_ref = refs[:5]
        dx_ref, dg_ref, dsh_ref, dsc_ref = refs[n_in:n_in + 4]
        if final:
            loss_ref = refs[n_in + 4]
        b, i = pl.program_id(0), pl.program_id(1)
        xv = x_ref[...]
        gv = g_ref[...]
        r = lax.rsqrt(jnp.mean(xv * xv, axis=-1, keepdims=True) + EPS)
        nrm = xv * r
        y = nrm * gv
        one_sc = 1.0 + sc_ref[...]
        if final:
            err = y * one_sc + sh_ref[...] - tg_ref[...]
            dhv = err * (1.0 / d)
        else:
            dhv = dh_ref[...].astype(F32)
        dy = dhv * one_sc
        dn = dy * gv
        dxv = r * (dn - nrm * jnp.mean(dn * nrm, axis=-1, keepdims=True))
        dtot = dxv if final else dres_ref[...] + dxv
        dx_ref[...] = dtot

        @pl.when(i == 0)
        def _():
            dsh_ref[...] = jnp.zeros_like(dsh_ref)
            dsc_ref[...] = jnp.zeros_like(dsc_ref)

        if branch is not None:
            yb_ref, gate_ref = refs[5:7]
            dyb_ref, dgate_ref = refs[-2:]
            dyb_ref[...] = (gate_ref[...] * dtot).astype(dyb_ref.dtype)

            @pl.when(i == 0)
            def _():
                dgate_ref[...] = jnp.zeros_like(dgate_ref)

            dgate_ref[...] += jnp.sum(dtot * yb_ref[...], axis=0, keepdims=True)

        @pl.when((i == 0) & (b == 0))
        def _():
            dg_ref[...] = jnp.zeros_like(dg_ref)
            if final:
                loss_ref[...] = jnp.zeros_like(loss_ref)

        dsh_ref[...] += jnp.sum(dhv, axis=0, keepdims=True)
        dsc_ref[...] += jnp.sum(dhv * y, axis=0, keepdims=True)
        dg_ref[...] += jnp.sum(dy * nrm, axis=0, keepdims=True)
        if final:
            loss_ref[...] += (0.5 / d) * jnp.sum(err * err)

    row = pl.BlockSpec((tr, d), lambda b, i: (b * nt + i, 0))
    vec = pl.BlockSpec((None, 1, d), lambda b, i: (b, 0, 0))
    gsp = pl.BlockSpec((1, d), lambda b, i: (0, 0))
    out_specs = [row, gsp, vec, vec]
    out_shape = [jax.ShapeDtypeStruct((t, d), F32), jax.ShapeDtypeStruct((1, d), F32),
                 jax.ShapeDtypeStruct((nb, 1, d), F32), jax.ShapeDtypeStruct((nb, 1, d), F32)]
    if final:
        ins, in_specs = [x, g, shift, scale, target], [row, gsp, vec, vec, row]
        out_specs.append(pl.BlockSpec((8, LANE), lambda b, i: (0, 0)))
        out_shape.append(jax.ShapeDtypeStruct((8, LANE), F32))
    else:
        ins, in_specs = [dh, x, g, scale, dres], [row, row, gsp, vec, row]
    if branch is not None:
        ins, in_specs = ins + list(branch), in_specs + [row, vec]
        out_specs += [row, vec]
        out_shape += [jax.ShapeDtypeStruct((t, d), BF16), jax.ShapeDtypeStruct((nb, 1, d), F32)]
    return pl.pallas_call(body, name=name, grid=(nb, nt), in_specs=in_specs, out_specs=out_specs,
                          out_shape=out_shape, compiler_params=_cp(("arbitrary", "arbitrary")))(*ins)


def _gate_add(name, x, y, gate, nb):
    t, d = x.shape
    s = t // nb
    tr = _tile(s, 512)
    nt = s // tr

    def body(x_ref, y_ref, g_ref, o_ref):
        o_ref[...] = x_ref[...] + g_ref[...] * y_ref[...]

    row = pl.BlockSpec((tr, d), lambda b, i: (b * nt + i, 0))
    vec = pl.BlockSpec((None, 1, d), lambda b, i: (b, 0, 0))
    return pl.pallas_call(body, name=name, grid=(nb, nt), in_specs=[row, row, vec], out_specs=row,
                          out_shape=jax.ShapeDtypeStruct((t, d), F32),
                          compiler_params=_cp(("parallel", "parallel")))(x, y, gate)


def _res_norm_fwd(name, x, y, gate, g, shift, scale, nb):
    t, d = x.shape
    s = t // nb
    tr = _tile(s, 512)
    nt = s // tr

    def body(x_ref, y_ref, gate_ref, g_ref, sh_ref, sc_ref, xo_ref, h_ref):
        xn = x_ref[...] + gate_ref[...] * y_ref[...]
        xo_ref[...] = xn
        r = lax.rsqrt(jnp.mean(xn * xn, axis=-1, keepdims=True) + EPS)
        h_ref[...] = (xn * r * g_ref[...] * (1.0 + sc_ref[...]) + sh_ref[...]).astype(h_ref.dtype)

    row = pl.BlockSpec((tr, d), lambda b, i: (b * nt + i, 0))
    vec = pl.BlockSpec((None, 1, d), lambda b, i: (b, 0, 0))
    return pl.pallas_call(body, name=name, grid=(nb, nt),
                          in_specs=[row, row, vec, pl.BlockSpec((1, d), lambda b, i: (0, 0)), vec, vec],
                          out_specs=[row, row],
                          out_shape=[jax.ShapeDtypeStruct((t, d), F32), jax.ShapeDtypeStruct((t, d), BF16)],
                          compiler_params=_cp(("parallel", "parallel")))(x, y, gate, g, shift, scale)


def _log_sigmoid(z):
    return jnp.minimum(z, 0.0) - jnp.log(1.0 + jnp.exp(-jnp.abs(z)))


def _split_dot(v, tri):
    hi = v.astype(BF16)
    lo = (v - hi.astype(F32)).astype(BF16)
    return (jnp.dot(hi, tri, preferred_element_type=F32) + jnp.dot(lo, tri, preferred_element_type=F32))


def _grid_ends(grid):
    ids = [pl.program_id(a) for a in range(len(grid))]
    first = functools.reduce(lambda u, w: u & w, [i == 0 for i in ids])
    last = functools.reduce(lambda u, w: u & w, [i == n - 1 for i, n in zip(ids, grid)])
    return first, last


def _attn_trips(nq):
    return nq * (nq + 1) // 2


def _next_trip(qi, jj, nq):
    wrap = jj >= qi
    nqi = jnp.where(wrap, jnp.minimum(qi + 1, nq - 1), qi)
    njj = jnp.where(wrap, jnp.where(qi + 1 < nq, 0, jj), jj + 1)
    return nqi, njj


def _attn_fwd(qkv, nb, s, d, ex):
    t = nb * s
    npair = d // LANE
    bk = ATT_BLK
    bq = min(ATT_BQ, s)
    nq = s // bq
    kpq = bq // bk
    nheads = LANE // HEAD_DIM
    scale = HEAD_DIM ** -0.5
    grid = (nb, npair)
    assert s // bk <= HEAD_DIM, "one carry lane per key block and head"
    assert bk == LANE, "the running sums are kept one 128-lane tile wide"
    assert kpq == ATT_UNROLL, "query block qi has exactly qi + 1 trips"

    def body(*refs):
        q_ref, k_ref, v_ref = refs[:3]
        ex_ins = refs[3:3 + ex.n]
        o_ref, car_ref = refs[3 + ex.n:5 + ex.n]
        ex_outs = refs[5 + ex.n:5 + 2 * ex.n]
        acc_s, run_s, z_s, arg_s = refs[5 + 2 * ex.n:9 + 2 * ex.n]
        sems = refs[9 + 2 * ex.n:]
        first, last = _grid_ends(grid)

        @pl.when(first)
        def _():
            ex.start(ex_ins, ex_outs, sems)

        lane = lax.broadcasted_iota(jnp.int32, (1, LANE), 1)
        row = lax.broadcasted_iota(jnp.int32, (bq, bk), 0)
        col = lax.broadcasted_iota(jnp.int32, (bq, bk), 1)
        trow = lax.broadcasted_iota(jnp.int32, (bk, bk), 0)
        tcol = lax.broadcasted_iota(jnp.int32, (bk, bk), 1)
        tri = (trow > tcol).astype(BF16)
        hms = [(lane // HEAD_DIM) == hh for hh in range(nheads)]

        def q0_of(qi):
            return pl.multiple_of(qi * bq, bq)

        def kblk_of(qi, jj, u):
            return (qi + 1) * kpq - 1 - (ATT_UNROLL * jj + u)

        def scores(qi, jj):
            q = q_ref[pl.ds(q0_of(qi), bq), :]
            qhs = [jnp.where(hm, q, jnp.zeros_like(q)) * scale for hm in hms]
            ks = [k_ref[pl.ds(pl.multiple_of(kblk_of(qi, jj, u) * bk, bk), bk), :] for u in range(ATT_UNROLL)]
            return [[lax.dot_general(qhs[hh], kj, _NT, preferred_element_type=F32) for kj in ks]
                    for hh in range(nheads)]

        def keep(zn):
            for hh in range(nheads):
                for u in range(ATT_UNROLL):
                    z_s[hh, u] = zn[hh][u]

        def exponents(qi, jj):
            q0 = q0_of(qi)
            car = car_ref[pl.ds(q0, bq), :]
            for hh in range(nheads):
                run = jnp.where(jj == 0, 0.0, run_s[hh])
                for u in range(ATT_UNROLL):
                    j = kblk_of(qi, jj, u)
                    mask = (j * bk + col) < (q0 + row)
                    z = z_s[hh, u]
                    lb = _log_sigmoid(z)
                    l1 = jnp.where(mask, lb - z, 0.0)
                    arg_s[hh, u] = jnp.where(mask, lb + (_split_dot(l1, tri) + run), -1e30)
                    car = jnp.where(lane == hh * HEAD_DIM + j, run, car)
                    run = run + jnp.sum(l1, axis=1, keepdims=True)
                run_s[hh] = run
            car_ref[pl.ds(q0, bq), :] = car

        def weigh(qi, jj):
            q0 = q0_of(qi)
            for hh in range(nheads):
                acc = None
                for u in range(ATT_UNROLL):
                    vj = v_ref[pl.ds(pl.multiple_of(kblk_of(qi, jj, u) * bk, bk), bk), :]
                    pv = jnp.dot(jnp.exp(arg_s[hh, u]).astype(BF16), vj, preferred_element_type=F32)
                    acc = pv if acc is None else acc + pv
                acc_s[hh, pl.ds(q0, bq), :] += acc

        def step(n, carry):
            qi, jj, pqi, pjj = carry
            nqi, njj = _next_trip(qi, jj, nq)
            zn = scores(nqi, njj)
            weigh(pqi, pjj)
            exponents(qi, jj)
            keep(zn)
            return nqi, njj, qi, jj

        acc_s[...] = jnp.zeros_like(acc_s)
        run_s[...] = jnp.zeros_like(run_s)
        car_ref[...] = jnp.zeros_like(car_ref)
        arg_s[...] = jnp.full(arg_s.shape, -1e30, F32)
        zero = jnp.int32(0)
        keep(scores(zero, zero))
        _, _, lqi, ljj = lax.fori_loop(0, _attn_trips(nq), step, (zero, zero, zero, zero))
        weigh(lqi, ljj)
        out = acc_s[0]
        for hh in range(1, nheads):
            out = jnp.where(hms[hh], acc_s[hh], out)
        o_ref[...] = out.astype(o_ref.dtype)

        @pl.when(last)
        def _():
            ex.wait(ex_ins, ex_outs, sems)

    seq = lambda off: pl.BlockSpec((s, LANE), lambda b, p: (b, off + p))
    res = pl.pallas_call(
        body, name="attn_fwd", grid=grid,
        in_specs=[seq(0), seq(npair), seq(2 * npair)] + ex.specs,
        out_specs=[seq(0), seq(0)] + ex.specs,
        out_shape=[jax.ShapeDtypeStruct((t, d), BF16), jax.ShapeDtypeStruct((t, d), F32)] + ex.out_shape,
        scratch_shapes=[pltpu.VMEM((nheads, s, LANE), F32), pltpu.VMEM((nheads, bq, LANE), F32),
                        pltpu.VMEM((nheads, ATT_UNROLL, bq, bk), F32),
                        pltpu.VMEM((nheads, ATT_UNROLL, bq, bk), F32)] + ex.scratch,
        compiler_params=_cp(("arbitrary", "arbitrary")))(qkv, qkv, qkv, *ex.arrs)
    return res[0], res[1], list(res[2:])


def _attn_bwd(qkv, car, do, nb, s, d, ex):
    t = nb * s
    npair = d // LANE
    bk = ATT_BLK
    bq = min(ATT_BQ, s)
    nq = s // bq
    kpq = bq // bk
    nheads = LANE // HEAD_DIM
    scale = HEAD_DIM ** -0.5
    grid = (nb, npair)
    assert kpq == ATT_UNROLL, "query block qi has exactly qi + 1 trips"

    def body(*refs):
        q_ref, k_ref, v_ref, car_ref, do_ref = refs[:5]
        ex_ins = refs[5:5 + ex.n]
        dq_ref, dk_ref, dv_ref = refs[5 + ex.n:8 + ex.n]
        ex_outs = refs[8 + ex.n:8 + 2 * ex.n]
        dk_acc, dv_acc, dq_s, rune_s, z_s, da_s, dz_s, a_s = refs[8 + 2 * ex.n:16 + 2 * ex.n]
        sems = refs[16 + 2 * ex.n:]
        first, last = _grid_ends(grid)

        @pl.when(first)
        def _():
            ex.start(ex_ins, ex_outs, sems)

        lane = lax.broadcasted_iota(jnp.int32, (1, LANE), 1)
        row = lax.broadcasted_iota(jnp.int32, (bq, bk), 0)
        col = lax.broadcasted_iota(jnp.int32, (bq, bk), 1)
        trow = lax.broadcasted_iota(jnp.int32, (bk, bk), 0)
        tcol = lax.broadcasted_iota(jnp.int32, (bk, bk), 1)
        tri_suf = (trow > tcol).astype(BF16)
        tri_pre = (trow < tcol).astype(BF16)
        hms = [(lane // HEAD_DIM) == hh for hh in range(nheads)]

        def q0_of(qi):
            return pl.multiple_of(qi * bq, bq)

        def k0_of(jj, u):
            return pl.multiple_of((ATT_UNROLL * jj + u) * bk, bk)

        def heads_of(ref, qi, factor):
            x = ref[pl.ds(q0_of(qi), bq), :]
            return [jnp.where(hm, x, jnp.zeros_like(x)) * factor for hm in hms]

        def products(qi, jj):
            qhs, dohs = heads_of(q_ref, qi, scale), heads_of(do_ref, qi, 1.0)
            ks = [k_ref[pl.ds(k0_of(jj, u), bk), :] for u in range(ATT_UNROLL)]
            vs = [v_ref[pl.ds(k0_of(jj, u), bk), :] for u in range(ATT_UNROLL)]
            zn = [[lax.dot_general(qhs[hh], kj, _NT, preferred_element_type=F32) for kj in ks] for hh in range(nheads)]
            dn = [[lax.dot_general(dohs[hh], vj, _NT, preferred_element_type=F32) for vj in vs] for hh in range(nheads)]
            return zn, dn

        def keep(zn, dn):
            for hh in range(nheads):
                for u in range(ATT_UNROLL):
                    z_s[hh, u] = zn[hh][u]
                    da_s[hh, u] = dn[hh][u]

        def middle(qi, jj):
            q0 = q0_of(qi)
            car = car_ref[pl.ds(q0, bq), :]
            for hh in range(nheads):
                run_e = jnp.where(jj == 0, 0.0, rune_s[hh])
                for u in range(ATT_UNROLL):
                    j = ATT_UNROLL * jj + u
                    mask = (j * bk + col) < (q0 + row)
                    z = z_s[hh, u]
                    lb = _log_sigmoid(z)
                    l1u = lb - z
                    l1 = jnp.where(mask, l1u, 0.0)
                    run = jnp.sum(jnp.where(lane == hh * HEAD_DIM + j, car, 0.0), axis=1, keepdims=True)
                    a = jnp.where(mask, jnp.exp(lb + (_split_dot(l1, tri_suf) + run)), 0.0)
                    e = da_s[hh, u] * a
                    dz = e * jnp.exp(l1u) - (_split_dot(e, tri_pre) + run_e) * jnp.exp(lb)
                    dz_s[hh, u] = jnp.where(mask, dz, 0.0).astype(BF16)
                    a_s[hh, u] = a.astype(BF16)
                    run_e = run_e + jnp.sum(e, axis=1, keepdims=True)
                rune_s[hh] = run_e

        def grads(qi, jj):
            q0 = q0_of(qi)
            qhs, dohs = heads_of(q_ref, qi, scale), heads_of(do_ref, qi, 1.0)
            dqs = [None] * nheads
            for u in range(ATT_UNROLL):
                k0 = k0_of(jj, u)
                kj = k_ref[pl.ds(k0, bk), :]
                for hh in range(nheads):
                    dzb = dz_s[hh, u]
                    dqu = jnp.dot(dzb, kj, preferred_element_type=F32)
                    dqs[hh] = dqu if dqs[hh] is None else dqs[hh] + dqu
                    dkh = lax.dot_general(dzb, qhs[hh], _TN, preferred_element_type=F32)
                    dvh = lax.dot_general(a_s[hh, u], dohs[hh], _TN, preferred_element_type=F32)
                    dk_blk = dkh if hh == 0 else dk_blk + dkh
                    dv_blk = dvh if hh == 0 else dv_blk + dvh
                dk_acc[pl.ds(k0, bk), :] += dk_blk
                dv_acc[pl.ds(k0, bk), :] += dv_blk
            for hh in range(nheads):
                dq_s[hh, pl.ds(q0, bq), :] += dqs[hh]

        def step(n, carry):
            qi, jj, pqi, pjj = carry
            nqi, njj = _next_trip(qi, jj, nq)
            zn, dn = products(nqi, njj)
            grads(pqi, pjj)
            middle(qi, jj)
            keep(zn, dn)
            return nqi, njj, qi, jj

        dk_acc[...] = jnp.zeros_like(dk_acc)
        dv_acc[...] = jnp.zeros_like(dv_acc)
        dq_s[...] = jnp.zeros_like(dq_s)
        rune_s[...] = jnp.zeros_like(rune_s)
        dz_s[...] = jnp.zeros_like(dz_s)
        a_s[...] = jnp.zeros_like(a_s)
        zero = jnp.int32(0)
        keep(*products(zero, zero))
        _, _, lqi, ljj = lax.fori_loop(0, _attn_trips(nq), step, (zero, zero, zero, zero))
        grads(lqi, ljj)
        dq_out = dq_s[0]
        for hh in range(1, nheads):
            dq_out = jnp.where(hms[hh], dq_s[hh], dq_out)
        dq_ref[...] = (dq_out * scale).astype(dq_ref.dtype)
        dk_ref[...] = dk_acc[...].astype(dk_ref.dtype)
        dv_ref[...] = dv_acc[...].astype(dv_ref.dtype)

        @pl.when(last)
        def _():
            ex.wait(ex_ins, ex_outs, sems)

    seq = lambda off: pl.BlockSpec((s, LANE), lambda b, p: (b, off + p))
    sds = jax.ShapeDtypeStruct((t, d), BF16)
    res = pl.pallas_call(
        body, name="attn_bwd", grid=grid,
        in_specs=[seq(0), seq(npair), seq(2 * npair), seq(0), seq(0)] + ex.specs,
        out_specs=[seq(0), seq(0), seq(0)] + ex.specs, out_shape=[sds, sds, sds] + ex.out_shape,
        scratch_shapes=[pltpu.VMEM((s, LANE), F32), pltpu.VMEM((s, LANE), F32),
                        pltpu.VMEM((nheads, s, LANE), F32), pltpu.VMEM((nheads, bq, LANE), F32),
                        pltpu.VMEM((nheads, ATT_UNROLL, bq, bk), F32), pltpu.VMEM((nheads, ATT_UNROLL, bq, bk), F32),
                        pltpu.VMEM((nheads, ATT_UNROLL, bq, bk), BF16),
                        pltpu.VMEM((nheads, ATT_UNROLL, bq, bk), BF16)] + ex.scratch,
        compiler_params=_cp(("arbitrary", "arbitrary")))(qkv, qkv, qkv, car, do, *ex.arrs)
    return res[0], res[1], res[2], list(res[3:])


def _conv3(u_ref, w, bias, c, r0, rc):
    x = u_ref[pl.ds(r0, rc), :].astype(F32)
    p0 = pl.multiple_of(jnp.maximum(r0 - 16, 0), 16)
    prev = u_ref[pl.ds(p0, 16), :].astype(F32)
    prev = jnp.where(c > 0, prev, 0.0)
    row = lax.broadcasted_iota(jnp.int32, (rc, 1), 0)
    s1 = jnp.where(row == 0, prev[15:16, :], pltpu.roll(x, 1, 0))
    s2 = jnp.where(row == 0, prev[14:15, :], jnp.where(row == 1, prev[15:16, :], pltpu.roll(x, 2, 0)))
    cv = w[2:3, :] * x + w[1:2, :] * s1 + w[0:1, :] * s2 + bias
    return cv, x, s1, s2


def _sigmoid(x):
    return 1.0 / (1.0 + jnp.exp(-x))


def _ffn_act_fwd(name, up8, cw8, cb8, nb, s):
    _, t, c_w = up8.shape
    rc = _tile(s, 256)
    nch = s // rc
    half = N_DEV // 2

    def body(ug_ref, uv_ref, wg_ref, wv_ref, bg_ref, bv_ref, act_ref):
        wg, wv, bg, bv = wg_ref[...], wv_ref[...], bg_ref[...], bv_ref[...]

        def chunk(c, carry):
            r0 = pl.multiple_of(c * rc, rc)
            cg = _conv3(ug_ref, wg, bg, c, r0, rc)[0]
            cv = _conv3(uv_ref, wv, bv, c, r0, rc)[0]
            act_ref[pl.ds(r0, rc), :] = (cg * _sigmoid(cg) * cv).astype(act_ref.dtype)
            return carry

        lax.fori_loop(0, nch, chunk, 0)

    def slab(off):
        return pl.BlockSpec((None, s, c_w), lambda k, b: (k + off, b, 0))

    def par(rows, off):
        return pl.BlockSpec((None, rows, c_w), lambda k, b: (k + off, 0, 0))

    return pl.pallas_call(
        body, name=name, grid=(half, nb),
        in_specs=[slab(0), slab(half), par(3, 0), par(3, half), par(1, 0), par(1, half)],
        out_specs=pl.BlockSpec((None, s, c_w), lambda k, b: (k, b, 0)),
        out_shape=jax.ShapeDtypeStruct((half, t, c_w), BF16),
        compiler_params=_cp(("parallel", "parallel")))(up8, up8, cw8, cw8, cb8, cb8)


def _ffn_act_bwd(name, up8, dact4, cw8, cb8, nb, s):
    _, t, c_w = up8.shape
    rc = _tile(s, 256)
    nch = s // rc
    half = N_DEV // 2

    def body(u_ref, da_ref, w_ref, b_ref, dup_ref, dcw_ref, dcb_ref):
        w2, b2 = w_ref[...], b_ref[...]
        row = lax.broadcasted_iota(jnp.int32, (rc, 1), 0)

        @pl.when(pl.program_id(1) == 0)
        def _():
            dcw_ref[...] = jnp.zeros_like(dcw_ref)
            dcb_ref[...] = jnp.zeros_like(dcb_ref)

        def chunk(i, carry):
            c = nch - 1 - i
            r0 = pl.multiple_of(c * rc, rc)
            convs = [_conv3(u_ref.at[h], w2[h], b2[h], c, r0, rc) for h in range(2)]
            gt, vl = convs[0][0], convs[1][0]
            da = da_ref[pl.ds(r0, rc), :].astype(F32)
            sg = _sigmoid(gt)
            dcvs = [da * vl * sg * (1.0 + gt * (1.0 - sg)), da * gt * sg]
            out = []
            for h in range(2):
                n0, n1, a0, a1, a2, ab = carry[6 * h:6 * h + 6]
                dcv, (_, x, s1, s2), w = dcvs[h], convs[h], w2[h]
                t1 = jnp.where(row == rc - 1, n0, pltpu.roll(dcv, rc - 1, 0))
                t2 = jnp.where(row == rc - 2, n0, jnp.where(row == rc - 1, n1, pltpu.roll(dcv, rc - 2, 0)))
                dup = w[2:3, :] * dcv + w[1:2, :] * t1 + w[0:1, :] * t2
                dup_ref[h, pl.ds(r0, rc), :] = dup.astype(dup_ref.dtype)
                out += [dcv[0:1, :], dcv[1:2, :],
                        a0 + jnp.sum(dcv * s2, axis=0, keepdims=True), a1 + jnp.sum(dcv * s1, axis=0, keepdims=True),
                        a2 + jnp.sum(dcv * x, axis=0, keepdims=True), ab + jnp.sum(dcv, axis=0, keepdims=True)]
            return tuple(out)

        z = jnp.zeros((1, c_w), F32)
        fin = lax.fori_loop(0, nch, chunk, (z,) * 12)
        for h in range(2):
            _, _, a0, a1, a2, ab = fin[6 * h:6 * h + 6]
            dcw_ref[h, 0:1, :] += a0
            dcw_ref[h, 1:2, :] += a1
            dcw_ref[h, 2:3, :] += a2
            dcb_ref[h] += ab

    def pair(rows, per_seq):
        return pl.BlockSpec((2, None, rows, c_w), (lambda k, b: (0, k, b, 0)) if per_seq else (lambda k, b: (0, k, 0, 0)))

    four = lambda a: a.reshape((2, half) + a.shape[1:])
    dup, dcw, dcb = pl.pallas_call(
        body, name=name, grid=(half, nb),
        in_specs=[pair(s, True), pl.BlockSpec((None, s, c_w), lambda k, b: (k, b, 0)), pair(3, False), pair(1, False)],
        out_specs=[pair(s, True), pair(3, False), pair(1, False)],
        out_shape=[jax.ShapeDtypeStruct((2, half, t, c_w), BF16), jax.ShapeDtypeStruct((2, half, 3, c_w), F32),
                   jax.ShapeDtypeStruct((2, half, 1, c_w), F32)],
        compiler_params=_cp(("parallel", "arbitrary")))(four(up8), dact4, four(cw8), four(cb8))
    return dup.reshape(N_DEV, t, c_w), dcw.reshape(N_DEV, 3, c_w), dcb.reshape(N_DEV, 1, c_w)


_GELU_C0 = math.sqrt(2.0 / math.pi)
_GELU_C1 = 0.044715


def _rowwise(name, body, ins, in_kinds, out_kinds, t, d, tr_pref=512):
    tr = _tile(t, tr_pref)
    row = pl.BlockSpec((tr, d), lambda i: (i, 0))
    vec = pl.BlockSpec((1, d), lambda i: (0, 0))
    in_specs = [row if k == "row" else vec for k in in_kinds]
    out_specs = [row if k[0] == "row" else vec for k in out_kinds]
    out_shape = [jax.ShapeDtypeStruct((t, d) if k[0] == "row" else (1, d), k[1]) for k in out_kinds]
    has_acc = any(k[0] == "acc" for k in out_kinds)
    return pl.pallas_call(body, name=name, grid=(t // tr,), in_specs=in_specs, out_specs=out_specs,
                          out_shape=out_shape,
                          compiler_params=_cp(("arbitrary",) if has_acc else ("parallel",)))(*ins)


def _ssm_post_fwd(ys, u, dskip):
    t, d = ys.shape

    def body(ys_ref, u_ref, ds_ref, y_ref, z_ref):
        y = ys_ref[...].astype(F32) + ds_ref[...] * u_ref[...].astype(F32)
        y_ref[...] = y
        th = jnp.tanh(_GELU_C0 * (y + _GELU_C1 * y * y * y))
        z_ref[...] = (0.5 * y * (1.0 + th)).astype(z_ref.dtype)

    return _rowwise("ssm_post_fwd", body, [ys, u, dskip], ["row", "row", "vec"],
                    [("row", F32), ("row", BF16)], t, d)


def _glu_fwd(z, gl, bglu):
    t, d = z.shape

    def body(z_ref, gl_ref, b_ref, o_ref):
        o_ref[...] = (z_ref[...].astype(F32) * _sigmoid(gl_ref[...] + b_ref[...])).astype(o_ref.dtype)

    return _rowwise("glu_fwd", body, [z, gl, bglu], ["row", "row", "vec"], [("row", BF16)], t, d)[0]


def _glu_bwd(dgg, z, gl, bglu):
    t, d = z.shape

    def body(dg_ref, z_ref, gl_ref, b_ref, dgl_ref, dz_ref, db_ref):
        sg = _sigmoid(gl_ref[...] + b_ref[...])
        dg = dg_ref[...]
        dgl = dg * z_ref[...].astype(F32) * sg * (1.0 - sg)
        dgl_ref[...] = dgl.astype(dgl_ref.dtype)
        dz_ref[...] = dg * sg

        @pl.when(pl.program_id(0) == 0)
        def _():
            db_ref[...] = jnp.zeros_like(db_ref)

        db_ref[...] += jnp.sum(dgl, axis=0, keepdims=True)

    return _rowwise("glu_bwd", body, [dgg, z, gl, bglu], ["row", "row", "row", "vec"],
                    [("row", BF16), ("row", F32), ("acc", F32)], t, d)


def _ssm_post_bwd(dz1, dz2, y, u, dskip):
    t, d = y.shape

    def body(a_ref, b_ref, y_ref, u_ref, ds_ref, dy_ref, du_ref, dd_ref):
        yv = y_ref[...]
        inner = _GELU_C0 * (yv + _GELU_C1 * yv * yv * yv)
        th = jnp.tanh(inner)
        dgelu = 0.5 * (1.0 + th) + 0.5 * yv * (1.0 - th * th) * _GELU_C0 * (1.0 + 3.0 * _GELU_C1 * yv * yv)
        dy = (a_ref[...] + b_ref[...]) * dgelu
        dy_ref[...] = dy.astype(dy_ref.dtype)
        du_ref[...] = dy * ds_ref[...]

        @pl.when(pl.program_id(0) == 0)
        def _():
            dd_ref[...] = jnp.zeros_like(dd_ref)

        dd_ref[...] += jnp.sum(dy * u_ref[...].astype(F32), axis=0, keepdims=True)

    return _rowwise("ssm_post_bwd", body, [dz1, dz2, y, u, dskip], ["row", "row", "row", "row", "vec"],
                    [("row", BF16), ("row", F32), ("acc", F32)], t, d)


def _add_cast(a, b):
    t, d = a.shape

    def body(a_ref, b_ref, o_ref):
        o_ref[...] = (a_ref[...].astype(F32) + b_ref[...].astype(F32)).astype(o_ref.dtype)

    return _rowwise("add_cast", body, [a, b], ["row", "row"], [("row", BF16)], t, d)[0]


def _ssm_scan(e_re, e_im, lam_re, lam_im, nb):
    r, n = e_re.shape
    nc = r // nb
    cb = _tile(n, 512)

    def body(er_ref, ei_ref, lr_ref, li_ref, xr_ref, xi_ref):
        lr, li = lr_ref[...], li_ref[...]
        rid = lax.broadcasted_iota(jnp.int32, (8, 1), 0)

        def tile(i, carry):
            out = []
            for b in range(nb):
                xr, xi = carry[2 * b:2 * b + 2]
                r0 = pl.multiple_of(b * nc + i * 8, 8)
                er, ei = er_ref[pl.ds(r0, 8), :], ei_ref[pl.ds(r0, 8), :]
                outr, outi = jnp.zeros((8, cb), F32), jnp.zeros((8, cb), F32)
                for j in range(8):
                    outr = jnp.where(rid == j, xr, outr)
                    outi = jnp.where(rid == j, xi, outi)
                    xr, xi = lr * xr - li * xi + er[j:j + 1, :], li * xr + lr * xi + ei[j:j + 1, :]
                xr_ref[pl.ds(r0, 8), :] = outr
                xi_ref[pl.ds(r0, 8), :] = outi
                out += [xr, xi]
            return tuple(out)

        lax.fori_loop(0, nc // 8, tile, (jnp.zeros((1, cb), F32),) * (2 * nb))

    mat = pl.BlockSpec((r, cb), lambda j: (0, j))
    vec = pl.BlockSpec((1, cb), lambda j: (0, j))
    sds = jax.ShapeDtypeStruct((r, n), F32)
    return pl.pallas_call(body, name="ssm_scan", grid=(n // cb,), in_specs=[mat, mat, vec, vec],
                          out_specs=[mat, mat], out_shape=[sds, sds],
                          compiler_params=_cp(("parallel",)))(e_re, e_im, lam_re, lam_im)


def _ssm_scan_bwd(dxp_re, dxp_im, lam_re, lam_im, nb):
    r, n = dxp_re.shape
    nc = r // nb
    cb = _tile(n, 512)

    def body(dr_ref, di_ref, lr_ref, li_ref, er_ref, ei_ref):
        lr, li = lr_ref[...], li_ref[...]
        rid = lax.broadcasted_iota(jnp.int32, (8, 1), 0)

        def tile(i, carry):
            out = []
            for b in range(nb):
                gr, gi = carry[2 * b:2 * b + 2]
                r0 = pl.multiple_of(b * nc + (nc // 8 - 1 - i) * 8, 8)
                dr, di = dr_ref[pl.ds(r0, 8), :], di_ref[pl.ds(r0, 8), :]
                outr, outi = jnp.zeros((8, cb), F32), jnp.zeros((8, cb), F32)
                for j in range(7, -1, -1):
                    outr = jnp.where(rid == j, gr, outr)
                    outi = jnp.where(rid == j, gi, outi)
                    gr, gi = dr[j:j + 1, :] + lr * gr + li * gi, di[j:j + 1, :] + lr * gi - li * gr
                er_ref[pl.ds(r0, 8), :] = outr
                ei_ref[pl.ds(r0, 8), :] = outi
                out += [gr, gi]
            return tuple(out)

        lax.fori_loop(0, nc // 8, tile, (jnp.zeros((1, cb), F32),) * (2 * nb))

    mat = pl.BlockSpec((r, cb), lambda j: (0, j))
    vec = pl.BlockSpec((1, cb), lambda j: (0, j))
    sds = jax.ShapeDtypeStruct((r, n), F32)
    return pl.pallas_call(body, name="ssm_scan_bwd", grid=(n // cb,), in_specs=[mat, mat, vec, vec],
                          out_specs=[mat, mat], out_shape=[sds, sds],
                          compiler_params=_cp(("parallel",)))(dxp_re, dxp_im, lam_re, lam_im)


def _ssm_compact(a_re, a_im, log_dt, b_re, b_im, c_re, c_im):
    g, p = a_re.shape
    h = b_re.shape[-1]
    ln = SSM_L
    sg = LANE // h
    na = g // sg
    hp = lax.Precision.HIGHEST
    lam = lax.complex(a_re, a_im)
    ldt = lam * jnp.exp(log_dt)[:, None]
    lam_bar = jnp.exp(ldt)
    bbar = ((lam_bar - 1.0) / lam)[..., None] * lax.complex(b_re, b_im)
    cm = lax.complex(c_re, c_im)
    steps = jnp.arange(ln + 1, dtype=F32)
    pw = jnp.exp(ldt[:, None, :] * steps[None, :, None])
    kd = jnp.einsum("ghp,gdp,gpk->gdhk", cm, pw[:, :ln], bbar, precision=hp).real

    def stacked(x, rows_per, cols_per):
        x = x.reshape(na, sg, ln, rows_per, cols_per).transpose(0, 2, 1, 3, 4).reshape(na, ln, sg * rows_per, cols_per)
        return jnp.pad(x, ((0, 0), (0, 0), (0, 0), (0, LANE - cols_per)))

    wxc = (pw[:, ln - 1 - jnp.arange(ln)][:, :, :, None] * bbar[:, None]).transpose(0, 1, 3, 2)
    cpc = (cm[:, None] * pw[:, 1:ln + 1][:, :, None, :]).transpose(0, 1, 3, 2)
    lam_l = pw[:, ln]
    return (stacked(kd.transpose(0, 1, 3, 2), h, h), stacked(wxc.real, h, p), stacked(wxc.imag, h, p),
            stacked(cpc.real, p, h), stacked(-cpc.imag, p, h),
            lam_l.real.reshape(1, g * p), lam_l.imag.reshape(1, g * p))


def _ssm_masks(h, p):
    sg = LANE // h
    r128 = lax.broadcasted_iota(jnp.int32, (LANE, LANE), 0)
    c128 = lax.broadcasted_iota(jnp.int32, (LANE, LANE), 1)
    rx = lax.broadcasted_iota(jnp.int32, (LANE, sg * p), 0)
    cx = lax.broadcasted_iota(jnp.int32, (LANE, sg * p), 1)
    ry = lax.broadcasted_iota(jnp.int32, (sg * p, LANE), 0)
    cy = lax.broadcasted_iota(jnp.int32, (sg * p, LANE), 1)
    f = lambda m: m.astype(F32)
    return dict(
        spread_h=f((r128 < h) & (c128 % h == r128)),
        spread_p=f((rx < p) & (cx % p == rx)),
        gather_h=f((c128 < h) & (r128 % h == c128)),
        gather_p=f((cy < p) & (ry % p == cy)),
        same_t=f(r128 // h == c128 // h), same_x=f(rx // h == cx // p), same_y=f(ry // p == cy // h))


def _place(a, spread):
    return jnp.dot(a.astype(BF16), spread.astype(BF16), preferred_element_type=F32)


def _ssm_expand(compact, ex):
    kt, wxr, wxi, wyr, wyi = compact
    na, ln = kt.shape[:2]
    wst = wyr.shape[2]
    h, p = SSM_H, SSM_P
    grid = (na,)

    def body(*refs):
        kt_ref, wxr_ref, wxi_ref, wyr_ref, wyi_ref = refs[:5]
        ex_ins = refs[5:5 + ex.n]
        tm_ref, xr_ref, xi_ref, yr_ref, yi_ref = refs[5 + ex.n:10 + ex.n]
        ex_outs = refs[10 + ex.n:10 + 2 * ex.n]
        sems = refs[10 + 2 * ex.n:]
        first, last = _grid_ends(grid)

        @pl.when(first)
        def _():
            ex.start(ex_ins, ex_outs, sems)

        m = _ssm_masks(h, p)
        ktb = [_place(kt_ref[lag], m["spread_h"]) * m["same_t"] for lag in range(ln)]
        zero = jnp.zeros((LANE, LANE), F32)
        for sig in range(ln):
            rows = slice(sig * LANE, (sig + 1) * LANE)
            tm_ref[rows, :] = jnp.concatenate([ktb[tau - sig] if tau >= sig else zero for tau in range(ln)],
                                              axis=1).astype(tm_ref.dtype)
            xr_ref[rows, :] = (_place(wxr_ref[sig], m["spread_p"]) * m["same_x"]).astype(xr_ref.dtype)
            xi_ref[rows, :] = (_place(wxi_ref[sig], m["spread_p"]) * m["same_x"]).astype(xi_ref.dtype)
        for tau in range(ln):
            cols = slice(tau * LANE, (tau + 1) * LANE)
            yr_ref[:, cols] = (_place(wyr_ref[tau], m["spread_h"]) * m["same_y"]).astype(yr_ref.dtype)
            yi_ref[:, cols] = (_place(wyi_ref[tau], m["spread_h"]) * m["same_y"]).astype(yi_ref.dtype)

        @pl.when(last)
        def _():
            ex.wait(ex_ins, ex_outs, sems)

    blk = lambda rows: pl.BlockSpec((None, ln, rows, LANE), lambda j: (j, 0, 0, 0))
    mat = lambda rows, cols: pl.BlockSpec((None, rows, cols), lambda j: (j, 0, 0))
    sds = lambda rows, cols: jax.ShapeDtypeStruct((na, rows, cols), BF16)
    wch = ln * LANE
    res = pl.pallas_call(
        body, name="ssm_expand", grid=grid,
        in_specs=[blk(LANE), blk(LANE), blk(LANE), blk(wst), blk(wst)] + ex.specs,
        out_specs=[mat(wch, wch), mat(wch, wst), mat(wch, wst), mat(wst, wch), mat(wst, wch)] + ex.specs,
        out_shape=[sds(wch, wch), sds(wch, wst), sds(wch, wst), sds(wst, wch), sds(wst, wch)] + ex.out_shape,
        scratch_shapes=ex.scratch,
        compiler_params=_cp(("arbitrary",)))(kt, wxr, wxi, wyr, wyi, *ex.arrs)
    return list(res[:5]), list(res[5:])


def _chunk_view(a):
    t, d = a.shape
    return a.reshape(t // SSM_L, SSM_L * d)


def _sg_specs(r4, d, wst):
    nblk = d // LANE
    cat = [pl.BlockSpec((r4, LANE), functools.partial(lambda j, tau: (0, tau * nblk + j), tau=tau))
           for tau in range(SSM_L)]
    plane = pl.BlockSpec((r4, wst), lambda j: (0, j))
    mat = lambda rows, cols: pl.BlockSpec((None, rows, cols), lambda j: (j, 0, 0))
    piece = pl.BlockSpec((r4, LANE), lambda j: (0, j))
    return cat, plane, mat, piece


def _lane_cat(refs):
    return jnp.concatenate([r[...] for r in refs], axis=1)


def _bdot(a, b, dims):
    return lax.dot_general(a.astype(BF16), b.astype(BF16), dims, preferred_element_type=F32)


def _ssm_core_fwd(u, ops, nb):
    tm, wxr, wxi, wyr, wyi, lam_re, lam_im = ops
    t, d = u.shape
    ln, na, wch, wst = SSM_L, tm.shape[0], tm.shape[1], wxr.shape[2]
    r4 = t // ln
    n = na * wst
    u4 = _chunk_view(u)
    cat, plane, mat, piece = _sg_specs(r4, d, wst)
    pds = jax.ShapeDtypeStruct((r4, n), F32)

    def states(*refs):
        ucat = _lane_cat(refs[:ln])
        wr_ref, wi_ref, er_ref, ei_ref = refs[ln:]
        er_ref[...] = _bdot(ucat, wr_ref[...], _NN)
        ei_ref[...] = _bdot(ucat, wi_ref[...], _NN)

    e_re, e_im = pl.pallas_call(
        states, name="ssm_states", grid=(na,), in_specs=cat + [mat(wch, wst)] * 2, out_specs=[plane, plane],
        out_shape=[pds, pds], compiler_params=_cp(("parallel",)))(*([u4] * ln), wxr, wxi)
    xp_re, xp_im = _ssm_scan(e_re, e_im, lam_re, lam_im, nb)

    def outputs(*refs):
        ucat = _lane_cat(refs[:ln])
        tm_ref, xr_ref, xi_ref, wr_ref, wi_ref = refs[ln:ln + 5]
        y = (_bdot(ucat, tm_ref[...], _NN) + _bdot(xr_ref[...], wr_ref[...], _NN)
             + _bdot(xi_ref[...], wi_ref[...], _NN))
        for tau, o_ref in enumerate(refs[ln + 5:]):
            o_ref[...] = y[:, tau * LANE:(tau + 1) * LANE].astype(o_ref.dtype)

    ys = pl.pallas_call(
        outputs, name="ssm_y", grid=(na,),
        in_specs=cat + [mat(wch, wch), plane, plane, mat(wst, wch), mat(wst, wch)], out_specs=[piece] * ln,
        out_shape=[jax.ShapeDtypeStruct((r4, d), BF16)] * ln,
        compiler_params=_cp(("parallel",)))(*([u4] * ln), tm, xp_re, xp_im, wyr, wyi)
    return jnp.concatenate(ys, axis=1).reshape(t, d), xp_re, xp_im


def _ssm_core_bwd(dy, u, xp_re, xp_im, ops, nb):
    tm, wxr, wxi, wyr, wyi, lam_re, lam_im = ops
    t, d = u.shape
    ln, na, wch, wst = SSM_L, tm.shape[0], tm.shape[1], wxr.shape[2]
    r4 = t // ln
    n = na * wst
    u4, dy4 = _chunk_view(u), _chunk_view(dy)
    cat, plane, mat, piece = _sg_specs(r4, d, wst)
    pds = jax.ShapeDtypeStruct((r4, n), F32)

    def dstates(*refs):
        dycat = _lane_cat(refs[:ln])
        wr_ref, wi_ref, dr_ref, di_ref = refs[ln:]
        dr_ref[...] = _bdot(dycat, wr_ref[...], _NT)
        di_ref[...] = _bdot(dycat, wi_ref[...], _NT)

    dxp_re, dxp_im = pl.pallas_call(
        dstates, name="ssm_dxp", grid=(na,), in_specs=cat + [mat(wst, wch)] * 2, out_specs=[plane, plane],
        out_shape=[pds, pds], compiler_params=_cp(("parallel",)))(*([dy4] * ln), wyr, wyi)
    de_re, de_im = _ssm_scan_bwd(dxp_re, dxp_im, lam_re, lam_im, nb)

    def dinputs(*refs):
        dycat = _lane_cat(refs[:ln])
        tm_ref, er_ref, ei_ref, wr_ref, wi_ref = refs[ln:ln + 5]
        du = (_bdot(dycat, tm_ref[...], _NT) + _bdot(er_ref[...], wr_ref[...], _NT)
              + _bdot(ei_ref[...], wi_ref[...], _NT))
        for tau, o_ref in enumerate(refs[ln + 5:]):
            o_ref[...] = du[:, tau * LANE:(tau + 1) * LANE].astype(o_ref.dtype)

    dus = pl.pallas_call(
        dinputs, name="ssm_du", grid=(na,),
        in_specs=cat + [mat(wch, wch), plane, plane, mat(wch, wst), mat(wch, wst)], out_specs=[piece] * ln,
        out_shape=[jax.ShapeDtypeStruct((r4, d), BF16)] * ln,
        compiler_params=_cp(("parallel",)))(*([dy4] * ln), tm, de_re, de_im, wxr, wxi)

    def doperators(*refs):
        ucat, dycat = _lane_cat(refs[:ln]), _lane_cat(refs[ln:2 * ln])
        (er_ref, ei_ref, xr_ref, xi_ref, dtm_ref, dwxr_ref, dwxi_ref, dwyr_ref, dwyi_ref,
         dlr_ref, dli_ref) = refs[2 * ln:]
        er, ei, xr, xi = er_ref[...], ei_ref[...], xr_ref[...], xi_ref[...]
        m = _ssm_masks(SSM_H, SSM_P)
        gather_h, gather_p = m["gather_h"].astype(BF16), m["gather_p"].astype(BF16)
        blk = lambda i: slice(i * LANE, (i + 1) * LANE)
        dtm = _bdot(ucat, dycat, _TN)
        for lag in range(ln):
            acc = dtm[blk(0), blk(lag)]
            for sig in range(1, ln - lag):
                acc = acc + dtm[blk(sig), blk(sig + lag)]
            dtm_ref[lag] = _split_dot(acc * m["same_t"], gather_h)
        for src, dst in ((er, dwxr_ref), (ei, dwxi_ref)):
            dwx = _bdot(ucat, src, _TN)
            for sig in range(ln):
                dst[sig] = _split_dot(dwx[blk(sig), :] * m["same_x"], gather_p)
        for src, dst in ((xr, dwyr_ref), (xi, dwyi_ref)):
            dwy = _bdot(src, dycat, _TN)
            for tau in range(ln):
                dst[tau] = _split_dot(dwy[:, blk(tau)] * m["same_y"], gather_h)
        dlr_ref[...] = jnp.sum(er * xr + ei * xi, axis=0, keepdims=True)
        dli_ref[...] = jnp.sum(ei * xr - er * xi, axis=0, keepdims=True)

    cblk = lambda rows: pl.BlockSpec((None, ln, rows, LANE), lambda j: (j, 0, 0, 0))
    cds = lambda rows: jax.ShapeDtypeStruct((na, ln, rows, LANE), F32)
    vec = pl.BlockSpec((1, wst), lambda j: (0, j))
    vds = jax.ShapeDtypeStruct((1, n), F32)
    d_compact = pl.pallas_call(
        doperators, name="ssm_dops", grid=(na,), in_specs=cat + cat + [plane] * 4,
        out_specs=[cblk(LANE), cblk(LANE), cblk(LANE), cblk(wst), cblk(wst), vec, vec],
        out_shape=[cds(LANE), cds(LANE), cds(LANE), cds(wst), cds(wst), vds, vds],
        compiler_params=_cp(("parallel",)))(*([u4] * ln), *([dy4] * ln), de_re, de_im, xp_re, xp_im)
    return jnp.concatenate(dus, axis=1).reshape(t, d), tuple(d_compact)


def _modfin_fwd(c_all, w_mod, w_fin):
    n, d = c_all.shape
    nl, _, cm = w_mod.shape
    cf = w_fin.shape[1]
    width = nl * cm + cf
    hp = lax.Precision.HIGHEST

    def body(c_ref, wm_ref, wf_ref, act_ref, out_ref):
        cv = c_ref[...]
        act = cv * _sigmoid(cv)
        act_ref[...] = act
        for i in range(nl):
            out_ref[:, i * cm:(i + 1) * cm] = jnp.dot(act, wm_ref[i], preferred_element_type=F32, precision=hp)
        out_ref[:, nl * cm:] = jnp.dot(act, wf_ref[...], preferred_element_type=F32, precision=hp)

    return pl.pallas_call(body, name="modfin_fwd",
                          out_shape=[jax.ShapeDtypeStruct((n, d), F32), jax.ShapeDtypeStruct((n, width), F32)],
                          compiler_params=_cp(None))(c_all, w_mod, w_fin)


def _modfin_bwd(c_act_t, dmod_loc, dfin_loc, dall):
    d, n = c_act_t.shape
    nl, _, cm = dmod_loc.shape
    cf = dfin_loc.shape[1]
    hp = lax.Precision.HIGHEST

    def body(ct_ref, dm_ref, df_ref, da_ref, gwm_ref, gwf_ref, gb_ref):
        ct = ct_ref[...]
        for i in range(nl):
            gwm_ref[i] = jnp.dot(ct, dm_ref[i], preferred_element_type=F32, precision=hp)
        gwf_ref[...] = jnp.dot(ct, df_ref[...], preferred_element_type=F32, precision=hp)
        gb_ref[...] = jnp.sum(da_ref[...], axis=0, keepdims=True)

    return pl.pallas_call(body, name="modfin_bwd",
                          out_shape=[jax.ShapeDtypeStruct((nl, d, cm), F32), jax.ShapeDtypeStruct((d, cf), F32),
                                     jax.ShapeDtypeStruct((1, dall.shape[1]), F32)],
                          compiler_params=_cp(None))(c_act_t, dmod_loc, dfin_loc, dall)


def _adamw(name, gparts, w, m, v):
    n, r, c = gparts.shape
    tr = _tile(r, 256)

    def body(gp_ref, w_ref, m_ref, v_ref, g_ref, d_ref, mo_ref, vo_ref):
        _adamw_step(gp_ref, w_ref, m_ref, v_ref, g_ref, d_ref, mo_ref, vo_ref)

    mat = pl.BlockSpec((tr, c), lambda i: (i, 0))
    sds = jax.ShapeDtypeStruct((r, c), F32)
    return pl.pallas_call(body, name=name, grid=(r // tr,),
                          in_specs=[pl.BlockSpec((n, tr, c), lambda i: (0, i, 0)), mat, mat, mat],
                          out_specs=[mat] * 4, out_shape=[sds] * 4,
                          compiler_params=_cp(("parallel",)))(gparts, w, m, v)


def _adamw_layers(name, gparts_l, w, m, v, ex=None):
    nl, r, c = w.shape
    n = gparts_l[0].shape[0]
    tr = _tile(r, 256)
    nt = r // tr
    grid = (nl, nt)
    ex = ex or _NoExchange()

    def body(*refs):
        w_ref, m_ref, v_ref = refs[nl:nl + 3]
        ex_ins = refs[nl + 3:nl + 3 + ex.n]
        outs = refs[nl + 3 + ex.n:nl + 7 + ex.n]
        ex_outs = refs[nl + 7 + ex.n:nl + 7 + 2 * ex.n]
        sems = refs[nl + 7 + 2 * ex.n:]
        first, last = _grid_ends(grid)

        @pl.when(first)
        def _():
            ex.start(ex_ins, ex_outs, sems)

        layer = pl.program_id(0)
        for i in range(nl):
            @pl.when(layer == i)
            def _(i=i):
                _adamw_step(refs[i], w_ref, m_ref, v_ref, *outs)

        @pl.when(last)
        def _():
            ex.wait(ex_ins, ex_outs, sems)

    def parts(i):
        return pl.BlockSpec((n, tr, c), lambda l, t: (0, jnp.where(l == i, t, jnp.where(l < i, 0, nt - 1)), 0))

    mat = pl.BlockSpec((None, tr, c), lambda l, t: (l, t, 0))
    sds = jax.ShapeDtypeStruct((nl, r, c), F32)
    res = pl.pallas_call(body, name=name, grid=grid,
                         in_specs=[parts(i) for i in range(nl)] + [mat] * 3 + ex.specs,
                         out_specs=[mat] * 4 + ex.specs, out_shape=[sds] * 4 + ex.out_shape,
                         scratch_shapes=ex.scratch,
                         compiler_params=_cp(("arbitrary", "arbitrary")))(*gparts_l, w, m, v, *ex.arrs)
    return (list(res[:4]), list(res[4:])) if ex.n else list(res)


def _adamw_step(gp_ref, w_ref, m_ref, v_ref, g_ref, d_ref, mo_ref, vo_ref):
    gsum = gp_ref[0].astype(F32)
    for j in range(1, gp_ref.shape[0]):
        gsum = gsum + gp_ref[j].astype(F32)
    mn = ADAM_B1 * m_ref[...] + (1.0 - ADAM_B1) * gsum
    vn = ADAM_B2 * v_ref[...] + (1.0 - ADAM_B2) * (gsum * gsum)
    g_ref[...] = gsum
    mo_ref[...] = mn
    vo_ref[...] = vn
    m_hat = mn * (1.0 / (1.0 - ADAM_B1 ** ADAM_STEP))
    v_hat = vn * (1.0 / (1.0 - ADAM_B2 ** ADAM_STEP))
    d_ref[...] = -ADAM_LR * (m_hat / (jnp.sqrt(v_hat) + ADAM_EPS) + ADAM_WD * w_ref[...])


def _adamw_many(name, entries):
    k = len(entries)

    def body(*refs):
        for i in range(k):
            _adamw_step(*refs[4 * i:4 * i + 4], *refs[4 * k + 4 * i:4 * k + 4 * i + 4])

    ops = [a for e in entries for a in e]
    out_shape = [jax.ShapeDtypeStruct(e[1].shape, F32) for e in entries for _ in range(4)]
    return pl.pallas_call(body, name=name, out_shape=out_shape, compiler_params=_cp(None))(*ops)


class _Exchange:
    def __init__(self, arrs, gathers):
        self.arrs = [pltpu.with_memory_space_constraint(a, pltpu.HBM) for a in arrs]
        self.gathers = list(gathers)
        self.n = len(arrs)
        self.out_shape = [pltpu.HBM(((N_DEV,) + a.shape) if g else a.shape, a.dtype)
                          for a, g in zip(arrs, self.gathers)]
        self.specs = [pl.BlockSpec(memory_space=pltpu.HBM)] * self.n
        self.scratch = [pltpu.SemaphoreType.DMA((self.n, N_DEV - 1)), pltpu.SemaphoreType.DMA((self.n, N_DEV - 1)),
                        pltpu.SemaphoreType.DMA((self.n,))]

    def _copies(self, ins, outs, sems):
        send_sems, recv_sems, local_sems = sems
        x, y, c = lax.axis_index("x"), lax.axis_index("y"), lax.axis_index("c")
        me = 4 * x + 2 * y + c
        local, sends, recvs = [], [], []
        for i in range(self.n):
            src_me = ins[i] if self.gathers[i] else ins[i].at[me]
            local.append(pltpu.make_async_copy(src_me, outs[i].at[me], local_sems.at[i]))
        for dd in range(1, N_DEV):
            px = jnp.bitwise_xor(x, dd >> 2)
            py = jnp.bitwise_xor(y, (dd >> 1) & 1)
            pc = jnp.bitwise_xor(c, dd & 1)
            pid = 4 * px + 2 * py + pc
            for i in range(self.n):
                src = ins[i] if self.gathers[i] else ins[i].at[pid]
                sems_i = dict(send_sem=send_sems.at[i, dd - 1], recv_sem=recv_sems.at[i, dd - 1],
                              device_id=(px, py, pc), device_id_type=MESH)
                sends.append(pltpu.make_async_remote_copy(src_ref=src, dst_ref=outs[i].at[me], **sems_i))
                recvs.append(pltpu.make_async_remote_copy(src_ref=src, dst_ref=outs[i].at[pid], **sems_i))
        return local, sends, recvs

    def start(self, ins, outs, sems):
        local, sends, _ = self._copies(ins, outs, sems)
        for cp in local + sends:
            cp.start()

    def wait(self, ins, outs, sems):
        local, sends, recvs = self._copies(ins, outs, sems)
        for cp in recvs:
            cp.wait_recv()
        for cp in sends:
            cp.wait_send()
        for cp in local:
            cp.wait()


class _NoExchange:
    n, arrs, specs, out_shape, scratch = 0, [], [], [], []

    def start(self, ins, outs, sems):
        pass

    def wait(self, ins, outs, sems):
        pass


def _exchange(name, arrs, gathers):
    ex = _Exchange(arrs, gathers)
    n = ex.n

    def body(*refs):
        ins, outs, sems = refs[:n], refs[n:2 * n], refs[2 * n:]
        ex.start(ins, outs, sems)
        ex.wait(ins, outs, sems)

    outs = pl.pallas_call(body, name=name, in_specs=ex.specs, out_specs=ex.specs, out_shape=ex.out_shape,
                          scratch_shapes=ex.scratch)(*ex.arrs)
    return list(outs)


def kernel(x, c, norm_mix, norm_ffn, w_mod, b_mod, w_qkv, w_o_attn, w_in_ssm, a_re, a_im, log_dt, b_re, b_im, c_re, c_im, d_skip, w_glu, b_glu, w_o_ssm, w_up, conv_w, conv_b, w_down, norm_out, w_fin, b_fin, loss_target, m_norm_mix, m_norm_ffn, m_w_mod, m_b_mod, m_w_qkv, m_w_o_attn, m_w_in_ssm, m_a_re, m_a_im, m_log_dt, m_b_re, m_b_im, m_c_re, m_c_im, m_d_skip, m_w_glu, m_b_glu, m_w_o_ssm, m_w_up, m_conv_w, m_conv_b, m_w_down, m_norm_out, m_w_fin, m_b_fin, v_norm_mix, v_norm_ffn, v_w_mod, v_b_mod, v_w_qkv, v_w_o_attn, v_w_in_ssm, v_a_re, v_a_im, v_log_dt, v_b_re, v_b_im, v_c_re, v_c_im, v_d_skip, v_w_glu, v_b_glu, v_w_o_ssm, v_w_up, v_conv_w, v_conv_b, v_w_down, v_norm_out, v_w_fin, v_b_fin):
    nb, s, d = x.shape
    t = nb * s
    n_seq = nb * N_DEV
    me = 4 * lax.axis_index("x") + 2 * lax.axis_index("y") + lax.axis_index("c")
    cm = w_mod.shape[2]
    cf = w_fin.shape[1]
    c_up = w_up.shape[2]
    r_dn = w_down.shape[1]
    g_ssm = d // SSM_H

    (c8,) = _exchange("gather_first", [c], [True])
    ssm_params = (a_re[0], a_im[0], log_dt[0], b_re[0], b_im[0], c_re[0], c_im[0])
    compact, ops_vjp = jax.vjp(_ssm_compact, *ssm_params)
    ssm_mats, (wq8,) = _ssm_expand(compact[:5], _Exchange([w_qkv[0].astype(BF16)], [True]))
    ops = (*ssm_mats, compact[5], compact[6])
    later = _Exchange([w_o_attn[0].astype(BF16), w_in_ssm[0].astype(BF16), w_glu[0].astype(BF16),
                       w_o_ssm[0].astype(BF16), w_up[0].astype(BF16), w_up[1].astype(BF16),
                       w_down[0].astype(BF16), w_down[1].astype(BF16), conv_w, d_skip, b_glu], [True] * 11)
    half = N_DEV // 2
    cb_l = [conv_b[i].reshape(N_DEV, 1, c_up) for i in range(2)]
    c_all = c8.reshape(n_seq, d)

    c_act, modloc = _modfin_fwd(c_all, w_mod, w_fin)
    (mod8,) = _exchange("gather_mod", [modloc], [True])
    mine = lax.dynamic_slice_in_dim(mod8, me * nb, nb, axis=1)
    mods = []
    for i in range(2):
        mi = mine[:, :, i * cm:(i + 1) * cm].transpose(1, 0, 2).reshape(nb, N_DEV * cm) + b_mod[i]
        mods.append([mi[:, j * d:(j + 1) * d].reshape(nb, 1, d) for j in range(6)])
    fin = mine[:, :, 2 * cm:].transpose(1, 0, 2).reshape(nb, N_DEV * cf) + b_fin
    sh_f, sc_f = fin[:, :d].reshape(nb, 1, d), fin[:, d:].reshape(nb, 1, d)

    row = lambda a: a.reshape(1, -1)
    x0 = x.reshape(t, d)

    def ffn_fwd(i, h):
        up = _mm(f"ffn{i}_up", h, wup8[i], (t // tm_, N_DEV, 1),
                 pl.BlockSpec((tm_, d), lambda a, b, k: (a, 0)), pl.BlockSpec((None, d, c_up), lambda a, b, k: (b, 0, 0)),
                 pl.BlockSpec((None, tm_, c_up), lambda a, b, k: (b, a, 0)),
                 jax.ShapeDtypeStruct((N_DEV, t, c_up), BF16), _NN, (tm_, c_up))
        act = _ffn_act_fwd(f"ffn{i}_act", up, cw_l[i], cb_l[i], nb, s)
        yf = _mm(f"ffn{i}_down", act, wd4[i], (t // tm_, 1, half),
                 pl.BlockSpec((None, tm_, c_up), lambda a, b, k: (k, a, 0)),
                 pl.BlockSpec((None, c_up, d), lambda a, b, k: (k, 0, 0)),
                 pl.BlockSpec((tm_, d), lambda a, b, k: (a, 0)), jax.ShapeDtypeStruct((t, d), F32), _NN, (tm_, d))
        return yf, (h, up, act)

    tm_ = _tile(t, 2048)
    sh1, sc1, g1, sh2, sc2, g2 = mods[0]
    h1 = _norm_mod_fwd("attn_norm", x0, row(norm_mix[0]), sh1, sc1, nb)
    cq = wq8.shape[2]
    qkv = _mm("attn_qkv", h1, wq8, (t // tm_, N_DEV, 1),
              pl.BlockSpec((tm_, d), lambda a, b, k: (a, 0)), pl.BlockSpec((None, d, cq), lambda a, b, k: (b, 0, 0)),
              pl.BlockSpec((tm_, cq), lambda a, b, k: (a, b)), jax.ShapeDtypeStruct((t, 3 * d), BF16), _NN, (tm_, cq))
    o_att, car_att, (wo8, win8, wglu8, wos8, wup8_0, wup8_1, wd8_0, wd8_1, cw8, dskip8, bglu8) = _attn_fwd(
        qkv, nb, s, d, later)
    wo = wo8.reshape(d, d)
    win = win8.reshape(d, d)
    wglu = wglu8.reshape(d, d)
    wos = wos8.reshape(d, d)
    wup8 = [wup8_0, wup8_1]
    wd4 = [wd8_0.reshape(half, 2 * r_dn, d), wd8_1.reshape(half, 2 * r_dn, d)]
    cw_l = [cw8[:, 0], cw8[:, 1]]
    dskip_f = dskip8.reshape(1, d)
    bglu_f = bglu8.reshape(1, d)
    ya = _mm_nn("attn_out", o_att, wo, F32)
    sh1b, sc1b, g1b, sh2b, sc2b, g2b = mods[1]
    x1, h2 = _res_norm_fwd("attn_res", x0, ya, g1, row(norm_ffn[0]), sh2, sc2, nb)
    yf0, ffn0_saved = ffn_fwd(0, h2)
    x2, h3 = _res_norm_fwd("ffn0_res", x1, yf0, g2, row(norm_mix[1]), sh1b, sc1b, nb)

    u = _mm_nn("ssm_in", h3, win, BF16)
    ys_core, xp_re, xp_im = _ssm_core_fwd(u, ops, nb)
    y_ssm, z_ssm = _ssm_post_fwd(ys_core, u, dskip_f)
    gl = _mm_nn("ssm_glu", z_ssm, wglu, F32)
    gg = _glu_fwd(z_ssm, gl, bglu_f)
    ys2 = _mm_nn("ssm_out", gg, wos, F32)
    x3, h4 = _res_norm_fwd("ssm_res", x2, ys2, g1b, row(norm_ffn[1]), sh2b, sc2b, nb)
    yf1, ffn1_saved = ffn_fwd(1, h4)
    x4 = _gate_add("ffn1_res", x3, yf1, g2b, nb)

    dx4, g_norm_out, dsh_f, dsc_f, loss_blk, dyf1, dg2b = _norm_mod_bwd(
        "final_norm", None, x4, row(norm_out), sh_f, sc_f, None, loss_target.reshape(t, d), nb, branch=(yf1, g2b))
    loss = lax.psum(loss_blk[0, 0], ("x", "y", "c"))

    def ffn_bwd(i, dyf, dxo, xin, sc, saved, branch):
        h, up, act = saved
        dact = _mm(f"ffn{i}_down_dx", dyf, wd4[i], (t // tm_, half, 1),
                   pl.BlockSpec((tm_, d), lambda a, b, k: (a, 0)), pl.BlockSpec((None, c_up, d), lambda a, b, k: (b, 0, 0)),
                   pl.BlockSpec((None, tm_, c_up), lambda a, b, k: (b, a, 0)),
                   jax.ShapeDtypeStruct((half, t, c_up), BF16), _NT, (tm_, c_up))
        tk = _tile(t, 1024)
        gwd = _mm(f"ffn{i}_down_dw", act, dyf, (half, 1, t // tk),
                  pl.BlockSpec((None, tk, c_up), lambda a, b, k: (a, k, 0)), pl.BlockSpec((tk, d), lambda a, b, k: (k, 0)),
                  pl.BlockSpec((None, c_up, d), lambda a, b, k: (a, 0, 0)),
                  jax.ShapeDtypeStruct((half, c_up, d), BF16), _TN, (c_up, d))
        dup, dcw, dcb = _ffn_act_bwd(f"ffn{i}_act_bwd", up, dact, cw_l[i], cb_l[i], nb, s)
        dh = _mm(f"ffn{i}_up_dx", dup, wup8[i], (t // tm_, 1, N_DEV),
                 pl.BlockSpec((None, tm_, c_up), lambda a, b, k: (k, a, 0)),
                 pl.BlockSpec((None, d, c_up), lambda a, b, k: (k, 0, 0)),
                 pl.BlockSpec((tm_, d), lambda a, b, k: (a, 0)), jax.ShapeDtypeStruct((t, d), F32), _NT, (tm_, d))
        gwup = _mm(f"ffn{i}_up_dw", h, dup, (1, N_DEV, t // tk),
                   pl.BlockSpec((tk, d), lambda a, b, k: (k, 0)), pl.BlockSpec((None, tk, c_up), lambda a, b, k: (b, k, 0)),
                   pl.BlockSpec((None, d, c_up), lambda a, b, k: (b, 0, 0)),
                   jax.ShapeDtypeStruct((N_DEV, d, c_up), BF16), _TN, (d, c_up))
        dxi, g_norm, dsh, dsc, dy_branch, dgate = _norm_mod_bwd(
            f"ffn{i}_norm_bwd", dh, xin, row(norm_ffn[i]), None, sc, dxo, None, nb, branch=branch)
        return dxi, (gwup, gwd.reshape(N_DEV, r_dn, d), dcw, dcb, g_norm, dsh, dsc), dy_branch, dgate

    dx3, (gwup1, gwd1, dcw1, dcb1, g_nffn1, dsh2b, dsc2b), dys2, dg1b = ffn_bwd(
        1, dyf1, dx4, x3, sc2b, ffn1_saved, (ys2, g1b))
    dgg = _mm_nt("ssm_out_dx", dys2, wos, F32)
    gwos = _mm_tn("ssm_out_dw", gg, dys2, BF16)
    dgl, dz1, g_bglu = _glu_bwd(dgg, z_ssm, gl, bglu_f)
    dz2 = _mm_nt("ssm_glu_dx", dgl, wglu, F32)
    gwglu = _mm_tn("ssm_glu_dw", z_ssm, dgl, BF16)
    dy_ssm, du_skip, g_dskip = _ssm_post_bwd(dz1, dz2, y_ssm, u, dskip_f)
    du_core, d_ops = _ssm_core_bwd(dy_ssm, u, xp_re, xp_im, ops, nb)
    du = _add_cast(du_core, du_skip)
    dh3 = _mm_nt("ssm_in_dx", du, win, F32)
    gwin = _mm_tn("ssm_in_dw", h3, du, BF16)
    dx2, g_nmix1, dsh1b, dsc1b, dyf0, dg2 = _norm_mod_bwd(
        "ssm_norm_bwd", dh3, x2, row(norm_mix[1]), None, sc1b, dx3, None, nb, branch=(yf0, g2))
    g_ssm_params = ops_vjp(d_ops)

    dx1, (gwup0, gwd0, dcw0, dcb0, g_nffn0, dsh2, dsc2), dya, dg1 = ffn_bwd(
        0, dyf0, dx2, x1, sc2, ffn0_saved, (ya, g1))
    do_att = _mm_nt("attn_out_dx", dya, wo, BF16)
    gwo = _mm_tn("attn_out_dw", o_att, dya, BF16)
    rows8 = lambda a: a.reshape(N_DEV, d // N_DEV, d)
    def two_d(w):
        shp = w.shape
        if len(shp) == 1:
            return (1, shp[0])
        if len(shp) == 2:
            return shp
        return (shp[0] * shp[1], math.prod(shp[2:]))

    ssm_w = [a_re, a_im, log_dt, b_re, b_im, c_re, c_im]
    ssm_partial = [g.reshape(two_d(w)) for g, w in zip(g_ssm_params, ssm_w)]
    early = _Exchange([rows8(gwo), rows8(gwin), rows8(gwglu), rows8(gwos), gwup0, gwup1, gwd0, gwd1] + ssm_partial,
                      [False] * 8 + [True] * 7)
    dq, dk, dv, early_res = _attn_bwd(qkv, car_att, do_att, nb, s, d, early)
    ro, rin, rglu, ros, rup0, rup1, rd0, rd1 = early_res[:8]
    ssm8 = early_res[8:]
    dqkv = jnp.concatenate([dq, dk, dv], axis=1)
    tk = _tile(t, 1024)
    gwq8 = _mm("attn_qkv_dw", h1, dqkv, (1, N_DEV, t // tk),
               pl.BlockSpec((tk, d), lambda a, b, k: (k, 0)),
               pl.BlockSpec((tk, cq), lambda a, b, k: (k, b)),
               pl.BlockSpec((None, d, cq), lambda a, b, k: (b, 0, 0)),
               jax.ShapeDtypeStruct((N_DEV, d, cq), BF16), _TN, (d, cq))
    dh1, (rq,) = _mm("attn_qkv_dx", dqkv, wq8, (t // tm_, 1, N_DEV),
                     pl.BlockSpec((tm_, cq), lambda a, b, k: (a, k)),
                     pl.BlockSpec((None, d, cq), lambda a, b, k: (k, 0, 0)),
                     pl.BlockSpec((tm_, d), lambda a, b, k: (a, 0)), jax.ShapeDtypeStruct((t, d), F32), _NT, (tm_, d),
                     ex=_Exchange([gwq8], [False]))
    dx0, g_nmix0, dsh1, dsc1 = _norm_mod_bwd("attn_norm_bwd", dh1, x0, row(norm_mix[0]), None, sc1, dx1, None, nb)
    grad_x = dx0.reshape(nb, s, d)

    dmod = [jnp.concatenate([a.reshape(nb, d) for a in grp], axis=1) for grp in
            ([dsh1, dsc1, dg1, dsh2, dsc2, dg2], [dsh1b, dsc1b, dg1b, dsh2b, dsc2b, dg2b])]
    dfin = jnp.concatenate([dsh_f.reshape(nb, d), dsc_f.reshape(nb, d)], axis=1)
    dmodfin = jnp.concatenate(dmod + [dfin], axis=1)
    flat = lambda a: a.reshape(1, -1)
    last_ex = _Exchange(
        [dmodfin, jnp.concatenate([g_nmix0, g_nmix1]), jnp.concatenate([g_nffn0, g_nffn1]), g_norm_out,
         jnp.concatenate([flat(dcb0), flat(dcb1)]), g_dskip, g_bglu, jnp.stack([dcw0, dcw1])], [True] * 8)
    upd = {}
    upd["w_up"], (dmf8, nmix8, nffn8, nout8, cb8, dskip_g8, bglu_g8, cw_g8) = _adamw_layers(
        "adamw_w_up", [rup0, rup1], w_up, m_w_up, v_w_up, ex=last_ex)
    shard = d // N_DEV
    dskip_g8 = lax.dynamic_slice_in_dim(dskip_g8, me * shard, shard, axis=2)
    bglu_g8 = lax.dynamic_slice_in_dim(bglu_g8, me * shard, shard, axis=2)
    cw_g8 = lax.dynamic_slice_in_dim(cw_g8, me, 1, axis=2).reshape(N_DEV, 2 * 3, c_up)

    dall = dmf8.reshape(n_seq, 14 * d)
    dmod_loc = jnp.stack([lax.dynamic_slice_in_dim(dall[:, i * 6 * d:(i + 1) * 6 * d], me * cm, cm, axis=1)
                          for i in range(2)])
    dfin_loc = lax.dynamic_slice_in_dim(dall[:, 12 * d:], me * cf, cf, axis=1)
    g_w_mod, g_w_fin, g_bias = _modfin_bwd(c_act.T, dmod_loc, dfin_loc, dall)
    g_b_mod = g_bias[0, :12 * d].reshape(2, 6 * d)
    g_b_fin = g_bias[0, 12 * d:]

    def big(name, parts, w, m, v):
        shp = w.shape
        r2 = lambda a: a.reshape(-1, shp[-1])
        res = _adamw(name, parts.reshape(parts.shape[0], -1, shp[-1]), r2(w), r2(m), r2(v))
        return [a.reshape(shp) for a in res]

    upd["w_mod"] = big("adamw_w_mod", g_w_mod[None], w_mod, m_w_mod, v_w_mod)
    upd["w_fin"] = big("adamw_w_fin", g_w_fin[None], w_fin, m_w_fin, v_w_fin)
    upd["w_qkv"] = big("adamw_w_qkv", rq, w_qkv, m_w_qkv, v_w_qkv)
    upd["w_o_attn"] = big("adamw_w_o_attn", ro, w_o_attn, m_w_o_attn, v_w_o_attn)
    upd["w_in_ssm"] = big("adamw_w_in_ssm", rin, w_in_ssm, m_w_in_ssm, v_w_in_ssm)
    upd["w_glu"] = big("adamw_w_glu", rglu, w_glu, m_w_glu, v_w_glu)
    upd["w_o_ssm"] = big("adamw_w_o_ssm", ros, w_o_ssm, m_w_o_ssm, v_w_o_ssm)
    upd["w_down"] = _adamw_layers("adamw_w_down", [rd0, rd1], w_down, m_w_down, v_w_down)

    small_names = ["norm_mix", "norm_ffn", "b_mod", "a_re", "a_im", "log_dt", "b_re", "b_im", "c_re", "c_im",
                   "d_skip", "b_glu", "conv_w", "conv_b", "norm_out", "b_fin"]
    small_g = [nmix8, nffn8, g_b_mod[None], *ssm8, dskip_g8, bglu_g8, cw_g8, cb8, nout8, g_b_fin[None]]
    small_w = [norm_mix, norm_ffn, b_mod, a_re, a_im, log_dt, b_re, b_im, c_re, c_im, d_skip, b_glu, conv_w, conv_b,
               norm_out, b_fin]
    small_m = [m_norm_mix, m_norm_ffn, m_b_mod, m_a_re, m_a_im, m_log_dt, m_b_re, m_b_im, m_c_re, m_c_im, m_d_skip,
               m_b_glu, m_conv_w, m_conv_b, m_norm_out, m_b_fin]
    small_v = [v_norm_mix, v_norm_ffn, v_b_mod, v_a_re, v_a_im, v_log_dt, v_b_re, v_b_im, v_c_re, v_c_im, v_d_skip,
               v_b_glu, v_conv_w, v_conv_b, v_norm_out, v_b_fin]
    entries = [(gp.reshape((gp.shape[0],) + two_d(w)), w.reshape(two_d(w)), m.reshape(two_d(w)), v.reshape(two_d(w)))
               for gp, w, m, v in zip(small_g, small_w, small_m, small_v)]
    res = _adamw_many("adamw_small", entries)
    for j, (nm, w) in enumerate(zip(small_names, small_w)):
        upd[nm] = [res[4 * j + k].reshape(w.shape) for k in range(4)]

    order = ["norm_mix", "norm_ffn", "w_mod", "b_mod", "w_qkv", "w_o_attn", "w_in_ssm", "a_re", "a_im", "log_dt",
             "b_re", "b_im", "c_re", "c_im", "d_skip", "w_glu", "b_glu", "w_o_ssm", "w_up", "conv_w", "conv_b",
             "w_down", "norm_out", "w_fin", "b_fin"]
    outs = [loss, grad_x]
    for k in range(4):
        outs += [upd[nm][k] for nm in order]
    return tuple(outs)
```

```python
import functools
import math

import jax
import jax.numpy as jnp
from jax import lax
from jax.experimental import pallas as pl
from jax.experimental.pallas import tpu as pltpu

F32 = jnp.float32
BF16 = jnp.bfloat16
MESH = pl.DeviceIdType.MESH

N_DEV = 8
HEAD_DIM = 64
ATT_BLK = 128
ATT_BQ = 256
ATT_UNROLL = 2
SSM_H = 16
SSM_P = 64
SSM_L = 4
EPS = 1e-6
ADAM_LR, ADAM_B1, ADAM_B2, ADAM_EPS, ADAM_WD, ADAM_STEP = 0.001, 0.9, 0.999, 1e-08, 0.01, 10
V7X_VMEM_LIMIT = 56 * 1024 * 1024
LANE = 128

_NN = (((1,), (0,)), ((), ()))
_NT = (((1,), (1,)), ((), ()))
_TN = (((0,), (0,)), ((), ()))


def _cp(sem):
    return pltpu.CompilerParams(dimension_semantics=sem, vmem_limit_bytes=V7X_VMEM_LIMIT)


def _tile(n, pref):
    if n <= pref:
        return n
    t = pref - pref % 16
    while t >= 16:
        if n % t == 0:
            return t
        t -= 16
    return n


def _mm(name, a, b, grid, a_spec, b_spec, out_spec, out_shape, dims, acc_shape, ex=None):
    nk = grid[-1]
    kax = len(grid) - 1
    ex = ex or _NoExchange()

    def body(*refs):
        a_ref, b_ref = refs[:2]
        ex_ins = refs[2:2 + ex.n]
        o_ref = refs[2 + ex.n]
        ex_outs = refs[3 + ex.n:3 + 2 * ex.n]
        acc_ref = refs[3 + 2 * ex.n]
        sems = refs[4 + 2 * ex.n:]
        first, last = _grid_ends(grid)
        k = pl.program_id(kax)

        @pl.when(first)
        def _():
            ex.start(ex_ins, ex_outs, sems)

        @pl.when(k == 0)
        def _():
            acc_ref[...] = jnp.zeros(acc_shape, F32)

        acc_ref[...] += lax.dot_general(a_ref[...].astype(BF16), b_ref[...].astype(BF16), dims,
                                        preferred_element_type=F32)

        @pl.when(k == nk - 1)
        def _():
            o_ref[...] = acc_ref[...].astype(o_ref.dtype)

        @pl.when(last)
        def _():
            ex.wait(ex_ins, ex_outs, sems)

    sem = ("arbitrary",) * len(grid) if ex.n else ("parallel",) * kax + ("arbitrary",)
    res = pl.pallas_call(
        body, name=name, grid=grid, in_specs=[a_spec, b_spec] + ex.specs, out_specs=[out_spec] + ex.specs,
        out_shape=[out_shape] + ex.out_shape, scratch_shapes=[pltpu.VMEM(acc_shape, F32)] + ex.scratch,
        compiler_params=_cp(sem))(a, b, *ex.arrs)
    return (res[0], list(res[1:])) if ex.n else res[0]


def _mm_nn(name, a, w, out_dtype):
    m, k = a.shape
    n = w.shape[1]
    tm, tn, tk = _tile(m, 512), _tile(n, 1024), _tile(k, 1024)
    return _mm(name, a, w, (m // tm, n // tn, k // tk),
               pl.BlockSpec((tm, tk), lambda i, j, kk: (i, kk)), pl.BlockSpec((tk, tn), lambda i, j, kk: (kk, j)),
               pl.BlockSpec((tm, tn), lambda i, j, kk: (i, j)), jax.ShapeDtypeStruct((m, n), out_dtype), _NN, (tm, tn))


def _mm_nt(name, a, w, out_dtype):
    m, n = a.shape
    k = w.shape[0]
    tm, tko, tn = _tile(m, 512), _tile(k, 1024), _tile(n, 1024)
    return _mm(name, a, w, (m // tm, k // tko, n // tn),
               pl.BlockSpec((tm, tn), lambda i, j, kk: (i, kk)), pl.BlockSpec((tko, tn), lambda i, j, kk: (j, kk)),
               pl.BlockSpec((tm, tko), lambda i, j, kk: (i, j)), jax.ShapeDtypeStruct((m, k), out_dtype), _NT, (tm, tko))


def _mm_tn(name, a, b, out_dtype):
    t, m = a.shape
    n = b.shape[1]
    tm, tn, tk = _tile(m, 512), _tile(n, 1024), _tile(t, 1024)
    return _mm(name, a, b, (m // tm, n // tn, t // tk),
               pl.BlockSpec((tk, tm), lambda i, j, kk: (kk, i)), pl.BlockSpec((tk, tn), lambda i, j, kk: (kk, j)),
               pl.BlockSpec((tm, tn), lambda i, j, kk: (i, j)), jax.ShapeDtypeStruct((m, n), out_dtype), _TN, (tm, tn))


def _norm_mod_fwd(name, x, g, shift, scale, nb):
    t, d = x.shape
    s = t // nb
    tr = _tile(s, 512)
    nt = s // tr

    def body(x_ref, g_ref, sh_ref, sc_ref, h_ref):
        xv = x_ref[...]
        r = lax.rsqrt(jnp.mean(xv * xv, axis=-1, keepdims=True) + EPS)
        y = xv * r * g_ref[...]
        h_ref[...] = (y * (1.0 + sc_ref[...]) + sh_ref[...]).astype(h_ref.dtype)

    row = pl.BlockSpec((tr, d), lambda b, i: (b * nt + i, 0))
    vec = pl.BlockSpec((None, 1, d), lambda b, i: (b, 0, 0))
    return pl.pallas_call(body, name=name, grid=(nb, nt),
                          in_specs=[row, pl.BlockSpec((1, d), lambda b, i: (0, 0)), vec, vec],
                          out_specs=row, out_shape=jax.ShapeDtypeStruct((t, d), BF16),
                          compiler_params=_cp(("parallel", "parallel")))(x, g, shift, scale)


def _norm_mod_bwd(name, dh, x, g, shift, scale, dres, target, nb, branch=None):
    t, d = x.shape
    s = t // nb
    tr = _tile(s, 512)
    nt = s // tr
    final = target is not None
    n_in = 5 + 2 * (branch is not None)

    def body(*refs):
        if final:
            x_ref, g_ref, sh_ref, sc_ref, tg_ref = refs[:5]
        else:
            dh_ref, x_ref, g_ref, sc_ref, dres_ref = refs[:5]
        dx_ref, dg_ref, dsh_ref, dsc_ref = refs[n_in:n_in + 4]
        if final:
            loss_ref = refs[n_in + 4]
        b, i = pl.program_id(0), pl.program_id(1)
        xv = x_ref[...]
        gv = g_ref[...]
        r = lax.rsqrt(jnp.mean(xv * xv, axis=-1, keepdims=True) + EPS)
        nrm = xv * r
        y = nrm * gv
        one_sc = 1.0 + sc_ref[...]
        if final:
            err = y * one_sc + sh_ref[...] - tg_ref[...]
            dhv = err * (1.0 / d)
        else:
            dhv = dh_ref[...].astype(F32)
        dy = dhv * one_sc
        dn = dy * gv
        dxv = r * (dn - nrm * jnp.mean(dn * nrm, axis=-1, keepdims=True))
        dtot = dxv if final else dres_ref[...] + dxv
        dx_ref[...] = dtot

        @pl.when(i == 0)
        def _():
            dsh_ref[...] = jnp.zeros_like(dsh_ref)
            dsc_ref[...] = jnp.zeros_like(dsc_ref)

        if branch is not None:
            yb_ref, gate_ref = refs[5:7]
            dyb_ref, dgate_ref = refs[-2:]
            dyb_ref[...] = (gate_ref[...] * dtot).astype(dyb_ref.dtype)

            @pl.when(i == 0)
            def _():
                dgate_ref[...] = jnp.zeros_like(dgate_ref)

            dgate_ref[...] += jnp.sum(dtot * yb_ref[...], axis=0, keepdims=True)

        @pl.when((i == 0) & (b == 0))
        def _():
            dg_ref[...] = jnp.zeros_like(dg_ref)
            if final:
                loss_ref[...] = jnp.zeros_like(loss_ref)

        dsh_ref[...] += jnp.sum(dhv, axis=0, keepdims=True)
        dsc_ref[...] += jnp.sum(dhv * y, axis=0, keepdims=True)
        dg_ref[...] += jnp.sum(dy * nrm, axis=0, keepdims=True)
        if final:
            loss_ref[...] += (0.5 / d) * jnp.sum(err * err)

    row = pl.BlockSpec((tr, d), lambda b, i: (b * nt + i, 0))
    vec = pl.BlockSpec((None, 1, d), lambda b, i: (b, 0, 0))
    gsp = pl.BlockSpec((1, d), lambda b, i: (0, 0))
    out_specs = [row, gsp, vec, vec]
    out_shape = [jax.ShapeDtypeStruct((t, d), F32), jax.ShapeDtypeStruct((1, d), F32),
                 jax.ShapeDtypeStruct((nb, 1, d), F32), jax.ShapeDtypeStruct((nb, 1, d), F32)]
    if final:
        ins, in_specs = [x, g, shift, scale, target], [row, gsp, vec, vec, row]
        out_specs.append(pl.BlockSpec((8, LANE), lambda b, i: (0, 0)))
        out_shape.append(jax.ShapeDtypeStruct((8, LANE), F32))
    else:
        ins, in_specs = [dh, x, g, scale, dres], [row, row, gsp, vec, row]
    if branch is not None:
        ins, in_specs = ins + list(branch), in_specs + [row, vec]
        out_specs += [row, vec]
        out_shape += [jax.ShapeDtypeStruct((t, d), BF16), jax.ShapeDtypeStruct((nb, 1, d), F32)]
    return pl.pallas_call(body, name=name, grid=(nb, nt), in_specs=in_specs, out_specs=out_specs,
                          out_shape=out_shape, compiler_params=_cp(("arbitrary", "arbitrary")))(*ins)


def _gate_add(name, x, y, gate, nb):
    t, d = x.shape
    s = t // nb
    tr = _tile(s, 512)
    nt = s // tr

    def body(x_ref, y_ref, g_ref, o_ref):
        o_ref[...] = x_ref[...] + g_ref[...] * y_ref[...]

    row = pl.BlockSpec((tr, d), lambda b, i: (b * nt + i, 0))
    vec = pl.BlockSpec((None, 1, d), lambda b, i: (b, 0, 0))
    return pl.pallas_call(body, name=name, grid=(nb, nt), in_specs=[row, row, vec], out_specs=row,
                          out_shape=jax.ShapeDtypeStruct((t, d), F32),
                          compiler_params=_cp(("parallel", "parallel")))(x, y, gate)


def _res_norm_fwd(name, x, y, gate, g, shift, scale, nb):
    t, d = x.shape
    s = t // nb
    tr = _tile(s, 512)
    nt = s // tr

    def body(x_ref, y_ref, gate_ref, g_ref, sh_ref, sc_ref, xo_ref, h_ref):
        xn = x_ref[...] + gate_ref[...] * y_ref[...]
        xo_ref[...] = xn
        r = lax.rsqrt(jnp.mean(xn * xn, axis=-1, keepdims=True) + EPS)
        h_ref[...] = (xn * r * g_ref[...] * (1.0 + sc_ref[...]) + sh_ref[...]).astype(h_ref.dtype)

    row = pl.BlockSpec((tr, d), lambda b, i: (b * nt + i, 0))
    vec = pl.BlockSpec((None, 1, d), lambda b, i: (b, 0, 0))
    return pl.pallas_call(body, name=name, grid=(nb, nt),
                          in_specs=[row, row, vec, pl.BlockSpec((1, d), lambda b, i: (0, 0)), vec, vec],
                          out_specs=[row, row],
                          out_shape=[jax.ShapeDtypeStruct((t, d), F32), jax.ShapeDtypeStruct((t, d), BF16)],
                          compiler_params=_cp(("parallel", "parallel")))(x, y, gate, g, shift, scale)


def _log_sigmoid(z):
    return jnp.minimum(z, 0.0) - jnp.log(1.0 + jnp.exp(-jnp.abs(z)))


def _split_dot(v, tri):
    hi = v.astype(BF16)
    lo = (v - hi.astype(F32)).astype(BF16)
    return (jnp.dot(hi, tri, preferred_element_type=F32) + jnp.dot(lo, tri, preferred_element_type=F32))


def _grid_ends(grid):
    ids = [pl.program_id(a) for a in range(len(grid))]
    first = functools.reduce(lambda u, w: u & w, [i == 0 for i in ids])
    last = functools.reduce(lambda u, w: u & w, [i == n - 1 for i, n in zip(ids, grid)])
    return first, last


def _attn_trips(nq):
    return nq * (nq + 1) // 2


def _next_trip(qi, jj, nq):
    wrap = jj >= qi
    nqi = jnp.where(wrap, jnp.minimum(qi + 1, nq - 1), qi)
    njj = jnp.where(wrap, jnp.where(qi + 1 < nq, 0, jj), jj + 1)
    return nqi, njj


def _attn_fwd(qkv, nb, s, d, ex):
    t = nb * s
    npair = d // LANE
    bk = ATT_BLK
    bq = min(ATT_BQ, s)
    nq = s // bq
    kpq = bq // bk
    nheads = LANE // HEAD_DIM
    scale = HEAD_DIM ** -0.5
    grid = (nb, npair)
    assert s // bk <= HEAD_DIM, "one carry lane per key block and head"
    assert bk == LANE, "the running sums are kept one 128-lane tile wide"
    assert kpq == ATT_UNROLL, "query block qi has exactly qi + 1 trips"

    def body(*refs):
        q_ref, k_ref, v_ref = refs[:3]
        ex_ins = refs[3:3 + ex.n]
        o_ref, car_ref = refs[3 + ex.n:5 + ex.n]
        ex_outs = refs[5 + ex.n:5 + 2 * ex.n]
        acc_s, run_s, z_s, arg_s = refs[5 + 2 * ex.n:9 + 2 * ex.n]
        sems = refs[9 + 2 * ex.n:]
        first, last = _grid_ends(grid)

        @pl.when(first)
        def _():
            ex.start(ex_ins, ex_outs, sems)

        lane = lax.broadcasted_iota(jnp.int32, (1, LANE), 1)
        row = lax.broadcasted_iota(jnp.int32, (bq, bk), 0)
        col = lax.broadcasted_iota(jnp.int32, (bq, bk), 1)
        trow = lax.broadcasted_iota(jnp.int32, (bk, bk), 0)
        tcol = lax.broadcasted_iota(jnp.int32, (bk, bk), 1)
        tri = (trow > tcol).astype(BF16)
        hms = [(lane // HEAD_DIM) == hh for hh in range(nheads)]

        def q0_of(qi):
            return pl.multiple_of(qi * bq, bq)

        def kblk_of(qi, jj, u):
            return (qi + 1) * kpq - 1 - (ATT_UNROLL * jj + u)

        def scores(qi, jj):
            q = q_ref[pl.ds(q0_of(qi), bq), :]
            qhs = [jnp.where(hm, q, jnp.zeros_like(q)) * scale for hm in hms]
            ks = [k_ref[pl.ds(pl.multiple_of(kblk_of(qi, jj, u) * bk, bk), bk), :] for u in range(ATT_UNROLL)]
            return [[lax.dot_general(qhs[hh], kj, _NT, preferred_element_type=F32) for kj in ks]
                    for hh in range(nheads)]

        def keep(zn):
            for hh in range(nheads):
                for u in range(ATT_UNROLL):
                    z_s[hh, u] = zn[hh][u]

        def exponents(qi, jj):
            q0 = q0_of(qi)
            car = car_ref[pl.ds(q0, bq), :]
            for hh in range(nheads):
                run = jnp.where(jj == 0, 0.0, run_s[hh])
                for u in range(ATT_UNROLL):
                    j = kblk_of(qi, jj, u)
                    mask = (j * bk + col) < (q0 + row)
                    z = z_s[hh, u]
                    lb = _log_sigmoid(z)
                    l1 = jnp.where(mask, lb - z, 0.0)
                    arg_s[hh, u] = jnp.where(mask, lb + (_split_dot(l1, tri) + run), -1e30)
                    car = jnp.where(lane == hh * HEAD_DIM + j, run, car)
                    run = run + jnp.sum(l1, axis=1, keepdims=True)
                run_s[hh] = run
            car_ref[pl.ds(q0, bq), :] = car

        def weigh(qi, jj):
            q0 = q0_of(qi)
            for hh in range(nheads):
                acc = None
                for u in range(ATT_UNROLL):
                    vj = v_ref[pl.ds(pl.multiple_of(kblk_of(qi, jj, u) * bk, bk), bk), :]
                    pv = jnp.dot(jnp.exp(arg_s[hh, u]).astype(BF16), vj, preferred_element_type=F32)
                    acc = pv if acc is None else acc + pv
                acc_s[hh, pl.ds(q0, bq), :] += acc

        def step(n, carry):
            qi, jj, pqi, pjj = carry
            nqi, njj = _next_trip(qi, jj, nq)
            zn = scores(nqi, njj)
            weigh(pqi, pjj)
            exponents(qi, jj)
            keep(zn)
            return nqi, njj, qi, jj

        acc_s[...] = jnp.zeros_like(acc_s)
        run_s[...] = jnp.zeros_like(run_s)
        car_ref[...] = jnp.zeros_like(car_ref)
        arg_s[...] = jnp.full(arg_s.shape, -1e30, F32)
        zero = jnp.int32(0)
        keep(scores(zero, zero))
        _, _, lqi, ljj = lax.fori_loop(0, _attn_trips(nq), step, (zero, zero, zero, zero))
        weigh(lqi, ljj)
        out = acc_s[0]
        for hh in range(1, nheads):
            out = jnp.where(hms[hh], acc_s[hh], out)
        o_ref[...] = out.astype(o_ref.dtype)

        @pl.when(last)
        def _():
            ex.wait(ex_ins, ex_outs, sems)

    seq = lambda off: pl.BlockSpec((s, LANE), lambda b, p: (b, off + p))
    res = pl.pallas_call(
        body, name="attn_fwd", grid=grid,
        in_specs=[seq(0), seq(npair), seq(2 * npair)] + ex.specs,
        out_specs=[seq(0), seq(0)] + ex.specs,
        out_shape=[jax.ShapeDtypeStruct((t, d), BF16), jax.ShapeDtypeStruct((t, d), F32)] + ex.out_shape,
        scratch_shapes=[pltpu.VMEM((nheads, s, LANE), F32), pltpu.VMEM((nheads, bq, LANE), F32),
                        pltpu.VMEM((nheads, ATT_UNROLL, bq, bk), F32),
                        pltpu.VMEM((nheads, ATT_UNROLL, bq, bk), F32)] + ex.scratch,
        compiler_params=_cp(("arbitrary", "arbitrary")))(qkv, qkv, qkv, *ex.arrs)
    return res[0], res[1], list(res[2:])


def _attn_bwd(qkv, car, do, nb, s, d, ex):
    t = nb * s
    npair = d // LANE
    bk = ATT_BLK
    bq = min(ATT_BQ, s)
    nq = s // bq
    kpq = bq // bk
    nheads = LANE // HEAD_DIM
    scale = HEAD_DIM ** -0.5
    grid = (nb, npair)
    assert kpq == ATT_UNROLL, "query block qi has exactly qi + 1 trips"

    def body(*refs):
        q_ref, k_ref, v_ref, car_ref, do_ref = refs[:5]
        ex_ins = refs[5:5 + ex.n]
        dq_ref, dk_ref, dv_ref = refs[5 + ex.n:8 + ex.n]
        ex_outs = refs[8 + ex.n:8 + 2 * ex.n]
        dk_acc, dv_acc, dq_s, rune_s, z_s, da_s, dz_s, a_s = refs[8 + 2 * ex.n:16 + 2 * ex.n]
        sems = refs[16 + 2 * ex.n:]
        first, last = _grid_ends(grid)

        @pl.when(first)
        def _():
            ex.start(ex_ins, ex_outs, sems)

        lane = lax.broadcasted_iota(jnp.int32, (1, LANE), 1)
        row = lax.broadcasted_iota(jnp.int32, (bq, bk), 0)
        col = lax.broadcasted_iota(jnp.int32, (bq, bk), 1)
        trow = lax.broadcasted_iota(jnp.int32, (bk, bk), 0)
        tcol = lax.broadcasted_iota(jnp.int32, (bk, bk), 1)
        tri_suf = (trow > tcol).astype(BF16)
        tri_pre = (trow < tcol).astype(BF16)
        hms = [(lane // HEAD_DIM) == hh for hh in range(nheads)]

        def q0_of(qi):
            return pl.multiple_of(qi * bq, bq)

        def k0_of(jj, u):
            return pl.multiple_of((ATT_UNROLL * jj + u) * bk, bk)

        def heads_of(ref, qi, factor):
            x = ref[pl.ds(q0_of(qi), bq), :]
            return [jnp.where(hm, x, jnp.zeros_like(x)) * factor for hm in hms]

        def products(qi, jj):
            qhs, dohs = heads_of(q_ref, qi, scale), heads_of(do_ref, qi, 1.0)
            ks = [k_ref[pl.ds(k0_of(jj, u), bk), :] for u in range(ATT_UNROLL)]
            vs = [v_ref[pl.ds(k0_of(jj, u), bk), :] for u in range(ATT_UNROLL)]
            zn = [[lax.dot_general(qhs[hh], kj, _NT, preferred_element_type=F32) for kj in ks] for hh in range(nheads)]
            dn = [[lax.dot_general(dohs[hh], vj, _NT, preferred_element_type=F32) for vj in vs] for hh in range(nheads)]
            return zn, dn

        def keep(zn, dn):
            for hh in range(nheads):
                for u in range(ATT_UNROLL):
                    z_s[hh, u] = zn[hh][u]
                    da_s[hh, u] = dn[hh][u]

        def middle(qi, jj):
            q0 = q0_of(qi)
            car = car_ref[pl.ds(q0, bq), :]
            for hh in range(nheads):
                run_e = jnp.where(jj == 0, 0.0, rune_s[hh])
                for u in range(ATT_UNROLL):
                    j = ATT_UNROLL * jj + u
                    mask = (j * bk + col) < (q0 + row)
                    z = z_s[hh, u]
                    lb = _log_sigmoid(z)
                    l1u = lb - z
                    l1 = jnp.where(mask, l1u, 0.0)
                    run = jnp.sum(jnp.where(lane == hh * HEAD_DIM + j, car, 0.0), axis=1, keepdims=True)
                    a = jnp.where(mask, jnp.exp(lb + (_split_dot(l1, tri_suf) + run)), 0.0)
                    e = da_s[hh, u] * a
                    dz = e * jnp.exp(l1u) - (_split_dot(e, tri_pre) + run_e) * jnp.exp(lb)
                    dz_s[hh, u] = jnp.where(mask, dz, 0.0).astype(BF16)
                    a_s[hh, u] = a.astype(BF16)
                    run_e = run_e + jnp.sum(e, axis=1, keepdims=True)
                rune_s[hh] = run_e

        def grads(qi, jj):
            q0 = q0_of(qi)
            qhs, dohs = heads_of(q_ref, qi, scale), heads_of(do_ref, qi, 1.0)
            dqs = [None] * nheads
            for u in range(ATT_UNROLL):
                k0 = k0_of(jj, u)
                kj = k_ref[pl.ds(k0, bk), :]
                for hh in range(nheads):
                    dzb = dz_s[hh, u]
                    dqu = jnp.dot(dzb, kj, preferred_element_type=F32)
                    dqs[hh] = dqu if dqs[hh] is None else dqs[hh] + dqu
                    dkh = lax.dot_general(dzb, qhs[hh], _TN, preferred_element_type=F32)
                    dvh = lax.dot_general(a_s[hh, u], dohs[hh], _TN, preferred_element_type=F32)
                    dk_blk = dkh if hh == 0 else dk_blk + dkh
                    dv_blk = dvh if hh == 0 else dv_blk + dvh
                dk_acc[pl.ds(k0, bk), :] += dk_blk
                dv_acc[pl.ds(k0, bk), :] += dv_blk
            for hh in range(nheads):
                dq_s[hh, pl.ds(q0, bq), :] += dqs[hh]

        def step(n, carry):
            qi, jj, pqi, pjj = carry
            nqi, njj = _next_trip(qi, jj, nq)
            zn, dn = products(nqi, njj)
            grads(pqi, pjj)
            middle(qi, jj)
            keep(zn, dn)
            return nqi, njj, qi, jj

        dk_acc[...] = jnp.zeros_like(dk_acc)
        dv_acc[...] = jnp.zeros_like(dv_acc)
        dq_s[...] = jnp.zeros_like(dq_s)
        rune_s[...] = jnp.zeros_like(rune_s)
        dz_s[...] = jnp.zeros_like(dz_s)
        a_s[...] = jnp.zeros_like(a_s)
        zero = jnp.int32(0)
        keep(*products(zero, zero))
        _, _, lqi, ljj = lax.fori_loop(0, _attn_trips(nq), step, (zero, zero, zero, zero))
        grads(lqi, ljj)
        dq_out = dq_s[0]
        for hh in range(1, nheads):
            dq_out = jnp.where(hms[hh], dq_s[hh], dq_out)
        dq_ref[...] = (dq_out * scale).astype(dq_ref.dtype)
        dk_ref[...] = dk_acc[...].astype(dk_ref.dtype)
        dv_ref[...] = dv_acc[...].astype(dv_ref.dtype)

        @pl.when(last)
        def _():
            ex.wait(ex_ins, ex_outs, sems)

    seq = lambda off: pl.BlockSpec((s, LANE), lambda b, p: (b, off + p))
    sds = jax.ShapeDtypeStruct((t, d), BF16)
    res = pl.pallas_call(
        body, name="attn_bwd", grid=grid,
        in_specs=[seq(0), seq(npair), seq(2 * npair), seq(0), seq(0)] + ex.specs,
        out_specs=[seq(0), seq(0), seq(0)] + ex.specs, out_shape=[sds, sds, sds] + ex.out_shape,
        scratch_shapes=[pltpu.VMEM((s, LANE), F32), pltpu.VMEM((s, LANE), F32),
                        pltpu.VMEM((nheads, s, LANE), F32), pltpu.VMEM((nheads, bq, LANE), F32),
                        pltpu.VMEM((nheads, ATT_UNROLL, bq, bk), F32), pltpu.VMEM((nheads, ATT_UNROLL, bq, bk), F32),
                        pltpu.VMEM((nheads, ATT_UNROLL, bq, bk), BF16),
                        pltpu.VMEM((nheads, ATT_UNROLL, bq, bk), BF16)] + ex.scratch,
        compiler_params=_cp(("arbitrary", "arbitrary")))(qkv, qkv, qkv, car, do, *ex.arrs)
    return res[0], res[1], res[2], list(res[3:])


def _conv3(u_ref, w, bias, c, r0, rc):
    x = u_ref[pl.ds(r0, rc), :].astype(F32)
    p0 = pl.multiple_of(jnp.maximum(r0 - 16, 0), 16)
    prev = u_ref[pl.ds(p0, 16), :].astype(F32)
    prev = jnp.where(c > 0, prev, 0.0)
    row = lax.broadcasted_iota(jnp.int32, (rc, 1), 0)
    s1 = jnp.where(row == 0, prev[15:16, :], pltpu.roll(x, 1, 0))
    s2 = jnp.where(row == 0, prev[14:15, :], jnp.where(row == 1, prev[15:16, :], pltpu.roll(x, 2, 0)))
    cv = w[2:3, :] * x + w[1:2, :] * s1 + w[0:1, :] * s2 + bias
    return cv, x, s1, s2


def _sigmoid(x):
    return 1.0 / (1.0 + jnp.exp(-x))


def _ffn_act_fwd(name, up8, cw8, cb8, nb, s):
    _, t, c_w = up8.shape
    rc = _tile(s, 256)
    nch = s // rc
    half = N_DEV // 2

    def body(ug_ref, uv_ref, wg_ref, wv_ref, bg_ref, bv_ref, act_ref):
        wg, wv, bg, bv = wg_ref[...], wv_ref[...], bg_ref[...], bv_ref[...]

        def chunk(c, carry):
            r0 = pl.multiple_of(c * rc, rc)
            cg = _conv3(ug_ref, wg, bg, c, r0, rc)[0]
            cv = _conv3(uv_ref, wv, bv, c, r0, rc)[0]
            act_ref[pl.ds(r0, rc), :] = (cg * _sigmoid(cg) * cv).astype(act_ref.dtype)
            return carry

        lax.fori_loop(0, nch, chunk, 0)

    def slab(off):
        return pl.BlockSpec((None, s, c_w), lambda k, b: (k + off, b, 0))

    def par(rows, off):
        return pl.BlockSpec((None, rows, c_w), lambda k, b: (k + off, 0, 0))

    return pl.pallas_call(
        body, name=name, grid=(half, nb),
        in_specs=[slab(0), slab(half), par(3, 0), par(3, half), par(1, 0), par(1, half)],
        out_specs=pl.BlockSpec((None, s, c_w), lambda k, b: (k, b, 0)),
        out_shape=jax.ShapeDtypeStruct((half, t, c_w), BF16),
        compiler_params=_cp(("parallel", "parallel")))(up8, up8, cw8, cw8, cb8, cb8)


def _ffn_act_bwd(name, up8, dact4, cw8, cb8, nb, s):
    _, t, c_w = up8.shape
    rc = _tile(s, 256)
    nch = s // rc
    half = N_DEV // 2

    def body(u_ref, da_ref, w_ref, b_ref, dup_ref, dcw_ref, dcb_ref):
        w2, b2 = w_ref[...], b_ref[...]
        row = lax.broadcasted_iota(jnp.int32, (rc, 1), 0)

        @pl.when(pl.program_id(1) == 0)
        def _():
            dcw_ref[...] = jnp.zeros_like(dcw_ref)
            dcb_ref[...] = jnp.zeros_like(dcb_ref)

        def chunk(i, carry):
            c = nch - 1 - i
            r0 = pl.multiple_of(c * rc, rc)
            convs = [_conv3(u_ref.at[h], w2[h], b2[h], c, r0, rc) for h in range(2)]
            gt, vl = convs[0][0], convs[1][0]
            da = da_ref[pl.ds(r0, rc), :].astype(F32)
            sg = _sigmoid(gt)
            dcvs = [da * vl * sg * (1.0 + gt * (1.0 - sg)), da * gt * sg]
            out = []
            for h in range(2):
                n0, n1, a0, a1, a2, ab = carry[6 * h:6 * h + 6]
                dcv, (_, x, s1, s2), w = dcvs[h], convs[h], w2[h]
                t1 = jnp.where(row == rc - 1, n0, pltpu.roll(dcv, rc - 1, 0))
                t2 = jnp.where(row == rc - 2, n0, jnp.where(row == rc - 1, n1, pltpu.roll(dcv, rc - 2, 0)))
                dup = w[2:3, :] * dcv + w[1:2, :] * t1 + w[0:1, :] * t2
                dup_ref[h, pl.ds(r0, rc), :] = dup.astype(dup_ref.dtype)
                out += [dcv[0:1, :], dcv[1:2, :],
                        a0 + jnp.sum(dcv * s2, axis=0, keepdims=True), a1 + jnp.sum(dcv * s1, axis=0, keepdims=True),
                        a2 + jnp.sum(dcv * x, axis=0, keepdims=True), ab + jnp.sum(dcv, axis=0, keepdims=True)]
            return tuple(out)

        z = jnp.zeros((1, c_w), F32)
        fin = lax.fori_loop(0, nch, chunk, (z,) * 12)
        for h in range(2):
            _, _, a0, a1, a2, ab = fin[6 * h:6 * h + 6]
            dcw_ref[h, 0:1, :] += a0
            dcw_ref[h, 1:2, :] += a1
            dcw_ref[h, 2:3, :] += a2
            dcb_ref[h] += ab

    def pair(rows, per_seq):
        return pl.BlockSpec((2, None, rows, c_w), (lambda k, b: (0, k, b, 0)) if per_seq else (lambda k, b: (0, k, 0, 0)))

    four = lambda a: a.reshape((2, half) + a.shape[1:])
    dup, dcw, dcb = pl.pallas_call(
        body, name=name, grid=(half, nb),
        in_specs=[pair(s, True), pl.BlockSpec((None, s, c_w), lambda k, b: (k, b, 0)), pair(3, False), pair(1, False)],
        out_specs=[pair(s, True), pair(3, False), pair(1, False)],
        out_shape=[jax.ShapeDtypeStruct((2, half, t, c_w), BF16), jax.ShapeDtypeStruct((2, half, 3, c_w), F32),
                   jax.ShapeDtypeStruct((2, half, 1, c_w), F32)],
        compiler_params=_cp(("parallel", "arbitrary")))(four(up8), dact4, four(cw8), four(cb8))
    return dup.reshape(N_DEV, t, c_w), dcw.reshape(N_DEV, 3, c_w), dcb.reshape(N_DEV, 1, c_w)


_GELU_C0 = math.sqrt(2.0 / math.pi)
_GELU_C1 = 0.044715


def _rowwise(name, body, ins, in_kinds, out_kinds, t, d, tr_pref=512):
    tr = _tile(t, tr_pref)
    row = pl.BlockSpec((tr, d), lambda i: (i, 0))
    vec = pl.BlockSpec((1, d), lambda i: (0, 0))
    in_specs = [row if k == "row" else vec for k in in_kinds]
    out_specs = [row if k[0] == "row" else vec for k in out_kinds]
    out_shape = [jax.ShapeDtypeStruct((t, d) if k[0] == "row" else (1, d), k[1]) for k in out_kinds]
    has_acc = any(k[0] == "acc" for k in out_kinds)
    return pl.pallas_call(body, name=name, grid=(t // tr,), in_specs=in_specs, out_specs=out_specs,
                          out_shape=out_shape,
                          compiler_params=_cp(("arbitrary",) if has_acc else ("parallel",)))(*ins)


def _ssm_post_fwd(ys, u, dskip):
    t, d = ys.shape

    def body(ys_ref, u_ref, ds_ref, y_ref, z_ref):
        y = ys_ref[...].astype(F32) + ds_ref[...] * u_ref[...].astype(F32)
        y_ref[...] = y
        th = jnp.tanh(_GELU_C0 * (y + _GELU_C1 * y * y * y))
        z_ref[...] = (0.5 * y * (1.0 + th)).astype(z_ref.dtype)

    return _rowwise("ssm_post_fwd", body, [ys, u, dskip], ["row", "row", "vec"],
                    [("row", F32), ("row", BF16)], t, d)


def _glu_fwd(z, gl, bglu):
    t, d = z.shape

    def body(z_ref, gl_ref, b_ref, o_ref):
        o_ref[...] = (z_ref[...].astype(F32) * _sigmoid(gl_ref[...] + b_ref[...])).astype(o_ref.dtype)

    return _rowwise("glu_fwd", body, [z, gl, bglu], ["row", "row", "vec"], [("row", BF16)], t, d)[0]


def _glu_bwd(dgg, z, gl, bglu):
    t, d = z.shape

    def body(dg_ref, z_ref, gl_ref, b_ref, dgl_ref, dz_ref, db_ref):
        sg = _sigmoid(gl_ref[...] + b_ref[...])
        dg = dg_ref[...]
        dgl = dg * z_ref[...].astype(F32) * sg * (1.0 - sg)
        dgl_ref[...] = dgl.astype(dgl_ref.dtype)
        dz_ref[...] = dg * sg

        @pl.when(pl.program_id(0) == 0)
        def _():
            db_ref[...] = jnp.zeros_like(db_ref)

        db_ref[...] += jnp.sum(dgl, axis=0, keepdims=True)

    return _rowwise("glu_bwd", body, [dgg, z, gl, bglu], ["row", "row", "row", "vec"],
                    [("row", BF16), ("row", F32), ("acc", F32)], t, d)


def _ssm_post_bwd(dz1, dz2, y, u, dskip):
    t, d = y.shape

    def body(a_ref, b_ref, y_ref, u_ref, ds_ref, dy_ref, du_ref, dd_ref):
        yv = y_ref[...]
        inner = _GELU_C0 * (yv + _GELU_C1 * yv * yv * yv)
        th = jnp.tanh(inner)
        dgelu = 0.5 * (1.0 + th) + 0.5 * yv * (1.0 - th * th) * _GELU_C0 * (1.0 + 3.0 * _GELU_C1 * yv * yv)
        dy = (a_ref[...] + b_ref[...]) * dgelu
        dy_ref[...] = dy.astype(dy_ref.dtype)
        du_ref[...] = dy * ds_ref[...]

        @pl.when(pl.program_id(0) == 0)
        def _():
            dd_ref[...] = jnp.zeros_like(dd_ref)

        dd_ref[...] += jnp.sum(dy * u_ref[...].astype(F32), axis=0, keepdims=True)

    return _rowwise("ssm_post_bwd", body, [dz1, dz2, y, u, dskip], ["row", "row", "row", "row", "vec"],
                    [("row", BF16), ("row", F32), ("acc", F32)], t, d)


def _add_cast(a, b):
    t, d = a.shape

    def body(a_ref, b_ref, o_ref):
        o_ref[...] = (a_ref[...].astype(F32) + b_ref[...].astype(F32)).astype(o_ref.dtype)

    return _rowwise("add_cast", body, [a, b], ["row", "row"], [("row", BF16)], t, d)[0]


def _ssm_scan(e_re, e_im, lam_re, lam_im, nb):
    r, n = e_re.shape
    nc = r // nb
    cb = _tile(n, 512)

    def body(er_ref, ei_ref, lr_ref, li_ref, xr_ref, xi_ref):
        lr, li = lr_ref[...], li_ref[...]
        rid = lax.broadcasted_iota(jnp.int32, (8, 1), 0)

        def tile(i, carry):
            out = []
            for b in range(nb):
                xr, xi = carry[2 * b:2 * b + 2]
                r0 = pl.multiple_of(b * nc + i * 8, 8)
                er, ei = er_ref[pl.ds(r0, 8), :], ei_ref[pl.ds(r0, 8), :]
                outr, outi = jnp.zeros((8, cb), F32), jnp.zeros((8, cb), F32)
                for j in range(8):
                    outr = jnp.where(rid == j, xr, outr)
                    outi = jnp.where(rid == j, xi, outi)
                    xr, xi = lr * xr - li * xi + er[j:j + 1, :], li * xr + lr * xi + ei[j:j + 1, :]
                xr_ref[pl.ds(r0, 8), :] = outr
                xi_ref[pl.ds(r0, 8), :] = outi
                out += [xr, xi]
            return tuple(out)

        lax.fori_loop(0, nc // 8, tile, (jnp.zeros((1, cb), F32),) * (2 * nb))

    mat = pl.BlockSpec((r, cb), lambda j: (0, j))
    vec = pl.BlockSpec((1, cb), lambda j: (0, j))
    sds = jax.ShapeDtypeStruct((r, n), F32)
    return pl.pallas_call(body, name="ssm_scan", grid=(n // cb,), in_specs=[mat, mat, vec, vec],
                          out_specs=[mat, mat], out_shape=[sds, sds],
                          compiler_params=_cp(("parallel",)))(e_re, e_im, lam_re, lam_im)


def _ssm_scan_bwd(dxp_re, dxp_im, lam_re, lam_im, nb):
    r, n = dxp_re.shape
    nc = r // nb
    cb = _tile(n, 512)

    def body(dr_ref, di_ref, lr_ref, li_ref, er_ref, ei_ref):
        lr, li = lr_ref[...], li_ref[...]
        rid = lax.broadcasted_iota(jnp.int32, (8, 1), 0)

        def tile(i, carry):
            out = []
            for b in range(nb):
                gr, gi = carry[2 * b:2 * b + 2]
                r0 = pl.multiple_of(b * nc + (nc // 8 - 1 - i) * 8, 8)
                dr, di = dr_ref[pl.ds(r0, 8), :], di_ref[pl.ds(r0, 8), :]
                outr, outi = jnp.zeros((8, cb), F32), jnp.zeros((8, cb), F32)
                for j in range(7, -1, -1):
                    outr = jnp.where(rid == j, gr, outr)
                    outi = jnp.where(rid == j, gi, outi)
                    gr, gi = dr[j:j + 1, :] + lr * gr + li * gi, di[j:j + 1, :] + lr * gi - li * gr
                er_ref[pl.ds(r0, 8), :] = outr
                ei_ref[pl.ds(r0, 8), :] = outi
                out += [gr, gi]
            return tuple(out)

        lax.fori_loop(0, nc // 8, tile, (jnp.zeros((1, cb), F32),) * (2 * nb))

    mat = pl.BlockSpec((r, cb), lambda j: (0, j))
    vec = pl.BlockSpec((1, cb), lambda j: (0, j))
    sds = jax.ShapeDtypeStruct((r, n), F32)
    return pl.pallas_call(body, name="ssm_scan_bwd", grid=(n // cb,), in_specs=[mat, mat, vec, vec],
                          out_specs=[mat, mat], out_shape=[sds, sds],
                          compiler_params=_cp(("parallel",)))(dxp_re, dxp_im, lam_re, lam_im)


def _ssm_compact(a_re, a_im, log_dt, b_re, b_im, c_re, c_im):
    g, p = a_re.shape
    h = b_re.shape[-1]
    ln = SSM_L
    sg = LANE // h
    na = g // sg
    hp = lax.Precision.HIGHEST
    lam = lax.complex(a_re, a_im)
    ldt = lam * jnp.exp(log_dt)[:, None]
    lam_bar = jnp.exp(ldt)
    bbar = ((lam_bar - 1.0) / lam)[..., None] * lax.complex(b_re, b_im)
    cm = lax.complex(c_re, c_im)
    steps = jnp.arange(ln + 1, dtype=F32)
    pw = jnp.exp(ldt[:, None, :] * steps[None, :, None])
    kd = jnp.einsum("ghp,gdp,gpk->gdhk", cm, pw[:, :ln], bbar, precision=hp).real

    def stacked(x, rows_per, cols_per):
        x = x.reshape(na, sg, ln, rows_per, cols_per).transpose(0, 2, 1, 3, 4).reshape(na, ln, sg * rows_per, cols_per)
        return jnp.pad(x, ((0, 0), (0, 0), (0, 0), (0, LANE - cols_per)))

    wxc = (pw[:, ln - 1 - jnp.arange(ln)][:, :, :, None] * bbar[:, None]).transpose(0, 1, 3, 2)
    cpc = (cm[:, None] * pw[:, 1:ln + 1][:, :, None, :]).transpose(0, 1, 3, 2)
    lam_l = pw[:, ln]
    return (stacked(kd.transpose(0, 1, 3, 2), h, h), stacked(wxc.real, h, p), stacked(wxc.imag, h, p),
            stacked(cpc.real, p, h), stacked(-cpc.imag, p, h),
            lam_l.real.reshape(1, g * p), lam_l.imag.reshape(1, g * p))


def _ssm_masks(h, p):
    sg = LANE // h
    r128 = lax.broadcasted_iota(jnp.int32, (LANE, LANE), 0)
    c128 = lax.broadcasted_iota(jnp.int32, (LANE, LANE), 1)
    rx = lax.broadcasted_iota(jnp.int32, (LANE, sg * p), 0)
    cx = lax.broadcasted_iota(jnp.int32, (LANE, sg * p), 1)
    ry = lax.broadcasted_iota(jnp.int32, (sg * p, LANE), 0)
    cy = lax.broadcasted_iota(jnp.int32, (sg * p, LANE), 1)
    f = lambda m: m.astype(F32)
    return dict(
        spread_h=f((r128 < h) & (c128 % h == r128)),
        spread_p=f((rx < p) & (cx % p == rx)),
        gather_h=f((c128 < h) & (r128 % h == c128)),
        gather_p=f((cy < p) & (ry % p == cy)),
        same_t=f(r128 // h == c128 // h), same_x=f(rx // h == cx // p), same_y=f(ry // p == cy // h))


def _place(a, spread):
    return jnp.dot(a.astype(BF16), spread.astype(BF16), preferred_element_type=F32)


def _ssm_expand(compact, ex):
    kt, wxr, wxi, wyr, wyi = compact
    na, ln = kt.shape[:2]
    wst = wyr.shape[2]
    h, p = SSM_H, SSM_P
    grid = (na,)

    def body(*refs):
        kt_ref, wxr_ref, wxi_ref, wyr_ref, wyi_ref = refs[:5]
        ex_ins = refs[5:5 + ex.n]
        tm_ref, xr_ref, xi_ref, yr_ref, yi_ref = refs[5 + ex.n:10 + ex.n]
        ex_outs = refs[10 + ex.n:10 + 2 * ex.n]
        sems = refs[10 + 2 * ex.n:]
        first, last = _grid_ends(grid)

        @pl.when(first)
        def _():
            ex.start(ex_ins, ex_outs, sems)

        m = _ssm_masks(h, p)
        ktb = [_place(kt_ref[lag], m["spread_h"]) * m["same_t"] for lag in range(ln)]
        zero = jnp.zeros((LANE, LANE), F32)
        for sig in range(ln):
            rows = slice(sig * LANE, (sig + 1) * LANE)
            tm_ref[rows, :] = jnp.concatenate([ktb[tau - sig] if tau >= sig else zero for tau in range(ln)],
                                              axis=1).astype(tm_ref.dtype)
            xr_ref[rows, :] = (_place(wxr_ref[sig], m["spread_p"]) * m["same_x"]).astype(xr_ref.dtype)
            xi_ref[rows, :] = (_place(wxi_ref[sig], m["spread_p"]) * m["same_x"]).astype(xi_ref.dtype)
        for tau in range(ln):
            cols = slice(tau * LANE, (tau + 1) * LANE)
            yr_ref[:, cols] = (_place(wyr_ref[tau], m["spread_h"]) * m["same_y"]).astype(yr_ref.dtype)
            yi_ref[:, cols] = (_place(wyi_ref[tau], m["spread_h"]) * m["same_y"]).astype(yi_ref.dtype)

        @pl.when(last)
        def _():
            ex.wait(ex_ins, ex_outs, sems)

    blk = lambda rows: pl.BlockSpec((None, ln, rows, LANE), lambda j: (j, 0, 0, 0))
    mat = lambda rows, cols: pl.BlockSpec((None, rows, cols), lambda j: (j, 0, 0))
    sds = lambda rows, cols: jax.ShapeDtypeStruct((na, rows, cols), BF16)
    wch = ln * LANE
    res = pl.pallas_call(
        body, name="ssm_expand", grid=grid,
        in_specs=[blk(LANE), blk(LANE), blk(LANE), blk(wst), blk(wst)] + ex.specs,
        out_specs=[mat(wch, wch), mat(wch, wst), mat(wch, wst), mat(wst, wch), mat(wst, wch)] + ex.specs,
        out_shape=[sds(wch, wch), sds(wch, wst), sds(wch, wst), sds(wst, wch), sds(wst, wch)] + ex.out_shape,
        scratch_shapes=ex.scratch,
        compiler_params=_cp(("arbitrary",)))(kt, wxr, wxi, wyr, wyi, *ex.arrs)
    return list(res[:5]), list(res[5:])


def _chunk_view(a):
    t, d = a.shape
    return a.reshape(t // SSM_L, SSM_L * d)


def _sg_specs(r4, d, wst):
    nblk = d // LANE
    cat = [pl.BlockSpec((r4, LANE), functools.partial(lambda j, tau: (0, tau * nblk + j), tau=tau))
           for tau in range(SSM_L)]
    plane = pl.BlockSpec((r4, wst), lambda j: (0, j))
    mat = lambda rows, cols: pl.BlockSpec((None, rows, cols), lambda j: (j, 0, 0))
    piece = pl.BlockSpec((r4, LANE), lambda j: (0, j))
    return cat, plane, mat, piece


def _lane_cat(refs):
    return jnp.concatenate([r[...] for r in refs], axis=1)


def _bdot(a, b, dims):
    return lax.dot_general(a.astype(BF16), b.astype(BF16), dims, preferred_element_type=F32)


def _ssm_core_fwd(u, ops, nb):
    tm, wxr, wxi, wyr, wyi, lam_re, lam_im = ops
    t, d = u.shape
    ln, na, wch, wst = SSM_L, tm.shape[0], tm.shape[1], wxr.shape[2]
    r4 = t // ln
    n = na * wst
    u4 = _chunk_view(u)
    cat, plane, mat, piece = _sg_specs(r4, d, wst)
    pds = jax.ShapeDtypeStruct((r4, n), F32)

    def states(*refs):
        ucat = _lane_cat(refs[:ln])
        wr_ref, wi_ref, er_ref, ei_ref = refs[ln:]
        er_ref[...] = _bdot(ucat, wr_ref[...], _NN)
        ei_ref[...] = _bdot(ucat, wi_ref[...], _NN)

    e_re, e_im = pl.pallas_call(
        states, name="ssm_states", grid=(na,), in_specs=cat + [mat(wch, wst)] * 2, out_specs=[plane, plane],
        out_shape=[pds, pds], compiler_params=_cp(("parallel",)))(*([u4] * ln), wxr, wxi)
    xp_re, xp_im = _ssm_scan(e_re, e_im, lam_re, lam_im, nb)

    def outputs(*refs):
        ucat = _lane_cat(refs[:ln])
        tm_ref, xr_ref, xi_ref, wr_ref, wi_ref = refs[ln:ln + 5]
        y = (_bdot(ucat, tm_ref[...], _NN) + _bdot(xr_ref[...], wr_ref[...], _NN)
             + _bdot(xi_ref[...], wi_ref[...], _NN))
        for tau, o_ref in enumerate(refs[ln + 5:]):
            o_ref[...] = y[:, tau * LANE:(tau + 1) * LANE].astype(o_ref.dtype)

    ys = pl.pallas_call(
        outputs, name="ssm_y", grid=(na,),
        in_specs=cat + [mat(wch, wch), plane, plane, mat(wst, wch), mat(wst, wch)], out_specs=[piece] * ln,
        out_shape=[jax.ShapeDtypeStruct((r4, d), BF16)] * ln,
        compiler_params=_cp(("parallel",)))(*([u4] * ln), tm, xp_re, xp_im, wyr, wyi)
    return jnp.concatenate(ys, axis=1).reshape(t, d), xp_re, xp_im


def _ssm_core_bwd(dy, u, xp_re, xp_im, ops, nb):
    tm, wxr, wxi, wyr, wyi, lam_re, lam_im = ops
    t, d = u.shape
    ln, na, wch, wst = SSM_L, tm.shape[0], tm.shape[1], wxr.shape[2]
    r4 = t // ln
    n = na * wst
    u4, dy4 = _chunk_view(u), _chunk_view(dy)
    cat, plane, mat, piece = _sg_specs(r4, d, wst)
    pds = jax.ShapeDtypeStruct((r4, n), F32)

    def dstates(*refs):
        dycat = _lane_cat(refs[:ln])
        wr_ref, wi_ref, dr_ref, di_ref = refs[ln:]
        dr_ref[...] = _bdot(dycat, wr_ref[...], _NT)
        di_ref[...] = _bdot(dycat, wi_ref[...], _NT)

    dxp_re, dxp_im = pl.pallas_call(
        dstates, name="ssm_dxp", grid=(na,), in_specs=cat + [mat(wst, wch)] * 2, out_specs=[plane, plane],
        out_shape=[pds, pds], compiler_params=_cp(("parallel",)))(*([dy4] * ln), wyr, wyi)
    de_re, de_im = _ssm_scan_bwd(dxp_re, dxp_im, lam_re, lam_im, nb)

    def dinputs(*refs):
        dycat = _lane_cat(refs[:ln])
        tm_ref, er_ref, ei_ref, wr_ref, wi_ref = refs[ln:ln + 5]
        du = (_bdot(dycat, tm_ref[...], _NT) + _bdot(er_ref[...], wr_ref[...], _NT)
              + _bdot(ei_ref[...], wi_ref[...], _NT))
        for tau, o_ref in enumerate(refs[ln + 5:]):
            o_ref[...] = du[:, tau * LANE:(tau + 1) * LANE].astype(o_ref.dtype)

    dus = pl.pallas_call(
        dinputs, name="ssm_du", grid=(na,),
        in_specs=cat + [mat(wch, wch), plane, plane, mat(wch, wst), mat(wch, wst)], out_specs=[piece] * ln,
        out_shape=[jax.ShapeDtypeStruct((r4, d), BF16)] * ln,
        compiler_params=_cp(("parallel",)))(*([dy4] * ln), tm, de_re, de_im, wxr, wxi)

    def doperators(*refs):
        ucat, dycat = _lane_cat(refs[:ln]), _lane_cat(refs[ln:2 * ln])
        (er_ref, ei_ref, xr_ref, xi_ref, dtm_ref, dwxr_ref, dwxi_ref, dwyr_ref, dwyi_ref,
         dlr_ref, dli_ref) = refs[2 * ln:]
        er, ei, xr, xi = er_ref[...], ei_ref[...], xr_ref[...], xi_ref[...]
        m = _ssm_masks(SSM_H, SSM_P)
        gather_h, gather_p = m["gather_h"].astype(BF16), m["gather_p"].astype(BF16)
        blk = lambda i: slice(i * LANE, (i + 1) * LANE)
        dtm = _bdot(ucat, dycat, _TN)
        for lag in range(ln):
            acc = dtm[blk(0), blk(lag)]
            for sig in range(1, ln - lag):
                acc = acc + dtm[blk(sig), blk(sig + lag)]
            dtm_ref[lag] = _split_dot(acc * m["same_t"], gather_h)
        for src, dst in ((er, dwxr_ref), (ei, dwxi_ref)):
            dwx = _bdot(ucat, src, _TN)
            for sig in range(ln):
                dst[sig] = _split_dot(dwx[blk(sig), :] * m["same_x"], gather_p)
        for src, dst in ((xr, dwyr_ref), (xi, dwyi_ref)):
            dwy = _bdot(src, dycat, _TN)
            for tau in range(ln):
                dst[tau] = _split_dot(dwy[:, blk(tau)] * m["same_y"], gather_h)
        dlr_ref[...] = jnp.sum(er * xr + ei * xi, axis=0, keepdims=True)
        dli_ref[...] = jnp.sum(ei * xr - er * xi, axis=0, keepdims=True)

    cblk = lambda rows: pl.BlockSpec((None, ln, rows, LANE), lambda j: (j, 0, 0, 0))
    cds = lambda rows: jax.ShapeDtypeStruct((na, ln, rows, LANE), F32)
    vec = pl.BlockSpec((1, wst), lambda j: (0, j))
    vds = jax.ShapeDtypeStruct((1, n), F32)
    d_compact = pl.pallas_call(
        doperators, name="ssm_dops", grid=(na,), in_specs=cat + cat + [plane] * 4,
        out_specs=[cblk(LANE), cblk(LANE), cblk(LANE), cblk(wst), cblk(wst), vec, vec],
        out_shape=[cds(LANE), cds(LANE), cds(LANE), cds(wst), cds(wst), vds, vds],
        compiler_params=_cp(("parallel",)))(*([u4] * ln), *([dy4] * ln), de_re, de_im, xp_re, xp_im)
    return jnp.concatenate(dus, axis=1).reshape(t, d), tuple(d_compact)


def _modfin_fwd(c_all, w_mod, w_fin):
    n, d = c_all.shape
    nl, _, cm = w_mod.shape
    cf = w_fin.shape[1]
    width = nl * cm + cf
    hp = lax.Precision.HIGHEST

    def body(c_ref, wm_ref, wf_ref, act_ref, out_ref):
        cv = c_ref[...]
        act = cv * _sigmoid(cv)
        act_ref[...] = act
        for i in range(nl):
            out_ref[:, i * cm:(i + 1) * cm] = jnp.dot(act, wm_ref[i], preferred_element_type=F32, precision=hp)
        out_ref[:, nl * cm:] = jnp.dot(act, wf_ref[...], preferred_element_type=F32, precision=hp)

    return pl.pallas_call(body, name="modfin_fwd",
                          out_shape=[jax.ShapeDtypeStruct((n, d), F32), jax.ShapeDtypeStruct((n, width), F32)],
                          compiler_params=_cp(None))(c_all, w_mod, w_fin)


def _modfin_bwd(c_act_t, dmod_loc, dfin_loc, dall):
    d, n = c_act_t.shape
    nl, _, cm = dmod_loc.shape
    cf = dfin_loc.shape[1]
    hp = lax.Precision.HIGHEST

    def body(ct_ref, dm_ref, df_ref, da_ref, gwm_ref, gwf_ref, gb_ref):
        ct = ct_ref[...]
        for i in range(nl):
            gwm_ref[i] = jnp.dot(ct, dm_ref[i], preferred_element_type=F32, precision=hp)
        gwf_ref[...] = jnp.dot(ct, df_ref[...], preferred_element_type=F32, precision=hp)
        gb_ref[...] = jnp.sum(da_ref[...], axis=0, keepdims=True)

    return pl.pallas_call(body, name="modfin_bwd",
                          out_shape=[jax.ShapeDtypeStruct((nl, d, cm), F32), jax.ShapeDtypeStruct((d, cf), F32),
                                     jax.ShapeDtypeStruct((1, dall.shape[1]), F32)],
                          compiler_params=_cp(None))(c_act_t, dmod_loc, dfin_loc, dall)


def _adamw(name, gparts, w, m, v):
    n, r, c = gparts.shape
    tr = _tile(r, 256)

    def body(gp_ref, w_ref, m_ref, v_ref, g_ref, d_ref, mo_ref, vo_ref):
        _adamw_step(gp_ref, w_ref, m_ref, v_ref, g_ref, d_ref, mo_ref, vo_ref)

    mat = pl.BlockSpec((tr, c), lambda i: (i, 0))
    sds = jax.ShapeDtypeStruct((r, c), F32)
    return pl.pallas_call(body, name=name, grid=(r // tr,),
                          in_specs=[pl.BlockSpec((n, tr, c), lambda i: (0, i, 0)), mat, mat, mat],
                          out_specs=[mat] * 4, out_shape=[sds] * 4,
                          compiler_params=_cp(("parallel",)))(gparts, w, m, v)


def _adamw_layers(name, gparts_l, w, m, v, ex=None):
    nl, r, c = w.shape
    n = gparts_l[0].shape[0]
    tr = _tile(r, 256)
    nt = r // tr
    grid = (nl, nt)
    ex = ex or _NoExchange()

    def body(*refs):
        w_ref, m_ref, v_ref = refs[nl:nl + 3]
        ex_ins = refs[nl + 3:nl + 3 + ex.n]
        outs = refs[nl + 3 + ex.n:nl + 7 + ex.n]
        ex_outs = refs[nl + 7 + ex.n:nl + 7 + 2 * ex.n]
        sems = refs[nl + 7 + 2 * ex.n:]
        first, last = _grid_ends(grid)

        @pl.when(first)
        def _():
            ex.start(ex_ins, ex_outs, sems)

        layer = pl.program_id(0)
        for i in range(nl):
            @pl.when(layer == i)
            def _(i=i):
                _adamw_step(refs[i], w_ref, m_ref, v_ref, *outs)

        @pl.when(last)
        def _():
            ex.wait(ex_ins, ex_outs, sems)

    def parts(i):
        return pl.BlockSpec((n, tr, c), lambda l, t: (0, jnp.where(l == i, t, jnp.where(l < i, 0, nt - 1)), 0))

    mat = pl.BlockSpec((None, tr, c), lambda l, t: (l, t, 0))
    sds = jax.ShapeDtypeStruct((nl, r, c), F32)
    res = pl.pallas_call(body, name=name, grid=grid,
                         in_specs=[parts(i) for i in range(nl)] + [mat] * 3 + ex.specs,
                         out_specs=[mat] * 4 + ex.specs, out_shape=[sds] * 4 + ex.out_shape,
                         scratch_shapes=ex.scratch,
                         compiler_params=_cp(("arbitrary", "arbitrary")))(*gparts_l, w, m, v, *ex.arrs)
    return (list(res[:4]), list(res[4:])) if ex.n else list(res)


def _adamw_step(gp_ref, w_ref, m_ref, v_ref, g_ref, d_ref, mo_ref, vo_ref):
    gsum = gp_ref[0].astype(F32)
    for j in range(1, gp_ref.shape[0]):
        gsum = gsum + gp_ref[j].astype(F32)
    mn = ADAM_B1 * m_ref[...] + (1.0 - ADAM_B1) * gsum
    vn = ADAM_B2 * v_ref[...] + (1.0 - ADAM_B2) * (gsum * gsum)
    g_ref[...] = gsum
    mo_ref[...] = mn
    vo_ref[...] = vn
    m_hat = mn * (1.0 / (1.0 - ADAM_B1 ** ADAM_STEP))
    v_hat = vn * (1.0 / (1.0 - ADAM_B2 ** ADAM_STEP))
    d_ref[...] = -ADAM_LR * (m_hat / (jnp.sqrt(v_hat) + ADAM_EPS) + ADAM_WD * w_ref[...])


def _adamw_many(name, entries):
    k = len(entries)

    def body(*refs):
        for i in range(k):
            _adamw_step(*refs[4 * i:4 * i + 4], *refs[4 * k + 4 * i:4 * k + 4 * i + 4])

    ops = [a for e in entries for a in e]
    out_shape = [jax.ShapeDtypeStruct(e[1].shape, F32) for e in entries for _ in range(4)]
    return pl.pallas_call(body, name=name, out_shape=out_shape, compiler_params=_cp(None))(*ops)


class _Exchange:
    def __init__(self, arrs, gathers):
        self.arrs = [pltpu.with_memory_space_constraint(a, pltpu.HBM) for a in arrs]
        self.gathers = list(gathers)
        self.n = len(arrs)
        self.out_shape = [pltpu.HBM(((N_DEV,) + a.shape) if g else a.shape, a.dtype)
                          for a, g in zip(arrs, self.gathers)]
        self.specs = [pl.BlockSpec(memory_space=pltpu.HBM)] * self.n
        self.scratch = [pltpu.SemaphoreType.DMA((self.n, N_DEV - 1)), pltpu.SemaphoreType.DMA((self.n, N_DEV - 1)),
                        pltpu.SemaphoreType.DMA((self.n,))]

    def _copies(self, ins, outs, sems):
        send_sems, recv_sems, local_sems = sems
        x, y, c = lax.axis_index("x"), lax.axis_index("y"), lax.axis_index("c")
        me = 4 * x + 2 * y + c
        local, sends, recvs = [], [], []
        for i in range(self.n):
            src_me = ins[i] if self.gathers[i] else ins[i].at[me]
            local.append(pltpu.make_async_copy(src_me, outs[i].at[me], local_sems.at[i]))
        for dd in range(1, N_DEV):
            px = jnp.bitwise_xor(x, dd >> 2)
            py = jnp.bitwise_xor(y, (dd >> 1) & 1)
            pc = jnp.bitwise_xor(c, dd & 1)
            pid = 4 * px + 2 * py + pc
            for i in range(self.n):
                src = ins[i] if self.gathers[i] else ins[i].at[pid]
                sems_i = dict(send_sem=send_sems.at[i, dd - 1], recv_sem=recv_sems.at[i, dd - 1],
                              device_id=(px, py, pc), device_id_type=MESH)
                sends.append(pltpu.make_async_remote_copy(src_ref=src, dst_ref=outs[i].at[me], **sems_i))
                recvs.append(pltpu.make_async_remote_copy(src_ref=src, dst_ref=outs[i].at[pid], **sems_i))
        return local, sends, recvs

    def start(self, ins, outs, sems):
        local, sends, _ = self._copies(ins, outs, sems)
        for cp in local + sends:
            cp.start()

    def wait(self, ins, outs, sems):
        local, sends, recvs = self._copies(ins, outs, sems)
        for cp in recvs:
            cp.wait_recv()
        for cp in sends:
            cp.wait_send()
        for cp in local:
            cp.wait()


class _NoExchange:
    n, arrs, specs, out_shape, scratch = 0, [], [], [], []

    def start(self, ins, outs, sems):
        pass

    def wait(self, ins, outs, sems):
        pass


def _exchange(name, arrs, gathers):
    ex = _Exchange(arrs, gathers)
    n = ex.n

    def body(*refs):
        ins, outs, sems = refs[:n], refs[n:2 * n], refs[2 * n:]
        ex.start(ins, outs, sems)
        ex.wait(ins, outs, sems)

    outs = pl.pallas_call(body, name=name, in_specs=ex.specs, out_specs=ex.specs, out_shape=ex.out_shape,
                          scratch_shapes=ex.scratch)(*ex.arrs)
    return list(outs)


def kernel(x, c, norm_mix, norm_ffn, w_mod, b_mod, w_qkv, w_o_attn, w_in_ssm, a_re, a_im, log_dt, b_re, b_im, c_re, c_im, d_skip, w_glu, b_glu, w_o_ssm, w_up, conv_w, conv_b, w_down, norm_out, w_fin, b_fin, loss_target, m_norm_mix, m_norm_ffn, m_w_mod, m_b_mod, m_w_qkv, m_w_o_attn, m_w_in_ssm, m_a_re, m_a_im, m_log_dt, m_b_re, m_b_im, m_c_re, m_c_im, m_d_skip, m_w_glu, m_b_glu, m_w_o_ssm, m_w_up, m_conv_w, m_conv_b, m_w_down, m_norm_out, m_w_fin, m_b_fin, v_norm_mix, v_norm_ffn, v_w_mod, v_b_mod, v_w_qkv, v_w_o_attn, v_w_in_ssm, v_a_re, v_a_im, v_log_dt, v_b_re, v_b_im, v_c_re, v_c_im, v_d_skip, v_w_glu, v_b_glu, v_w_o_ssm, v_w_up, v_conv_w, v_conv_b, v_w_down, v_norm_out, v_w_fin, v_b_fin):
    nb, s, d = x.shape
    t = nb * s
    n_seq = nb * N_DEV
    me = 4 * lax.axis_index("x") + 2 * lax.axis_index("y") + lax.axis_index("c")
    cm = w_mod.shape[2]
    cf = w_fin.shape[1]
    c_up = w_up.shape[2]
    r_dn = w_down.shape[1]
    g_ssm = d // SSM_H

    (c8,) = _exchange("gather_first", [c], [True])
    c_all = c8.reshape(n_seq, d)
    c_act, modloc = _modfin_fwd(c_all, w_mod, w_fin)
    ssm_params = (a_re[0], a_im[0], log_dt[0], b_re[0], b_im[0], c_re[0], c_im[0])
    compact, ops_vjp = jax.vjp(_ssm_compact, *ssm_params)
    ssm_mats, (wq8, mod8) = _ssm_expand(compact[:5], _Exchange([w_qkv[0].astype(BF16), modloc], [True, True]))
    ops = (*ssm_mats, compact[5], compact[6])
    later = _Exchange([w_o_attn[0].astype(BF16), w_in_ssm[0].astype(BF16), w_glu[0].astype(BF16),
                       w_o_ssm[0].astype(BF16), w_up[0].astype(BF16), w_up[1].astype(BF16),
                       w_down[0].astype(BF16), w_down[1].astype(BF16), conv_w, d_skip, b_glu], [True] * 11)
    half = N_DEV // 2
    cb_l = [conv_b[i].reshape(N_DEV, 1, c_up) for i in range(2)]
    mine =lax.dynamic_slice_in_dim(mod8, me * nb, nb, axis=1)
    mods = []
    for i in range(2):
        mi = mine[:, :, i * cm:(i + 1) * cm].transpose(1, 0, 2).reshape(nb, N_DEV * cm) + b_mod[i]
        mods.append([mi[:, j * d:(j + 1) * d].reshape(nb, 1, d) for j in range(6)])
    fin = mine[:, :, 2 * cm:].transpose(1, 0, 2).reshape(nb, N_DEV * cf) + b_fin
    sh_f, sc_f = fin[:, :d].reshape(nb, 1, d), fin[:, d:].reshape(nb, 1, d)

    row = lambda a: a.reshape(1, -1)
    x0 = x.reshape(t, d)

    def ffn_fwd(i, h):
        up = _mm(f"ffn{i}_up", h, wup8[i], (t // tm_, N_DEV, 1),
                 pl.BlockSpec((tm_, d), lambda a, b, k: (a, 0)), pl.BlockSpec((None, d, c_up), lambda a, b, k: (b, 0, 0)),
                 pl.BlockSpec((None, tm_, c_up), lambda a, b, k: (b, a, 0)),
                 jax.ShapeDtypeStruct((N_DEV, t, c_up), BF16), _NN, (tm_, c_up))
        act = _ffn_act_fwd(f"ffn{i}_act", up, cw_l[i], cb_l[i], nb, s)
        yf = _mm(f"ffn{i}_down", act, wd4[i], (t // tm_, 1, half),
                 pl.BlockSpec((None, tm_, c_up), lambda a, b, k: (k, a, 0)),
                 pl.BlockSpec((None, c_up, d), lambda a, b, k: (k, 0, 0)),
                 pl.BlockSpec((tm_, d), lambda a, b, k: (a, 0)), jax.ShapeDtypeStruct((t, d), F32), _NN, (tm_, d))
        return yf, (h, up, act)

    tm_ = _tile(t, 2048)
    sh1, sc1, g1, sh2, sc2, g2 = mods[0]
    h1 = _norm_mod_fwd("attn_norm", x0, row(norm_mix[0]), sh1, sc1, nb)
    cq = wq8.shape[2]
    qkv = _mm("attn_qkv", h1, wq8, (t // tm_, N_DEV, 1),
              pl.BlockSpec((tm_, d), lambda a, b, k: (a, 0)), pl.BlockSpec((None, d, cq), lambda a, b, k: (b, 0, 0)),
              pl.BlockSpec((tm_, cq), lambda a, b, k: (a, b)), jax.ShapeDtypeStruct((t, 3 * d), BF16), _NN, (tm_, cq))
    o_att, car_att, (wo8, win8, wglu8, wos8, wup8_0, wup8_1, wd8_0, wd8_1, cw8, dskip8, bglu8) = _attn_fwd(
        qkv, nb, s, d, later)
    wo = wo8.reshape(d, d)
    win = win8.reshape(d, d)
    wglu = wglu8.reshape(d, d)
    wos = wos8.reshape(d, d)
    wup8 = [wup8_0, wup8_1]
    wd4 = [wd8_0.reshape(half, 2 * r_dn, d), wd8_1.reshape(half, 2 * r_dn, d)]
    cw_l = [cw8[:, 0], cw8[:, 1]]
    dskip_f = dskip8.reshape(1, d)
    bglu_f = bglu8.reshape(1, d)
    ya = _mm_nn("attn_out", o_att, wo, F32)
    sh1b, sc1b, g1b, sh2b, sc2b, g2b = mods[1]
    x1, h2 = _res_norm_fwd("attn_res", x0, ya, g1, row(norm_ffn[0]), sh2, sc2, nb)
    yf0, ffn0_saved = ffn_fwd(0, h2)
    x2, h3 = _res_norm_fwd("ffn0_res", x1, yf0, g2, row(norm_mix[1]), sh1b, sc1b, nb)

    u = _mm_nn("ssm_in", h3, win, BF16)
    ys_core, xp_re, xp_im = _ssm_core_fwd(u, ops, nb)
    y_ssm, z_ssm = _ssm_post_fwd(ys_core, u, dskip_f)
    gl = _mm_nn("ssm_glu", z_ssm, wglu, F32)
    gg = _glu_fwd(z_ssm, gl, bglu_f)
    ys2 = _mm_nn("ssm_out", gg, wos, F32)
    x3, h4 = _res_norm_fwd("ssm_res", x2, ys2, g1b, row(norm_ffn[1]), sh2b, sc2b, nb)
    yf1, ffn1_saved = ffn_fwd(1, h4)
    x4 = _gate_add("ffn1_res", x3, yf1, g2b, nb)

    dx4, g_norm_out, dsh_f, dsc_f, loss_blk, dyf1, dg2b = _norm_mod_bwd(
        "final_norm", None, x4, row(norm_out), sh_f, sc_f, None, loss_target.reshape(t, d), nb, branch=(yf1, g2b))
    loss = lax.psum(loss_blk[0, 0], ("x", "y", "c"))

    def ffn_bwd(i, dyf, dxo, xin, sc, saved, branch):
        h, up, act = saved
        dact = _mm(f"ffn{i}_down_dx", dyf, wd4[i], (t // tm_, half, 1),
                   pl.BlockSpec((tm_, d), lambda a, b, k: (a, 0)), pl.BlockSpec((None, c_up, d), lambda a, b, k: (b, 0, 0)),
                   pl.BlockSpec((None, tm_, c_up), lambda a, b, k: (b, a, 0)),
                   jax.ShapeDtypeStruct((half, t, c_up), BF16), _NT, (tm_, c_up))
        tk = _tile(t, 1024)
        gwd = _mm(f"ffn{i}_down_dw", act, dyf, (half, 1, t // tk),
                  pl.BlockSpec((None, tk, c_up), lambda a, b, k: (a, k, 0)), pl.BlockSpec((tk, d), lambda a, b, k: (k, 0)),
                  pl.BlockSpec((None, c_up, d), lambda a, b, k: (a, 0, 0)),
                  jax.ShapeDtypeStruct((half, c_up, d), BF16), _TN, (c_up, d))
        dup, dcw, dcb = _ffn_act_bwd(f"ffn{i}_act_bwd", up, dact, cw_l[i], cb_l[i], nb, s)
        dh = _mm(f"ffn{i}_up_dx", dup, wup8[i], (t // tm_, 1, N_DEV),
                 pl.BlockSpec((None, tm_, c_up), lambda a, b, k: (k, a, 0)),
                 pl.BlockSpec((None, d, c_up), lambda a, b, k: (k, 0, 0)),
                 pl.BlockSpec((tm_, d), lambda a, b, k: (a, 0)), jax.ShapeDtypeStruct((t, d), F32), _NT, (tm_, d))
        gwup = _mm(f"ffn{i}_up_dw", h, dup, (1, N_DEV, t // tk),
                   pl.BlockSpec((tk, d), lambda a, b, k: (k, 0)), pl.BlockSpec((None, tk, c_up), lambda a, b, k: (b, k, 0)),
                   pl.BlockSpec((None, d, c_up), lambda a, b, k: (b, 0, 0)),
                   jax.ShapeDtypeStruct((N_DEV, d, c_up), BF16), _TN, (d, c_up))
        dxi, g_norm, dsh, dsc, dy_branch, dgate = _norm_mod_bwd(
            f"ffn{i}_norm_bwd", dh, xin, row(norm_ffn[i]), None, sc, dxo, None, nb, branch=branch)
        return dxi, (gwup, gwd.reshape(N_DEV, r_dn, d), dcw, dcb, g_norm, dsh, dsc), dy_branch, dgate

    dx3, (gwup1, gwd1, dcw1, dcb1, g_nffn1, dsh2b, dsc2b), dys2, dg1b = ffn_bwd(
        1, dyf1, dx4, x3, sc2b, ffn1_saved, (ys2, g1b))
    dgg = _mm_nt("ssm_out_dx", dys2, wos, F32)
    gwos = _mm_tn("ssm_out_dw", gg, dys2, BF16)
    dgl, dz1, g_bglu = _glu_bwd(dgg, z_ssm, gl, bglu_f)
    dz2 = _mm_nt("ssm_glu_dx", dgl, wglu, F32)
    gwglu = _mm_tn("ssm_glu_dw", z_ssm, dgl, BF16)
    dy_ssm, du_skip, g_dskip = _ssm_post_bwd(dz1, dz2, y_ssm, u, dskip_f)
    du_core, d_ops = _ssm_core_bwd(dy_ssm, u, xp_re, xp_im, ops, nb)
    du = _add_cast(du_core, du_skip)
    dh3 = _mm_nt("ssm_in_dx", du, win, F32)
    gwin = _mm_tn("ssm_in_dw", h3, du, BF16)
    dx2, g_nmix1, dsh1b, dsc1b, dyf0, dg2 = _norm_mod_bwd(
        "ssm_norm_bwd", dh3, x2, row(norm_mix[1]), None, sc1b, dx3, None, nb, branch=(yf0, g2))
    g_ssm_params = ops_vjp(d_ops)

    dx1, (gwup0, gwd0, dcw0, dcb0, g_nffn0, dsh2, dsc2), dya, dg1 = ffn_bwd(
        0, dyf0, dx2, x1, sc2, ffn0_saved, (ya, g1))
    do_att = _mm_nt("attn_out_dx", dya, wo, BF16)
    gwo = _mm_tn("attn_out_dw", o_att, dya, BF16)
    rows8 = lambda a: a.reshape(N_DEV, d // N_DEV, d)
    def two_d(w):
        shp = w.shape
        if len(shp) == 1:
            return (1, shp[0])
        if len(shp) == 2:
            return shp
        return (shp[0] * shp[1], math.prod(shp[2:]))

    ssm_w = [a_re, a_im, log_dt, b_re, b_im, c_re, c_im]
    ssm_partial = [g.reshape(two_d(w)) for g, w in zip(g_ssm_params, ssm_w)]
    early = _Exchange([rows8(gwo), rows8(gwin), rows8(gwglu), rows8(gwos), gwup0, gwup1, gwd0, gwd1] + ssm_partial,
                      [False] * 8 + [True] * 7)
    dq, dk, dv, early_res = _attn_bwd(qkv, car_att, do_att, nb, s, d, early)
    ro, rin, rglu, ros, rup0, rup1, rd0, rd1 = early_res[:8]
    ssm8 = early_res[8:]
    dqkv = jnp.concatenate([dq, dk, dv], axis=1)
    tk = _tile(t, 1024)
    gwq8 = _mm("attn_qkv_dw", h1, dqkv, (1, N_DEV, t // tk),
               pl.BlockSpec((tk, d), lambda a, b, k: (k, 0)),
               pl.BlockSpec((tk, cq), lambda a, b, k: (k, b)),
               pl.BlockSpec((None, d, cq), lambda a, b, k: (b, 0, 0)),
               jax.ShapeDtypeStruct((N_DEV, d, cq), BF16), _TN, (d, cq))
    dh1, (rq,) = _mm("attn_qkv_dx", dqkv, wq8, (t // tm_, 1, N_DEV),
                     pl.BlockSpec((tm_, cq), lambda a, b, k: (a, k)),
                     pl.BlockSpec((None, d, cq), lambda a, b, k: (k, 0, 0)),
                     pl.BlockSpec((tm_, d), lambda a, b, k: (a, 0)), jax.ShapeDtypeStruct((t, d), F32), _NT, (tm_, d),
                     ex=_Exchange([gwq8], [False]))
    dx0, g_nmix0, dsh1, dsc1 = _norm_mod_bwd("attn_norm_bwd", dh1, x0, row(norm_mix[0]), None, sc1, dx1, None, nb)
    grad_x = dx0.reshape(nb, s, d)

    dmod = [jnp.concatenate([a.reshape(nb, d) for a in grp], axis=1) for grp in
            ([dsh1, dsc1, dg1, dsh2, dsc2, dg2], [dsh1b, dsc1b, dg1b, dsh2b, dsc2b, dg2b])]
    dfin = jnp.concatenate([dsh_f.reshape(nb, d), dsc_f.reshape(nb, d)], axis=1)
    dmodfin = jnp.concatenate(dmod + [dfin], axis=1)
    flat = lambda a: a.reshape(1, -1)
    last_ex = _Exchange(
        [dmodfin, jnp.concatenate([g_nmix0, g_nmix1]), jnp.concatenate([g_nffn0, g_nffn1]), g_norm_out,
         jnp.concatenate([flat(dcb0), flat(dcb1)]), g_dskip, g_bglu, jnp.stack([dcw0, dcw1])], [True] * 8)
    upd = {}
    upd["w_up"], (dmf8, nmix8, nffn8, nout8, cb8, dskip_g8, bglu_g8, cw_g8) = _adamw_layers(
        "adamw_w_up", [rup0, rup1], w_up, m_w_up, v_w_up, ex=last_ex)
    shard = d // N_DEV
    dskip_g8 = lax.dynamic_slice_in_dim(dskip_g8, me * shard, shard, axis=2)
    bglu_g8 = lax.dynamic_slice_in_dim(bglu_g8, me * shard, shard, axis=2)
    cw_g8 = lax.dynamic_slice_in_dim(cw_g8, me, 1, axis=2).reshape(N_DEV, 2 * 3, c_up)

    dall = dmf8.reshape(n_seq, 14 * d)
    dmod_loc = jnp.stack([lax.dynamic_slice_in_dim(dall[:, i * 6 * d:(i + 1) * 6 * d], me * cm, cm, axis=1)
                          for i in range(2)])
    dfin_loc = lax.dynamic_slice_in_dim(dall[:, 12 * d:], me * cf, cf, axis=1)
    g_w_mod, g_w_fin, g_bias = _modfin_bwd(c_act.T, dmod_loc, dfin_loc, dall)
    g_b_mod = g_bias[0, :12 * d].reshape(2, 6 * d)
    g_b_fin = g_bias[0, 12 * d:]

    def big(name, parts, w, m, v):
        shp = w.shape
        r2 = lambda a: a.reshape(-1, shp[-1])
        res = _adamw(name, parts.reshape(parts.shape[0], -1, shp[-1]), r2(w), r2(m), r2(v))
        return [a.reshape(shp) for a in res]

    upd["w_mod"] = big("adamw_w_mod", g_w_mod[None], w_mod, m_w_mod, v_w_mod)
    upd["w_fin"] = big("adamw_w_fin", g_w_fin[None], w_fin, m_w_fin, v_w_fin)
    upd["w_qkv"] = big("adamw_w_qkv", rq, w_qkv, m_w_qkv, v_w_qkv)
    upd["w_o_attn"] = big("adamw_w_o_attn", ro, w_o_attn, m_w_o_attn, v_w_o_attn)
    upd["w_in_ssm"] = big("adamw_w_in_ssm", rin, w_in_ssm, m_w_in_ssm, v_w_in_ssm)
    upd["w_glu"] = big("adamw_w_glu", rglu, w_glu, m_w_glu, v_w_glu)
    upd["w_o_ssm"] = big("adamw_w_o_ssm", ros, w_o_ssm, m_w_o_ssm, v_w_o_ssm)
    upd["w_down"] = _adamw_layers("adamw_w_down", [rd0, rd1], w_down, m_w_down, v_w_down)

    small_names = ["norm_mix", "norm_ffn", "b_mod", "a_re", "a_im", "log_dt", "b_re", "b_im", "c_re", "c_im",
                   "d_skip", "b_glu", "conv_w", "conv_b", "norm_out", "b_fin"]
    small_g = [nmix8, nffn8, g_b_mod[None], *ssm8, dskip_g8, bglu_g8, cw_g8, cb8, nout8, g_b_fin[None]]
    small_w = [norm_mix, norm_ffn, b_mod, a_re, a_im, log_dt, b_re, b_im, c_re, c_im, d_skip, b_glu, conv_w, conv_b,
               norm_out, b_fin]
    small_m = [m_norm_mix, m_norm_ffn, m_b_mod, m_a_re, m_a_im, m_log_dt, m_b_re, m_b_im, m_c_re, m_c_im, m_d_skip,
               m_b_glu, m_conv_w, m_conv_b, m_norm_out, m_b_fin]
    small_v = [v_norm_mix, v_norm_ffn, v_b_mod, v_a_re, v_a_im, v_log_dt, v_b_re, v_b_im, v_c_re, v_c_im, v_d_skip,
               v_b_glu, v_conv_w, v_conv_b, v_norm_out, v_b_fin]
    entries = [(gp.reshape((gp.shape[0],) + two_d(w)), w.reshape(two_d(w)), m.reshape(two_d(w)), v.reshape(two_d(w)))
               for gp, w, m, v in zip(small_g, small_w, small_m, small_v)]
    res = _adamw_many("adamw_small", entries)
    for j, (nm, w) in enumerate(zip(small_names, small_w)):
        upd[nm] = [res[4 * j + k].reshape(w.shape) for k in range(4)]

    order = ["norm_mix", "norm_ffn", "w_mod", "b_mod", "w_qkv", "w_o_attn", "w_in_ssm", "a_re", "a_im", "log_dt",
             "b_re", "b_im", "c_re", "c_im", "d_skip", "w_glu", "b_glu", "w_o_ssm", "w_up", "conv_w", "conv_b",
             "w_down", "norm_out", "w_fin", "b_fin"]
    outs = [loss, grad_x]
    for k in range(4):
        outs += [upd[nm][k] for nm in order]
    return tuple(outs)
```

```python
import functools
import math

import jax
import jax.numpy as jnp
from jax import lax
from jax.experimental import pallas as pl
from jax.experimental.pallas import tpu as pltpu

F32 = jnp.float32
BF16 = jnp.bfloat16
MESH = pl.DeviceIdType.MESH

N_DEV = 8
HEAD_DIM = 64
ATT_BLK = 128
ATT_BQ = 256
ATT_UNROLL = 2
SSM_H = 16
SSM_P = 64
SSM_L = 4
EPS = 1e-6
ADAM_LR, ADAM_B1, ADAM_B2, ADAM_EPS, ADAM_WD, ADAM_STEP = 0.001, 0.9, 0.999, 1e-08, 0.01, 10
V7X_VMEM_LIMIT = 56 * 1024 * 1024
LANE = 128

_NN = (((1,), (0,)), ((), ()))
_NT = (((1,), (1,)), ((), ()))
_TN = (((0,), (0,)), ((), ()))


def _cp(sem):
    return pltpu.CompilerParams(dimension_semantics=sem, vmem_limit_bytes=V7X_VMEM_LIMIT)


def _tile(n, pref):
    if n <= pref:
        return n
    t = pref - pref % 16
    while t >= 16:
        if n % t == 0:
            return t
        t -= 16
    return n


def _mm(name, a, b, grid, a_spec, b_spec, out_spec, out_shape, dims, acc_shape, ex=None):
    nk = grid[-1]
    kax = len(grid) - 1
    ex = ex or _NoExchange()

    def body(*refs):
        a_ref, b_ref = refs[:2]
        ex_ins = refs[2:2 + ex.n]
        o_ref = refs[2 + ex.n]
        ex_outs = refs[3 + ex.n:3 + 2 * ex.n]
        acc_ref = refs[3 + 2 * ex.n]
        sems = refs[4 + 2 * ex.n:]
        first, last = _grid_ends(grid)
        k = pl.program_id(kax)

        @pl.when(first)
        def _():
            ex.start(ex_ins, ex_outs, sems)

        @pl.when(k == 0)
        def _():
            acc_ref[...] = jnp.zeros(acc_shape, F32)

        acc_ref[...] += lax.dot_general(a_ref[...].astype(BF16), b_ref[...].astype(BF16), dims,
                                        preferred_element_type=F32)

        @pl.when(k == nk - 1)
        def _():
            o_ref[...] = acc_ref[...].astype(o_ref.dtype)

        @pl.when(last)
        def _():
            ex.wait(ex_ins, ex_outs, sems)

    sem = ("arbitrary",) * len(grid) if ex.n else ("parallel",) * kax + ("arbitrary",)
    res = pl.pallas_call(
        body, name=name, grid=grid, in_specs=[a_spec, b_spec] + ex.specs, out_specs=[out_spec] + ex.specs,
        out_shape=[out_shape] + ex.out_shape, scratch_shapes=[pltpu.VMEM(acc_shape, F32)] + ex.scratch,
        compiler_params=_cp(sem))(a, b, *ex.arrs)
    return (res[0], list(res[1:])) if ex.n else res[0]


def _mm_nn(name, a, w, out_dtype):
    m, k = a.shape
    n = w.shape[1]
    tm, tn, tk = _tile(m, 512), _tile(n, 1024), _tile(k, 1024)
    return _mm(name, a, w, (m // tm, n // tn, k // tk),
               pl.BlockSpec((tm, tk), lambda i, j, kk: (i, kk)), pl.BlockSpec((tk, tn), lambda i, j, kk: (kk, j)),
               pl.BlockSpec((tm, tn), lambda i, j, kk: (i, j)), jax.ShapeDtypeStruct((m, n), out_dtype), _NN, (tm, tn))


def _mm_nt(name, a, w, out_dtype):
    m, n = a.shape
    k = w.shape[0]
    tm, tko, tn = _tile(m, 512), _tile(k, 1024), _tile(n, 1024)
    return _mm(name, a, w, (m // tm, k // tko, n // tn),
               pl.BlockSpec((tm, tn), lambda i, j, kk: (i, kk)), pl.BlockSpec((tko, tn), lambda i, j, kk: (j, kk)),
               pl.BlockSpec((tm, tko), lambda i, j, kk: (i, j)), jax.ShapeDtypeStruct((m, k), out_dtype), _NT, (tm, tko))


def _mm_tn(name, a, b, out_dtype):
    t, m = a.shape
    n = b.shape[1]
    tm, tn, tk = _tile(m, 512), _tile(n, 1024), _tile(t, 1024)
    return _mm(name, a, b, (m // tm, n // tn, t // tk),
               pl.BlockSpec((tk, tm), lambda i, j, kk: (kk, i)), pl.BlockSpec((tk, tn), lambda i, j, kk: (kk, j)),
               pl.BlockSpec((tm, tn), lambda i, j, kk: (i, j)), jax.ShapeDtypeStruct((m, n), out_dtype), _TN, (tm, tn))


def _norm_mod_fwd(name, x, g, shift, scale, nb):
    t, d = x.shape
    s = t // nb
    tr = _tile(s, 512)
    nt = s // tr

    def body(x_ref, g_ref, sh_ref, sc_ref, h_ref):
        xv = x_ref[...]
        r = lax.rsqrt(jnp.mean(xv * xv, axis=-1, keepdims=True) + EPS)
        y = xv * r * g_ref[...]
        h_ref[...] = (y * (1.0 + sc_ref[...]) + sh_ref[...]).astype(h_ref.dtype)

    row = pl.BlockSpec((tr, d), lambda b, i: (b * nt + i, 0))
    vec = pl.BlockSpec((None, 1, d), lambda b, i: (b, 0, 0))
    return pl.pallas_call(body, name=name, grid=(nb, nt),
                          in_specs=[row, pl.BlockSpec((1, d), lambda b, i: (0, 0)), vec, vec],
                          out_specs=row, out_shape=jax.ShapeDtypeStruct((t, d), BF16),
                          compiler_params=_cp(("parallel", "parallel")))(x, g, shift, scale)


def _norm_mod_bwd(name, dh, x, g, shift, scale, dres, target, nb, branch=None):
    t, d = x.shape
    s = t // nb
    tr = _tile(s, 512)
    nt = s // tr
    final = target is not None
    n_in = 5 + 2 * (branch is not None)

    def body(*refs):
        if final:
            x_ref, g_ref, sh_ref, sc_ref, tg_ref = refs[:5]
        else:
            dh_ref, x_ref, g_ref, sc_ref, dres_ref = refs[:5]
        dx_ref, dg_ref, dsh_ref, dsc_ref = refs[n_in:n_in + 4]
        if final:
            loss_ref = refs[n_in + 4]
        b, i = pl.program_id(0), pl.program_id(1)
        xv = x_ref[...]
        gv = g_ref[...]
        r = lax.rsqrt(jnp.mean(xv * xv, axis=-1, keepdims=True) + EPS)
        nrm = xv * r
        y = nrm * gv
        one_sc = 1.0 + sc_ref[...]
        if final:
            err = y * one_sc + sh_ref[...] - tg_ref[...]
            dhv = err * (1.0 / d)
        else:
            dhv = dh_ref[...].astype(F32)
        dy = dhv * one_sc
        dn = dy * gv
        dxv = r * (dn - nrm * jnp.mean(dn * nrm, axis=-1, keepdims=True))
        dtot = dxv if final else dres_ref[...] + dxv
        dx_ref[...] = dtot

        @pl.when(i == 0)
        def _():
            dsh_ref[...] = jnp.zeros_like(dsh_ref)
            dsc_ref[...] = jnp.zeros_like(dsc_ref)

        if branch is not None:
            yb_ref, gate_ref = refs[5:7]
            dyb_ref, dgate_ref = refs[-2:]
            dyb_ref[...] = (gate_ref[...] * dtot).astype(dyb_ref.dtype)

            @pl.when(i == 0)
            def _():
                dgate_ref[...] = jnp.zeros_like(dgate_ref)

            dgate_ref[...] += jnp.sum(dtot * yb_ref[...], axis=0, keepdims=True)

        @pl.when((i == 0) & (b == 0))
        def _():
            dg_ref[...] = jnp.zeros_like(dg_ref)
            if final:
                loss_ref[...] = jnp.zeros_like(loss_ref)

        dsh_ref[...] += jnp.sum(dhv, axis=0, keepdims=True)
        dsc_ref[...] += jnp.sum(dhv * y, axis=0, keepdims=True)
        dg_ref[...] += jnp.sum(dy * nrm, axis=0, keepdims=True)
        if final:
            loss_ref[...] += (0.5 / d) * jnp.sum(err * err)

    row = pl.BlockSpec((tr, d), lambda b, i: (b * nt + i, 0))
    vec = pl.BlockSpec((None, 1, d), lambda b, i: (b, 0, 0))
    gsp = pl.BlockSpec((1, d), lambda b, i: (0, 0))
    out_specs = [row, gsp, vec, vec]
    out_shape = [jax.ShapeDtypeStruct((t, d), F32), jax.ShapeDtypeStruct((1, d), F32),
                 jax.ShapeDtypeStruct((nb, 1, d), F32), jax.ShapeDtypeStruct((nb, 1, d), F32)]
    if final:
        ins, in_specs = [x, g, shift, scale, target], [row, gsp, vec, vec, row]
        out_specs.append(pl.BlockSpec((8, LANE), lambda b, i: (0, 0)))
        out_shape.append(jax.ShapeDtypeStruct((8, LANE), F32))
    else:
        ins, in_specs = [dh, x, g, scale, dres], [row, row, gsp, vec, row]
    if branch is not None:
        ins, in_specs = ins + list(branch), in_specs + [row, vec]
        out_specs += [row, vec]
        out_shape += [jax.ShapeDtypeStruct((t, d), BF16), jax.ShapeDtypeStruct((nb, 1, d), F32)]
    return pl.pallas_call(body, name=name, grid=(nb, nt), in_specs=in_specs, out_specs=out_specs,
                          out_shape=out_shape, compiler_params=_cp(("arbitrary", "arbitrary")))(*ins)


def _gate_add(name, x, y, gate, nb):
    t, d = x.shape
    s = t // nb
    tr = _tile(s, 512)
    nt = s // tr

    def body(x_ref, y_ref, g_ref, o_ref):
        o_ref[...] = x_ref[...] + g_ref[...] * y_ref[...]

    row = pl.BlockSpec((tr, d), lambda b, i: (b * nt + i, 0))
    vec = pl.BlockSpec((None, 1, d), lambda b, i: (b, 0, 0))
    return pl.pallas_call(body, name=name, grid=(nb, nt), in_specs=[row, row, vec], out_specs=row,
                          out_shape=jax.ShapeDtypeStruct((t, d), F32),
                          compiler_params=_cp(("parallel", "parallel")))(x, y, gate)


def _res_norm_fwd(name, x, y, gate, g, shift, scale, nb):
    t, d = x.shape
    s = t // nb
    tr = _tile(s, 512)
    nt = s // tr

    def body(x_ref, y_ref, gate_ref, g_ref, sh_ref, sc_ref, xo_ref, h_ref):
        xn = x_ref[...] + gate_ref[...] * y_ref[...]
        xo_ref[...] = xn
        r = lax.rsqrt(jnp.mean(xn * xn, axis=-1, keepdims=True) + EPS)
        h_ref[...] = (xn * r * g_ref[...] * (1.0 + sc_ref[...]) + sh_ref[...]).astype(h_ref.dtype)

    row = pl.BlockSpec((tr, d), lambda b, i: (b * nt + i, 0))
    vec = pl.BlockSpec((None, 1, d), lambda b, i: (b, 0, 0))
    return pl.pallas_call(body, name=name, grid=(nb, nt),
                          in_specs=[row, row, vec, pl.BlockSpec((1, d), lambda b, i: (0, 0)), vec, vec],
                          out_specs=[row, row],
                          out_shape=[jax.ShapeDtypeStruct((t, d), F32), jax.ShapeDtypeStruct((t, d), BF16)],
                          compiler_params=_cp(("parallel", "parallel")))(x, y, gate, g, shift, scale)


def _log_sigmoid(z):
    return jnp.minimum(z, 0.0) - jnp.log(1.0 + jnp.exp(-jnp.abs(z)))


def _split_dot(v, tri):
    hi = v.astype(BF16)
    lo = (v - hi.astype(F32)).astype(BF16)
    return (jnp.dot(hi, tri, preferred_element_type=F32) + jnp.dot(lo, tri, preferred_element_type=F32))


def _grid_ends(grid):
    ids = [pl.program_id(a) for a in range(len(grid))]
    first = functools.reduce(lambda u, w: u & w, [i == 0 for i in ids])
    last = functools.reduce(lambda u, w: u & w, [i == n - 1 for i, n in zip(ids, grid)])
    return first, last


def _attn_trips(nq):
    return nq * (nq + 1) // 2


def _next_trip(qi, jj, nq):
    wrap = jj >= qi
    nqi = jnp.where(wrap, jnp.minimum(qi + 1, nq - 1), qi)
    njj = jnp.where(wrap, jnp.where(qi + 1 < nq, 0, jj), jj + 1)
    return nqi, njj


def _attn_fwd(qkv, nb, s, d, ex):
    t = nb * s
    npair = d // LANE
    bk = ATT_BLK
    bq = min(ATT_BQ, s)
    nq = s // bq
    kpq = bq // bk
    nheads = LANE // HEAD_DIM
    scale = HEAD_DIM ** -0.5
    grid = (nb, npair)
    assert s // bk <= HEAD_DIM, "one carry lane per key block and head"
    assert bk == LANE, "the running sums are kept one 128-lane tile wide"
    assert kpq == ATT_UNROLL, "query block qi has exactly qi + 1 trips"

    def body(*refs):
        q_ref, k_ref, v_ref = refs[:3]
        ex_ins = refs[3:3 + ex.n]
        o_ref, car_ref = refs[3 + ex.n:5 + ex.n]
        ex_outs = refs[5 + ex.n:5 + 2 * ex.n]
        acc_s, run_s, z_s, arg_s = refs[5 + 2 * ex.n:9 + 2 * ex.n]
        sems = refs[9 + 2 * ex.n:]
        first, last = _grid_ends(grid)

        @pl.when(first)
        def _():
            ex.start(ex_ins, ex_outs, sems)

        lane = lax.broadcasted_iota(jnp.int32, (1, LANE), 1)
        row = lax.broadcasted_iota(jnp.int32, (bq, bk), 0)
        col = lax.broadcasted_iota(jnp.int32, (bq, bk), 1)
        trow = lax.broadcasted_iota(jnp.int32, (bk, bk), 0)
        tcol = lax.broadcasted_iota(jnp.int32, (bk, bk), 1)
        tri = (trow > tcol).astype(BF16)
        hms = [(lane // HEAD_DIM) == hh for hh in range(nheads)]

        def q0_of(qi):
            return pl.multiple_of(qi * bq, bq)

        def kblk_of(qi, jj, u):
            return (qi + 1) * kpq - 1 - (ATT_UNROLL * jj + u)

        def scores(qi, jj):
            q = q_ref[pl.ds(q0_of(qi), bq), :]
            qhs = [jnp.where(hm, q, jnp.zeros_like(q)) * scale for hm in hms]
            ks = [k_ref[pl.ds(pl.multiple_of(kblk_of(qi, jj, u) * bk, bk), bk), :] for u in range(ATT_UNROLL)]
            return [[lax.dot_general(qhs[hh], kj, _NT, preferred_element_type=F32) for kj in ks]
                    for hh in range(nheads)]

        def keep(zn):
            for hh in range(nheads):
                for u in range(ATT_UNROLL):
                    z_s[hh, u] = zn[hh][u]

        def exponents(qi, jj):
            q0 = q0_of(qi)
            car = car_ref[pl.ds(q0, bq), :]
            hb = bq // 2
            for hh in range(nheads):
                run = jnp.where(jj == 0, 0.0, run_s[hh])
                for u in range(ATT_UNROLL):
                    j = kblk_of(qi, jj, u)
                    sums = []
                    for part in range(2):
                        rows = slice(part * hb, (part + 1) * hb)
                        mask = (j * bk + col[rows]) < (q0 + row[rows])
                        z = z_s[hh, u, rows, :]
                        lb = _log_sigmoid(z)
                        l1 = jnp.where(mask, lb - z, 0.0)
                        arg_s[hh, u, rows, :] = jnp.where(mask, lb + (_split_dot(l1, tri) + run[rows]), -1e30)
                        sums.append(jnp.sum(l1, axis=1, keepdims=True))
                    car = jnp.where(lane == hh * HEAD_DIM + j, run, car)
                    run = run + jnp.concatenate(sums, axis=0)
                run_s[hh] = run
            car_ref[pl.ds(q0, bq), :] = car

        def weigh(qi, jj):
            q0 = q0_of(qi)
            for hh in range(nheads):
                acc = None
                for u in range(ATT_UNROLL):
                    vj = v_ref[pl.ds(pl.multiple_of(kblk_of(qi, jj, u) * bk, bk), bk), :]
                    pv = jnp.dot(jnp.exp(arg_s[hh, u]).astype(BF16), vj, preferred_element_type=F32)
                    acc = pv if acc is None else acc + pv
                acc_s[hh, pl.ds(q0, bq), :] += acc

        def step(n, carry):
            qi, jj, pqi, pjj = carry
            nqi, njj = _next_trip(qi, jj, nq)
            zn = scores(nqi, njj)
            weigh(pqi, pjj)
            exponents(qi, jj)
            keep(zn)
            return nqi, njj, qi, jj

        acc_s[...] = jnp.zeros_like(acc_s)
        run_s[...] = jnp.zeros_like(run_s)
        car_ref[...] = jnp.zeros_like(car_ref)
        arg_s[...] = jnp.full(arg_s.shape, -1e30, F32)
        zero = jnp.int32(0)
        keep(scores(zero, zero))
        _, _, lqi, ljj = lax.fori_loop(0, _attn_trips(nq), step, (zero, zero, zero, zero))
        weigh(lqi, ljj)
        out = acc_s[0]
        for hh in range(1, nheads):
            out = jnp.where(hms[hh], acc_s[hh], out)
        o_ref[...] = out.astype(o_ref.dtype)

        @pl.when(last)
        def _():
            ex.wait(ex_ins, ex_outs, sems)

    seq = lambda off: pl.BlockSpec((s, LANE), lambda b, p: (b, off + p))
    res = pl.pallas_call(
        body, name="attn_fwd", grid=grid,
        in_specs=[seq(0), seq(npair), seq(2 * npair)] + ex.specs,
        out_specs=[seq(0), seq(0)] + ex.specs,
        out_shape=[jax.ShapeDtypeStruct((t, d), BF16), jax.ShapeDtypeStruct((t, d), F32)] + ex.out_shape,
        scratch_shapes=[pltpu.VMEM((nheads, s, LANE), F32), pltpu.VMEM((nheads, bq, LANE), F32),
                        pltpu.VMEM((nheads, ATT_UNROLL, bq, bk), F32),
                        pltpu.VMEM((nheads, ATT_UNROLL, bq, bk), F32)] + ex.scratch,
        compiler_params=_cp(("arbitrary", "arbitrary")))(qkv, qkv, qkv, *ex.arrs)
    return res[0], res[1], list(res[2:])


def _attn_bwd(qkv, car, do, nb, s, d, ex):
    t = nb * s
    npair = d // LANE
    bk = ATT_BLK
    bq = min(ATT_BQ, s)
    nq = s // bq
    kpq = bq // bk
    nheads = LANE // HEAD_DIM
    scale = HEAD_DIM ** -0.5
    grid = (nb, npair)
    assert kpq == ATT_UNROLL, "query block qi has exactly qi + 1 trips"

    def body(*refs):
        q_ref, k_ref, v_ref, car_ref, do_ref = refs[:5]
        ex_ins = refs[5:5 + ex.n]
        dq_ref, dk_ref, dv_ref = refs[5 + ex.n:8 + ex.n]
        ex_outs = refs[8 + ex.n:8 + 2 * ex.n]
        dk_acc, dv_acc, dq_s, rune_s, z_s, da_s, dz_s, a_s = refs[8 + 2 * ex.n:16 + 2 * ex.n]
        sems = refs[16 + 2 * ex.n:]
        first, last = _grid_ends(grid)

        @pl.when(first)
        def _():
            ex.start(ex_ins, ex_outs, sems)

        lane = lax.broadcasted_iota(jnp.int32, (1, LANE), 1)
        row = lax.broadcasted_iota(jnp.int32, (bq, bk), 0)
        col = lax.broadcasted_iota(jnp.int32, (bq, bk), 1)
        trow = lax.broadcasted_iota(jnp.int32, (bk, bk), 0)
        tcol = lax.broadcasted_iota(jnp.int32, (bk, bk), 1)
        tri_suf = (trow > tcol).astype(BF16)
        tri_pre = (trow < tcol).astype(BF16)
        hms = [(lane // HEAD_DIM) == hh for hh in range(nheads)]

        def q0_of(qi):
            return pl.multiple_of(qi * bq, bq)

        def k0_of(jj, u):
            return pl.multiple_of((ATT_UNROLL * jj + u) * bk, bk)

        def heads_of(ref, qi, factor):
            x = ref[pl.ds(q0_of(qi), bq), :]
            return [jnp.where(hm, x, jnp.zeros_like(x)) * factor for hm in hms]

        def products(qi, jj):
            qhs, dohs = heads_of(q_ref, qi, scale), heads_of(do_ref, qi, 1.0)
            ks = [k_ref[pl.ds(k0_of(jj, u), bk), :] for u in range(ATT_UNROLL)]
            vs = [v_ref[pl.ds(k0_of(jj, u), bk), :] for u in range(ATT_UNROLL)]
            zn = [[lax.dot_general(qhs[hh], kj, _NT, preferred_element_type=F32) for kj in ks] for hh in range(nheads)]
            dn = [[lax.dot_general(dohs[hh], vj, _NT, preferred_element_type=F32) for vj in vs] for hh in range(nheads)]
            return zn, dn

        def keep(zn, dn):
            for hh in range(nheads):
                for u in range(ATT_UNROLL):
                    z_s[hh, u] = zn[hh][u]
                    da_s[hh, u] = dn[hh][u]

        def middle(qi, jj):
            q0 = q0_of(qi)
            car = car_ref[pl.ds(q0, bq), :]
            for hh in range(nheads):
                run_e = jnp.where(jj == 0, 0.0, rune_s[hh])
                for u in range(ATT_UNROLL):
                    j = ATT_UNROLL * jj + u
                    mask = (j * bk + col) < (q0 + row)
                    z = z_s[hh, u]
                    lb = _log_sigmoid(z)
                    l1u = lb - z
                    l1 = jnp.where(mask, l1u, 0.0)
                    run = jnp.sum(jnp.where(lane == hh * HEAD_DIM + j, car, 0.0), axis=1, keepdims=True)
                    a = jnp.where(mask, jnp.exp(lb + (_split_dot(l1, tri_suf) + run)), 0.0)
                    e = da_s[hh, u] * a
                    dz = e * jnp.exp(l1u) - (_split_dot(e, tri_pre) + run_e) * jnp.exp(lb)
                    dz_s[hh, u] = jnp.where(mask, dz, 0.0).astype(BF16)
                    a_s[hh, u] = a.astype(BF16)
                    run_e = run_e + jnp.sum(e, axis=1, keepdims=True)
                rune_s[hh] = run_e

        def grads(qi, jj):
            q0 = q0_of(qi)
            qhs, dohs = heads_of(q_ref, qi, scale), heads_of(do_ref, qi, 1.0)
            dqs = [None] * nheads
            for u in range(ATT_UNROLL):
                k0 = k0_of(jj, u)
                kj = k_ref[pl.ds(k0, bk), :]
                for hh in range(nheads):
                    dzb = dz_s[hh, u]
                    dqu = jnp.dot(dzb, kj, preferred_element_type=F32)
                    dqs[hh] = dqu if dqs[hh] is None else dqs[hh] + dqu
                    dkh = lax.dot_general(dzb, qhs[hh], _TN, preferred_element_type=F32)
                    dvh = lax.dot_general(a_s[hh, u], dohs[hh], _TN, preferred_element_type=F32)
                    dk_blk = dkh if hh == 0 else dk_blk + dkh
                    dv_blk = dvh if hh == 0 else dv_blk + dvh
                dk_acc[pl.ds(k0, bk), :] += dk_blk
                dv_acc[pl.ds(k0, bk), :] += dv_blk
            for hh in range(nheads):
                dq_s[hh, pl.ds(q0, bq), :] += dqs[hh]

        def step(n, carry):
            qi, jj, pqi, pjj = carry
            nqi, njj = _next_trip(qi, jj, nq)
            zn, dn = products(nqi, njj)
            grads(pqi, pjj)
            middle(qi, jj)
            keep(zn, dn)
            return nqi, njj, qi, jj

        dk_acc[...] = jnp.zeros_like(dk_acc)
        dv_acc[...] = jnp.zeros_like(dv_acc)
        dq_s[...] = jnp.zeros_like(dq_s)
        rune_s[...] = jnp.zeros_like(rune_s)
        dz_s[...] = jnp.zeros_like(dz_s)
        a_s[...] = jnp.zeros_like(a_s)
        zero = jnp.int32(0)
        keep(*products(zero, zero))
        _, _, lqi, ljj = lax.fori_loop(0, _attn_trips(nq), step, (zero, zero, zero, zero))
        grads(lqi, ljj)
        dq_out = dq_s[0]
        for hh in range(1, nheads):
            dq_out = jnp.where(hms[hh], dq_s[hh], dq_out)
        dq_ref[...] = (dq_out * scale).astype(dq_ref.dtype)
        dk_ref[...] = dk_acc[...].astype(dk_ref.dtype)
        dv_ref[...] = dv_acc[...].astype(dv_ref.dtype)

        @pl.when(last)
        def _():
            ex.wait(ex_ins, ex_outs, sems)

    seq = lambda off: pl.BlockSpec((s, LANE), lambda b, p: (b, off + p))
    sds = jax.ShapeDtypeStruct((t, d), BF16)
    res = pl.pallas_call(
        body, name="attn_bwd", grid=grid,
        in_specs=[seq(0), seq(npair), seq(2 * npair), seq(0), seq(0)] + ex.specs,
        out_specs=[seq(0), seq(0), seq(0)] + ex.specs, out_shape=[sds, sds, sds] + ex.out_shape,
        scratch_shapes=[pltpu.VMEM((s, LANE), F32), pltpu.VMEM((s, LANE), F32),
                        pltpu.VMEM((nheads, s, LANE), F32), pltpu.VMEM((nheads, bq, LANE), F32),
                        pltpu.VMEM((nheads, ATT_UNROLL, bq, bk), F32), pltpu.VMEM((nheads, ATT_UNROLL, bq, bk), F32),
                        pltpu.VMEM((nheads, ATT_UNROLL, bq, bk), BF16),
                        pltpu.VMEM((nheads, ATT_UNROLL, bq, bk), BF16)] + ex.scratch,
        compiler_params=_cp(("arbitrary", "arbitrary")))(qkv, qkv, qkv, car, do, *ex.arrs)
    return res[0], res[1], res[2], list(res[3:])


def _conv3(u_ref, w, bias, c, r0, rc):
    x = u_ref[pl.ds(r0, rc), :].astype(F32)
    p0 = pl.multiple_of(jnp.maximum(r0 - 16, 0), 16)
    prev = u_ref[pl.ds(p0, 16), :].astype(F32)
    prev = jnp.where(c > 0, prev, 0.0)
    row = lax.broadcasted_iota(jnp.int32, (rc, 1), 0)
    s1 = jnp.where(row == 0, prev[15:16, :], pltpu.roll(x, 1, 0))
    s2 = jnp.where(row == 0, prev[14:15, :], jnp.where(row == 1, prev[15:16, :], pltpu.roll(x, 2, 0)))
    cv = w[2:3, :] * x + w[1:2, :] * s1 + w[0:1, :] * s2 + bias
    return cv, x, s1, s2


def _sigmoid(x):
    return 1.0 / (1.0 + jnp.exp(-x))


def _ffn_act_fwd(name, up8, cw8, cb8, nb, s):
    _, t, c_w = up8.shape
    rc = _tile(s, 256)
    nch = s // rc
    half = N_DEV // 2

    def body(ug_ref, uv_ref, wg_ref, wv_ref, bg_ref, bv_ref, act_ref):
        wg, wv, bg, bv = wg_ref[...], wv_ref[...], bg_ref[...], bv_ref[...]

        def chunk(c, carry):
            r0 = pl.multiple_of(c * rc, rc)
            cg = _conv3(ug_ref, wg, bg, c, r0, rc)[0]
            cv = _conv3(uv_ref, wv, bv, c, r0, rc)[0]
            act_ref[pl.ds(r0, rc), :] = (cg * _sigmoid(cg) * cv).astype(act_ref.dtype)
            return carry

        lax.fori_loop(0, nch, chunk, 0)

    def slab(off):
        return pl.BlockSpec((None, s, c_w), lambda k, b: (k + off, b, 0))

    def par(rows, off):
        return pl.BlockSpec((None, rows, c_w), lambda k, b: (k + off, 0, 0))

    return pl.pallas_call(
        body, name=name, grid=(half, nb),
        in_specs=[slab(0), slab(half), par(3, 0), par(3, half), par(1, 0), par(1, half)],
        out_specs=pl.BlockSpec((None, s, c_w), lambda k, b: (k, b, 0)),
        out_shape=jax.ShapeDtypeStruct((half, t, c_w), BF16),
        compiler_params=_cp(("parallel", "parallel")))(up8, up8, cw8, cw8, cb8, cb8)


def _ffn_act_bwd(name, up8, dact4, cw8, cb8, nb, s):
    _, t, c_w = up8.shape
    rc = _tile(s, 256)
    nch = s // rc
    half = N_DEV // 2

    def body(u_ref, da_ref, w_ref, b_ref, dup_ref, dcw_ref, dcb_ref):
        w2, b2 = w_ref[...], b_ref[...]
        row = lax.broadcasted_iota(jnp.int32, (rc, 1), 0)

        @pl.when(pl.program_id(1) == 0)
        def _():
            dcw_ref[...] = jnp.zeros_like(dcw_ref)
            dcb_ref[...] = jnp.zeros_like(dcb_ref)

        def chunk(i, carry):
            c = nch - 1 - i
            r0 = pl.multiple_of(c * rc, rc)
            convs = [_conv3(u_ref.at[h], w2[h], b2[h], c, r0, rc) for h in range(2)]
            gt, vl = convs[0][0], convs[1][0]
            da = da_ref[pl.ds(r0, rc), :].astype(F32)
            sg = _sigmoid(gt)
            dcvs = [da * vl * sg * (1.0 + gt * (1.0 - sg)), da * gt * sg]
            out = []
            for h in range(2):
                n0, n1, a0, a1, a2, ab = carry[6 * h:6 * h + 6]
                dcv, (_, x, s1, s2), w = dcvs[h], convs[h], w2[h]
                t1 = jnp.where(row == rc - 1, n0, pltpu.roll(dcv, rc - 1, 0))
                t2 = jnp.where(row == rc - 2, n0, jnp.where(row == rc - 1, n1, pltpu.roll(dcv, rc - 2, 0)))
                dup = w[2:3, :] * dcv + w[1:2, :] * t1 + w[0:1, :] * t2
                dup_ref[h, pl.ds(r0, rc), :] = dup.astype(dup_ref.dtype)
                out += [dcv[0:1, :], dcv[1:2, :],
                        a0 + jnp.sum(dcv * s2, axis=0, keepdims=True), a1 + jnp.sum(dcv * s1, axis=0, keepdims=True),
                        a2 + jnp.sum(dcv * x, axis=0, keepdims=True), ab + jnp.sum(dcv, axis=0, keepdims=True)]
            return tuple(out)

        z = jnp.zeros((1, c_w), F32)
        fin = lax.fori_loop(0, nch, chunk, (z,) * 12)
        for h in range(2):
            _, _, a0, a1, a2, ab = fin[6 * h:6 * h + 6]
            dcw_ref[h, 0:1, :] += a0
            dcw_ref[h, 1:2, :] += a1
            dcw_ref[h, 2:3, :] += a2
            dcb_ref[h] += ab

    def pair(rows, per_seq):
        return pl.BlockSpec((2, None, rows, c_w), (lambda k, b: (0, k, b, 0)) if per_seq else (lambda k, b: (0, k, 0, 0)))

    four = lambda a: a.reshape((2, half) + a.shape[1:])
    dup, dcw, dcb = pl.pallas_call(
        body, name=name, grid=(half, nb),
        in_specs=[pair(s, True), pl.BlockSpec((None, s, c_w), lambda k, b: (k, b, 0)), pair(3, False), pair(1, False)],
        out_specs=[pair(s, True), pair(3, False), pair(1, False)],
        out_shape=[jax.ShapeDtypeStruct((2, half, t, c_w), BF16), jax.ShapeDtypeStruct((2, half, 3, c_w), F32),
                   jax.ShapeDtypeStruct((2, half, 1, c_w), F32)],
        compiler_params=_cp(("parallel", "arbitrary")))(four(up8), dact4, four(cw8), four(cb8))
    return dup.reshape(N_DEV, t, c_w), dcw.reshape(N_DEV, 3, c_w), dcb.reshape(N_DEV, 1, c_w)


_GELU_C0 = math.sqrt(2.0 / math.pi)
_GELU_C1 = 0.044715


def _rowwise(name, body, ins, in_kinds, out_kinds, t, d, tr_pref=512):
    tr = _tile(t, tr_pref)
    row = pl.BlockSpec((tr, d), lambda i: (i, 0))
    vec = pl.BlockSpec((1, d), lambda i: (0, 0))
    in_specs = [row if k == "row" else vec for k in in_kinds]
    out_specs = [row if k[0] == "row" else vec for k in out_kinds]
    out_shape = [jax.ShapeDtypeStruct((t, d) if k[0] == "row" else (1, d), k[1]) for k in out_kinds]
    has_acc = any(k[0] == "acc" for k in out_kinds)
    return pl.pallas_call(body, name=name, grid=(t // tr,), in_specs=in_specs, out_specs=out_specs,
                          out_shape=out_shape,
                          compiler_params=_cp(("arbitrary",) if has_acc else ("parallel",)))(*ins)


def _ssm_post_fwd(ys, u, dskip):
    t, d = ys.shape

    def body(ys_ref, u_ref, ds_ref, y_ref, z_ref):
        y = ys_ref[...].astype(F32) + ds_ref[...] * u_ref[...].astype(F32)
        y_ref[...] = y
        th = jnp.tanh(_GELU_C0 * (y + _GELU_C1 * y * y * y))
        z_ref[...] = (0.5 * y * (1.0 + th)).astype(z_ref.dtype)

    return _rowwise("ssm_post_fwd", body, [ys, u, dskip], ["row", "row", "vec"],
                    [("row", F32), ("row", BF16)], t, d)


def _glu_fwd(z, gl, bglu):
    t, d = z.shape

    def body(z_ref, gl_ref, b_ref, o_ref):
        o_ref[...] = (z_ref[...].astype(F32) * _sigmoid(gl_ref[...] + b_ref[...])).astype(o_ref.dtype)

    return _rowwise("glu_fwd", body, [z, gl, bglu], ["row", "row", "vec"], [("row", BF16)], t, d)[0]


def _glu_bwd(dgg, z, gl, bglu):
    t, d = z.shape

    def body(dg_ref, z_ref, gl_ref, b_ref, dgl_ref, dz_ref, db_ref):
        sg = _sigmoid(gl_ref[...] + b_ref[...])
        dg = dg_ref[...]
        dgl = dg * z_ref[...].astype(F32) * sg * (1.0 - sg)
        dgl_ref[...] = dgl.astype(dgl_ref.dtype)
        dz_ref[...] = dg * sg

        @pl.when(pl.program_id(0) == 0)
        def _():
            db_ref[...] = jnp.zeros_like(db_ref)

        db_ref[...] += jnp.sum(dgl, axis=0, keepdims=True)

    return _rowwise("glu_bwd", body, [dgg, z, gl, bglu], ["row", "row", "row", "vec"],
                    [("row", BF16), ("row", F32), ("acc", F32)], t, d)


def _ssm_post_bwd(dz1, dz2, y, u, dskip):
    t, d = y.shape

    def body(a_ref, b_ref, y_ref, u_ref, ds_ref, dy_ref, du_ref, dd_ref):
        yv = y_ref[...]
        inner = _GELU_C0 * (yv + _GELU_C1 * yv * yv * yv)
        th = jnp.tanh(inner)
        dgelu = 0.5 * (1.0 + th) + 0.5 * yv * (1.0 - th * th) * _GELU_C0 * (1.0 + 3.0 * _GELU_C1 * yv * yv)
        dy = (a_ref[...] + b_ref[...]) * dgelu
        dy_ref[...] = dy.astype(dy_ref.dtype)
        du_ref[...] = dy * ds_ref[...]

        @pl.when(pl.program_id(0) == 0)
        def _():
            dd_ref[...] = jnp.zeros_like(dd_ref)

        dd_ref[...] += jnp.sum(dy * u_ref[...].astype(F32), axis=0, keepdims=True)

    return _rowwise("ssm_post_bwd", body, [dz1, dz2, y, u, dskip], ["row", "row", "row", "row", "vec"],
                    [("row", BF16), ("row", F32), ("acc", F32)], t, d)


def _add_cast(a, b):
    t, d = a.shape

    def body(a_ref, b_ref, o_ref):
        o_ref[...] = (a_ref[...].astype(F32) + b_ref[...].astype(F32)).astype(o_ref.dtype)

    return _rowwise("add_cast", body, [a, b], ["row", "row"], [("row", BF16)], t, d)[0]


def _ssm_scan(e_re, e_im, lam_re, lam_im, nb):
    r, n = e_re.shape
    nc = r // nb
    cb = _tile(n, 512)

    def body(er_ref, ei_ref, lr_ref, li_ref, xr_ref, xi_ref):
        lr, li = lr_ref[...], li_ref[...]
        rid = lax.broadcasted_iota(jnp.int32, (8, 1), 0)

        def tile(i, carry):
            out = []
            for b in range(nb):
                xr, xi = carry[2 * b:2 * b + 2]
                r0 = pl.multiple_of(b * nc + i * 8, 8)
                er, ei = er_ref[pl.ds(r0, 8), :], ei_ref[pl.ds(r0, 8), :]
                outr, outi = jnp.zeros((8, cb), F32), jnp.zeros((8, cb), F32)
                for j in range(8):
                    outr = jnp.where(rid == j, xr, outr)
                    outi = jnp.where(rid == j, xi, outi)
                    xr, xi = lr * xr - li * xi + er[j:j + 1, :], li * xr + lr * xi + ei[j:j + 1, :]
                xr_ref[pl.ds(r0, 8), :] = outr
                xi_ref[pl.ds(r0, 8), :] = outi
                out += [xr, xi]
            return tuple(out)

        lax.fori_loop(0, nc // 8, tile, (jnp.zeros((1, cb), F32),) * (2 * nb))

    mat = pl.BlockSpec((r, cb), lambda j: (0, j))
    vec = pl.BlockSpec((1, cb), lambda j: (0, j))
    sds = jax.ShapeDtypeStruct((r, n), F32)
    return pl.pallas_call(body, name="ssm_scan", grid=(n // cb,), in_specs=[mat, mat, vec, vec],
                          out_specs=[mat, mat], out_shape=[sds, sds],
                          compiler_params=_cp(("parallel",)))(e_re, e_im, lam_re, lam_im)


def _ssm_scan_bwd(dxp_re, dxp_im, lam_re, lam_im, nb):
    r, n = dxp_re.shape
    nc = r // nb
    cb = _tile(n, 512)

    def body(dr_ref, di_ref, lr_ref, li_ref, er_ref, ei_ref):
        lr, li = lr_ref[...], li_ref[...]
        rid = lax.broadcasted_iota(jnp.int32, (8, 1), 0)

        def tile(i, carry):
            out = []
            for b in range(nb):
                gr, gi = carry[2 * b:2 * b + 2]
                r0 = pl.multiple_of(b * nc + (nc // 8 - 1 - i) * 8, 8)
                dr, di = dr_ref[pl.ds(r0, 8), :], di_ref[pl.ds(r0, 8), :]
                outr, outi = jnp.zeros((8, cb), F32), jnp.zeros((8, cb), F32)
                for j in range(7, -1, -1):
                    outr = jnp.where(rid == j, gr, outr)
                    outi = jnp.where(rid == j, gi, outi)
                    gr, gi = dr[j:j + 1, :] + lr * gr + li * gi, di[j:j + 1, :] + lr * gi - li * gr
                er_ref[pl.ds(r0, 8), :] = outr
                ei_ref[pl.ds(r0, 8), :] = outi
                out += [gr, gi]
            return tuple(out)

        lax.fori_loop(0, nc // 8, tile, (jnp.zeros((1, cb), F32),) * (2 * nb))

    mat = pl.BlockSpec((r, cb), lambda j: (0, j))
    vec = pl.BlockSpec((1, cb), lambda j: (0, j))
    sds = jax.ShapeDtypeStruct((r, n), F32)
    return pl.pallas_call(body, name="ssm_scan_bwd", grid=(n // cb,), in_specs=[mat, mat, vec, vec],
                          out_specs=[mat, mat], out_shape=[sds, sds],
                          compiler_params=_cp(("parallel",)))(dxp_re, dxp_im, lam_re, lam_im)


def _ssm_compact(a_re, a_im, log_dt, b_re, b_im, c_re, c_im):
    g, p = a_re.shape
    h = b_re.shape[-1]
    ln = SSM_L
    sg = LANE // h
    na = g // sg
    hp = lax.Precision.HIGHEST
    lam = lax.complex(a_re, a_im)
    ldt = lam * jnp.exp(log_dt)[:, None]
    lam_bar = jnp.exp(ldt)
    bbar = ((lam_bar - 1.0) / lam)[..., None] * lax.complex(b_re, b_im)
    cm = lax.complex(c_re, c_im)
    steps = jnp.arange(ln + 1, dtype=F32)
    pw = jnp.exp(ldt[:, None, :] * steps[None, :, None])
    kd = jnp.einsum("ghp,gdp,gpk->gdhk", cm, pw[:, :ln], bbar, precision=hp).real

    def stacked(x, rows_per, cols_per):
        x = x.reshape(na, sg, ln, rows_per, cols_per).transpose(0, 2, 1, 3, 4).reshape(na, ln, sg * rows_per, cols_per)
        return jnp.pad(x, ((0, 0), (0, 0), (0, 0), (0, LANE - cols_per)))

    wxc = (pw[:, ln - 1 - jnp.arange(ln)][:, :, :, None] * bbar[:, None]).transpose(0, 1, 3, 2)
    cpc = (cm[:, None] * pw[:, 1:ln + 1][:, :, None, :]).transpose(0, 1, 3, 2)
    lam_l = pw[:, ln]
    return (stacked(kd.transpose(0, 1, 3, 2), h, h), stacked(wxc.real, h, p), stacked(wxc.imag, h, p),
            stacked(cpc.real, p, h), stacked(-cpc.imag, p, h),
            lam_l.real.reshape(1, g * p), lam_l.imag.reshape(1, g * p))


def _ssm_masks(h, p):
    sg = LANE // h
    r128 = lax.broadcasted_iota(jnp.int32, (LANE, LANE), 0)
    c128 = lax.broadcasted_iota(jnp.int32, (LANE, LANE), 1)
    rx = lax.broadcasted_iota(jnp.int32, (LANE, sg * p), 0)
    cx = lax.broadcasted_iota(jnp.int32, (LANE, sg * p), 1)
    ry = lax.broadcasted_iota(jnp.int32, (sg * p, LANE), 0)
    cy = lax.broadcasted_iota(jnp.int32, (sg * p, LANE), 1)
    f = lambda m: m.astype(F32)
    return dict(
        spread_h=f((r128 < h) & (c128 % h == r128)),
        spread_p=f((rx < p) & (cx % p == rx)),
        gather_h=f((c128 < h) & (r128 % h == c128)),
        gather_p=f((cy < p) & (ry % p == cy)),
        same_t=f(r128 // h == c128 // h), same_x=f(rx // h == cx // p), same_y=f(ry // p == cy // h))


def _place(a, spread):
    return jnp.dot(a.astype(BF16), spread.astype(BF16), preferred_element_type=F32)


def _ssm_expand(compact, ex):
    kt, wxr, wxi, wyr, wyi = compact
    na, ln = kt.shape[:2]
    wst = wyr.shape[2]
    h, p = SSM_H, SSM_P
    grid = (na,)

    def body(*refs):
        kt_ref, wxr_ref, wxi_ref, wyr_ref, wyi_ref = refs[:5]
        ex_ins = refs[5:5 + ex.n]
        tm_ref, xr_ref, xi_ref, yr_ref, yi_ref = refs[5 + ex.n:10 + ex.n]
        ex_outs = refs[10 + ex.n:10 + 2 * ex.n]
        sems = refs[10 + 2 * ex.n:]
        first, last = _grid_ends(grid)

        @pl.when(first)
        def _():
            ex.start(ex_ins, ex_outs, sems)

        m = _ssm_masks(h, p)
        ktb = [_place(kt_ref[lag], m["spread_h"]) * m["same_t"] for lag in range(ln)]
        zero = jnp.zeros((LANE, LANE), F32)
        for sig in range(ln):
            rows = slice(sig * LANE, (sig + 1) * LANE)
            tm_ref[rows, :] = jnp.concatenate([ktb[tau - sig] if tau >= sig else zero for tau in range(ln)],
                                              axis=1).astype(tm_ref.dtype)
            xr_ref[rows, :] = (_place(wxr_ref[sig], m["spread_p"]) * m["same_x"]).astype(xr_ref.dtype)
            xi_ref[rows, :] = (_place(wxi_ref[sig], m["spread_p"]) * m["same_x"]).astype(xi_ref.dtype)
        for tau in range(ln):
            cols = slice(tau * LANE, (tau + 1) * LANE)
            yr_ref[:, cols] = (_place(wyr_ref[tau], m["spread_h"]) * m["same_y"]).astype(yr_ref.dtype)
            yi_ref[:, cols] = (_place(wyi_ref[tau], m["spread_h"]) * m["same_y"]).astype(yi_ref.dtype)

        @pl.when(last)
        def _():
            ex.wait(ex_ins, ex_outs, sems)

    blk = lambda rows: pl.BlockSpec((None, ln, rows, LANE), lambda j: (j, 0, 0, 0))
    mat = lambda rows, cols: pl.BlockSpec((None, rows, cols), lambda j: (j, 0, 0))
    sds = lambda rows, cols: jax.ShapeDtypeStruct((na, rows, cols), BF16)
    wch = ln * LANE
    res = pl.pallas_call(
        body, name="ssm_expand", grid=grid,
        in_specs=[blk(LANE), blk(LANE), blk(LANE), blk(wst), blk(wst)] + ex.specs,
        out_specs=[mat(wch, wch), mat(wch, wst), mat(wch, wst), mat(wst, wch), mat(wst, wch)] + ex.specs,
        out_shape=[sds(wch, wch), sds(wch, wst), sds(wch, wst), sds(wst, wch), sds(wst, wch)] + ex.out_shape,
        scratch_shapes=ex.scratch,
        compiler_params=_cp(("arbitrary",)))(kt, wxr, wxi, wyr, wyi, *ex.arrs)
    return list(res[:5]), list(res[5:])


def _chunk_view(a):
    t, d = a.shape
    return a.reshape(t // SSM_L, SSM_L * d)


def _sg_specs(r4, d, wst):
    nblk = d // LANE
    cat = [pl.BlockSpec((r4, LANE), functools.partial(lambda j, tau: (0, tau * nblk + j), tau=tau))
           for tau in range(SSM_L)]
    plane = pl.BlockSpec((r4, wst), lambda j: (0, j))
    mat = lambda rows, cols: pl.BlockSpec((None, rows, cols), lambda j: (j, 0, 0))
    piece = pl.BlockSpec((r4, LANE), lambda j: (0, j))
    return cat, plane, mat, piece


def _lane_cat(refs):
    return jnp.concatenate([r[...] for r in refs], axis=1)


def _bdot(a, b, dims):
    return lax.dot_general(a.astype(BF16), b.astype(BF16), dims, preferred_element_type=F32)


def _ssm_core_fwd(u, ops, nb):
    tm, wxr, wxi, wyr, wyi, lam_re, lam_im = ops
    t, d = u.shape
    ln, na, wch, wst = SSM_L, tm.shape[0], tm.shape[1], wxr.shape[2]
    r4 = t // ln
    n = na * wst
    u4 = _chunk_view(u)
    cat, plane, mat, piece = _sg_specs(r4, d, wst)
    pds = jax.ShapeDtypeStruct((r4, n), F32)

    def states(*refs):
        ucat = _lane_cat(refs[:ln])
        wr_ref, wi_ref, er_ref, ei_ref = refs[ln:]
        er_ref[...] = _bdot(ucat, wr_ref[...], _NN)
        ei_ref[...] = _bdot(ucat, wi_ref[...], _NN)

    e_re, e_im = pl.pallas_call(
        states, name="ssm_states", grid=(na,), in_specs=cat + [mat(wch, wst)] * 2, out_specs=[plane, plane],
        out_shape=[pds, pds], compiler_params=_cp(("parallel",)))(*([u4] * ln), wxr, wxi)
    xp_re, xp_im = _ssm_scan(e_re, e_im, lam_re, lam_im, nb)

    def outputs(*refs):
        ucat = _lane_cat(refs[:ln])
        tm_ref, xr_ref, xi_ref, wr_ref, wi_ref = refs[ln:ln + 5]
        y = (_bdot(ucat, tm_ref[...], _NN) + _bdot(xr_ref[...], wr_ref[...], _NN)
             + _bdot(xi_ref[...], wi_ref[...], _NN))
        for tau, o_ref in enumerate(refs[ln + 5:]):
            o_ref[...] = y[:, tau * LANE:(tau + 1) * LANE].astype(o_ref.dtype)

    ys = pl.pallas_call(
        outputs, name="ssm_y", grid=(na,),
        in_specs=cat + [mat(wch, wch), plane, plane, mat(wst, wch), mat(wst, wch)], out_specs=[piece] * ln,
        out_shape=[jax.ShapeDtypeStruct((r4, d), BF16)] * ln,
        compiler_params=_cp(("parallel",)))(*([u4] * ln), tm, xp_re, xp_im, wyr, wyi)
    return jnp.concatenate(ys, axis=1).reshape(t, d), xp_re, xp_im


def _ssm_core_bwd(dy, u, xp_re, xp_im, ops, nb):
    tm, wxr, wxi, wyr, wyi, lam_re, lam_im = ops
    t, d = u.shape
    ln, na, wch, wst = SSM_L, tm.shape[0], tm.shape[1], wxr.shape[2]
    r4 = t // ln
    n = na * wst
    u4, dy4 = _chunk_view(u), _chunk_view(dy)
    cat, plane, mat, piece = _sg_specs(r4, d, wst)
    pds = jax.ShapeDtypeStruct((r4, n), F32)

    def dstates(*refs):
        dycat = _lane_cat(refs[:ln])
        wr_ref, wi_ref, dr_ref, di_ref = refs[ln:]
        dr_ref[...] = _bdot(dycat, wr_ref[...], _NT)
        di_ref[...] = _bdot(dycat, wi_ref[...], _NT)

    dxp_re, dxp_im = pl.pallas_call(
        dstates, name="ssm_dxp", grid=(na,), in_specs=cat + [mat(wst, wch)] * 2, out_specs=[plane, plane],
        out_shape=[pds, pds], compiler_params=_cp(("parallel",)))(*([dy4] * ln), wyr, wyi)
    de_re, de_im = _ssm_scan_bwd(dxp_re, dxp_im, lam_re, lam_im, nb)

    def dinputs(*refs):
        dycat = _lane_cat(refs[:ln])
        tm_ref, er_ref, ei_ref, wr_ref, wi_ref = refs[ln:ln + 5]
        du = (_bdot(dycat, tm_ref[...], _NT) + _bdot(er_ref[...], wr_ref[...], _NT)
              + _bdot(ei_ref[...], wi_ref[...], _NT))
        for tau, o_ref in enumerate(refs[ln + 5:]):
            o_ref[...] = du[:, tau * LANE:(tau + 1) * LANE].astype(o_ref.dtype)

    dus = pl.pallas_call(
        dinputs, name="ssm_du", grid=(na,),
        in_specs=cat + [mat(wch, wch), plane, plane, mat(wch, wst), mat(wch, wst)], out_specs=[piece] * ln,
        out_shape=[jax.ShapeDtypeStruct((r4, d), BF16)] * ln,
        compiler_params=_cp(("parallel",)))(*([dy4] * ln), tm, de_re, de_im, wxr, wxi)

    def doperators(*refs):
        ucat, dycat = _lane_cat(refs[:ln]), _lane_cat(refs[ln:2 * ln])
        (er_ref, ei_ref, xr_ref, xi_ref, dtm_ref, dwxr_ref, dwxi_ref, dwyr_ref, dwyi_ref,
         dlr_ref, dli_ref) = refs[2 * ln:]
        er, ei, xr, xi = er_ref[...], ei_ref[...], xr_ref[...], xi_ref[...]
        m = _ssm_masks(SSM_H, SSM_P)
        gather_h, gather_p = m["gather_h"].astype(BF16), m["gather_p"].astype(BF16)
        blk = lambda i: slice(i * LANE, (i + 1) * LANE)
        dtm = _bdot(ucat, dycat, _TN)
        for lag in range(ln):
            acc = dtm[blk(0), blk(lag)]
            for sig in range(1, ln - lag):
                acc = acc + dtm[blk(sig), blk(sig + lag)]
            dtm_ref[lag] = _split_dot(acc * m["same_t"], gather_h)
        for src, dst in ((er, dwxr_ref), (ei, dwxi_ref)):
            dwx = _bdot(ucat, src, _TN)
            for sig in range(ln):
                dst[sig] = _split_dot(dwx[blk(sig), :] * m["same_x"], gather_p)
        for src, dst in ((xr, dwyr_ref), (xi, dwyi_ref)):
            dwy = _bdot(src, dycat, _TN)
            for tau in range(ln):
                dst[tau] = _split_dot(dwy[:, blk(tau)] * m["same_y"], gather_h)
        dlr_ref[...] = jnp.sum(er * xr + ei * xi, axis=0, keepdims=True)
        dli_ref[...] = jnp.sum(ei * xr - er * xi, axis=0, keepdims=True)

    cblk = lambda rows: pl.BlockSpec((None, ln, rows, LANE), lambda j: (j, 0, 0, 0))
    cds = lambda rows: jax.ShapeDtypeStruct((na, ln, rows, LANE), F32)
    vec = pl.BlockSpec((1, wst), lambda j: (0, j))
    vds = jax.ShapeDtypeStruct((1, n), F32)
    d_compact = pl.pallas_call(
        doperators, name="ssm_dops", grid=(na,), in_specs=cat + cat + [plane] * 4,
        out_specs=[cblk(LANE), cblk(LANE), cblk(LANE), cblk(wst), cblk(wst), vec, vec],
        out_shape=[cds(LANE), cds(LANE), cds(LANE), cds(wst), cds(wst), vds, vds],
        compiler_params=_cp(("parallel",)))(*([u4] * ln), *([dy4] * ln), de_re, de_im, xp_re, xp_im)
    return jnp.concatenate(dus, axis=1).reshape(t, d), tuple(d_compact)


def _modfin_fwd(c_all, w_mod, w_fin):
    n, d = c_all.shape
    nl, _, cm = w_mod.shape
    cf = w_fin.shape[1]
    width = nl * cm + cf
    hp = lax.Precision.HIGHEST

    def body(c_ref, wm_ref, wf_ref, act_ref, out_ref):
        cv = c_ref[...]
        act = cv * _sigmoid(cv)
        act_ref[...] = act
        for i in range(nl):
            out_ref[:, i * cm:(i + 1) * cm] = jnp.dot(act, wm_ref[i], preferred_element_type=F32, precision=hp)
        out_ref[:, nl * cm:] = jnp.dot(act, wf_ref[...], preferred_element_type=F32, precision=hp)

    return pl.pallas_call(body, name="modfin_fwd",
                          out_shape=[jax.ShapeDtypeStruct((n, d), F32), jax.ShapeDtypeStruct((n, width), F32)],
                          compiler_params=_cp(None))(c_all, w_mod, w_fin)


def _modfin_bwd(c_act_t, dmod_loc, dfin_loc, dall):
    d, n = c_act_t.shape
    nl, _, cm = dmod_loc.shape
    cf = dfin_loc.shape[1]
    hp = lax.Precision.HIGHEST

    def body(ct_ref, dm_ref, df_ref, da_ref, gwm_ref, gwf_ref, gb_ref):
        ct = ct_ref[...]
        for i in range(nl):
            gwm_ref[i] = jnp.dot(ct, dm_ref[i], preferred_element_type=F32, precision=hp)
        gwf_ref[...] = jnp.dot(ct, df_ref[...], preferred_element_type=F32, precision=hp)
        gb_ref[...] = jnp.sum(da_ref[...], axis=0, keepdims=True)

    return pl.pallas_call(body, name="modfin_bwd",
                          out_shape=[jax.ShapeDtypeStruct((nl, d, cm), F32), jax.ShapeDtypeStruct((d, cf), F32),
                                     jax.ShapeDtypeStruct((1, dall.shape[1]), F32)],
                          compiler_params=_cp(None))(c_act_t, dmod_loc, dfin_loc, dall)


def _adamw(name, gparts, w, m, v):
    n, r, c = gparts.shape
    tr = _tile(r, 256)

    def body(gp_ref, w_ref, m_ref, v_ref, g_ref, d_ref, mo_ref, vo_ref):
        _adamw_step(gp_ref, w_ref, m_ref, v_ref, g_ref, d_ref, mo_ref, vo_ref)

    mat = pl.BlockSpec((tr, c), lambda i: (i, 0))
    sds = jax.ShapeDtypeStruct((r, c), F32)
    return pl.pallas_call(body, name=name, grid=(r // tr,),
                          in_specs=[pl.BlockSpec((n, tr, c), lambda i: (0, i, 0)), mat, mat, mat],
                          out_specs=[mat] * 4, out_shape=[sds] * 4,
                          compiler_params=_cp(("parallel",)))(gparts, w, m, v)


def _adamw_layers(name, gparts_l, w, m, v, ex=None):
    nl, r, c = w.shape
    n = gparts_l[0].shape[0]
    tr = _tile(r, 256)
    nt = r // tr
    grid = (nl, nt)
    ex = ex or _NoExchange()

    def body(*refs):
        w_ref, m_ref, v_ref = refs[nl:nl + 3]
        ex_ins = refs[nl + 3:nl + 3 + ex.n]
        outs = refs[nl + 3 + ex.n:nl + 7 + ex.n]
        ex_outs = refs[nl + 7 + ex.n:nl + 7 + 2 * ex.n]
        sems = refs[nl + 7 + 2 * ex.n:]
        first, last = _grid_ends(grid)

        @pl.when(first)
        def _():
            ex.start(ex_ins, ex_outs, sems)

        layer = pl.program_id(0)
        for i in range(nl):
            @pl.when(layer == i)
            def _(i=i):
                _adamw_step(refs[i], w_ref, m_ref, v_ref, *outs)

        @pl.when(last)
        def _():
            ex.wait(ex_ins, ex_outs, sems)

    def parts(i):
        return pl.BlockSpec((n, tr, c), lambda l, t: (0, jnp.where(l == i, t, jnp.where(l < i, 0, nt - 1)), 0))

    mat = pl.BlockSpec((None, tr, c), lambda l, t: (l, t, 0))
    sds = jax.ShapeDtypeStruct((nl, r, c), F32)
    res = pl.pallas_call(body, name=name, grid=grid,
                         in_specs=[parts(i) for i in range(nl)] + [mat] * 3 + ex.specs,
                         out_specs=[mat] * 4 + ex.specs, out_shape=[sds] * 4 + ex.out_shape,
                         scratch_shapes=ex.scratch,
                         compiler_params=_cp(("arbitrary", "arbitrary")))(*gparts_l, w, m, v, *ex.arrs)
    return (list(res[:4]), list(res[4:])) if ex.n else list(res)


def _adamw_step(gp_ref, w_ref, m_ref, v_ref, g_ref, d_ref, mo_ref, vo_ref):
    gsum = gp_ref[0].astype(F32)
    for j in range(1, gp_ref.shape[0]):
        gsum = gsum + gp_ref[j].astype(F32)
    mn = ADAM_B1 * m_ref[...] + (1.0 - ADAM_B1) * gsum
    vn = ADAM_B2 * v_ref[...] + (1.0 - ADAM_B2) * (gsum * gsum)
    g_ref[...] = gsum
    mo_ref[...] = mn
    vo_ref[...] = vn
    m_hat = mn * (1.0 / (1.0 - ADAM_B1 ** ADAM_STEP))
    v_hat = vn * (1.0 / (1.0 - ADAM_B2 ** ADAM_STEP))
    d_ref[...] = -ADAM_LR * (m_hat / (jnp.sqrt(v_hat) + ADAM_EPS) + ADAM_WD * w_ref[...])


def _adamw_many(name, entries):
    k = len(entries)

    def body(*refs):
        for i in range(k):
            _adamw_step(*refs[4 * i:4 * i + 4], *refs[4 * k + 4 * i:4 * k + 4 * i + 4])

    ops = [a for e in entries for a in e]
    out_shape = [jax.ShapeDtypeStruct(e[1].shape, F32) for e in entries for _ in range(4)]
    return pl.pallas_call(body, name=name, out_shape=out_shape, compiler_params=_cp(None))(*ops)


class _Exchange:
    def __init__(self, arrs, gathers):
        self.arrs = [pltpu.with_memory_space_constraint(a, pltpu.HBM) for a in arrs]
        self.gathers = list(gathers)
        self.n = len(arrs)
        self.out_shape = [pltpu.HBM(((N_DEV,) + a.shape) if g else a.shape, a.dtype)
                          for a, g in zip(arrs, self.gathers)]
        self.specs = [pl.BlockSpec(memory_space=pltpu.HBM)] * self.n
        self.scratch = [pltpu.SemaphoreType.DMA((self.n, N_DEV - 1)), pltpu.SemaphoreType.DMA((self.n, N_DEV - 1)),
                        pltpu.SemaphoreType.DMA((self.n,))]

    def _copies(self, ins, outs, sems):
        send_sems, recv_sems, local_sems = sems
        x, y, c = lax.axis_index("x"), lax.axis_index("y"), lax.axis_index("c")
        me = 4 * x + 2 * y + c
        local, sends, recvs = [], [], []
        for i in range(self.n):
            src_me = ins[i] if self.gathers[i] else ins[i].at[me]
            local.append(pltpu.make_async_copy(src_me, outs[i].at[me], local_sems.at[i]))
        for dd in range(1, N_DEV):
            px = jnp.bitwise_xor(x, dd >> 2)
            py = jnp.bitwise_xor(y, (dd >> 1) & 1)
            pc = jnp.bitwise_xor(c, dd & 1)
            pid = 4 * px + 2 * py + pc
            for i in range(self.n):
                src = ins[i] if self.gathers[i] else ins[i].at[pid]
                sems_i = dict(send_sem=send_sems.at[i, dd - 1], recv_sem=recv_sems.at[i, dd - 1],
                              device_id=(px, py, pc), device_id_type=MESH)
                sends.append(pltpu.make_async_remote_copy(src_ref=src, dst_ref=outs[i].at[me], **sems_i))
                recvs.append(pltpu.make_async_remote_copy(src_ref=src, dst_ref=outs[i].at[pid], **sems_i))
        return local, sends, recvs

    def start(self, ins, outs, sems):
        local, sends, _ = self._copies(ins, outs, sems)
        for cp in local + sends:
            cp.start()

    def wait(self, ins, outs, sems):
        local, sends, recvs = self._copies(ins, outs, sems)
        for cp in recvs:
            cp.wait_recv()
        for cp in sends:
            cp.wait_send()
        for cp in local:
            cp.wait()


class _NoExchange:
    n, arrs, specs, out_shape, scratch = 0, [], [], [], []

    def start(self, ins, outs, sems):
        pass

    def wait(self, ins, outs, sems):
        pass


def _exchange(name, arrs, gathers):
    ex = _Exchange(arrs, gathers)
    n = ex.n

    def body(*refs):
        ins, outs, sems = refs[:n], refs[n:2 * n], refs[2 * n:]
        ex.start(ins, outs, sems)
        ex.wait(ins, outs, sems)

    outs = pl.pallas_call(body, name=name, in_specs=ex.specs, out_specs=ex.specs, out_shape=ex.out_shape,
                          scratch_shapes=ex.scratch)(*ex.arrs)
    return list(outs)


def kernel(x, c, norm_mix, norm_ffn, w_mod, b_mod, w_qkv, w_o_attn, w_in_ssm, a_re, a_im, log_dt, b_re, b_im, c_re, c_im, d_skip, w_glu, b_glu, w_o_ssm, w_up, conv_w, conv_b, w_down, norm_out, w_fin, b_fin, loss_target, m_norm_mix, m_norm_ffn, m_w_mod, m_b_mod, m_w_qkv, m_w_o_attn, m_w_in_ssm, m_a_re, m_a_im, m_log_dt, m_b_re, m_b_im, m_c_re, m_c_im, m_d_skip, m_w_glu, m_b_glu, m_w_o_ssm, m_w_up, m_conv_w, m_conv_b, m_w_down, m_norm_out, m_w_fin, m_b_fin, v_norm_mix, v_norm_ffn, v_w_mod, v_b_mod, v_w_qkv, v_w_o_attn, v_w_in_ssm, v_a_re, v_a_im, v_log_dt, v_b_re, v_b_im, v_c_re, v_c_im, v_d_skip, v_w_glu, v_b_glu, v_w_o_ssm, v_w_up, v_conv_w, v_conv_b, v_w_down, v_norm_out, v_w_fin, v_b_fin):
    nb, s, d = x.shape
    t = nb * s
    n_seq = nb * N_DEV
    me = 4 * lax.axis_index("x") + 2 * lax.axis_index("y") + lax.axis_index("c")
    cm = w_mod.shape[2]
    cf = w_fin.shape[1]
    c_up = w_up.shape[2]
    r_dn = w_down.shape[1]
    g_ssm = d // SSM_H

    (c8,) = _exchange("gather_first", [c], [True])
    ssm_params = (a_re[0], a_im[0], log_dt[0], b_re[0], b_im[0], c_re[0], c_im[0])
    compact, ops_vjp = jax.vjp(_ssm_compact, *ssm_params)
    ssm_mats, (wq8,) = _ssm_expand(compact[:5], _Exchange([w_qkv[0].astype(BF16)], [True]))
    ops = (*ssm_mats, compact[5], compact[6])
    later = _Exchange([w_o_attn[0].astype(BF16), w_in_ssm[0].astype(BF16), w_glu[0].astype(BF16),
                       w_o_ssm[0].astype(BF16), w_up[0].astype(BF16), w_up[1].astype(BF16),
                       w_down[0].astype(BF16), w_down[1].astype(BF16), conv_w, d_skip, b_glu], [True] * 11)
    half = N_DEV // 2
    cb_l = [conv_b[i].reshape(N_DEV, 1, c_up) for i in range(2)]
    c_all = c8.reshape(n_seq, d)

    c_act, modloc = _modfin_fwd(c_all, w_mod, w_fin)
    (mod8,) = _exchange("gather_mod", [modloc], [True])
    mine = lax.dynamic_slice_in_dim(mod8, me * nb, nb, axis=1)
    mods = []
    for i in range(2):
        mi = mine[:, :, i * cm:(i + 1) * cm].transpose(1, 0, 2).reshape(nb, N_DEV * cm) + b_mod[i]
        mods.append([mi[:, j * d:(j + 1) * d].reshape(nb, 1, d) for j in range(6)])
    fin = mine[:, :, 2 * cm:].transpose(1, 0, 2).reshape(nb, N_DEV * cf) + b_fin
    sh_f, sc_f = fin[:, :d].reshape(nb, 1, d), fin[:, d:].reshape(nb, 1, d)

    row = lambda a: a.reshape(1, -1)
    x0 = x.reshape(t, d)

    def ffn_fwd(i, h):
        up = _mm(f"ffn{i}_up", h, wup8[i], (t // tm_, N_DEV, 1),
                 pl.BlockSpec((tm_, d), lambda a, b, k: (a, 0)), pl.BlockSpec((None, d, c_up), lambda a, b, k: (b, 0, 0)),
                 pl.BlockSpec((None, tm_, c_up), lambda a, b, k: (b, a, 0)),
                 jax.ShapeDtypeStruct((N_DEV, t, c_up), BF16), _NN, (tm_, c_up))
        act = _ffn_act_fwd(f"ffn{i}_act", up, cw_l[i], cb_l[i], nb, s)
        yf = _mm(f"ffn{i}_down", act, wd4[i], (t // tm_, 1, half),
                 pl.BlockSpec((None, tm_, c_up), lambda a, b, k: (k, a, 0)),
                 pl.BlockSpec((None, c_up, d), lambda a, b, k: (k, 0, 0)),
                 pl.BlockSpec((tm_, d), lambda a, b, k: (a, 0)), jax.ShapeDtypeStruct((t, d), F32), _NN, (tm_, d))
        return yf, (h, up, act)

    tm_ = _tile(t, 2048)
    sh1, sc1, g1, sh2, sc2, g2 = mods[0]
    h1 = _norm_mod_fwd("attn_norm", x0, row(norm_mix[0]), sh1, sc1, nb)
    cq = wq8.shape[2]
    qkv = _mm("attn_qkv", h1, wq8, (t // tm_, N_DEV, 1),
              pl.BlockSpec((tm_, d), lambda a, b, k: (a, 0)), pl.BlockSpec((None, d, cq), lambda a, b, k: (b, 0, 0)),
              pl.BlockSpec((tm_, cq), lambda a, b, k: (a, b)), jax.ShapeDtypeStruct((t, 3 * d), BF16), _NN, (tm_, cq))
    o_att, car_att, (wo8, win8, wglu8, wos8, wup8_0, wup8_1, wd8_0, wd8_1, cw8, dskip8, bglu8) = _attn_fwd(
        qkv, nb, s, d, later)
    wo = wo8.reshape(d, d)
    win = win8.reshape(d, d)
    wglu = wglu8.reshape(d, d)
    wos = wos8.reshape(d, d)
    wup8 = [wup8_0, wup8_1]
    wd4 = [wd8_0.reshape(half, 2 * r_dn, d), wd8_1.reshape(half, 2 * r_dn, d)]
    cw_l = [cw8[:, 0], cw8[:, 1]]
    dskip_f = dskip8.reshape(1, d)
    bglu_f = bglu8.reshape(1, d)
    ya = _mm_nn("attn_out", o_att, wo, F32)
    sh1b, sc1b, g1b, sh2b, sc2b, g2b = mods[1]
    x1, h2 = _res_norm_fwd("attn_res", x0, ya, g1, row(norm_ffn[0]), sh2, sc2, nb)
    yf0, ffn0_saved = ffn_fwd(0, h2)
    x2, h3 = _res_norm_fwd("ffn0_res", x1, yf0, g2, row(norm_mix[1]), sh1b, sc1b, nb)

    u = _mm_nn("ssm_in", h3, win, BF16)
    ys_core, xp_re, xp_im = _ssm_core_fwd(u, ops, nb)
    y_ssm, z_ssm = _ssm_post_fwd(ys_core, u, dskip_f)
    gl = _mm_nn("ssm_glu", z_ssm, wglu, F32)
    gg = _glu_fwd(z_ssm, gl, bglu_f)
    ys2 = _mm_nn("ssm_out", gg, wos, F32)
    x3, h4 = _res_norm_fwd("ssm_res", x2, ys2, g1b, row(norm_ffn[1]), sh2b, sc2b, nb)
    yf1, ffn1_saved = ffn_fwd(1, h4)
    x4 = _gate_add("ffn1_res", x3, yf1, g2b, nb)

    dx4, g_norm_out, dsh_f, dsc_f, loss_blk, dyf1, dg2b = _norm_mod_bwd(
        "final_norm", None, x4, row(norm_out), sh_f, sc_f, None, loss_target.reshape(t, d), nb, branch=(yf1, g2b))
    loss = lax.psum(loss_blk[0, 0], ("x", "y", "c"))

    def ffn_bwd(i, dyf, dxo, xin, sc, saved, branch):
        h, up, act = saved
        dact = _mm(f"ffn{i}_down_dx", dyf, wd4[i], (t // tm_, half, 1),
                   pl.BlockSpec((tm_, d), lambda a, b, k: (a, 0)), pl.BlockSpec((None, c_up, d), lambda a, b, k: (b, 0, 0)),
                   pl.BlockSpec((None, tm_, c_up), lambda a, b, k: (b, a, 0)),
                   jax.ShapeDtypeStruct((half, t, c_up), BF16), _NT, (tm_, c_up))
        tk = _tile(t, 1024)
        gwd = _mm(f"ffn{i}_down_dw", act, dyf, (half, 1, t // tk),
                  pl.BlockSpec((None, tk, c_up), lambda a, b, k: (a, k, 0)), pl.BlockSpec((tk, d), lambda a, b, k: (k, 0)),
                  pl.BlockSpec((None, c_up, d), lambda a, b, k: (a, 0, 0)),
                  jax.ShapeDtypeStruct((half, c_up, d), BF16), _TN, (c_up, d))
        dup, dcw, dcb = _ffn_act_bwd(f"ffn{i}_act_bwd", up, dact, cw_l[i], cb_l[i], nb, s)
        dh = _mm(f"ffn{i}_up_dx", dup, wup8[i], (t // tm_, 1, N_DEV),
                 pl.BlockSpec((None, tm_, c_up), lambda a, b, k: (k, a, 0)),
                 pl.BlockSpec((None, d, c_up), lambda a, b, k: (k, 0, 0)),
                 pl.BlockSpec((tm_, d), lambda a, b, k: (a, 0)), jax.ShapeDtypeStruct((t, d), F32), _NT, (tm_, d))
        gwup = _mm(f"ffn{i}_up_dw", h, dup, (1, N_DEV, t // tk),
                   pl.BlockSpec((tk, d), lambda a, b, k: (k, 0)), pl.BlockSpec((None, tk, c_up), lambda a, b, k: (b, k, 0)),
                   pl.BlockSpec((None, d, c_up), lambda a, b, k: (b, 0, 0)),
                   jax.ShapeDtypeStruct((N_DEV, d, c_up), BF16), _TN, (d, c_up))
        dxi, g_norm, dsh, dsc, dy_branch, dgate = _norm_mod_bwd(
            f"ffn{i}_norm_bwd", dh, xin, row(norm_ffn[i]), None, sc, dxo, None, nb, branch=branch)
        return dxi, (gwup, gwd.reshape(N_DEV, r_dn, d), dcw, dcb, g_norm, dsh, dsc), dy_branch, dgate

    dx3, (gwup1, gwd1, dcw1, dcb1, g_nffn1, dsh2b, dsc2b), dys2, dg1b = ffn_bwd(
        1, dyf1, dx4, x3, sc2b, ffn1_saved, (ys2, g1b))
    dgg = _mm_nt("ssm_out_dx", dys2, wos, F32)
    gwos = _mm_tn("ssm_out_dw", gg, dys2, BF16)
    dgl, dz1, g_bglu = _glu_bwd(dgg, z_ssm, gl, bglu_f)
    dz2 = _mm_nt("ssm_glu_dx", dgl, wglu, F32)
    gwglu = _mm_tn("ssm_glu_dw", z_ssm, dgl, BF16)
    dy_ssm, du_skip, g_dskip = _ssm_post_bwd(dz1, dz2, y_ssm, u, dskip_f)
    du_core, d_ops = _ssm_core_bwd(dy_ssm, u, xp_re, xp_im, ops, nb)
    du = _add_cast(du_core, du_skip)
    dh3 = _mm_nt("ssm_in_dx", du, win, F32)
    gwin = _mm_tn("ssm_in_dw", h3, du, BF16)
    dx2, g_nmix1, dsh1b, dsc1b, dyf0, dg2 = _norm_mod_bwd(
        "ssm_norm_bwd", dh3, x2, row(norm_mix[1]), None, sc1b, dx3, None, nb, branch=(yf0, g2))
    g_ssm_params = ops_vjp(d_ops)

    dx1, (gwup0, gwd0, dcw0, dcb0, g_nffn0, dsh2, dsc2), dya, dg1 = ffn_bwd(
        0, dyf0, dx2, x1, sc2, ffn0_saved, (ya, g1))
    do_att = _mm_nt("attn_out_dx", dya, wo, BF16)
    gwo = _mm_tn("attn_out_dw", o_att, dya, BF16)
    rows8 = lambda a: a.reshape(N_DEV, d // N_DEV, d)
    def two_d(w):
        shp = w.shape
        if len(shp) == 1:
            return (1, shp[0])
        if len(shp) == 2:
            return shp
        return (shp[0] * shp[1], math.prod(shp[2:]))

    ssm_w = [a_re, a_im, log_dt, b_re, b_im, c_re, c_im]
    ssm_partial = [g.reshape(two_d(w)) for g, w in zip(g_ssm_params, ssm_w)]
    early = _Exchange([rows8(gwo), rows8(gwin), rows8(gwglu), rows8(gwos), gwup0, gwup1, gwd0, gwd1] + ssm_partial,
                      [False] * 8 + [True] * 7)
    dq, dk, dv, early_res = _attn_bwd(qkv, car_att, do_att, nb, s, d, early)
    ro, rin, rglu, ros, rup0, rup1, rd0, rd1 = early_res[:8]
    ssm8 = early_res[8:]
    dqkv = jnp.concatenate([dq, dk, dv], axis=1)
    tk = _tile(t, 1024)
    gwq8 = _mm("attn_qkv_dw", h1, dqkv, (1, N_DEV, t // tk),
               pl.BlockSpec((tk, d), lambda a, b, k: (k, 0)),
               pl.BlockSpec((tk, cq), lambda a, b, k: (k, b)),
               pl.BlockSpec((None, d, cq), lambda a, b, k: (b, 0, 0)),
               jax.ShapeDtypeStruct((N_DEV, d, cq), BF16), _TN, (d, cq))
    dh1, (rq,) = _mm("attn_qkv_dx", dqkv, wq8, (t // tm_, 1, N_DEV),
                     pl.BlockSpec((tm_, cq), lambda a, b, k: (a, k)),
                     pl.BlockSpec((None, d, cq), lambda a, b, k: (k, 0, 0)),
                     pl.BlockSpec((tm_, d), lambda a, b, k: (a, 0)), jax.ShapeDtypeStruct((t, d), F32), _NT, (tm_, d),
                     ex=_Exchange([gwq8], [False]))
    dx0, g_nmix0, dsh1, dsc1 = _norm_mod_bwd("attn_norm_bwd", dh1, x0, row(norm_mix[0]), None, sc1, dx1, None, nb)
    grad_x = dx0.reshape(nb, s, d)

    dmod = [jnp.concatenate([a.reshape(nb, d) for a in grp], axis=1) for grp in
            ([dsh1, dsc1, dg1, dsh2, dsc2, dg2], [dsh1b, dsc1b, dg1b, dsh2b, dsc2b, dg2b])]
    dfin = jnp.concatenate([dsh_f.reshape(nb, d), dsc_f.reshape(nb, d)], axis=1)
    dmodfin = jnp.concatenate(dmod + [dfin], axis=1)
    flat = lambda a: a.reshape(1, -1)
    last_ex = _Exchange(
        [dmodfin, jnp.concatenate([g_nmix0, g_nmix1]), jnp.concatenate([g_nffn0, g_nffn1]), g_norm_out,
         jnp.concatenate([flat(dcb0), flat(dcb1)]), g_dskip, g_bglu, jnp.stack([dcw0, dcw1])], [True] * 8)
    upd = {}
    upd["w_up"], (dmf8, nmix8, nffn8, nout8, cb8, dskip_g8, bglu_g8, cw_g8) = _adamw_layers(
        "adamw_w_up", [rup0, rup1], w_up, m_w_up, v_w_up, ex=last_ex)
    shard = d // N_DEV
    dskip_g8 = lax.dynamic_slice_in_dim(dskip_g8, me * shard, shard, axis=2)
    bglu_g8 = lax.dynamic_slice_in_dim(bglu_g8, me * shard, shard, axis=2)
    cw_g8 = lax.dynamic_slice_in_dim(cw_g8, me, 1, axis=2).reshape(N_DEV, 2 * 3, c_up)

    dall = dmf8.reshape(n_seq, 14 * d)
    dmod_loc = jnp.stack([lax.dynamic_slice_in_dim(dall[:, i * 6 * d:(i + 1) * 6 * d], me * cm, cm, axis=1)
                          for i in range(2)])
    dfin_loc = lax.dynamic_slice_in_dim(dall[:, 12 * d:], me * cf, cf, axis=1)
    g_w_mod, g_w_fin, g_bias = _modfin_bwd(c_act.T, dmod_loc, dfin_loc, dall)
    g_b_mod = g_bias[0, :12 * d].reshape(2, 6 * d)
    g_b_fin = g_bias[0, 12 * d:]

    def big(name, parts, w, m, v):
        shp = w.shape
        r2 = lambda a: a.reshape(-1, shp[-1])
        res = _adamw(name, parts.reshape(parts.shape[0], -1, shp[-1]), r2(w), r2(m), r2(v))
        return [a.reshape(shp) for a in res]

    upd["w_mod"] = big("adamw_w_mod", g_w_mod[None], w_mod, m_w_mod, v_w_mod)
    upd["w_fin"] = big("adamw_w_fin", g_w_fin[None], w_fin, m_w_fin, v_w_fin)
    upd["w_qkv"] = big("adamw_w_qkv", rq, w_qkv, m_w_qkv, v_w_qkv)
    upd["w_o_attn"] = big("adamw_w_o_attn", ro, w_o_attn, m_w_o_attn, v_w_o_attn)
    upd["w_in_ssm"] = big("adamw_w_in_ssm", rin, w_in_ssm, m_w_in_ssm, v_w_in_ssm)
    upd["w_glu"] = big("adamw_w_glu", rglu, w_glu, m_w_glu, v_w_glu)
    upd["w_o_ssm"] = big("adamw_w_o_ssm", ros, w_o_ssm, m_w_o_ssm, v_w_o_ssm)
    upd["w_down"] = _adamw_layers("adamw_w_down", [rd0, rd1], w_down, m_w_down, v_w_down)

    small_names = ["norm_mix", "norm_ffn", "b_mod", "a_re", "a_im", "log_dt", "b_re", "b_im", "c_re", "c_im",
                   "d_skip", "b_glu", "conv_w", "conv_b", "norm_out", "b_fin"]
    small_g = [nmix8, nffn8, g_b_mod[None], *ssm8, dskip_g8, bglu_g8, cw_g8, cb8, nout8, g_b_fin[None]]
    small_w = [norm_mix, norm_ffn, b_mod, a_re, a_im, log_dt, b_re, b_im, c_re, c_im, d_skip, b_glu, conv_w, conv_b,
               norm_out, b_fin]
    small_m = [m_norm_mix, m_norm_ffn, m_b_mod, m_a_re, m_a_im, m_log_dt, m_b_re, m_b_im, m_c_re, m_c_im, m_d_skip,
               m_b_glu, m_conv_w, m_conv_b, m_norm_out, m_b_fin]
    small_v = [v_norm_mix, v_norm_ffn, v_b_mod, v_a_re, v_a_im, v_log_dt, v_b_re, v_b_im, v_c_re, v_c_im, v_d_skip,
               v_b_glu, v_conv_w, v_conv_b, v_norm_out, v_b_fin]
    entries = [(gp.reshape((gp.shape[0],) + two_d(w)), w.reshape(two_d(w)), m.reshape(two_d(w)), v.reshape(two_d(w)))
               for gp, w, m, v in zip(small_g, small_w, small_m, small_v)]
    res = _adamw_many("adamw_small", entries)
    for j, (nm, w) in enumerate(zip(small_names, small_w)):
        upd[nm] = [res[4 * j + k].reshape(w.shape) for k in range(4)]

    order = ["norm_mix", "norm_ffn", "w_mod", "b_mod", "w_qkv", "w_o_attn", "w_in_ssm", "a_re", "a_im", "log_dt",
             "b_re", "b_im", "c_re", "c_im", "d_skip", "w_glu", "b_glu", "w_o_ssm", "w_up", "conv_w", "conv_b",
             "w_down", "norm_out", "w_fin", "b_fin"]
    outs = [loss, grad_x]
    for k in range(4):
        outs += [upd[nm][k] for nm in order]
    return tuple(outs)
```
